```python
import math
import jax, jax.numpy as jnp
from jax import lax
import numpy as np

D_MODEL = 1024
BATCH = 8
SEQ = 4096
DEPTH = 2

N_EVEN = (DEPTH + 1) // 2
N_ODD = DEPTH // 2

HGRN_DIM = D_MODEL // 2
HGRN_HEAD_DIM = 128
HGRN_HEADS = HGRN_DIM // HGRN_HEAD_DIM
HGRN_CHUNK = 64

S5_DIM = D_MODEL - HGRN_DIM
S5_GROUP = 16
S5_GROUPS = S5_DIM // S5_GROUP
S5_STATE = 64
S5_DT_MIN = 1e-3
S5_DT_MAX = 1e-1

EVEN_IN = 4 * HGRN_DIM + S5_DIM

MLA_HEADS = 8
MLA_Q_RANK = 384
MLA_KV_RANK = 256
MLA_NOPE = 128
MLA_ROPE = 64
MLA_V = 128
MLA_QK = MLA_NOPE + MLA_ROPE
ODD_IN = MLA_Q_RANK + MLA_KV_RANK + MLA_ROPE
ROPE_THETA = 10000.0
Q_BLOCK = 128

D_FF = 2816
CONV_W = 3
EPS = 1e-6

kernel_name = "hybrid_hgrn2_s5_mla_convffn"


def rmsnorm(x, g):
    xf = x.astype(jnp.float32)
    y = xf * lax.rsqrt(jnp.mean(xf * xf, axis=-1, keepdims=True) + EPS)
    return (y * g.astype(jnp.float32)).astype(x.dtype)


def hgrn2(q, f, i, g, lb, norm_g):
    f32 = jnp.float32
    B_, S_, _ = q.shape
    H, Dh, C = HGRN_HEADS, HGRN_HEAD_DIM, HGRN_CHUNK
    N = S_ // C
    lb = lb.astype(f32)
    forget = lb + (1.0 - lb) * jax.nn.sigmoid(f.astype(f32))
    key_in = 1.0 - forget

    def heads(t):
        return t.astype(f32).reshape(B_, N, C, H, Dh).transpose(0, 3, 1, 2, 4)

    qh, kh, vh = heads(q), heads(key_in), heads(i)
    b = jnp.cumsum(heads(jnp.log(forget)), axis=3)
    b_last = b[:, :, :, -1:, :]
    qd = qh * jnp.exp(b)
    kd = kh * jnp.exp(-b)
    causal = jnp.tril(jnp.ones((C, C), dtype=bool))
    att = jnp.where(causal, jnp.einsum('bhntd,bhnsd->bhnts', qd, kd), 0.0)
    o_intra = jnp.einsum('bhnts,bhnsv->bhntv', att, vh)
    d_state = jnp.einsum('bhnsd,bhnsv->bhndv', kh * jnp.exp(b_last - b), vh)
    decay = jnp.exp(b_last[:, :, :, 0, :])

    def step(state, inp):
        ds_n, dec_n = inp
        return dec_n[..., None] * state + ds_n, state

    s0 = jnp.zeros((B_, H, Dh, Dh), f32)
    _, s_start = lax.scan(step, s0, (d_state.transpose(2, 0, 1, 3, 4), decay.transpose(2, 0, 1, 3)))
    s_start = s_start.transpose(1, 2, 0, 3, 4)
    o = o_intra + jnp.einsum('bhntd,bhndv->bhntv', qd, s_start)
    o = o.transpose(0, 2, 3, 1, 4).reshape(B_, S_, H, Dh)
    o = o * lax.rsqrt(jnp.mean(o * o, axis=-1, keepdims=True) + EPS) * norm_g.astype(f32).reshape(H, Dh)
    return o.reshape(B_, S_, HGRN_DIM) * jax.nn.silu(g.astype(f32))


def s5(u, a_re, a_im, log_dt, b_re, b_im, c_re, c_im, d_skip, w_glu, b_glu):
    f32 = jnp.float32
    B_, S_, _ = u.shape
    G, P, Hc = S5_GROUPS, S5_STATE, S5_GROUP
    uf = u.astype(f32).reshape(B_, S_, G, Hc)
    ar, ai = a_re.astype(f32), a_im.astype(f32)
    dt = jnp.exp(log_dt.astype(f32))[:, None]
    mag = jnp.exp(ar * dt)
    abar_re, abar_im = mag * jnp.cos(ai * dt), mag * jnp.sin(ai * dt)
    den = ar * ar + ai * ai
    xr, xi = abar_re - 1.0, abar_im
    coef_re = ((xr * ar + xi * ai) / den)[..., None]
    coef_im = ((xi * ar - xr * ai) / den)[..., None]
    br, bi = b_re.astype(f32), b_im.astype(f32)
    bb_re = coef_re * br - coef_im * bi
    bb_im = coef_re * bi + coef_im * br
    bu_re = jnp.einsum('bsgh,gph->bsgp', uf, bb_re)
    bu_im = jnp.einsum('bsgh,gph->bsgp', uf, bb_im)
    a_seq_re = jnp.broadcast_to(abar_re, (1, S_, G, P))
    a_seq_im = jnp.broadcast_to(abar_im, (1, S_, G, P))

    def combine(left, right):
        a1r, a1i, b1r, b1i = left
        a2r, a2i, b2r, b2i = right
        return (a2r * a1r - a2i * a1i, a2r * a1i + a2i * a1r,
                a2r * b1r - a2i * b1i + b2r, a2r * b1i + a2i * b1r + b2i)

    _, _, s_re, s_im = lax.associative_scan(combine, (a_seq_re, a_seq_im, bu_re, bu_im), axis=1)
    y = (jnp.einsum('bsgp,ghp->bsgh', s_re, c_re.astype(f32))
         - jnp.einsum('bsgp,ghp->bsgh', s_im, c_im.astype(f32))
         + d_skip.astype(f32).reshape(G, Hc) * uf)
    z = jax.nn.gelu(y.reshape(B_, S_, S5_DIM))
    return z * jax.nn.sigmoid(z @ w_glu.astype(f32) + b_glu.astype(f32))


def even_mixer(hn, w_in, lb, hgrn_norm_g, a_re, a_im, log_dt, b_re, b_im, c_re, c_im,
               d_skip, w_glu, b_glu, w_out):
    proj = hn @ w_in
    q, f, i, g, u = jnp.split(proj, [HGRN_DIM, 2 * HGRN_DIM, 3 * HGRN_DIM, 4 * HGRN_DIM], axis=-1)
    y_a = hgrn2(q, f, i, g, lb, hgrn_norm_g).astype(hn.dtype)
    y_b = s5(u, a_re, a_im, log_dt, b_re, b_im, c_re, c_im, d_skip, w_glu, b_glu).astype(hn.dtype)
    return jnp.concatenate([y_a, y_b], axis=-1) @ w_out


def rotate(x, cos, sin):
    x1, x2 = jnp.split(x, 2, axis=-1)
    return jnp.concatenate([x1 * cos - x2 * sin, x1 * sin + x2 * cos], axis=-1)


def mla(hn, positions, w_in, q_norm_g, w_uq, kv_norm_g, w_ukv, w_out):
    B_, S_, _ = hn.shape
    H = MLA_HEADS
    proj = hn @ w_in
    cq, ckv, k_rope = jnp.split(proj, [MLA_Q_RANK, MLA_Q_RANK + MLA_KV_RANK], axis=-1)
    q = (rmsnorm(cq, q_norm_g) @ w_uq).reshape(B_, S_, H, MLA_QK)
    kv = (rmsnorm(ckv, kv_norm_g) @ w_ukv).reshape(B_, S_, H, MLA_NOPE + MLA_V)
    q_nope, q_rope = jnp.split(q, [MLA_NOPE], axis=-1)
    k_nope, v = jnp.split(kv, [MLA_NOPE], axis=-1)
    freqs = ROPE_THETA ** (-jnp.arange(0, MLA_ROPE, 2, dtype=jnp.float32) / MLA_ROPE)
    ang = positions.astype(jnp.float32)[..., None] * freqs
    cos = jnp.cos(ang)[:, :, None, :].astype(hn.dtype)
    sin = jnp.sin(ang)[:, :, None, :].astype(hn.dtype)
    q_rope = rotate(q_rope, cos, sin)
    k_rope = rotate(k_rope[:, :, None, :], cos, sin)
    q = jnp.concatenate([q_nope, q_rope], axis=-1)
    k = jnp.concatenate([k_nope, jnp.broadcast_to(k_rope, (B_, S_, H, MLA_ROPE))], axis=-1)
    scale = MLA_QK ** -0.5
    nb = S_ // Q_BLOCK
    qb = q.reshape(B_, nb, Q_BLOCK, H, MLA_QK).transpose(1, 0, 2, 3, 4)
    kpos = jnp.arange(S_)

    def attend(args):
        q_blk, blk = args
        s = jnp.einsum('bqhd,bkhd->bhqk', q_blk, k).astype(jnp.float32) * scale
        qpos = blk * Q_BLOCK + jnp.arange(Q_BLOCK)
        s = jnp.where(kpos[None, :] <= qpos[:, None], s, -jnp.inf)
        p = jax.nn.softmax(s, axis=-1).astype(v.dtype)
        return jnp.einsum('bhqk,bkhd->bqhd', p, v)

    o = lax.map(attend, (qb, jnp.arange(nb)))
    o = o.transpose(1, 0, 2, 3, 4).reshape(B_, S_, H * MLA_V)
    return o @ w_out


def conv_ffn(hn, w_in, conv_w, conv_b, w_out):
    a, u = jnp.split(hn @ w_in, 2, axis=-1)
    a = lax.conv_general_dilated(a, conv_w[:, None, :], window_strides=(1,),
                                 padding=[(CONV_W - 1, 0)],
                                 dimension_numbers=('NWC', 'WIO', 'NWC'),
                                 feature_group_count=D_FF) + conv_b
    return (jax.nn.silu(a) * u) @ w_out


def _fwd_setup_inputs(seed: int = 0) -> dict:
    key = jax.random.key(seed)
    ks = iter(jax.random.split(key, 40))
    nrm = lambda shape, s: jax.random.normal(next(ks), shape, jnp.float32) * s
    D = D_MODEL
    G, P, Hc = S5_GROUPS, S5_STATE, S5_GROUP
    x = nrm((BATCH, SEQ, D), 1.0)
    offset = jax.random.randint(next(ks), (BATCH, 1), 0, SEQ)
    positions = (offset + jnp.arange(SEQ)[None, :]).astype(jnp.int32)
    return {
        "x": x,
        "positions": positions,
        "norm_mix_g": 1.0 + nrm((DEPTH, D), 0.02),
        "norm_ffn_g": 1.0 + nrm((DEPTH, D), 0.02),
        "final_norm_g": 1.0 + nrm((D,), 0.02),
        "even_w_in": nrm((N_EVEN, D, EVEN_IN), D ** -0.5),
        "hgrn_lb_logits": nrm((N_EVEN + 1, HGRN_DIM), 0.1),
        "hgrn_norm_g": 1.0 + nrm((N_EVEN, HGRN_DIM), 0.02),
        "s5_a_re": -0.5 + nrm((N_EVEN, G, P), 0.01),
        "s5_a_im": jnp.pi * jnp.arange(P, dtype=jnp.float32) + nrm((N_EVEN, G, P), 0.01),
        "s5_log_dt": jax.random.uniform(next(ks), (N_EVEN, G), jnp.float32,
                                        math.log(S5_DT_MIN), math.log(S5_DT_MAX)),
        "s5_b_re": nrm((N_EVEN, G, P, Hc), (2 * Hc) ** -0.5),
        "s5_b_im": nrm((N_EVEN, G, P, Hc), (2 * Hc) ** -0.5),
        "s5_c_re": nrm((N_EVEN, G, Hc, P), (2 * P) ** -0.5),
        "s5_c_im": nrm((N_EVEN, G, Hc, P), (2 * P) ** -0.5),
        "s5_d": nrm((N_EVEN, S5_DIM), 1.0),
        "s5_w_glu": nrm((N_EVEN, S5_DIM, S5_DIM), S5_DIM ** -0.5),
        "s5_b_glu": nrm((N_EVEN, S5_DIM), 0.02),
        "even_w_out": nrm((N_EVEN, HGRN_DIM + S5_DIM, D), (HGRN_DIM + S5_DIM) ** -0.5),
        "odd_w_in": nrm((N_ODD, D, ODD_IN), D ** -0.5),
        "mla_q_norm_g": 1.0 + nrm((N_ODD, MLA_Q_RANK), 0.02),
        "mla_w_uq": nrm((N_ODD, MLA_Q_RANK, MLA_HEADS * MLA_QK), MLA_Q_RANK ** -0.5),
        "mla_kv_norm_g": 1.0 + nrm((N_ODD, MLA_KV_RANK), 0.02),
        "mla_w_ukv": nrm((N_ODD, MLA_KV_RANK, MLA_HEADS * (MLA_NOPE + MLA_V)), MLA_KV_RANK ** -0.5),
        "odd_w_out": nrm((N_ODD, MLA_HEADS * MLA_V, D), (MLA_HEADS * MLA_V) ** -0.5),
        "ffn_w_in": nrm((DEPTH, D, 2 * D_FF), D ** -0.5),
        "ffn_conv_w": nrm((DEPTH, CONV_W, D_FF), CONV_W ** -0.5),
        "ffn_conv_b": nrm((DEPTH, D_FF), 0.02),
        "ffn_w_out": nrm((DEPTH, D_FF, D), D_FF ** -0.5),
    }


def _fwd_reference(x, positions, norm_mix_g, norm_ffn_g, final_norm_g,
              even_w_in, hgrn_lb_logits, hgrn_norm_g,
              s5_a_re, s5_a_im, s5_log_dt, s5_b_re, s5_b_im, s5_c_re, s5_c_im,
              s5_d, s5_w_glu, s5_b_glu, even_w_out,
              odd_w_in, mla_q_norm_g, mla_w_uq, mla_kv_norm_g, mla_w_ukv, odd_w_out,
              ffn_w_in, ffn_conv_w, ffn_conv_b, ffn_w_out):
    lower_bounds = jnp.cumsum(jax.nn.softmax(hgrn_lb_logits.astype(jnp.float32), axis=0), axis=0)
    h = x
    for layer in range(DEPTH):
        j = layer // 2
        hn = rmsnorm(h, norm_mix_g[layer])
        if layer % 2 == 0:
            mix = even_mixer(hn, even_w_in[j], lower_bounds[j], hgrn_norm_g[j],
                             s5_a_re[j], s5_a_im[j], s5_log_dt[j], s5_b_re[j], s5_b_im[j],
                             s5_c_re[j], s5_c_im[j], s5_d[j], s5_w_glu[j], s5_b_glu[j],
                             even_w_out[j])
        else:
            mix = mla(hn, positions, odd_w_in[j], mla_q_norm_g[j], mla_w_uq[j],
                      mla_kv_norm_g[j], mla_w_ukv[j], odd_w_out[j])
        h = h + mix
        h = h + conv_ffn(rmsnorm(h, norm_ffn_g[layer]), ffn_w_in[layer], ffn_conv_w[layer],
                         ffn_conv_b[layer], ffn_w_out[layer])
    return rmsnorm(h, final_norm_g)


import jax as _jax
import jax.numpy as _jnp

TWIN_FORMAT = 'train_step'
FWD_PARAMS = ['x', 'positions', 'norm_mix_g', 'norm_ffn_g', 'final_norm_g', 'even_w_in', 'hgrn_lb_logits', 'hgrn_norm_g', 's5_a_re', 's5_a_im', 's5_log_dt', 's5_b_re', 's5_b_im', 's5_c_re', 's5_c_im', 's5_d', 's5_w_glu', 's5_b_glu', 'even_w_out', 'odd_w_in', 'mla_q_norm_g', 'mla_w_uq', 'mla_kv_norm_g', 'mla_w_ukv', 'odd_w_out', 'ffn_w_in', 'ffn_conv_w', 'ffn_conv_b', 'ffn_w_out']
TWIN_WEIGHTS = ['norm_mix_g', 'norm_ffn_g', 'final_norm_g', 'even_w_in', 'hgrn_lb_logits', 'hgrn_norm_g', 's5_a_re', 's5_a_im', 's5_log_dt', 's5_b_re', 's5_b_im', 's5_c_re', 's5_c_im', 's5_d', 's5_w_glu', 's5_b_glu', 'even_w_out', 'odd_w_in', 'mla_q_norm_g', 'mla_w_uq', 'mla_kv_norm_g', 'mla_w_ukv', 'odd_w_out', 'ffn_w_in', 'ffn_conv_w', 'ffn_conv_b', 'ffn_w_out']
TWIN_DIFF_INPUT = 'x'
TWIN_INPUTS = ['x', 'positions', 'norm_mix_g', 'norm_ffn_g', 'final_norm_g', 'even_w_in', 'hgrn_lb_logits', 'hgrn_norm_g', 's5_a_re', 's5_a_im', 's5_log_dt', 's5_b_re', 's5_b_im', 's5_c_re', 's5_c_im', 's5_d', 's5_w_glu', 's5_b_glu', 'even_w_out', 'odd_w_in', 'mla_q_norm_g', 'mla_w_uq', 'mla_kv_norm_g', 'mla_w_ukv', 'odd_w_out', 'ffn_w_in', 'ffn_conv_w', 'ffn_conv_b', 'ffn_w_out', 'loss_target', 'm_norm_mix_g', 'm_norm_ffn_g', 'm_final_norm_g', 'm_even_w_in', 'm_hgrn_lb_logits', 'm_hgrn_norm_g', 'm_s5_a_re', 'm_s5_a_im', 'm_s5_log_dt', 'm_s5_b_re', 'm_s5_b_im', 'm_s5_c_re', 'm_s5_c_im', 'm_s5_d', 'm_s5_w_glu', 'm_s5_b_glu', 'm_even_w_out', 'm_odd_w_in', 'm_mla_q_norm_g', 'm_mla_w_uq', 'm_mla_kv_norm_g', 'm_mla_w_ukv', 'm_odd_w_out', 'm_ffn_w_in', 'm_ffn_conv_w', 'm_ffn_conv_b', 'm_ffn_w_out', 'v_norm_mix_g', 'v_norm_ffn_g', 'v_final_norm_g', 'v_even_w_in', 'v_hgrn_lb_logits', 'v_hgrn_norm_g', 'v_s5_a_re', 'v_s5_a_im', 'v_s5_log_dt', 'v_s5_b_re', 'v_s5_b_im', 'v_s5_c_re', 'v_s5_c_im', 'v_s5_d', 'v_s5_w_glu', 'v_s5_b_glu', 'v_even_w_out', 'v_odd_w_in', 'v_mla_q_norm_g', 'v_mla_w_uq', 'v_mla_kv_norm_g', 'v_mla_w_ukv', 'v_odd_w_out', 'v_ffn_w_in', 'v_ffn_conv_w', 'v_ffn_conv_b', 'v_ffn_w_out']
TWIN_OUTPUTS = ['loss', 'grad_x', 'grad_norm_mix_g', 'grad_norm_ffn_g', 'grad_final_norm_g', 'grad_even_w_in', 'grad_hgrn_lb_logits', 'grad_hgrn_norm_g', 'grad_s5_a_re', 'grad_s5_a_im', 'grad_s5_log_dt', 'grad_s5_b_re', 'grad_s5_b_im', 'grad_s5_c_re', 'grad_s5_c_im', 'grad_s5_d', 'grad_s5_w_glu', 'grad_s5_b_glu', 'grad_even_w_out', 'grad_odd_w_in', 'grad_mla_q_norm_g', 'grad_mla_w_uq', 'grad_mla_kv_norm_g', 'grad_mla_w_ukv', 'grad_odd_w_out', 'grad_ffn_w_in', 'grad_ffn_conv_w', 'grad_ffn_conv_b', 'grad_ffn_w_out', 'delta_norm_mix_g', 'delta_norm_ffn_g', 'delta_final_norm_g', 'delta_even_w_in', 'delta_hgrn_lb_logits', 'delta_hgrn_norm_g', 'delta_s5_a_re', 'delta_s5_a_im', 'delta_s5_log_dt', 'delta_s5_b_re', 'delta_s5_b_im', 'delta_s5_c_re', 'delta_s5_c_im', 'delta_s5_d', 'delta_s5_w_glu', 'delta_s5_b_glu', 'delta_even_w_out', 'delta_odd_w_in', 'delta_mla_q_norm_g', 'delta_mla_w_uq', 'delta_mla_kv_norm_g', 'delta_mla_w_ukv', 'delta_odd_w_out', 'delta_ffn_w_in', 'delta_ffn_conv_w', 'delta_ffn_conv_b', 'delta_ffn_w_out', 'new_m_norm_mix_g', 'new_m_norm_ffn_g', 'new_m_final_norm_g', 'new_m_even_w_in', 'new_m_hgrn_lb_logits', 'new_m_hgrn_norm_g', 'new_m_s5_a_re', 'new_m_s5_a_im', 'new_m_s5_log_dt', 'new_m_s5_b_re', 'new_m_s5_b_im', 'new_m_s5_c_re', 'new_m_s5_c_im', 'new_m_s5_d', 'new_m_s5_w_glu', 'new_m_s5_b_glu', 'new_m_even_w_out', 'new_m_odd_w_in', 'new_m_mla_q_norm_g', 'new_m_mla_w_uq', 'new_m_mla_kv_norm_g', 'new_m_mla_w_ukv', 'new_m_odd_w_out', 'new_m_ffn_w_in', 'new_m_ffn_conv_w', 'new_m_ffn_conv_b', 'new_m_ffn_w_out', 'new_v_norm_mix_g', 'new_v_norm_ffn_g', 'new_v_final_norm_g', 'new_v_even_w_in', 'new_v_hgrn_lb_logits', 'new_v_hgrn_norm_g', 'new_v_s5_a_re', 'new_v_s5_a_im', 'new_v_s5_log_dt', 'new_v_s5_b_re', 'new_v_s5_b_im', 'new_v_s5_c_re', 'new_v_s5_c_im', 'new_v_s5_d', 'new_v_s5_w_glu', 'new_v_s5_b_glu', 'new_v_even_w_out', 'new_v_odd_w_in', 'new_v_mla_q_norm_g', 'new_v_mla_w_uq', 'new_v_mla_kv_norm_g', 'new_v_mla_w_ukv', 'new_v_odd_w_out', 'new_v_ffn_w_in', 'new_v_ffn_conv_w', 'new_v_ffn_conv_b', 'new_v_ffn_w_out']
TWIN_LEAF_KINDS = {'loss': 'loss', 'grad_x': 'grad_x', 'grad_norm_mix_g': 'grad_w', 'grad_norm_ffn_g': 'grad_w', 'grad_final_norm_g': 'grad_w', 'grad_even_w_in': 'grad_w', 'grad_hgrn_lb_logits': 'grad_w', 'grad_hgrn_norm_g': 'grad_w', 'grad_s5_a_re': 'grad_w', 'grad_s5_a_im': 'grad_w', 'grad_s5_log_dt': 'grad_w', 'grad_s5_b_re': 'grad_w', 'grad_s5_b_im': 'grad_w', 'grad_s5_c_re': 'grad_w', 'grad_s5_c_im': 'grad_w', 'grad_s5_d': 'grad_w', 'grad_s5_w_glu': 'grad_w', 'grad_s5_b_glu': 'grad_w', 'grad_even_w_out': 'grad_w', 'grad_odd_w_in': 'grad_w', 'grad_mla_q_norm_g': 'grad_w', 'grad_mla_w_uq': 'grad_w', 'grad_mla_kv_norm_g': 'grad_w', 'grad_mla_w_ukv': 'grad_w', 'grad_odd_w_out': 'grad_w', 'grad_ffn_w_in': 'grad_w', 'grad_ffn_conv_w': 'grad_w', 'grad_ffn_conv_b': 'grad_w', 'grad_ffn_w_out': 'grad_w', 'delta_norm_mix_g': 'delta_w', 'delta_norm_ffn_g': 'delta_w', 'delta_final_norm_g': 'delta_w', 'delta_even_w_in': 'delta_w', 'delta_hgrn_lb_logits': 'delta_w', 'delta_hgrn_norm_g': 'delta_w', 'delta_s5_a_re': 'delta_w', 'delta_s5_a_im': 'delta_w', 'delta_s5_log_dt': 'delta_w', 'delta_s5_b_re': 'delta_w', 'delta_s5_b_im': 'delta_w', 'delta_s5_c_re': 'delta_w', 'delta_s5_c_im': 'delta_w', 'delta_s5_d': 'delta_w', 'delta_s5_w_glu': 'delta_w', 'delta_s5_b_glu': 'delta_w', 'delta_even_w_out': 'delta_w', 'delta_odd_w_in': 'delta_w', 'delta_mla_q_norm_g': 'delta_w', 'delta_mla_w_uq': 'delta_w', 'delta_mla_kv_norm_g': 'delta_w', 'delta_mla_w_ukv': 'delta_w', 'delta_odd_w_out': 'delta_w', 'delta_ffn_w_in': 'delta_w', 'delta_ffn_conv_w': 'delta_w', 'delta_ffn_conv_b': 'delta_w', 'delta_ffn_w_out': 'delta_w', 'new_m_norm_mix_g': 'new_m', 'new_m_norm_ffn_g': 'new_m', 'new_m_final_norm_g': 'new_m', 'new_m_even_w_in': 'new_m', 'new_m_hgrn_lb_logits': 'new_m', 'new_m_hgrn_norm_g': 'new_m', 'new_m_s5_a_re': 'new_m', 'new_m_s5_a_im': 'new_m', 'new_m_s5_log_dt': 'new_m', 'new_m_s5_b_re': 'new_m', 'new_m_s5_b_im': 'new_m', 'new_m_s5_c_re': 'new_m', 'new_m_s5_c_im': 'new_m', 'new_m_s5_d': 'new_m', 'new_m_s5_w_glu': 'new_m', 'new_m_s5_b_glu': 'new_m', 'new_m_even_w_out': 'new_m', 'new_m_odd_w_in': 'new_m', 'new_m_mla_q_norm_g': 'new_m', 'new_m_mla_w_uq': 'new_m', 'new_m_mla_kv_norm_g': 'new_m', 'new_m_mla_w_ukv': 'new_m', 'new_m_odd_w_out': 'new_m', 'new_m_ffn_w_in': 'new_m', 'new_m_ffn_conv_w': 'new_m', 'new_m_ffn_conv_b': 'new_m', 'new_m_ffn_w_out': 'new_m', 'new_v_norm_mix_g': 'new_v', 'new_v_norm_ffn_g': 'new_v', 'new_v_final_norm_g': 'new_v', 'new_v_even_w_in': 'new_v', 'new_v_hgrn_lb_logits': 'new_v', 'new_v_hgrn_norm_g': 'new_v', 'new_v_s5_a_re': 'new_v', 'new_v_s5_a_im': 'new_v', 'new_v_s5_log_dt': 'new_v', 'new_v_s5_b_re': 'new_v', 'new_v_s5_b_im': 'new_v', 'new_v_s5_c_re': 'new_v', 'new_v_s5_c_im': 'new_v', 'new_v_s5_d': 'new_v', 'new_v_s5_w_glu': 'new_v', 'new_v_s5_b_glu': 'new_v', 'new_v_even_w_out': 'new_v', 'new_v_odd_w_in': 'new_v', 'new_v_mla_q_norm_g': 'new_v', 'new_v_mla_w_uq': 'new_v', 'new_v_mla_kv_norm_g': 'new_v', 'new_v_mla_w_ukv': 'new_v', 'new_v_odd_w_out': 'new_v', 'new_v_ffn_w_in': 'new_v', 'new_v_ffn_conv_w': 'new_v', 'new_v_ffn_conv_b': 'new_v', 'new_v_ffn_w_out': 'new_v'}


def _forward(args):
    return _fwd_reference(*[args[k] for k in FWD_PARAMS])


def _output_shape():
    out = _jax.eval_shape(lambda: _forward(_fwd_setup_inputs(0)))
    return out.shape, out.dtype

N_MICROBATCH = 1
ADAM_LR = 0.001
ADAM_B1 = 0.9
ADAM_B2 = 0.999
ADAM_EPS = 1e-08
ADAM_WD = 0.01
ADAM_STEP = 10
PER_EXAMPLE_BATCH_AXIS = {'x': 0, 'positions': 0, 'loss_target': 0}
SHARED_INPUTS = []
_WEIGHT_DTYPES = {'norm_mix_g': _jnp.float32, 'norm_ffn_g': _jnp.float32, 'final_norm_g': _jnp.float32, 'even_w_in': _jnp.float32, 'hgrn_lb_logits': _jnp.float32, 'hgrn_norm_g': _jnp.float32, 's5_a_re': _jnp.float32, 's5_a_im': _jnp.float32, 's5_log_dt': _jnp.float32, 's5_b_re': _jnp.float32, 's5_b_im': _jnp.float32, 's5_c_re': _jnp.float32, 's5_c_im': _jnp.float32, 's5_d': _jnp.float32, 's5_w_glu': _jnp.float32, 's5_b_glu': _jnp.float32, 'even_w_out': _jnp.float32, 'odd_w_in': _jnp.float32, 'mla_q_norm_g': _jnp.float32, 'mla_w_uq': _jnp.float32, 'mla_kv_norm_g': _jnp.float32, 'mla_w_ukv': _jnp.float32, 'odd_w_out': _jnp.float32, 'ffn_w_in': _jnp.float32, 'ffn_conv_w': _jnp.float32, 'ffn_conv_b': _jnp.float32, 'ffn_w_out': _jnp.float32}
MOMENT_SCALE = {'norm_mix_g': 1.439352e-01, 'norm_ffn_g': 1.226131e-01, 'final_norm_g': 3.195798e+01, 'even_w_in': 1.203975e-01, 'hgrn_lb_logits': 8.350754e-02, 'hgrn_norm_g': 1.199344e-01, 's5_a_re': 3.164811e-03, 's5_a_im': 2.839896e-03, 's5_log_dt': 3.819049e+00, 's5_b_re': 2.325411e-03, 's5_b_im': 2.188304e-03, 's5_c_re': 4.735024e-03, 's5_c_im': 4.425777e-03, 's5_d': 7.983253e-02, 's5_w_glu': 2.169792e-02, 's5_b_glu': 3.400543e-02, 'even_w_out': 9.673246e-02, 'odd_w_in': 6.085160e-02, 'mla_q_norm_g': 4.251430e-02, 'mla_w_uq': 2.153074e-02, 'mla_kv_norm_g': 8.236481e-02, 'mla_w_ukv': 2.931440e-02, 'odd_w_out': 3.551875e-02, 'ffn_w_in': 5.239098e-02, 'ffn_conv_w': 5.323765e-02, 'ffn_conv_b': 5.178996e-02, 'ffn_w_out': 8.549421e-02}


def _to_microbatches(a, axis):
    t = _jnp.moveaxis(a, axis, 0)
    t = t.reshape((N_MICROBATCH, t.shape[0] // N_MICROBATCH) + t.shape[1:])
    return _jnp.moveaxis(t, 1, axis + 1)


def setup_inputs(seed: int = 0) -> dict:
    inp = _fwd_setup_inputs(seed)
    key = _jax.random.fold_in(_jax.random.key(seed), 7919)
    shape, _ = _output_shape()
    out = dict(inp)
    out["loss_target"] = _jax.random.normal(_jax.random.fold_in(key, 0), shape, _jnp.float32)
    for i, name in enumerate(TWIN_WEIGHTS):
        w = inp[name].astype(_jnp.float32)
        if MOMENT_SCALE is None:
            s = _jnp.sqrt(_jnp.mean(_jnp.square(w)) + 1e-30)
        else:
            s = MOMENT_SCALE[name]
        km, kv = _jax.random.split(_jax.random.fold_in(key, i + 1))
        out[name] = w
        out["m_" + name] = s * _jax.random.normal(km, w.shape, _jnp.float32)
        out["v_" + name] = (s * s) * _jax.random.uniform(kv, w.shape, _jnp.float32, 0.5, 1.5)
    if N_MICROBATCH > 1:
        for name, axis in PER_EXAMPLE_BATCH_AXIS.items():
            out[name] = _to_microbatches(out[name], axis)
    return {'x': out['x'], 'positions': out['positions'], 'norm_mix_g': out['norm_mix_g'], 'norm_ffn_g': out['norm_ffn_g'], 'final_norm_g': out['final_norm_g'], 'even_w_in': out['even_w_in'], 'hgrn_lb_logits': out['hgrn_lb_logits'], 'hgrn_norm_g': out['hgrn_norm_g'], 's5_a_re': out['s5_a_re'], 's5_a_im': out['s5_a_im'], 's5_log_dt': out['s5_log_dt'], 's5_b_re': out['s5_b_re'], 's5_b_im': out['s5_b_im'], 's5_c_re': out['s5_c_re'], 's5_c_im': out['s5_c_im'], 's5_d': out['s5_d'], 's5_w_glu': out['s5_w_glu'], 's5_b_glu': out['s5_b_glu'], 'even_w_out': out['even_w_out'], 'odd_w_in': out['odd_w_in'], 'mla_q_norm_g': out['mla_q_norm_g'], 'mla_w_uq': out['mla_w_uq'], 'mla_kv_norm_g': out['mla_kv_norm_g'], 'mla_w_ukv': out['mla_w_ukv'], 'odd_w_out': out['odd_w_out'], 'ffn_w_in': out['ffn_w_in'], 'ffn_conv_w': out['ffn_conv_w'], 'ffn_conv_b': out['ffn_conv_b'], 'ffn_w_out': out['ffn_w_out'], 'loss_target': out['loss_target'], 'm_norm_mix_g': out['m_norm_mix_g'], 'm_norm_ffn_g': out['m_norm_ffn_g'], 'm_final_norm_g': out['m_final_norm_g'], 'm_even_w_in': out['m_even_w_in'], 'm_hgrn_lb_logits': out['m_hgrn_lb_logits'], 'm_hgrn_norm_g': out['m_hgrn_norm_g'], 'm_s5_a_re': out['m_s5_a_re'], 'm_s5_a_im': out['m_s5_a_im'], 'm_s5_log_dt': out['m_s5_log_dt'], 'm_s5_b_re': out['m_s5_b_re'], 'm_s5_b_im': out['m_s5_b_im'], 'm_s5_c_re': out['m_s5_c_re'], 'm_s5_c_im': out['m_s5_c_im'], 'm_s5_d': out['m_s5_d'], 'm_s5_w_glu': out['m_s5_w_glu'], 'm_s5_b_glu': out['m_s5_b_glu'], 'm_even_w_out': out['m_even_w_out'], 'm_odd_w_in': out['m_odd_w_in'], 'm_mla_q_norm_g': out['m_mla_q_norm_g'], 'm_mla_w_uq': out['m_mla_w_uq'], 'm_mla_kv_norm_g': out['m_mla_kv_norm_g'], 'm_mla_w_ukv': out['m_mla_w_ukv'], 'm_odd_w_out': out['m_odd_w_out'], 'm_ffn_w_in': out['m_ffn_w_in'], 'm_ffn_conv_w': out['m_ffn_conv_w'], 'm_ffn_conv_b': out['m_ffn_conv_b'], 'm_ffn_w_out': out['m_ffn_w_out'], 'v_norm_mix_g': out['v_norm_mix_g'], 'v_norm_ffn_g': out['v_norm_ffn_g'], 'v_final_norm_g': out['v_final_norm_g'], 'v_even_w_in': out['v_even_w_in'], 'v_hgrn_lb_logits': out['v_hgrn_lb_logits'], 'v_hgrn_norm_g': out['v_hgrn_norm_g'], 'v_s5_a_re': out['v_s5_a_re'], 'v_s5_a_im': out['v_s5_a_im'], 'v_s5_log_dt': out['v_s5_log_dt'], 'v_s5_b_re': out['v_s5_b_re'], 'v_s5_b_im': out['v_s5_b_im'], 'v_s5_c_re': out['v_s5_c_re'], 'v_s5_c_im': out['v_s5_c_im'], 'v_s5_d': out['v_s5_d'], 'v_s5_w_glu': out['v_s5_w_glu'], 'v_s5_b_glu': out['v_s5_b_glu'], 'v_even_w_out': out['v_even_w_out'], 'v_odd_w_in': out['v_odd_w_in'], 'v_mla_q_norm_g': out['v_mla_q_norm_g'], 'v_mla_w_uq': out['v_mla_w_uq'], 'v_mla_kv_norm_g': out['v_mla_kv_norm_g'], 'v_mla_w_ukv': out['v_mla_w_ukv'], 'v_odd_w_out': out['v_odd_w_out'], 'v_ffn_w_in': out['v_ffn_w_in'], 'v_ffn_conv_w': out['v_ffn_conv_w'], 'v_ffn_conv_b': out['v_ffn_conv_b'], 'v_ffn_w_out': out['v_ffn_w_out']}


def _loss(weights, diff, rest, loss_target):
    with _jax.named_scope("forward"):
        args = {**rest, TWIN_DIFF_INPUT: diff, **{k: w.astype(_WEIGHT_DTYPES[k]) for k, w in weights.items()}}
        y = _forward(args)
    with _jax.named_scope("loss_head"):
        err = _jnp.square(y.astype(_jnp.float32) - loss_target)
        return 0.5 * _jnp.sum(_jnp.mean(err, axis=-1)) if err.ndim else 0.5 * err


def _adamw(w, g, m, v):
    m = ADAM_B1 * m + (1.0 - ADAM_B1) * g
    v = ADAM_B2 * v + (1.0 - ADAM_B2) * _jnp.square(g)
    m_hat = m / (1.0 - ADAM_B1 ** ADAM_STEP)
    v_hat = v / (1.0 - ADAM_B2 ** ADAM_STEP)
    delta = -ADAM_LR * (m_hat / (_jnp.sqrt(v_hat) + ADAM_EPS) + ADAM_WD * w)
    return delta, m, v


def reference(x, positions, norm_mix_g, norm_ffn_g, final_norm_g, even_w_in, hgrn_lb_logits, hgrn_norm_g, s5_a_re, s5_a_im, s5_log_dt, s5_b_re, s5_b_im, s5_c_re, s5_c_im, s5_d, s5_w_glu, s5_b_glu, even_w_out, odd_w_in, mla_q_norm_g, mla_w_uq, mla_kv_norm_g, mla_w_ukv, odd_w_out, ffn_w_in, ffn_conv_w, ffn_conv_b, ffn_w_out, loss_target, m_norm_mix_g, m_norm_ffn_g, m_final_norm_g, m_even_w_in, m_hgrn_lb_logits, m_hgrn_norm_g, m_s5_a_re, m_s5_a_im, m_s5_log_dt, m_s5_b_re, m_s5_b_im, m_s5_c_re, m_s5_c_im, m_s5_d, m_s5_w_glu, m_s5_b_glu, m_even_w_out, m_odd_w_in, m_mla_q_norm_g, m_mla_w_uq, m_mla_kv_norm_g, m_mla_w_ukv, m_odd_w_out, m_ffn_w_in, m_ffn_conv_w, m_ffn_conv_b, m_ffn_w_out, v_norm_mix_g, v_norm_ffn_g, v_final_norm_g, v_even_w_in, v_hgrn_lb_logits, v_hgrn_norm_g, v_s5_a_re, v_s5_a_im, v_s5_log_dt, v_s5_b_re, v_s5_b_im, v_s5_c_re, v_s5_c_im, v_s5_d, v_s5_w_glu, v_s5_b_glu, v_even_w_out, v_odd_w_in, v_mla_q_norm_g, v_mla_w_uq, v_mla_kv_norm_g, v_mla_w_ukv, v_odd_w_out, v_ffn_w_in, v_ffn_conv_w, v_ffn_conv_b, v_ffn_w_out):
    given = dict(x=x, positions=positions, norm_mix_g=norm_mix_g, norm_ffn_g=norm_ffn_g, final_norm_g=final_norm_g, even_w_in=even_w_in, hgrn_lb_logits=hgrn_lb_logits, hgrn_norm_g=hgrn_norm_g, s5_a_re=s5_a_re, s5_a_im=s5_a_im, s5_log_dt=s5_log_dt, s5_b_re=s5_b_re, s5_b_im=s5_b_im, s5_c_re=s5_c_re, s5_c_im=s5_c_im, s5_d=s5_d, s5_w_glu=s5_w_glu, s5_b_glu=s5_b_glu, even_w_out=even_w_out, odd_w_in=odd_w_in, mla_q_norm_g=mla_q_norm_g, mla_w_uq=mla_w_uq, mla_kv_norm_g=mla_kv_norm_g, mla_w_ukv=mla_w_ukv, odd_w_out=odd_w_out, ffn_w_in=ffn_w_in, ffn_conv_w=ffn_conv_w, ffn_conv_b=ffn_conv_b, ffn_w_out=ffn_w_out, loss_target=loss_target, m_norm_mix_g=m_norm_mix_g, m_norm_ffn_g=m_norm_ffn_g, m_final_norm_g=m_final_norm_g, m_even_w_in=m_even_w_in, m_hgrn_lb_logits=m_hgrn_lb_logits, m_hgrn_norm_g=m_hgrn_norm_g, m_s5_a_re=m_s5_a_re, m_s5_a_im=m_s5_a_im, m_s5_log_dt=m_s5_log_dt, m_s5_b_re=m_s5_b_re, m_s5_b_im=m_s5_b_im, m_s5_c_re=m_s5_c_re, m_s5_c_im=m_s5_c_im, m_s5_d=m_s5_d, m_s5_w_glu=m_s5_w_glu, m_s5_b_glu=m_s5_b_glu, m_even_w_out=m_even_w_out, m_odd_w_in=m_odd_w_in, m_mla_q_norm_g=m_mla_q_norm_g, m_mla_w_uq=m_mla_w_uq, m_mla_kv_norm_g=m_mla_kv_norm_g, m_mla_w_ukv=m_mla_w_ukv, m_odd_w_out=m_odd_w_out, m_ffn_w_in=m_ffn_w_in, m_ffn_conv_w=m_ffn_conv_w, m_ffn_conv_b=m_ffn_conv_b, m_ffn_w_out=m_ffn_w_out, v_norm_mix_g=v_norm_mix_g, v_norm_ffn_g=v_norm_ffn_g, v_final_norm_g=v_final_norm_g, v_even_w_in=v_even_w_in, v_hgrn_lb_logits=v_hgrn_lb_logits, v_hgrn_norm_g=v_hgrn_norm_g, v_s5_a_re=v_s5_a_re, v_s5_a_im=v_s5_a_im, v_s5_log_dt=v_s5_log_dt, v_s5_b_re=v_s5_b_re, v_s5_b_im=v_s5_b_im, v_s5_c_re=v_s5_c_re, v_s5_c_im=v_s5_c_im, v_s5_d=v_s5_d, v_s5_w_glu=v_s5_w_glu, v_s5_b_glu=v_s5_b_glu, v_even_w_out=v_even_w_out, v_odd_w_in=v_odd_w_in, v_mla_q_norm_g=v_mla_q_norm_g, v_mla_w_uq=v_mla_w_uq, v_mla_kv_norm_g=v_mla_kv_norm_g, v_mla_w_ukv=v_mla_w_ukv, v_odd_w_out=v_odd_w_out, v_ffn_w_in=v_ffn_w_in, v_ffn_conv_w=v_ffn_conv_w, v_ffn_conv_b=v_ffn_conv_b, v_ffn_w_out=v_ffn_w_out)
    weights = {n: given[n] for n in TWIN_WEIGHTS}
    shared = {n: given[n] for n in SHARED_INPUTS}
    per_example = {n: given[n] for n in ['x', 'positions']}
    grad_fn = _jax.value_and_grad(_loss, argnums=(0, 1))

    def one_microbatch(ex, loss_target):
        ex = dict(ex)
        diff = ex.pop(TWIN_DIFF_INPUT)
        return grad_fn(weights, diff, {**shared, **ex}, loss_target)

    if N_MICROBATCH == 1:
        loss, (grad_w, grad_x) = one_microbatch(per_example, given["loss_target"])
    else:
        def body(carry, xs):
            loss_sum, grad_sum = carry
            l_k, (gw_k, gx_k) = one_microbatch(xs[0], xs[1])
            with _jax.named_scope("update"):
                return (loss_sum + l_k, _jax.tree.map(_jnp.add, grad_sum, gw_k)), gx_k

        init = (_jnp.zeros((), _jnp.float32), _jax.tree.map(_jnp.zeros_like, weights))
        (loss, grad_w), grad_x = _jax.lax.scan(body, init, (per_example, given["loss_target"]))
    with _jax.named_scope("update"):
        delta_w, new_m, new_v = {}, {}, {}
        for n in TWIN_WEIGHTS:
            delta_w[n], new_m[n], new_v[n] = _adamw(weights[n], grad_w[n], given["m_" + n], given["v_" + n])
    return (loss, grad_x, *[grad_w[n] for n in TWIN_WEIGHTS], *[delta_w[n] for n in TWIN_WEIGHTS],
            *[new_m[n] for n in TWIN_WEIGHTS], *[new_v[n] for n in TWIN_WEIGHTS])
```

```python
import functools
import math

import numpy as np
import jax
import jax.numpy as jnp
from jax import lax
from jax.experimental import pallas as pl
from jax.experimental.pallas import tpu as pltpu

F32 = jnp.float32
BF16 = jnp.bfloat16
_MXU_DTYPE = jnp.bfloat16
_HI = lax.Precision.HIGHEST

D_MODEL = 1024
HGRN_DIM = 512
HGRN_HEAD_DIM = 128
HGRN_HEADS = 4
HGRN_CHUNK = 64
S5_DIM = 512
S5_GROUPS = 32
S5_GROUP = 16
S5_STATE = 64
S5_WIDTH = S5_GROUPS * S5_STATE
EVEN_IN = 4 * HGRN_DIM + S5_DIM
MLA_HEADS = 8
MLA_Q_RANK = 384
MLA_KV_RANK = 256
MLA_NOPE = 128
MLA_ROPE = 64
MLA_V = 128
MLA_QK = MLA_NOPE + MLA_ROPE
ODD_IN = MLA_Q_RANK + MLA_KV_RANK + MLA_ROPE
ODD_IN_PAD = MLA_Q_RANK + MLA_KV_RANK + 2 * 128
ROPE_THETA = 10000.0
D_FF = 2816
EPS = 1e-6
ADAM_LR = 0.001
ADAM_B1 = 0.9
ADAM_B2 = 0.999
ADAM_EPS = 1e-08
ADAM_WD = 0.01
ADAM_STEP = 10

N_DEV = 8
LANE = 128
SUBLANE = 8
VMEM_LIMIT_BYTES = 56 * 1024 * 1024
PACK_COLS = 512
MESH = pl.DeviceIdType.MESH


def _cparams(sem=None):
    return pltpu.CompilerParams(dimension_semantics=sem, vmem_limit_bytes=VMEM_LIMIT_BYTES)


def _pick(n, cands):
    for c in cands:
        if n % c == 0:
            return c
    raise ValueError(f"no tile for {n} in {cands}")


def _sigmoid(x):
    return 1.0 / (1.0 + jnp.exp(-x))


def _silu(x):
    return x * _sigmoid(x)


def _gelu(x):
    return 0.5 * x * (1.0 + jnp.tanh(math.sqrt(2.0 / math.pi) * (x + 0.044715 * (x * x * x))))


def _rms(x, g):
    return x * lax.rsqrt(jnp.mean(x * x, axis=-1, keepdims=True) + EPS) * g


def _mxu(a, b, ca, cb):
    return lax.dot_general(a.astype(_MXU_DTYPE), b.astype(_MXU_DTYPE), (((ca,), (cb,)), ((), ())),
                           preferred_element_type=F32)


@functools.partial(jax.custom_vjp, nondiff_argnums=(2, 3))
def _mxu_ad(a, b, ca, cb):
    return _mxu(a, b, ca, cb)


def _mxu_ad_fwd(a, b, ca, cb):
    return _mxu(a, b, ca, cb), (a, b)


def _mxu_ad_bwd(ca, cb, saved, g):
    a, b = saved
    fa, fb = 1 - ca, 1 - cb
    da = _mxu(g, b, 1, fb) if ca == 1 else _mxu(b, g, fb, 1)
    db = _mxu(a, g, fa, 0) if cb == 0 else _mxu(g, a, 0, fa)
    return da, db


_mxu_ad.defvjp(_mxu_ad_fwd, _mxu_ad_bwd)


def _tri(n, upper=False):
    row = lax.broadcasted_iota(jnp.int32, (n, n), 0)
    col = lax.broadcasted_iota(jnp.int32, (n, n), 1)
    return (col >= row) if upper else (col <= row)


def _cumsum_rows(x):
    return jnp.dot(_tri(x.shape[0]).astype(F32), x, precision=_HI, preferred_element_type=F32)


@jax.custom_vjp
def _cumsum_rows_ad(x):
    return _cumsum_rows(x)


def _cumsum_rows_ad_fwd(x):
    return _cumsum_rows(x), None


def _cumsum_rows_ad_bwd(_, g):
    return (jnp.dot(_tri(g.shape[0], upper=True).astype(F32), g, precision=_HI, preferred_element_type=F32),)


_cumsum_rows_ad.defvjp(_cumsum_rows_ad_fwd, _cumsum_rows_ad_bwd)


def _mm(a, b, *, ta=False, tb=False, res=None, out_dtype=F32, a_cols=None, name):
    if ta:
        K, M = a.shape if a_cols is None else (a.shape[0], a_cols[1])
    else:
        M, K = a.shape if a_cols is None else (a.shape[0], a_cols[1])
    N = b.shape[0] if tb else b.shape[1]
    assert (b.shape[1] if tb else b.shape[0]) == K, (a.shape, b.shape, ta, tb)
    tm = _pick(M, (512, 256, 128))
    tn = _pick(N, (512, 384, 256, 128))
    tk = _pick(K, (512, 384, 256, 128))
    nk = K // tk
    coff = 0
    if a_cols is not None:
        assert a_cols[0] % (tm if ta else tk) == 0
        coff = a_cols[0] // (tm if ta else tk)
    a_spec = pl.BlockSpec((tk, tm), lambda i, j, k: (k, i + coff)) if ta else pl.BlockSpec((tm, tk), lambda i, j, k: (i, k + coff))
    b_spec = pl.BlockSpec((tn, tk), lambda i, j, k: (j, k)) if tb else pl.BlockSpec((tk, tn), lambda i, j, k: (k, j))
    o_spec = pl.BlockSpec((tm, tn), lambda i, j, k: (i, j))
    ca, cb = (0 if ta else 1), (1 if tb else 0)
    has_res = res is not None

    def body(*refs):
        a_ref, b_ref = refs[0], refs[1]
        res_ref = refs[2] if has_res else None
        o_ref = refs[2 + has_res]
        part = _mxu(a_ref[...], b_ref[...], ca, cb)
        if nk == 1:
            o_ref[...] = (part + res_ref[...] if has_res else part).astype(out_dtype)
            return
        acc_ref = refs[3 + has_res]
        k = pl.program_id(2)

        @pl.when(k == 0)
        def _():
            acc_ref[...] = part

        @pl.when(k > 0)
        def _():
            acc_ref[...] += part

        @pl.when(k == nk - 1)
        def _():
            o_ref[...] = (acc_ref[...] + res_ref[...] if has_res else acc_ref[...]).astype(out_dtype)

    ins = [a, b] + ([res] if has_res else [])
    in_specs = [a_spec, b_spec] + ([o_spec] if has_res else [])
    return pl.pallas_call(
        body, name=name, grid=(M // tm, N // tn, nk),
        in_specs=in_specs, out_specs=o_spec,
        out_shape=jax.ShapeDtypeStruct((M, N), out_dtype),
        scratch_shapes=[pltpu.VMEM((tm, tn), F32)] if nk > 1 else [],
        compiler_params=_cparams(("parallel", "parallel", "arbitrary")),
    )(*ins)


def _rows(fn, row_ins, const_ins, row_outs, acc_outs, *, name, tm=256):
    norm = [(r, 0, r.shape[1]) if not isinstance(r, tuple) else r for r in row_ins]
    T = norm[0][0].shape[0]
    tm = min(tm, T)
    nr, nc, no, na = len(norm), len(const_ins), len(row_outs), len(acc_outs)

    def body(*refs):
        i = pl.program_id(0)
        vals = [r[...] for r in refs[:nr + nc]]
        outs = fn(*vals)
        for o_ref, o in zip(refs[nr + nc:nr + nc + no], outs[:no]):
            o_ref[...] = o.astype(o_ref.dtype)
        for a_ref, o in zip(refs[nr + nc + no:], outs[no:]):
            @pl.when(i == 0)
            def _(a_ref=a_ref, o=o):
                a_ref[...] = o

            @pl.when(i > 0)
            def _(a_ref=a_ref, o=o):
                a_ref[...] += o

    in_specs = []
    for arr, off, w in norm:
        assert off % w == 0, (off, w)
        in_specs.append(pl.BlockSpec((tm, w), lambda i, b=off // w: (i, b)))
    for c in const_ins:
        in_specs.append(pl.BlockSpec(c.shape, lambda i: (0, 0)))
    out_specs = [pl.BlockSpec((tm, w), lambda i: (i, 0)) for w, _ in row_outs]
    out_specs += [pl.BlockSpec(s, lambda i: (0, 0)) for s in acc_outs]
    out_shape = [jax.ShapeDtypeStruct((T, w), dt) for w, dt in row_outs]
    out_shape += [jax.ShapeDtypeStruct(s, F32) for s in acc_outs]
    return pl.pallas_call(
        body, name=name, grid=(T // tm,), in_specs=in_specs, out_specs=out_specs, out_shape=out_shape,
        compiler_params=_cparams(("arbitrary",)),
    )(*[n[0] for n in norm], *const_ins)


FFN_COL_TILE = LANE
FFN_ROW_CHUNK = 512


def _shift_down(ext, s, rows):
    return pltpu.roll(ext, s, 0)[SUBLANE:SUBLANE + rows]


def _shift_up(ext, s, rows):
    return pltpu.roll(ext, rows + SUBLANE - s, 0)[:rows]


def _ffn_chunks(T):
    r = min(FFN_ROW_CHUNK, T)
    return r, T // r


def _ext_before(ref, c, r):
    if c == 0:
        return jnp.concatenate([jnp.zeros((SUBLANE, ref.shape[1]), F32), ref[0:r, :]], axis=0)
    return ref[c * r - SUBLANE:(c + 1) * r, :]


def _ext_after(ref, c, r, nch):
    if c == nch - 1:
        return jnp.concatenate([ref[c * r:(c + 1) * r, :], jnp.zeros((SUBLANE, ref.shape[1]), F32)], axis=0)
    return ref[c * r:(c + 1) * r + SUBLANE, :]


def _ffn_mid_fwd(au, conv_w, conv_b, *, name):
    T = au.shape[0]
    tc = FFN_COL_TILE
    ncol = D_FF // tc
    r, nch = _ffn_chunks(T)

    def body(a_ref, u_ref, w_ref, b_ref, z_ref):
        w0, w1, w2, bias = w_ref[0:1, :], w_ref[1:2, :], w_ref[2:3, :], b_ref[...]
        for c in range(nch):
            ext = _ext_before(a_ref, c, r)
            pre = w0 * _shift_down(ext, 2, r) + w1 * _shift_down(ext, 1, r) + w2 * ext[SUBLANE:] + bias
            z_ref[c * r:(c + 1) * r, :] = (_silu(pre) * u_ref[c * r:(c + 1) * r, :]).astype(z_ref.dtype)

    return pl.pallas_call(
        body, name=name, grid=(ncol,),
        in_specs=[pl.BlockSpec((T, tc), lambda j: (0, j)), pl.BlockSpec((T, tc), lambda j: (0, j + ncol)),
                  pl.BlockSpec((3, tc), lambda j: (0, j)), pl.BlockSpec((1, tc), lambda j: (0, j))],
        out_specs=pl.BlockSpec((T, tc), lambda j: (0, j)),
        out_shape=jax.ShapeDtypeStruct((T, D_FF), BF16),
        compiler_params=_cparams(("parallel",)),
    )(au, au, conv_w, conv_b)


def _ffn_mid_bwd(au, dz, conv_w, conv_b, *, name):
    T = au.shape[0]
    tc = FFN_COL_TILE
    ncol = D_FF // tc
    r, nch = _ffn_chunks(T)

    def body(a_ref, u_ref, dz_ref, w_ref, b_ref, da_ref, du_ref, dw_ref, db_ref, dpre_ref):
        w0, w1, w2, bias = w_ref[0:1, :], w_ref[1:2, :], w_ref[2:3, :], b_ref[...]
        dw0 = jnp.zeros((1, tc), F32)
        dw1 = jnp.zeros((1, tc), F32)
        dw2 = jnp.zeros((1, tc), F32)
        db = jnp.zeros((1, tc), F32)
        for c in range(nch):
            rows = slice(c * r, (c + 1) * r)
            ext = _ext_before(a_ref, c, r)
            a2, a1, a0 = _shift_down(ext, 2, r), _shift_down(ext, 1, r), ext[SUBLANE:]
            pre = w0 * a2 + w1 * a1 + w2 * a0 + bias
            sg = _sigmoid(pre)
            act = pre * sg
            dzc = dz_ref[rows, :]
            du_ref[rows, :] = (dzc * act).astype(du_ref.dtype)
            dpre = dzc * u_ref[rows, :] * (sg * (1.0 + pre * (1.0 - sg)))
            dpre_ref[rows, :] = dpre
            dw0 += jnp.sum(dpre * a2, axis=0, keepdims=True)
            dw1 += jnp.sum(dpre * a1, axis=0, keepdims=True)
            dw2 += jnp.sum(dpre * a0, axis=0, keepdims=True)
            db += jnp.sum(dpre, axis=0, keepdims=True)
        for c in range(nch):
            ext = _ext_after(dpre_ref, c, r, nch)
            da = w0 * _shift_up(ext, 2, r) + w1 * _shift_up(ext, 1, r) + w2 * ext[:r]
            da_ref[c * r:(c + 1) * r, :] = da.astype(da_ref.dtype)
        dw_ref[0:1, :] = dw0
        dw_ref[1:2, :] = dw1
        dw_ref[2:3, :] = dw2
        db_ref[...] = db

    col = lambda j: (0, j)
    return pl.pallas_call(
        body, name=name, grid=(ncol,),
        in_specs=[pl.BlockSpec((T, tc), col), pl.BlockSpec((T, tc), lambda j: (0, j + ncol)), pl.BlockSpec((T, tc), col),
                  pl.BlockSpec((3, tc), col), pl.BlockSpec((1, tc), col)],
        out_specs=[pl.BlockSpec((T, tc), col), pl.BlockSpec((T, tc), col), pl.BlockSpec((3, tc), col), pl.BlockSpec((1, tc), col)],
        out_shape=[jax.ShapeDtypeStruct((T, D_FF), BF16), jax.ShapeDtypeStruct((T, D_FF), BF16),
                   jax.ShapeDtypeStruct((3, D_FF), F32), jax.ShapeDtypeStruct((1, D_FF), F32)],
        scratch_shapes=[pltpu.VMEM((T, tc), F32)],
        compiler_params=_cparams(("parallel",)),
    )(au, au, dz, conv_w, conv_b)


HGRN_BLOCK = 512


def _hgrn_chunk(dot, cumsum, q, f, i, g, lb, ng, st):
    C = q.shape[0]
    forget = lb + (1.0 - lb) * _sigmoid(f)
    k = 1.0 - forget
    b = cumsum(jnp.log(forget))
    b_last = b[C - 1:C, :]
    qd = q * jnp.exp(b)
    kd = k * jnp.exp(-b)
    att = jnp.where(_tri(C), dot(qd, kd, 1, 1), 0.0)
    o = dot(att, i, 1, 0) + dot(qd, st, 1, 1)
    st_new = st * jnp.exp(b_last) + dot(i, k * jnp.exp(b_last - b), 0, 0)
    on = o * lax.rsqrt(jnp.mean(o * o, axis=-1, keepdims=True) + EPS) * ng
    return on * _silu(g), st_new


def _hgrn_specs(T, rev):
    tb = min(HGRN_BLOCK, T)
    nb = T // tb
    blk = (lambda n: nb - 1 - n) if rev else (lambda n: n)
    hd = HGRN_HEAD_DIM
    proj_specs = [pl.BlockSpec((tb, hd), lambda h, n, k=k: (blk(n), h + HGRN_HEADS * k)) for k in range(4)]
    vec_spec = pl.BlockSpec((1, hd), lambda h, n: (0, h))
    tok_spec = pl.BlockSpec((tb, hd), lambda h, n: (blk(n), h))
    st_spec = pl.BlockSpec((1, tb // HGRN_CHUNK, hd, hd), lambda h, n: (h, blk(n), 0, 0))
    return tb, nb, proj_specs, vec_spec, tok_spec, st_spec


def _hgrn_fwd(proj, lb, ng, *, name):
    T = proj.shape[0]
    tb, nb, proj_specs, vec_spec, tok_spec, st_spec = _hgrn_specs(T, False)
    nsub = tb // HGRN_CHUNK
    hd = HGRN_HEAD_DIM

    def body(q_ref, f_ref, i_ref, g_ref, lb_ref, ng_ref, y_ref, sts_ref, st_ref):
        @pl.when(pl.program_id(1) == 0)
        def _():
            st_ref[...] = jnp.zeros_like(st_ref)

        st = st_ref[...]
        for s in range(nsub):
            rows = slice(s * HGRN_CHUNK, (s + 1) * HGRN_CHUNK)
            sts_ref[0, s] = st
            y, st = _hgrn_chunk(_mxu, _cumsum_rows, q_ref[rows, :], f_ref[rows, :], i_ref[rows, :], g_ref[rows, :],
                                lb_ref[...], ng_ref[...], st)
            y_ref[rows, :] = y
        st_ref[...] = st

    return pl.pallas_call(
        body, name=name, grid=(HGRN_HEADS, nb),
        in_specs=proj_specs + [vec_spec, vec_spec], out_specs=[tok_spec, st_spec],
        out_shape=[jax.ShapeDtypeStruct((T, HGRN_DIM), F32),
                   jax.ShapeDtypeStruct((HGRN_HEADS, T // HGRN_CHUNK, hd, hd), F32)],
        scratch_shapes=[pltpu.VMEM((hd, hd), F32)],
        compiler_params=_cparams(("arbitrary", "arbitrary")),
    )(proj, proj, proj, proj, lb, ng)


def _hgrn_bwd(proj, lb, ng, states, dmix, *, name):
    T = proj.shape[0]
    tb, nb, proj_specs, vec_spec, tok_spec, st_spec = _hgrn_specs(T, True)
    nsub = tb // HGRN_CHUNK
    hd = HGRN_HEAD_DIM
    chunk = functools.partial(_hgrn_chunk, _mxu_ad, _cumsum_rows_ad)

    def body(q_ref, f_ref, i_ref, g_ref, lb_ref, ng_ref, sts_ref, dy_ref,
             dq_ref, df_ref, di_ref, dg_ref, dlb_ref, dng_ref, dst_ref):
        @pl.when(pl.program_id(1) == 0)
        def _():
            dst_ref[...] = jnp.zeros_like(dst_ref)
            dlb_ref[...] = jnp.zeros_like(dlb_ref)
            dng_ref[...] = jnp.zeros_like(dng_ref)

        dst = dst_ref[...]
        dlb = jnp.zeros((1, hd), F32)
        dng = jnp.zeros((1, hd), F32)
        for s in reversed(range(nsub)):
            rows = slice(s * HGRN_CHUNK, (s + 1) * HGRN_CHUNK)
            _, vjp = jax.vjp(chunk, q_ref[rows, :], f_ref[rows, :], i_ref[rows, :], g_ref[rows, :],
                             lb_ref[...], ng_ref[...], sts_ref[0, s])
            dq, df, di, dg, dlb_s, dng_s, dst = vjp((dy_ref[rows, :], dst))
            dq_ref[rows, :] = dq
            df_ref[rows, :] = df
            di_ref[rows, :] = di
            dg_ref[rows, :] = dg
            dlb += dlb_s
            dng += dng_s
        dst_ref[...] = dst
        dlb_ref[...] += dlb
        dng_ref[...] += dng

    tok_out = jax.ShapeDtypeStruct((T, HGRN_DIM), F32)
    vec_out = jax.ShapeDtypeStruct((1, HGRN_DIM), F32)
    return pl.pallas_call(
        body, name=name, grid=(HGRN_HEADS, nb),
        in_specs=proj_specs + [vec_spec, vec_spec, st_spec, tok_spec],
        out_specs=[tok_spec] * 4 + [vec_spec, vec_spec],
        out_shape=[tok_out] * 4 + [vec_out, vec_out],
        scratch_shapes=[pltpu.VMEM((hd, hd), F32)],
        compiler_params=_cparams(("arbitrary", "arbitrary")),
    )(proj, proj, proj, proj, lb, ng, states, dmix)


S5_LANES = 512
S5_ROWS = 512


def _cmul(ar, ai, br, bi):
    return ar * br - ai * bi, ar * bi + ai * br


def _power_table(ar, ai, exps):
    a2 = _cmul(ar, ai, ar, ai)
    a4 = _cmul(*a2, *a2)
    e = exps - 1
    pr = jnp.broadcast_to(ar, exps.shape)
    pi = jnp.broadcast_to(ai, exps.shape)
    for bit, (fr, fi) in enumerate(((ar, ai), a2, a4)):
        nr, ni = _cmul(pr, pi, fr, fi)
        on = ((e >> bit) & 1) == 1
        pr, pi = jnp.where(on, nr, pr), jnp.where(on, ni, pi)
    return pr, pi, a2, a4


def _s5_scan_fwd(bu, a_re, a_im, *, name):
    T = bu.shape[0]
    w, tr = S5_LANES, min(S5_ROWS, T)
    ncol, nt = S5_WIDTH // w, T // tr

    def body(br_ref, bi_ref, ar_ref, ai_ref, sr_ref, si_ref, carry_ref):
        @pl.when(pl.program_id(1) == 0)
        def _():
            carry_ref[...] = jnp.zeros_like(carry_ref)

        ar, ai = ar_ref[...], ai_ref[...]
        rowi = lax.broadcasted_iota(jnp.int32, (SUBLANE, w), 0)
        pr, pi, a2, a4 = _power_table(ar, ai, rowi + 1)

        def tile(i, carry):
            cr, ci = carry
            rows = pl.ds(pl.multiple_of(i * SUBLANE, SUBLANE), SUBLANE)
            xr, xi = br_ref[rows, :], bi_ref[rows, :]
            for s, (fr, fi) in ((1, (ar, ai)), (2, a2), (4, a4)):
                keep = rowi >= s
                zr = jnp.where(keep, pltpu.roll(xr, s, 0), 0.0)
                zi = jnp.where(keep, pltpu.roll(xi, s, 0), 0.0)
                xr, xi = xr + fr * zr - fi * zi, xi + fr * zi + fi * zr
            xr, xi = xr + pr * cr - pi * ci, xi + pr * ci + pi * cr
            sr_ref[rows, :] = xr
            si_ref[rows, :] = xi
            return xr[SUBLANE - 1:SUBLANE, :], xi[SUBLANE - 1:SUBLANE, :]

        cr, ci = lax.fori_loop(0, tr // SUBLANE, tile, (carry_ref[0:1, :], carry_ref[1:2, :]))
        carry_ref[0:1, :] = cr
        carry_ref[1:2, :] = ci

    out = jax.ShapeDtypeStruct((T, S5_WIDTH), F32)
    return pl.pallas_call(
        body, name=name, grid=(ncol, nt),
        in_specs=[pl.BlockSpec((tr, w), lambda j, t: (t, j)), pl.BlockSpec((tr, w), lambda j, t: (t, j + ncol)),
                  pl.BlockSpec((1, w), lambda j, t: (0, j)), pl.BlockSpec((1, w), lambda j, t: (0, j))],
        out_specs=[pl.BlockSpec((tr, w), lambda j, t: (t, j))] * 2,
        out_shape=[out, out],
        scratch_shapes=[pltpu.VMEM((2, w), F32)],
        compiler_params=_cparams(("parallel", "arbitrary")),
    )(bu, bu, a_re, a_im)


def _s5_scan_bwd(g_re, g_im, s_re, s_im, a_re, a_im, *, name):
    T = g_re.shape[0]
    w, tr = S5_LANES, min(S5_ROWS, T)
    ncol, nt = S5_WIDTH // w, T // tr
    ntile = tr // SUBLANE

    def body(gr_ref, gi_ref, sr_ref, si_ref, ar_ref, ai_ref, lr_ref, li_ref, dar_ref, dai_ref, carry_ref):
        @pl.when(pl.program_id(1) == 0)
        def _():
            carry_ref[...] = jnp.zeros_like(carry_ref)
            dar_ref[...] = jnp.zeros_like(dar_ref)
            dai_ref[...] = jnp.zeros_like(dai_ref)

        ar, ai = ar_ref[...], -ai_ref[...]
        rowi = lax.broadcasted_iota(jnp.int32, (SUBLANE, w), 0)
        pr, pi, a2, a4 = _power_table(ar, ai, SUBLANE - rowi)
        last = rowi == SUBLANE - 1

        def tile(i, carry):
            cr, ci, dar, dai = carry
            rows = pl.ds(pl.multiple_of((ntile - 1 - i) * SUBLANE, SUBLANE), SUBLANE)
            xr, xi = gr_ref[rows, :], gi_ref[rows, :]
            for s, (fr, fi) in ((1, (ar, ai)), (2, a2), (4, a4)):
                keep = rowi < SUBLANE - s
                zr = jnp.where(keep, pltpu.roll(xr, SUBLANE - s, 0), 0.0)
                zi = jnp.where(keep, pltpu.roll(xi, SUBLANE - s, 0), 0.0)
                xr, xi = xr + fr * zr - fi * zi, xi + fr * zi + fi * zr
            xr, xi = xr + pr * cr - pi * ci, xi + pr * ci + pi * cr
            lr_ref[rows, :] = xr
            li_ref[rows, :] = xi
            nr = jnp.where(last, cr, pltpu.roll(xr, SUBLANE - 1, 0))
            ni = jnp.where(last, ci, pltpu.roll(xi, SUBLANE - 1, 0))
            sr, si = sr_ref[rows, :], si_ref[rows, :]
            return xr[0:1, :], xi[0:1, :], dar + nr * sr + ni * si, dai + ni * sr - nr * si

        cr, ci, dar, dai = lax.fori_loop(
            0, ntile, tile, (carry_ref[0:1, :], carry_ref[1:2, :], jnp.zeros((SUBLANE, w), F32), jnp.zeros((SUBLANE, w), F32)))
        carry_ref[0:1, :] = cr
        carry_ref[1:2, :] = ci
        dar_ref[...] += dar
        dai_ref[...] += dai

    tok = pl.BlockSpec((tr, w), lambda j, t: (nt - 1 - t, j))
    vec = pl.BlockSpec((1, w), lambda j, t: (0, j))
    acc = pl.BlockSpec((SUBLANE, w), lambda j, t: (0, j))
    out = jax.ShapeDtypeStruct((T, S5_WIDTH), F32)
    accs = jax.ShapeDtypeStruct((SUBLANE, S5_WIDTH), F32)
    return pl.pallas_call(
        body, name=name, grid=(ncol, nt),
        in_specs=[tok, tok, tok, tok, vec, vec], out_specs=[tok, tok, acc, acc],
        out_shape=[out, out, accs, accs],
        scratch_shapes=[pltpu.VMEM((2, w), F32)],
        compiler_params=_cparams(("parallel", "arbitrary")),
    )(g_re, g_im, s_re, s_im, a_re, a_im)


ATTN_BLOCK = 512
_NEG = -1e30


def _attn_scores(qn, qr, kn, kr, q0, k0):
    s = (_mxu(qn, kn, 1, 1) + _mxu(qr, kr, 1, 1)) * (MLA_QK ** -0.5)
    qpos = q0 + lax.broadcasted_iota(jnp.int32, s.shape, 0)
    kpos = k0 + lax.broadcasted_iota(jnp.int32, s.shape, 1)
    return s, kpos <= qpos


def _attn_fwd(q_all, q_rope, kv, k_rope, *, name):
    T = q_all.shape[0]
    tq = min(ATTN_BLOCK, T)
    nq = T // tq

    def body(qn_ref, qr_ref, kn_ref, v_ref, kr_ref, o_ref, lse_ref):
        i = pl.program_id(1)
        qn, qr = qn_ref[...], qr_ref[0]

        def step(j, carry):
            m, l, acc = carry
            ks = pl.ds(pl.multiple_of(j * tq, tq), tq)
            s, ok = _attn_scores(qn, qr, kn_ref[ks, :], kr_ref[ks, :], i * tq, j * tq)
            s = jnp.where(ok, s, _NEG)
            m_new = jnp.maximum(m, jnp.max(s, axis=-1, keepdims=True))
            p = jnp.exp(s - m_new)
            alpha = jnp.exp(m - m_new)
            return m_new, alpha * l + jnp.sum(p, axis=-1, keepdims=True), alpha * acc + _mxu(p, v_ref[ks, :], 1, 0)

        m, l, acc = lax.fori_loop(0, i + 1, step, (jnp.full((tq, 1), _NEG, F32), jnp.zeros((tq, 1), F32), jnp.zeros((tq, MLA_V), F32)))
        o_ref[...] = acc / l
        lse_ref[0] = m + jnp.log(l)

    return pl.pallas_call(
        body, name=name, grid=(MLA_HEADS, nq),
        in_specs=[pl.BlockSpec((tq, MLA_NOPE), lambda h, i: (i, h)), pl.BlockSpec((1, tq, MLA_ROPE), lambda h, i: (h, i, 0)),
                  pl.BlockSpec((T, MLA_NOPE), lambda h, i: (0, 2 * h)), pl.BlockSpec((T, MLA_V), lambda h, i: (0, 2 * h + 1)),
                  pl.BlockSpec((T, MLA_ROPE), lambda h, i: (0, 0))],
        out_specs=[pl.BlockSpec((tq, MLA_V), lambda h, i: (i, h)), pl.BlockSpec((1, tq, 1), lambda h, i: (h, i, 0))],
        out_shape=[jax.ShapeDtypeStruct((T, MLA_HEADS * MLA_V), F32), jax.ShapeDtypeStruct((MLA_HEADS, T, 1), F32)],
        compiler_params=_cparams(("arbitrary", "arbitrary")),
    )(q_all, q_rope, kv, kv, k_rope)


def _attn_bwd(q_all, q_rope, kv, k_rope, o, lse, do, *, name):
    T = q_all.shape[0]
    tk = min(ATTN_BLOCK, T)
    nk = T // tk
    scale = MLA_QK ** -0.5

    def body(qn_ref, qr_ref, kv_ref, kr_ref, o_ref, lse_ref, do_ref, dqn_ref, dqr_ref, dkv_ref, dkr_ref):
        h, j = pl.program_id(0), pl.program_id(1)

        @pl.when(j == 0)
        def _():
            dqn_ref[...] = jnp.zeros_like(dqn_ref)
            dqr_ref[...] = jnp.zeros_like(dqr_ref)

        @pl.when((j == 0) & (h == 0))
        def _():
            dkr_ref[...] = jnp.zeros_like(dkr_ref)

        krows = pl.ds(pl.multiple_of(j * tk, tk), tk)
        kn, v, kr = kv_ref[:, :MLA_NOPE], kv_ref[:, MLA_NOPE:], kr_ref[krows, :]

        def step(i, carry):
            dkn, dv, dkr = carry
            qs = pl.ds(pl.multiple_of(i * tk, tk), tk)
            qn, qr, dob = qn_ref[qs, :], qr_ref[0, qs, :], do_ref[qs, :]
            s, ok = _attn_scores(qn, qr, kn, kr, i * tk, j * tk)
            p = jnp.where(ok, jnp.exp(s - lse_ref[0, qs, :]), 0.0)
            dp = _mxu(dob, v, 1, 1)
            delta = jnp.sum(dob * o_ref[qs, :], axis=-1, keepdims=True)
            ds = p * (dp - delta) * scale
            dqn_ref[qs, :] += _mxu(ds, kn, 1, 0)
            dqr_ref[0, qs, :] += _mxu(ds, kr, 1, 0)
            return dkn + _mxu(ds, qn, 0, 0), dv + _mxu(p, dob, 0, 0), dkr + _mxu(ds, qr, 0, 0)

        dkn, dv, dkr = lax.fori_loop(j, nk, step, (jnp.zeros((tk, MLA_NOPE), F32), jnp.zeros((tk, MLA_V), F32), jnp.zeros((tk, MLA_ROPE), F32)))
        dkv_ref[:, :MLA_NOPE] = dkn
        dkv_ref[:, MLA_NOPE:] = dv
        dkr_ref[krows, :] += dkr

    head_cols = pl.BlockSpec((T, MLA_NOPE), lambda h, j: (0, h))
    head_rope = pl.BlockSpec((1, T, MLA_ROPE), lambda h, j: (h, 0, 0))
    kv_spec = pl.BlockSpec((tk, MLA_NOPE + MLA_V), lambda h, j: (j, h))
    kr_spec = pl.BlockSpec((T, MLA_ROPE), lambda h, j: (0, 0))
    return pl.pallas_call(
        body, name=name, grid=(MLA_HEADS, nk),
        in_specs=[head_cols, head_rope, kv_spec, kr_spec, head_cols, pl.BlockSpec((1, T, 1), lambda h, j: (h, 0, 0)), head_cols],
        out_specs=[head_cols, head_rope, kv_spec, kr_spec],
        out_shape=[jax.ShapeDtypeStruct((T, MLA_HEADS * MLA_NOPE), F32), jax.ShapeDtypeStruct((MLA_HEADS, T, MLA_ROPE), F32),
                   jax.ShapeDtypeStruct((T, MLA_HEADS * (MLA_NOPE + MLA_V)), F32), jax.ShapeDtypeStruct((T, MLA_ROPE), F32)],
        compiler_params=_cparams(("arbitrary", "arbitrary")),
    )(q_all, q_rope, kv, k_rope, o, lse, do)


def _s5_discretize(a_re, a_im, log_dt, bt_re, bt_im, lb_logits):
    dt = jnp.exp(log_dt)
    mag = jnp.exp(a_re * dt)
    abr, abi = mag * jnp.cos(a_im * dt), mag * jnp.sin(a_im * dt)
    den = a_re * a_re + a_im * a_im
    xr, xi = abr - 1.0, abi
    cr = ((xr * a_re + xi * a_im) / den)[:, None, :]
    ci = ((xi * a_re - xr * a_im) / den)[:, None, :]
    e = jnp.exp(lb_logits - jnp.max(lb_logits, axis=0, keepdims=True))
    lb = e[0:1, :] / jnp.sum(e, axis=0, keepdims=True)
    return abr, abi, cr * bt_re - ci * bt_im, cr * bt_im + ci * bt_re, lb


def _whole(shape):
    return pl.BlockSpec(shape, lambda: (0,) * len(shape))


def _s5_params_fwd(a_re, a_im, log_dt, bt_re, bt_im, lb_logits):
    ins = (a_re, a_im, log_dt, bt_re, bt_im, lb_logits)
    outs = [jax.ShapeDtypeStruct(s, F32) for s in (a_re.shape, a_re.shape, bt_re.shape, bt_re.shape, (1, lb_logits.shape[1]))]

    def body(*refs):
        res = _s5_discretize(*[r[...] for r in refs[:6]])
        for o_ref, o in zip(refs[6:], res):
            o_ref[...] = o

    return pl.pallas_call(body, name="s5_params_fwd", in_specs=[_whole(a.shape) for a in ins],
                          out_specs=[_whole(o.shape) for o in outs], out_shape=outs, compiler_params=_cparams())(*ins)


def _s5_params_bwd(a_re, a_im, log_dt, bt_re, bt_im, lb_logits, d_abr, d_abi, d_bbr, d_bbi, d_lb):
    ins = (a_re, a_im, log_dt, bt_re, bt_im, lb_logits, d_abr, d_abi, d_bbr, d_bbi, d_lb)
    outs = [jax.ShapeDtypeStruct(a.shape, F32) for a in ins[:6]]

    def body(*refs):
        _, vjp = jax.vjp(_s5_discretize, *[r[...] for r in refs[:6]])
        for o_ref, o in zip(refs[11:], vjp(tuple(r[...] for r in refs[6:11]))):
            o_ref[...] = o

    return pl.pallas_call(body, name="s5_params_bwd", in_specs=[_whole(a.shape) for a in ins],
                          out_specs=[_whole(o.shape) for o in outs], out_shape=outs, compiler_params=_cparams())(*ins)


def _adamw(w, g, m, v, *, name):
    R, C = w.shape
    tr = _pick(R, (256, 128, 64, 32, 16, 8)) if R % SUBLANE == 0 else R

    def body(w_ref, g_ref, m_ref, v_ref, d_ref, mo_ref, vo_ref):
        gv = g_ref[...]
        m2 = ADAM_B1 * m_ref[...] + (1.0 - ADAM_B1) * gv
        v2 = ADAM_B2 * v_ref[...] + (1.0 - ADAM_B2) * (gv * gv)
        m_hat = m2 / (1.0 - ADAM_B1 ** ADAM_STEP)
        v_hat = v2 / (1.0 - ADAM_B2 ** ADAM_STEP)
        d_ref[...] = -ADAM_LR * (m_hat / (jnp.sqrt(v_hat) + ADAM_EPS) + ADAM_WD * w_ref[...])
        mo_ref[...] = m2
        vo_ref[...] = v2

    spec = pl.BlockSpec((tr, C), lambda i: (i, 0))
    out = jax.ShapeDtypeStruct((R, C), F32)
    return pl.pallas_call(body, name=name, grid=(R // tr,), in_specs=[spec] * 4, out_specs=[spec] * 3,
                          out_shape=[out] * 3, compiler_params=_cparams(("parallel",)))(w, g, m, v)


def _exchange(src, *, group, scatter, name):
    G = 4 if group == "chips" else 2
    slab = src.shape[1:] if scatter else src.shape
    assert (not scatter) or src.shape[0] == G

    def body(src_ref, out_ref, send_sems, recv_sems, local_sem):
        x, y, c = lax.axis_index("x"), lax.axis_index("y"), lax.axis_index("c")
        if group == "chips":
            me = 2 * x + y
            peers = [((1 - x, y, c), 2 * (1 - x) + y), ((x, 1 - y, c), 2 * x + 1 - y), ((1 - x, 1 - y, c), 2 * (1 - x) + 1 - y)]
        else:
            me = c
            peers = [((x, y, 1 - c), 1 - c)]

        def outgoing(slot):
            return src_ref.at[slot] if scatter else src_ref

        def copy(k, dev, src_slot, dst_slot):
            return pltpu.make_async_remote_copy(src_ref=outgoing(src_slot), dst_ref=out_ref.at[dst_slot],
                                                send_sem=send_sems.at[k], recv_sem=recv_sems.at[k],
                                                device_id=dev, device_id_type=MESH)

        mine = pltpu.make_async_copy(outgoing(me), out_ref.at[me], local_sem)
        mine.start()
        sends = [copy(k, dev, slot, me) for k, (dev, slot) in enumerate(peers)]
        for cp in sends:
            cp.start()
        for k, (dev, slot) in enumerate(peers):
            copy(k, dev, slot, slot).wait_recv()
        for cp in sends:
            cp.wait_send()
        mine.wait()

    n = len((0, 1, 2) if group == "chips" else (0,))
    return pl.pallas_call(
        body, name=name,
        in_specs=[pl.BlockSpec(memory_space=pl.ANY)], out_specs=pl.BlockSpec(memory_space=pl.ANY),
        out_shape=jax.ShapeDtypeStruct((G,) + tuple(slab), src.dtype),
        scratch_shapes=[pltpu.SemaphoreType.DMA((n,)), pltpu.SemaphoreType.DMA((n,)), pltpu.SemaphoreType.DMA],
        compiler_params=pltpu.CompilerParams(has_side_effects=True),
    )(src)


def _sum_slabs(buf, out_dtype, *, name):
    G, R, C = buf.shape
    tr = _pick(R, (512, 256, 128, 64, 32, 16, 8))

    def body(b_ref, o_ref):
        acc = b_ref[0].astype(F32)
        for s in range(1, G):
            acc = acc + b_ref[s].astype(F32)
        o_ref[...] = acc.astype(out_dtype)

    return pl.pallas_call(body, name=name, grid=(R // tr,), in_specs=[pl.BlockSpec((G, tr, C), lambda i: (0, i, 0))],
                          out_specs=pl.BlockSpec((tr, C), lambda i: (i, 0)), out_shape=jax.ShapeDtypeStruct((R, C), out_dtype),
                          compiler_params=_cparams(("parallel",)))(buf)


def _rms_fwd_fn(h, g):
    return (_rms(h, g),)


def _rms_bwd_fn(h, dhn, dres, g):
    _, vjp = jax.vjp(_rms, h, g)
    dh, dg = vjp(dhn)
    return dh + dres, dg


def _loss_fn(h, tgt, g):
    y, vjp = jax.vjp(_rms, h, g)
    diff = y - tgt
    dh, dg = vjp(diff * (1.0 / D_MODEL))
    return dh, dg, (0.5 / D_MODEL) * jnp.sum(diff * diff, axis=0, keepdims=True)


def _s5_act(ys, u, d):
    return _gelu(ys + d * u)


def _s5_gate(z, gl, b):
    return z * _sigmoid(gl + b)


def _s5_act_fn(ys, u, d):
    return (_s5_act(ys, u, d),)


def _s5_mix_fn(ya, z, gl, b):
    return (jnp.concatenate([ya, _s5_gate(z, gl, b)], axis=1),)


def _s5_gate_bwd_fn(z, gl, dyb, b):
    _, vjp = jax.vjp(_s5_gate, z, gl, b)
    return vjp(dyb)


def _s5_act_bwd_fn(ys, u, dz1, dz2, d):
    _, vjp = jax.vjp(_s5_act, ys, u, d)
    return vjp(dz1 + dz2)


def _dproj_fn(dq, df, di, dg, du1, du2):
    return (jnp.concatenate([dq, df, di, dg, du1 + du2], axis=1),)


def _rope_pair(r1, r2, pos, freqs):
    ang = pos.astype(F32) * freqs
    c, s = jnp.cos(ang), jnp.sin(ang)
    return r1 * c - r2 * s, r1 * s + r2 * c


_ODD_SPLITS = (0, MLA_Q_RANK, MLA_Q_RANK + MLA_KV_RANK, MLA_Q_RANK + MLA_KV_RANK + LANE, ODD_IN_PAD)


def _mla_prep(cq, ckv, k1, k2, qg, kvg, pos, freqs):
    ko1, ko2 = _rope_pair(k1, k2, pos, freqs)
    return _rms(cq, qg), _rms(ckv, kvg), ko1, ko2


def _mla_prep_fn(proj, pos, qg, kvg, freqs):
    parts = [proj[:, a:b] for a, b in zip(_ODD_SPLITS[:-1], _ODD_SPLITS[1:])]
    return _mla_prep(*parts, qg, kvg, pos, freqs)


def _mla_prep_bwd_fn(proj, pos, dqn, dkvn, dko1, dko2, qg, kvg, freqs):
    parts = [proj[:, a:b] for a, b in zip(_ODD_SPLITS[:-1], _ODD_SPLITS[1:])]
    _, vjp = jax.vjp(lambda *a: _mla_prep(*a, pos, freqs), *parts, qg, kvg)
    dcq, dckv, dk1, dk2, dqg, dkvg = vjp((dqn, dkvn, dko1, dko2))
    return jnp.concatenate([dcq, dckv, dk1, dk2], axis=1), dqg, dkvg


def _rope_q_fn(r1, r2, pos, freqs):
    return _rope_pair(r1, r2, pos, freqs)


def _rope_q_bwd_fn(dqn, do1, do2, pos, freqs):
    dr1, dr2 = _rope_pair(do1, do2, pos, -freqs)
    return (jnp.concatenate([dqn, dr1, dr2], axis=1),)


W_NAMES = ("norm_mix_g", "norm_ffn_g", "final_norm_g", "even_w_in", "hgrn_lb_logits", "hgrn_norm_g", "s5_a_re", "s5_a_im",
           "s5_log_dt", "s5_b_re", "s5_b_im", "s5_c_re", "s5_c_im", "s5_d", "s5_w_glu", "s5_b_glu", "even_w_out", "odd_w_in",
           "mla_q_norm_g", "mla_w_uq", "mla_kv_norm_g", "mla_w_ukv", "odd_w_out", "ffn_w_in", "ffn_conv_w", "ffn_conv_b",
           "ffn_w_out")
SHARDED = (("even_w_in", 2, False), ("s5_w_glu", 1, False), ("even_w_out", 1, False), ("odd_w_in", 1, False),
           ("mla_q_norm_g", 1, True), ("mla_w_uq", 2, False), ("mla_kv_norm_g", 1, True), ("mla_w_ukv", 2, False),
           ("odd_w_out", 1, False), ("ffn_w_in", 2, False), ("ffn_conv_w", 2, True), ("ffn_w_out", 1, False))
SHARDED_NAMES = tuple(s[0] for s in SHARDED)
REPLICATED = tuple(n for n in W_NAMES if n not in SHARDED_NAMES)
SMALL_SHARDED = ("mla_q_norm_g", "mla_kv_norm_g", "ffn_conv_w")
BF16_ROWS = 16


def _pack(flats, cols, row_mult):
    flat = jnp.concatenate(flats)
    per = cols * row_mult
    pad = (-flat.shape[0]) % per
    return jnp.pad(flat, (0, pad)).reshape(-1, cols)


def _pack_dev(flats, cols, row_mult):
    flat = jnp.concatenate(flats, axis=1)
    pad = (-flat.shape[1]) % (cols * row_mult)
    return jnp.pad(flat, ((0, 0), (0, pad))).reshape(flat.shape[0], -1, cols)


def _unpack(flat, shapes):
    out, off = [], 0
    for shp in shapes:
        n = int(np.prod(shp))
        out.append(flat[..., off:off + n].reshape(flat.shape[:-1] + tuple(shp)))
        off += n
    return out


def _gather_weights(w):
    flats = []
    for name, _, exact in SHARDED:
        f = w[name].reshape(-1)
        flats.append(lax.bitcast_convert_type(f, BF16).reshape(-1) if exact else f.astype(BF16))
    pack = _pack(flats, PACK_COLS, BF16_ROWS)
    rows = pack.shape[0]
    g1 = _exchange(pack, group="chips", scatter=False, name="gather_w_chips")
    g2 = _exchange(g1.reshape(4 * rows, PACK_COLS), group="core", scatter=False, name="gather_w_core")
    allp = g2.reshape(2, 4, rows * PACK_COLS).transpose(1, 0, 2).reshape(N_DEV, rows * PACK_COLS)
    words = jnp.dtype(F32).itemsize // jnp.dtype(BF16).itemsize
    shapes = [((words if exact else 1) * int(np.prod(w[name].shape)),) for name, _, exact in SHARDED]
    full = {}
    for (name, ax, exact), seg in zip(SHARDED, _unpack(allp, shapes)):
        if exact:
            seg = lax.bitcast_convert_type(seg.reshape(N_DEV, -1, words), F32)
        shards = seg.reshape((N_DEV,) + w[name].shape)
        full[name] = jnp.concatenate([shards[d] for d in range(N_DEV)], axis=ax)
    return full


def _reduce_scatter_grads(gfull, w):
    flats = []
    for name, ax, _ in SHARDED:
        parts = jnp.stack(jnp.split(gfull[name], N_DEV, axis=ax))
        flats.append(parts.reshape(N_DEV, -1).astype(BF16))
    pack = _pack_dev(flats, PACK_COLS, BF16_ROWS)
    rows = pack.shape[1]
    by_core = pack.reshape(4, 2, rows, PACK_COLS).transpose(1, 0, 2, 3).reshape(2, 4 * rows, PACK_COLS)
    r1 = _exchange(by_core, group="core", scatter=True, name="scatter_g_core")
    s1 = _sum_slabs(r1, BF16, name="sum_g_core").reshape(4, rows, PACK_COLS)
    r2 = _exchange(s1, group="chips", scatter=True, name="scatter_g_chips")
    mine = _sum_slabs(r2, F32, name="sum_g_chips").reshape(-1)
    return dict(zip(SHARDED_NAMES, _unpack(mine, [w[n].shape for n in SHARDED_NAMES])))


def _all_reduce_small(grads, w):
    vec = _pack([grads[n].reshape(-1).astype(F32) for n in REPLICATED], LANE, SUBLANE)
    rows = vec.shape[0]
    a1 = _exchange(vec, group="chips", scatter=False, name="gather_small_chips")
    a2 = _exchange(a1.reshape(4 * rows, LANE), group="core", scatter=False, name="gather_small_core")
    tot = _sum_slabs(a2.reshape(N_DEV, rows, LANE), F32, name="sum_small").reshape(-1)
    return dict(zip(REPLICATED, _unpack(tot, [w[n].shape for n in REPLICATED])))


def _block_diag(blocks):
    G, a, b = blocks.shape
    return jnp.einsum('gab,gk->gakb', blocks, jnp.eye(G, dtype=blocks.dtype)).reshape(G * a, G * b)


def _diag_blocks(mat, a, b):
    G = mat.shape[0] // a
    return jnp.einsum('gagb->gab', mat.reshape(G, a, G, b))


def _ffn_fwd(h, g, w_in, conv_w, conv_b, w_out, tag):
    hn, = _rows(_rms_fwd_fn, [h], [g], [(D_MODEL, BF16)], [], name=f"ffn{tag}_norm")
    au = _mm(hn, w_in, name=f"ffn{tag}_in")
    z = _ffn_mid_fwd(au, conv_w, conv_b, name=f"ffn{tag}_mid")
    return _mm(z, w_out, res=h, name=f"ffn{tag}_out"), (hn, au, z)


def _ffn_bwd(h, dh, saved, g, w_in, conv_w, conv_b, w_out, tag):
    hn, au, z = saved
    dz = _mm(dh, w_out, tb=True, name=f"ffn{tag}_dz")
    dw_out = _mm(z, dh, ta=True, out_dtype=BF16, name=f"ffn{tag}_dwout")
    da, du, dcw, dcb = _ffn_mid_bwd(au, dz, conv_w, conv_b, name=f"ffn{tag}_dmid")
    dhn = _mm(da, w_in[:, :D_FF], tb=True, name=f"ffn{tag}_dhn_a")
    dhn = _mm(du, w_in[:, D_FF:], tb=True, res=dhn, name=f"ffn{tag}_dhn_u")
    dw_in = jnp.concatenate([_mm(hn, da, ta=True, out_dtype=BF16, name=f"ffn{tag}_dwin_a"),
                             _mm(hn, du, ta=True, out_dtype=BF16, name=f"ffn{tag}_dwin_u")], axis=1)
    dh_in, dg = _rows(_rms_bwd_fn, [h, dhn, dh], [g], [(D_MODEL, F32)], [(1, D_MODEL)], name=f"ffn{tag}_dnorm")
    return dh_in, dict(g=dg, w_in=dw_in, conv_w=dcw, conv_b=dcb, w_out=dw_out)


def kernel(x, positions, norm_mix_g, norm_ffn_g, final_norm_g, even_w_in, hgrn_lb_logits, hgrn_norm_g, s5_a_re, s5_a_im, s5_log_dt, s5_b_re, s5_b_im, s5_c_re, s5_c_im, s5_d, s5_w_glu, s5_b_glu, even_w_out, odd_w_in, mla_q_norm_g, mla_w_uq, mla_kv_norm_g, mla_w_ukv, odd_w_out, ffn_w_in, ffn_conv_w, ffn_conv_b, ffn_w_out, loss_target, m_norm_mix_g, m_norm_ffn_g, m_final_norm_g, m_even_w_in, m_hgrn_lb_logits, m_hgrn_norm_g, m_s5_a_re, m_s5_a_im, m_s5_log_dt, m_s5_b_re, m_s5_b_im, m_s5_c_re, m_s5_c_im, m_s5_d, m_s5_w_glu, m_s5_b_glu, m_even_w_out, m_odd_w_in, m_mla_q_norm_g, m_mla_w_uq, m_mla_kv_norm_g, m_mla_w_ukv, m_odd_w_out, m_ffn_w_in, m_ffn_conv_w, m_ffn_conv_b, m_ffn_w_out, v_norm_mix_g, v_norm_ffn_g, v_final_norm_g, v_even_w_in, v_hgrn_lb_logits, v_hgrn_norm_g, v_s5_a_re, v_s5_a_im, v_s5_log_dt, v_s5_b_re, v_s5_b_im, v_s5_c_re, v_s5_c_im, v_s5_d, v_s5_w_glu, v_s5_b_glu, v_even_w_out, v_odd_w_in, v_mla_q_norm_g, v_mla_w_uq, v_mla_kv_norm_g, v_mla_w_ukv, v_odd_w_out, v_ffn_w_in, v_ffn_conv_w, v_ffn_conv_b, v_ffn_w_out):
    given = dict(locals())
    w = {n: given[n] for n in W_NAMES}
    mom = {n: given["m_" + n] for n in W_NAMES}
    var = {n: given["v_" + n] for n in W_NAMES}
    T = x.shape[1]
    h0 = x[0]
    tgt = loss_target[0]
    pos = positions.reshape(T, 1)

    full = _gather_weights(w)
    w_ein = full["even_w_in"][0]
    w_glu = full["s5_w_glu"][0]
    w_eout = full["even_w_out"][0]
    w_oin = full["odd_w_in"][0]
    zpad = jnp.zeros((D_MODEL, LANE - MLA_ROPE // 2), BF16)
    kr0 = MLA_Q_RANK + MLA_KV_RANK
    w_oin_pad = jnp.concatenate([w_oin[:, :kr0], w_oin[:, kr0:kr0 + MLA_ROPE // 2], zpad, w_oin[:, kr0 + MLA_ROPE // 2:], zpad], axis=1)
    qg, kvg = full["mla_q_norm_g"], full["mla_kv_norm_g"]
    w_uq3 = full["mla_w_uq"][0].reshape(MLA_Q_RANK, MLA_HEADS, MLA_QK)
    half = MLA_ROPE // 2
    w_uq_perm = jnp.concatenate([w_uq3[:, :, :MLA_NOPE].reshape(MLA_Q_RANK, -1),
                                 w_uq3[:, :, MLA_NOPE:MLA_NOPE + half].reshape(MLA_Q_RANK, -1),
                                 w_uq3[:, :, MLA_NOPE + half:].reshape(MLA_Q_RANK, -1)], axis=1)
    w_ukv = full["mla_w_ukv"][0]
    w_oout = full["odd_w_out"][0]
    w_fin, conv_w, w_fout = full["ffn_w_in"], full["ffn_conv_w"], full["ffn_w_out"]
    freqs = ROPE_THETA ** (-jnp.arange(0, MLA_ROPE, 2, dtype=F32) / MLA_ROPE)
    freqs_q = jnp.tile(freqs, MLA_HEADS)[None, :]
    freqs_k = jnp.concatenate([freqs, jnp.zeros((LANE - half,), F32)])[None, :]

    sp_in = (s5_a_re[0], s5_a_im[0], s5_log_dt[0][:, None], s5_b_re[0].transpose(0, 2, 1), s5_b_im[0].transpose(0, 2, 1),
             hgrn_lb_logits)
    abr, abi, bbt_re, bbt_im, lb0 = _s5_params_fwd(*sp_in)
    a_re, a_im = abr.reshape(1, S5_WIDTH), abi.reshape(1, S5_WIDTH)
    bb_re, bb_im = _block_diag(bbt_re).astype(BF16), _block_diag(bbt_im).astype(BF16)
    bb_cat = jnp.concatenate([bb_re, bb_im], axis=1)
    c_re = _block_diag(s5_c_re[0].transpose(0, 2, 1)).astype(BF16)
    c_im_neg = _block_diag(-s5_c_im[0].transpose(0, 2, 1)).astype(BF16)
    u_cols = (4 * HGRN_DIM, S5_DIM)

    hn0, = _rows(_rms_fwd_fn, [h0], [norm_mix_g[0:1]], [(D_MODEL, BF16)], [], name="mix0_norm")
    proj = _mm(hn0, w_ein, name="even_in")
    y_a, states = _hgrn_fwd(proj, lb0, hgrn_norm_g, name="hgrn_fwd")
    bu = _mm(proj, bb_cat, a_cols=u_cols, name="s5_bu")
    s_re, s_im = _s5_scan_fwd(bu, a_re, a_im, name="s5_scan_fwd")
    ys = _mm(s_im, c_im_neg, res=_mm(s_re, c_re, name="s5_y_re"), name="s5_y_im")
    z5, = _rows(_s5_act_fn, [ys, (proj,) + u_cols], [s5_d], [(S5_DIM, F32)], [], name="s5_act")
    gl = _mm(z5, w_glu, name="s5_glu")
    mixin, = _rows(_s5_mix_fn, [y_a, z5, gl], [s5_b_glu], [(D_MODEL, BF16)], [], name="s5_mix")
    h1 = _mm(mixin, w_eout, res=h0, name="even_out")
    h2, ffn0_saved = _ffn_fwd(h1, norm_ffn_g[0:1], w_fin[0], conv_w[0], ffn_conv_b[0:1], w_fout[0], 0)

    hn1, = _rows(_rms_fwd_fn, [h2], [norm_mix_g[1:2]], [(D_MODEL, BF16)], [], name="mix1_norm")
    proj_o = _mm(hn1, w_oin_pad, name="odd_in")
    qn, kvn, ko1, ko2 = _rows(_mla_prep_fn, [proj_o, pos], [qg, kvg, freqs_k],
                              [(MLA_Q_RANK, BF16), (MLA_KV_RANK, BF16), (LANE, F32), (LANE, F32)], [], name="mla_prep")
    q_all = _mm(qn, w_uq_perm, name="mla_uq")
    kv = _mm(kvn, w_ukv, name="mla_ukv")
    nope_w = MLA_HEADS * MLA_NOPE
    rope_w = MLA_HEADS * half
    o1, o2 = _rows(_rope_q_fn, [(q_all, nope_w, rope_w), (q_all, nope_w + rope_w, rope_w), pos], [freqs_q],
                   [(rope_w, F32), (rope_w, F32)], [], name="mla_rope_q")
    q_rope = jnp.concatenate([o1.reshape(T, MLA_HEADS, half), o2.reshape(T, MLA_HEADS, half)], axis=2).transpose(1, 0, 2)
    k_rope = jnp.concatenate([ko1[:, :half], ko2[:, :half]], axis=1)
    o, lse = _attn_fwd(q_all, q_rope, kv, k_rope, name="attn_fwd")
    h3 = _mm(o, w_oout, res=h2, name="odd_out")
    h4, ffn1_saved = _ffn_fwd(h3, norm_ffn_g[1:2], w_fin[1], conv_w[1], ffn_conv_b[1:2], w_fout[1], 1)

    dh4, d_final_g, loss_cols = _rows(_loss_fn, [h4, tgt], [final_norm_g[None, :]], [(D_MODEL, F32)],
                                      [(1, D_MODEL), (1, D_MODEL)], name="loss_head")
    loss = lax.psum(jnp.sum(loss_cols), ("x", "y", "c"))

    dh3, gf1 = _ffn_bwd(h3, dh4, ffn1_saved, norm_ffn_g[1:2], w_fin[1], conv_w[1], ffn_conv_b[1:2], w_fout[1], 1)
    do = _mm(dh3, w_oout, tb=True, name="odd_out_dx")
    d_w_oout = _mm(o, dh3, ta=True, out_dtype=BF16, name="odd_out_dw")
    dq_nope, dq_rope, dkv, dk_rope = _attn_bwd(q_all, q_rope, kv, k_rope, o, lse, do, name="attn_bwd")
    dq_rope_t = dq_rope.transpose(1, 0, 2)
    do1, do2 = dq_rope_t[:, :, :half].reshape(T, rope_w), dq_rope_t[:, :, half:].reshape(T, rope_w)
    lane_pad = ((0, 0), (0, LANE - half))
    dko1, dko2 = jnp.pad(dk_rope[:, :half], lane_pad), jnp.pad(dk_rope[:, half:], lane_pad)
    dq_all, = _rows(_rope_q_bwd_fn, [dq_nope, do1, do2, pos], [freqs_q], [(MLA_HEADS * MLA_QK, BF16)], [], name="mla_rope_q_bwd")
    d_w_uq_perm = _mm(qn, dq_all, ta=True, out_dtype=BF16, name="mla_uq_dw")
    dqn = _mm(dq_all, w_uq_perm, tb=True, name="mla_uq_dx")
    d_w_ukv = _mm(kvn, dkv, ta=True, out_dtype=BF16, name="mla_ukv_dw")
    dkvn = _mm(dkv, w_ukv, tb=True, name="mla_ukv_dx")
    dproj_o, d_qg, d_kvg = _rows(_mla_prep_bwd_fn, [proj_o, pos, dqn, dkvn, dko1, dko2], [qg, kvg, freqs_k],
                                 [(ODD_IN_PAD, BF16)], [(1, MLA_Q_RANK), (1, MLA_KV_RANK)], name="mla_prep_bwd")
    d_w_oin_pad = _mm(hn1, dproj_o, ta=True, out_dtype=BF16, name="odd_in_dw")
    dhn1 = _mm(dproj_o, w_oin_pad, tb=True, name="odd_in_dx")
    dh2, d_mix_g1 = _rows(_rms_bwd_fn, [h2, dhn1, dh3], [norm_mix_g[1:2]], [(D_MODEL, F32)], [(1, D_MODEL)], name="mix1_dnorm")
    d_w_oin = jnp.concatenate([d_w_oin_pad[:, :kr0 + half], d_w_oin_pad[:, kr0 + LANE:kr0 + LANE + half]], axis=1)
    d3 = d_w_uq_perm
    d_w_uq = jnp.concatenate([d3[:, :nope_w].reshape(MLA_Q_RANK, MLA_HEADS, MLA_NOPE),
                              d3[:, nope_w:nope_w + rope_w].reshape(MLA_Q_RANK, MLA_HEADS, half),
                              d3[:, nope_w + rope_w:].reshape(MLA_Q_RANK, MLA_HEADS, half)], axis=2).reshape(MLA_Q_RANK, -1)

    dh1, gf0 = _ffn_bwd(h1, dh2, ffn0_saved, norm_ffn_g[0:1], w_fin[0], conv_w[0], ffn_conv_b[0:1], w_fout[0], 0)
    dmix = _mm(dh1, w_eout, tb=True, name="even_out_dx")
    d_w_eout = _mm(mixin, dh1, ta=True, out_dtype=BF16, name="even_out_dw")
    dq, df, di, dg, d_lb0, d_hgrn_g = _hgrn_bwd(proj, lb0, hgrn_norm_g, states, dmix, name="hgrn_bwd")
    dz1, dgl, d_b_glu = _rows(_s5_gate_bwd_fn, [z5, gl, (dmix, HGRN_DIM, S5_DIM)], [s5_b_glu],
                              [(S5_DIM, F32), (S5_DIM, BF16)], [(1, S5_DIM)], name="s5_gate_bwd")
    dz2 = _mm(dgl, w_glu, tb=True, name="s5_glu_dx")
    d_w_glu = _mm(z5, dgl, ta=True, out_dtype=BF16, name="s5_glu_dw")
    dys, du1, d_s5_d = _rows(_s5_act_bwd_fn, [ys, (proj,) + u_cols, dz1, dz2], [s5_d],
                             [(S5_DIM, BF16), (S5_DIM, F32)], [(1, S5_DIM)], name="s5_act_bwd")
    ds_re = _mm(dys, c_re, tb=True, name="s5_ds_re")
    ds_im = _mm(dys, c_im_neg, tb=True, name="s5_ds_im")
    d_c_re = _mm(s_re, dys, ta=True, name="s5_dc_re")
    d_c_im_neg = _mm(s_im, dys, ta=True, name="s5_dc_im")
    lam_re, lam_im, d_ar, d_ai = _s5_scan_bwd(ds_re, ds_im, s_re, s_im, a_re, a_im, name="s5_scan_bwd")
    du2 = _mm(lam_im, bb_im, tb=True, res=_mm(lam_re, bb_re, tb=True, name="s5_du_re"), name="s5_du_im")
    d_bb_re = _mm(proj, lam_re, ta=True, a_cols=u_cols, name="s5_dbb_re")
    d_bb_im = _mm(proj, lam_im, ta=True, a_cols=u_cols, name="s5_dbb_im")
    dproj, = _rows(_dproj_fn, [dq, df, di, dg, du1, du2], [], [(EVEN_IN, BF16)], [], name="even_dproj")
    d_w_ein = _mm(hn0, dproj, ta=True, out_dtype=BF16, name="even_in_dw")
    dhn0 = _mm(dproj, w_ein, tb=True, name="even_in_dx")
    grad_x, d_mix_g0 = _rows(_rms_bwd_fn, [h0, dhn0, dh1], [norm_mix_g[0:1]], [(D_MODEL, F32)], [(1, D_MODEL)], name="mix0_dnorm")
    sp_g = _s5_params_bwd(*sp_in, d_ar.sum(0).reshape(S5_GROUPS, S5_STATE), d_ai.sum(0).reshape(S5_GROUPS, S5_STATE),
                          _diag_blocks(d_bb_re, S5_GROUP, S5_STATE), _diag_blocks(d_bb_im, S5_GROUP, S5_STATE), d_lb0)
    d_a_re, d_a_im, d_log_dt, d_bt_re, d_bt_im, d_lb_logits = sp_g

    g_sharded = dict(
        even_w_in=d_w_ein[None], s5_w_glu=d_w_glu[None], even_w_out=d_w_eout[None], odd_w_in=d_w_oin[None],
        mla_q_norm_g=d_qg, mla_w_uq=d_w_uq[None], mla_kv_norm_g=d_kvg, mla_w_ukv=d_w_ukv[None], odd_w_out=d_w_oout[None],
        ffn_w_in=jnp.stack([gf0["w_in"], gf1["w_in"]]), ffn_conv_w=jnp.stack([gf0["conv_w"], gf1["conv_w"]]),
        ffn_w_out=jnp.stack([gf0["w_out"], gf1["w_out"]]))
    g_repl = dict(
        norm_mix_g=jnp.concatenate([d_mix_g0, d_mix_g1]), norm_ffn_g=jnp.concatenate([gf0["g"], gf1["g"]]),
        final_norm_g=d_final_g[0], hgrn_lb_logits=d_lb_logits, hgrn_norm_g=d_hgrn_g,
        s5_a_re=d_a_re[None], s5_a_im=d_a_im[None], s5_log_dt=d_log_dt[:, 0][None],
        s5_b_re=d_bt_re.transpose(0, 2, 1)[None], s5_b_im=d_bt_im.transpose(0, 2, 1)[None],
        s5_c_re=_diag_blocks(d_c_re, S5_STATE, S5_GROUP).transpose(0, 2, 1)[None],
        s5_c_im=-_diag_blocks(d_c_im_neg, S5_STATE, S5_GROUP).transpose(0, 2, 1)[None],
        s5_d=d_s5_d, s5_b_glu=d_b_glu, ffn_conv_b=jnp.concatenate([gf0["conv_b"], gf1["conv_b"]]))
    grads = {**_reduce_scatter_grads(g_sharded, w), **_all_reduce_small(g_repl, w)}

    delta, new_m, new_v = {}, {}, {}
    small = [n for n in W_NAMES if n in REPLICATED or n in SMALL_SHARDED]
    for n in W_NAMES:
        if n in small:
            continue
        shp = w[n].shape
        two_d = (-1, shp[-1])
        d_, m_, v_ = _adamw(w[n].reshape(two_d), grads[n].reshape(two_d), mom[n].reshape(two_d), var[n].reshape(two_d), name=f"adamw_{n}")
        delta[n], new_m[n], new_v[n] = d_.reshape(shp), m_.reshape(shp), v_.reshape(shp)
    packed = [_pack([t[n].reshape(-1) for n in small], LANE, SUBLANE) for t in (w, grads, mom, var)]
    outs = _adamw(*packed, name="adamw_small")
    small_shapes = [w[n].shape for n in small]
    for dst, o_ in zip((delta, new_m, new_v), outs):
        dst.update(zip(small, _unpack(o_.reshape(-1), small_shapes)))

    return (loss, grad_x[None], *[grads[n] for n in W_NAMES], *[delta[n] for n in W_NAMES],
            *[new_m[n] for n in W_NAMES], *[new_v[n] for n in W_NAMES])
```

```python
import functools
import math

import numpy as np
import jax
import jax.numpy as jnp
from jax import lax
from jax.experimental import pallas as pl
from jax.experimental.pallas import tpu as pltpu

F32 = jnp.float32
BF16 = jnp.bfloat16
_MXU_DTYPE = jnp.bfloat16
_HI = lax.Precision.HIGHEST

D_MODEL = 1024
HGRN_DIM = 512
HGRN_HEAD_DIM = 128
HGRN_HEADS = 4
HGRN_CHUNK = 64
S5_DIM = 512
S5_GROUPS = 32
S5_GROUP = 16
S5_STATE = 64
S5_WIDTH = S5_GROUPS * S5_STATE
EVEN_IN = 4 * HGRN_DIM + S5_DIM
MLA_HEADS = 8
MLA_Q_RANK = 384
MLA_KV_RANK = 256
MLA_NOPE = 128
MLA_ROPE = 64
MLA_V = 128
MLA_QK = MLA_NOPE + MLA_ROPE
ODD_IN = MLA_Q_RANK + MLA_KV_RANK + MLA_ROPE
ODD_IN_PAD = MLA_Q_RANK + MLA_KV_RANK + 2 * 128
ROPE_THETA = 10000.0
D_FF = 2816
EPS = 1e-6
ADAM_LR = 0.001
ADAM_B1 = 0.9
ADAM_B2 = 0.999
ADAM_EPS = 1e-08
ADAM_WD = 0.01
ADAM_STEP = 10

N_DEV = 8
LANE = 128
SUBLANE = 8
VMEM_LIMIT_BYTES = 56 * 1024 * 1024
MESH = pl.DeviceIdType.MESH


def _cparams(sem=None):
    return pltpu.CompilerParams(dimension_semantics=sem, vmem_limit_bytes=VMEM_LIMIT_BYTES)


def _pick(n, cands):
    for c in cands:
        if n % c == 0:
            return c
    raise ValueError(f"no tile for {n} in {cands}")


def _sigmoid(x):
    return 1.0 / (1.0 + jnp.exp(-x))


def _silu(x):
    return x * _sigmoid(x)


def _gelu(x):
    return 0.5 * x * (1.0 + jnp.tanh(math.sqrt(2.0 / math.pi) * (x + 0.044715 * (x * x * x))))


def _rms(x, g):
    return x * lax.rsqrt(jnp.mean(x * x, axis=-1, keepdims=True) + EPS) * g


def _mxu(a, b, ca, cb):
    return lax.dot_general(a.astype(_MXU_DTYPE), b.astype(_MXU_DTYPE), (((ca,), (cb,)), ((), ())),
                           preferred_element_type=F32)


@functools.partial(jax.custom_vjp, nondiff_argnums=(2, 3))
def _mxu_ad(a, b, ca, cb):
    return _mxu(a, b, ca, cb)


def _mxu_ad_fwd(a, b, ca, cb):
    return _mxu(a, b, ca, cb), (a, b)


def _mxu_ad_bwd(ca, cb, saved, g):
    a, b = saved
    fa, fb = 1 - ca, 1 - cb
    da = _mxu(g, b, 1, fb) if ca == 1 else _mxu(b, g, fb, 1)
    db = _mxu(a, g, fa, 0) if cb == 0 else _mxu(g, a, 0, fa)
    return da, db


_mxu_ad.defvjp(_mxu_ad_fwd, _mxu_ad_bwd)


def _tri(n, upper=False):
    row = lax.broadcasted_iota(jnp.int32, (n, n), 0)
    col = lax.broadcasted_iota(jnp.int32, (n, n), 1)
    return (col >= row) if upper else (col <= row)


def _cumsum_rows(x):
    return jnp.dot(_tri(x.shape[0]).astype(F32), x, precision=_HI, preferred_element_type=F32)


@jax.custom_vjp
def _cumsum_rows_ad(x):
    return _cumsum_rows(x)


def _cumsum_rows_ad_fwd(x):
    return _cumsum_rows(x), None


def _cumsum_rows_ad_bwd(_, g):
    return (jnp.dot(_tri(g.shape[0], upper=True).astype(F32), g, precision=_HI, preferred_element_type=F32),)


_cumsum_rows_ad.defvjp(_cumsum_rows_ad_fwd, _cumsum_rows_ad_bwd)


MM_VMEM_BUDGET = 36 * 1024 * 1024
MM_MAX_TILE = 1408


def _lane_divisors(n, cap, offs=()):
    return [d for d in range(min(n, cap) // LANE * LANE, 0, -LANE) if n % d == 0 and all(o % d == 0 for o in offs)]


def _mm_tiles(M, N, K, sa, sb, so, has_res, m_offs, n_offs, k_offs):
    best = None
    for tm in _lane_divisors(M, MM_MAX_TILE, m_offs):
        for tn in _lane_divisors(N, MM_MAX_TILE, n_offs):
            for tk in _lane_divisors(K, K, k_offs):
                nk = K // tk
                vmem = 2 * (tm * tk * sa + tk * tn * sb + tm * tn * so + tm * tn * 4 * has_res) + (tm * tn * 4 if nk > 1 else 0)
                if vmem <= MM_VMEM_BUDGET:
                    key = (-nk, tm * tn, tn)
                    if best is None or key > best[0]:
                        best = (key, tm, tn, tk)
                    break
    return best[1:]


def _mm(a, b, *, ta=False, tb=False, res=None, out_dtype=F32, a_cols=None, b_cols=None, name):
    a_minor = a.shape[1] if a_cols is None else a_cols[1]
    b_minor = b.shape[1] if b_cols is None else b_cols[1]
    K, M = (a.shape[0], a_minor) if ta else (a_minor, a.shape[0])
    N = b.shape[0] if tb else b_minor
    assert (b_minor if tb else b.shape[0]) == K, (a.shape, b.shape, ta, tb)
    a_off = 0 if a_cols is None else a_cols[0]
    b_off = 0 if b_cols is None else b_cols[0]
    has_res = res is not None
    tm, tn, tk = _mm_tiles(M, N, K, a.dtype.itemsize, b.dtype.itemsize, jnp.dtype(out_dtype).itemsize, has_res,
                           (a_off,) if ta else (), () if tb else (b_off,), ((a_off,) if not ta else ()) + ((b_off,) if tb else ()))
    nk = K // tk
    am, ak = (a_off // tm, 0) if ta else (0, a_off // tk)
    bn, bk = (0, b_off // tk) if tb else (b_off // tn, 0)
    a_spec = pl.BlockSpec((tk, tm), lambda i, j, k: (k, i + am)) if ta else pl.BlockSpec((tm, tk), lambda i, j, k: (i, k + ak))
    b_spec = pl.BlockSpec((tn, tk), lambda i, j, k: (j, k + bk)) if tb else pl.BlockSpec((tk, tn), lambda i, j, k: (k, j + bn))
    o_spec = pl.BlockSpec((tm, tn), lambda i, j, k: (i, j))
    ca, cb = (0 if ta else 1), (1 if tb else 0)

    def body(*refs):
        a_ref, b_ref = refs[0], refs[1]
        res_ref = refs[2] if has_res else None
        o_ref = refs[2 + has_res]
        part = _mxu(a_ref[...], b_ref[...], ca, cb)
        if nk == 1:
            o_ref[...] = (part + res_ref[...] if has_res else part).astype(out_dtype)
            return
        acc_ref = refs[3 + has_res]
        k = pl.program_id(2)

        @pl.when(k == 0)
        def _():
            acc_ref[...] = part

        @pl.when(k > 0)
        def _():
            acc_ref[...] += part

        @pl.when(k == nk - 1)
        def _():
            o_ref[...] = (acc_ref[...] + res_ref[...] if has_res else acc_ref[...]).astype(out_dtype)

    ins = [a, b] + ([res] if has_res else [])
    in_specs = [a_spec, b_spec] + ([o_spec] if has_res else [])
    return pl.pallas_call(
        body, name=name, grid=(M // tm, N // tn, nk),
        in_specs=in_specs, out_specs=o_spec,
        out_shape=jax.ShapeDtypeStruct((M, N), out_dtype),
        scratch_shapes=[pltpu.VMEM((tm, tn), F32)] if nk > 1 else [],
        compiler_params=_cparams(("parallel", "parallel", "arbitrary")),
    )(*ins)


def _rows(fn, row_ins, const_ins, row_outs, acc_outs, *, name, tm=256):
    norm = [(r, 0, r.shape[1]) if not isinstance(r, tuple) else r for r in row_ins]
    T = norm[0][0].shape[0]
    tm = min(tm, T)
    nr, nc, no, na = len(norm), len(const_ins), len(row_outs), len(acc_outs)

    def body(*refs):
        i = pl.program_id(0)
        vals = [r[...] for r in refs[:nr + nc]]
        outs = fn(*vals)
        for o_ref, o in zip(refs[nr + nc:nr + nc + no], outs[:no]):
            o_ref[...] = o.astype(o_ref.dtype)
        for a_ref, o in zip(refs[nr + nc + no:], outs[no:]):
            @pl.when(i == 0)
            def _(a_ref=a_ref, o=o):
                a_ref[...] = o

            @pl.when(i > 0)
            def _(a_ref=a_ref, o=o):
                a_ref[...] += o

    in_specs = []
    for arr, off, w in norm:
        assert off % w == 0, (off, w)
        in_specs.append(pl.BlockSpec((tm, w), lambda i, b=off // w: (i, b)))
    for c in const_ins:
        in_specs.append(pl.BlockSpec(c.shape, lambda i: (0, 0)))
    out_specs = [pl.BlockSpec((tm, w), lambda i: (i, 0)) for w, _ in row_outs]
    out_specs += [pl.BlockSpec(s, lambda i: (0, 0)) for s in acc_outs]
    out_shape = [jax.ShapeDtypeStruct((T, w), dt) for w, dt in row_outs]
    out_shape += [jax.ShapeDtypeStruct(s, F32) for s in acc_outs]
    return pl.pallas_call(
        body, name=name, grid=(T // tm,), in_specs=in_specs, out_specs=out_specs, out_shape=out_shape,
        compiler_params=_cparams(("arbitrary",)),
    )(*[n[0] for n in norm], *const_ins)


FFN_COL_TILE = LANE
FFN_ROW_CHUNK = 512


def _shift_down(ext, s, rows):
    return pltpu.roll(ext, s, 0)[SUBLANE:SUBLANE + rows]


def _shift_up(ext, s, rows):
    return pltpu.roll(ext, rows + SUBLANE - s, 0)[:rows]


def _ffn_chunks(T):
    r = min(FFN_ROW_CHUNK, T)
    return r, T // r


def _ext_before(ref, c, r):
    if c == 0:
        return jnp.concatenate([jnp.zeros((SUBLANE, ref.shape[1]), F32), ref[0:r, :]], axis=0)
    return ref[c * r - SUBLANE:(c + 1) * r, :]


def _ext_after(ref, c, r, nch):
    if c == nch - 1:
        return jnp.concatenate([ref[c * r:(c + 1) * r, :], jnp.zeros((SUBLANE, ref.shape[1]), F32)], axis=0)
    return ref[c * r:(c + 1) * r + SUBLANE, :]


def _ffn_mid_fwd(au, conv_w, conv_b, *, name):
    T = au.shape[0]
    tc = FFN_COL_TILE
    ncol = D_FF // tc
    r, nch = _ffn_chunks(T)

    def body(a_ref, u_ref, w_ref, b_ref, z_ref):
        w0, w1, w2, bias = w_ref[0:1, :], w_ref[1:2, :], w_ref[2:3, :], b_ref[...]
        for c in range(nch):
            ext = _ext_before(a_ref, c, r)
            pre = w0 * _shift_down(ext, 2, r) + w1 * _shift_down(ext, 1, r) + w2 * ext[SUBLANE:] + bias
            z_ref[c * r:(c + 1) * r, :] = (_silu(pre) * u_ref[c * r:(c + 1) * r, :]).astype(z_ref.dtype)

    return pl.pallas_call(
        body, name=name, grid=(ncol,),
        in_specs=[pl.BlockSpec((T, tc), lambda j: (0, j)), pl.BlockSpec((T, tc), lambda j: (0, j + ncol)),
                  pl.BlockSpec((3, tc), lambda j: (0, j)), pl.BlockSpec((1, tc), lambda j: (0, j))],
        out_specs=pl.BlockSpec((T, tc), lambda j: (0, j)),
        out_shape=jax.ShapeDtypeStruct((T, D_FF), BF16),
        compiler_params=_cparams(("parallel",)),
    )(au, au, conv_w, conv_b)


def _ffn_mid_bwd(au, dz, conv_w, conv_b, *, name):
    T = au.shape[0]
    tc = FFN_COL_TILE
    ncol = D_FF // tc
    r, nch = _ffn_chunks(T)

    def body(a_ref, u_ref, dz_ref, w_ref, b_ref, da_ref, du_ref, dw_ref, db_ref, dpre_ref):
        w0, w1, w2, bias = w_ref[0:1, :], w_ref[1:2, :], w_ref[2:3, :], b_ref[...]
        dw0 = jnp.zeros((1, tc), F32)
        dw1 = jnp.zeros((1, tc), F32)
        dw2 = jnp.zeros((1, tc), F32)
        db = jnp.zeros((1, tc), F32)
        for c in range(nch):
            rows = slice(c * r, (c + 1) * r)
            ext = _ext_before(a_ref, c, r)
            a2, a1, a0 = _shift_down(ext, 2, r), _shift_down(ext, 1, r), ext[SUBLANE:]
            pre = w0 * a2 + w1 * a1 + w2 * a0 + bias
            sg = _sigmoid(pre)
            act = pre * sg
            dzc = dz_ref[rows, :]
            du_ref[rows, :] = (dzc * act).astype(du_ref.dtype)
            dpre = dzc * u_ref[rows, :] * (sg * (1.0 + pre * (1.0 - sg)))
            dpre_ref[rows, :] = dpre
            dw0 += jnp.sum(dpre * a2, axis=0, keepdims=True)
            dw1 += jnp.sum(dpre * a1, axis=0, keepdims=True)
            dw2 += jnp.sum(dpre * a0, axis=0, keepdims=True)
            db += jnp.sum(dpre, axis=0, keepdims=True)
        for c in range(nch):
            ext = _ext_after(dpre_ref, c, r, nch)
            da = w0 * _shift_up(ext, 2, r) + w1 * _shift_up(ext, 1, r) + w2 * ext[:r]
            da_ref[c * r:(c + 1) * r, :] = da.astype(da_ref.dtype)
        dw_ref[0:1, :] = dw0
        dw_ref[1:2, :] = dw1
        dw_ref[2:3, :] = dw2
        db_ref[...] = db

    col = lambda j: (0, j)
    return pl.pallas_call(
        body, name=name, grid=(ncol,),
        in_specs=[pl.BlockSpec((T, tc), col), pl.BlockSpec((T, tc), lambda j: (0, j + ncol)), pl.BlockSpec((T, tc), col),
                  pl.BlockSpec((3, tc), col), pl.BlockSpec((1, tc), col)],
        out_specs=[pl.BlockSpec((T, tc), col), pl.BlockSpec((T, tc), col), pl.BlockSpec((3, tc), col), pl.BlockSpec((1, tc), col)],
        out_shape=[jax.ShapeDtypeStruct((T, D_FF), BF16), jax.ShapeDtypeStruct((T, D_FF), BF16),
                   jax.ShapeDtypeStruct((3, D_FF), F32), jax.ShapeDtypeStruct((1, D_FF), F32)],
        scratch_shapes=[pltpu.VMEM((T, tc), F32)],
        compiler_params=_cparams(("parallel",)),
    )(au, au, dz, conv_w, conv_b)


HGRN_BLOCK = 512


def _hgrn_chunk(dot, cumsum, q, f, i, g, lb, ng, st):
    C = q.shape[0]
    forget = lb + (1.0 - lb) * _sigmoid(f)
    k = 1.0 - forget
    b = cumsum(jnp.log(forget))
    b_last = b[C - 1:C, :]
    qd = q * jnp.exp(b)
    kd = k * jnp.exp(-b)
    att = jnp.where(_tri(C), dot(qd, kd, 1, 1), 0.0)
    o = dot(att, i, 1, 0) + dot(qd, st, 1, 1)
    st_new = st * jnp.exp(b_last) + dot(i, k * jnp.exp(b_last - b), 0, 0)
    on = o * lax.rsqrt(jnp.mean(o * o, axis=-1, keepdims=True) + EPS) * ng
    return on * _silu(g), st_new


def _hgrn_specs(T, rev):
    tb = min(HGRN_BLOCK, T)
    nb = T // tb
    blk = (lambda n: nb - 1 - n) if rev else (lambda n: n)
    hd = HGRN_HEAD_DIM
    proj_specs = [pl.BlockSpec((tb, hd), lambda h, n, k=k: (blk(n), h + HGRN_HEADS * k)) for k in range(4)]
    vec_spec = pl.BlockSpec((1, hd), lambda h, n: (0, h))
    tok_spec = pl.BlockSpec((tb, hd), lambda h, n: (blk(n), h))
    st_spec = pl.BlockSpec((1, tb // HGRN_CHUNK, hd, hd), lambda h, n: (h, blk(n), 0, 0))
    return tb, nb, proj_specs, vec_spec, tok_spec, st_spec


def _hgrn_fwd(proj, lb, ng, *, name):
    T = proj.shape[0]
    tb, nb, proj_specs, vec_spec, tok_spec, st_spec = _hgrn_specs(T, False)
    nsub = tb // HGRN_CHUNK
    hd = HGRN_HEAD_DIM

    def body(q_ref, f_ref, i_ref, g_ref, lb_ref, ng_ref, y_ref, sts_ref, st_ref):
        @pl.when(pl.program_id(1) == 0)
        def _():
            st_ref[...] = jnp.zeros_like(st_ref)

        st = st_ref[...]
        for s in range(nsub):
            rows = slice(s * HGRN_CHUNK, (s + 1) * HGRN_CHUNK)
            sts_ref[0, s] = st
            y, st = _hgrn_chunk(_mxu, _cumsum_rows, q_ref[rows, :], f_ref[rows, :], i_ref[rows, :], g_ref[rows, :],
                                lb_ref[...], ng_ref[...], st)
            y_ref[rows, :] = y
        st_ref[...] = st

    return pl.pallas_call(
        body, name=name, grid=(HGRN_HEADS, nb),
        in_specs=proj_specs + [vec_spec, vec_spec], out_specs=[tok_spec, st_spec],
        out_shape=[jax.ShapeDtypeStruct((T, HGRN_DIM), F32),
                   jax.ShapeDtypeStruct((HGRN_HEADS, T // HGRN_CHUNK, hd, hd), F32)],
        scratch_shapes=[pltpu.VMEM((hd, hd), F32)],
        compiler_params=_cparams(("arbitrary", "arbitrary")),
    )(proj, proj, proj, proj, lb, ng)


def _hgrn_bwd(proj, lb, ng, states, dmix, *, name):
    T = proj.shape[0]
    tb, nb, proj_specs, vec_spec, tok_spec, st_spec = _hgrn_specs(T, True)
    nsub = tb // HGRN_CHUNK
    hd = HGRN_HEAD_DIM
    chunk = functools.partial(_hgrn_chunk, _mxu_ad, _cumsum_rows_ad)

    def body(q_ref, f_ref, i_ref, g_ref, lb_ref, ng_ref, sts_ref, dy_ref,
             dq_ref, df_ref, di_ref, dg_ref, dlb_ref, dng_ref, dst_ref):
        @pl.when(pl.program_id(1) == 0)
        def _():
            dst_ref[...] = jnp.zeros_like(dst_ref)
            dlb_ref[...] = jnp.zeros_like(dlb_ref)
            dng_ref[...] = jnp.zeros_like(dng_ref)

        dst = dst_ref[...]
        dlb = jnp.zeros((1, hd), F32)
        dng = jnp.zeros((1, hd), F32)
        for s in reversed(range(nsub)):
            rows = slice(s * HGRN_CHUNK, (s + 1) * HGRN_CHUNK)
            _, vjp = jax.vjp(chunk, q_ref[rows, :], f_ref[rows, :], i_ref[rows, :], g_ref[rows, :],
                             lb_ref[...], ng_ref[...], sts_ref[0, s])
            dq, df, di, dg, dlb_s, dng_s, dst = vjp((dy_ref[rows, :], dst))
            dq_ref[rows, :] = dq
            df_ref[rows, :] = df
            di_ref[rows, :] = di
            dg_ref[rows, :] = dg
            dlb += dlb_s
            dng += dng_s
        dst_ref[...] = dst
        dlb_ref[...] += dlb
        dng_ref[...] += dng

    tok_out = jax.ShapeDtypeStruct((T, HGRN_DIM), F32)
    vec_out = jax.ShapeDtypeStruct((1, HGRN_DIM), F32)
    return pl.pallas_call(
        body, name=name, grid=(HGRN_HEADS, nb),
        in_specs=proj_specs + [vec_spec, vec_spec, st_spec, tok_spec],
        out_specs=[tok_spec] * 4 + [vec_spec, vec_spec],
        out_shape=[tok_out] * 4 + [vec_out, vec_out],
        scratch_shapes=[pltpu.VMEM((hd, hd), F32)],
        compiler_params=_cparams(("arbitrary", "arbitrary")),
    )(proj, proj, proj, proj, lb, ng, states, dmix)


S5_LANES = 512
S5_ROWS = 512


def _cmul(ar, ai, br, bi):
    return ar * br - ai * bi, ar * bi + ai * br


def _power_table(ar, ai, exps):
    a2 = _cmul(ar, ai, ar, ai)
    a4 = _cmul(*a2, *a2)
    e = exps - 1
    pr = jnp.broadcast_to(ar, exps.shape)
    pi = jnp.broadcast_to(ai, exps.shape)
    for bit, (fr, fi) in enumerate(((ar, ai), a2, a4)):
        nr, ni = _cmul(pr, pi, fr, fi)
        on = ((e >> bit) & 1) == 1
        pr, pi = jnp.where(on, nr, pr), jnp.where(on, ni, pi)
    return pr, pi, a2, a4


def _s5_scan_fwd(bu, a_re, a_im, *, name):
    T = bu.shape[0]
    w, tr = S5_LANES, min(S5_ROWS, T)
    ncol, nt = S5_WIDTH // w, T // tr

    def body(br_ref, bi_ref, ar_ref, ai_ref, sr_ref, si_ref, carry_ref):
        @pl.when(pl.program_id(1) == 0)
        def _():
            carry_ref[...] = jnp.zeros_like(carry_ref)

        ar, ai = ar_ref[...], ai_ref[...]
        rowi = lax.broadcasted_iota(jnp.int32, (SUBLANE, w), 0)
        pr, pi, a2, a4 = _power_table(ar, ai, rowi + 1)

        def tile(i, carry):
            cr, ci = carry
            rows = pl.ds(pl.multiple_of(i * SUBLANE, SUBLANE), SUBLANE)
            xr, xi = br_ref[rows, :], bi_ref[rows, :]
            for s, (fr, fi) in ((1, (ar, ai)), (2, a2), (4, a4)):
                keep = rowi >= s
                zr = jnp.where(keep, pltpu.roll(xr, s, 0), 0.0)
                zi = jnp.where(keep, pltpu.roll(xi, s, 0), 0.0)
                xr, xi = xr + fr * zr - fi * zi, xi + fr * zi + fi * zr
            xr, xi = xr + pr * cr - pi * ci, xi + pr * ci + pi * cr
            sr_ref[rows, :] = xr
            si_ref[rows, :] = xi
            return xr[SUBLANE - 1:SUBLANE, :], xi[SUBLANE - 1:SUBLANE, :]

        cr, ci = lax.fori_loop(0, tr // SUBLANE, tile, (carry_ref[0:1, :], carry_ref[1:2, :]))
        carry_ref[0:1, :] = cr
        carry_ref[1:2, :] = ci

    out = jax.ShapeDtypeStruct((T, S5_WIDTH), F32)
    return pl.pallas_call(
        body, name=name, grid=(ncol, nt),
        in_specs=[pl.BlockSpec((tr, w), lambda j, t: (t, j)), pl.BlockSpec((tr, w), lambda j, t: (t, j + ncol)),
                  pl.BlockSpec((1, w), lambda j, t: (0, j)), pl.BlockSpec((1, w), lambda j, t: (0, j))],
        out_specs=[pl.BlockSpec((tr, w), lambda j, t: (t, j))] * 2,
        out_shape=[out, out],
        scratch_shapes=[pltpu.VMEM((2, w), F32)],
        compiler_params=_cparams(("parallel", "arbitrary")),
    )(bu, bu, a_re, a_im)


def _s5_scan_bwd(g_re, g_im, s_re, s_im, a_re, a_im, *, name):
    T = g_re.shape[0]
    w, tr = S5_LANES, min(S5_ROWS, T)
    ncol, nt = S5_WIDTH // w, T // tr
    ntile = tr // SUBLANE

    def body(gr_ref, gi_ref, sr_ref, si_ref, ar_ref, ai_ref, lr_ref, li_ref, dar_ref, dai_ref, carry_ref):
        @pl.when(pl.program_id(1) == 0)
        def _():
            carry_ref[...] = jnp.zeros_like(carry_ref)
            dar_ref[...] = jnp.zeros_like(dar_ref)
            dai_ref[...] = jnp.zeros_like(dai_ref)

        ar, ai = ar_ref[...], -ai_ref[...]
        rowi = lax.broadcasted_iota(jnp.int32, (SUBLANE, w), 0)
        pr, pi, a2, a4 = _power_table(ar, ai, SUBLANE - rowi)
        last = rowi == SUBLANE - 1

        def tile(i, carry):
            cr, ci, dar, dai = carry
            rows = pl.ds(pl.multiple_of((ntile - 1 - i) * SUBLANE, SUBLANE), SUBLANE)
            xr, xi = gr_ref[rows, :], gi_ref[rows, :]
            for s, (fr, fi) in ((1, (ar, ai)), (2, a2), (4, a4)):
                keep = rowi < SUBLANE - s
                zr = jnp.where(keep, pltpu.roll(xr, SUBLANE - s, 0), 0.0)
                zi = jnp.where(keep, pltpu.roll(xi, SUBLANE - s, 0), 0.0)
                xr, xi = xr + fr * zr - fi * zi, xi + fr * zi + fi * zr
            xr, xi = xr + pr * cr - pi * ci, xi + pr * ci + pi * cr
            lr_ref[rows, :] = xr
            li_ref[rows, :] = xi
            nr = jnp.where(last, cr, pltpu.roll(xr, SUBLANE - 1, 0))
            ni = jnp.where(last, ci, pltpu.roll(xi, SUBLANE - 1, 0))
            sr, si = sr_ref[rows, :], si_ref[rows, :]
            return xr[0:1, :], xi[0:1, :], dar + nr * sr + ni * si, dai + ni * sr - nr * si

        cr, ci, dar, dai = lax.fori_loop(
            0, ntile, tile, (carry_ref[0:1, :], carry_ref[1:2, :], jnp.zeros((SUBLANE, w), F32), jnp.zeros((SUBLANE, w), F32)))
        carry_ref[0:1, :] = cr
        carry_ref[1:2, :] = ci
        dar_ref[...] += dar
        dai_ref[...] += dai

    tok = pl.BlockSpec((tr, w), lambda j, t: (nt - 1 - t, j))
    vec = pl.BlockSpec((1, w), lambda j, t: (0, j))
    acc = pl.BlockSpec((SUBLANE, w), lambda j, t: (0, j))
    out = jax.ShapeDtypeStruct((T, S5_WIDTH), F32)
    accs = jax.ShapeDtypeStruct((SUBLANE, S5_WIDTH), F32)
    return pl.pallas_call(
        body, name=name, grid=(ncol, nt),
        in_specs=[tok, tok, tok, tok, vec, vec], out_specs=[tok, tok, acc, acc],
        out_shape=[out, out, accs, accs],
        scratch_shapes=[pltpu.VMEM((2, w), F32)],
        compiler_params=_cparams(("parallel", "arbitrary")),
    )(g_re, g_im, s_re, s_im, a_re, a_im)


ATTN_BLOCK = 512
_NEG = -1e30


def _attn_scores(qn, qr, kn, kr, q0, k0):
    s = (_mxu(qn, kn, 1, 1) + _mxu(qr, kr, 1, 1)) * (MLA_QK ** -0.5)
    qpos = q0 + lax.broadcasted_iota(jnp.int32, s.shape, 0)
    kpos = k0 + lax.broadcasted_iota(jnp.int32, s.shape, 1)
    return s, kpos <= qpos


def _attn_fwd(q_all, q_rope, kv, k_rope, *, name):
    T = q_all.shape[0]
    tq = min(ATTN_BLOCK, T)
    nq = T // tq

    def body(qn_ref, qr_ref, kn_ref, v_ref, kr_ref, o_ref, lse_ref):
        i = pl.program_id(1)
        qn, qr = qn_ref[...], qr_ref[0]

        def step(j, carry):
            m, l, acc = carry
            ks = pl.ds(pl.multiple_of(j * tq, tq), tq)
            s, ok = _attn_scores(qn, qr, kn_ref[ks, :], kr_ref[ks, :], i * tq, j * tq)
            s = jnp.where(ok, s, _NEG)
            m_new = jnp.maximum(m, jnp.max(s, axis=-1, keepdims=True))
            p = jnp.exp(s - m_new)
            alpha = jnp.exp(m - m_new)
            return m_new, alpha * l + jnp.sum(p, axis=-1, keepdims=True), alpha * acc + _mxu(p, v_ref[ks, :], 1, 0)

        m, l, acc = lax.fori_loop(0, i + 1, step, (jnp.full((tq, 1), _NEG, F32), jnp.zeros((tq, 1), F32), jnp.zeros((tq, MLA_V), F32)))
        o_ref[...] = acc / l
        lse_ref[0] = m + jnp.log(l)

    return pl.pallas_call(
        body, name=name, grid=(MLA_HEADS, nq),
        in_specs=[pl.BlockSpec((tq, MLA_NOPE), lambda h, i: (i, h)), pl.BlockSpec((1, tq, MLA_ROPE), lambda h, i: (h, i, 0)),
                  pl.BlockSpec((T, MLA_NOPE), lambda h, i: (0, 2 * h)), pl.BlockSpec((T, MLA_V), lambda h, i: (0, 2 * h + 1)),
                  pl.BlockSpec((T, MLA_ROPE), lambda h, i: (0, 0))],
        out_specs=[pl.BlockSpec((tq, MLA_V), lambda h, i: (i, h)), pl.BlockSpec((1, tq, 1), lambda h, i: (h, i, 0))],
        out_shape=[jax.ShapeDtypeStruct((T, MLA_HEADS * MLA_V), F32), jax.ShapeDtypeStruct((MLA_HEADS, T, 1), F32)],
        compiler_params=_cparams(("arbitrary", "arbitrary")),
    )(q_all, q_rope, kv, kv, k_rope)


def _attn_bwd(q_all, q_rope, kv, k_rope, o, lse, do, *, name):
    T = q_all.shape[0]
    tk = min(ATTN_BLOCK, T)
    nk = T // tk
    scale = MLA_QK ** -0.5

    def body(qn_ref, qr_ref, kv_ref, kr_ref, o_ref, lse_ref, do_ref, dqn_ref, dqr_ref, dkv_ref, dkr_ref):
        h, j = pl.program_id(0), pl.program_id(1)

        @pl.when(j == 0)
        def _():
            dqn_ref[...] = jnp.zeros_like(dqn_ref)
            dqr_ref[...] = jnp.zeros_like(dqr_ref)

        @pl.when((j == 0) & (h == 0))
        def _():
            dkr_ref[...] = jnp.zeros_like(dkr_ref)

        krows = pl.ds(pl.multiple_of(j * tk, tk), tk)
        kn, v, kr = kv_ref[:, :MLA_NOPE], kv_ref[:, MLA_NOPE:], kr_ref[krows, :]

        def step(i, carry):
            dkn, dv, dkr = carry
            qs = pl.ds(pl.multiple_of(i * tk, tk), tk)
            qn, qr, dob = qn_ref[qs, :], qr_ref[0, qs, :], do_ref[qs, :]
            s, ok = _attn_scores(qn, qr, kn, kr, i * tk, j * tk)
            p = jnp.where(ok, jnp.exp(s - lse_ref[0, qs, :]), 0.0)
            dp = _mxu(dob, v, 1, 1)
            delta = jnp.sum(dob * o_ref[qs, :], axis=-1, keepdims=True)
            ds = p * (dp - delta) * scale
            dqn_ref[qs, :] += _mxu(ds, kn, 1, 0)
            dqr_ref[0, qs, :] += _mxu(ds, kr, 1, 0)
            return dkn + _mxu(ds, qn, 0, 0), dv + _mxu(p, dob, 0, 0), dkr + _mxu(ds, qr, 0, 0)

        dkn, dv, dkr = lax.fori_loop(j, nk, step, (jnp.zeros((tk, MLA_NOPE), F32), jnp.zeros((tk, MLA_V), F32), jnp.zeros((tk, MLA_ROPE), F32)))
        dkv_ref[:, :MLA_NOPE] = dkn
        dkv_ref[:, MLA_NOPE:] = dv
        dkr_ref[krows, :] += dkr

    head_cols = pl.BlockSpec((T, MLA_NOPE), lambda h, j: (0, h))
    head_rope = pl.BlockSpec((1, T, MLA_ROPE), lambda h, j: (h, 0, 0))
    kv_spec = pl.BlockSpec((tk, MLA_NOPE + MLA_V), lambda h, j: (j, h))
    kr_spec = pl.BlockSpec((T, MLA_ROPE), lambda h, j: (0, 0))
    return pl.pallas_call(
        body, name=name, grid=(MLA_HEADS, nk),
        in_specs=[head_cols, head_rope, kv_spec, kr_spec, head_cols, pl.BlockSpec((1, T, 1), lambda h, j: (h, 0, 0)), head_cols],
        out_specs=[head_cols, head_rope, kv_spec, kr_spec],
        out_shape=[jax.ShapeDtypeStruct((T, MLA_HEADS * MLA_NOPE), F32), jax.ShapeDtypeStruct((MLA_HEADS, T, MLA_ROPE), F32),
                   jax.ShapeDtypeStruct((T, MLA_HEADS * (MLA_NOPE + MLA_V)), F32), jax.ShapeDtypeStruct((T, MLA_ROPE), F32)],
        compiler_params=_cparams(("arbitrary", "arbitrary")),
    )(q_all, q_rope, kv, k_rope, o, lse, do)


def _s5_discretize(a_re, a_im, log_dt, bt_re, bt_im, lb_logits):
    dt = jnp.exp(log_dt)
    mag = jnp.exp(a_re * dt)
    abr, abi = mag * jnp.cos(a_im * dt), mag * jnp.sin(a_im * dt)
    den = a_re * a_re + a_im * a_im
    xr, xi = abr - 1.0, abi
    cr = ((xr * a_re + xi * a_im) / den)[:, None, :]
    ci = ((xi * a_re - xr * a_im) / den)[:, None, :]
    e = jnp.exp(lb_logits - jnp.max(lb_logits, axis=0, keepdims=True))
    lb = e[0:1, :] / jnp.sum(e, axis=0, keepdims=True)
    return abr, abi, cr * bt_re - ci * bt_im, cr * bt_im + ci * bt_re, lb


def _whole(shape):
    return pl.BlockSpec(shape, lambda: (0,) * len(shape))


def _s5_params_fwd(a_re, a_im, log_dt, bt_re, bt_im, lb_logits):
    ins = (a_re, a_im, log_dt, bt_re, bt_im, lb_logits)
    outs = [jax.ShapeDtypeStruct(s, F32) for s in (a_re.shape, a_re.shape, bt_re.shape, bt_re.shape, (1, lb_logits.shape[1]))]

    def body(*refs):
        res = _s5_discretize(*[r[...] for r in refs[:6]])
        for o_ref, o in zip(refs[6:], res):
            o_ref[...] = o

    return pl.pallas_call(body, name="s5_params_fwd", in_specs=[_whole(a.shape) for a in ins],
                          out_specs=[_whole(o.shape) for o in outs], out_shape=outs, compiler_params=_cparams())(*ins)


def _s5_params_bwd(a_re, a_im, log_dt, bt_re, bt_im, lb_logits, d_abr, d_abi, d_bbr, d_bbi, d_lb):
    ins = (a_re, a_im, log_dt, bt_re, bt_im, lb_logits, d_abr, d_abi, d_bbr, d_bbi, d_lb)
    outs = [jax.ShapeDtypeStruct(a.shape, F32) for a in ins[:6]]

    def body(*refs):
        _, vjp = jax.vjp(_s5_discretize, *[r[...] for r in refs[:6]])
        for o_ref, o in zip(refs[11:], vjp(tuple(r[...] for r in refs[6:11]))):
            o_ref[...] = o

    return pl.pallas_call(body, name="s5_params_bwd", in_specs=[_whole(a.shape) for a in ins],
                          out_specs=[_whole(o.shape) for o in outs], out_shape=outs, compiler_params=_cparams())(*ins)


def _adamw(w, g, m, v, *, name):
    R, C = w.shape
    tr = _pick(R, (256, 128, 64, 32, 16, 8)) if R % SUBLANE == 0 else R
    slabs = g.shape[0] if g.ndim == 3 else 0

    def body(w_ref, g_ref, m_ref, v_ref, *outs):
        if slabs:
            gv = g_ref[0].astype(F32)
            for s in range(1, slabs):
                gv = gv + g_ref[s].astype(F32)
            outs[0][...] = gv
            outs = outs[1:]
        else:
            gv = g_ref[...]
        d_ref, mo_ref, vo_ref = outs
        m2 = ADAM_B1 * m_ref[...] + (1.0 - ADAM_B1) * gv
        v2 = ADAM_B2 * v_ref[...] + (1.0 - ADAM_B2) * (gv * gv)
        m_hat = m2 / (1.0 - ADAM_B1 ** ADAM_STEP)
        v_hat = v2 / (1.0 - ADAM_B2 ** ADAM_STEP)
        d_ref[...] = -ADAM_LR * (m_hat / (jnp.sqrt(v_hat) + ADAM_EPS) + ADAM_WD * w_ref[...])
        mo_ref[...] = m2
        vo_ref[...] = v2

    spec = pl.BlockSpec((tr, C), lambda i: (i, 0))
    g_spec = pl.BlockSpec((slabs, tr, C), lambda i: (0, i, 0)) if slabs else spec
    out = jax.ShapeDtypeStruct((R, C), F32)
    n_out = 4 if slabs else 3
    return pl.pallas_call(body, name=name, grid=(R // tr,), in_specs=[spec, g_spec, spec, spec], out_specs=[spec] * n_out,
                          out_shape=[out] * n_out, compiler_params=_cparams(("parallel",)))(w, g, m, v)


N_CHIPS = 4
N_CORES = 2


def _exchange(srcs, *, group, scatter, name):
    chips = group == "chips"
    n_arr = len(srcs)
    if chips:
        out_shapes = [s.shape if scatter else (N_CHIPS,) + s.shape for s in srcs]
        per_arr = N_CHIPS - 1
    else:
        out_shapes = [s.shape if scatter else s.shape[:1] + (N_CORES,) + s.shape[1:] for s in srcs]
        per_arr = N_CHIPS

    def body(*refs):
        src_refs, out_refs = refs[:n_arr], refs[n_arr:2 * n_arr]
        send_sems, recv_sems, local_sems = refs[2 * n_arr:]
        x, y, c = lax.axis_index("x"), lax.axis_index("y"), lax.axis_index("c")
        me_chip = 2 * x + y
        sends, recvs, locals_ = [], [], []
        for a, (s_ref, o_ref) in enumerate(zip(src_refs, out_refs)):
            if chips:
                peers = [((1 - x, y, c), 2 * (1 - x) + y), ((x, 1 - y, c), 2 * x + 1 - y), ((1 - x, 1 - y, c), 2 * (1 - x) + 1 - y)]
                plan = [(dev, s_ref.at[p] if scatter else s_ref, o_ref.at[me_chip], o_ref.at[p]) for dev, p in peers]
                own = [(s_ref.at[me_chip] if scatter else s_ref, o_ref.at[me_chip])]
            else:
                sib = (x, y, 1 - c)
                plan = [(sib, s_ref.at[k, 1 - c] if scatter else s_ref.at[k], o_ref.at[k, c], o_ref.at[k, 1 - c]) for k in range(N_CHIPS)]
                own = [(s_ref.at[k, c] if scatter else s_ref.at[k], o_ref.at[k, c]) for k in range(N_CHIPS)]
            for j, (dev, src, dst_there, dst_here) in enumerate(plan):
                k = a * per_arr + j
                sends.append(pltpu.make_async_remote_copy(src_ref=src, dst_ref=dst_there, send_sem=send_sems.at[k],
                                                          recv_sem=recv_sems.at[k], device_id=dev, device_id_type=MESH))
                recvs.append(pltpu.make_async_remote_copy(src_ref=src, dst_ref=dst_here, send_sem=send_sems.at[k],
                                                          recv_sem=recv_sems.at[k], device_id=dev, device_id_type=MESH))
            for j, (src, dst) in enumerate(own):
                locals_.append(pltpu.make_async_copy(src, dst, local_sems.at[a * len(own) + j]))
        for cp in sends + locals_:
            cp.start()
        for cp in recvs:
            cp.wait_recv()
        for cp in sends:
            cp.wait_send()
        for cp in locals_:
            cp.wait()

    n_remote = n_arr * per_arr
    n_local = n_arr * (1 if chips else N_CHIPS)
    any_spec = pl.BlockSpec(memory_space=pl.ANY)
    return pl.pallas_call(
        body, name=name, in_specs=[any_spec] * n_arr, out_specs=[any_spec] * n_arr,
        out_shape=[jax.ShapeDtypeStruct(shp, s.dtype) for shp, s in zip(out_shapes, srcs)],
        scratch_shapes=[pltpu.SemaphoreType.DMA((n_remote,)), pltpu.SemaphoreType.DMA((n_remote,)), pltpu.SemaphoreType.DMA((n_local,))],
        compiler_params=pltpu.CompilerParams(has_side_effects=True),
    )(*srcs)


def _sum_pairs(buf, *, name):
    _, _, R, C = buf.shape
    tr = _pick(R, (512, 256, 128, 64, 32, 16, 8)) if R % SUBLANE == 0 else R

    def body(b_ref, o_ref):
        o_ref[0] = (b_ref[0, 0].astype(F32) + b_ref[0, 1].astype(F32)).astype(o_ref.dtype)

    return pl.pallas_call(body, name=name, grid=(N_CHIPS, R // tr),
                          in_specs=[pl.BlockSpec((1, N_CORES, tr, C), lambda k, i: (k, 0, i, 0))],
                          out_specs=pl.BlockSpec((1, tr, C), lambda k, i: (k, i, 0)),
                          out_shape=jax.ShapeDtypeStruct((N_CHIPS, R, C), buf.dtype),
                          compiler_params=_cparams(("parallel", "parallel")))(buf)


def _sum_slabs(buf, out_dtype, *, name):
    G, R, C = buf.shape
    tr = _pick(R, (512, 256, 128, 64, 32, 16, 8))

    def body(b_ref, o_ref):
        acc = b_ref[0].astype(F32)
        for s in range(1, G):
            acc = acc + b_ref[s].astype(F32)
        o_ref[...] = acc.astype(out_dtype)

    return pl.pallas_call(body, name=name, grid=(R // tr,), in_specs=[pl.BlockSpec((G, tr, C), lambda i: (0, i, 0))],
                          out_specs=pl.BlockSpec((tr, C), lambda i: (i, 0)), out_shape=jax.ShapeDtypeStruct((R, C), out_dtype),
                          compiler_params=_cparams(("parallel",)))(buf)


def _rms_fwd_fn(h, g):
    return (_rms(h, g),)


def _rms_bwd_fn(h, dhn, dres, g):
    _, vjp = jax.vjp(_rms, h, g)
    dh, dg = vjp(dhn)
    return dh + dres, dg


def _loss_fn(h, tgt, g):
    y, vjp = jax.vjp(_rms, h, g)
    diff = y - tgt
    dh, dg = vjp(diff * (1.0 / D_MODEL))
    return dh, dg, (0.5 / D_MODEL) * jnp.sum(diff * diff, axis=0, keepdims=True)


def _s5_act(ys, u, d):
    return _gelu(ys + d * u)


def _s5_gate(z, gl, b):
    return z * _sigmoid(gl + b)


def _s5_act_fn(ys, u, d):
    return (_s5_act(ys, u, d),)


def _s5_mix_fn(ya, z, gl, b):
    return (jnp.concatenate([ya, _s5_gate(z, gl, b)], axis=1),)


def _s5_gate_bwd_fn(z, gl, dyb, b):
    _, vjp = jax.vjp(_s5_gate, z, gl, b)
    return vjp(dyb)


def _s5_act_bwd_fn(ys, u, dz1, dz2, d):
    _, vjp = jax.vjp(_s5_act, ys, u, d)
    return vjp(dz1 + dz2)


def _dproj_fn(dq, df, di, dg, du1, du2):
    return (jnp.concatenate([dq, df, di, dg, du1 + du2], axis=1),)


def _rope_pair(r1, r2, pos, freqs):
    ang = pos.astype(F32) * freqs
    c, s = jnp.cos(ang), jnp.sin(ang)
    return r1 * c - r2 * s, r1 * s + r2 * c


_ODD_SPLITS = (0, MLA_Q_RANK, MLA_Q_RANK + MLA_KV_RANK, MLA_Q_RANK + MLA_KV_RANK + LANE, ODD_IN_PAD)


def _mla_prep(cq, ckv, k1, k2, qg, kvg, pos, freqs):
    ko1, ko2 = _rope_pair(k1, k2, pos, freqs)
    return _rms(cq, qg), _rms(ckv, kvg), ko1, ko2


def _mla_prep_fn(proj, pos, qg, kvg, freqs):
    parts = [proj[:, a:b] for a, b in zip(_ODD_SPLITS[:-1], _ODD_SPLITS[1:])]
    return _mla_prep(*parts, qg, kvg, pos, freqs)


def _mla_prep_bwd_fn(proj, pos, dqn, dkvn, dko1, dko2, qg, kvg, freqs):
    parts = [proj[:, a:b] for a, b in zip(_ODD_SPLITS[:-1], _ODD_SPLITS[1:])]
    _, vjp = jax.vjp(lambda *a: _mla_prep(*a, pos, freqs), *parts, qg, kvg)
    dcq, dckv, dk1, dk2, dqg, dkvg = vjp((dqn, dkvn, dko1, dko2))
    return jnp.concatenate([dcq, dckv, dk1, dk2], axis=1), dqg, dkvg


def _rope_q_fn(r1, r2, pos, freqs):
    return _rope_pair(r1, r2, pos, freqs)


def _rope_q_bwd_fn(dqn, do1, do2, pos, freqs):
    dr1, dr2 = _rope_pair(do1, do2, pos, -freqs)
    return (jnp.concatenate([dqn, dr1, dr2], axis=1),)


W_NAMES = ("norm_mix_g", "norm_ffn_g", "final_norm_g", "even_w_in", "hgrn_lb_logits", "hgrn_norm_g", "s5_a_re", "s5_a_im",
           "s5_log_dt", "s5_b_re", "s5_b_im", "s5_c_re", "s5_c_im", "s5_d", "s5_w_glu", "s5_b_glu", "even_w_out", "odd_w_in",
           "mla_q_norm_g", "mla_w_uq", "mla_kv_norm_g", "mla_w_ukv", "odd_w_out", "ffn_w_in", "ffn_conv_w", "ffn_conv_b",
           "ffn_w_out")
BIG_UNITS = (("even_w_in", 0, "col"), ("s5_w_glu", 0, "row"), ("even_w_out", 0, "row"), ("odd_w_in", 0, "row"),
             ("mla_w_uq", 0, "col"), ("mla_w_ukv", 0, "col"), ("odd_w_out", 0, "row"),
             ("ffn_w_in", 0, "col"), ("ffn_w_in", 1, "col"), ("ffn_w_out", 0, "row"), ("ffn_w_out", 1, "row"))
BIG_NAMES = tuple(dict.fromkeys(u[0] for u in BIG_UNITS))
SMALL_SHARDED = (("mla_q_norm_g", 1), ("mla_kv_norm_g", 1), ("ffn_conv_w", 2))
SMALL_SHARDED_NAMES = tuple(s[0] for s in SMALL_SHARDED)
REPLICATED = tuple(n for n in W_NAMES if n not in BIG_NAMES + SMALL_SHARDED_NAMES)


def _pack(flats, cols, row_mult):
    flat = jnp.concatenate(flats, axis=-1)
    pad = (-flat.shape[-1]) % (cols * row_mult)
    flat = jnp.pad(flat, [(0, 0)] * (flat.ndim - 1) + [(0, pad)])
    return flat.reshape(flat.shape[:-1] + (-1, cols))


def _unpack(flat, shapes):
    out, off = [], 0
    for shp in shapes:
        n = int(np.prod(shp))
        out.append(flat[..., off:off + n].reshape(flat.shape[:-1] + tuple(shp)))
        off += n
    return out


def _gather_weights(w):
    srcs = [w[n][l].astype(BF16) for n, l, _ in BIG_UNITS]
    srcs.append(_pack([w[n].reshape(-1) for n in SMALL_SHARDED_NAMES], LANE, SUBLANE))
    g1 = _exchange(srcs, group="chips", scatter=False, name="gather_w_chips")
    g2 = _exchange(g1, group="core", scatter=False, name="gather_w_core")
    big = {}
    for (n, l, kind), g in zip(BIG_UNITS, g2):
        r, c = g.shape[2:]
        big[(n, l)] = g.reshape(N_DEV * r, c) if kind == "row" else g.transpose(2, 0, 1, 3).reshape(r, N_DEV * c)
    parts = _unpack(g2[-1].reshape(N_DEV, -1), [w[n].shape for n in SMALL_SHARDED_NAMES])
    small = {}
    for (n, ax), p in zip(SMALL_SHARDED, parts):
        shp = list(w[n].shape)
        shp[ax] *= N_DEV
        small[n] = jnp.moveaxis(p, 0, ax).reshape(shp)
    return big, small


def _reduce_scatter_grads(g_big, g_small, w):
    srcs = []
    for n, l, kind in BIG_UNITS:
        g = g_big[(n, l)].astype(BF16)
        if kind == "row":
            srcs.append(g.reshape(N_CHIPS, N_CORES, g.shape[0] // N_DEV, g.shape[1]))
        else:
            srcs.append(g.reshape(g.shape[0], N_CHIPS, N_CORES, g.shape[1] // N_DEV).transpose(1, 2, 0, 3))
    flats = []
    for n, ax in SMALL_SHARDED:
        shp = list(w[n].shape)
        g = g_small[n].astype(F32).reshape(shp[:ax] + [N_DEV] + shp[ax:])
        flats.append(jnp.moveaxis(g, ax, 0).reshape(N_DEV, -1))
    small = _pack(flats, LANE, SUBLANE)
    srcs.append(small.reshape((N_CHIPS, N_CORES) + small.shape[1:]))
    r1 = _exchange(srcs, group="core", scatter=True, name="scatter_g_core")
    s1 = [_sum_pairs(r, name=f"sum_g_core_{i}") for i, r in enumerate(r1)]
    return _exchange(s1, group="chips", scatter=True, name="scatter_g_chips")


def _all_reduce_small(grads, w):
    vec = _pack([grads[n].reshape(-1).astype(F32) for n in REPLICATED], LANE, SUBLANE)
    a1 = _exchange([vec], group="chips", scatter=False, name="gather_small_chips")
    a2, = _exchange(a1, group="core", scatter=False, name="gather_small_core")
    tot = _sum_slabs(a2.reshape((N_DEV,) + vec.shape), F32, name="sum_small").reshape(-1)
    return dict(zip(REPLICATED, _unpack(tot, [w[n].shape for n in REPLICATED])))


def _block_diag(blocks):
    G, a, b = blocks.shape
    return jnp.einsum('gab,gk->gakb', blocks, jnp.eye(G, dtype=blocks.dtype)).reshape(G * a, G * b)


def _diag_blocks(mat, a, b):
    G = mat.shape[0] // a
    return jnp.einsum('gagb->gab', mat.reshape(G, a, G, b))


def _ffn_fwd(h, g, w_in, conv_w, conv_b, w_out, tag):
    hn, = _rows(_rms_fwd_fn, [h], [g], [(D_MODEL, BF16)], [], name=f"ffn{tag}_norm")
    au = _mm(hn, w_in, name=f"ffn{tag}_in")
    z = _ffn_mid_fwd(au, conv_w, conv_b, name=f"ffn{tag}_mid")
    return _mm(z, w_out, res=h, name=f"ffn{tag}_out"), (hn, au, z)


def _ffn_bwd(h, dh, saved, g, w_in, conv_w, conv_b, w_out, tag):
    hn, au, z = saved
    dz = _mm(dh, w_out, tb=True, name=f"ffn{tag}_dz")
    dw_out = _mm(z, dh, ta=True, out_dtype=BF16, name=f"ffn{tag}_dwout")
    da, du, dcw, dcb = _ffn_mid_bwd(au, dz, conv_w, conv_b, name=f"ffn{tag}_dmid")
    dhn = _mm(da, w_in, tb=True, b_cols=(0, D_FF), name=f"ffn{tag}_dhn_a")
    dhn = _mm(du, w_in, tb=True, b_cols=(D_FF, D_FF), res=dhn, name=f"ffn{tag}_dhn_u")
    dw_in = jnp.concatenate([_mm(hn, da, ta=True, out_dtype=BF16, name=f"ffn{tag}_dwin_a"),
                             _mm(hn, du, ta=True, out_dtype=BF16, name=f"ffn{tag}_dwin_u")], axis=1)
    dh_in, dg = _rows(_rms_bwd_fn, [h, dhn, dh], [g], [(D_MODEL, F32)], [(1, D_MODEL)], name=f"ffn{tag}_dnorm")
    return dh_in, dict(g=dg, w_in=dw_in, conv_w=dcw, conv_b=dcb, w_out=dw_out)


def kernel(x, positions, norm_mix_g, norm_ffn_g, final_norm_g, even_w_in, hgrn_lb_logits, hgrn_norm_g, s5_a_re, s5_a_im, s5_log_dt, s5_b_re, s5_b_im, s5_c_re, s5_c_im, s5_d, s5_w_glu, s5_b_glu, even_w_out, odd_w_in, mla_q_norm_g, mla_w_uq, mla_kv_norm_g, mla_w_ukv, odd_w_out, ffn_w_in, ffn_conv_w, ffn_conv_b, ffn_w_out, loss_target, m_norm_mix_g, m_norm_ffn_g, m_final_norm_g, m_even_w_in, m_hgrn_lb_logits, m_hgrn_norm_g, m_s5_a_re, m_s5_a_im, m_s5_log_dt, m_s5_b_re, m_s5_b_im, m_s5_c_re, m_s5_c_im, m_s5_d, m_s5_w_glu, m_s5_b_glu, m_even_w_out, m_odd_w_in, m_mla_q_norm_g, m_mla_w_uq, m_mla_kv_norm_g, m_mla_w_ukv, m_odd_w_out, m_ffn_w_in, m_ffn_conv_w, m_ffn_conv_b, m_ffn_w_out, v_norm_mix_g, v_norm_ffn_g, v_final_norm_g, v_even_w_in, v_hgrn_lb_logits, v_hgrn_norm_g, v_s5_a_re, v_s5_a_im, v_s5_log_dt, v_s5_b_re, v_s5_b_im, v_s5_c_re, v_s5_c_im, v_s5_d, v_s5_w_glu, v_s5_b_glu, v_even_w_out, v_odd_w_in, v_mla_q_norm_g, v_mla_w_uq, v_mla_kv_norm_g, v_mla_w_ukv, v_odd_w_out, v_ffn_w_in, v_ffn_conv_w, v_ffn_conv_b, v_ffn_w_out):
    given = dict(locals())
    w = {n: given[n] for n in W_NAMES}
    mom = {n: given["m_" + n] for n in W_NAMES}
    var = {n: given["v_" + n] for n in W_NAMES}
    T = x.shape[1]
    h0 = x[0]
    tgt = loss_target[0]
    pos = positions.reshape(T, 1)

    full, full_small = _gather_weights(w)
    w_ein = full["even_w_in", 0]
    w_glu = full["s5_w_glu", 0]
    w_eout = full["even_w_out", 0]
    w_oin = full["odd_w_in", 0]
    zpad = jnp.zeros((D_MODEL, LANE - MLA_ROPE // 2), BF16)
    kr0 = MLA_Q_RANK + MLA_KV_RANK
    w_oin_pad = jnp.concatenate([w_oin[:, :kr0], w_oin[:, kr0:kr0 + MLA_ROPE // 2], zpad, w_oin[:, kr0 + MLA_ROPE // 2:], zpad], axis=1)
    qg, kvg = full_small["mla_q_norm_g"], full_small["mla_kv_norm_g"]
    w_uq3 = full["mla_w_uq", 0].reshape(MLA_Q_RANK, MLA_HEADS, MLA_QK)
    half = MLA_ROPE // 2
    w_uq_perm = jnp.concatenate([w_uq3[:, :, :MLA_NOPE].reshape(MLA_Q_RANK, -1),
                                 w_uq3[:, :, MLA_NOPE:MLA_NOPE + half].reshape(MLA_Q_RANK, -1),
                                 w_uq3[:, :, MLA_NOPE + half:].reshape(MLA_Q_RANK, -1)], axis=1)
    w_ukv = full["mla_w_ukv", 0]
    w_oout = full["odd_w_out", 0]
    w_fin = [full["ffn_w_in", 0], full["ffn_w_in", 1]]
    w_fout = [full["ffn_w_out", 0], full["ffn_w_out", 1]]
    conv_w = full_small["ffn_conv_w"]
    freqs = ROPE_THETA ** (-jnp.arange(0, MLA_ROPE, 2, dtype=F32) / MLA_ROPE)
    freqs_q = jnp.tile(freqs, MLA_HEADS)[None, :]
    freqs_k = jnp.concatenate([freqs, jnp.zeros((LANE - half,), F32)])[None, :]

    sp_in = (s5_a_re[0], s5_a_im[0], s5_log_dt[0][:, None], s5_b_re[0].transpose(0, 2, 1), s5_b_im[0].transpose(0, 2, 1),
             hgrn_lb_logits)
    abr, abi, bbt_re, bbt_im, lb0 = _s5_params_fwd(*sp_in)
    a_re, a_im = abr.reshape(1, S5_WIDTH), abi.reshape(1, S5_WIDTH)
    bb_re, bb_im = _block_diag(bbt_re).astype(BF16), _block_diag(bbt_im).astype(BF16)
    bb_cat = jnp.concatenate([bb_re, bb_im], axis=1)
    c_re = _block_diag(s5_c_re[0].transpose(0, 2, 1)).astype(BF16)
    c_im_neg = _block_diag(-s5_c_im[0].transpose(0, 2, 1)).astype(BF16)
    u_cols = (4 * HGRN_DIM, S5_DIM)

    hn0, = _rows(_rms_fwd_fn, [h0], [norm_mix_g[0:1]], [(D_MODEL, BF16)], [], name="mix0_norm")
    proj = _mm(hn0, w_ein, name="even_in")
    y_a, states = _hgrn_fwd(proj, lb0, hgrn_norm_g, name="hgrn_fwd")
    bu = _mm(proj, bb_cat, a_cols=u_cols, name="s5_bu")
    s_re, s_im = _s5_scan_fwd(bu, a_re, a_im, name="s5_scan_fwd")
    ys = _mm(s_im, c_im_neg, res=_mm(s_re, c_re, name="s5_y_re"), name="s5_y_im")
    z5, = _rows(_s5_act_fn, [ys, (proj,) + u_cols], [s5_d], [(S5_DIM, F32)], [], name="s5_act")
    gl = _mm(z5, w_glu, name="s5_glu")
    mixin, = _rows(_s5_mix_fn, [y_a, z5, gl], [s5_b_glu], [(D_MODEL, BF16)], [], name="s5_mix")
    h1 = _mm(mixin, w_eout, res=h0, name="even_out")
    h2, ffn0_saved = _ffn_fwd(h1, norm_ffn_g[0:1], w_fin[0], conv_w[0], ffn_conv_b[0:1], w_fout[0], 0)

    hn1, = _rows(_rms_fwd_fn, [h2], [norm_mix_g[1:2]], [(D_MODEL, BF16)], [], name="mix1_norm")
    proj_o = _mm(hn1, w_oin_pad, name="odd_in")
    qn, kvn, ko1, ko2 = _rows(_mla_prep_fn, [proj_o, pos], [qg, kvg, freqs_k],
                              [(MLA_Q_RANK, BF16), (MLA_KV_RANK, BF16), (LANE, F32), (LANE, F32)], [], name="mla_prep")
    q_all = _mm(qn, w_uq_perm, name="mla_uq")
    kv = _mm(kvn, w_ukv, name="mla_ukv")
    nope_w = MLA_HEADS * MLA_NOPE
    rope_w = MLA_HEADS * half
    o1, o2 = _rows(_rope_q_fn, [(q_all, nope_w, rope_w), (q_all, nope_w + rope_w, rope_w), pos], [freqs_q],
                   [(rope_w, F32), (rope_w, F32)], [], name="mla_rope_q")
    q_rope = jnp.concatenate([o1.reshape(T, MLA_HEADS, half), o2.reshape(T, MLA_HEADS, half)], axis=2).transpose(1, 0, 2)
    k_rope = jnp.concatenate([ko1[:, :half], ko2[:, :half]], axis=1)
    o, lse = _attn_fwd(q_all, q_rope, kv, k_rope, name="attn_fwd")
    h3 = _mm(o, w_oout, res=h2, name="odd_out")
    h4, ffn1_saved = _ffn_fwd(h3, norm_ffn_g[1:2], w_fin[1], conv_w[1], ffn_conv_b[1:2], w_fout[1], 1)

    dh4, d_final_g, loss_cols = _rows(_loss_fn, [h4, tgt], [final_norm_g[None, :]], [(D_MODEL, F32)],
                                      [(1, D_MODEL), (1, D_MODEL)], name="loss_head")
    loss = lax.psum(jnp.sum(loss_cols), ("x", "y", "c"))

    dh3, gf1 = _ffn_bwd(h3, dh4, ffn1_saved, norm_ffn_g[1:2], w_fin[1], conv_w[1], ffn_conv_b[1:2], w_fout[1], 1)
    do = _mm(dh3, w_oout, tb=True, name="odd_out_dx")
    d_w_oout = _mm(o, dh3, ta=True, out_dtype=BF16, name="odd_out_dw")
    dq_nope, dq_rope, dkv, dk_rope = _attn_bwd(q_all, q_rope, kv, k_rope, o, lse, do, name="attn_bwd")
    dq_rope_t = dq_rope.transpose(1, 0, 2)
    do1, do2 = dq_rope_t[:, :, :half].reshape(T, rope_w), dq_rope_t[:, :, half:].reshape(T, rope_w)
    lane_pad = ((0, 0), (0, LANE - half))
    dko1, dko2 = jnp.pad(dk_rope[:, :half], lane_pad), jnp.pad(dk_rope[:, half:], lane_pad)
    dq_all, = _rows(_rope_q_bwd_fn, [dq_nope, do1, do2, pos], [freqs_q], [(MLA_HEADS * MLA_QK, BF16)], [], name="mla_rope_q_bwd")
    d_w_uq_perm = _mm(qn, dq_all, ta=True, out_dtype=BF16, name="mla_uq_dw")
    dqn = _mm(dq_all, w_uq_perm, tb=True, name="mla_uq_dx")
    d_w_ukv = _mm(kvn, dkv, ta=True, out_dtype=BF16, name="mla_ukv_dw")
    dkvn = _mm(dkv, w_ukv, tb=True, name="mla_ukv_dx")
    dproj_o, d_qg, d_kvg = _rows(_mla_prep_bwd_fn, [proj_o, pos, dqn, dkvn, dko1, dko2], [qg, kvg, freqs_k],
                                 [(ODD_IN_PAD, BF16)], [(1, MLA_Q_RANK), (1, MLA_KV_RANK)], name="mla_prep_bwd")
    d_w_oin_pad = _mm(hn1, dproj_o, ta=True, out_dtype=BF16, name="odd_in_dw")
    dhn1 = _mm(dproj_o, w_oin_pad, tb=True, name="odd_in_dx")
    dh2, d_mix_g1 = _rows(_rms_bwd_fn, [h2, dhn1, dh3], [norm_mix_g[1:2]], [(D_MODEL, F32)], [(1, D_MODEL)], name="mix1_dnorm")
    d_w_oin = jnp.concatenate([d_w_oin_pad[:, :kr0 + half], d_w_oin_pad[:, kr0 + LANE:kr0 + LANE + half]], axis=1)
    d3 = d_w_uq_perm
    d_w_uq = jnp.concatenate([d3[:, :nope_w].reshape(MLA_Q_RANK, MLA_HEADS, MLA_NOPE),
                              d3[:, nope_w:nope_w + rope_w].reshape(MLA_Q_RANK, MLA_HEADS, half),
                              d3[:, nope_w + rope_w:].reshape(MLA_Q_RANK, MLA_HEADS, half)], axis=2).reshape(MLA_Q_RANK, -1)

    dh1, gf0 = _ffn_bwd(h1, dh2, ffn0_saved, norm_ffn_g[0:1], w_fin[0], conv_w[0], ffn_conv_b[0:1], w_fout[0], 0)
    dmix = _mm(dh1, w_eout, tb=True, name="even_out_dx")
    d_w_eout = _mm(mixin, dh1, ta=True, out_dtype=BF16, name="even_out_dw")
    dq, df, di, dg, d_lb0, d_hgrn_g = _hgrn_bwd(proj, lb0, hgrn_norm_g, states, dmix, name="hgrn_bwd")
    dz1, dgl, d_b_glu = _rows(_s5_gate_bwd_fn, [z5, gl, (dmix, HGRN_DIM, S5_DIM)], [s5_b_glu],
                              [(S5_DIM, F32), (S5_DIM, BF16)], [(1, S5_DIM)], name="s5_gate_bwd")
    dz2 = _mm(dgl, w_glu, tb=True, name="s5_glu_dx")
    d_w_glu = _mm(z5, dgl, ta=True, out_dtype=BF16, name="s5_glu_dw")
    dys, du1, d_s5_d = _rows(_s5_act_bwd_fn, [ys, (proj,) + u_cols, dz1, dz2], [s5_d],
                             [(S5_DIM, BF16), (S5_DIM, F32)], [(1, S5_DIM)], name="s5_act_bwd")
    ds_re = _mm(dys, c_re, tb=True, name="s5_ds_re")
    ds_im = _mm(dys, c_im_neg, tb=True, name="s5_ds_im")
    d_c_re = _mm(s_re, dys, ta=True, name="s5_dc_re")
    d_c_im_neg = _mm(s_im, dys, ta=True, name="s5_dc_im")
    lam_re, lam_im, d_ar, d_ai = _s5_scan_bwd(ds_re, ds_im, s_re, s_im, a_re, a_im, name="s5_scan_bwd")
    du2 = _mm(lam_im, bb_im, tb=True, res=_mm(lam_re, bb_re, tb=True, name="s5_du_re"), name="s5_du_im")
    d_bb_re = _mm(proj, lam_re, ta=True, a_cols=u_cols, name="s5_dbb_re")
    d_bb_im = _mm(proj, lam_im, ta=True, a_cols=u_cols, name="s5_dbb_im")
    dproj, = _rows(_dproj_fn, [dq, df, di, dg, du1, du2], [], [(EVEN_IN, BF16)], [], name="even_dproj")
    d_w_ein = _mm(hn0, dproj, ta=True, out_dtype=BF16, name="even_in_dw")
    dhn0 = _mm(dproj, w_ein, tb=True, name="even_in_dx")
    grad_x, d_mix_g0 = _rows(_rms_bwd_fn, [h0, dhn0, dh1], [norm_mix_g[0:1]], [(D_MODEL, F32)], [(1, D_MODEL)], name="mix0_dnorm")
    sp_g = _s5_params_bwd(*sp_in, d_ar.sum(0).reshape(S5_GROUPS, S5_STATE), d_ai.sum(0).reshape(S5_GROUPS, S5_STATE),
                          _diag_blocks(d_bb_re, S5_GROUP, S5_STATE), _diag_blocks(d_bb_im, S5_GROUP, S5_STATE), d_lb0)
    d_a_re, d_a_im, d_log_dt, d_bt_re, d_bt_im, d_lb_logits = sp_g

    g_big = {("even_w_in", 0): d_w_ein, ("s5_w_glu", 0): d_w_glu, ("even_w_out", 0): d_w_eout, ("odd_w_in", 0): d_w_oin,
             ("mla_w_uq", 0): d_w_uq, ("mla_w_ukv", 0): d_w_ukv, ("odd_w_out", 0): d_w_oout,
             ("ffn_w_in", 0): gf0["w_in"], ("ffn_w_in", 1): gf1["w_in"], ("ffn_w_out", 0): gf0["w_out"], ("ffn_w_out", 1): gf1["w_out"]}
    g_small = dict(mla_q_norm_g=d_qg, mla_kv_norm_g=d_kvg, ffn_conv_w=jnp.stack([gf0["conv_w"], gf1["conv_w"]]))
    g_repl = dict(
        norm_mix_g=jnp.concatenate([d_mix_g0, d_mix_g1]), norm_ffn_g=jnp.concatenate([gf0["g"], gf1["g"]]),
        final_norm_g=d_final_g[0], hgrn_lb_logits=d_lb_logits, hgrn_norm_g=d_hgrn_g,
        s5_a_re=d_a_re[None], s5_a_im=d_a_im[None], s5_log_dt=d_log_dt[:, 0][None],
        s5_b_re=d_bt_re.transpose(0, 2, 1)[None], s5_b_im=d_bt_im.transpose(0, 2, 1)[None],
        s5_c_re=_diag_blocks(d_c_re, S5_STATE, S5_GROUP).transpose(0, 2, 1)[None],
        s5_c_im=-_diag_blocks(d_c_im_neg, S5_STATE, S5_GROUP).transpose(0, 2, 1)[None],
        s5_d=d_s5_d, s5_b_glu=d_b_glu, ffn_conv_b=jnp.concatenate([gf0["conv_b"], gf1["conv_b"]]))
    partial = _reduce_scatter_grads(g_big, g_small, w)
    grads = _all_reduce_small(g_repl, w)
    small_sum = _sum_slabs(partial[-1], F32, name="sum_g_small").reshape(-1)
    grads.update(zip(SMALL_SHARDED_NAMES, _unpack(small_sum, [w[n].shape for n in SMALL_SHARDED_NAMES])))

    delta, new_m, new_v = {}, {}, {}
    per_unit = {}
    for (n, l, _), slabs in zip(BIG_UNITS, partial):
        per_unit[n, l] = _adamw(w[n][l], slabs, mom[n][l], var[n][l], name=f"adamw_{n}_{l}")
    for n in BIG_NAMES:
        layers = [per_unit[n, l] for l in range(w[n].shape[0])]
        grads[n], delta[n], new_m[n], new_v[n] = (jnp.stack([lay[k] for lay in layers]) for k in range(4))
    small = [n for n in W_NAMES if n not in BIG_NAMES]
    packed = [_pack([t[n].reshape(-1) for n in small], LANE, SUBLANE) for t in (w, grads, mom, var)]
    outs = _adamw(*packed, name="adamw_small")
    small_shapes = [w[n].shape for n in small]
    for dst, o_ in zip((delta, new_m, new_v), outs):
        dst.update(zip(small, _unpack(o_.reshape(-1), small_shapes)))

    return (loss, grad_x[None], *[grads[n] for n in W_NAMES], *[delta[n] for n in W_NAMES],
            *[new_m[n] for n in W_NAMES], *[new_v[n] for n in W_NAMES])
```

```python
import functools
import math

import numpy as np
import jax
import jax.numpy as jnp
from jax import lax
from jax.experimental import pallas as pl
from jax.experimental.pallas import tpu as pltpu

F32 = jnp.float32
BF16 = jnp.bfloat16
_MXU_DTYPE = jnp.bfloat16
_HI = lax.Precision.HIGHEST

D_MODEL = 1024
HGRN_DIM = 512
HGRN_HEAD_DIM = 128
HGRN_HEADS = 4
HGRN_CHUNK = 64
S5_DIM = 512
S5_GROUPS = 32
S5_GROUP = 16
S5_STATE = 64
S5_WIDTH = S5_GROUPS * S5_STATE
EVEN_IN = 4 * HGRN_DIM + S5_DIM
MLA_HEADS = 8
MLA_Q_RANK = 384
MLA_KV_RANK = 256
MLA_NOPE = 128
MLA_ROPE = 64
MLA_V = 128
MLA_QK = MLA_NOPE + MLA_ROPE
ODD_IN = MLA_Q_RANK + MLA_KV_RANK + MLA_ROPE
ODD_IN_PAD = MLA_Q_RANK + MLA_KV_RANK + 2 * 128
ROPE_THETA = 10000.0
D_FF = 2816
EPS = 1e-6
ADAM_LR = 0.001
ADAM_B1 = 0.9
ADAM_B2 = 0.999
ADAM_EPS = 1e-08
ADAM_WD = 0.01
ADAM_STEP = 10

N_DEV = 8
LANE = 128
SUBLANE = 8
VMEM_LIMIT_BYTES = 56 * 1024 * 1024
MESH = pl.DeviceIdType.MESH


def _cparams(sem=None):
    return pltpu.CompilerParams(dimension_semantics=sem, vmem_limit_bytes=VMEM_LIMIT_BYTES)


def _pick(n, cands):
    for c in cands:
        if n % c == 0:
            return c
    raise ValueError(f"no tile for {n} in {cands}")


def _sigmoid(x):
    return 1.0 / (1.0 + jnp.exp(-x))


def _silu(x):
    return x * _sigmoid(x)


def _gelu(x):
    return 0.5 * x * (1.0 + jnp.tanh(math.sqrt(2.0 / math.pi) * (x + 0.044715 * (x * x * x))))


def _rms(x, g):
    return x * lax.rsqrt(jnp.mean(x * x, axis=-1, keepdims=True) + EPS) * g


def _mxu(a, b, ca, cb):
    return lax.dot_general(a.astype(_MXU_DTYPE), b.astype(_MXU_DTYPE), (((ca,), (cb,)), ((), ())),
                           preferred_element_type=F32)


@functools.partial(jax.custom_vjp, nondiff_argnums=(2, 3))
def _mxu_ad(a, b, ca, cb):
    return _mxu(a, b, ca, cb)


def _mxu_ad_fwd(a, b, ca, cb):
    return _mxu(a, b, ca, cb), (a, b)


def _mxu_ad_bwd(ca, cb, saved, g):
    a, b = saved
    fa, fb = 1 - ca, 1 - cb
    da = _mxu(g, b, 1, fb) if ca == 1 else _mxu(b, g, fb, 1)
    db = _mxu(a, g, fa, 0) if cb == 0 else _mxu(g, a, 0, fa)
    return da, db


_mxu_ad.defvjp(_mxu_ad_fwd, _mxu_ad_bwd)


def _tri(n, upper=False):
    row = lax.broadcasted_iota(jnp.int32, (n, n), 0)
    col = lax.broadcasted_iota(jnp.int32, (n, n), 1)
    return (col >= row) if upper else (col <= row)


def _cumsum_rows(x):
    return jnp.dot(_tri(x.shape[0]).astype(F32), x, precision=_HI, preferred_element_type=F32)


@jax.custom_vjp
def _cumsum_rows_ad(x):
    return _cumsum_rows(x)


def _cumsum_rows_ad_fwd(x):
    return _cumsum_rows(x), None


def _cumsum_rows_ad_bwd(_, g):
    return (jnp.dot(_tri(g.shape[0], upper=True).astype(F32), g, precision=_HI, preferred_element_type=F32),)


_cumsum_rows_ad.defvjp(_cumsum_rows_ad_fwd, _cumsum_rows_ad_bwd)


MM_VMEM_BUDGET = 36 * 1024 * 1024
MM_MAX_TILE = 1408


def _lane_divisors(n, cap, offs=()):
    return [d for d in range(min(n, cap) // LANE * LANE, 0, -LANE) if n % d == 0 and all(o % d == 0 for o in offs)]


def _mm_tiles(M, N, K, sa, sb, so, has_res, m_offs, n_offs, k_offs):
    best = None
    for tm in _lane_divisors(M, MM_MAX_TILE, m_offs):
        for tn in _lane_divisors(N, MM_MAX_TILE, n_offs):
            for tk in _lane_divisors(K, K, k_offs):
                nk = K // tk
                vmem = 2 * (tm * tk * sa + tk * tn * sb + tm * tn * so + tm * tn * 4 * has_res) + (tm * tn * 4 if nk > 1 else 0)
                if vmem <= MM_VMEM_BUDGET:
                    key = (-nk, tm * tn, tn)
                    if best is None or key > best[0]:
                        best = (key, tm, tn, tk)
                    break
    return best[1:]


def _mm(a, b, *, ta=False, tb=False, res=None, out_dtype=F32, a_cols=None, b_cols=None, name):
    a_minor = a.shape[1] if a_cols is None else a_cols[1]
    b_minor = b.shape[1] if b_cols is None else b_cols[1]
    K, M = (a.shape[0], a_minor) if ta else (a_minor, a.shape[0])
    N = b.shape[0] if tb else b_minor
    assert (b_minor if tb else b.shape[0]) == K, (a.shape, b.shape, ta, tb)
    a_off = 0 if a_cols is None else a_cols[0]
    b_off = 0 if b_cols is None else b_cols[0]
    has_res = res is not None
    tm, tn, tk = _mm_tiles(M, N, K, a.dtype.itemsize, b.dtype.itemsize, jnp.dtype(out_dtype).itemsize, has_res,
                           (a_off,) if ta else (), () if tb else (b_off,), ((a_off,) if not ta else ()) + ((b_off,) if tb else ()))
    nk = K // tk
    am, ak = (a_off // tm, 0) if ta else (0, a_off // tk)
    bn, bk = (0, b_off // tk) if tb else (b_off // tn, 0)
    a_spec = pl.BlockSpec((tk, tm), lambda i, j, k: (k, i + am)) if ta else pl.BlockSpec((tm, tk), lambda i, j, k: (i, k + ak))
    b_spec = pl.BlockSpec((tn, tk), lambda i, j, k: (j, k + bk)) if tb else pl.BlockSpec((tk, tn), lambda i, j, k: (k, j + bn))
    o_spec = pl.BlockSpec((tm, tn), lambda i, j, k: (i, j))
    ca, cb = (0 if ta else 1), (1 if tb else 0)

    def body(*refs):
        a_ref, b_ref = refs[0], refs[1]
        res_ref = refs[2] if has_res else None
        o_ref = refs[2 + has_res]
        part = _mxu(a_ref[...], b_ref[...], ca, cb)
        if nk == 1:
            o_ref[...] = (part + res_ref[...] if has_res else part).astype(out_dtype)
            return
        acc_ref = refs[3 + has_res]
        k = pl.program_id(2)

        @pl.when(k == 0)
        def _():
            acc_ref[...] = part

        @pl.when(k > 0)
        def _():
            acc_ref[...] += part

        @pl.when(k == nk - 1)
        def _():
            o_ref[...] = (acc_ref[...] + res_ref[...] if has_res else acc_ref[...]).astype(out_dtype)

    ins = [a, b] + ([res] if has_res else [])
    in_specs = [a_spec, b_spec] + ([o_spec] if has_res else [])
    return pl.pallas_call(
        body, name=name, grid=(M // tm, N // tn, nk),
        in_specs=in_specs, out_specs=o_spec,
        out_shape=jax.ShapeDtypeStruct((M, N), out_dtype),
        scratch_shapes=[pltpu.VMEM((tm, tn), F32)] if nk > 1 else [],
        compiler_params=_cparams(("parallel", "parallel", "arbitrary")),
    )(*ins)


def _rows(fn, row_ins, const_ins, row_outs, acc_outs, *, name, tm=256):
    norm = [(r, 0, r.shape[1]) if not isinstance(r, tuple) else r for r in row_ins]
    T = norm[0][0].shape[0]
    tm = min(tm, T)
    nr, nc, no, na = len(norm), len(const_ins), len(row_outs), len(acc_outs)

    def body(*refs):
        i = pl.program_id(0)
        vals = [r[...] for r in refs[:nr + nc]]
        outs = fn(*vals)
        for o_ref, o in zip(refs[nr + nc:nr + nc + no], outs[:no]):
            o_ref[...] = o.astype(o_ref.dtype)
        for a_ref, o in zip(refs[nr + nc + no:], outs[no:]):
            @pl.when(i == 0)
            def _(a_ref=a_ref, o=o):
                a_ref[...] = o

            @pl.when(i > 0)
            def _(a_ref=a_ref, o=o):
                a_ref[...] += o

    in_specs = []
    for arr, off, w in norm:
        assert off % w == 0, (off, w)
        in_specs.append(pl.BlockSpec((tm, w), lambda i, b=off // w: (i, b)))
    for c in const_ins:
        in_specs.append(pl.BlockSpec(c.shape, lambda i: (0, 0)))
    out_specs = [pl.BlockSpec((tm, w), lambda i: (i, 0)) for w, _ in row_outs]
    out_specs += [pl.BlockSpec(s, lambda i: (0, 0)) for s in acc_outs]
    out_shape = [jax.ShapeDtypeStruct((T, w), dt) for w, dt in row_outs]
    out_shape += [jax.ShapeDtypeStruct(s, F32) for s in acc_outs]
    return pl.pallas_call(
        body, name=name, grid=(T // tm,), in_specs=in_specs, out_specs=out_specs, out_shape=out_shape,
        compiler_params=_cparams(("arbitrary",)),
    )(*[n[0] for n in norm], *const_ins)


FFN_COL_TILE = LANE
FFN_ROW_CHUNK = 512


def _shift_down(ext, s, rows):
    return pltpu.roll(ext, s, 0)[SUBLANE:SUBLANE + rows]


def _shift_up(ext, s, rows):
    return pltpu.roll(ext, rows + SUBLANE - s, 0)[:rows]


def _ffn_chunks(T):
    r = min(FFN_ROW_CHUNK, T)
    return r, T // r


def _ext_before(ref, c, r):
    if c == 0:
        return jnp.concatenate([jnp.zeros((SUBLANE, ref.shape[1]), F32), ref[0:r, :]], axis=0)
    return ref[c * r - SUBLANE:(c + 1) * r, :]


def _ext_after(ref, c, r, nch):
    if c == nch - 1:
        return jnp.concatenate([ref[c * r:(c + 1) * r, :], jnp.zeros((SUBLANE, ref.shape[1]), F32)], axis=0)
    return ref[c * r:(c + 1) * r + SUBLANE, :]


def _ffn_mid_fwd(au, conv_w, conv_b, *, name):
    T = au.shape[0]
    tc = FFN_COL_TILE
    ncol = D_FF // tc
    r, nch = _ffn_chunks(T)

    def body(a_ref, u_ref, w_ref, b_ref, z_ref):
        w0, w1, w2, bias = w_ref[0:1, :], w_ref[1:2, :], w_ref[2:3, :], b_ref[...]
        for c in range(nch):
            ext = _ext_before(a_ref, c, r)
            pre = w0 * _shift_down(ext, 2, r) + w1 * _shift_down(ext, 1, r) + w2 * ext[SUBLANE:] + bias
            z_ref[c * r:(c + 1) * r, :] = (_silu(pre) * u_ref[c * r:(c + 1) * r, :]).astype(z_ref.dtype)

    return pl.pallas_call(
        body, name=name, grid=(ncol,),
        in_specs=[pl.BlockSpec((T, tc), lambda j: (0, j)), pl.BlockSpec((T, tc), lambda j: (0, j + ncol)),
                  pl.BlockSpec((3, tc), lambda j: (0, j)), pl.BlockSpec((1, tc), lambda j: (0, j))],
        out_specs=pl.BlockSpec((T, tc), lambda j: (0, j)),
        out_shape=jax.ShapeDtypeStruct((T, D_FF), BF16),
        compiler_params=_cparams(("parallel",)),
    )(au, au, conv_w, conv_b)


def _ffn_mid_bwd(au, dz, conv_w, conv_b, *, name):
    T = au.shape[0]
    tc = FFN_COL_TILE
    ncol = D_FF // tc
    r, nch = _ffn_chunks(T)

    def body(a_ref, u_ref, dz_ref, w_ref, b_ref, da_ref, du_ref, dw_ref, db_ref, dpre_ref):
        w0, w1, w2, bias = w_ref[0:1, :], w_ref[1:2, :], w_ref[2:3, :], b_ref[...]
        dw0 = jnp.zeros((1, tc), F32)
        dw1 = jnp.zeros((1, tc), F32)
        dw2 = jnp.zeros((1, tc), F32)
        db = jnp.zeros((1, tc), F32)
        for c in range(nch):
            rows = slice(c * r, (c + 1) * r)
            ext = _ext_before(a_ref, c, r)
            a2, a1, a0 = _shift_down(ext, 2, r), _shift_down(ext, 1, r), ext[SUBLANE:]
            pre = w0 * a2 + w1 * a1 + w2 * a0 + bias
            sg = _sigmoid(pre)
            act = pre * sg
            dzc = dz_ref[rows, :]
            du_ref[rows, :] = (dzc * act).astype(du_ref.dtype)
            dpre = dzc * u_ref[rows, :] * (sg * (1.0 + pre * (1.0 - sg)))
            dpre_ref[rows, :] = dpre
            dw0 += jnp.sum(dpre * a2, axis=0, keepdims=True)
            dw1 += jnp.sum(dpre * a1, axis=0, keepdims=True)
            dw2 += jnp.sum(dpre * a0, axis=0, keepdims=True)
            db += jnp.sum(dpre, axis=0, keepdims=True)
        for c in range(nch):
            ext = _ext_after(dpre_ref, c, r, nch)
            da = w0 * _shift_up(ext, 2, r) + w1 * _shift_up(ext, 1, r) + w2 * ext[:r]
            da_ref[c * r:(c + 1) * r, :] = da.astype(da_ref.dtype)
        dw_ref[0:1, :] = dw0
        dw_ref[1:2, :] = dw1
        dw_ref[2:3, :] = dw2
        db_ref[...] = db

    col = lambda j: (0, j)
    return pl.pallas_call(
        body, name=name, grid=(ncol,),
        in_specs=[pl.BlockSpec((T, tc), col), pl.BlockSpec((T, tc), lambda j: (0, j + ncol)), pl.BlockSpec((T, tc), col),
                  pl.BlockSpec((3, tc), col), pl.BlockSpec((1, tc), col)],
        out_specs=[pl.BlockSpec((T, tc), col), pl.BlockSpec((T, tc), col), pl.BlockSpec((3, tc), col), pl.BlockSpec((1, tc), col)],
        out_shape=[jax.ShapeDtypeStruct((T, D_FF), BF16), jax.ShapeDtypeStruct((T, D_FF), BF16),
                   jax.ShapeDtypeStruct((3, D_FF), F32), jax.ShapeDtypeStruct((1, D_FF), F32)],
        scratch_shapes=[pltpu.VMEM((T, tc), F32)],
        compiler_params=_cparams(("parallel",)),
    )(au, au, dz, conv_w, conv_b)


HGRN_BLOCK = 512


def _hgrn_chunk(dot, cumsum, q, f, i, g, lb, ng, st):
    C = q.shape[0]
    forget = lb + (1.0 - lb) * _sigmoid(f)
    k = 1.0 - forget
    b = cumsum(jnp.log(forget))
    b_last = b[C - 1:C, :]
    qd = q * jnp.exp(b)
    kd = k * jnp.exp(-b)
    att = jnp.where(_tri(C), dot(qd, kd, 1, 1), 0.0)
    o = dot(att, i, 1, 0) + dot(qd, st, 1, 1)
    st_new = st * jnp.exp(b_last) + dot(i, k * jnp.exp(b_last - b), 0, 0)
    on = o * lax.rsqrt(jnp.mean(o * o, axis=-1, keepdims=True) + EPS) * ng
    return on * _silu(g), st_new


def _hgrn_specs(T, rev):
    tb = min(HGRN_BLOCK, T)
    nb = T // tb
    blk = (lambda n: nb - 1 - n) if rev else (lambda n: n)
    hd = HGRN_HEAD_DIM
    proj_specs = [pl.BlockSpec((tb, hd), lambda h, n, k=k: (blk(n), h + HGRN_HEADS * k)) for k in range(4)]
    vec_spec = pl.BlockSpec((1, hd), lambda h, n: (0, h))
    tok_spec = pl.BlockSpec((tb, hd), lambda h, n: (blk(n), h))
    st_spec = pl.BlockSpec((1, tb // HGRN_CHUNK, hd, hd), lambda h, n: (h, blk(n), 0, 0))
    return tb, nb, proj_specs, vec_spec, tok_spec, st_spec


def _hgrn_fwd(proj, lb, ng, *, name):
    T = proj.shape[0]
    tb, nb, proj_specs, vec_spec, tok_spec, st_spec = _hgrn_specs(T, False)
    nsub = tb // HGRN_CHUNK
    hd = HGRN_HEAD_DIM

    def body(q_ref, f_ref, i_ref, g_ref, lb_ref, ng_ref, y_ref, sts_ref, st_ref):
        @pl.when(pl.program_id(1) == 0)
        def _():
            st_ref[...] = jnp.zeros_like(st_ref)

        st = st_ref[...]
        for s in range(nsub):
            rows = slice(s * HGRN_CHUNK, (s + 1) * HGRN_CHUNK)
            sts_ref[0, s] = st
            y, st = _hgrn_chunk(_mxu, _cumsum_rows, q_ref[rows, :], f_ref[rows, :], i_ref[rows, :], g_ref[rows, :],
                                lb_ref[...], ng_ref[...], st)
            y_ref[rows, :] = y
        st_ref[...] = st

    return pl.pallas_call(
        body, name=name, grid=(HGRN_HEADS, nb),
        in_specs=proj_specs + [vec_spec, vec_spec], out_specs=[tok_spec, st_spec],
        out_shape=[jax.ShapeDtypeStruct((T, HGRN_DIM), F32),
                   jax.ShapeDtypeStruct((HGRN_HEADS, T // HGRN_CHUNK, hd, hd), F32)],
        scratch_shapes=[pltpu.VMEM((hd, hd), F32)],
        compiler_params=_cparams(("arbitrary", "arbitrary")),
    )(proj, proj, proj, proj, lb, ng)


def _hgrn_bwd(proj, lb, ng, states, dmix, *, name):
    T = proj.shape[0]
    tb, nb, proj_specs, vec_spec, tok_spec, st_spec = _hgrn_specs(T, True)
    nsub = tb // HGRN_CHUNK
    hd = HGRN_HEAD_DIM
    chunk = functools.partial(_hgrn_chunk, _mxu_ad, _cumsum_rows_ad)

    def body(q_ref, f_ref, i_ref, g_ref, lb_ref, ng_ref, sts_ref, dy_ref,
             dq_ref, df_ref, di_ref, dg_ref, dlb_ref, dng_ref, dst_ref):
        @pl.when(pl.program_id(1) == 0)
        def _():
            dst_ref[...] = jnp.zeros_like(dst_ref)
            dlb_ref[...] = jnp.zeros_like(dlb_ref)
            dng_ref[...] = jnp.zeros_like(dng_ref)

        dst = dst_ref[...]
        dlb = jnp.zeros((1, hd), F32)
        dng = jnp.zeros((1, hd), F32)
        for s in reversed(range(nsub)):
            rows = slice(s * HGRN_CHUNK, (s + 1) * HGRN_CHUNK)
            _, vjp = jax.vjp(chunk, q_ref[rows, :], f_ref[rows, :], i_ref[rows, :], g_ref[rows, :],
                             lb_ref[...], ng_ref[...], sts_ref[0, s])
            dq, df, di, dg, dlb_s, dng_s, dst = vjp((dy_ref[rows, :], dst))
            dq_ref[rows, :] = dq
            df_ref[rows, :] = df
            di_ref[rows, :] = di
            dg_ref[rows, :] = dg
            dlb += dlb_s
            dng += dng_s
        dst_ref[...] = dst
        dlb_ref[...] += dlb
        dng_ref[...] += dng

    tok_out = jax.ShapeDtypeStruct((T, HGRN_DIM), F32)
    vec_out = jax.ShapeDtypeStruct((1, HGRN_DIM), F32)
    return pl.pallas_call(
        body, name=name, grid=(HGRN_HEADS, nb),
        in_specs=proj_specs + [vec_spec, vec_spec, st_spec, tok_spec],
        out_specs=[tok_spec] * 4 + [vec_spec, vec_spec],
        out_shape=[tok_out] * 4 + [vec_out, vec_out],
        scratch_shapes=[pltpu.VMEM((hd, hd), F32)],
        compiler_params=_cparams(("arbitrary", "arbitrary")),
    )(proj, proj, proj, proj, lb, ng, states, dmix)


S5_LANES = 512
S5_ROWS = 512


def _cmul(ar, ai, br, bi):
    return ar * br - ai * bi, ar * bi + ai * br


def _power_table(ar, ai, exps):
    a2 = _cmul(ar, ai, ar, ai)
    a4 = _cmul(*a2, *a2)
    e = exps - 1
    pr = jnp.broadcast_to(ar, exps.shape)
    pi = jnp.broadcast_to(ai, exps.shape)
    for bit, (fr, fi) in enumerate(((ar, ai), a2, a4)):
        nr, ni = _cmul(pr, pi, fr, fi)
        on = ((e >> bit) & 1) == 1
        pr, pi = jnp.where(on, nr, pr), jnp.where(on, ni, pi)
    return pr, pi, a2, a4


def _s5_scan_fwd(bu, a_re, a_im, *, name):
    T = bu.shape[0]
    w, tr = S5_LANES, min(S5_ROWS, T)
    ncol, nt = S5_WIDTH // w, T // tr

    def body(br_ref, bi_ref, ar_ref, ai_ref, sr_ref, si_ref, carry_ref):
        @pl.when(pl.program_id(1) == 0)
        def _():
            carry_ref[...] = jnp.zeros_like(carry_ref)

        ar, ai = ar_ref[...], ai_ref[...]
        rowi = lax.broadcasted_iota(jnp.int32, (SUBLANE, w), 0)
        pr, pi, a2, a4 = _power_table(ar, ai, rowi + 1)

        def tile(i, carry):
            cr, ci = carry
            rows = pl.ds(pl.multiple_of(i * SUBLANE, SUBLANE), SUBLANE)
            xr, xi = br_ref[rows, :], bi_ref[rows, :]
            for s, (fr, fi) in ((1, (ar, ai)), (2, a2), (4, a4)):
                keep = rowi >= s
                zr = jnp.where(keep, pltpu.roll(xr, s, 0), 0.0)
                zi = jnp.where(keep, pltpu.roll(xi, s, 0), 0.0)
                xr, xi = xr + fr * zr - fi * zi, xi + fr * zi + fi * zr
            xr, xi = xr + pr * cr - pi * ci, xi + pr * ci + pi * cr
            sr_ref[rows, :] = xr
            si_ref[rows, :] = xi
            return xr[SUBLANE - 1:SUBLANE, :], xi[SUBLANE - 1:SUBLANE, :]

        cr, ci = lax.fori_loop(0, tr // SUBLANE, tile, (carry_ref[0:1, :], carry_ref[1:2, :]))
        carry_ref[0:1, :] = cr
        carry_ref[1:2, :] = ci

    out = jax.ShapeDtypeStruct((T, S5_WIDTH), F32)
    return pl.pallas_call(
        body, name=name, grid=(ncol, nt),
        in_specs=[pl.BlockSpec((tr, w), lambda j, t: (t, j)), pl.BlockSpec((tr, w), lambda j, t: (t, j + ncol)),
                  pl.BlockSpec((1, w), lambda j, t: (0, j)), pl.BlockSpec((1, w), lambda j, t: (0, j))],
        out_specs=[pl.BlockSpec((tr, w), lambda j, t: (t, j))] * 2,
        out_shape=[out, out],
        scratch_shapes=[pltpu.VMEM((2, w), F32)],
        compiler_params=_cparams(("parallel", "arbitrary")),
    )(bu, bu, a_re, a_im)


def _s5_scan_bwd(g_re, g_im, s_re, s_im, a_re, a_im, *, name):
    T = g_re.shape[0]
    w, tr = S5_LANES, min(S5_ROWS, T)
    ncol, nt = S5_WIDTH // w, T // tr
    ntile = tr // SUBLANE

    def body(gr_ref, gi_ref, sr_ref, si_ref, ar_ref, ai_ref, lr_ref, li_ref, dar_ref, dai_ref, carry_ref):
        @pl.when(pl.program_id(1) == 0)
        def _():
            carry_ref[...] = jnp.zeros_like(carry_ref)
            dar_ref[...] = jnp.zeros_like(dar_ref)
            dai_ref[...] = jnp.zeros_like(dai_ref)

        ar, ai = ar_ref[...], -ai_ref[...]
        rowi = lax.broadcasted_iota(jnp.int32, (SUBLANE, w), 0)
        pr, pi, a2, a4 = _power_table(ar, ai, SUBLANE - rowi)
        last = rowi == SUBLANE - 1

        def tile(i, carry):
            cr, ci, dar, dai = carry
            rows = pl.ds(pl.multiple_of((ntile - 1 - i) * SUBLANE, SUBLANE), SUBLANE)
            xr, xi = gr_ref[rows, :], gi_ref[rows, :]
            for s, (fr, fi) in ((1, (ar, ai)), (2, a2), (4, a4)):
                keep = rowi < SUBLANE - s
                zr = jnp.where(keep, pltpu.roll(xr, SUBLANE - s, 0), 0.0)
                zi = jnp.where(keep, pltpu.roll(xi, SUBLANE - s, 0), 0.0)
                xr, xi = xr + fr * zr - fi * zi, xi + fr * zi + fi * zr
            xr, xi = xr + pr * cr - pi * ci, xi + pr * ci + pi * cr
            lr_ref[rows, :] = xr
            li_ref[rows, :] = xi
            nr = jnp.where(last, cr, pltpu.roll(xr, SUBLANE - 1, 0))
            ni = jnp.where(last, ci, pltpu.roll(xi, SUBLANE - 1, 0))
            sr, si = sr_ref[rows, :], si_ref[rows, :]
            return xr[0:1, :], xi[0:1, :], dar + nr * sr + ni * si, dai + ni * sr - nr * si

        cr, ci, dar, dai = lax.fori_loop(
            0, ntile, tile, (carry_ref[0:1, :], carry_ref[1:2, :], jnp.zeros((SUBLANE, w), F32), jnp.zeros((SUBLANE, w), F32)))
        carry_ref[0:1, :] = cr
        carry_ref[1:2, :] = ci
        dar_ref[...] += dar
        dai_ref[...] += dai

    tok = pl.BlockSpec((tr, w), lambda j, t: (nt - 1 - t, j))
    vec = pl.BlockSpec((1, w), lambda j, t: (0, j))
    acc = pl.BlockSpec((SUBLANE, w), lambda j, t: (0, j))
    out = jax.ShapeDtypeStruct((T, S5_WIDTH), F32)
    accs = jax.ShapeDtypeStruct((SUBLANE, S5_WIDTH), F32)
    return pl.pallas_call(
        body, name=name, grid=(ncol, nt),
        in_specs=[tok, tok, tok, tok, vec, vec], out_specs=[tok, tok, acc, acc],
        out_shape=[out, out, accs, accs],
        scratch_shapes=[pltpu.VMEM((2, w), F32)],
        compiler_params=_cparams(("parallel", "arbitrary")),
    )(g_re, g_im, s_re, s_im, a_re, a_im)


ATTN_BLOCK = 512
_NEG = -1e30


def _attn_scores(qn, qr, kn, kr, q0, k0):
    s = (_mxu(qn, kn, 1, 1) + _mxu(qr, kr, 1, 1)) * (MLA_QK ** -0.5)
    qpos = q0 + lax.broadcasted_iota(jnp.int32, s.shape, 0)
    kpos = k0 + lax.broadcasted_iota(jnp.int32, s.shape, 1)
    return s, kpos <= qpos


def _attn_fwd(q_all, q_rope, kv, k_rope, *, name):
    T = q_all.shape[0]
    tq = min(ATTN_BLOCK, T)
    nq = T // tq

    def body(qn_ref, qr_ref, kn_ref, v_ref, kr_ref, o_ref, lse_ref):
        i = pl.program_id(1)
        qn, qr = qn_ref[...], qr_ref[0]

        def step(j, carry):
            m, l, acc = carry
            ks = pl.ds(pl.multiple_of(j * tq, tq), tq)
            s, ok = _attn_scores(qn, qr, kn_ref[ks, :], kr_ref[ks, :], i * tq, j * tq)
            s = jnp.where(ok, s, _NEG)
            m_new = jnp.maximum(m, jnp.max(s, axis=-1, keepdims=True))
            p = jnp.exp(s - m_new)
            alpha = jnp.exp(m - m_new)
            return m_new, alpha * l + jnp.sum(p, axis=-1, keepdims=True), alpha * acc + _mxu(p, v_ref[ks, :], 1, 0)

        m, l, acc = lax.fori_loop(0, i + 1, step, (jnp.full((tq, 1), _NEG, F32), jnp.zeros((tq, 1), F32), jnp.zeros((tq, MLA_V), F32)))
        o_ref[...] = acc / l
        lse_ref[0] = m + jnp.log(l)

    return pl.pallas_call(
        body, name=name, grid=(MLA_HEADS, nq),
        in_specs=[pl.BlockSpec((tq, MLA_NOPE), lambda h, i: (i, h)), pl.BlockSpec((1, tq, MLA_ROPE), lambda h, i: (h, i, 0)),
                  pl.BlockSpec((T, MLA_NOPE), lambda h, i: (0, 2 * h)), pl.BlockSpec((T, MLA_V), lambda h, i: (0, 2 * h + 1)),
                  pl.BlockSpec((T, MLA_ROPE), lambda h, i: (0, 0))],
        out_specs=[pl.BlockSpec((tq, MLA_V), lambda h, i: (i, h)), pl.BlockSpec((1, tq, 1), lambda h, i: (h, i, 0))],
        out_shape=[jax.ShapeDtypeStruct((T, MLA_HEADS * MLA_V), F32), jax.ShapeDtypeStruct((MLA_HEADS, T, 1), F32)],
        compiler_params=_cparams(("arbitrary", "arbitrary")),
    )(q_all, q_rope, kv, kv, k_rope)


def _attn_bwd(q_all, q_rope, kv, k_rope, o, lse, do, *, name):
    T = q_all.shape[0]
    tk = min(ATTN_BLOCK, T)
    nk = T // tk
    scale = MLA_QK ** -0.5

    def body(qn_ref, qr_ref, kv_ref, kr_ref, o_ref, lse_ref, do_ref, dqn_ref, dqr_ref, dkv_ref, dkr_ref):
        h, j = pl.program_id(0), pl.program_id(1)

        @pl.when(j == 0)
        def _():
            dqn_ref[...] = jnp.zeros_like(dqn_ref)
            dqr_ref[...] = jnp.zeros_like(dqr_ref)

        @pl.when((j == 0) & (h == 0))
        def _():
            dkr_ref[...] = jnp.zeros_like(dkr_ref)

        krows = pl.ds(pl.multiple_of(j * tk, tk), tk)
        kn, v, kr = kv_ref[:, :MLA_NOPE], kv_ref[:, MLA_NOPE:], kr_ref[krows, :]

        def step(i, carry):
            dkn, dv, dkr = carry
            qs = pl.ds(pl.multiple_of(i * tk, tk), tk)
            qn, qr, dob = qn_ref[qs, :], qr_ref[0, qs, :], do_ref[qs, :]
            s, ok = _attn_scores(qn, qr, kn, kr, i * tk, j * tk)
            p = jnp.where(ok, jnp.exp(s - lse_ref[0, qs, :]), 0.0)
            dp = _mxu(dob, v, 1, 1)
            delta = jnp.sum(dob * o_ref[qs, :], axis=-1, keepdims=True)
            ds = p * (dp - delta) * scale
            dqn_ref[qs, :] += _mxu(ds, kn, 1, 0)
            dqr_ref[0, qs, :] += _mxu(ds, kr, 1, 0)
            return dkn + _mxu(ds, qn, 0, 0), dv + _mxu(p, dob, 0, 0), dkr + _mxu(ds, qr, 0, 0)

        dkn, dv, dkr = lax.fori_loop(j, nk, step, (jnp.zeros((tk, MLA_NOPE), F32), jnp.zeros((tk, MLA_V), F32), jnp.zeros((tk, MLA_ROPE), F32)))
        dkv_ref[:, :MLA_NOPE] = dkn
        dkv_ref[:, MLA_NOPE:] = dv
        dkr_ref[krows, :] += dkr

    head_cols = pl.BlockSpec((T, MLA_NOPE), lambda h, j: (0, h))
    head_rope = pl.BlockSpec((1, T, MLA_ROPE), lambda h, j: (h, 0, 0))
    kv_spec = pl.BlockSpec((tk, MLA_NOPE + MLA_V), lambda h, j: (j, h))
    kr_spec = pl.BlockSpec((T, MLA_ROPE), lambda h, j: (0, 0))
    return pl.pallas_call(
        body, name=name, grid=(MLA_HEADS, nk),
        in_specs=[head_cols, head_rope, kv_spec, kr_spec, head_cols, pl.BlockSpec((1, T, 1), lambda h, j: (h, 0, 0)), head_cols],
        out_specs=[head_cols, head_rope, kv_spec, kr_spec],
        out_shape=[jax.ShapeDtypeStruct((T, MLA_HEADS * MLA_NOPE), F32), jax.ShapeDtypeStruct((MLA_HEADS, T, MLA_ROPE), F32),
                   jax.ShapeDtypeStruct((T, MLA_HEADS * (MLA_NOPE + MLA_V)), F32), jax.ShapeDtypeStruct((T, MLA_ROPE), F32)],
        compiler_params=_cparams(("arbitrary", "arbitrary")),
    )(q_all, q_rope, kv, k_rope, o, lse, do)


def _s5_discretize(a_re, a_im, log_dt, bt_re, bt_im, lb_logits):
    dt = jnp.exp(log_dt)
    mag = jnp.exp(a_re * dt)
    abr, abi = mag * jnp.cos(a_im * dt), mag * jnp.sin(a_im * dt)
    den = a_re * a_re + a_im * a_im
    xr, xi = abr - 1.0, abi
    cr = ((xr * a_re + xi * a_im) / den)[:, None, :]
    ci = ((xi * a_re - xr * a_im) / den)[:, None, :]
    e = jnp.exp(lb_logits - jnp.max(lb_logits, axis=0, keepdims=True))
    lb = e[0:1, :] / jnp.sum(e, axis=0, keepdims=True)
    return abr, abi, cr * bt_re - ci * bt_im, cr * bt_im + ci * bt_re, lb


def _whole(shape):
    return pl.BlockSpec(shape, lambda: (0,) * len(shape))


def _s5_params_fwd(a_re, a_im, log_dt, bt_re, bt_im, lb_logits):
    ins = (a_re, a_im, log_dt, bt_re, bt_im, lb_logits)
    outs = [jax.ShapeDtypeStruct(s, F32) for s in (a_re.shape, a_re.shape, bt_re.shape, bt_re.shape, (1, lb_logits.shape[1]))]

    def body(*refs):
        res = _s5_discretize(*[r[...] for r in refs[:6]])
        for o_ref, o in zip(refs[6:], res):
            o_ref[...] = o

    return pl.pallas_call(body, name="s5_params_fwd", in_specs=[_whole(a.shape) for a in ins],
                          out_specs=[_whole(o.shape) for o in outs], out_shape=outs, compiler_params=_cparams())(*ins)


def _s5_params_bwd(a_re, a_im, log_dt, bt_re, bt_im, lb_logits, d_abr, d_abi, d_bbr, d_bbi, d_lb):
    ins = (a_re, a_im, log_dt, bt_re, bt_im, lb_logits, d_abr, d_abi, d_bbr, d_bbi, d_lb)
    outs = [jax.ShapeDtypeStruct(a.shape, F32) for a in ins[:6]]

    def body(*refs):
        _, vjp = jax.vjp(_s5_discretize, *[r[...] for r in refs[:6]])
        for o_ref, o in zip(refs[11:], vjp(tuple(r[...] for r in refs[6:11]))):
            o_ref[...] = o

    return pl.pallas_call(body, name="s5_params_bwd", in_specs=[_whole(a.shape) for a in ins],
                          out_specs=[_whole(o.shape) for o in outs], out_shape=outs, compiler_params=_cparams())(*ins)


def _adamw(w, g, m, v, *, name):
    R, C = w.shape
    tr = _pick(R, (256, 128, 64, 32, 16, 8)) if R % SUBLANE == 0 else R
    slabs = g.shape[0] if g.ndim == 3 else 0

    def body(w_ref, g_ref, m_ref, v_ref, *outs):
        if slabs:
            gv = g_ref[0].astype(F32)
            for s in range(1, slabs):
                gv = gv + g_ref[s].astype(F32)
            outs[0][...] = gv
            outs = outs[1:]
        else:
            gv = g_ref[...]
        d_ref, mo_ref, vo_ref = outs
        m2 = ADAM_B1 * m_ref[...] + (1.0 - ADAM_B1) * gv
        v2 = ADAM_B2 * v_ref[...] + (1.0 - ADAM_B2) * (gv * gv)
        m_hat = m2 / (1.0 - ADAM_B1 ** ADAM_STEP)
        v_hat = v2 / (1.0 - ADAM_B2 ** ADAM_STEP)
        d_ref[...] = -ADAM_LR * (m_hat / (jnp.sqrt(v_hat) + ADAM_EPS) + ADAM_WD * w_ref[...])
        mo_ref[...] = m2
        vo_ref[...] = v2

    spec = pl.BlockSpec((tr, C), lambda i: (i, 0))
    g_spec = pl.BlockSpec((slabs, tr, C), lambda i: (0, i, 0)) if slabs else spec
    out = jax.ShapeDtypeStruct((R, C), F32)
    n_out = 4 if slabs else 3
    return pl.pallas_call(body, name=name, grid=(R // tr,), in_specs=[spec, g_spec, spec, spec], out_specs=[spec] * n_out,
                          out_shape=[out] * n_out, compiler_params=_cparams(("parallel",)))(w, g, m, v)


N_CHIPS = 4
N_CORES = 2


def _exchange(srcs, *, group, scatter, name):
    chips = group == "chips"
    n_arr = len(srcs)
    if chips:
        out_shapes = [s.shape if scatter else (N_CHIPS,) + s.shape for s in srcs]
        per_arr = N_CHIPS - 1
    else:
        out_shapes = [s.shape if scatter else s.shape[:1] + (N_CORES,) + s.shape[1:] for s in srcs]
        per_arr = N_CHIPS

    def body(*refs):
        src_refs, out_refs = refs[:n_arr], refs[n_arr:2 * n_arr]
        send_sems, recv_sems, local_sems = refs[2 * n_arr:]
        x, y, c = lax.axis_index("x"), lax.axis_index("y"), lax.axis_index("c")
        me_chip = 2 * x + y
        sends, recvs, locals_ = [], [], []
        for a, (s_ref, o_ref) in enumerate(zip(src_refs, out_refs)):
            if chips:
                peers = [((1 - x, y, c), 2 * (1 - x) + y), ((x, 1 - y, c), 2 * x + 1 - y), ((1 - x, 1 - y, c), 2 * (1 - x) + 1 - y)]
                plan = [(dev, s_ref.at[p] if scatter else s_ref, o_ref.at[me_chip], o_ref.at[p]) for dev, p in peers]
                own = [(s_ref.at[me_chip] if scatter else s_ref, o_ref.at[me_chip])]
            else:
                sib = (x, y, 1 - c)
                plan = [(sib, s_ref.at[k, 1 - c] if scatter else s_ref.at[k], o_ref.at[k, c], o_ref.at[k, 1 - c]) for k in range(N_CHIPS)]
                own = [(s_ref.at[k, c] if scatter else s_ref.at[k], o_ref.at[k, c]) for k in range(N_CHIPS)]
            for j, (dev, src, dst_there, dst_here) in enumerate(plan):
                k = a * per_arr + j
                sends.append(pltpu.make_async_remote_copy(src_ref=src, dst_ref=dst_there, send_sem=send_sems.at[k],
                                                          recv_sem=recv_sems.at[k], device_id=dev, device_id_type=MESH))
                recvs.append(pltpu.make_async_remote_copy(src_ref=src, dst_ref=dst_here, send_sem=send_sems.at[k],
                                                          recv_sem=recv_sems.at[k], device_id=dev, device_id_type=MESH))
            for j, (src, dst) in enumerate(own):
                locals_.append(pltpu.make_async_copy(src, dst, local_sems.at[a * len(own) + j]))
        for cp in sends + locals_:
            cp.start()
        for cp in recvs:
            cp.wait_recv()
        for cp in sends:
            cp.wait_send()
        for cp in locals_:
            cp.wait()

    n_remote = n_arr * per_arr
    n_local = n_arr * (1 if chips else N_CHIPS)
    any_spec = pl.BlockSpec(memory_space=pl.ANY)
    return pl.pallas_call(
        body, name=name, in_specs=[any_spec] * n_arr, out_specs=[any_spec] * n_arr,
        out_shape=[jax.ShapeDtypeStruct(shp, s.dtype) for shp, s in zip(out_shapes, srcs)],
        scratch_shapes=[pltpu.SemaphoreType.DMA((n_remote,)), pltpu.SemaphoreType.DMA((n_remote,)), pltpu.SemaphoreType.DMA((n_local,))],
        compiler_params=pltpu.CompilerParams(has_side_effects=True),
    )(*srcs)


CORE_STAGE_BYTES = 7 * 1024 * 1024


def _exchange_core_staged(srcs, *, scatter, name):
    n_arr = len(srcs)
    out_shapes = [s.shape if scatter else s.shape[:1] + (N_CORES,) + s.shape[1:] for s in srcs]
    slab_shapes = [s.shape[2:] if scatter else s.shape[1:] for s in srcs]
    n_unit = n_arr * N_CHIPS

    def body(*refs):
        src_refs, out_refs = refs[:n_arr], refs[n_arr:2 * n_arr]
        sbufs, rbufs = refs[2 * n_arr:2 * n_arr + n_unit], refs[2 * n_arr + n_unit:2 * n_arr + 2 * n_unit]
        send_sems, recv_sems, in_sems, out_sems, own_sems = refs[2 * n_arr + 2 * n_unit:]
        x, y, c = lax.axis_index("x"), lax.axis_index("y"), lax.axis_index("c")
        sib = (x, y, 1 - c)
        loads, pushes, drains, owns = [], [], [], []
        for a, (s_ref, o_ref) in enumerate(zip(src_refs, out_refs)):
            for k in range(N_CHIPS):
                u = a * N_CHIPS + k
                loads.append(pltpu.make_async_copy(s_ref.at[k, 1 - c] if scatter else s_ref.at[k], sbufs[u], in_sems.at[u]))
                pushes.append(pltpu.make_async_remote_copy(src_ref=sbufs[u], dst_ref=rbufs[u], send_sem=send_sems.at[u],
                                                           recv_sem=recv_sems.at[u], device_id=sib, device_id_type=MESH))
                drains.append(pltpu.make_async_copy(rbufs[u], o_ref.at[k, 1 - c], out_sems.at[u]))
                owns.append(pltpu.make_async_copy(s_ref.at[k, c] if scatter else s_ref.at[k], o_ref.at[k, c], own_sems.at[u]))
        for cp in loads + owns:
            cp.start()
        for ld, push in zip(loads, pushes):
            ld.wait()
            push.start()
        for push, drain in zip(pushes, drains):
            push.wait_recv()
            drain.start()
        for push, drain in zip(pushes, drains):
            push.wait_send()
            drain.wait()
        for cp in owns:
            cp.wait()

    any_spec = pl.BlockSpec(memory_space=pl.ANY)
    stage = [pltpu.VMEM(shp, s.dtype) for shp, s in zip(slab_shapes, srcs) for _ in range(N_CHIPS)]
    return pl.pallas_call(
        body, name=name, in_specs=[any_spec] * n_arr, out_specs=[any_spec] * n_arr,
        out_shape=[jax.ShapeDtypeStruct(shp, s.dtype) for shp, s in zip(out_shapes, srcs)],
        scratch_shapes=stage + stage + [pltpu.SemaphoreType.DMA((n_unit,)) for _ in range(5)],
        compiler_params=pltpu.CompilerParams(has_side_effects=True, vmem_limit_bytes=VMEM_LIMIT_BYTES),
    )(*srcs)


def _exchange_core(srcs, *, scatter, name):
    groups, cur, size = [], [], 0
    for i, s in enumerate(srcs):
        nbytes = s.size * s.dtype.itemsize // (N_CORES if scatter else 1)
        if cur and size + nbytes > CORE_STAGE_BYTES:
            groups.append(cur)
            cur, size = [], 0
        cur.append(i)
        size += nbytes
    groups.append(cur)
    outs = [None] * len(srcs)
    for gi, idx in enumerate(groups):
        for i, o in zip(idx, _exchange_core_staged([srcs[i] for i in idx], scatter=scatter, name=f"{name}_{gi}")):
            outs[i] = o
    return outs


def _sum_pairs(buf, *, name):
    _, _, R, C = buf.shape
    tr = _pick(R, (512, 256, 128, 64, 32, 16, 8)) if R % SUBLANE == 0 else R

    def body(b_ref, o_ref):
        o_ref[0] = (b_ref[0, 0].astype(F32) + b_ref[0, 1].astype(F32)).astype(o_ref.dtype)

    return pl.pallas_call(body, name=name, grid=(N_CHIPS, R // tr),
                          in_specs=[pl.BlockSpec((1, N_CORES, tr, C), lambda k, i: (k, 0, i, 0))],
                          out_specs=pl.BlockSpec((1, tr, C), lambda k, i: (k, i, 0)),
                          out_shape=jax.ShapeDtypeStruct((N_CHIPS, R, C), buf.dtype),
                          compiler_params=_cparams(("parallel", "parallel")))(buf)


def _sum_slabs(buf, out_dtype, *, name):
    G, R, C = buf.shape
    tr = _pick(R, (512, 256, 128, 64, 32, 16, 8))

    def body(b_ref, o_ref):
        acc = b_ref[0].astype(F32)
        for s in range(1, G):
            acc = acc + b_ref[s].astype(F32)
        o_ref[...] = acc.astype(out_dtype)

    return pl.pallas_call(body, name=name, grid=(R // tr,), in_specs=[pl.BlockSpec((G, tr, C), lambda i: (0, i, 0))],
                          out_specs=pl.BlockSpec((tr, C), lambda i: (i, 0)), out_shape=jax.ShapeDtypeStruct((R, C), out_dtype),
                          compiler_params=_cparams(("parallel",)))(buf)


def _rms_fwd_fn(h, g):
    return (_rms(h, g),)


def _rms_bwd_fn(h, dhn, dres, g):
    _, vjp = jax.vjp(_rms, h, g)
    dh, dg = vjp(dhn)
    return dh + dres, dg


def _loss_fn(h, tgt, g):
    y, vjp = jax.vjp(_rms, h, g)
    diff = y - tgt
    dh, dg = vjp(diff * (1.0 / D_MODEL))
    return dh, dg, (0.5 / D_MODEL) * jnp.sum(diff * diff, axis=0, keepdims=True)


def _s5_act(ys, u, d):
    return _gelu(ys + d * u)


def _s5_gate(z, gl, b):
    return z * _sigmoid(gl + b)


def _s5_act_fn(ys, u, d):
    return (_s5_act(ys, u, d),)


def _s5_mix_fn(ya, z, gl, b):
    return (jnp.concatenate([ya, _s5_gate(z, gl, b)], axis=1),)


def _s5_gate_bwd_fn(z, gl, dyb, b):
    _, vjp = jax.vjp(_s5_gate, z, gl, b)
    return vjp(dyb)


def _s5_act_bwd_fn(ys, u, dz1, dz2, d):
    _, vjp = jax.vjp(_s5_act, ys, u, d)
    return vjp(dz1 + dz2)


def _dproj_fn(dq, df, di, dg, du1, du2):
    return (jnp.concatenate([dq, df, di, dg, du1 + du2], axis=1),)


def _rope_pair(r1, r2, pos, freqs):
    ang = pos.astype(F32) * freqs
    c, s = jnp.cos(ang), jnp.sin(ang)
    return r1 * c - r2 * s, r1 * s + r2 * c


_ODD_SPLITS = (0, MLA_Q_RANK, MLA_Q_RANK + MLA_KV_RANK, MLA_Q_RANK + MLA_KV_RANK + LANE, ODD_IN_PAD)


def _mla_prep(cq, ckv, k1, k2, qg, kvg, pos, freqs):
    ko1, ko2 = _rope_pair(k1, k2, pos, freqs)
    return _rms(cq, qg), _rms(ckv, kvg), ko1, ko2


def _mla_prep_fn(proj, pos, qg, kvg, freqs):
    parts = [proj[:, a:b] for a, b in zip(_ODD_SPLITS[:-1], _ODD_SPLITS[1:])]
    return _mla_prep(*parts, qg, kvg, pos, freqs)


def _mla_prep_bwd_fn(proj, pos, dqn, dkvn, dko1, dko2, qg, kvg, freqs):
    parts = [proj[:, a:b] for a, b in zip(_ODD_SPLITS[:-1], _ODD_SPLITS[1:])]
    _, vjp = jax.vjp(lambda *a: _mla_prep(*a, pos, freqs), *parts, qg, kvg)
    dcq, dckv, dk1, dk2, dqg, dkvg = vjp((dqn, dkvn, dko1, dko2))
    return jnp.concatenate([dcq, dckv, dk1, dk2], axis=1), dqg, dkvg


def _rope_q_fn(r1, r2, pos, freqs):
    return _rope_pair(r1, r2, pos, freqs)


def _rope_q_bwd_fn(dqn, do1, do2, pos, freqs):
    dr1, dr2 = _rope_pair(do1, do2, pos, -freqs)
    return (jnp.concatenate([dqn, dr1, dr2], axis=1),)


W_NAMES = ("norm_mix_g", "norm_ffn_g", "final_norm_g", "even_w_in", "hgrn_lb_logits", "hgrn_norm_g", "s5_a_re", "s5_a_im",
           "s5_log_dt", "s5_b_re", "s5_b_im", "s5_c_re", "s5_c_im", "s5_d", "s5_w_glu", "s5_b_glu", "even_w_out", "odd_w_in",
           "mla_q_norm_g", "mla_w_uq", "mla_kv_norm_g", "mla_w_ukv", "odd_w_out", "ffn_w_in", "ffn_conv_w", "ffn_conv_b",
           "ffn_w_out")
BIG_UNITS = (("even_w_in", 0, "col"), ("s5_w_glu", 0, "row"), ("even_w_out", 0, "row"), ("odd_w_in", 0, "row"),
             ("mla_w_uq", 0, "col"), ("mla_w_ukv", 0, "col"), ("odd_w_out", 0, "row"),
             ("ffn_w_in", 0, "col"), ("ffn_w_in", 1, "col"), ("ffn_w_out", 0, "row"), ("ffn_w_out", 1, "row"))
BIG_NAMES = tuple(dict.fromkeys(u[0] for u in BIG_UNITS))
SMALL_SHARDED = (("mla_q_norm_g", 1), ("mla_kv_norm_g", 1), ("ffn_conv_w", 2))
SMALL_SHARDED_NAMES = tuple(s[0] for s in SMALL_SHARDED)
REPLICATED = tuple(n for n in W_NAMES if n not in BIG_NAMES + SMALL_SHARDED_NAMES)


def _pack(flats, cols, row_mult):
    flat = jnp.concatenate(flats, axis=-1)
    pad = (-flat.shape[-1]) % (cols * row_mult)
    flat = jnp.pad(flat, [(0, 0)] * (flat.ndim - 1) + [(0, pad)])
    return flat.reshape(flat.shape[:-1] + (-1, cols))


def _unpack(flat, shapes):
    out, off = [], 0
    for shp in shapes:
        n = int(np.prod(shp))
        out.append(flat[..., off:off + n].reshape(flat.shape[:-1] + tuple(shp)))
        off += n
    return out


def _gather_weights(w):
    srcs = [w[n][l].astype(BF16) for n, l, _ in BIG_UNITS]
    srcs.append(_pack([w[n].reshape(-1) for n in SMALL_SHARDED_NAMES], LANE, SUBLANE))
    g1 = _exchange(srcs, group="chips", scatter=False, name="gather_w_chips")
    g2 = _exchange_core(g1, scatter=False, name="gather_w_core")
    big = {}
    for (n, l, kind), g in zip(BIG_UNITS, g2):
        r, c = g.shape[2:]
        big[(n, l)] = g.reshape(N_DEV * r, c) if kind == "row" else g.transpose(2, 0, 1, 3).reshape(r, N_DEV * c)
    parts = _unpack(g2[-1].reshape(N_DEV, -1), [w[n].shape for n in SMALL_SHARDED_NAMES])
    small = {}
    for (n, ax), p in zip(SMALL_SHARDED, parts):
        shp = list(w[n].shape)
        shp[ax] *= N_DEV
        small[n] = jnp.moveaxis(p, 0, ax).reshape(shp)
    return big, small


def _reduce_scatter_grads(g_big, g_small, w):
    srcs = []
    for n, l, kind in BIG_UNITS:
        g = g_big[(n, l)].astype(BF16)
        if kind == "row":
            srcs.append(g.reshape(N_CHIPS, N_CORES, g.shape[0] // N_DEV, g.shape[1]))
        else:
            srcs.append(g.reshape(g.shape[0], N_CHIPS, N_CORES, g.shape[1] // N_DEV).transpose(1, 2, 0, 3))
    flats = []
    for n, ax in SMALL_SHARDED:
        shp = list(w[n].shape)
        g = g_small[n].astype(F32).reshape(shp[:ax] + [N_DEV] + shp[ax:])
        flats.append(jnp.moveaxis(g, ax, 0).reshape(N_DEV, -1))
    small = _pack(flats, LANE, SUBLANE)
    srcs.append(small.reshape((N_CHIPS, N_CORES) + small.shape[1:]))
    r1 = _exchange_core(srcs, scatter=True, name="scatter_g_core")
    s1 = [_sum_pairs(r, name=f"sum_g_core_{i}") for i, r in enumerate(r1)]
    return _exchange(s1, group="chips", scatter=True, name="scatter_g_chips")


def _all_reduce_small(grads, w):
    vec = _pack([grads[n].reshape(-1).astype(F32) for n in REPLICATED], LANE, SUBLANE)
    a1 = _exchange([vec], group="chips", scatter=False, name="gather_small_chips")
    a2, = _exchange_core(a1, scatter=False, name="gather_small_core")
    tot = _sum_slabs(a2.reshape((N_DEV,) + vec.shape), F32, name="sum_small").reshape(-1)
    return dict(zip(REPLICATED, _unpack(tot, [w[n].shape for n in REPLICATED])))


def _block_diag(blocks):
    G, a, b = blocks.shape
    return jnp.einsum('gab,gk->gakb', blocks, jnp.eye(G, dtype=blocks.dtype)).reshape(G * a, G * b)


def _diag_blocks(mat, a, b):
    G = mat.shape[0] // a
    return jnp.einsum('gagb->gab', mat.reshape(G, a, G, b))


def _ffn_fwd(h, g, w_in, conv_w, conv_b, w_out, tag):
    hn, = _rows(_rms_fwd_fn, [h], [g], [(D_MODEL, BF16)], [], name=f"ffn{tag}_norm")
    au = _mm(hn, w_in, name=f"ffn{tag}_in")
    z = _ffn_mid_fwd(au, conv_w, conv_b, name=f"ffn{tag}_mid")
    return _mm(z, w_out, res=h, name=f"ffn{tag}_out"), (hn, au, z)


def _ffn_bwd(h, dh, saved, g, w_in, conv_w, conv_b, w_out, tag):
    hn, au, z = saved
    dz = _mm(dh, w_out, tb=True, name=f"ffn{tag}_dz")
    dw_out = _mm(z, dh, ta=True, out_dtype=BF16, name=f"ffn{tag}_dwout")
    da, du, dcw, dcb = _ffn_mid_bwd(au, dz, conv_w, conv_b, name=f"ffn{tag}_dmid")
    dhn = _mm(da, w_in, tb=True, b_cols=(0, D_FF), name=f"ffn{tag}_dhn_a")
    dhn = _mm(du, w_in, tb=True, b_cols=(D_FF, D_FF), res=dhn, name=f"ffn{tag}_dhn_u")
    dw_in = jnp.concatenate([_mm(hn, da, ta=True, out_dtype=BF16, name=f"ffn{tag}_dwin_a"),
                             _mm(hn, du, ta=True, out_dtype=BF16, name=f"ffn{tag}_dwin_u")], axis=1)
    dh_in, dg = _rows(_rms_bwd_fn, [h, dhn, dh], [g], [(D_MODEL, F32)], [(1, D_MODEL)], name=f"ffn{tag}_dnorm")
    return dh_in, dict(g=dg, w_in=dw_in, conv_w=dcw, conv_b=dcb, w_out=dw_out)


def kernel(x, positions, norm_mix_g, norm_ffn_g, final_norm_g, even_w_in, hgrn_lb_logits, hgrn_norm_g, s5_a_re, s5_a_im, s5_log_dt, s5_b_re, s5_b_im, s5_c_re, s5_c_im, s5_d, s5_w_glu, s5_b_glu, even_w_out, odd_w_in, mla_q_norm_g, mla_w_uq, mla_kv_norm_g, mla_w_ukv, odd_w_out, ffn_w_in, ffn_conv_w, ffn_conv_b, ffn_w_out, loss_target, m_norm_mix_g, m_norm_ffn_g, m_final_norm_g, m_even_w_in, m_hgrn_lb_logits, m_hgrn_norm_g, m_s5_a_re, m_s5_a_im, m_s5_log_dt, m_s5_b_re, m_s5_b_im, m_s5_c_re, m_s5_c_im, m_s5_d, m_s5_w_glu, m_s5_b_glu, m_even_w_out, m_odd_w_in, m_mla_q_norm_g, m_mla_w_uq, m_mla_kv_norm_g, m_mla_w_ukv, m_odd_w_out, m_ffn_w_in, m_ffn_conv_w, m_ffn_conv_b, m_ffn_w_out, v_norm_mix_g, v_norm_ffn_g, v_final_norm_g, v_even_w_in, v_hgrn_lb_logits, v_hgrn_norm_g, v_s5_a_re, v_s5_a_im, v_s5_log_dt, v_s5_b_re, v_s5_b_im, v_s5_c_re, v_s5_c_im, v_s5_d, v_s5_w_glu, v_s5_b_glu, v_even_w_out, v_odd_w_in, v_mla_q_norm_g, v_mla_w_uq, v_mla_kv_norm_g, v_mla_w_ukv, v_odd_w_out, v_ffn_w_in, v_ffn_conv_w, v_ffn_conv_b, v_ffn_w_out):
    given = dict(locals())
    w = {n: given[n] for n in W_NAMES}
    mom = {n: given["m_" + n] for n in W_NAMES}
    var = {n: given["v_" + n] for n in W_NAMES}
    T = x.shape[1]
    h0 = x[0]
    tgt = loss_target[0]
    pos = positions.reshape(T, 1)

    full, full_small = _gather_weights(w)
    w_ein = full["even_w_in", 0]
    w_glu = full["s5_w_glu", 0]
    w_eout = full["even_w_out", 0]
    w_oin = full["odd_w_in", 0]
    zpad = jnp.zeros((D_MODEL, LANE - MLA_ROPE // 2), BF16)
    kr0 = MLA_Q_RANK + MLA_KV_RANK
    w_oin_pad = jnp.concatenate([w_oin[:, :kr0], w_oin[:, kr0:kr0 + MLA_ROPE // 2], zpad, w_oin[:, kr0 + MLA_ROPE // 2:], zpad], axis=1)
    qg, kvg = full_small["mla_q_norm_g"], full_small["mla_kv_norm_g"]
    w_uq3 = full["mla_w_uq", 0].reshape(MLA_Q_RANK, MLA_HEADS, MLA_QK)
    half = MLA_ROPE // 2
    w_uq_perm = jnp.concatenate([w_uq3[:, :, :MLA_NOPE].reshape(MLA_Q_RANK, -1),
                                 w_uq3[:, :, MLA_NOPE:MLA_NOPE + half].reshape(MLA_Q_RANK, -1),
                                 w_uq3[:, :, MLA_NOPE + half:].reshape(MLA_Q_RANK, -1)], axis=1)
    w_ukv = full["mla_w_ukv", 0]
    w_oout = full["odd_w_out", 0]
    w_fin = [full["ffn_w_in", 0], full["ffn_w_in", 1]]
    w_fout = [full["ffn_w_out", 0], full["ffn_w_out", 1]]
    conv_w = full_small["ffn_conv_w"]
    freqs = ROPE_THETA ** (-jnp.arange(0, MLA_ROPE, 2, dtype=F32) / MLA_ROPE)
    freqs_q = jnp.tile(freqs, MLA_HEADS)[None, :]
    freqs_k = jnp.concatenate([freqs, jnp.zeros((LANE - half,), F32)])[None, :]

    sp_in = (s5_a_re[0], s5_a_im[0], s5_log_dt[0][:, None], s5_b_re[0].transpose(0, 2, 1), s5_b_im[0].transpose(0, 2, 1),
             hgrn_lb_logits)
    abr, abi, bbt_re, bbt_im, lb0 = _s5_params_fwd(*sp_in)
    a_re, a_im = abr.reshape(1, S5_WIDTH), abi.reshape(1, S5_WIDTH)
    bb_re, bb_im = _block_diag(bbt_re).astype(BF16), _block_diag(bbt_im).astype(BF16)
    bb_cat = jnp.concatenate([bb_re, bb_im], axis=1)
    c_re = _block_diag(s5_c_re[0].transpose(0, 2, 1)).astype(BF16)
    c_im_neg = _block_diag(-s5_c_im[0].transpose(0, 2, 1)).astype(BF16)
    u_cols = (4 * HGRN_DIM, S5_DIM)

    hn0, = _rows(_rms_fwd_fn, [h0], [norm_mix_g[0:1]], [(D_MODEL, BF16)], [], name="mix0_norm")
    proj = _mm(hn0, w_ein, name="even_in")
    y_a, states = _hgrn_fwd(proj, lb0, hgrn_norm_g, name="hgrn_fwd")
    bu = _mm(proj, bb_cat, a_cols=u_cols, name="s5_bu")
    s_re, s_im = _s5_scan_fwd(bu, a_re, a_im, name="s5_scan_fwd")
    ys = _mm(s_im, c_im_neg, res=_mm(s_re, c_re, name="s5_y_re"), name="s5_y_im")
    z5, = _rows(_s5_act_fn, [ys, (proj,) + u_cols], [s5_d], [(S5_DIM, F32)], [], name="s5_act")
    gl = _mm(z5, w_glu, name="s5_glu")
    mixin, = _rows(_s5_mix_fn, [y_a, z5, gl], [s5_b_glu], [(D_MODEL, BF16)], [], name="s5_mix")
    h1 = _mm(mixin, w_eout, res=h0, name="even_out")
    h2, ffn0_saved = _ffn_fwd(h1, norm_ffn_g[0:1], w_fin[0], conv_w[0], ffn_conv_b[0:1], w_fout[0], 0)

    hn1, = _rows(_rms_fwd_fn, [h2], [norm_mix_g[1:2]], [(D_MODEL, BF16)], [], name="mix1_norm")
    proj_o = _mm(hn1, w_oin_pad, name="odd_in")
    qn, kvn, ko1, ko2 = _rows(_mla_prep_fn, [proj_o, pos], [qg, kvg, freqs_k],
                              [(MLA_Q_RANK, BF16), (MLA_KV_RANK, BF16), (LANE, F32), (LANE, F32)], [], name="mla_prep")
    q_all = _mm(qn, w_uq_perm, name="mla_uq")
    kv = _mm(kvn, w_ukv, name="mla_ukv")
    nope_w = MLA_HEADS * MLA_NOPE
    rope_w = MLA_HEADS * half
    o1, o2 = _rows(_rope_q_fn, [(q_all, nope_w, rope_w), (q_all, nope_w + rope_w, rope_w), pos], [freqs_q],
                   [(rope_w, F32), (rope_w, F32)], [], name="mla_rope_q")
    q_rope = jnp.concatenate([o1.reshape(T, MLA_HEADS, half), o2.reshape(T, MLA_HEADS, half)], axis=2).transpose(1, 0, 2)
    k_rope = jnp.concatenate([ko1[:, :half], ko2[:, :half]], axis=1)
    o, lse = _attn_fwd(q_all, q_rope, kv, k_rope, name="attn_fwd")
    h3 = _mm(o, w_oout, res=h2, name="odd_out")
    h4, ffn1_saved = _ffn_fwd(h3, norm_ffn_g[1:2], w_fin[1], conv_w[1], ffn_conv_b[1:2], w_fout[1], 1)

    dh4, d_final_g, loss_cols = _rows(_loss_fn, [h4, tgt], [final_norm_g[None, :]], [(D_MODEL, F32)],
                                      [(1, D_MODEL), (1, D_MODEL)], name="loss_head")
    loss = lax.psum(jnp.sum(loss_cols), ("x", "y", "c"))

    dh3, gf1 = _ffn_bwd(h3, dh4, ffn1_saved, norm_ffn_g[1:2], w_fin[1], conv_w[1], ffn_conv_b[1:2], w_fout[1], 1)
    do = _mm(dh3, w_oout, tb=True, name="odd_out_dx")
    d_w_oout = _mm(o, dh3, ta=True, out_dtype=BF16, name="odd_out_dw")
    dq_nope, dq_rope, dkv, dk_rope = _attn_bwd(q_all, q_rope, kv, k_rope, o, lse, do, name="attn_bwd")
    dq_rope_t = dq_rope.transpose(1, 0, 2)
    do1, do2 = dq_rope_t[:, :, :half].reshape(T, rope_w), dq_rope_t[:, :, half:].reshape(T, rope_w)
    lane_pad = ((0, 0), (0, LANE - half))
    dko1, dko2 = jnp.pad(dk_rope[:, :half], lane_pad), jnp.pad(dk_rope[:, half:], lane_pad)
    dq_all, = _rows(_rope_q_bwd_fn, [dq_nope, do1, do2, pos], [freqs_q], [(MLA_HEADS * MLA_QK, BF16)], [], name="mla_rope_q_bwd")
    d_w_uq_perm = _mm(qn, dq_all, ta=True, out_dtype=BF16, name="mla_uq_dw")
    dqn = _mm(dq_all, w_uq_perm, tb=True, name="mla_uq_dx")
    d_w_ukv = _mm(kvn, dkv, ta=True, out_dtype=BF16, name="mla_ukv_dw")
    dkvn = _mm(dkv, w_ukv, tb=True, name="mla_ukv_dx")
    dproj_o, d_qg, d_kvg = _rows(_mla_prep_bwd_fn, [proj_o, pos, dqn, dkvn, dko1, dko2], [qg, kvg, freqs_k],
                                 [(ODD_IN_PAD, BF16)], [(1, MLA_Q_RANK), (1, MLA_KV_RANK)], name="mla_prep_bwd")
    d_w_oin_pad = _mm(hn1, dproj_o, ta=True, out_dtype=BF16, name="odd_in_dw")
    dhn1 = _mm(dproj_o, w_oin_pad, tb=True, name="odd_in_dx")
    dh2, d_mix_g1 = _rows(_rms_bwd_fn, [h2, dhn1, dh3], [norm_mix_g[1:2]], [(D_MODEL, F32)], [(1, D_MODEL)], name="mix1_dnorm")
    d_w_oin = jnp.concatenate([d_w_oin_pad[:, :kr0 + half], d_w_oin_pad[:, kr0 + LANE:kr0 + LANE + half]], axis=1)
    d3 = d_w_uq_perm
    d_w_uq = jnp.concatenate([d3[:, :nope_w].reshape(MLA_Q_RANK, MLA_HEADS, MLA_NOPE),
                              d3[:, nope_w:nope_w + rope_w].reshape(MLA_Q_RANK, MLA_HEADS, half),
                              d3[:, nope_w + rope_w:].reshape(MLA_Q_RANK, MLA_HEADS, half)], axis=2).reshape(MLA_Q_RANK, -1)

    dh1, gf0 = _ffn_bwd(h1, dh2, ffn0_saved, norm_ffn_g[0:1], w_fin[0], conv_w[0], ffn_conv_b[0:1], w_fout[0], 0)
    dmix = _mm(dh1, w_eout, tb=True, name="even_out_dx")
    d_w_eout = _mm(mixin, dh1, ta=True, out_dtype=BF16, name="even_out_dw")
    dq, df, di, dg, d_lb0, d_hgrn_g = _hgrn_bwd(proj, lb0, hgrn_norm_g, states, dmix, name="hgrn_bwd")
    dz1, dgl, d_b_glu = _rows(_s5_gate_bwd_fn, [z5, gl, (dmix, HGRN_DIM, S5_DIM)], [s5_b_glu],
                              [(S5_DIM, F32), (S5_DIM, BF16)], [(1, S5_DIM)], name="s5_gate_bwd")
    dz2 = _mm(dgl, w_glu, tb=True, name="s5_glu_dx")
    d_w_glu = _mm(z5, dgl, ta=True, out_dtype=BF16, name="s5_glu_dw")
    dys, du1, d_s5_d = _rows(_s5_act_bwd_fn, [ys, (proj,) + u_cols, dz1, dz2], [s5_d],
                             [(S5_DIM, BF16), (S5_DIM, F32)], [(1, S5_DIM)], name="s5_act_bwd")
    ds_re = _mm(dys, c_re, tb=True, name="s5_ds_re")
    ds_im = _mm(dys, c_im_neg, tb=True, name="s5_ds_im")
    d_c_re = _mm(s_re, dys, ta=True, name="s5_dc_re")
    d_c_im_neg = _mm(s_im, dys, ta=True, name="s5_dc_im")
    lam_re, lam_im, d_ar, d_ai = _s5_scan_bwd(ds_re, ds_im, s_re, s_im, a_re, a_im, name="s5_scan_bwd")
    du2 = _mm(lam_im, bb_im, tb=True, res=_mm(lam_re, bb_re, tb=True, name="s5_du_re"), name="s5_du_im")
    d_bb_re = _mm(proj, lam_re, ta=True, a_cols=u_cols, name="s5_dbb_re")
    d_bb_im = _mm(proj, lam_im, ta=True, a_cols=u_cols, name="s5_dbb_im")
    dproj, = _rows(_dproj_fn, [dq, df, di, dg, du1, du2], [], [(EVEN_IN, BF16)], [], name="even_dproj")
    d_w_ein = _mm(hn0, dproj, ta=True, out_dtype=BF16, name="even_in_dw")
    dhn0 = _mm(dproj, w_ein, tb=True, name="even_in_dx")
    grad_x, d_mix_g0 = _rows(_rms_bwd_fn, [h0, dhn0, dh1], [norm_mix_g[0:1]], [(D_MODEL, F32)], [(1, D_MODEL)], name="mix0_dnorm")
    sp_g = _s5_params_bwd(*sp_in, d_ar.sum(0).reshape(S5_GROUPS, S5_STATE), d_ai.sum(0).reshape(S5_GROUPS, S5_STATE),
                          _diag_blocks(d_bb_re, S5_GROUP, S5_STATE), _diag_blocks(d_bb_im, S5_GROUP, S5_STATE), d_lb0)
    d_a_re, d_a_im, d_log_dt, d_bt_re, d_bt_im, d_lb_logits = sp_g

    g_big = {("even_w_in", 0): d_w_ein, ("s5_w_glu", 0): d_w_glu, ("even_w_out", 0): d_w_eout, ("odd_w_in", 0): d_w_oin,
             ("mla_w_uq", 0): d_w_uq, ("mla_w_ukv", 0): d_w_ukv, ("odd_w_out", 0): d_w_oout,
             ("ffn_w_in", 0): gf0["w_in"], ("ffn_w_in", 1): gf1["w_in"], ("ffn_w_out", 0): gf0["w_out"], ("ffn_w_out", 1): gf1["w_out"]}
    g_small = dict(mla_q_norm_g=d_qg, mla_kv_norm_g=d_kvg, ffn_conv_w=jnp.stack([gf0["conv_w"], gf1["conv_w"]]))
    g_repl = dict(
        norm_mix_g=jnp.concatenate([d_mix_g0, d_mix_g1]), norm_ffn_g=jnp.concatenate([gf0["g"], gf1["g"]]),
        final_norm_g=d_final_g[0], hgrn_lb_logits=d_lb_logits, hgrn_norm_g=d_hgrn_g,
        s5_a_re=d_a_re[None], s5_a_im=d_a_im[None], s5_log_dt=d_log_dt[:, 0][None],
        s5_b_re=d_bt_re.transpose(0, 2, 1)[None], s5_b_im=d_bt_im.transpose(0, 2, 1)[None],
        s5_c_re=_diag_blocks(d_c_re, S5_STATE, S5_GROUP).transpose(0, 2, 1)[None],
        s5_c_im=-_diag_blocks(d_c_im_neg, S5_STATE, S5_GROUP).transpose(0, 2, 1)[None],
        s5_d=d_s5_d, s5_b_glu=d_b_glu, ffn_conv_b=jnp.concatenate([gf0["conv_b"], gf1["conv_b"]]))
    partial = _reduce_scatter_grads(g_big, g_small, w)
    grads = _all_reduce_small(g_repl, w)
    small_sum = _sum_slabs(partial[-1], F32, name="sum_g_small").reshape(-1)
    grads.update(zip(SMALL_SHARDED_NAMES, _unpack(small_sum, [w[n].shape for n in SMALL_SHARDED_NAMES])))

    delta, new_m, new_v = {}, {}, {}
    per_unit = {}
    for (n, l, _), slabs in zip(BIG_UNITS, partial):
        per_unit[n, l] = _adamw(w[n][l], slabs, mom[n][l], var[n][l], name=f"adamw_{n}_{l}")
    for n in BIG_NAMES:
        layers = [per_unit[n, l] for l in range(w[n].shape[0])]
        grads[n], delta[n], new_m[n], new_v[n] = (jnp.stack([lay[k] for lay in layers]) for k in range(4))
    small = [n for n in W_NAMES if n not in BIG_NAMES]
    packed = [_pack([t[n].reshape(-1) for n in small], LANE, SUBLANE) for t in (w, grads, mom, var)]
    outs = _adamw(*packed, name="adamw_small")
    small_shapes = [w[n].shape for n in small]
    for dst, o_ in zip((delta, new_m, new_v), outs):
        dst.update(zip(small, _unpack(o_.reshape(-1), small_shapes)))

    return (loss, grad_x[None], *[grads[n] for n in W_NAMES], *[delta[n] for n in W_NAMES],
            *[new_m[n] for n in W_NAMES], *[new_v[n] for n in W_NAMES])
```

```python
import functools
import math

import numpy as np
import jax
import jax.numpy as jnp
from jax import lax
from jax.experimental import pallas as pl
from jax.experimental.pallas import tpu as pltpu

F32 = jnp.float32
BF16 = jnp.bfloat16
_MXU_DTYPE = jnp.bfloat16
_HI = lax.Precision.HIGHEST

D_MODEL = 1024
HGRN_DIM = 512
HGRN_HEAD_DIM = 128
HGRN_HEADS = 4
HGRN_CHUNK = 64
S5_DIM = 512
S5_GROUPS = 32
S5_GROUP = 16
S5_STATE = 64
S5_WIDTH = S5_GROUPS * S5_STATE
EVEN_IN = 4 * HGRN_DIM + S5_DIM
MLA_HEADS = 8
MLA_Q_RANK = 384
MLA_KV_RANK = 256
MLA_NOPE = 128
MLA_ROPE = 64
MLA_V = 128
MLA_QK = MLA_NOPE + MLA_ROPE
ODD_IN = MLA_Q_RANK + MLA_KV_RANK + MLA_ROPE
ODD_IN_PAD = MLA_Q_RANK + MLA_KV_RANK + 2 * 128
ROPE_THETA = 10000.0
D_FF = 2816
EPS = 1e-6
ADAM_LR = 0.001
ADAM_B1 = 0.9
ADAM_B2 = 0.999
ADAM_EPS = 1e-08
ADAM_WD = 0.01
ADAM_STEP = 10

N_DEV = 8
LANE = 128
SUBLANE = 8
VMEM_LIMIT_BYTES = 56 * 1024 * 1024
MESH = pl.DeviceIdType.MESH


def _cparams(sem=None):
    return pltpu.CompilerParams(dimension_semantics=sem, vmem_limit_bytes=VMEM_LIMIT_BYTES)


def _pick(n, cands):
    for c in cands:
        if n % c == 0:
            return c
    raise ValueError(f"no tile for {n} in {cands}")


def _sigmoid(x):
    return 1.0 / (1.0 + jnp.exp(-x))


def _silu(x):
    return x * _sigmoid(x)


def _gelu(x):
    return 0.5 * x * (1.0 + jnp.tanh(math.sqrt(2.0 / math.pi) * (x + 0.044715 * (x * x * x))))


def _rms(x, g):
    return x * lax.rsqrt(jnp.mean(x * x, axis=-1, keepdims=True) + EPS) * g


def _mxu(a, b, ca, cb):
    return lax.dot_general(a.astype(_MXU_DTYPE), b.astype(_MXU_DTYPE), (((ca,), (cb,)), ((), ())),
                           preferred_element_type=F32)


@functools.partial(jax.custom_vjp, nondiff_argnums=(2, 3))
def _mxu_ad(a, b, ca, cb):
    return _mxu(a, b, ca, cb)


def _mxu_ad_fwd(a, b, ca, cb):
    return _mxu(a, b, ca, cb), (a, b)


def _mxu_ad_bwd(ca, cb, saved, g):
    a, b = saved
    fa, fb = 1 - ca, 1 - cb
    da = _mxu(g, b, 1, fb) if ca == 1 else _mxu(b, g, fb, 1)
    db = _mxu(a, g, fa, 0) if cb == 0 else _mxu(g, a, 0, fa)
    return da, db


_mxu_ad.defvjp(_mxu_ad_fwd, _mxu_ad_bwd)


def _tri(n, upper=False):
    row = lax.broadcasted_iota(jnp.int32, (n, n), 0)
    col = lax.broadcasted_iota(jnp.int32, (n, n), 1)
    return (col >= row) if upper else (col <= row)


def _cumsum_rows(x):
    return jnp.dot(_tri(x.shape[0]).astype(F32), x, precision=_HI, preferred_element_type=F32)


@jax.custom_vjp
def _cumsum_rows_ad(x):
    return _cumsum_rows(x)


def _cumsum_rows_ad_fwd(x):
    return _cumsum_rows(x), None


def _cumsum_rows_ad_bwd(_, g):
    return (jnp.dot(_tri(g.shape[0], upper=True).astype(F32), g, precision=_HI, preferred_element_type=F32),)


_cumsum_rows_ad.defvjp(_cumsum_rows_ad_fwd, _cumsum_rows_ad_bwd)


MM_VMEM_BUDGET = 36 * 1024 * 1024
MM_MAX_TILE = 1408


def _lane_divisors(n, cap, offs=()):
    return [d for d in range(min(n, cap) // LANE * LANE, 0, -LANE) if n % d == 0 and all(o % d == 0 for o in offs)]


def _mm_tiles(M, N, K, sa, sb, so, has_res, m_offs, n_offs, k_offs):
    best = None
    for tm in _lane_divisors(M, MM_MAX_TILE, m_offs):
        for tn in _lane_divisors(N, MM_MAX_TILE, n_offs):
            for tk in _lane_divisors(K, K, k_offs):
                nk = K // tk
                vmem = 2 * (tm * tk * sa + tk * tn * sb + tm * tn * so + tm * tn * 4 * has_res) + (tm * tn * 4 if nk > 1 else 0)
                if vmem <= MM_VMEM_BUDGET:
                    key = (-nk, tm * tn, tn)
                    if best is None or key > best[0]:
                        best = (key, tm, tn, tk)
                    break
    return best[1:]


def _mm(a, b, *, ta=False, tb=False, res=None, out_dtype=F32, a_cols=None, b_cols=None, name):
    a_minor = a.shape[1] if a_cols is None else a_cols[1]
    b_minor = b.shape[1] if b_cols is None else b_cols[1]
    K, M = (a.shape[0], a_minor) if ta else (a_minor, a.shape[0])
    N = b.shape[0] if tb else b_minor
    assert (b_minor if tb else b.shape[0]) == K, (a.shape, b.shape, ta, tb)
    a_off = 0 if a_cols is None else a_cols[0]
    b_off = 0 if b_cols is None else b_cols[0]
    has_res = res is not None
    tm, tn, tk = _mm_tiles(M, N, K, a.dtype.itemsize, b.dtype.itemsize, jnp.dtype(out_dtype).itemsize, has_res,
                           (a_off,) if ta else (), () if tb else (b_off,), ((a_off,) if not ta else ()) + ((b_off,) if tb else ()))
    nk = K // tk
    am, ak = (a_off // tm, 0) if ta else (0, a_off // tk)
    bn, bk = (0, b_off // tk) if tb else (b_off // tn, 0)
    a_spec = pl.BlockSpec((tk, tm), lambda i, j, k: (k, i + am)) if ta else pl.BlockSpec((tm, tk), lambda i, j, k: (i, k + ak))
    b_spec = pl.BlockSpec((tn, tk), lambda i, j, k: (j, k + bk)) if tb else pl.BlockSpec((tk, tn), lambda i, j, k: (k, j + bn))
    o_spec = pl.BlockSpec((tm, tn), lambda i, j, k: (i, j))
    ca, cb = (0 if ta else 1), (1 if tb else 0)

    def body(*refs):
        a_ref, b_ref = refs[0], refs[1]
        res_ref = refs[2] if has_res else None
        o_ref = refs[2 + has_res]
        part = _mxu(a_ref[...], b_ref[...], ca, cb)
        if nk == 1:
            o_ref[...] = (part + res_ref[...] if has_res else part).astype(out_dtype)
            return
        acc_ref = refs[3 + has_res]
        k = pl.program_id(2)

        @pl.when(k == 0)
        def _():
            acc_ref[...] = part

        @pl.when(k > 0)
        def _():
            acc_ref[...] += part

        @pl.when(k == nk - 1)
        def _():
            o_ref[...] = (acc_ref[...] + res_ref[...] if has_res else acc_ref[...]).astype(out_dtype)

    ins = [a, b] + ([res] if has_res else [])
    in_specs = [a_spec, b_spec] + ([o_spec] if has_res else [])
    return pl.pallas_call(
        body, name=name, grid=(M // tm, N // tn, nk),
        in_specs=in_specs, out_specs=o_spec,
        out_shape=jax.ShapeDtypeStruct((M, N), out_dtype),
        scratch_shapes=[pltpu.VMEM((tm, tn), F32)] if nk > 1 else [],
        compiler_params=_cparams(("parallel", "parallel", "arbitrary")),
    )(*ins)


def _rows(fn, row_ins, const_ins, row_outs, acc_outs, *, name, tm=256):
    norm = [(r, 0, r.shape[1]) if not isinstance(r, tuple) else r for r in row_ins]
    T = norm[0][0].shape[0]
    tm = min(tm, T)
    nr, nc, no, na = len(norm), len(const_ins), len(row_outs), len(acc_outs)

    def body(*refs):
        i = pl.program_id(0)
        vals = [r[...] for r in refs[:nr + nc]]
        outs = fn(*vals)
        for o_ref, o in zip(refs[nr + nc:nr + nc + no], outs[:no]):
            o_ref[...] = o.astype(o_ref.dtype)
        for a_ref, o in zip(refs[nr + nc + no:], outs[no:]):
            @pl.when(i == 0)
            def _(a_ref=a_ref, o=o):
                a_ref[...] = o

            @pl.when(i > 0)
            def _(a_ref=a_ref, o=o):
                a_ref[...] += o

    in_specs = []
    for arr, off, w in norm:
        assert off % w == 0, (off, w)
        in_specs.append(pl.BlockSpec((tm, w), lambda i, b=off // w: (i, b)))
    for c in const_ins:
        in_specs.append(pl.BlockSpec(c.shape, lambda i: (0, 0)))
    out_specs = [pl.BlockSpec((tm, w), lambda i: (i, 0)) for w, _ in row_outs]
    out_specs += [pl.BlockSpec(s, lambda i: (0, 0)) for s in acc_outs]
    out_shape = [jax.ShapeDtypeStruct((T, w), dt) for w, dt in row_outs]
    out_shape += [jax.ShapeDtypeStruct(s, F32) for s in acc_outs]
    return pl.pallas_call(
        body, name=name, grid=(T // tm,), in_specs=in_specs, out_specs=out_specs, out_shape=out_shape,
        compiler_params=_cparams(("arbitrary",)),
    )(*[n[0] for n in norm], *const_ins)


FFN_COL_TILE = LANE
FFN_ROW_CHUNK = 512


def _shift_down(ext, s, rows):
    return pltpu.roll(ext, s, 0)[SUBLANE:SUBLANE + rows]


def _shift_up(ext, s, rows):
    return pltpu.roll(ext, rows + SUBLANE - s, 0)[:rows]


def _ffn_chunks(T):
    r = min(FFN_ROW_CHUNK, T)
    return r, T // r


def _ext_before(ref, c, r):
    if c == 0:
        return jnp.concatenate([jnp.zeros((SUBLANE, ref.shape[1]), F32), ref[0:r, :]], axis=0)
    return ref[c * r - SUBLANE:(c + 1) * r, :]


def _ext_after(ref, c, r, nch):
    if c == nch - 1:
        return jnp.concatenate([ref[c * r:(c + 1) * r, :], jnp.zeros((SUBLANE, ref.shape[1]), F32)], axis=0)
    return ref[c * r:(c + 1) * r + SUBLANE, :]


def _ffn_mid_fwd(au, conv_w, conv_b, *, name):
    T = au.shape[0]
    tc = FFN_COL_TILE
    ncol = D_FF // tc
    r, nch = _ffn_chunks(T)

    def body(a_ref, u_ref, w_ref, b_ref, z_ref):
        w0, w1, w2, bias = w_ref[0:1, :], w_ref[1:2, :], w_ref[2:3, :], b_ref[...]
        for c in range(nch):
            ext = _ext_before(a_ref, c, r)
            pre = w0 * _shift_down(ext, 2, r) + w1 * _shift_down(ext, 1, r) + w2 * ext[SUBLANE:] + bias
            z_ref[c * r:(c + 1) * r, :] = (_silu(pre) * u_ref[c * r:(c + 1) * r, :]).astype(z_ref.dtype)

    return pl.pallas_call(
        body, name=name, grid=(ncol,),
        in_specs=[pl.BlockSpec((T, tc), lambda j: (0, j)), pl.BlockSpec((T, tc), lambda j: (0, j + ncol)),
                  pl.BlockSpec((3, tc), lambda j: (0, j)), pl.BlockSpec((1, tc), lambda j: (0, j))],
        out_specs=pl.BlockSpec((T, tc), lambda j: (0, j)),
        out_shape=jax.ShapeDtypeStruct((T, D_FF), BF16),
        compiler_params=_cparams(("parallel",)),
    )(au, au, conv_w, conv_b)


def _ffn_mid_bwd(au, dz, conv_w, conv_b, *, name):
    T = au.shape[0]
    tc = FFN_COL_TILE
    ncol = D_FF // tc
    r, nch = _ffn_chunks(T)

    def body(a_ref, u_ref, dz_ref, w_ref, b_ref, da_ref, du_ref, dw_ref, db_ref, dpre_ref):
        w0, w1, w2, bias = w_ref[0:1, :], w_ref[1:2, :], w_ref[2:3, :], b_ref[...]
        dw0 = jnp.zeros((1, tc), F32)
        dw1 = jnp.zeros((1, tc), F32)
        dw2 = jnp.zeros((1, tc), F32)
        db = jnp.zeros((1, tc), F32)
        for c in range(nch):
            rows = slice(c * r, (c + 1) * r)
            ext = _ext_before(a_ref, c, r)
            a2, a1, a0 = _shift_down(ext, 2, r), _shift_down(ext, 1, r), ext[SUBLANE:]
            pre = w0 * a2 + w1 * a1 + w2 * a0 + bias
            sg = _sigmoid(pre)
            act = pre * sg
            dzc = dz_ref[rows, :]
            du_ref[rows, :] = (dzc * act).astype(du_ref.dtype)
            dpre = dzc * u_ref[rows, :] * (sg * (1.0 + pre * (1.0 - sg)))
            dpre_ref[rows, :] = dpre
            dw0 += jnp.sum(dpre * a2, axis=0, keepdims=True)
            dw1 += jnp.sum(dpre * a1, axis=0, keepdims=True)
            dw2 += jnp.sum(dpre * a0, axis=0, keepdims=True)
            db += jnp.sum(dpre, axis=0, keepdims=True)
        for c in range(nch):
            ext = _ext_after(dpre_ref, c, r, nch)
            da = w0 * _shift_up(ext, 2, r) + w1 * _shift_up(ext, 1, r) + w2 * ext[:r]
            da_ref[c * r:(c + 1) * r, :] = da.astype(da_ref.dtype)
        dw_ref[0:1, :] = dw0
        dw_ref[1:2, :] = dw1
        dw_ref[2:3, :] = dw2
        db_ref[...] = db

    col = lambda j: (0, j)
    return pl.pallas_call(
        body, name=name, grid=(ncol,),
        in_specs=[pl.BlockSpec((T, tc), col), pl.BlockSpec((T, tc), lambda j: (0, j + ncol)), pl.BlockSpec((T, tc), col),
                  pl.BlockSpec((3, tc), col), pl.BlockSpec((1, tc), col)],
        out_specs=[pl.BlockSpec((T, tc), col), pl.BlockSpec((T, tc), col), pl.BlockSpec((3, tc), col), pl.BlockSpec((1, tc), col)],
        out_shape=[jax.ShapeDtypeStruct((T, D_FF), BF16), jax.ShapeDtypeStruct((T, D_FF), BF16),
                   jax.ShapeDtypeStruct((3, D_FF), F32), jax.ShapeDtypeStruct((1, D_FF), F32)],
        scratch_shapes=[pltpu.VMEM((T, tc), F32)],
        compiler_params=_cparams(("parallel",)),
    )(au, au, dz, conv_w, conv_b)


HGRN_BLOCK = 512


def _hgrn_chunk(dot, cumsum, q, f, i, g, lb, ng, st):
    C = q.shape[0]
    forget = lb + (1.0 - lb) * _sigmoid(f)
    k = 1.0 - forget
    b = cumsum(jnp.log(forget))
    b_last = b[C - 1:C, :]
    qd = q * jnp.exp(b)
    kd = k * jnp.exp(-b)
    att = jnp.where(_tri(C), dot(qd, kd, 1, 1), 0.0)
    o = dot(att, i, 1, 0) + dot(qd, st, 1, 1)
    st_new = st * jnp.exp(b_last) + dot(i, k * jnp.exp(b_last - b), 0, 0)
    on = o * lax.rsqrt(jnp.mean(o * o, axis=-1, keepdims=True) + EPS) * ng
    return on * _silu(g), st_new


def _hgrn_specs(T, rev):
    tb = min(HGRN_BLOCK, T)
    nb = T // tb
    blk = (lambda n: nb - 1 - n) if rev else (lambda n: n)
    hd = HGRN_HEAD_DIM
    proj_specs = [pl.BlockSpec((tb, hd), lambda h, n, k=k: (blk(n), h + HGRN_HEADS * k)) for k in range(4)]
    vec_spec = pl.BlockSpec((1, hd), lambda h, n: (0, h))
    tok_spec = pl.BlockSpec((tb, hd), lambda h, n: (blk(n), h))
    st_spec = pl.BlockSpec((1, tb // HGRN_CHUNK, hd, hd), lambda h, n: (h, blk(n), 0, 0))
    return tb, nb, proj_specs, vec_spec, tok_spec, st_spec


def _hgrn_fwd(proj, lb, ng, *, name):
    T = proj.shape[0]
    tb, nb, proj_specs, vec_spec, tok_spec, st_spec = _hgrn_specs(T, False)
    nsub = tb // HGRN_CHUNK
    hd = HGRN_HEAD_DIM

    def body(q_ref, f_ref, i_ref, g_ref, lb_ref, ng_ref, y_ref, sts_ref, st_ref):
        @pl.when(pl.program_id(1) == 0)
        def _():
            st_ref[...] = jnp.zeros_like(st_ref)

        st = st_ref[...]
        for s in range(nsub):
            rows = slice(s * HGRN_CHUNK, (s + 1) * HGRN_CHUNK)
            sts_ref[0, s] = st
            y, st = _hgrn_chunk(_mxu, _cumsum_rows, q_ref[rows, :], f_ref[rows, :], i_ref[rows, :], g_ref[rows, :],
                                lb_ref[...], ng_ref[...], st)
            y_ref[rows, :] = y
        st_ref[...] = st

    return pl.pallas_call(
        body, name=name, grid=(HGRN_HEADS, nb),
        in_specs=proj_specs + [vec_spec, vec_spec], out_specs=[tok_spec, st_spec],
        out_shape=[jax.ShapeDtypeStruct((T, HGRN_DIM), F32),
                   jax.ShapeDtypeStruct((HGRN_HEADS, T // HGRN_CHUNK, hd, hd), F32)],
        scratch_shapes=[pltpu.VMEM((hd, hd), F32)],
        compiler_params=_cparams(("arbitrary", "arbitrary")),
    )(proj, proj, proj, proj, lb, ng)


def _hgrn_bwd(proj, lb, ng, states, dmix, *, name):
    T = proj.shape[0]
    tb, nb, proj_specs, vec_spec, tok_spec, st_spec = _hgrn_specs(T, True)
    nsub = tb // HGRN_CHUNK
    hd = HGRN_HEAD_DIM
    chunk = functools.partial(_hgrn_chunk, _mxu_ad, _cumsum_rows_ad)

    def body(q_ref, f_ref, i_ref, g_ref, lb_ref, ng_ref, sts_ref, dy_ref,
             dq_ref, df_ref, di_ref, dg_ref, dlb_ref, dng_ref, dst_ref):
        @pl.when(pl.program_id(1) == 0)
        def _():
            dst_ref[...] = jnp.zeros_like(dst_ref)
            dlb_ref[...] = jnp.zeros_like(dlb_ref)
            dng_ref[...] = jnp.zeros_like(dng_ref)

        dst = dst_ref[...]
        dlb = jnp.zeros((1, hd), F32)
        dng = jnp.zeros((1, hd), F32)
        for s in reversed(range(nsub)):
            rows = slice(s * HGRN_CHUNK, (s + 1) * HGRN_CHUNK)
            _, vjp = jax.vjp(chunk, q_ref[rows, :], f_ref[rows, :], i_ref[rows, :], g_ref[rows, :],
                             lb_ref[...], ng_ref[...], sts_ref[0, s])
            dq, df, di, dg, dlb_s, dng_s, dst = vjp((dy_ref[rows, :], dst))
            dq_ref[rows, :] = dq
            df_ref[rows, :] = df
            di_ref[rows, :] = di
            dg_ref[rows, :] = dg
            dlb += dlb_s
            dng += dng_s
        dst_ref[...] = dst
        dlb_ref[...] += dlb
        dng_ref[...] += dng

    tok_out = jax.ShapeDtypeStruct((T, HGRN_DIM), F32)
    vec_out = jax.ShapeDtypeStruct((1, HGRN_DIM), F32)
    return pl.pallas_call(
        body, name=name, grid=(HGRN_HEADS, nb),
        in_specs=proj_specs + [vec_spec, vec_spec, st_spec, tok_spec],
        out_specs=[tok_spec] * 4 + [vec_spec, vec_spec],
        out_shape=[tok_out] * 4 + [vec_out, vec_out],
        scratch_shapes=[pltpu.VMEM((hd, hd), F32)],
        compiler_params=_cparams(("arbitrary", "arbitrary")),
    )(proj, proj, proj, proj, lb, ng, states, dmix)


S5_LANES = 512
S5_ROWS = 512


def _cmul(ar, ai, br, bi):
    return ar * br - ai * bi, ar * bi + ai * br


def _power_table(ar, ai, exps):
    a2 = _cmul(ar, ai, ar, ai)
    a4 = _cmul(*a2, *a2)
    e = exps - 1
    pr = jnp.broadcast_to(ar, exps.shape)
    pi = jnp.broadcast_to(ai, exps.shape)
    for bit, (fr, fi) in enumerate(((ar, ai), a2, a4)):
        nr, ni = _cmul(pr, pi, fr, fi)
        on = ((e >> bit) & 1) == 1
        pr, pi = jnp.where(on, nr, pr), jnp.where(on, ni, pi)
    return pr, pi, a2, a4


def _s5_scan_fwd(bu, a_re, a_im, *, name):
    T = bu.shape[0]
    w, tr = S5_LANES, min(S5_ROWS, T)
    ncol, nt = S5_WIDTH // w, T // tr

    def body(br_ref, bi_ref, ar_ref, ai_ref, sr_ref, si_ref, carry_ref):
        @pl.when(pl.program_id(1) == 0)
        def _():
            carry_ref[...] = jnp.zeros_like(carry_ref)

        ar, ai = ar_ref[...], ai_ref[...]
        rowi = lax.broadcasted_iota(jnp.int32, (SUBLANE, w), 0)
        pr, pi, a2, a4 = _power_table(ar, ai, rowi + 1)

        def tile(i, carry):
            cr, ci = carry
            rows = pl.ds(pl.multiple_of(i * SUBLANE, SUBLANE), SUBLANE)
            xr, xi = br_ref[rows, :], bi_ref[rows, :]
            for s, (fr, fi) in ((1, (ar, ai)), (2, a2), (4, a4)):
                keep = rowi >= s
                zr = jnp.where(keep, pltpu.roll(xr, s, 0), 0.0)
                zi = jnp.where(keep, pltpu.roll(xi, s, 0), 0.0)
                xr, xi = xr + fr * zr - fi * zi, xi + fr * zi + fi * zr
            xr, xi = xr + pr * cr - pi * ci, xi + pr * ci + pi * cr
            sr_ref[rows, :] = xr
            si_ref[rows, :] = xi
            return xr[SUBLANE - 1:SUBLANE, :], xi[SUBLANE - 1:SUBLANE, :]

        cr, ci = lax.fori_loop(0, tr // SUBLANE, tile, (carry_ref[0:1, :], carry_ref[1:2, :]))
        carry_ref[0:1, :] = cr
        carry_ref[1:2, :] = ci

    out = jax.ShapeDtypeStruct((T, S5_WIDTH), F32)
    return pl.pallas_call(
        body, name=name, grid=(ncol, nt),
        in_specs=[pl.BlockSpec((tr, w), lambda j, t: (t, j)), pl.BlockSpec((tr, w), lambda j, t: (t, j + ncol)),
                  pl.BlockSpec((1, w), lambda j, t: (0, j)), pl.BlockSpec((1, w), lambda j, t: (0, j))],
        out_specs=[pl.BlockSpec((tr, w), lambda j, t: (t, j))] * 2,
        out_shape=[out, out],
        scratch_shapes=[pltpu.VMEM((2, w), F32)],
        compiler_params=_cparams(("parallel", "arbitrary")),
    )(bu, bu, a_re, a_im)


def _s5_scan_bwd(g_re, g_im, s_re, s_im, a_re, a_im, *, name):
    T = g_re.shape[0]
    w, tr = S5_LANES, min(S5_ROWS, T)
    ncol, nt = S5_WIDTH // w, T // tr
    ntile = tr // SUBLANE

    def body(gr_ref, gi_ref, sr_ref, si_ref, ar_ref, ai_ref, lr_ref, li_ref, dar_ref, dai_ref, carry_ref):
        @pl.when(pl.program_id(1) == 0)
        def _():
            carry_ref[...] = jnp.zeros_like(carry_ref)
            dar_ref[...] = jnp.zeros_like(dar_ref)
            dai_ref[...] = jnp.zeros_like(dai_ref)

        ar, ai = ar_ref[...], -ai_ref[...]
        rowi = lax.broadcasted_iota(jnp.int32, (SUBLANE, w), 0)
        pr, pi, a2, a4 = _power_table(ar, ai, SUBLANE - rowi)
        last = rowi == SUBLANE - 1

        def tile(i, carry):
            cr, ci, dar, dai = carry
            rows = pl.ds(pl.multiple_of((ntile - 1 - i) * SUBLANE, SUBLANE), SUBLANE)
            xr, xi = gr_ref[rows, :], gi_ref[rows, :]
            for s, (fr, fi) in ((1, (ar, ai)), (2, a2), (4, a4)):
                keep = rowi < SUBLANE - s
                zr = jnp.where(keep, pltpu.roll(xr, SUBLANE - s, 0), 0.0)
                zi = jnp.where(keep, pltpu.roll(xi, SUBLANE - s, 0), 0.0)
                xr, xi = xr + fr * zr - fi * zi, xi + fr * zi + fi * zr
            xr, xi = xr + pr * cr - pi * ci, xi + pr * ci + pi * cr
            lr_ref[rows, :] = xr
            li_ref[rows, :] = xi
            nr = jnp.where(last, cr, pltpu.roll(xr, SUBLANE - 1, 0))
            ni = jnp.where(last, ci, pltpu.roll(xi, SUBLANE - 1, 0))
            sr, si = sr_ref[rows, :], si_ref[rows, :]
            return xr[0:1, :], xi[0:1, :], dar + nr * sr + ni * si, dai + ni * sr - nr * si

        cr, ci, dar, dai = lax.fori_loop(
            0, ntile, tile, (carry_ref[0:1, :], carry_ref[1:2, :], jnp.zeros((SUBLANE, w), F32), jnp.zeros((SUBLANE, w), F32)))
        carry_ref[0:1, :] = cr
        carry_ref[1:2, :] = ci
        dar_ref[...] += dar
        dai_ref[...] += dai

    tok = pl.BlockSpec((tr, w), lambda j, t: (nt - 1 - t, j))
    vec = pl.BlockSpec((1, w), lambda j, t: (0, j))
    acc = pl.BlockSpec((SUBLANE, w), lambda j, t: (0, j))
    out = jax.ShapeDtypeStruct((T, S5_WIDTH), F32)
    accs = jax.ShapeDtypeStruct((SUBLANE, S5_WIDTH), F32)
    return pl.pallas_call(
        body, name=name, grid=(ncol, nt),
        in_specs=[tok, tok, tok, tok, vec, vec], out_specs=[tok, tok, acc, acc],
        out_shape=[out, out, accs, accs],
        scratch_shapes=[pltpu.VMEM((2, w), F32)],
        compiler_params=_cparams(("parallel", "arbitrary")),
    )(g_re, g_im, s_re, s_im, a_re, a_im)


ATTN_BLOCK = 512
_NEG = -1e30


def _attn_scores(qn, qr, kn, kr, q0, k0):
    s = (_mxu(qn, kn, 1, 1) + _mxu(qr, kr, 1, 1)) * (MLA_QK ** -0.5)
    qpos = q0 + lax.broadcasted_iota(jnp.int32, s.shape, 0)
    kpos = k0 + lax.broadcasted_iota(jnp.int32, s.shape, 1)
    return s, kpos <= qpos


def _attn_fwd(q_all, q_rope, kv, k_rope, *, name):
    T = q_all.shape[0]
    tq = min(ATTN_BLOCK, T)
    nq = T // tq

    def body(qn_ref, qr_ref, kn_ref, v_ref, kr_ref, o_ref, lse_ref):
        i = pl.program_id(1)
        qn, qr = qn_ref[...], qr_ref[0]

        def step(j, carry):
            m, l, acc = carry
            ks = pl.ds(pl.multiple_of(j * tq, tq), tq)
            s, ok = _attn_scores(qn, qr, kn_ref[ks, :], kr_ref[ks, :], i * tq, j * tq)
            s = jnp.where(ok, s, _NEG)
            m_new = jnp.maximum(m, jnp.max(s, axis=-1, keepdims=True))
            p = jnp.exp(s - m_new)
            alpha = jnp.exp(m - m_new)
            return m_new, alpha * l + jnp.sum(p, axis=-1, keepdims=True), alpha * acc + _mxu(p, v_ref[ks, :], 1, 0)

        m, l, acc = lax.fori_loop(0, i + 1, step, (jnp.full((tq, 1), _NEG, F32), jnp.zeros((tq, 1), F32), jnp.zeros((tq, MLA_V), F32)))
        o_ref[...] = acc / l
        lse_ref[0] = m + jnp.log(l)

    return pl.pallas_call(
        body, name=name, grid=(MLA_HEADS, nq),
        in_specs=[pl.BlockSpec((tq, MLA_NOPE), lambda h, i: (i, h)), pl.BlockSpec((1, tq, MLA_ROPE), lambda h, i: (h, i, 0)),
                  pl.BlockSpec((T, MLA_NOPE), lambda h, i: (0, 2 * h)), pl.BlockSpec((T, MLA_V), lambda h, i: (0, 2 * h + 1)),
                  pl.BlockSpec((T, MLA_ROPE), lambda h, i: (0, 0))],
        out_specs=[pl.BlockSpec((tq, MLA_V), lambda h, i: (i, h)), pl.BlockSpec((1, tq, 1), lambda h, i: (h, i, 0))],
        out_shape=[jax.ShapeDtypeStruct((T, MLA_HEADS * MLA_V), F32), jax.ShapeDtypeStruct((MLA_HEADS, T, 1), F32)],
        compiler_params=_cparams(("arbitrary", "arbitrary")),
    )(q_all, q_rope, kv, kv, k_rope)


def _attn_bwd(q_all, q_rope, kv, k_rope, o, lse, do, *, name):
    T = q_all.shape[0]
    tk = min(ATTN_BLOCK, T)
    nk = T // tk
    scale = MLA_QK ** -0.5

    def body(qn_ref, qr_ref, kv_ref, kr_ref, o_ref, lse_ref, do_ref, dqn_ref, dqr_ref, dkv_ref, dkr_ref):
        h, j = pl.program_id(0), pl.program_id(1)

        @pl.when(j == 0)
        def _():
            dqn_ref[...] = jnp.zeros_like(dqn_ref)
            dqr_ref[...] = jnp.zeros_like(dqr_ref)

        @pl.when((j == 0) & (h == 0))
        def _():
            dkr_ref[...] = jnp.zeros_like(dkr_ref)

        krows = pl.ds(pl.multiple_of(j * tk, tk), tk)
        kn, v, kr = kv_ref[:, :MLA_NOPE], kv_ref[:, MLA_NOPE:], kr_ref[krows, :]

        def step(i, carry):
            dkn, dv, dkr = carry
            qs = pl.ds(pl.multiple_of(i * tk, tk), tk)
            qn, qr, dob = qn_ref[qs, :], qr_ref[0, qs, :], do_ref[qs, :]
            s, ok = _attn_scores(qn, qr, kn, kr, i * tk, j * tk)
            p = jnp.where(ok, jnp.exp(s - lse_ref[0, qs, :]), 0.0)
            dp = _mxu(dob, v, 1, 1)
            delta = jnp.sum(dob * o_ref[qs, :], axis=-1, keepdims=True)
            ds = p * (dp - delta) * scale
            dqn_ref[qs, :] += _mxu(ds, kn, 1, 0)
            dqr_ref[0, qs, :] += _mxu(ds, kr, 1, 0)
            return dkn + _mxu(ds, qn, 0, 0), dv + _mxu(p, dob, 0, 0), dkr + _mxu(ds, qr, 0, 0)

        dkn, dv, dkr = lax.fori_loop(j, nk, step, (jnp.zeros((tk, MLA_NOPE), F32), jnp.zeros((tk, MLA_V), F32), jnp.zeros((tk, MLA_ROPE), F32)))
        dkv_ref[:, :MLA_NOPE] = dkn
        dkv_ref[:, MLA_NOPE:] = dv
        dkr_ref[krows, :] += dkr

    head_cols = pl.BlockSpec((T, MLA_NOPE), lambda h, j: (0, h))
    head_rope = pl.BlockSpec((1, T, MLA_ROPE), lambda h, j: (h, 0, 0))
    kv_spec = pl.BlockSpec((tk, MLA_NOPE + MLA_V), lambda h, j: (j, h))
    kr_spec = pl.BlockSpec((T, MLA_ROPE), lambda h, j: (0, 0))
    return pl.pallas_call(
        body, name=name, grid=(MLA_HEADS, nk),
        in_specs=[head_cols, head_rope, kv_spec, kr_spec, head_cols, pl.BlockSpec((1, T, 1), lambda h, j: (h, 0, 0)), head_cols],
        out_specs=[head_cols, head_rope, kv_spec, kr_spec],
        out_shape=[jax.ShapeDtypeStruct((T, MLA_HEADS * MLA_NOPE), F32), jax.ShapeDtypeStruct((MLA_HEADS, T, MLA_ROPE), F32),
                   jax.ShapeDtypeStruct((T, MLA_HEADS * (MLA_NOPE + MLA_V)), F32), jax.ShapeDtypeStruct((T, MLA_ROPE), F32)],
        compiler_params=_cparams(("arbitrary", "arbitrary")),
    )(q_all, q_rope, kv, k_rope, o, lse, do)


def _s5_discretize(a_re, a_im, log_dt, bt_re, bt_im, lb_logits):
    dt = jnp.exp(log_dt)
    mag = jnp.exp(a_re * dt)
    abr, abi = mag * jnp.cos(a_im * dt), mag * jnp.sin(a_im * dt)
    den = a_re * a_re + a_im * a_im
    xr, xi = abr - 1.0, abi
    cr = ((xr * a_re + xi * a_im) / den)[:, None, :]
    ci = ((xi * a_re - xr * a_im) / den)[:, None, :]
    e = jnp.exp(lb_logits - jnp.max(lb_logits, axis=0, keepdims=True))
    lb = e[0:1, :] / jnp.sum(e, axis=0, keepdims=True)
    return abr, abi, cr * bt_re - ci * bt_im, cr * bt_im + ci * bt_re, lb


def _whole(shape):
    return pl.BlockSpec(shape, lambda: (0,) * len(shape))


def _s5_params_fwd(a_re, a_im, log_dt, bt_re, bt_im, lb_logits):
    ins = (a_re, a_im, log_dt, bt_re, bt_im, lb_logits)
    outs = [jax.ShapeDtypeStruct(s, F32) for s in (a_re.shape, a_re.shape, bt_re.shape, bt_re.shape, (1, lb_logits.shape[1]))]

    def body(*refs):
        res = _s5_discretize(*[r[...] for r in refs[:6]])
        for o_ref, o in zip(refs[6:], res):
            o_ref[...] = o

    return pl.pallas_call(body, name="s5_params_fwd", in_specs=[_whole(a.shape) for a in ins],
                          out_specs=[_whole(o.shape) for o in outs], out_shape=outs, compiler_params=_cparams())(*ins)


def _s5_params_bwd(a_re, a_im, log_dt, bt_re, bt_im, lb_logits, d_abr, d_abi, d_bbr, d_bbi, d_lb):
    ins = (a_re, a_im, log_dt, bt_re, bt_im, lb_logits, d_abr, d_abi, d_bbr, d_bbi, d_lb)
    outs = [jax.ShapeDtypeStruct(a.shape, F32) for a in ins[:6]]

    def body(*refs):
        _, vjp = jax.vjp(_s5_discretize, *[r[...] for r in refs[:6]])
        for o_ref, o in zip(refs[11:], vjp(tuple(r[...] for r in refs[6:11]))):
            o_ref[...] = o

    return pl.pallas_call(body, name="s5_params_bwd", in_specs=[_whole(a.shape) for a in ins],
                          out_specs=[_whole(o.shape) for o in outs], out_shape=outs, compiler_params=_cparams())(*ins)


def _adamw(w, g, m, v, *, name):
    R, C = w.shape
    tr = _pick(R, (256, 128, 64, 32, 16, 8)) if R % SUBLANE == 0 else R
    slabs = g.shape[0] if g.ndim == 3 else 0

    def body(w_ref, g_ref, m_ref, v_ref, *outs):
        if slabs:
            gv = g_ref[0].astype(F32)
            for s in range(1, slabs):
                gv = gv + g_ref[s].astype(F32)
            outs[0][...] = gv
            outs = outs[1:]
        else:
            gv = g_ref[...]
        d_ref, mo_ref, vo_ref = outs
        m2 = ADAM_B1 * m_ref[...] + (1.0 - ADAM_B1) * gv
        v2 = ADAM_B2 * v_ref[...] + (1.0 - ADAM_B2) * (gv * gv)
        m_hat = m2 / (1.0 - ADAM_B1 ** ADAM_STEP)
        v_hat = v2 / (1.0 - ADAM_B2 ** ADAM_STEP)
        d_ref[...] = -ADAM_LR * (m_hat / (jnp.sqrt(v_hat) + ADAM_EPS) + ADAM_WD * w_ref[...])
        mo_ref[...] = m2
        vo_ref[...] = v2

    spec = pl.BlockSpec((tr, C), lambda i: (i, 0))
    g_spec = pl.BlockSpec((slabs, tr, C), lambda i: (0, i, 0)) if slabs else spec
    out = jax.ShapeDtypeStruct((R, C), F32)
    n_out = 4 if slabs else 3
    return pl.pallas_call(body, name=name, grid=(R // tr,), in_specs=[spec, g_spec, spec, spec], out_specs=[spec] * n_out,
                          out_shape=[out] * n_out, compiler_params=_cparams(("parallel",)))(w, g, m, v)


N_CHIPS = 4
N_CORES = 2


_FLIPS = tuple((dx, dy, dc) for dx in (0, 1) for dy in (0, 1) for dc in (0, 1) if (dx, dy, dc) != (0, 0, 0))


def _exchange(srcs, *, scatter, name):
    n_arr = len(srcs)
    n_peer = len(_FLIPS)
    out_shapes = [s.shape if scatter else (N_CHIPS, N_CORES) + s.shape for s in srcs]

    def body(*refs):
        src_refs, out_refs = refs[:n_arr], refs[n_arr:2 * n_arr]
        send_sems, recv_sems, local_sems = refs[2 * n_arr:]
        x, y, c = lax.axis_index("x"), lax.axis_index("y"), lax.axis_index("c")
        me_chip = 2 * x + y
        peers = [((1 - x if dx else x), (1 - y if dy else y), (1 - c if dc else c)) for dx, dy, dc in _FLIPS]
        sends, recvs, locals_ = [], [], []
        for a, (s_ref, o_ref) in enumerate(zip(src_refs, out_refs)):
            for j, (px, py, pc) in enumerate(peers):
                k = a * n_peer + j
                p_chip = 2 * px + py
                src = s_ref.at[p_chip, pc] if scatter else s_ref
                sends.append(pltpu.make_async_remote_copy(src_ref=src, dst_ref=o_ref.at[me_chip, c], send_sem=send_sems.at[k],
                                                          recv_sem=recv_sems.at[k], device_id=(px, py, pc), device_id_type=MESH))
                recvs.append(pltpu.make_async_remote_copy(src_ref=src, dst_ref=o_ref.at[p_chip, pc], send_sem=send_sems.at[k],
                                                          recv_sem=recv_sems.at[k], device_id=(px, py, pc), device_id_type=MESH))
            locals_.append(pltpu.make_async_copy(s_ref.at[me_chip, c] if scatter else s_ref, o_ref.at[me_chip, c], local_sems.at[a]))
        for cp in sends + locals_:
            cp.start()
        for cp in recvs:
            cp.wait_recv()
        for cp in sends:
            cp.wait_send()
        for cp in locals_:
            cp.wait()

    any_spec = pl.BlockSpec(memory_space=pl.ANY)
    return pl.pallas_call(
        body, name=name, in_specs=[any_spec] * n_arr, out_specs=[any_spec] * n_arr,
        out_shape=[jax.ShapeDtypeStruct(shp, s.dtype) for shp, s in zip(out_shapes, srcs)],
        scratch_shapes=[pltpu.SemaphoreType.DMA((n_arr * n_peer,)), pltpu.SemaphoreType.DMA((n_arr * n_peer,)),
                        pltpu.SemaphoreType.DMA((n_arr,))],
        compiler_params=pltpu.CompilerParams(has_side_effects=True),
    )(*srcs)


def _sum_slabs(buf, out_dtype, *, name):
    G, R, C = buf.shape
    tr = _pick(R, (512, 256, 128, 64, 32, 16, 8))

    def body(b_ref, o_ref):
        acc = b_ref[0].astype(F32)
        for s in range(1, G):
            acc = acc + b_ref[s].astype(F32)
        o_ref[...] = acc.astype(out_dtype)

    return pl.pallas_call(body, name=name, grid=(R // tr,), in_specs=[pl.BlockSpec((G, tr, C), lambda i: (0, i, 0))],
                          out_specs=pl.BlockSpec((tr, C), lambda i: (i, 0)), out_shape=jax.ShapeDtypeStruct((R, C), out_dtype),
                          compiler_params=_cparams(("parallel",)))(buf)


def _rms_fwd_fn(h, g):
    return (_rms(h, g),)


def _rms_bwd_fn(h, dhn, dres, g):
    _, vjp = jax.vjp(_rms, h, g)
    dh, dg = vjp(dhn)
    return dh + dres, dg


def _loss_fn(h, tgt, g):
    y, vjp = jax.vjp(_rms, h, g)
    diff = y - tgt
    dh, dg = vjp(diff * (1.0 / D_MODEL))
    return dh, dg, (0.5 / D_MODEL) * jnp.sum(diff * diff, axis=0, keepdims=True)


def _s5_act(ys, u, d):
    return _gelu(ys + d * u)


def _s5_gate(z, gl, b):
    return z * _sigmoid(gl + b)


def _s5_act_fn(ys, u, d):
    return (_s5_act(ys, u, d),)


def _s5_mix_fn(ya, z, gl, b):
    return (jnp.concatenate([ya, _s5_gate(z, gl, b)], axis=1),)


def _s5_gate_bwd_fn(z, gl, dyb, b):
    _, vjp = jax.vjp(_s5_gate, z, gl, b)
    return vjp(dyb)


def _s5_act_bwd_fn(ys, u, dz1, dz2, d):
    _, vjp = jax.vjp(_s5_act, ys, u, d)
    return vjp(dz1 + dz2)


def _dproj_fn(dq, df, di, dg, du1, du2):
    return (jnp.concatenate([dq, df, di, dg, du1 + du2], axis=1),)


def _rope_pair(r1, r2, pos, freqs):
    ang = pos.astype(F32) * freqs
    c, s = jnp.cos(ang), jnp.sin(ang)
    return r1 * c - r2 * s, r1 * s + r2 * c


_ODD_SPLITS = (0, MLA_Q_RANK, MLA_Q_RANK + MLA_KV_RANK, MLA_Q_RANK + MLA_KV_RANK + LANE, ODD_IN_PAD)


def _mla_prep(cq, ckv, k1, k2, qg, kvg, pos, freqs):
    ko1, ko2 = _rope_pair(k1, k2, pos, freqs)
    return _rms(cq, qg), _rms(ckv, kvg), ko1, ko2


def _mla_prep_fn(proj, pos, qg, kvg, freqs):
    parts = [proj[:, a:b] for a, b in zip(_ODD_SPLITS[:-1], _ODD_SPLITS[1:])]
    return _mla_prep(*parts, qg, kvg, pos, freqs)


def _mla_prep_bwd_fn(proj, pos, dqn, dkvn, dko1, dko2, qg, kvg, freqs):
    parts = [proj[:, a:b] for a, b in zip(_ODD_SPLITS[:-1], _ODD_SPLITS[1:])]
    _, vjp = jax.vjp(lambda *a: _mla_prep(*a, pos, freqs), *parts, qg, kvg)
    dcq, dckv, dk1, dk2, dqg, dkvg = vjp((dqn, dkvn, dko1, dko2))
    return jnp.concatenate([dcq, dckv, dk1, dk2], axis=1), dqg, dkvg


def _rope_q_fn(r1, r2, pos, freqs):
    return _rope_pair(r1, r2, pos, freqs)


def _rope_q_bwd_fn(dqn, do1, do2, pos, freqs):
    dr1, dr2 = _rope_pair(do1, do2, pos, -freqs)
    return (jnp.concatenate([dqn, dr1, dr2], axis=1),)


W_NAMES = ("norm_mix_g", "norm_ffn_g", "final_norm_g", "even_w_in", "hgrn_lb_logits", "hgrn_norm_g", "s5_a_re", "s5_a_im",
           "s5_log_dt", "s5_b_re", "s5_b_im", "s5_c_re", "s5_c_im", "s5_d", "s5_w_glu", "s5_b_glu", "even_w_out", "odd_w_in",
           "mla_q_norm_g", "mla_w_uq", "mla_kv_norm_g", "mla_w_ukv", "odd_w_out", "ffn_w_in", "ffn_conv_w", "ffn_conv_b",
           "ffn_w_out")
BIG_UNITS = (("even_w_in", 0, "col"), ("s5_w_glu", 0, "row"), ("even_w_out", 0, "row"), ("odd_w_in", 0, "row"),
             ("mla_w_uq", 0, "col"), ("mla_w_ukv", 0, "col"), ("odd_w_out", 0, "row"),
             ("ffn_w_in", 0, "col"), ("ffn_w_in", 1, "col"), ("ffn_w_out", 0, "row"), ("ffn_w_out", 1, "row"))
BIG_NAMES = tuple(dict.fromkeys(u[0] for u in BIG_UNITS))
SMALL_SHARDED = (("mla_q_norm_g", 1), ("mla_kv_norm_g", 1), ("ffn_conv_w", 2))
SMALL_SHARDED_NAMES = tuple(s[0] for s in SMALL_SHARDED)
REPLICATED = tuple(n for n in W_NAMES if n not in BIG_NAMES + SMALL_SHARDED_NAMES)


def _pack(flats, cols, row_mult):
    flat = jnp.concatenate(flats, axis=-1)
    pad = (-flat.shape[-1]) % (cols * row_mult)
    flat = jnp.pad(flat, [(0, 0)] * (flat.ndim - 1) + [(0, pad)])
    return flat.reshape(flat.shape[:-1] + (-1, cols))


def _unpack(flat, shapes):
    out, off = [], 0
    for shp in shapes:
        n = int(np.prod(shp))
        out.append(flat[..., off:off + n].reshape(flat.shape[:-1] + tuple(shp)))
        off += n
    return out


def _gather_weights(w):
    srcs = [w[n][l].astype(BF16) for n, l, _ in BIG_UNITS]
    srcs.append(_pack([w[n].reshape(-1) for n in SMALL_SHARDED_NAMES], LANE, SUBLANE))
    g2 = _exchange(srcs, scatter=False, name="gather_w")
    big = {}
    for (n, l, kind), g in zip(BIG_UNITS, g2):
        r, c = g.shape[2:]
        big[(n, l)] = g.reshape(N_DEV * r, c) if kind == "row" else g.transpose(2, 0, 1, 3).reshape(r, N_DEV * c)
    parts = _unpack(g2[-1].reshape(N_DEV, -1), [w[n].shape for n in SMALL_SHARDED_NAMES])
    small = {}
    for (n, ax), p in zip(SMALL_SHARDED, parts):
        shp = list(w[n].shape)
        shp[ax] *= N_DEV
        small[n] = jnp.moveaxis(p, 0, ax).reshape(shp)
    return big, small


def _exchange_grads(g_big, g_small, g_repl, w):
    srcs = []
    for n, l, kind in BIG_UNITS:
        g = g_big[(n, l)].astype(BF16)
        if kind == "row":
            srcs.append(g.reshape(N_CHIPS, N_CORES, g.shape[0] // N_DEV, g.shape[1]))
        else:
            srcs.append(g.reshape(g.shape[0], N_CHIPS, N_CORES, g.shape[1] // N_DEV).transpose(1, 2, 0, 3))
    flats = []
    for n, ax in SMALL_SHARDED:
        shp = list(w[n].shape)
        g = g_small[n].astype(F32).reshape(shp[:ax] + [N_DEV] + shp[ax:])
        flats.append(jnp.moveaxis(g, ax, 0).reshape(N_DEV, -1))
    small = _pack(flats, LANE, SUBLANE)
    srcs.append(small.reshape((N_CHIPS, N_CORES) + small.shape[1:]))
    vec = _pack([g_repl[n].reshape(-1).astype(F32) for n in REPLICATED], LANE, SUBLANE)
    srcs.append(jnp.broadcast_to(vec, (N_CHIPS, N_CORES) + vec.shape))
    outs = _exchange(srcs, scatter=True, name="exchange_g")
    return [o.reshape((N_DEV,) + o.shape[2:]) for o in outs]


def _block_diag(blocks):
    G, a, b = blocks.shape
    return jnp.einsum('gab,gk->gakb', blocks, jnp.eye(G, dtype=blocks.dtype)).reshape(G * a, G * b)


def _diag_blocks(mat, a, b):
    G = mat.shape[0] // a
    return jnp.einsum('gagb->gab', mat.reshape(G, a, G, b))


def _ffn_fwd(h, g, w_in, conv_w, conv_b, w_out, tag):
    hn, = _rows(_rms_fwd_fn, [h], [g], [(D_MODEL, BF16)], [], name=f"ffn{tag}_norm")
    au = _mm(hn, w_in, name=f"ffn{tag}_in")
    z = _ffn_mid_fwd(au, conv_w, conv_b, name=f"ffn{tag}_mid")
    return _mm(z, w_out, res=h, name=f"ffn{tag}_out"), (hn, au, z)


def _ffn_bwd(h, dh, saved, g, w_in, conv_w, conv_b, w_out, tag):
    hn, au, z = saved
    dz = _mm(dh, w_out, tb=True, name=f"ffn{tag}_dz")
    dw_out = _mm(z, dh, ta=True, out_dtype=BF16, name=f"ffn{tag}_dwout")
    da, du, dcw, dcb = _ffn_mid_bwd(au, dz, conv_w, conv_b, name=f"ffn{tag}_dmid")
    dhn = _mm(da, w_in, tb=True, b_cols=(0, D_FF), name=f"ffn{tag}_dhn_a")
    dhn = _mm(du, w_in, tb=True, b_cols=(D_FF, D_FF), res=dhn, name=f"ffn{tag}_dhn_u")
    dw_in = jnp.concatenate([_mm(hn, da, ta=True, out_dtype=BF16, name=f"ffn{tag}_dwin_a"),
                             _mm(hn, du, ta=True, out_dtype=BF16, name=f"ffn{tag}_dwin_u")], axis=1)
    dh_in, dg = _rows(_rms_bwd_fn, [h, dhn, dh], [g], [(D_MODEL, F32)], [(1, D_MODEL)], name=f"ffn{tag}_dnorm")
    return dh_in, dict(g=dg, w_in=dw_in, conv_w=dcw, conv_b=dcb, w_out=dw_out)


def kernel(x, positions, norm_mix_g, norm_ffn_g, final_norm_g, even_w_in, hgrn_lb_logits, hgrn_norm_g, s5_a_re, s5_a_im, s5_log_dt, s5_b_re, s5_b_im, s5_c_re, s5_c_im, s5_d, s5_w_glu, s5_b_glu, even_w_out, odd_w_in, mla_q_norm_g, mla_w_uq, mla_kv_norm_g, mla_w_ukv, odd_w_out, ffn_w_in, ffn_conv_w, ffn_conv_b, ffn_w_out, loss_target, m_norm_mix_g, m_norm_ffn_g, m_final_norm_g, m_even_w_in, m_hgrn_lb_logits, m_hgrn_norm_g, m_s5_a_re, m_s5_a_im, m_s5_log_dt, m_s5_b_re, m_s5_b_im, m_s5_c_re, m_s5_c_im, m_s5_d, m_s5_w_glu, m_s5_b_glu, m_even_w_out, m_odd_w_in, m_mla_q_norm_g, m_mla_w_uq, m_mla_kv_norm_g, m_mla_w_ukv, m_odd_w_out, m_ffn_w_in, m_ffn_conv_w, m_ffn_conv_b, m_ffn_w_out, v_norm_mix_g, v_norm_ffn_g, v_final_norm_g, v_even_w_in, v_hgrn_lb_logits, v_hgrn_norm_g, v_s5_a_re, v_s5_a_im, v_s5_log_dt, v_s5_b_re, v_s5_b_im, v_s5_c_re, v_s5_c_im, v_s5_d, v_s5_w_glu, v_s5_b_glu, v_even_w_out, v_odd_w_in, v_mla_q_norm_g, v_mla_w_uq, v_mla_kv_norm_g, v_mla_w_ukv, v_odd_w_out, v_ffn_w_in, v_ffn_conv_w, v_ffn_conv_b, v_ffn_w_out):
    given = dict(locals())
    w = {n: given[n] for n in W_NAMES}
    mom = {n: given["m_" + n] for n in W_NAMES}
    var = {n: given["v_" + n] for n in W_NAMES}
    T = x.shape[1]
    h0 = x[0]
    tgt = loss_target[0]
    pos = positions.reshape(T, 1)

    full, full_small = _gather_weights(w)
    w_ein = full["even_w_in", 0]
    w_glu = full["s5_w_glu", 0]
    w_eout = full["even_w_out", 0]
    w_oin = full["odd_w_in", 0]
    zpad = jnp.zeros((D_MODEL, LANE - MLA_ROPE // 2), BF16)
    kr0 = MLA_Q_RANK + MLA_KV_RANK
    w_oin_pad = jnp.concatenate([w_oin[:, :kr0], w_oin[:, kr0:kr0 + MLA_ROPE // 2], zpad, w_oin[:, kr0 + MLA_ROPE // 2:], zpad], axis=1)
    qg, kvg = full_small["mla_q_norm_g"], full_small["mla_kv_norm_g"]
    w_uq3 = full["mla_w_uq", 0].reshape(MLA_Q_RANK, MLA_HEADS, MLA_QK)
    half = MLA_ROPE // 2
    w_uq_perm = jnp.concatenate([w_uq3[:, :, :MLA_NOPE].reshape(MLA_Q_RANK, -1),
                                 w_uq3[:, :, MLA_NOPE:MLA_NOPE + half].reshape(MLA_Q_RANK, -1),
                                 w_uq3[:, :, MLA_NOPE + half:].reshape(MLA_Q_RANK, -1)], axis=1)
    w_ukv = full["mla_w_ukv", 0]
    w_oout = full["odd_w_out", 0]
    w_fin = [full["ffn_w_in", 0], full["ffn_w_in", 1]]
    w_fout = [full["ffn_w_out", 0], full["ffn_w_out", 1]]
    conv_w = full_small["ffn_conv_w"]
    freqs = ROPE_THETA ** (-jnp.arange(0, MLA_ROPE, 2, dtype=F32) / MLA_ROPE)
    freqs_q = jnp.tile(freqs, MLA_HEADS)[None, :]
    freqs_k = jnp.concatenate([freqs, jnp.zeros((LANE - half,), F32)])[None, :]

    sp_in = (s5_a_re[0], s5_a_im[0], s5_log_dt[0][:, None], s5_b_re[0].transpose(0, 2, 1), s5_b_im[0].transpose(0, 2, 1),
             hgrn_lb_logits)
    abr, abi, bbt_re, bbt_im, lb0 = _s5_params_fwd(*sp_in)
    a_re, a_im = abr.reshape(1, S5_WIDTH), abi.reshape(1, S5_WIDTH)
    bb_re, bb_im = _block_diag(bbt_re).astype(BF16), _block_diag(bbt_im).astype(BF16)
    bb_cat = jnp.concatenate([bb_re, bb_im], axis=1)
    c_re = _block_diag(s5_c_re[0].transpose(0, 2, 1)).astype(BF16)
    c_im_neg = _block_diag(-s5_c_im[0].transpose(0, 2, 1)).astype(BF16)
    u_cols = (4 * HGRN_DIM, S5_DIM)

    hn0, = _rows(_rms_fwd_fn, [h0], [norm_mix_g[0:1]], [(D_MODEL, BF16)], [], name="mix0_norm")
    proj = _mm(hn0, w_ein, name="even_in")
    y_a, states = _hgrn_fwd(proj, lb0, hgrn_norm_g, name="hgrn_fwd")
    bu = _mm(proj, bb_cat, a_cols=u_cols, name="s5_bu")
    s_re, s_im = _s5_scan_fwd(bu, a_re, a_im, name="s5_scan_fwd")
    ys = _mm(s_im, c_im_neg, res=_mm(s_re, c_re, name="s5_y_re"), name="s5_y_im")
    z5, = _rows(_s5_act_fn, [ys, (proj,) + u_cols], [s5_d], [(S5_DIM, F32)], [], name="s5_act")
    gl = _mm(z5, w_glu, name="s5_glu")
    mixin, = _rows(_s5_mix_fn, [y_a, z5, gl], [s5_b_glu], [(D_MODEL, BF16)], [], name="s5_mix")
    h1 = _mm(mixin, w_eout, res=h0, name="even_out")
    h2, ffn0_saved = _ffn_fwd(h1, norm_ffn_g[0:1], w_fin[0], conv_w[0], ffn_conv_b[0:1], w_fout[0], 0)

    hn1, = _rows(_rms_fwd_fn, [h2], [norm_mix_g[1:2]], [(D_MODEL, BF16)], [], name="mix1_norm")
    proj_o = _mm(hn1, w_oin_pad, name="odd_in")
    qn, kvn, ko1, ko2 = _rows(_mla_prep_fn, [proj_o, pos], [qg, kvg, freqs_k],
                              [(MLA_Q_RANK, BF16), (MLA_KV_RANK, BF16), (LANE, F32), (LANE, F32)], [], name="mla_prep")
    q_all = _mm(qn, w_uq_perm, name="mla_uq")
    kv = _mm(kvn, w_ukv, name="mla_ukv")
    nope_w = MLA_HEADS * MLA_NOPE
    rope_w = MLA_HEADS * half
    o1, o2 = _rows(_rope_q_fn, [(q_all, nope_w, rope_w), (q_all, nope_w + rope_w, rope_w), pos], [freqs_q],
                   [(rope_w, F32), (rope_w, F32)], [], name="mla_rope_q")
    q_rope = jnp.concatenate([o1.reshape(T, MLA_HEADS, half), o2.reshape(T, MLA_HEADS, half)], axis=2).transpose(1, 0, 2)
    k_rope = jnp.concatenate([ko1[:, :half], ko2[:, :half]], axis=1)
    o, lse = _attn_fwd(q_all, q_rope, kv, k_rope, name="attn_fwd")
    h3 = _mm(o, w_oout, res=h2, name="odd_out")
    h4, ffn1_saved = _ffn_fwd(h3, norm_ffn_g[1:2], w_fin[1], conv_w[1], ffn_conv_b[1:2], w_fout[1], 1)

    dh4, d_final_g, loss_cols = _rows(_loss_fn, [h4, tgt], [final_norm_g[None, :]], [(D_MODEL, F32)],
                                      [(1, D_MODEL), (1, D_MODEL)], name="loss_head")
    loss = lax.psum(jnp.sum(loss_cols), ("x", "y", "c"))

    dh3, gf1 = _ffn_bwd(h3, dh4, ffn1_saved, norm_ffn_g[1:2], w_fin[1], conv_w[1], ffn_conv_b[1:2], w_fout[1], 1)
    do = _mm(dh3, w_oout, tb=True, name="odd_out_dx")
    d_w_oout = _mm(o, dh3, ta=True, out_dtype=BF16, name="odd_out_dw")
    dq_nope, dq_rope, dkv, dk_rope = _attn_bwd(q_all, q_rope, kv, k_rope, o, lse, do, name="attn_bwd")
    dq_rope_t = dq_rope.transpose(1, 0, 2)
    do1, do2 = dq_rope_t[:, :, :half].reshape(T, rope_w), dq_rope_t[:, :, half:].reshape(T, rope_w)
    lane_pad = ((0, 0), (0, LANE - half))
    dko1, dko2 = jnp.pad(dk_rope[:, :half], lane_pad), jnp.pad(dk_rope[:, half:], lane_pad)
    dq_all, = _rows(_rope_q_bwd_fn, [dq_nope, do1, do2, pos], [freqs_q], [(MLA_HEADS * MLA_QK, BF16)], [], name="mla_rope_q_bwd")
    d_w_uq_perm = _mm(qn, dq_all, ta=True, out_dtype=BF16, name="mla_uq_dw")
    dqn = _mm(dq_all, w_uq_perm, tb=True, name="mla_uq_dx")
    d_w_ukv = _mm(kvn, dkv, ta=True, out_dtype=BF16, name="mla_ukv_dw")
    dkvn = _mm(dkv, w_ukv, tb=True, name="mla_ukv_dx")
    dproj_o, d_qg, d_kvg = _rows(_mla_prep_bwd_fn, [proj_o, pos, dqn, dkvn, dko1, dko2], [qg, kvg, freqs_k],
                                 [(ODD_IN_PAD, BF16)], [(1, MLA_Q_RANK), (1, MLA_KV_RANK)], name="mla_prep_bwd")
    d_w_oin_pad = _mm(hn1, dproj_o, ta=True, out_dtype=BF16, name="odd_in_dw")
    dhn1 = _mm(dproj_o, w_oin_pad, tb=True, name="odd_in_dx")
    dh2, d_mix_g1 = _rows(_rms_bwd_fn, [h2, dhn1, dh3], [norm_mix_g[1:2]], [(D_MODEL, F32)], [(1, D_MODEL)], name="mix1_dnorm")
    d_w_oin = jnp.concatenate([d_w_oin_pad[:, :kr0 + half], d_w_oin_pad[:, kr0 + LANE:kr0 + LANE + half]], axis=1)
    d3 = d_w_uq_perm
    d_w_uq = jnp.concatenate([d3[:, :nope_w].reshape(MLA_Q_RANK, MLA_HEADS, MLA_NOPE),
                              d3[:, nope_w:nope_w + rope_w].reshape(MLA_Q_RANK, MLA_HEADS, half),
                              d3[:, nope_w + rope_w:].reshape(MLA_Q_RANK, MLA_HEADS, half)], axis=2).reshape(MLA_Q_RANK, -1)

    dh1, gf0 = _ffn_bwd(h1, dh2, ffn0_saved, norm_ffn_g[0:1], w_fin[0], conv_w[0], ffn_conv_b[0:1], w_fout[0], 0)
    dmix = _mm(dh1, w_eout, tb=True, name="even_out_dx")
    d_w_eout = _mm(mixin, dh1, ta=True, out_dtype=BF16, name="even_out_dw")
    dq, df, di, dg, d_lb0, d_hgrn_g = _hgrn_bwd(proj, lb0, hgrn_norm_g, states, dmix, name="hgrn_bwd")
    dz1, dgl, d_b_glu = _rows(_s5_gate_bwd_fn, [z5, gl, (dmix, HGRN_DIM, S5_DIM)], [s5_b_glu],
                              [(S5_DIM, F32), (S5_DIM, BF16)], [(1, S5_DIM)], name="s5_gate_bwd")
    dz2 = _mm(dgl, w_glu, tb=True, name="s5_glu_dx")
    d_w_glu = _mm(z5, dgl, ta=True, out_dtype=BF16, name="s5_glu_dw")
    dys, du1, d_s5_d = _rows(_s5_act_bwd_fn, [ys, (proj,) + u_cols, dz1, dz2], [s5_d],
                             [(S5_DIM, BF16), (S5_DIM, F32)], [(1, S5_DIM)], name="s5_act_bwd")
    ds_re = _mm(dys, c_re, tb=True, name="s5_ds_re")
    ds_im = _mm(dys, c_im_neg, tb=True, name="s5_ds_im")
    d_c_re = _mm(s_re, dys, ta=True, name="s5_dc_re")
    d_c_im_neg = _mm(s_im, dys, ta=True, name="s5_dc_im")
    lam_re, lam_im, d_ar, d_ai = _s5_scan_bwd(ds_re, ds_im, s_re, s_im, a_re, a_im, name="s5_scan_bwd")
    du2 = _mm(lam_im, bb_im, tb=True, res=_mm(lam_re, bb_re, tb=True, name="s5_du_re"), name="s5_du_im")
    d_bb_re = _mm(proj, lam_re, ta=True, a_cols=u_cols, name="s5_dbb_re")
    d_bb_im = _mm(proj, lam_im, ta=True, a_cols=u_cols, name="s5_dbb_im")
    dproj, = _rows(_dproj_fn, [dq, df, di, dg, du1, du2], [], [(EVEN_IN, BF16)], [], name="even_dproj")
    d_w_ein = _mm(hn0, dproj, ta=True, out_dtype=BF16, name="even_in_dw")
    dhn0 = _mm(dproj, w_ein, tb=True, name="even_in_dx")
    grad_x, d_mix_g0 = _rows(_rms_bwd_fn, [h0, dhn0, dh1], [norm_mix_g[0:1]], [(D_MODEL, F32)], [(1, D_MODEL)], name="mix0_dnorm")
    sp_g = _s5_params_bwd(*sp_in, d_ar.sum(0).reshape(S5_GROUPS, S5_STATE), d_ai.sum(0).reshape(S5_GROUPS, S5_STATE),
                          _diag_blocks(d_bb_re, S5_GROUP, S5_STATE), _diag_blocks(d_bb_im, S5_GROUP, S5_STATE), d_lb0)
    d_a_re, d_a_im, d_log_dt, d_bt_re, d_bt_im, d_lb_logits = sp_g

    g_big = {("even_w_in", 0): d_w_ein, ("s5_w_glu", 0): d_w_glu, ("even_w_out", 0): d_w_eout, ("odd_w_in", 0): d_w_oin,
             ("mla_w_uq", 0): d_w_uq, ("mla_w_ukv", 0): d_w_ukv, ("odd_w_out", 0): d_w_oout,
             ("ffn_w_in", 0): gf0["w_in"], ("ffn_w_in", 1): gf1["w_in"], ("ffn_w_out", 0): gf0["w_out"], ("ffn_w_out", 1): gf1["w_out"]}
    g_small = dict(mla_q_norm_g=d_qg, mla_kv_norm_g=d_kvg, ffn_conv_w=jnp.stack([gf0["conv_w"], gf1["conv_w"]]))
    g_repl = dict(
        norm_mix_g=jnp.concatenate([d_mix_g0, d_mix_g1]), norm_ffn_g=jnp.concatenate([gf0["g"], gf1["g"]]),
        final_norm_g=d_final_g[0], hgrn_lb_logits=d_lb_logits, hgrn_norm_g=d_hgrn_g,
        s5_a_re=d_a_re[None], s5_a_im=d_a_im[None], s5_log_dt=d_log_dt[:, 0][None],
        s5_b_re=d_bt_re.transpose(0, 2, 1)[None], s5_b_im=d_bt_im.transpose(0, 2, 1)[None],
        s5_c_re=_diag_blocks(d_c_re, S5_STATE, S5_GROUP).transpose(0, 2, 1)[None],
        s5_c_im=-_diag_blocks(d_c_im_neg, S5_STATE, S5_GROUP).transpose(0, 2, 1)[None],
        s5_d=d_s5_d, s5_b_glu=d_b_glu, ffn_conv_b=jnp.concatenate([gf0["conv_b"], gf1["conv_b"]]))
    partial = _exchange_grads(g_big, g_small, g_repl, w)
    repl_sum = _sum_slabs(partial[-1], F32, name="sum_g_repl").reshape(-1)
    grads = dict(zip(REPLICATED, _unpack(repl_sum, [w[n].shape for n in REPLICATED])))
    small_sum = _sum_slabs(partial[-2], F32, name="sum_g_small").reshape(-1)
    grads.update(zip(SMALL_SHARDED_NAMES, _unpack(small_sum, [w[n].shape for n in SMALL_SHARDED_NAMES])))

    delta, new_m, new_v = {}, {}, {}
    per_unit = {}
    for (n, l, _), slabs in zip(BIG_UNITS, partial):
        per_unit[n, l] = _adamw(w[n][l], slabs, mom[n][l], var[n][l], name=f"adamw_{n}_{l}")
    for n in BIG_NAMES:
        layers = [per_unit[n, l] for l in range(w[n].shape[0])]
        grads[n], delta[n], new_m[n], new_v[n] = (jnp.stack([lay[k] for lay in layers]) for k in range(4))
    small = [n for n in W_NAMES if n not in BIG_NAMES]
    packed = [_pack([t[n].reshape(-1) for n in small], LANE, SUBLANE) for t in (w, grads, mom, var)]
    outs = _adamw(*packed, name="adamw_small")
    small_shapes = [w[n].shape for n in small]
    for dst, o_ in zip((delta, new_m, new_v), outs):
        dst.update(zip(small, _unpack(o_.reshape(-1), small_shapes)))

    return (loss, grad_x[None], *[grads[n] for n in W_NAMES], *[delta[n] for n in W_NAMES],
            *[new_m[n] for n in W_NAMES], *[new_v[n] for n in W_NAMES])
```

```python
import functools
import math

import numpy as np
import jax
import jax.numpy as jnp
from jax import lax
from jax.experimental import pallas as pl
from jax.experimental.pallas import tpu as pltpu

F32 = jnp.float32
BF16 = jnp.bfloat16
_MXU_DTYPE = jnp.bfloat16
_HI = lax.Precision.HIGHEST

D_MODEL = 1024
HGRN_DIM = 512
HGRN_HEAD_DIM = 128
HGRN_HEADS = 4
HGRN_CHUNK = 64
S5_DIM = 512
S5_GROUPS = 32
S5_GROUP = 16
S5_STATE = 64
S5_WIDTH = S5_GROUPS * S5_STATE
EVEN_IN = 4 * HGRN_DIM + S5_DIM
MLA_HEADS = 8
MLA_Q_RANK = 384
MLA_KV_RANK = 256
MLA_NOPE = 128
MLA_ROPE = 64
MLA_V = 128
MLA_QK = MLA_NOPE + MLA_ROPE
ODD_IN = MLA_Q_RANK + MLA_KV_RANK + MLA_ROPE
ODD_IN_PAD = MLA_Q_RANK + MLA_KV_RANK + 2 * 128
ROPE_THETA = 10000.0
D_FF = 2816
EPS = 1e-6
ADAM_LR = 0.001
ADAM_B1 = 0.9
ADAM_B2 = 0.999
ADAM_EPS = 1e-08
ADAM_WD = 0.01
ADAM_STEP = 10

N_DEV = 8
LANE = 128
SUBLANE = 8
VMEM_LIMIT_BYTES = 56 * 1024 * 1024
MESH = pl.DeviceIdType.MESH


def _cparams(sem=None):
    return pltpu.CompilerParams(dimension_semantics=sem, vmem_limit_bytes=VMEM_LIMIT_BYTES)


def _pick(n, cands):
    for c in cands:
        if n % c == 0:
            return c
    raise ValueError(f"no tile for {n} in {cands}")


def _sigmoid(x):
    return 1.0 / (1.0 + jnp.exp(-x))


def _silu(x):
    return x * _sigmoid(x)


def _gelu(x):
    return 0.5 * x * (1.0 + jnp.tanh(math.sqrt(2.0 / math.pi) * (x + 0.044715 * (x * x * x))))


def _rms(x, g):
    return x * lax.rsqrt(jnp.mean(x * x, axis=-1, keepdims=True) + EPS) * g


def _mxu(a, b, ca, cb):
    return lax.dot_general(a.astype(_MXU_DTYPE), b.astype(_MXU_DTYPE), (((ca,), (cb,)), ((), ())),
                           preferred_element_type=F32)


@functools.partial(jax.custom_vjp, nondiff_argnums=(2, 3))
def _mxu_ad(a, b, ca, cb):
    return _mxu(a, b, ca, cb)


def _mxu_ad_fwd(a, b, ca, cb):
    return _mxu(a, b, ca, cb), (a, b)


def _mxu_ad_bwd(ca, cb, saved, g):
    a, b = saved
    fa, fb = 1 - ca, 1 - cb
    da = _mxu(g, b, 1, fb) if ca == 1 else _mxu(b, g, fb, 1)
    db = _mxu(a, g, fa, 0) if cb == 0 else _mxu(g, a, 0, fa)
    return da, db


_mxu_ad.defvjp(_mxu_ad_fwd, _mxu_ad_bwd)


def _tri(n, upper=False):
    row = lax.broadcasted_iota(jnp.int32, (n, n), 0)
    col = lax.broadcasted_iota(jnp.int32, (n, n), 1)
    return (col >= row) if upper else (col <= row)


def _cumsum_rows(x):
    return jnp.dot(_tri(x.shape[0]).astype(F32), x, precision=_HI, preferred_element_type=F32)


@jax.custom_vjp
def _cumsum_rows_ad(x):
    return _cumsum_rows(x)


def _cumsum_rows_ad_fwd(x):
    return _cumsum_rows(x), None


def _cumsum_rows_ad_bwd(_, g):
    return (jnp.dot(_tri(g.shape[0], upper=True).astype(F32), g, precision=_HI, preferred_element_type=F32),)


_cumsum_rows_ad.defvjp(_cumsum_rows_ad_fwd, _cumsum_rows_ad_bwd)


MM_VMEM_BUDGET = 36 * 1024 * 1024
MM_MAX_TILE = 1408


def _lane_divisors(n, cap, offs=()):
    return [d for d in range(min(n, cap) // LANE * LANE, 0, -LANE) if n % d == 0 and all(o % d == 0 for o in offs)]


def _mm_tiles(M, N, K, sa, sb, so, has_res, m_offs, n_offs, k_offs):
    best = None
    for tm in _lane_divisors(M, MM_MAX_TILE, m_offs):
        for tn in _lane_divisors(N, MM_MAX_TILE, n_offs):
            for tk in _lane_divisors(K, K, k_offs):
                nk = K // tk
                vmem = 2 * (tm * tk * sa + tk * tn * sb + tm * tn * so + tm * tn * 4 * has_res) + (tm * tn * 4 if nk > 1 else 0)
                if vmem <= MM_VMEM_BUDGET:
                    key = (-nk, tm * tn, tn)
                    if best is None or key > best[0]:
                        best = (key, tm, tn, tk)
                    break
    return best[1:]


def _mm(a, b, *, ta=False, tb=False, res=None, out_dtype=F32, a_cols=None, b_cols=None, deps=(), name):
    a_minor = a.shape[1] if a_cols is None else a_cols[1]
    b_minor = b.shape[1] if b_cols is None else b_cols[1]
    K, M = (a.shape[0], a_minor) if ta else (a_minor, a.shape[0])
    N = b.shape[0] if tb else b_minor
    assert (b_minor if tb else b.shape[0]) == K, (a.shape, b.shape, ta, tb)
    a_off = 0 if a_cols is None else a_cols[0]
    b_off = 0 if b_cols is None else b_cols[0]
    has_res = res is not None
    tm, tn, tk = _mm_tiles(M, N, K, a.dtype.itemsize, b.dtype.itemsize, jnp.dtype(out_dtype).itemsize, has_res,
                           (a_off,) if ta else (), () if tb else (b_off,), ((a_off,) if not ta else ()) + ((b_off,) if tb else ()))
    nk = K // tk
    am, ak = (a_off // tm, 0) if ta else (0, a_off // tk)
    bn, bk = (0, b_off // tk) if tb else (b_off // tn, 0)
    a_spec = pl.BlockSpec((tk, tm), lambda i, j, k: (k, i + am)) if ta else pl.BlockSpec((tm, tk), lambda i, j, k: (i, k + ak))
    b_spec = pl.BlockSpec((tn, tk), lambda i, j, k: (j, k + bk)) if tb else pl.BlockSpec((tk, tn), lambda i, j, k: (k, j + bn))
    o_spec = pl.BlockSpec((tm, tn), lambda i, j, k: (i, j))
    ca, cb = (0 if ta else 1), (1 if tb else 0)

    n_fixed = 2 + has_res + len(deps)

    def body(*refs):
        a_ref, b_ref = refs[0], refs[1]
        res_ref = refs[2] if has_res else None
        o_ref = refs[n_fixed]
        part = _mxu(a_ref[...], b_ref[...], ca, cb)
        if nk == 1:
            o_ref[...] = (part + res_ref[...] if has_res else part).astype(out_dtype)
            return
        acc_ref = refs[n_fixed + 1]
        k = pl.program_id(2)

        @pl.when(k == 0)
        def _():
            acc_ref[...] = part

        @pl.when(k > 0)
        def _():
            acc_ref[...] += part

        @pl.when(k == nk - 1)
        def _():
            o_ref[...] = (acc_ref[...] + res_ref[...] if has_res else acc_ref[...]).astype(out_dtype)

    ins = [a, b] + ([res] if has_res else []) + list(deps)
    in_specs = [a_spec, b_spec] + ([o_spec] if has_res else []) + [pl.BlockSpec(memory_space=pl.ANY)] * len(deps)
    return pl.pallas_call(
        body, name=name, grid=(M // tm, N // tn, nk),
        in_specs=in_specs, out_specs=o_spec,
        out_shape=jax.ShapeDtypeStruct((M, N), out_dtype),
        scratch_shapes=[pltpu.VMEM((tm, tn), F32)] if nk > 1 else [],
        compiler_params=_cparams(("parallel", "parallel", "arbitrary")),
    )(*ins)


def _rows(fn, row_ins, const_ins, row_outs, acc_outs, *, name, tm=256, deps=()):
    norm = [(r, 0, r.shape[1]) if not isinstance(r, tuple) else r for r in row_ins]
    T = norm[0][0].shape[0]
    tm = min(tm, T)
    nr, nc, no, na = len(norm), len(const_ins), len(row_outs), len(acc_outs)
    first_out = nr + nc + len(deps)

    def body(*refs):
        i = pl.program_id(0)
        vals = [r[...] for r in refs[:nr + nc]]
        outs = fn(*vals)
        for o_ref, o in zip(refs[first_out:first_out + no], outs[:no]):
            o_ref[...] = o.astype(o_ref.dtype)
        for a_ref, o in zip(refs[first_out + no:], outs[no:]):
            @pl.when(i == 0)
            def _(a_ref=a_ref, o=o):
                a_ref[...] = o

            @pl.when(i > 0)
            def _(a_ref=a_ref, o=o):
                a_ref[...] += o

    in_specs = []
    for arr, off, w in norm:
        assert off % w == 0, (off, w)
        in_specs.append(pl.BlockSpec((tm, w), lambda i, b=off // w: (i, b)))
    for c in const_ins:
        in_specs.append(pl.BlockSpec(c.shape, lambda i: (0, 0)))
    in_specs += [pl.BlockSpec(memory_space=pl.ANY)] * len(deps)
    out_specs = [pl.BlockSpec((tm, w), lambda i: (i, 0)) for w, _ in row_outs]
    out_specs += [pl.BlockSpec(s, lambda i: (0, 0)) for s in acc_outs]
    out_shape = [jax.ShapeDtypeStruct((T, w), dt) for w, dt in row_outs]
    out_shape += [jax.ShapeDtypeStruct(s, F32) for s in acc_outs]
    return pl.pallas_call(
        body, name=name, grid=(T // tm,), in_specs=in_specs, out_specs=out_specs, out_shape=out_shape,
        compiler_params=_cparams(("arbitrary",)),
    )(*[n[0] for n in norm], *const_ins, *deps)


FFN_COL_TILE = LANE
FFN_ROW_CHUNK = 512


def _shift_down(ext, s, rows):
    return pltpu.roll(ext, s, 0)[SUBLANE:SUBLANE + rows]


def _shift_up(ext, s, rows):
    return pltpu.roll(ext, rows + SUBLANE - s, 0)[:rows]


def _ffn_chunks(T):
    r = min(FFN_ROW_CHUNK, T)
    return r, T // r


def _ext_before(ref, c, r):
    if c == 0:
        return jnp.concatenate([jnp.zeros((SUBLANE, ref.shape[1]), F32), ref[0:r, :]], axis=0)
    return ref[c * r - SUBLANE:(c + 1) * r, :]


def _ext_after(ref, c, r, nch):
    if c == nch - 1:
        return jnp.concatenate([ref[c * r:(c + 1) * r, :], jnp.zeros((SUBLANE, ref.shape[1]), F32)], axis=0)
    return ref[c * r:(c + 1) * r + SUBLANE, :]


def _ffn_mid_fwd(au, conv_w, conv_b, *, name):
    T = au.shape[0]
    tc = FFN_COL_TILE
    ncol = D_FF // tc
    r, nch = _ffn_chunks(T)

    def body(a_ref, u_ref, w_ref, b_ref, z_ref):
        w0, w1, w2, bias = w_ref[0:1, :], w_ref[1:2, :], w_ref[2:3, :], b_ref[...]
        for c in range(nch):
            ext = _ext_before(a_ref, c, r)
            pre = w0 * _shift_down(ext, 2, r) + w1 * _shift_down(ext, 1, r) + w2 * ext[SUBLANE:] + bias
            z_ref[c * r:(c + 1) * r, :] = (_silu(pre) * u_ref[c * r:(c + 1) * r, :]).astype(z_ref.dtype)

    return pl.pallas_call(
        body, name=name, grid=(ncol,),
        in_specs=[pl.BlockSpec((T, tc), lambda j: (0, j)), pl.BlockSpec((T, tc), lambda j: (0, j + ncol)),
                  pl.BlockSpec((3, tc), lambda j: (0, j)), pl.BlockSpec((1, tc), lambda j: (0, j))],
        out_specs=pl.BlockSpec((T, tc), lambda j: (0, j)),
        out_shape=jax.ShapeDtypeStruct((T, D_FF), BF16),
        compiler_params=_cparams(("parallel",)),
    )(au, au, conv_w, conv_b)


def _ffn_mid_bwd(au, dz, conv_w, conv_b, *, name):
    T = au.shape[0]
    tc = FFN_COL_TILE
    ncol = D_FF // tc
    r, nch = _ffn_chunks(T)

    def body(a_ref, u_ref, dz_ref, w_ref, b_ref, da_ref, du_ref, dw_ref, db_ref, dpre_ref):
        w0, w1, w2, bias = w_ref[0:1, :], w_ref[1:2, :], w_ref[2:3, :], b_ref[...]
        dw0 = jnp.zeros((1, tc), F32)
        dw1 = jnp.zeros((1, tc), F32)
        dw2 = jnp.zeros((1, tc), F32)
        db = jnp.zeros((1, tc), F32)
        for c in range(nch):
            rows = slice(c * r, (c + 1) * r)
            ext = _ext_before(a_ref, c, r)
            a2, a1, a0 = _shift_down(ext, 2, r), _shift_down(ext, 1, r), ext[SUBLANE:]
            pre = w0 * a2 + w1 * a1 + w2 * a0 + bias
            sg = _sigmoid(pre)
            act = pre * sg
            dzc = dz_ref[rows, :]
            du_ref[rows, :] = (dzc * act).astype(du_ref.dtype)
            dpre = dzc * u_ref[rows, :] * (sg * (1.0 + pre * (1.0 - sg)))
            dpre_ref[rows, :] = dpre
            dw0 += jnp.sum(dpre * a2, axis=0, keepdims=True)
            dw1 += jnp.sum(dpre * a1, axis=0, keepdims=True)
            dw2 += jnp.sum(dpre * a0, axis=0, keepdims=True)
            db += jnp.sum(dpre, axis=0, keepdims=True)
        for c in range(nch):
            ext = _ext_after(dpre_ref, c, r, nch)
            da = w0 * _shift_up(ext, 2, r) + w1 * _shift_up(ext, 1, r) + w2 * ext[:r]
            da_ref[c * r:(c + 1) * r, :] = da.astype(da_ref.dtype)
        dw_ref[0:1, :] = dw0
        dw_ref[1:2, :] = dw1
        dw_ref[2:3, :] = dw2
        db_ref[...] = db

    col = lambda j: (0, j)
    return pl.pallas_call(
        body, name=name, grid=(ncol,),
        in_specs=[pl.BlockSpec((T, tc), col), pl.BlockSpec((T, tc), lambda j: (0, j + ncol)), pl.BlockSpec((T, tc), col),
                  pl.BlockSpec((3, tc), col), pl.BlockSpec((1, tc), col)],
        out_specs=[pl.BlockSpec((T, tc), col), pl.BlockSpec((T, tc), col), pl.BlockSpec((3, tc), col), pl.BlockSpec((1, tc), col)],
        out_shape=[jax.ShapeDtypeStruct((T, D_FF), BF16), jax.ShapeDtypeStruct((T, D_FF), BF16),
                   jax.ShapeDtypeStruct((3, D_FF), F32), jax.ShapeDtypeStruct((1, D_FF), F32)],
        scratch_shapes=[pltpu.VMEM((T, tc), F32)],
        compiler_params=_cparams(("parallel",)),
    )(au, au, dz, conv_w, conv_b)


HGRN_BLOCK = 512


def _hgrn_chunk(dot, cumsum, q, f, i, g, lb, ng, st):
    C = q.shape[0]
    forget = lb + (1.0 - lb) * _sigmoid(f)
    k = 1.0 - forget
    b = cumsum(jnp.log(forget))
    b_last = b[C - 1:C, :]
    qd = q * jnp.exp(b)
    kd = k * jnp.exp(-b)
    att = jnp.where(_tri(C), dot(qd, kd, 1, 1), 0.0)
    o = dot(att, i, 1, 0) + dot(qd, st, 1, 1)
    st_new = st * jnp.exp(b_last) + dot(i, k * jnp.exp(b_last - b), 0, 0)
    on = o * lax.rsqrt(jnp.mean(o * o, axis=-1, keepdims=True) + EPS) * ng
    return on * _silu(g), st_new


def _hgrn_specs(T, rev):
    tb = min(HGRN_BLOCK, T)
    nb = T // tb
    blk = (lambda n: nb - 1 - n) if rev else (lambda n: n)
    hd = HGRN_HEAD_DIM
    proj_specs = [pl.BlockSpec((tb, hd), lambda h, n, k=k: (blk(n), h + HGRN_HEADS * k)) for k in range(4)]
    vec_spec = pl.BlockSpec((1, hd), lambda h, n: (0, h))
    tok_spec = pl.BlockSpec((tb, hd), lambda h, n: (blk(n), h))
    st_spec = pl.BlockSpec((1, tb // HGRN_CHUNK, hd, hd), lambda h, n: (h, blk(n), 0, 0))
    return tb, nb, proj_specs, vec_spec, tok_spec, st_spec


def _hgrn_fwd(proj, lb, ng, *, name):
    T = proj.shape[0]
    tb, nb, proj_specs, vec_spec, tok_spec, st_spec = _hgrn_specs(T, False)
    nsub = tb // HGRN_CHUNK
    hd = HGRN_HEAD_DIM

    def body(q_ref, f_ref, i_ref, g_ref, lb_ref, ng_ref, y_ref, sts_ref, st_ref):
        @pl.when(pl.program_id(1) == 0)
        def _():
            st_ref[...] = jnp.zeros_like(st_ref)

        st = st_ref[...]
        for s in range(nsub):
            rows = slice(s * HGRN_CHUNK, (s + 1) * HGRN_CHUNK)
            sts_ref[0, s] = st
            y, st = _hgrn_chunk(_mxu, _cumsum_rows, q_ref[rows, :], f_ref[rows, :], i_ref[rows, :], g_ref[rows, :],
                                lb_ref[...], ng_ref[...], st)
            y_ref[rows, :] = y
        st_ref[...] = st

    return pl.pallas_call(
        body, name=name, grid=(HGRN_HEADS, nb),
        in_specs=proj_specs + [vec_spec, vec_spec], out_specs=[tok_spec, st_spec],
        out_shape=[jax.ShapeDtypeStruct((T, HGRN_DIM), F32),
                   jax.ShapeDtypeStruct((HGRN_HEADS, T // HGRN_CHUNK, hd, hd), F32)],
        scratch_shapes=[pltpu.VMEM((hd, hd), F32)],
        compiler_params=_cparams(("arbitrary", "arbitrary")),
    )(proj, proj, proj, proj, lb, ng)


def _hgrn_bwd(proj, lb, ng, states, dmix, *, name):
    T = proj.shape[0]
    tb, nb, proj_specs, vec_spec, tok_spec, st_spec = _hgrn_specs(T, True)
    nsub = tb // HGRN_CHUNK
    hd = HGRN_HEAD_DIM
    chunk = functools.partial(_hgrn_chunk, _mxu_ad, _cumsum_rows_ad)

    def body(q_ref, f_ref, i_ref, g_ref, lb_ref, ng_ref, sts_ref, dy_ref,
             dq_ref, df_ref, di_ref, dg_ref, dlb_ref, dng_ref, dst_ref):
        @pl.when(pl.program_id(1) == 0)
        def _():
            dst_ref[...] = jnp.zeros_like(dst_ref)
            dlb_ref[...] = jnp.zeros_like(dlb_ref)
            dng_ref[...] = jnp.zeros_like(dng_ref)

        dst = dst_ref[...]
        dlb = jnp.zeros((1, hd), F32)
        dng = jnp.zeros((1, hd), F32)
        for s in reversed(range(nsub)):
            rows = slice(s * HGRN_CHUNK, (s + 1) * HGRN_CHUNK)
            _, vjp = jax.vjp(chunk, q_ref[rows, :], f_ref[rows, :], i_ref[rows, :], g_ref[rows, :],
                             lb_ref[...], ng_ref[...], sts_ref[0, s])
            dq, df, di, dg, dlb_s, dng_s, dst = vjp((dy_ref[rows, :], dst))
            dq_ref[rows, :] = dq
            df_ref[rows, :] = df
            di_ref[rows, :] = di
            dg_ref[rows, :] = dg
            dlb += dlb_s
            dng += dng_s
        dst_ref[...] = dst
        dlb_ref[...] += dlb
        dng_ref[...] += dng

    tok_out = jax.ShapeDtypeStruct((T, HGRN_DIM), F32)
    vec_out = jax.ShapeDtypeStruct((1, HGRN_DIM), F32)
    return pl.pallas_call(
        body, name=name, grid=(HGRN_HEADS, nb),
        in_specs=proj_specs + [vec_spec, vec_spec, st_spec, tok_spec],
        out_specs=[tok_spec] * 4 + [vec_spec, vec_spec],
        out_shape=[tok_out] * 4 + [vec_out, vec_out],
        scratch_shapes=[pltpu.VMEM((hd, hd), F32)],
        compiler_params=_cparams(("arbitrary", "arbitrary")),
    )(proj, proj, proj, proj, lb, ng, states, dmix)


S5_LANES = 512
S5_ROWS = 512


def _cmul(ar, ai, br, bi):
    return ar * br - ai * bi, ar * bi + ai * br


def _power_table(ar, ai, exps):
    a2 = _cmul(ar, ai, ar, ai)
    a4 = _cmul(*a2, *a2)
    e = exps - 1
    pr = jnp.broadcast_to(ar, exps.shape)
    pi = jnp.broadcast_to(ai, exps.shape)
    for bit, (fr, fi) in enumerate(((ar, ai), a2, a4)):
        nr, ni = _cmul(pr, pi, fr, fi)
        on = ((e >> bit) & 1) == 1
        pr, pi = jnp.where(on, nr, pr), jnp.where(on, ni, pi)
    return pr, pi, a2, a4


def _s5_scan_fwd(bu, a_re, a_im, *, name):
    T = bu.shape[0]
    w, tr = S5_LANES, min(S5_ROWS, T)
    ncol, nt = S5_WIDTH // w, T // tr

    def body(br_ref, bi_ref, ar_ref, ai_ref, sr_ref, si_ref, carry_ref):
        @pl.when(pl.program_id(1) == 0)
        def _():
            carry_ref[...] = jnp.zeros_like(carry_ref)

        ar, ai = ar_ref[...], ai_ref[...]
        rowi = lax.broadcasted_iota(jnp.int32, (SUBLANE, w), 0)
        pr, pi, a2, a4 = _power_table(ar, ai, rowi + 1)

        def tile(i, carry):
            cr, ci = carry
            rows = pl.ds(pl.multiple_of(i * SUBLANE, SUBLANE), SUBLANE)
            xr, xi = br_ref[rows, :], bi_ref[rows, :]
            for s, (fr, fi) in ((1, (ar, ai)), (2, a2), (4, a4)):
                keep = rowi >= s
                zr = jnp.where(keep, pltpu.roll(xr, s, 0), 0.0)
                zi = jnp.where(keep, pltpu.roll(xi, s, 0), 0.0)
                xr, xi = xr + fr * zr - fi * zi, xi + fr * zi + fi * zr
            xr, xi = xr + pr * cr - pi * ci, xi + pr * ci + pi * cr
            sr_ref[rows, :] = xr
            si_ref[rows, :] = xi
            return xr[SUBLANE - 1:SUBLANE, :], xi[SUBLANE - 1:SUBLANE, :]

        cr, ci = lax.fori_loop(0, tr // SUBLANE, tile, (carry_ref[0:1, :], carry_ref[1:2, :]))
        carry_ref[0:1, :] = cr
        carry_ref[1:2, :] = ci

    out = jax.ShapeDtypeStruct((T, S5_WIDTH), F32)
    return pl.pallas_call(
        body, name=name, grid=(ncol, nt),
        in_specs=[pl.BlockSpec((tr, w), lambda j, t: (t, j)), pl.BlockSpec((tr, w), lambda j, t: (t, j + ncol)),
                  pl.BlockSpec((1, w), lambda j, t: (0, j)), pl.BlockSpec((1, w), lambda j, t: (0, j))],
        out_specs=[pl.BlockSpec((tr, w), lambda j, t: (t, j))] * 2,
        out_shape=[out, out],
        scratch_shapes=[pltpu.VMEM((2, w), F32)],
        compiler_params=_cparams(("parallel", "arbitrary")),
    )(bu, bu, a_re, a_im)


def _s5_scan_bwd(g_re, g_im, s_re, s_im, a_re, a_im, *, name):
    T = g_re.shape[0]
    w, tr = S5_LANES, min(S5_ROWS, T)
    ncol, nt = S5_WIDTH // w, T // tr
    ntile = tr // SUBLANE

    def body(gr_ref, gi_ref, sr_ref, si_ref, ar_ref, ai_ref, lr_ref, li_ref, dar_ref, dai_ref, carry_ref):
        @pl.when(pl.program_id(1) == 0)
        def _():
            carry_ref[...] = jnp.zeros_like(carry_ref)
            dar_ref[...] = jnp.zeros_like(dar_ref)
            dai_ref[...] = jnp.zeros_like(dai_ref)

        ar, ai = ar_ref[...], -ai_ref[...]
        rowi = lax.broadcasted_iota(jnp.int32, (SUBLANE, w), 0)
        pr, pi, a2, a4 = _power_table(ar, ai, SUBLANE - rowi)
        last = rowi == SUBLANE - 1

        def tile(i, carry):
            cr, ci, dar, dai = carry
            rows = pl.ds(pl.multiple_of((ntile - 1 - i) * SUBLANE, SUBLANE), SUBLANE)
            xr, xi = gr_ref[rows, :], gi_ref[rows, :]
            for s, (fr, fi) in ((1, (ar, ai)), (2, a2), (4, a4)):
                keep = rowi < SUBLANE - s
                zr = jnp.where(keep, pltpu.roll(xr, SUBLANE - s, 0), 0.0)
                zi = jnp.where(keep, pltpu.roll(xi, SUBLANE - s, 0), 0.0)
                xr, xi = xr + fr * zr - fi * zi, xi + fr * zi + fi * zr
            xr, xi = xr + pr * cr - pi * ci, xi + pr * ci + pi * cr
            lr_ref[rows, :] = xr
            li_ref[rows, :] = xi
            nr = jnp.where(last, cr, pltpu.roll(xr, SUBLANE - 1, 0))
            ni = jnp.where(last, ci, pltpu.roll(xi, SUBLANE - 1, 0))
            sr, si = sr_ref[rows, :], si_ref[rows, :]
            return xr[0:1, :], xi[0:1, :], dar + nr * sr + ni * si, dai + ni * sr - nr * si

        cr, ci, dar, dai = lax.fori_loop(
            0, ntile, tile, (carry_ref[0:1, :], carry_ref[1:2, :], jnp.zeros((SUBLANE, w), F32), jnp.zeros((SUBLANE, w), F32)))
        carry_ref[0:1, :] = cr
        carry_ref[1:2, :] = ci
        dar_ref[...] += dar
        dai_ref[...] += dai

    tok = pl.BlockSpec((tr, w), lambda j, t: (nt - 1 - t, j))
    vec = pl.BlockSpec((1, w), lambda j, t: (0, j))
    acc = pl.BlockSpec((SUBLANE, w), lambda j, t: (0, j))
    out = jax.ShapeDtypeStruct((T, S5_WIDTH), F32)
    accs = jax.ShapeDtypeStruct((SUBLANE, S5_WIDTH), F32)
    return pl.pallas_call(
        body, name=name, grid=(ncol, nt),
        in_specs=[tok, tok, tok, tok, vec, vec], out_specs=[tok, tok, acc, acc],
        out_shape=[out, out, accs, accs],
        scratch_shapes=[pltpu.VMEM((2, w), F32)],
        compiler_params=_cparams(("parallel", "arbitrary")),
    )(g_re, g_im, s_re, s_im, a_re, a_im)


ATTN_BLOCK = 512
_NEG = -1e30


def _attn_scores(qn, qr, kn, kr, q0, k0):
    s = (_mxu(qn, kn, 1, 1) + _mxu(qr, kr, 1, 1)) * (MLA_QK ** -0.5)
    qpos = q0 + lax.broadcasted_iota(jnp.int32, s.shape, 0)
    kpos = k0 + lax.broadcasted_iota(jnp.int32, s.shape, 1)
    return s, kpos <= qpos


def _attn_fwd(q_all, q_rope, kv, k_rope, *, name):
    T = q_all.shape[0]
    tq = min(ATTN_BLOCK, T)
    nq = T // tq

    def body(qn_ref, qr_ref, kn_ref, v_ref, kr_ref, o_ref, lse_ref):
        i = pl.program_id(1)
        qn, qr = qn_ref[...], qr_ref[0]

        def step(j, carry):
            m, l, acc = carry
            ks = pl.ds(pl.multiple_of(j * tq, tq), tq)
            s, ok = _attn_scores(qn, qr, kn_ref[ks, :], kr_ref[ks, :], i * tq, j * tq)
            s = jnp.where(ok, s, _NEG)
            m_new = jnp.maximum(m, jnp.max(s, axis=-1, keepdims=True))
            p = jnp.exp(s - m_new)
            alpha = jnp.exp(m - m_new)
            return m_new, alpha * l + jnp.sum(p, axis=-1, keepdims=True), alpha * acc + _mxu(p, v_ref[ks, :], 1, 0)

        m, l, acc = lax.fori_loop(0, i + 1, step, (jnp.full((tq, 1), _NEG, F32), jnp.zeros((tq, 1), F32), jnp.zeros((tq, MLA_V), F32)))
        o_ref[...] = acc / l
        lse_ref[0] = m + jnp.log(l)

    return pl.pallas_call(
        body, name=name, grid=(MLA_HEADS, nq),
        in_specs=[pl.BlockSpec((tq, MLA_NOPE), lambda h, i: (i, h)), pl.BlockSpec((1, tq, MLA_ROPE), lambda h, i: (h, i, 0)),
                  pl.BlockSpec((T, MLA_NOPE), lambda h, i: (0, 2 * h)), pl.BlockSpec((T, MLA_V), lambda h, i: (0, 2 * h + 1)),
                  pl.BlockSpec((T, MLA_ROPE), lambda h, i: (0, 0))],
        out_specs=[pl.BlockSpec((tq, MLA_V), lambda h, i: (i, h)), pl.BlockSpec((1, tq, 1), lambda h, i: (h, i, 0))],
        out_shape=[jax.ShapeDtypeStruct((T, MLA_HEADS * MLA_V), F32), jax.ShapeDtypeStruct((MLA_HEADS, T, 1), F32)],
        compiler_params=_cparams(("arbitrary", "arbitrary")),
    )(q_all, q_rope, kv, kv, k_rope)


def _attn_bwd(q_all, q_rope, kv, k_rope, o, lse, do, *, name):
    T = q_all.shape[0]
    tk = min(ATTN_BLOCK, T)
    nk = T // tk
    scale = MLA_QK ** -0.5

    def body(qn_ref, qr_ref, kv_ref, kr_ref, o_ref, lse_ref, do_ref, dqn_ref, dqr_ref, dkv_ref, dkr_ref):
        h, j = pl.program_id(0), pl.program_id(1)

        @pl.when(j == 0)
        def _():
            dqn_ref[...] = jnp.zeros_like(dqn_ref)
            dqr_ref[...] = jnp.zeros_like(dqr_ref)

        @pl.when((j == 0) & (h == 0))
        def _():
            dkr_ref[...] = jnp.zeros_like(dkr_ref)

        krows = pl.ds(pl.multiple_of(j * tk, tk), tk)
        kn, v, kr = kv_ref[:, :MLA_NOPE], kv_ref[:, MLA_NOPE:], kr_ref[krows, :]

        def step(i, carry):
            dkn, dv, dkr = carry
            qs = pl.ds(pl.multiple_of(i * tk, tk), tk)
            qn, qr, dob = qn_ref[qs, :], qr_ref[0, qs, :], do_ref[qs, :]
            s, ok = _attn_scores(qn, qr, kn, kr, i * tk, j * tk)
            p = jnp.where(ok, jnp.exp(s - lse_ref[0, qs, :]), 0.0)
            dp = _mxu(dob, v, 1, 1)
            delta = jnp.sum(dob * o_ref[qs, :], axis=-1, keepdims=True)
            ds = p * (dp - delta) * scale
            dqn_ref[qs, :] += _mxu(ds, kn, 1, 0)
            dqr_ref[0, qs, :] += _mxu(ds, kr, 1, 0)
            return dkn + _mxu(ds, qn, 0, 0), dv + _mxu(p, dob, 0, 0), dkr + _mxu(ds, qr, 0, 0)

        dkn, dv, dkr = lax.fori_loop(j, nk, step, (jnp.zeros((tk, MLA_NOPE), F32), jnp.zeros((tk, MLA_V), F32), jnp.zeros((tk, MLA_ROPE), F32)))
        dkv_ref[:, :MLA_NOPE] = dkn
        dkv_ref[:, MLA_NOPE:] = dv
        dkr_ref[krows, :] += dkr

    head_cols = pl.BlockSpec((T, MLA_NOPE), lambda h, j: (0, h))
    head_rope = pl.BlockSpec((1, T, MLA_ROPE), lambda h, j: (h, 0, 0))
    kv_spec = pl.BlockSpec((tk, MLA_NOPE + MLA_V), lambda h, j: (j, h))
    kr_spec = pl.BlockSpec((T, MLA_ROPE), lambda h, j: (0, 0))
    return pl.pallas_call(
        body, name=name, grid=(MLA_HEADS, nk),
        in_specs=[head_cols, head_rope, kv_spec, kr_spec, head_cols, pl.BlockSpec((1, T, 1), lambda h, j: (h, 0, 0)), head_cols],
        out_specs=[head_cols, head_rope, kv_spec, kr_spec],
        out_shape=[jax.ShapeDtypeStruct((T, MLA_HEADS * MLA_NOPE), F32), jax.ShapeDtypeStruct((MLA_HEADS, T, MLA_ROPE), F32),
                   jax.ShapeDtypeStruct((T, MLA_HEADS * (MLA_NOPE + MLA_V)), F32), jax.ShapeDtypeStruct((T, MLA_ROPE), F32)],
        compiler_params=_cparams(("arbitrary", "arbitrary")),
    )(q_all, q_rope, kv, k_rope, o, lse, do)


def _s5_discretize(a_re, a_im, log_dt, bt_re, bt_im, lb_logits):
    dt = jnp.exp(log_dt)
    mag = jnp.exp(a_re * dt)
    abr, abi = mag * jnp.cos(a_im * dt), mag * jnp.sin(a_im * dt)
    den = a_re * a_re + a_im * a_im
    xr, xi = abr - 1.0, abi
    cr = ((xr * a_re + xi * a_im) / den)[:, None, :]
    ci = ((xi * a_re - xr * a_im) / den)[:, None, :]
    e = jnp.exp(lb_logits - jnp.max(lb_logits, axis=0, keepdims=True))
    lb = e[0:1, :] / jnp.sum(e, axis=0, keepdims=True)
    return abr, abi, cr * bt_re - ci * bt_im, cr * bt_im + ci * bt_re, lb


def _whole(shape):
    return pl.BlockSpec(shape, lambda: (0,) * len(shape))


def _s5_params_fwd(a_re, a_im, log_dt, bt_re, bt_im, lb_logits):
    ins = (a_re, a_im, log_dt, bt_re, bt_im, lb_logits)
    outs = [jax.ShapeDtypeStruct(s, F32) for s in (a_re.shape, a_re.shape, bt_re.shape, bt_re.shape, (1, lb_logits.shape[1]))]

    def body(*refs):
        res = _s5_discretize(*[r[...] for r in refs[:6]])
        for o_ref, o in zip(refs[6:], res):
            o_ref[...] = o

    return pl.pallas_call(body, name="s5_params_fwd", in_specs=[_whole(a.shape) for a in ins],
                          out_specs=[_whole(o.shape) for o in outs], out_shape=outs, compiler_params=_cparams())(*ins)


def _s5_params_bwd(a_re, a_im, log_dt, bt_re, bt_im, lb_logits, d_abr, d_abi, d_bbr, d_bbi, d_lb):
    ins = (a_re, a_im, log_dt, bt_re, bt_im, lb_logits, d_abr, d_abi, d_bbr, d_bbi, d_lb)
    outs = [jax.ShapeDtypeStruct(a.shape, F32) for a in ins[:6]]

    def body(*refs):
        _, vjp = jax.vjp(_s5_discretize, *[r[...] for r in refs[:6]])
        for o_ref, o in zip(refs[11:], vjp(tuple(r[...] for r in refs[6:11]))):
            o_ref[...] = o

    return pl.pallas_call(body, name="s5_params_bwd", in_specs=[_whole(a.shape) for a in ins],
                          out_specs=[_whole(o.shape) for o in outs], out_shape=outs, compiler_params=_cparams())(*ins)


def _adamw(w, g, m, v, *, name):
    R, C = w.shape
    tr = _pick(R, (256, 128, 64, 32, 16, 8)) if R % SUBLANE == 0 else R
    slabs = g.shape[0] if g.ndim == 3 else 0

    def body(w_ref, g_ref, m_ref, v_ref, *outs):
        if slabs:
            gv = g_ref[0].astype(F32)
            for s in range(1, slabs):
                gv = gv + g_ref[s].astype(F32)
            outs[0][...] = gv
            outs = outs[1:]
        else:
            gv = g_ref[...]
        d_ref, mo_ref, vo_ref = outs
        m2 = ADAM_B1 * m_ref[...] + (1.0 - ADAM_B1) * gv
        v2 = ADAM_B2 * v_ref[...] + (1.0 - ADAM_B2) * (gv * gv)
        m_hat = m2 / (1.0 - ADAM_B1 ** ADAM_STEP)
        v_hat = v2 / (1.0 - ADAM_B2 ** ADAM_STEP)
        d_ref[...] = -ADAM_LR * (m_hat / (jnp.sqrt(v_hat) + ADAM_EPS) + ADAM_WD * w_ref[...])
        mo_ref[...] = m2
        vo_ref[...] = v2

    spec = pl.BlockSpec((tr, C), lambda i: (i, 0))
    g_spec = pl.BlockSpec((slabs, tr, C), lambda i: (0, i, 0)) if slabs else spec
    out = jax.ShapeDtypeStruct((R, C), F32)
    n_out = 4 if slabs else 3
    return pl.pallas_call(body, name=name, grid=(R // tr,), in_specs=[spec, g_spec, spec, spec], out_specs=[spec] * n_out,
                          out_shape=[out] * n_out, compiler_params=_cparams(("parallel",)))(w, g, m, v)


N_CHIPS = 4
N_CORES = 2


_FLIPS = tuple((dx, dy, dc) for dx in (0, 1) for dy in (0, 1) for dc in (0, 1) if (dx, dy, dc) != (0, 0, 0))


_HBM = pl.BlockSpec(memory_space=pltpu.HBM)
_SEM = pl.BlockSpec(memory_space=pltpu.SEMAPHORE)
_SPLIT_COPY = pltpu.CompilerParams(has_side_effects=pltpu.SideEffectType.DATAFLOW_SIDE_EFFECTING)


def _exchange_copies(src_refs, land_refs, send_sems, recv_sems, scatter, arriving):
    x, y, c = lax.axis_index("x"), lax.axis_index("y"), lax.axis_index("c")
    me_chip = 2 * x + y
    copies = []
    for a, (s_ref, l_ref) in enumerate(zip(src_refs, land_refs)):
        for j, (dx, dy, dc) in enumerate(_FLIPS):
            px, py, pc = (1 - x if dx else x), (1 - y if dy else y), (1 - c if dc else c)
            k = a * len(_FLIPS) + j
            p_chip = 2 * px + py
            copies.append(pltpu.make_async_remote_copy(
                src_ref=s_ref.at[p_chip, pc] if scatter else s_ref, dst_ref=l_ref.at[p_chip, pc] if arriving else l_ref.at[me_chip, c],
                send_sem=send_sems.at[k], recv_sem=recv_sems.at[k], device_id=(px, py, pc), device_id_type=MESH))
    return copies


def _exchange_start(srcs, *, scatter, name, after=()):
    n_arr = len(srcs)
    n_sem = n_arr * len(_FLIPS)
    n_in = 2 * n_arr + len(after)
    lands = [lax.empty(s.shape if scatter else (N_CHIPS, N_CORES) + s.shape, s.dtype) for s in srcs]

    def body(*refs):
        src_refs, land_refs = refs[:n_arr], refs[n_arr:2 * n_arr]
        for cp in _exchange_copies(src_refs, land_refs, refs[n_in], refs[n_in + 1], scatter, arriving=False):
            cp.start()
        refs[-1][...] = jnp.zeros_like(refs[-1])

    thru = [pltpu.HBM(a.shape, a.dtype) for a in srcs + lands]
    outs = pl.pallas_call(
        body, name=name,
        out_shape=(pltpu.SemaphoreType.DMA((n_sem,)), pltpu.SemaphoreType.DMA((n_sem,)), *thru,
                   jax.ShapeDtypeStruct((SUBLANE, LANE), F32)),
        in_specs=[_HBM] * (2 * n_arr) + [pl.BlockSpec(memory_space=pl.ANY)] * len(after),
        out_specs=(_SEM, _SEM, *[_HBM] * (2 * n_arr), pl.BlockSpec(memory_space=pltpu.VMEM)),
        input_output_aliases={i: 2 + i for i in range(2 * n_arr)}, compiler_params=_SPLIT_COPY,
    )(*[pltpu.with_memory_space_constraint(a, pltpu.HBM) for a in srcs + lands], *after)
    return outs[0], outs[1], list(outs[2:2 + n_arr]), list(outs[2 + n_arr:2 + 2 * n_arr]), outs[-1]


def _exchange_wait(started, after, *, scatter, name):
    send_sems, recv_sems, srcs, lands, _ = started
    n_arr = len(srcs)

    def body(*refs):
        src_refs, land_refs = refs[:n_arr], refs[n_arr:2 * n_arr]
        for cp in _exchange_copies(src_refs, land_refs, refs[2 * n_arr], refs[2 * n_arr + 1], scatter, arriving=True):
            cp.wait_send()
            cp.wait_recv()

    outs = pl.pallas_call(
        body, name=name, out_shape=[pltpu.HBM(a.shape, a.dtype) for a in srcs + lands],
        in_specs=[_HBM] * (2 * n_arr) + [_SEM, _SEM, pl.BlockSpec(memory_space=pl.ANY)], out_specs=[_HBM] * (2 * n_arr),
        input_output_aliases={i: i for i in range(2 * n_arr)}, compiler_params=_SPLIT_COPY,
    )(*srcs, *lands, send_sems, recv_sems, after)
    return list(outs[:n_arr]), list(outs[n_arr:])


def _with_own(land, own):
    me_chip = 2 * lax.axis_index("x") + lax.axis_index("y")
    return lax.dynamic_update_slice(land, own[None, None], (me_chip, lax.axis_index("c")) + (0,) * own.ndim)


def _sum_slabs(buf, out_dtype, *, name):
    G, R, C = buf.shape
    tr = _pick(R, (512, 256, 128, 64, 32, 16, 8))

    def body(b_ref, o_ref):
        acc = b_ref[0].astype(F32)
        for s in range(1, G):
            acc = acc + b_ref[s].astype(F32)
        o_ref[...] = acc.astype(out_dtype)

    return pl.pallas_call(body, name=name, grid=(R // tr,), in_specs=[pl.BlockSpec((G, tr, C), lambda i: (0, i, 0))],
                          out_specs=pl.BlockSpec((tr, C), lambda i: (i, 0)), out_shape=jax.ShapeDtypeStruct((R, C), out_dtype),
                          compiler_params=_cparams(("parallel",)))(buf)


def _rms_fwd_fn(h, g):
    return (_rms(h, g),)


def _rms_bwd_fn(h, dhn, dres, g):
    _, vjp = jax.vjp(_rms, h, g)
    dh, dg = vjp(dhn)
    return dh + dres, dg


def _loss_fn(h, tgt, g):
    y, vjp = jax.vjp(_rms, h, g)
    diff = y - tgt
    dh, dg = vjp(diff * (1.0 / D_MODEL))
    return dh, dg, (0.5 / D_MODEL) * jnp.sum(diff * diff, axis=0, keepdims=True)


def _s5_act(ys, u, d):
    return _gelu(ys + d * u)


def _s5_gate(z, gl, b):
    return z * _sigmoid(gl + b)


def _s5_act_fn(ys, u, d):
    return (_s5_act(ys, u, d),)


def _s5_mix_fn(ya, z, gl, b):
    return (jnp.concatenate([ya, _s5_gate(z, gl, b)], axis=1),)


def _s5_gate_bwd_fn(z, gl, dyb, b):
    _, vjp = jax.vjp(_s5_gate, z, gl, b)
    return vjp(dyb)


def _s5_act_bwd_fn(ys, u, dz1, dz2, d):
    _, vjp = jax.vjp(_s5_act, ys, u, d)
    return vjp(dz1 + dz2)


def _dproj_fn(dq, df, di, dg, du1, du2):
    return (jnp.concatenate([dq, df, di, dg, du1 + du2], axis=1),)


def _rope_pair(r1, r2, pos, freqs):
    ang = pos.astype(F32) * freqs
    c, s = jnp.cos(ang), jnp.sin(ang)
    return r1 * c - r2 * s, r1 * s + r2 * c


_ODD_SPLITS = (0, MLA_Q_RANK, MLA_Q_RANK + MLA_KV_RANK, MLA_Q_RANK + MLA_KV_RANK + LANE, ODD_IN_PAD)


def _mla_prep(cq, ckv, k1, k2, qg, kvg, pos, freqs):
    ko1, ko2 = _rope_pair(k1, k2, pos, freqs)
    return _rms(cq, qg), _rms(ckv, kvg), ko1, ko2


def _mla_prep_fn(proj, pos, qg, kvg, freqs):
    parts = [proj[:, a:b] for a, b in zip(_ODD_SPLITS[:-1], _ODD_SPLITS[1:])]
    return _mla_prep(*parts, qg, kvg, pos, freqs)


def _mla_prep_bwd_fn(proj, pos, dqn, dkvn, dko1, dko2, qg, kvg, freqs):
    parts = [proj[:, a:b] for a, b in zip(_ODD_SPLITS[:-1], _ODD_SPLITS[1:])]
    _, vjp = jax.vjp(lambda *a: _mla_prep(*a, pos, freqs), *parts, qg, kvg)
    dcq, dckv, dk1, dk2, dqg, dkvg = vjp((dqn, dkvn, dko1, dko2))
    return jnp.concatenate([dcq, dckv, dk1, dk2], axis=1), dqg, dkvg


def _rope_q_fn(r1, r2, pos, freqs):
    return _rope_pair(r1, r2, pos, freqs)


def _rope_q_bwd_fn(dqn, do1, do2, pos, freqs):
    dr1, dr2 = _rope_pair(do1, do2, pos, -freqs)
    return (jnp.concatenate([dqn, dr1, dr2], axis=1),)


W_NAMES = ("norm_mix_g", "norm_ffn_g", "final_norm_g", "even_w_in", "hgrn_lb_logits", "hgrn_norm_g", "s5_a_re", "s5_a_im",
           "s5_log_dt", "s5_b_re", "s5_b_im", "s5_c_re", "s5_c_im", "s5_d", "s5_w_glu", "s5_b_glu", "even_w_out", "odd_w_in",
           "mla_q_norm_g", "mla_w_uq", "mla_kv_norm_g", "mla_w_ukv", "odd_w_out", "ffn_w_in", "ffn_conv_w", "ffn_conv_b",
           "ffn_w_out")
BIG_UNITS = (("even_w_in", 0, "col"), ("s5_w_glu", 0, "row"), ("even_w_out", 0, "row"), ("odd_w_in", 0, "row"),
             ("mla_w_uq", 0, "col"), ("mla_w_ukv", 0, "col"), ("odd_w_out", 0, "row"),
             ("ffn_w_in", 0, "col"), ("ffn_w_in", 1, "col"), ("ffn_w_out", 0, "row"), ("ffn_w_out", 1, "row"))
BIG_NAMES = tuple(dict.fromkeys(u[0] for u in BIG_UNITS))
SMALL_SHARDED = (("mla_q_norm_g", 1), ("mla_kv_norm_g", 1), ("ffn_conv_w", 2))
SMALL_SHARDED_NAMES = tuple(s[0] for s in SMALL_SHARDED)
REPLICATED = tuple(n for n in W_NAMES if n not in BIG_NAMES + SMALL_SHARDED_NAMES)


def _pack(flats, cols, row_mult):
    flat = jnp.concatenate(flats, axis=-1)
    pad = (-flat.shape[-1]) % (cols * row_mult)
    flat = jnp.pad(flat, [(0, 0)] * (flat.ndim - 1) + [(0, pad)])
    return flat.reshape(flat.shape[:-1] + (-1, cols))


def _unpack(flat, shapes):
    out, off = [], 0
    for shp in shapes:
        n = int(np.prod(shp))
        out.append(flat[..., off:off + n].reshape(flat.shape[:-1] + tuple(shp)))
        off += n
    return out


UNIT_KIND = {(n, l): kind for n, l, kind in BIG_UNITS}
STAGES = ((("even_w_in", 0), ("s5_w_glu", 0), ("even_w_out", 0)),
          (("ffn_w_in", 0), ("ffn_w_out", 0)),
          (("odd_w_in", 0), ("mla_w_uq", 0), ("mla_w_ukv", 0), ("odd_w_out", 0)),
          (("ffn_w_in", 1), ("ffn_w_out", 1)))


def _gather_start(w, stage, with_small, after):
    srcs = [w[n][l].astype(BF16) for n, l in STAGES[stage]]
    if with_small:
        srcs.append(_pack([w[n].reshape(-1) for n in SMALL_SHARDED_NAMES], LANE, SUBLANE))
    return _exchange_start(srcs, scatter=False, name=f"gather_start_{stage}", after=after)


def _gather_finish(started, after, w, stage, with_small):
    srcs, lands = _exchange_wait(started, after, scatter=False, name=f"gather_wait_{stage}")
    lands = [_with_own(land, src) for land, src in zip(lands, srcs)]
    big = {}
    for unit, g in zip(STAGES[stage], lands):
        r, c = g.shape[2:]
        big[unit] = g.reshape(N_DEV * r, c) if UNIT_KIND[unit] == "row" else g.transpose(2, 0, 1, 3).reshape(r, N_DEV * c)
    if not with_small:
        return big
    parts = _unpack(lands[-1].reshape(N_DEV, -1), [w[n].shape for n in SMALL_SHARDED_NAMES])
    small = {}
    for (n, ax), p in zip(SMALL_SHARDED, parts):
        shp = list(w[n].shape)
        shp[ax] *= N_DEV
        small[n] = jnp.moveaxis(p, 0, ax).reshape(shp)
    return big, small


def _scatter_start(g_big, stage, extra=()):
    srcs = []
    for unit in STAGES[stage]:
        g = g_big[unit].astype(BF16)
        if UNIT_KIND[unit] == "row":
            srcs.append(g.reshape(N_CHIPS, N_CORES, g.shape[0] // N_DEV, g.shape[1]))
        else:
            srcs.append(g.reshape(g.shape[0], N_CHIPS, N_CORES, g.shape[1] // N_DEV).transpose(1, 2, 0, 3))
    return _exchange_start(srcs + list(extra), scatter=True, name=f"scatter_start_{stage}")


def _scatter_finish(started, after, stage):
    srcs, lands = _exchange_wait(started, after, scatter=True, name=f"scatter_wait_{stage}")
    me_chip, c = 2 * lax.axis_index("x") + lax.axis_index("y"), lax.axis_index("c")
    outs = []
    for land, src in zip(lands, srcs):
        own = lax.dynamic_slice(src, (me_chip, c) + (0,) * (src.ndim - 2), (1, 1) + src.shape[2:])[0, 0]
        outs.append(_with_own(land, own).reshape((N_DEV,) + land.shape[2:]))
    return outs


def _small_grad_packs(g_small, g_repl, w):
    flats = []
    for n, ax in SMALL_SHARDED:
        shp = list(w[n].shape)
        g = g_small[n].astype(F32).reshape(shp[:ax] + [N_DEV] + shp[ax:])
        flats.append(jnp.moveaxis(g, ax, 0).reshape(N_DEV, -1))
    small = _pack(flats, LANE, SUBLANE)
    vec = _pack([g_repl[n].reshape(-1).astype(F32) for n in REPLICATED], LANE, SUBLANE)
    return small.reshape((N_CHIPS, N_CORES) + small.shape[1:]), jnp.broadcast_to(vec, (N_CHIPS, N_CORES) + vec.shape)


def _block_diag(blocks):
    G, a, b = blocks.shape
    return jnp.einsum('gab,gk->gakb', blocks, jnp.eye(G, dtype=blocks.dtype)).reshape(G * a, G * b)


def _diag_blocks(mat, a, b):
    G = mat.shape[0] // a
    return jnp.einsum('gagb->gab', mat.reshape(G, a, G, b))


def _ffn_fwd(h, g, w_in, conv_w, conv_b, w_out, tag):
    hn, = _rows(_rms_fwd_fn, [h], [g], [(D_MODEL, BF16)], [], name=f"ffn{tag}_norm")
    au = _mm(hn, w_in, name=f"ffn{tag}_in")
    z = _ffn_mid_fwd(au, conv_w, conv_b, name=f"ffn{tag}_mid")
    return _mm(z, w_out, res=h, name=f"ffn{tag}_out"), (hn, au, z)


def _ffn_bwd(h, dh, saved, g, w_in, conv_w, conv_b, w_out, tag, deps=()):
    hn, au, z = saved
    dz = _mm(dh, w_out, tb=True, deps=deps, name=f"ffn{tag}_dz")
    dw_out = _mm(z, dh, ta=True, out_dtype=BF16, name=f"ffn{tag}_dwout")
    da, du, dcw, dcb = _ffn_mid_bwd(au, dz, conv_w, conv_b, name=f"ffn{tag}_dmid")
    dhn = _mm(da, w_in, tb=True, b_cols=(0, D_FF), name=f"ffn{tag}_dhn_a")
    dhn = _mm(du, w_in, tb=True, b_cols=(D_FF, D_FF), res=dhn, name=f"ffn{tag}_dhn_u")
    dw_in = jnp.concatenate([_mm(hn, da, ta=True, out_dtype=BF16, name=f"ffn{tag}_dwin_a"),
                             _mm(hn, du, ta=True, out_dtype=BF16, name=f"ffn{tag}_dwin_u")], axis=1)
    dh_in, dg = _rows(_rms_bwd_fn, [h, dhn, dh], [g], [(D_MODEL, F32)], [(1, D_MODEL)], name=f"ffn{tag}_dnorm")
    return dh_in, dict(g=dg, w_in=dw_in, conv_w=dcw, conv_b=dcb, w_out=dw_out)


def kernel(x, positions, norm_mix_g, norm_ffn_g, final_norm_g, even_w_in, hgrn_lb_logits, hgrn_norm_g, s5_a_re, s5_a_im, s5_log_dt, s5_b_re, s5_b_im, s5_c_re, s5_c_im, s5_d, s5_w_glu, s5_b_glu, even_w_out, odd_w_in, mla_q_norm_g, mla_w_uq, mla_kv_norm_g, mla_w_ukv, odd_w_out, ffn_w_in, ffn_conv_w, ffn_conv_b, ffn_w_out, loss_target, m_norm_mix_g, m_norm_ffn_g, m_final_norm_g, m_even_w_in, m_hgrn_lb_logits, m_hgrn_norm_g, m_s5_a_re, m_s5_a_im, m_s5_log_dt, m_s5_b_re, m_s5_b_im, m_s5_c_re, m_s5_c_im, m_s5_d, m_s5_w_glu, m_s5_b_glu, m_even_w_out, m_odd_w_in, m_mla_q_norm_g, m_mla_w_uq, m_mla_kv_norm_g, m_mla_w_ukv, m_odd_w_out, m_ffn_w_in, m_ffn_conv_w, m_ffn_conv_b, m_ffn_w_out, v_norm_mix_g, v_norm_ffn_g, v_final_norm_g, v_even_w_in, v_hgrn_lb_logits, v_hgrn_norm_g, v_s5_a_re, v_s5_a_im, v_s5_log_dt, v_s5_b_re, v_s5_b_im, v_s5_c_re, v_s5_c_im, v_s5_d, v_s5_w_glu, v_s5_b_glu, v_even_w_out, v_odd_w_in, v_mla_q_norm_g, v_mla_w_uq, v_mla_kv_norm_g, v_mla_w_ukv, v_odd_w_out, v_ffn_w_in, v_ffn_conv_w, v_ffn_conv_b, v_ffn_w_out):
    given = dict(locals())
    w = {n: given[n] for n in W_NAMES}
    mom = {n: given["m_" + n] for n in W_NAMES}
    var = {n: given["v_" + n] for n in W_NAMES}
    T = x.shape[1]
    h0 = x[0]
    tgt = loss_target[0]
    pos = positions.reshape(T, 1)

    gathers = []
    for s in range(len(STAGES)):
        gathers.append(_gather_start(w, s, with_small=(s == 0), after=[g[4] for g in gathers[-1:]]))
    half = MLA_ROPE // 2
    kr0 = MLA_Q_RANK + MLA_KV_RANK
    freqs = ROPE_THETA ** (-jnp.arange(0, MLA_ROPE, 2, dtype=F32) / MLA_ROPE)
    freqs_q = jnp.tile(freqs, MLA_HEADS)[None, :]
    freqs_k = jnp.concatenate([freqs, jnp.zeros((LANE - half,), F32)])[None, :]

    sp_in = (s5_a_re[0], s5_a_im[0], s5_log_dt[0][:, None], s5_b_re[0].transpose(0, 2, 1), s5_b_im[0].transpose(0, 2, 1),
             hgrn_lb_logits)
    abr, abi, bbt_re, bbt_im, lb0 = _s5_params_fwd(*sp_in)
    a_re, a_im = abr.reshape(1, S5_WIDTH), abi.reshape(1, S5_WIDTH)
    bb_re, bb_im = _block_diag(bbt_re).astype(BF16), _block_diag(bbt_im).astype(BF16)
    bb_cat = jnp.concatenate([bb_re, bb_im], axis=1)
    c_re = _block_diag(s5_c_re[0].transpose(0, 2, 1)).astype(BF16)
    c_im_neg = _block_diag(-s5_c_im[0].transpose(0, 2, 1)).astype(BF16)
    u_cols = (4 * HGRN_DIM, S5_DIM)

    hn0, = _rows(_rms_fwd_fn, [h0], [norm_mix_g[0:1]], [(D_MODEL, BF16)], [], name="mix0_norm", deps=[gathers[-1][4]])
    full, full_small = _gather_finish(gathers[0], hn0, w, 0, True)
    w_ein, w_glu, w_eout = full["even_w_in", 0], full["s5_w_glu", 0], full["even_w_out", 0]
    qg, kvg, conv_w = full_small["mla_q_norm_g"], full_small["mla_kv_norm_g"], full_small["ffn_conv_w"]
    proj = _mm(hn0, w_ein, name="even_in")
    y_a, states = _hgrn_fwd(proj, lb0, hgrn_norm_g, name="hgrn_fwd")
    bu = _mm(proj, bb_cat, a_cols=u_cols, name="s5_bu")
    s_re, s_im = _s5_scan_fwd(bu, a_re, a_im, name="s5_scan_fwd")
    ys = _mm(s_im, c_im_neg, res=_mm(s_re, c_re, name="s5_y_re"), name="s5_y_im")
    z5, = _rows(_s5_act_fn, [ys, (proj,) + u_cols], [s5_d], [(S5_DIM, F32)], [], name="s5_act")
    gl = _mm(z5, w_glu, name="s5_glu")
    mixin, = _rows(_s5_mix_fn, [y_a, z5, gl], [s5_b_glu], [(D_MODEL, BF16)], [], name="s5_mix")
    h1 = _mm(mixin, w_eout, res=h0, name="even_out")
    full.update(_gather_finish(gathers[1], h1, w, 1, False))
    w_fin, w_fout = [full["ffn_w_in", 0]], [full["ffn_w_out", 0]]
    h2, ffn0_saved = _ffn_fwd(h1, norm_ffn_g[0:1], w_fin[0], conv_w[0], ffn_conv_b[0:1], w_fout[0], 0)

    full.update(_gather_finish(gathers[2], h2, w, 2, False))
    w_oin, w_ukv, w_oout = full["odd_w_in", 0], full["mla_w_ukv", 0], full["odd_w_out", 0]
    zpad = jnp.zeros((D_MODEL, LANE - half), BF16)
    w_oin_pad = jnp.concatenate([w_oin[:, :kr0], w_oin[:, kr0:kr0 + half], zpad, w_oin[:, kr0 + half:], zpad], axis=1)
    w_uq3 = full["mla_w_uq", 0].reshape(MLA_Q_RANK, MLA_HEADS, MLA_QK)
    w_uq_perm = jnp.concatenate([w_uq3[:, :, :MLA_NOPE].reshape(MLA_Q_RANK, -1),
                                 w_uq3[:, :, MLA_NOPE:MLA_NOPE + half].reshape(MLA_Q_RANK, -1),
                                 w_uq3[:, :, MLA_NOPE + half:].reshape(MLA_Q_RANK, -1)], axis=1)
    hn1, = _rows(_rms_fwd_fn, [h2], [norm_mix_g[1:2]], [(D_MODEL, BF16)], [], name="mix1_norm")
    proj_o = _mm(hn1, w_oin_pad, name="odd_in")
    qn, kvn, ko1, ko2 = _rows(_mla_prep_fn, [proj_o, pos], [qg, kvg, freqs_k],
                              [(MLA_Q_RANK, BF16), (MLA_KV_RANK, BF16), (LANE, F32), (LANE, F32)], [], name="mla_prep")
    q_all = _mm(qn, w_uq_perm, name="mla_uq")
    kv = _mm(kvn, w_ukv, name="mla_ukv")
    nope_w = MLA_HEADS * MLA_NOPE
    rope_w = MLA_HEADS * half
    o1, o2 = _rows(_rope_q_fn, [(q_all, nope_w, rope_w), (q_all, nope_w + rope_w, rope_w), pos], [freqs_q],
                   [(rope_w, F32), (rope_w, F32)], [], name="mla_rope_q")
    q_rope = jnp.concatenate([o1.reshape(T, MLA_HEADS, half), o2.reshape(T, MLA_HEADS, half)], axis=2).transpose(1, 0, 2)
    k_rope = jnp.concatenate([ko1[:, :half], ko2[:, :half]], axis=1)
    o, lse = _attn_fwd(q_all, q_rope, kv, k_rope, name="attn_fwd")
    h3 = _mm(o, w_oout, res=h2, name="odd_out")
    full.update(_gather_finish(gathers[3], h3, w, 3, False))
    w_fin.append(full["ffn_w_in", 1])
    w_fout.append(full["ffn_w_out", 1])
    h4, ffn1_saved = _ffn_fwd(h3, norm_ffn_g[1:2], w_fin[1], conv_w[1], ffn_conv_b[1:2], w_fout[1], 1)

    dh4, d_final_g, loss_cols = _rows(_loss_fn, [h4, tgt], [final_norm_g[None, :]], [(D_MODEL, F32)],
                                      [(1, D_MODEL), (1, D_MODEL)], name="loss_head")
    loss = lax.psum(jnp.sum(loss_cols), ("x", "y", "c"))

    dh3, gf1 = _ffn_bwd(h3, dh4, ffn1_saved, norm_ffn_g[1:2], w_fin[1], conv_w[1], ffn_conv_b[1:2], w_fout[1], 1)
    scatters = {3: _scatter_start({("ffn_w_in", 1): gf1["w_in"], ("ffn_w_out", 1): gf1["w_out"]}, 3)}
    do = _mm(dh3, w_oout, tb=True, deps=[scatters[3][4]], name="odd_out_dx")
    d_w_oout = _mm(o, dh3, ta=True, out_dtype=BF16, name="odd_out_dw")
    dq_nope, dq_rope, dkv, dk_rope = _attn_bwd(q_all, q_rope, kv, k_rope, o, lse, do, name="attn_bwd")
    dq_rope_t = dq_rope.transpose(1, 0, 2)
    do1, do2 = dq_rope_t[:, :, :half].reshape(T, rope_w), dq_rope_t[:, :, half:].reshape(T, rope_w)
    lane_pad = ((0, 0), (0, LANE - half))
    dko1, dko2 = jnp.pad(dk_rope[:, :half], lane_pad), jnp.pad(dk_rope[:, half:], lane_pad)
    dq_all, = _rows(_rope_q_bwd_fn, [dq_nope, do1, do2, pos], [freqs_q], [(MLA_HEADS * MLA_QK, BF16)], [], name="mla_rope_q_bwd")
    d_w_uq_perm = _mm(qn, dq_all, ta=True, out_dtype=BF16, name="mla_uq_dw")
    dqn = _mm(dq_all, w_uq_perm, tb=True, name="mla_uq_dx")
    d_w_ukv = _mm(kvn, dkv, ta=True, out_dtype=BF16, name="mla_ukv_dw")
    dkvn = _mm(dkv, w_ukv, tb=True, name="mla_ukv_dx")
    dproj_o, d_qg, d_kvg = _rows(_mla_prep_bwd_fn, [proj_o, pos, dqn, dkvn, dko1, dko2], [qg, kvg, freqs_k],
                                 [(ODD_IN_PAD, BF16)], [(1, MLA_Q_RANK), (1, MLA_KV_RANK)], name="mla_prep_bwd")
    d_w_oin_pad = _mm(hn1, dproj_o, ta=True, out_dtype=BF16, name="odd_in_dw")
    dhn1 = _mm(dproj_o, w_oin_pad, tb=True, name="odd_in_dx")
    dh2, d_mix_g1 = _rows(_rms_bwd_fn, [h2, dhn1, dh3], [norm_mix_g[1:2]], [(D_MODEL, F32)], [(1, D_MODEL)], name="mix1_dnorm")
    d_w_oin = jnp.concatenate([d_w_oin_pad[:, :kr0 + half], d_w_oin_pad[:, kr0 + LANE:kr0 + LANE + half]], axis=1)
    d3 = d_w_uq_perm
    d_w_uq = jnp.concatenate([d3[:, :nope_w].reshape(MLA_Q_RANK, MLA_HEADS, MLA_NOPE),
                              d3[:, nope_w:nope_w + rope_w].reshape(MLA_Q_RANK, MLA_HEADS, half),
                              d3[:, nope_w + rope_w:].reshape(MLA_Q_RANK, MLA_HEADS, half)], axis=2).reshape(MLA_Q_RANK, -1)
    scatters[2] = _scatter_start({("odd_w_in", 0): d_w_oin, ("mla_w_uq", 0): d_w_uq, ("mla_w_ukv", 0): d_w_ukv,
                                  ("odd_w_out", 0): d_w_oout}, 2)

    dh1, gf0 = _ffn_bwd(h1, dh2, ffn0_saved, norm_ffn_g[0:1], w_fin[0], conv_w[0], ffn_conv_b[0:1], w_fout[0], 0,
                        deps=[scatters[2][4]])
    scatters[1] = _scatter_start({("ffn_w_in", 0): gf0["w_in"], ("ffn_w_out", 0): gf0["w_out"]}, 1)
    dmix = _mm(dh1, w_eout, tb=True, deps=[scatters[1][4]], name="even_out_dx")
    d_w_eout = _mm(mixin, dh1, ta=True, out_dtype=BF16, name="even_out_dw")
    dq, df, di, dg, d_lb0, d_hgrn_g = _hgrn_bwd(proj, lb0, hgrn_norm_g, states, dmix, name="hgrn_bwd")
    dz1, dgl, d_b_glu = _rows(_s5_gate_bwd_fn, [z5, gl, (dmix, HGRN_DIM, S5_DIM)], [s5_b_glu],
                              [(S5_DIM, F32), (S5_DIM, BF16)], [(1, S5_DIM)], name="s5_gate_bwd")
    dz2 = _mm(dgl, w_glu, tb=True, name="s5_glu_dx")
    d_w_glu = _mm(z5, dgl, ta=True, out_dtype=BF16, name="s5_glu_dw")
    dys, du1, d_s5_d = _rows(_s5_act_bwd_fn, [ys, (proj,) + u_cols, dz1, dz2], [s5_d],
                             [(S5_DIM, BF16), (S5_DIM, F32)], [(1, S5_DIM)], name="s5_act_bwd")
    ds_re = _mm(dys, c_re, tb=True, name="s5_ds_re")
    ds_im = _mm(dys, c_im_neg, tb=True, name="s5_ds_im")
    d_c_re = _mm(s_re, dys, ta=True, name="s5_dc_re")
    d_c_im_neg = _mm(s_im, dys, ta=True, name="s5_dc_im")
    lam_re, lam_im, d_ar, d_ai = _s5_scan_bwd(ds_re, ds_im, s_re, s_im, a_re, a_im, name="s5_scan_bwd")
    du2 = _mm(lam_im, bb_im, tb=True, res=_mm(lam_re, bb_re, tb=True, name="s5_du_re"), name="s5_du_im")
    d_bb_re = _mm(proj, lam_re, ta=True, a_cols=u_cols, name="s5_dbb_re")
    d_bb_im = _mm(proj, lam_im, ta=True, a_cols=u_cols, name="s5_dbb_im")
    dproj, = _rows(_dproj_fn, [dq, df, di, dg, du1, du2], [], [(EVEN_IN, BF16)], [], name="even_dproj")
    d_w_ein = _mm(hn0, dproj, ta=True, out_dtype=BF16, name="even_in_dw")
    dhn0 = _mm(dproj, w_ein, tb=True, name="even_in_dx")
    grad_x, d_mix_g0 = _rows(_rms_bwd_fn, [h0, dhn0, dh1], [norm_mix_g[0:1]], [(D_MODEL, F32)], [(1, D_MODEL)], name="mix0_dnorm")
    sp_g = _s5_params_bwd(*sp_in, d_ar.sum(0).reshape(S5_GROUPS, S5_STATE), d_ai.sum(0).reshape(S5_GROUPS, S5_STATE),
                          _diag_blocks(d_bb_re, S5_GROUP, S5_STATE), _diag_blocks(d_bb_im, S5_GROUP, S5_STATE), d_lb0)
    d_a_re, d_a_im, d_log_dt, d_bt_re, d_bt_im, d_lb_logits = sp_g

    g_small = dict(mla_q_norm_g=d_qg, mla_kv_norm_g=d_kvg, ffn_conv_w=jnp.stack([gf0["conv_w"], gf1["conv_w"]]))
    g_repl = dict(
        norm_mix_g=jnp.concatenate([d_mix_g0, d_mix_g1]), norm_ffn_g=jnp.concatenate([gf0["g"], gf1["g"]]),
        final_norm_g=d_final_g[0], hgrn_lb_logits=d_lb_logits, hgrn_norm_g=d_hgrn_g,
        s5_a_re=d_a_re[None], s5_a_im=d_a_im[None], s5_log_dt=d_log_dt[:, 0][None],
        s5_b_re=d_bt_re.transpose(0, 2, 1)[None], s5_b_im=d_bt_im.transpose(0, 2, 1)[None],
        s5_c_re=_diag_blocks(d_c_re, S5_STATE, S5_GROUP).transpose(0, 2, 1)[None],
        s5_c_im=-_diag_blocks(d_c_im_neg, S5_STATE, S5_GROUP).transpose(0, 2, 1)[None],
        s5_d=d_s5_d, s5_b_glu=d_b_glu, ffn_conv_b=jnp.concatenate([gf0["conv_b"], gf1["conv_b"]]))
    scatters[0] = _scatter_start({("even_w_in", 0): d_w_ein, ("s5_w_glu", 0): d_w_glu, ("even_w_out", 0): d_w_eout}, 0,
                                 extra=_small_grad_packs(g_small, g_repl, w))

    delta, new_m, new_v = {}, {}, {}
    per_unit = {}
    after = grad_x
    for stage in (3, 2, 1, 0):
        partial = _scatter_finish(scatters[stage], after, stage)
        for (n, l), slabs in zip(STAGES[stage], partial):
            per_unit[n, l] = _adamw(w[n][l], slabs, mom[n][l], var[n][l], name=f"adamw_{n}_{l}")
        after = per_unit[STAGES[stage][-1]][0]
    repl_sum = _sum_slabs(partial[-1], F32, name="sum_g_repl").reshape(-1)
    grads = dict(zip(REPLICATED, _unpack(repl_sum, [w[n].shape for n in REPLICATED])))
    small_sum = _sum_slabs(partial[-2], F32, name="sum_g_small").reshape(-1)
    grads.update(zip(SMALL_SHARDED_NAMES, _unpack(small_sum, [w[n].shape for n in SMALL_SHARDED_NAMES])))
    for n in BIG_NAMES:
        layers = [per_unit[n, l] for l in range(w[n].shape[0])]
        grads[n], delta[n], new_m[n], new_v[n] = (jnp.stack([lay[k] for lay in layers]) for k in range(4))
    small = [n for n in W_NAMES if n not in BIG_NAMES]
    packed = [_pack([t[n].reshape(-1) for n in small], LANE, SUBLANE) for t in (w, grads, mom, var)]
    outs = _adamw(*packed, name="adamw_small")
    small_shapes = [w[n].shape for n in small]
    for dst, o_ in zip((delta, new_m, new_v), outs):
        dst.update(zip(small, _unpack(o_.reshape(-1), small_shapes)))

    return (loss, grad_x[None], *[grads[n] for n in W_NAMES], *[delta[n] for n in W_NAMES],
            *[new_m[n] for n in W_NAMES], *[new_v[n] for n in W_NAMES])
```

```python
import functools
import math

import numpy as np
import jax
import jax.numpy as jnp
from jax import lax
from jax.experimental import pallas as pl
from jax.experimental.pallas import tpu as pltpu

F32 = jnp.float32
BF16 = jnp.bfloat16
_MXU_DTYPE = jnp.bfloat16

D_MODEL = 1024
HGRN_DIM = 512
HGRN_HEAD_DIM = 128
HGRN_HEADS = 4
HGRN_CHUNK = 64
S5_DIM = 512
S5_GROUPS = 32
S5_GROUP = 16
S5_STATE = 64
S5_WIDTH = S5_GROUPS * S5_STATE
EVEN_IN = 4 * HGRN_DIM + S5_DIM
MLA_HEADS = 8
MLA_Q_RANK = 384
MLA_KV_RANK = 256
MLA_NOPE = 128
MLA_ROPE = 64
MLA_V = 128
MLA_QK = MLA_NOPE + MLA_ROPE
ODD_IN = MLA_Q_RANK + MLA_KV_RANK + MLA_ROPE
ODD_IN_PAD = MLA_Q_RANK + MLA_KV_RANK + 2 * 128
ROPE_THETA = 10000.0
D_FF = 2816
EPS = 1e-6
ADAM_LR = 0.001
ADAM_B1 = 0.9
ADAM_B2 = 0.999
ADAM_EPS = 1e-08
ADAM_WD = 0.01
ADAM_STEP = 10

N_DEV = 8
LANE = 128
SUBLANE = 8
VMEM_LIMIT_BYTES = 56 * 1024 * 1024
MESH = pl.DeviceIdType.MESH


def _cparams(sem=None):
    return pltpu.CompilerParams(dimension_semantics=sem, vmem_limit_bytes=VMEM_LIMIT_BYTES)


def _pick(n, cands):
    for c in cands:
        if n % c == 0:
            return c
    raise ValueError(f"no tile for {n} in {cands}")


def _sigmoid(x):
    return 1.0 / (1.0 + jnp.exp(-x))


def _silu(x):
    return x * _sigmoid(x)


def _gelu(x):
    return 0.5 * x * (1.0 + jnp.tanh(math.sqrt(2.0 / math.pi) * (x + 0.044715 * (x * x * x))))


def _rms(x, g):
    return x * lax.rsqrt(jnp.mean(x * x, axis=-1, keepdims=True) + EPS) * g


def _mxu(a, b, ca, cb):
    return lax.dot_general(a.astype(_MXU_DTYPE), b.astype(_MXU_DTYPE), (((ca,), (cb,)), ((), ())),
                           preferred_element_type=F32)


@functools.partial(jax.custom_vjp, nondiff_argnums=(2, 3))
def _mxu_ad(a, b, ca, cb):
    return _mxu(a, b, ca, cb)


def _mxu_ad_fwd(a, b, ca, cb):
    return _mxu(a, b, ca, cb), (a, b)


def _mxu_ad_bwd(ca, cb, saved, g):
    a, b = saved
    fa, fb = 1 - ca, 1 - cb
    da = _mxu(g, b, 1, fb) if ca == 1 else _mxu(b, g, fb, 1)
    db = _mxu(a, g, fa, 0) if cb == 0 else _mxu(g, a, 0, fa)
    return da, db


_mxu_ad.defvjp(_mxu_ad_fwd, _mxu_ad_bwd)


def _tri(n):
    row = lax.broadcasted_iota(jnp.int32, (n, n), 0)
    col = lax.broadcasted_iota(jnp.int32, (n, n), 1)
    return col <= row


def _cumsum_rows(x, reverse=False):
    n = x.shape[0]
    rowi = lax.broadcasted_iota(jnp.int32, x.shape, 0)
    s = 1
    while s < n:
        if reverse:
            x = x + jnp.where(rowi < n - s, pltpu.roll(x, n - s, 0), 0.0)
        else:
            x = x + jnp.where(rowi >= s, pltpu.roll(x, s, 0), 0.0)
        s *= 2
    return x


@jax.custom_vjp
def _cumsum_rows_ad(x):
    return _cumsum_rows(x)


def _cumsum_rows_ad_fwd(x):
    return _cumsum_rows(x), None


def _cumsum_rows_ad_bwd(_, g):
    return (_cumsum_rows(g, reverse=True),)


_cumsum_rows_ad.defvjp(_cumsum_rows_ad_fwd, _cumsum_rows_ad_bwd)


MM_VMEM_BUDGET = 36 * 1024 * 1024
MM_MAX_TILE = 1408


def _lane_divisors(n, cap, offs=()):
    return [d for d in range(min(n, cap) // LANE * LANE, 0, -LANE) if n % d == 0 and all(o % d == 0 for o in offs)]


def _mm_tiles(M, N, K, sa, sb, so, has_res, m_offs, n_offs, k_offs):
    best = None
    for tm in _lane_divisors(M, MM_MAX_TILE, m_offs):
        for tn in _lane_divisors(N, MM_MAX_TILE, n_offs):
            for tk in _lane_divisors(K, K, k_offs):
                nk = K // tk
                vmem = 2 * (tm * tk * sa + tk * tn * sb + tm * tn * so + tm * tn * 4 * has_res) + (tm * tn * 4 if nk > 1 else 0)
                if vmem <= MM_VMEM_BUDGET:
                    key = (-nk, tm * tn, tn)
                    if best is None or key > best[0]:
                        best = (key, tm, tn, tk)
                    break
    return best[1:]


def _mm(a, b, *, ta=False, tb=False, res=None, out_dtype=F32, a_cols=None, b_cols=None, deps=(), name):
    a_minor = a.shape[1] if a_cols is None else a_cols[1]
    b_minor = b.shape[1] if b_cols is None else b_cols[1]
    K, M = (a.shape[0], a_minor) if ta else (a_minor, a.shape[0])
    N = b.shape[0] if tb else b_minor
    assert (b_minor if tb else b.shape[0]) == K, (a.shape, b.shape, ta, tb)
    a_off = 0 if a_cols is None else a_cols[0]
    b_off = 0 if b_cols is None else b_cols[0]
    has_res = res is not None
    tm, tn, tk = _mm_tiles(M, N, K, a.dtype.itemsize, b.dtype.itemsize, jnp.dtype(out_dtype).itemsize, has_res,
                           (a_off,) if ta else (), () if tb else (b_off,), ((a_off,) if not ta else ()) + ((b_off,) if tb else ()))
    nk = K // tk
    am, ak = (a_off // tm, 0) if ta else (0, a_off // tk)
    bn, bk = (0, b_off // tk) if tb else (b_off // tn, 0)
    a_spec = pl.BlockSpec((tk, tm), lambda i, j, k: (k, i + am)) if ta else pl.BlockSpec((tm, tk), lambda i, j, k: (i, k + ak))
    b_spec = pl.BlockSpec((tn, tk), lambda i, j, k: (j, k + bk)) if tb else pl.BlockSpec((tk, tn), lambda i, j, k: (k, j + bn))
    o_spec = pl.BlockSpec((tm, tn), lambda i, j, k: (i, j))
    ca, cb = (0 if ta else 1), (1 if tb else 0)

    n_fixed = 2 + has_res + len(deps)

    def body(*refs):
        a_ref, b_ref = refs[0], refs[1]
        res_ref = refs[2] if has_res else None
        o_ref = refs[n_fixed]
        part = _mxu(a_ref[...], b_ref[...], ca, cb)
        if nk == 1:
            o_ref[...] = (part + res_ref[...] if has_res else part).astype(out_dtype)
            return
        acc_ref = refs[n_fixed + 1]
        k = pl.program_id(2)

        @pl.when(k == 0)
        def _():
            acc_ref[...] = part

        @pl.when(k > 0)
        def _():
            acc_ref[...] += part

        @pl.when(k == nk - 1)
        def _():
            o_ref[...] = (acc_ref[...] + res_ref[...] if has_res else acc_ref[...]).astype(out_dtype)

    ins = [a, b] + ([res] if has_res else []) + list(deps)
    in_specs = [a_spec, b_spec] + ([o_spec] if has_res else []) + [pl.BlockSpec(memory_space=pl.ANY)] * len(deps)
    return pl.pallas_call(
        body, name=name, grid=(M // tm, N // tn, nk),
        in_specs=in_specs, out_specs=o_spec,
        out_shape=jax.ShapeDtypeStruct((M, N), out_dtype),
        scratch_shapes=[pltpu.VMEM((tm, tn), F32)] if nk > 1 else [],
        compiler_params=_cparams(("parallel", "parallel", "arbitrary")),
    )(*ins)


def _rows(fn, row_ins, const_ins, row_outs, acc_outs, *, name, tm=256, deps=()):
    norm = [(r, 0, r.shape[1]) if not isinstance(r, tuple) else r for r in row_ins]
    T = norm[0][0].shape[0]
    tm = min(tm, T)
    nr, nc, no, na = len(norm), len(const_ins), len(row_outs), len(acc_outs)
    first_out = nr + nc + len(deps)

    def body(*refs):
        i = pl.program_id(0)
        vals = [r[...] for r in refs[:nr + nc]]
        outs = fn(*vals)
        for o_ref, o in zip(refs[first_out:first_out + no], outs[:no]):
            o_ref[...] = o.astype(o_ref.dtype)
        for a_ref, o in zip(refs[first_out + no:], outs[no:]):
            @pl.when(i == 0)
            def _(a_ref=a_ref, o=o):
                a_ref[...] = o

            @pl.when(i > 0)
            def _(a_ref=a_ref, o=o):
                a_ref[...] += o

    in_specs = []
    for arr, off, w in norm:
        assert off % w == 0, (off, w)
        in_specs.append(pl.BlockSpec((tm, w), lambda i, b=off // w: (i, b)))
    for c in const_ins:
        in_specs.append(pl.BlockSpec(c.shape, lambda i: (0, 0)))
    in_specs += [pl.BlockSpec(memory_space=pl.ANY)] * len(deps)
    out_specs = [pl.BlockSpec((tm, w), lambda i: (i, 0)) for w, _ in row_outs]
    out_specs += [pl.BlockSpec(s, lambda i: (0, 0)) for s in acc_outs]
    out_shape = [jax.ShapeDtypeStruct((T, w), dt) for w, dt in row_outs]
    out_shape += [jax.ShapeDtypeStruct(s, F32) for s in acc_outs]
    return pl.pallas_call(
        body, name=name, grid=(T // tm,), in_specs=in_specs, out_specs=out_specs, out_shape=out_shape,
        compiler_params=_cparams(("arbitrary",)),
    )(*[n[0] for n in norm], *const_ins, *deps)


FFN_COL_TILE = LANE
FFN_ROW_CHUNK = 512


def _shift_down(ext, s, rows):
    return pltpu.roll(ext, s, 0)[SUBLANE:SUBLANE + rows]


def _shift_up(ext, s, rows):
    return pltpu.roll(ext, rows + SUBLANE - s, 0)[:rows]


def _ffn_chunks(T):
    r = min(FFN_ROW_CHUNK, T)
    return r, T // r


def _ext_before(ref, c, r):
    if c == 0:
        return jnp.concatenate([jnp.zeros((SUBLANE, ref.shape[1]), F32), ref[0:r, :]], axis=0)
    return ref[c * r - SUBLANE:(c + 1) * r, :]


def _ext_after(ref, c, r, nch):
    if c == nch - 1:
        return jnp.concatenate([ref[c * r:(c + 1) * r, :], jnp.zeros((SUBLANE, ref.shape[1]), F32)], axis=0)
    return ref[c * r:(c + 1) * r + SUBLANE, :]


def _ffn_mid_fwd(au, conv_w, conv_b, *, name):
    T = au.shape[0]
    tc = FFN_COL_TILE
    ncol = D_FF // tc
    r, nch = _ffn_chunks(T)

    def body(a_ref, u_ref, w_ref, b_ref, z_ref):
        w0, w1, w2, bias = w_ref[0:1, :], w_ref[1:2, :], w_ref[2:3, :], b_ref[...]
        for c in range(nch):
            ext = _ext_before(a_ref, c, r)
            pre = w0 * _shift_down(ext, 2, r) + w1 * _shift_down(ext, 1, r) + w2 * ext[SUBLANE:] + bias
            z_ref[c * r:(c + 1) * r, :] = (_silu(pre) * u_ref[c * r:(c + 1) * r, :]).astype(z_ref.dtype)

    return pl.pallas_call(
        body, name=name, grid=(ncol,),
        in_specs=[pl.BlockSpec((T, tc), lambda j: (0, j)), pl.BlockSpec((T, tc), lambda j: (0, j + ncol)),
                  pl.BlockSpec((3, tc), lambda j: (0, j)), pl.BlockSpec((1, tc), lambda j: (0, j))],
        out_specs=pl.BlockSpec((T, tc), lambda j: (0, j)),
        out_shape=jax.ShapeDtypeStruct((T, D_FF), BF16),
        compiler_params=_cparams(("parallel",)),
    )(au, au, conv_w, conv_b)


def _ffn_mid_bwd(au, dz, conv_w, conv_b, *, name):
    T = au.shape[0]
    tc = FFN_COL_TILE
    ncol = D_FF // tc
    r, nch = _ffn_chunks(T)

    def body(a_ref, u_ref, dz_ref, w_ref, b_ref, da_ref, du_ref, dw_ref, db_ref, dpre_ref):
        w0, w1, w2, bias = w_ref[0:1, :], w_ref[1:2, :], w_ref[2:3, :], b_ref[...]
        dw0 = jnp.zeros((1, tc), F32)
        dw1 = jnp.zeros((1, tc), F32)
        dw2 = jnp.zeros((1, tc), F32)
        db = jnp.zeros((1, tc), F32)
        for c in range(nch):
            rows = slice(c * r, (c + 1) * r)
            ext = _ext_before(a_ref, c, r)
            a2, a1, a0 = _shift_down(ext, 2, r), _shift_down(ext, 1, r), ext[SUBLANE:]
            pre = w0 * a2 + w1 * a1 + w2 * a0 + bias
            sg = _sigmoid(pre)
            act = pre * sg
            dzc = dz_ref[rows, :]
            du_ref[rows, :] = (dzc * act).astype(du_ref.dtype)
            dpre = dzc * u_ref[rows, :] * (sg * (1.0 + pre * (1.0 - sg)))
            dpre_ref[rows, :] = dpre
            dw0 += jnp.sum(dpre * a2, axis=0, keepdims=True)
            dw1 += jnp.sum(dpre * a1, axis=0, keepdims=True)
            dw2 += jnp.sum(dpre * a0, axis=0, keepdims=True)
            db += jnp.sum(dpre, axis=0, keepdims=True)
        for c in range(nch):
            ext = _ext_after(dpre_ref, c, r, nch)
            da = w0 * _shift_up(ext, 2, r) + w1 * _shift_up(ext, 1, r) + w2 * ext[:r]
            da_ref[c * r:(c + 1) * r, :] = da.astype(da_ref.dtype)
        dw_ref[0:1, :] = dw0
        dw_ref[1:2, :] = dw1
        dw_ref[2:3, :] = dw2
        db_ref[...] = db

    col = lambda j: (0, j)
    return pl.pallas_call(
        body, name=name, grid=(ncol,),
        in_specs=[pl.BlockSpec((T, tc), col), pl.BlockSpec((T, tc), lambda j: (0, j + ncol)), pl.BlockSpec((T, tc), col),
                  pl.BlockSpec((3, tc), col), pl.BlockSpec((1, tc), col)],
        out_specs=[pl.BlockSpec((T, tc), col), pl.BlockSpec((T, tc), col), pl.BlockSpec((3, tc), col), pl.BlockSpec((1, tc), col)],
        out_shape=[jax.ShapeDtypeStruct((T, D_FF), BF16), jax.ShapeDtypeStruct((T, D_FF), BF16),
                   jax.ShapeDtypeStruct((3, D_FF), F32), jax.ShapeDtypeStruct((1, D_FF), F32)],
        scratch_shapes=[pltpu.VMEM((T, tc), F32)],
        compiler_params=_cparams(("parallel",)),
    )(au, au, dz, conv_w, conv_b)


HGRN_BLOCK = 256


def _hgrn_chunk(dot, cumsum, q, f, i, g, lb, ng, st):
    C = q.shape[0]
    forget = lb + (1.0 - lb) * _sigmoid(f)
    k = 1.0 - forget
    b = cumsum(jnp.log(forget))
    b_last = b[C - 1:C, :]
    qd = q * jnp.exp(b)
    kd = k * jnp.exp(-b)
    att = jnp.where(_tri(C), dot(qd, kd, 1, 1), 0.0)
    o = dot(att, i, 1, 0) + dot(qd, st, 1, 1)
    st_new = st * jnp.exp(b_last) + dot(i, k * jnp.exp(b_last - b), 0, 0)
    on = o * lax.rsqrt(jnp.mean(o * o, axis=-1, keepdims=True) + EPS) * ng
    return on * _silu(g), st_new


def _hgrn_specs(T, rev):
    tb = min(HGRN_BLOCK, T)
    nb = T // tb
    blk = (lambda n: nb - 1 - n) if rev else (lambda n: n)
    hd = HGRN_HEAD_DIM
    proj_specs = [pl.BlockSpec((tb, HGRN_DIM), lambda n, k=k: (blk(n), k)) for k in range(4)]
    vec_spec = pl.BlockSpec((1, HGRN_DIM), lambda n: (0, 0))
    tok_spec = pl.BlockSpec((tb, HGRN_DIM), lambda n: (blk(n), 0))
    st_spec = pl.BlockSpec((HGRN_HEADS, tb // HGRN_CHUNK, hd, hd), lambda n: (0, blk(n), 0, 0))
    return tb, nb, proj_specs, vec_spec, tok_spec, st_spec


def _head_cols(h):
    return slice(h * HGRN_HEAD_DIM, (h + 1) * HGRN_HEAD_DIM)


def _hgrn_fwd(proj, lb, ng, *, name):
    T = proj.shape[0]
    tb, nb, proj_specs, vec_spec, tok_spec, st_spec = _hgrn_specs(T, False)
    nsub = tb // HGRN_CHUNK
    hd = HGRN_HEAD_DIM

    def body(q_ref, f_ref, i_ref, g_ref, lb_ref, ng_ref, y_ref, sts_ref, st_ref):
        @pl.when(pl.program_id(0) == 0)
        def _():
            st_ref[...] = jnp.zeros_like(st_ref)

        st = [st_ref[h] for h in range(HGRN_HEADS)]
        for s in range(nsub):
            rows = slice(s * HGRN_CHUNK, (s + 1) * HGRN_CHUNK)
            for h in range(HGRN_HEADS):
                cols = _head_cols(h)
                sts_ref[h, s] = st[h]
                y, st[h] = _hgrn_chunk(_mxu, _cumsum_rows, q_ref[rows, cols], f_ref[rows, cols], i_ref[rows, cols],
                                       g_ref[rows, cols], lb_ref[:, cols], ng_ref[:, cols], st[h])
                y_ref[rows, cols] = y
        for h in range(HGRN_HEADS):
            st_ref[h] = st[h]

    return pl.pallas_call(
        body, name=name, grid=(nb,),
        in_specs=proj_specs + [vec_spec, vec_spec], out_specs=[tok_spec, st_spec],
        out_shape=[jax.ShapeDtypeStruct((T, HGRN_DIM), F32),
                   jax.ShapeDtypeStruct((HGRN_HEADS, T // HGRN_CHUNK, hd, hd), F32)],
        scratch_shapes=[pltpu.VMEM((HGRN_HEADS, hd, hd), F32)],
        compiler_params=_cparams(("arbitrary",)),
    )(proj, proj, proj, proj, lb, ng)


def _hgrn_bwd(proj, lb, ng, states, dmix, *, name):
    T = proj.shape[0]
    tb, nb, proj_specs, vec_spec, tok_spec, st_spec = _hgrn_specs(T, True)
    nsub = tb // HGRN_CHUNK
    hd = HGRN_HEAD_DIM
    chunk = functools.partial(_hgrn_chunk, _mxu_ad, _cumsum_rows_ad)

    def body(q_ref, f_ref, i_ref, g_ref, lb_ref, ng_ref, sts_ref, dy_ref,
             dq_ref, df_ref, di_ref, dg_ref, dlb_ref, dng_ref, dst_ref):
        @pl.when(pl.program_id(0) == 0)
        def _():
            dst_ref[...] = jnp.zeros_like(dst_ref)
            dlb_ref[...] = jnp.zeros_like(dlb_ref)
            dng_ref[...] = jnp.zeros_like(dng_ref)

        dst = [dst_ref[h] for h in range(HGRN_HEADS)]
        dlb = [jnp.zeros((1, hd), F32)] * HGRN_HEADS
        dng = [jnp.zeros((1, hd), F32)] * HGRN_HEADS
        for s in reversed(range(nsub)):
            rows = slice(s * HGRN_CHUNK, (s + 1) * HGRN_CHUNK)
            for h in range(HGRN_HEADS):
                cols = _head_cols(h)
                _, vjp = jax.vjp(chunk, q_ref[rows, cols], f_ref[rows, cols], i_ref[rows, cols], g_ref[rows, cols],
                                 lb_ref[:, cols], ng_ref[:, cols], sts_ref[h, s])
                dq, df, di, dg, dlb_s, dng_s, dst[h] = vjp((dy_ref[rows, cols], dst[h]))
                dq_ref[rows, cols] = dq
                df_ref[rows, cols] = df
                di_ref[rows, cols] = di
                dg_ref[rows, cols] = dg
                dlb[h] = dlb[h] + dlb_s
                dng[h] = dng[h] + dng_s
        for h in range(HGRN_HEADS):
            dst_ref[h] = dst[h]
            dlb_ref[:, _head_cols(h)] += dlb[h]
            dng_ref[:, _head_cols(h)] += dng[h]

    tok_out = jax.ShapeDtypeStruct((T, HGRN_DIM), F32)
    vec_out = jax.ShapeDtypeStruct((1, HGRN_DIM), F32)
    return pl.pallas_call(
        body, name=name, grid=(nb,),
        in_specs=proj_specs + [vec_spec, vec_spec, st_spec, tok_spec],
        out_specs=[tok_spec] * 4 + [vec_spec, vec_spec],
        out_shape=[tok_out] * 4 + [vec_out, vec_out],
        scratch_shapes=[pltpu.VMEM((HGRN_HEADS, hd, hd), F32)],
        compiler_params=_cparams(("arbitrary",)),
    )(proj, proj, proj, proj, lb, ng, states, dmix)


S5_LANES = 512
S5_ROWS = 512


def _cmul(ar, ai, br, bi):
    return ar * br - ai * bi, ar * bi + ai * br


def _power_table(ar, ai, exps):
    a2 = _cmul(ar, ai, ar, ai)
    a4 = _cmul(*a2, *a2)
    e = exps - 1
    pr = jnp.broadcast_to(ar, exps.shape)
    pi = jnp.broadcast_to(ai, exps.shape)
    for bit, (fr, fi) in enumerate(((ar, ai), a2, a4)):
        nr, ni = _cmul(pr, pi, fr, fi)
        on = ((e >> bit) & 1) == 1
        pr, pi = jnp.where(on, nr, pr), jnp.where(on, ni, pi)
    return pr, pi, a2, a4


def _s5_scan_fwd(bu, a_re, a_im, *, name):
    T = bu.shape[0]
    w, tr = S5_LANES, min(S5_ROWS, T)
    ncol, nt = S5_WIDTH // w, T // tr

    def body(br_ref, bi_ref, ar_ref, ai_ref, sr_ref, si_ref, carry_ref):
        @pl.when(pl.program_id(1) == 0)
        def _():
            carry_ref[...] = jnp.zeros_like(carry_ref)

        ar, ai = ar_ref[...], ai_ref[...]
        rowi = lax.broadcasted_iota(jnp.int32, (SUBLANE, w), 0)
        pr, pi, a2, a4 = _power_table(ar, ai, rowi + 1)

        def tile(i, carry):
            cr, ci = carry
            rows = pl.ds(pl.multiple_of(i * SUBLANE, SUBLANE), SUBLANE)
            xr, xi = br_ref[rows, :], bi_ref[rows, :]
            for s, (fr, fi) in ((1, (ar, ai)), (2, a2), (4, a4)):
                keep = rowi >= s
                zr = jnp.where(keep, pltpu.roll(xr, s, 0), 0.0)
                zi = jnp.where(keep, pltpu.roll(xi, s, 0), 0.0)
                xr, xi = xr + fr * zr - fi * zi, xi + fr * zi + fi * zr
            xr, xi = xr + pr * cr - pi * ci, xi + pr * ci + pi * cr
            sr_ref[rows, :] = xr
            si_ref[rows, :] = xi
            return xr[SUBLANE - 1:SUBLANE, :], xi[SUBLANE - 1:SUBLANE, :]

        cr, ci = lax.fori_loop(0, tr // SUBLANE, tile, (carry_ref[0:1, :], carry_ref[1:2, :]))
        carry_ref[0:1, :] = cr
        carry_ref[1:2, :] = ci

    out = jax.ShapeDtypeStruct((T, S5_WIDTH), F32)
    return pl.pallas_call(
        body, name=name, grid=(ncol, nt),
        in_specs=[pl.BlockSpec((tr, w), lambda j, t: (t, j)), pl.BlockSpec((tr, w), lambda j, t: (t, j + ncol)),
                  pl.BlockSpec((1, w), lambda j, t: (0, j)), pl.BlockSpec((1, w), lambda j, t: (0, j))],
        out_specs=[pl.BlockSpec((tr, w), lambda j, t: (t, j))] * 2,
        out_shape=[out, out],
        scratch_shapes=[pltpu.VMEM((2, w), F32)],
        compiler_params=_cparams(("parallel", "arbitrary")),
    )(bu, bu, a_re, a_im)


def _s5_scan_bwd(g_re, g_im, s_re, s_im, a_re, a_im, *, name):
    T = g_re.shape[0]
    w, tr = S5_LANES, min(S5_ROWS, T)
    ncol, nt = S5_WIDTH // w, T // tr
    ntile = tr // SUBLANE

    def body(gr_ref, gi_ref, sr_ref, si_ref, ar_ref, ai_ref, lr_ref, li_ref, dar_ref, dai_ref, carry_ref):
        @pl.when(pl.program_id(1) == 0)
        def _():
            carry_ref[...] = jnp.zeros_like(carry_ref)
            dar_ref[...] = jnp.zeros_like(dar_ref)
            dai_ref[...] = jnp.zeros_like(dai_ref)

        ar, ai = ar_ref[...], -ai_ref[...]
        rowi = lax.broadcasted_iota(jnp.int32, (SUBLANE, w), 0)
        pr, pi, a2, a4 = _power_table(ar, ai, SUBLANE - rowi)
        last = rowi == SUBLANE - 1

        def tile(i, carry):
            cr, ci, dar, dai = carry
            rows = pl.ds(pl.multiple_of((ntile - 1 - i) * SUBLANE, SUBLANE), SUBLANE)
            xr, xi = gr_ref[rows, :], gi_ref[rows, :]
            for s, (fr, fi) in ((1, (ar, ai)), (2, a2), (4, a4)):
                keep = rowi < SUBLANE - s
                zr = jnp.where(keep, pltpu.roll(xr, SUBLANE - s, 0), 0.0)
                zi = jnp.where(keep, pltpu.roll(xi, SUBLANE - s, 0), 0.0)
                xr, xi = xr + fr * zr - fi * zi, xi + fr * zi + fi * zr
            xr, xi = xr + pr * cr - pi * ci, xi + pr * ci + pi * cr
            lr_ref[rows, :] = xr
            li_ref[rows, :] = xi
            nr = jnp.where(last, cr, pltpu.roll(xr, SUBLANE - 1, 0))
            ni = jnp.where(last, ci, pltpu.roll(xi, SUBLANE - 1, 0))
            sr, si = sr_ref[rows, :], si_ref[rows, :]
            return xr[0:1, :], xi[0:1, :], dar + nr * sr + ni * si, dai + ni * sr - nr * si

        cr, ci, dar, dai = lax.fori_loop(
            0, ntile, tile, (carry_ref[0:1, :], carry_ref[1:2, :], jnp.zeros((SUBLANE, w), F32), jnp.zeros((SUBLANE, w), F32)))
        carry_ref[0:1, :] = cr
        carry_ref[1:2, :] = ci
        dar_ref[...] += dar
        dai_ref[...] += dai

    tok = pl.BlockSpec((tr, w), lambda j, t: (nt - 1 - t, j))
    vec = pl.BlockSpec((1, w), lambda j, t: (0, j))
    acc = pl.BlockSpec((SUBLANE, w), lambda j, t: (0, j))
    out = jax.ShapeDtypeStruct((T, S5_WIDTH), F32)
    accs = jax.ShapeDtypeStruct((SUBLANE, S5_WIDTH), F32)
    return pl.pallas_call(
        body, name=name, grid=(ncol, nt),
        in_specs=[tok, tok, tok, tok, vec, vec], out_specs=[tok, tok, acc, acc],
        out_shape=[out, out, accs, accs],
        scratch_shapes=[pltpu.VMEM((2, w), F32)],
        compiler_params=_cparams(("parallel", "arbitrary")),
    )(g_re, g_im, s_re, s_im, a_re, a_im)


ATTN_BLOCK = 512
_NEG = -1e30


def _qk_cat(nope, rope):
    return jnp.concatenate([nope.astype(_MXU_DTYPE), rope.astype(_MXU_DTYPE)], axis=1)


def _attn_scores(q, k, q0, k0):
    s = _mxu(q, k, 1, 1) * (MLA_QK ** -0.5)
    qpos = q0 + lax.broadcasted_iota(jnp.int32, s.shape, 0)
    kpos = k0 + lax.broadcasted_iota(jnp.int32, s.shape, 1)
    return s, kpos <= qpos


def _attn_fwd(q_all, q_rope, kv, k_rope, *, name):
    T = q_all.shape[0]
    tq = min(ATTN_BLOCK, T)
    nq = T // tq

    def body(qn_ref, qr_ref, kn_ref, v_ref, kr_ref, o_ref, lse_ref):
        i = pl.program_id(1)
        q = _qk_cat(qn_ref[...], qr_ref[0])

        def step(j, carry):
            m, l, acc = carry
            ks = pl.ds(pl.multiple_of(j * tq, tq), tq)
            s, ok = _attn_scores(q, _qk_cat(kn_ref[ks, :], kr_ref[ks, :]), i * tq, j * tq)
            s = jnp.where(ok, s, _NEG)
            m_new = jnp.maximum(m, jnp.max(s, axis=-1, keepdims=True))
            p = jnp.exp(s - m_new)
            alpha = jnp.exp(m - m_new)
            return m_new, alpha * l + jnp.sum(p, axis=-1, keepdims=True), alpha * acc + _mxu(p, v_ref[ks, :], 1, 0)

        m, l, acc = lax.fori_loop(0, i + 1, step, (jnp.full((tq, 1), _NEG, F32), jnp.zeros((tq, 1), F32), jnp.zeros((tq, MLA_V), F32)))
        o_ref[...] = acc / l
        lse_ref[0] = m + jnp.log(l)

    return pl.pallas_call(
        body, name=name, grid=(MLA_HEADS, nq),
        in_specs=[pl.BlockSpec((tq, MLA_NOPE), lambda h, i: (i, h)), pl.BlockSpec((1, tq, MLA_ROPE), lambda h, i: (h, i, 0)),
                  pl.BlockSpec((T, MLA_NOPE), lambda h, i: (0, 2 * h)), pl.BlockSpec((T, MLA_V), lambda h, i: (0, 2 * h + 1)),
                  pl.BlockSpec((T, MLA_ROPE), lambda h, i: (0, 0))],
        out_specs=[pl.BlockSpec((tq, MLA_V), lambda h, i: (i, h)), pl.BlockSpec((1, tq, 1), lambda h, i: (h, i, 0))],
        out_shape=[jax.ShapeDtypeStruct((T, MLA_HEADS * MLA_V), F32), jax.ShapeDtypeStruct((MLA_HEADS, T, 1), F32)],
        compiler_params=_cparams(("arbitrary", "arbitrary")),
    )(q_all, q_rope, kv, kv, k_rope)


def _attn_bwd(q_all, q_rope, kv, k_rope, o, lse, do, *, name):
    T = q_all.shape[0]
    tk = min(ATTN_BLOCK, T)
    nk = T // tk
    scale = MLA_QK ** -0.5

    def body(qn_ref, qr_ref, kv_ref, kr_ref, o_ref, lse_ref, do_ref, dqn_ref, dqr_ref, dkv_ref, dkr_ref, delta_ref):
        h, j = pl.program_id(0), pl.program_id(1)

        @pl.when(j == 0)
        def _():
            dqn_ref[...] = jnp.zeros_like(dqn_ref)
            dqr_ref[...] = jnp.zeros_like(dqr_ref)
            delta_ref[...] = jnp.sum(do_ref[...] * o_ref[...], axis=-1, keepdims=True)

        @pl.when((j == 0) & (h == 0))
        def _():
            dkr_ref[...] = jnp.zeros_like(dkr_ref)

        krows = pl.ds(pl.multiple_of(j * tk, tk), tk)
        k = _qk_cat(kv_ref[:, :MLA_NOPE], kr_ref[krows, :])
        v = kv_ref[:, MLA_NOPE:].astype(_MXU_DTYPE)

        def step(i, carry):
            dk, dv = carry
            qs = pl.ds(pl.multiple_of(i * tk, tk), tk)
            q, dob = _qk_cat(qn_ref[qs, :], qr_ref[0, qs, :]), do_ref[qs, :].astype(_MXU_DTYPE)
            s, ok = _attn_scores(q, k, i * tk, j * tk)
            p = jnp.where(ok, jnp.exp(s - lse_ref[0, qs, :]), 0.0)
            ds = p * (_mxu(dob, v, 1, 1) - delta_ref[qs, :]) * scale
            dq = _mxu(ds, k, 1, 0)
            dqn_ref[qs, :] += dq[:, :MLA_NOPE]
            dqr_ref[0, qs, :] += dq[:, MLA_NOPE:]
            return dk + _mxu(ds, q, 0, 0), dv + _mxu(p, dob, 0, 0)

        dk, dv = lax.fori_loop(j, nk, step, (jnp.zeros((tk, MLA_QK), F32), jnp.zeros((tk, MLA_V), F32)))
        dkv_ref[:, :MLA_NOPE] = dk[:, :MLA_NOPE]
        dkv_ref[:, MLA_NOPE:] = dv
        dkr_ref[krows, :] += dk[:, MLA_NOPE:]

    head_cols = pl.BlockSpec((T, MLA_NOPE), lambda h, j: (0, h))
    head_rope = pl.BlockSpec((1, T, MLA_ROPE), lambda h, j: (h, 0, 0))
    kv_spec = pl.BlockSpec((tk, MLA_NOPE + MLA_V), lambda h, j: (j, h))
    kr_spec = pl.BlockSpec((T, MLA_ROPE), lambda h, j: (0, 0))
    return pl.pallas_call(
        body, name=name, grid=(MLA_HEADS, nk),
        in_specs=[head_cols, head_rope, kv_spec, kr_spec, head_cols, pl.BlockSpec((1, T, 1), lambda h, j: (h, 0, 0)), head_cols],
        out_specs=[head_cols, head_rope, kv_spec, kr_spec],
        out_shape=[jax.ShapeDtypeStruct((T, MLA_HEADS * MLA_NOPE), F32), jax.ShapeDtypeStruct((MLA_HEADS, T, MLA_ROPE), F32),
                   jax.ShapeDtypeStruct((T, MLA_HEADS * (MLA_NOPE + MLA_V)), F32), jax.ShapeDtypeStruct((T, MLA_ROPE), F32)],
        scratch_shapes=[pltpu.VMEM((T, 1), F32)],
        compiler_params=_cparams(("arbitrary", "arbitrary")),
    )(q_all, q_rope, kv, k_rope, o, lse, do)


def _s5_discretize(a_re, a_im, log_dt, bt_re, bt_im, lb_logits):
    dt = jnp.exp(log_dt)
    mag = jnp.exp(a_re * dt)
    abr, abi = mag * jnp.cos(a_im * dt), mag * jnp.sin(a_im * dt)
    den = a_re * a_re + a_im * a_im
    xr, xi = abr - 1.0, abi
    cr = ((xr * a_re + xi * a_im) / den)[:, None, :]
    ci = ((xi * a_re - xr * a_im) / den)[:, None, :]
    e = jnp.exp(lb_logits - jnp.max(lb_logits, axis=0, keepdims=True))
    lb = e[0:1, :] / jnp.sum(e, axis=0, keepdims=True)
    return abr, abi, cr * bt_re - ci * bt_im, cr * bt_im + ci * bt_re, lb


def _whole(shape):
    return pl.BlockSpec(shape, lambda: (0,) * len(shape))


def _s5_params_fwd(a_re, a_im, log_dt, bt_re, bt_im, lb_logits):
    ins = (a_re, a_im, log_dt, bt_re, bt_im, lb_logits)
    outs = [jax.ShapeDtypeStruct(s, F32) for s in (a_re.shape, a_re.shape, bt_re.shape, bt_re.shape, (1, lb_logits.shape[1]))]

    def body(*refs):
        res = _s5_discretize(*[r[...] for r in refs[:6]])
        for o_ref, o in zip(refs[6:], res):
            o_ref[...] = o

    return pl.pallas_call(body, name="s5_params_fwd", in_specs=[_whole(a.shape) for a in ins],
                          out_specs=[_whole(o.shape) for o in outs], out_shape=outs, compiler_params=_cparams())(*ins)


def _s5_params_bwd(a_re, a_im, log_dt, bt_re, bt_im, lb_logits, d_abr, d_abi, d_bbr, d_bbi, d_lb):
    ins = (a_re, a_im, log_dt, bt_re, bt_im, lb_logits, d_abr, d_abi, d_bbr, d_bbi, d_lb)
    outs = [jax.ShapeDtypeStruct(a.shape, F32) for a in ins[:6]]

    def body(*refs):
        _, vjp = jax.vjp(_s5_discretize, *[r[...] for r in refs[:6]])
        for o_ref, o in zip(refs[11:], vjp(tuple(r[...] for r in refs[6:11]))):
            o_ref[...] = o

    return pl.pallas_call(body, name="s5_params_bwd", in_specs=[_whole(a.shape) for a in ins],
                          out_specs=[_whole(o.shape) for o in outs], out_shape=outs, compiler_params=_cparams())(*ins)


def _adamw(w, g, m, v, *, name):
    R, C = w.shape
    tr = _pick(R, (256, 128, 64, 32, 16, 8)) if R % SUBLANE == 0 else R
    slabs = g.shape[0] if g.ndim == 3 else 0

    def body(w_ref, g_ref, m_ref, v_ref, *outs):
        if slabs:
            gv = g_ref[0].astype(F32)
            for s in range(1, slabs):
                gv = gv + g_ref[s].astype(F32)
            outs[0][...] = gv
            outs = outs[1:]
        else:
            gv = g_ref[...]
        d_ref, mo_ref, vo_ref = outs
        m2 = ADAM_B1 * m_ref[...] + (1.0 - ADAM_B1) * gv
        v2 = ADAM_B2 * v_ref[...] + (1.0 - ADAM_B2) * (gv * gv)
        m_hat = m2 / (1.0 - ADAM_B1 ** ADAM_STEP)
        v_hat = v2 / (1.0 - ADAM_B2 ** ADAM_STEP)
        d_ref[...] = -ADAM_LR * (m_hat / (jnp.sqrt(v_hat) + ADAM_EPS) + ADAM_WD * w_ref[...])
        mo_ref[...] = m2
        vo_ref[...] = v2

    spec = pl.BlockSpec((tr, C), lambda i: (i, 0))
    g_spec = pl.BlockSpec((slabs, tr, C), lambda i: (0, i, 0)) if slabs else spec
    out = jax.ShapeDtypeStruct((R, C), F32)
    n_out = 4 if slabs else 3
    return pl.pallas_call(body, name=name, grid=(R // tr,), in_specs=[spec, g_spec, spec, spec], out_specs=[spec] * n_out,
                          out_shape=[out] * n_out, compiler_params=_cparams(("parallel",)))(w, g, m, v)


N_CHIPS = 4
N_CORES = 2


_FLIPS = tuple((dx, dy, dc) for dx in (0, 1) for dy in (0, 1) for dc in (0, 1) if (dx, dy, dc) != (0, 0, 0))


_HBM = pl.BlockSpec(memory_space=pltpu.HBM)
_SEM = pl.BlockSpec(memory_space=pltpu.SEMAPHORE)
_SPLIT_COPY = pltpu.CompilerParams(has_side_effects=pltpu.SideEffectType.DATAFLOW_SIDE_EFFECTING)


def _exchange_copies(src_refs, land_refs, send_sems, recv_sems, scatter, arriving):
    x, y, c = lax.axis_index("x"), lax.axis_index("y"), lax.axis_index("c")
    me_chip = 2 * x + y
    copies = []
    for a, (s_ref, l_ref) in enumerate(zip(src_refs, land_refs)):
        for j, (dx, dy, dc) in enumerate(_FLIPS):
            px, py, pc = (1 - x if dx else x), (1 - y if dy else y), (1 - c if dc else c)
            k = a * len(_FLIPS) + j
            p_chip = 2 * px + py
            copies.append(pltpu.make_async_remote_copy(
                src_ref=s_ref.at[p_chip, pc] if scatter else s_ref, dst_ref=l_ref.at[p_chip, pc] if arriving else l_ref.at[me_chip, c],
                send_sem=send_sems.at[k], recv_sem=recv_sems.at[k], device_id=(px, py, pc), device_id_type=MESH))
    return copies


def _exchange_start(srcs, *, scatter, name, after=()):
    n_arr = len(srcs)
    n_sem = n_arr * len(_FLIPS)
    n_in = 2 * n_arr + len(after)
    lands = [lax.empty(s.shape if scatter else (N_CHIPS, N_CORES) + s.shape, s.dtype) for s in srcs]

    def body(*refs):
        src_refs, land_refs = refs[:n_arr], refs[n_arr:2 * n_arr]
        for cp in _exchange_copies(src_refs, land_refs, refs[n_in], refs[n_in + 1], scatter, arriving=False):
            cp.start()
        refs[-1][...] = jnp.zeros_like(refs[-1])

    thru = [pltpu.HBM(a.shape, a.dtype) for a in srcs + lands]
    outs = pl.pallas_call(
        body, name=name,
        out_shape=(pltpu.SemaphoreType.DMA((n_sem,)), pltpu.SemaphoreType.DMA((n_sem,)), *thru,
                   jax.ShapeDtypeStruct((SUBLANE, LANE), F32)),
        in_specs=[_HBM] * (2 * n_arr) + [pl.BlockSpec(memory_space=pl.ANY)] * len(after),
        out_specs=(_SEM, _SEM, *[_HBM] * (2 * n_arr), pl.BlockSpec(memory_space=pltpu.VMEM)),
        input_output_aliases={i: 2 + i for i in range(2 * n_arr)}, compiler_params=_SPLIT_COPY,
    )(*[pltpu.with_memory_space_constraint(a, pltpu.HBM) for a in srcs + lands], *after)
    return outs[0], outs[1], list(outs[2:2 + n_arr]), list(outs[2 + n_arr:2 + 2 * n_arr]), outs[-1]


def _exchange_wait(started, after, *, scatter, name):
    send_sems, recv_sems, srcs, lands, _ = started
    n_arr = len(srcs)

    def body(*refs):
        src_refs, land_refs = refs[:n_arr], refs[n_arr:2 * n_arr]
        for cp in _exchange_copies(src_refs, land_refs, refs[2 * n_arr], refs[2 * n_arr + 1], scatter, arriving=True):
            cp.wait_send()
            cp.wait_recv()

    outs = pl.pallas_call(
        body, name=name, out_shape=[pltpu.HBM(a.shape, a.dtype) for a in srcs + lands],
        in_specs=[_HBM] * (2 * n_arr) + [_SEM, _SEM, pl.BlockSpec(memory_space=pl.ANY)], out_specs=[_HBM] * (2 * n_arr),
        input_output_aliases={i: i for i in range(2 * n_arr)}, compiler_params=_SPLIT_COPY,
    )(*srcs, *lands, send_sems, recv_sems, after)
    return list(outs[:n_arr]), list(outs[n_arr:])


def _with_own(land, own):
    me_chip = 2 * lax.axis_index("x") + lax.axis_index("y")
    return lax.dynamic_update_slice(land, own[None, None], (me_chip, lax.axis_index("c")) + (0,) * own.ndim)


def _sum_slabs(buf, out_dtype, *, name):
    G, R, C = buf.shape
    tr = _pick(R, (512, 256, 128, 64, 32, 16, 8))

    def body(b_ref, o_ref):
        acc = b_ref[0].astype(F32)
        for s in range(1, G):
            acc = acc + b_ref[s].astype(F32)
        o_ref[...] = acc.astype(out_dtype)

    return pl.pallas_call(body, name=name, grid=(R // tr,), in_specs=[pl.BlockSpec((G, tr, C), lambda i: (0, i, 0))],
                          out_specs=pl.BlockSpec((tr, C), lambda i: (i, 0)), out_shape=jax.ShapeDtypeStruct((R, C), out_dtype),
                          compiler_params=_cparams(("parallel",)))(buf)


def _rms_fwd_fn(h, g):
    return (_rms(h, g),)


def _rms_bwd_fn(h, dhn, dres, g):
    _, vjp = jax.vjp(_rms, h, g)
    dh, dg = vjp(dhn)
    return dh + dres, dg


def _loss_fn(h, tgt, g):
    y, vjp = jax.vjp(_rms, h, g)
    diff = y - tgt
    dh, dg = vjp(diff * (1.0 / D_MODEL))
    return dh, dg, (0.5 / D_MODEL) * jnp.sum(diff * diff, axis=0, keepdims=True)


def _s5_act(ys, u, d):
    return _gelu(ys + d * u)


def _s5_gate(z, gl, b):
    return z * _sigmoid(gl + b)


def _s5_act_fn(ys, u, d):
    return (_s5_act(ys, u, d),)


def _s5_mix_fn(ya, z, gl, b):
    return (jnp.concatenate([ya, _s5_gate(z, gl, b)], axis=1),)


def _s5_gate_bwd_fn(z, gl, dyb, b):
    _, vjp = jax.vjp(_s5_gate, z, gl, b)
    return vjp(dyb)


def _s5_act_bwd_fn(ys, u, dz1, dz2, d):
    _, vjp = jax.vjp(_s5_act, ys, u, d)
    return vjp(dz1 + dz2)


def _dproj_fn(dq, df, di, dg, du1, du2):
    return (jnp.concatenate([dq, df, di, dg, du1 + du2], axis=1),)


def _rope_pair(r1, r2, pos, freqs):
    ang = pos.astype(F32) * freqs
    c, s = jnp.cos(ang), jnp.sin(ang)
    return r1 * c - r2 * s, r1 * s + r2 * c


_ODD_SPLITS = (0, MLA_Q_RANK, MLA_Q_RANK + MLA_KV_RANK, MLA_Q_RANK + MLA_KV_RANK + LANE, ODD_IN_PAD)


def _mla_prep(cq, ckv, k1, k2, qg, kvg, pos, freqs):
    ko1, ko2 = _rope_pair(k1, k2, pos, freqs)
    return _rms(cq, qg), _rms(ckv, kvg), ko1, ko2


def _mla_prep_fn(proj, pos, qg, kvg, freqs):
    parts = [proj[:, a:b] for a, b in zip(_ODD_SPLITS[:-1], _ODD_SPLITS[1:])]
    return _mla_prep(*parts, qg, kvg, pos, freqs)


def _mla_prep_bwd_fn(proj, pos, dqn, dkvn, dko1, dko2, qg, kvg, freqs):
    parts = [proj[:, a:b] for a, b in zip(_ODD_SPLITS[:-1], _ODD_SPLITS[1:])]
    _, vjp = jax.vjp(lambda *a: _mla_prep(*a, pos, freqs), *parts, qg, kvg)
    dcq, dckv, dk1, dk2, dqg, dkvg = vjp((dqn, dkvn, dko1, dko2))
    return jnp.concatenate([dcq, dckv, dk1, dk2], axis=1), dqg, dkvg


def _rope_q_fn(r1, r2, pos, freqs):
    return _rope_pair(r1, r2, pos, freqs)


def _rope_q_bwd_fn(dqn, do1, do2, pos, freqs):
    dr1, dr2 = _rope_pair(do1, do2, pos, -freqs)
    return (jnp.concatenate([dqn, dr1, dr2], axis=1),)


W_NAMES = ("norm_mix_g", "norm_ffn_g", "final_norm_g", "even_w_in", "hgrn_lb_logits", "hgrn_norm_g", "s5_a_re", "s5_a_im",
           "s5_log_dt", "s5_b_re", "s5_b_im", "s5_c_re", "s5_c_im", "s5_d", "s5_w_glu", "s5_b_glu", "even_w_out", "odd_w_in",
           "mla_q_norm_g", "mla_w_uq", "mla_kv_norm_g", "mla_w_ukv", "odd_w_out", "ffn_w_in", "ffn_conv_w", "ffn_conv_b",
           "ffn_w_out")
BIG_UNITS = (("even_w_in", 0, "col"), ("s5_w_glu", 0, "row"), ("even_w_out", 0, "row"), ("odd_w_in", 0, "row"),
             ("mla_w_uq", 0, "col"), ("mla_w_ukv", 0, "col"), ("odd_w_out", 0, "row"),
             ("ffn_w_in", 0, "col"), ("ffn_w_in", 1, "col"), ("ffn_w_out", 0, "row"), ("ffn_w_out", 1, "row"))
BIG_NAMES = tuple(dict.fromkeys(u[0] for u in BIG_UNITS))
SMALL_SHARDED = (("mla_q_norm_g", 1), ("mla_kv_norm_g", 1), ("ffn_conv_w", 2))
SMALL_SHARDED_NAMES = tuple(s[0] for s in SMALL_SHARDED)
REPLICATED = tuple(n for n in W_NAMES if n not in BIG_NAMES + SMALL_SHARDED_NAMES)


def _pack(flats, cols, row_mult):
    flat = jnp.concatenate(flats, axis=-1)
    pad = (-flat.shape[-1]) % (cols * row_mult)
    flat = jnp.pad(flat, [(0, 0)] * (flat.ndim - 1) + [(0, pad)])
    return flat.reshape(flat.shape[:-1] + (-1, cols))


def _unpack(flat, shapes):
    out, off = [], 0
    for shp in shapes:
        n = int(np.prod(shp))
        out.append(flat[..., off:off + n].reshape(flat.shape[:-1] + tuple(shp)))
        off += n
    return out


UNIT_KIND = {(n, l): kind for n, l, kind in BIG_UNITS}
STAGES = ((("even_w_in", 0), ("s5_w_glu", 0), ("even_w_out", 0)),
          (("ffn_w_in", 0), ("ffn_w_out", 0)),
          (("odd_w_in", 0), ("mla_w_uq", 0), ("mla_w_ukv", 0), ("odd_w_out", 0)),
          (("ffn_w_in", 1), ("ffn_w_out", 1)))


def _gather_start(w, stage, with_small, after):
    srcs = [w[n][l].astype(BF16) for n, l in STAGES[stage]]
    if with_small:
        srcs.append(_pack([w[n].reshape(-1) for n in SMALL_SHARDED_NAMES], LANE, SUBLANE))
    return _exchange_start(srcs, scatter=False, name=f"gather_start_{stage}", after=after)


def _gather_finish(started, after, w, stage, with_small):
    srcs, lands = _exchange_wait(started, after, scatter=False, name=f"gather_wait_{stage}")
    lands = [_with_own(land, src) for land, src in zip(lands, srcs)]
    big = {}
    for unit, g in zip(STAGES[stage], lands):
        r, c = g.shape[2:]
        big[unit] = g.reshape(N_DEV * r, c) if UNIT_KIND[unit] == "row" else g.transpose(2, 0, 1, 3).reshape(r, N_DEV * c)
    if not with_small:
        return big
    parts = _unpack(lands[-1].reshape(N_DEV, -1), [w[n].shape for n in SMALL_SHARDED_NAMES])
    small = {}
    for (n, ax), p in zip(SMALL_SHARDED, parts):
        shp = list(w[n].shape)
        shp[ax] *= N_DEV
        small[n] = jnp.moveaxis(p, 0, ax).reshape(shp)
    return big, small


def _scatter_start(g_big, stage, extra=()):
    srcs = []
    for unit in STAGES[stage]:
        g = g_big[unit].astype(BF16)
        if UNIT_KIND[unit] == "row":
            srcs.append(g.reshape(N_CHIPS, N_CORES, g.shape[0] // N_DEV, g.shape[1]))
        else:
            srcs.append(g.reshape(g.shape[0], N_CHIPS, N_CORES, g.shape[1] // N_DEV).transpose(1, 2, 0, 3))
    return _exchange_start(srcs + list(extra), scatter=True, name=f"scatter_start_{stage}")


def _scatter_finish(started, after, stage):
    srcs, lands = _exchange_wait(started, after, scatter=True, name=f"scatter_wait_{stage}")
    me_chip, c = 2 * lax.axis_index("x") + lax.axis_index("y"), lax.axis_index("c")
    outs = []
    for land, src in zip(lands, srcs):
        own = lax.dynamic_slice(src, (me_chip, c) + (0,) * (src.ndim - 2), (1, 1) + src.shape[2:])[0, 0]
        outs.append(_with_own(land, own).reshape((N_DEV,) + land.shape[2:]))
    return outs


def _small_grad_packs(g_small, g_repl, w):
    flats = []
    for n, ax in SMALL_SHARDED:
        shp = list(w[n].shape)
        g = g_small[n].astype(F32).reshape(shp[:ax] + [N_DEV] + shp[ax:])
        flats.append(jnp.moveaxis(g, ax, 0).reshape(N_DEV, -1))
    small = _pack(flats, LANE, SUBLANE)
    vec = _pack([g_repl[n].reshape(-1).astype(F32) for n in REPLICATED], LANE, SUBLANE)
    return small.reshape((N_CHIPS, N_CORES) + small.shape[1:]), jnp.broadcast_to(vec, (N_CHIPS, N_CORES) + vec.shape)


def _block_diag(blocks):
    G, a, b = blocks.shape
    return jnp.einsum('gab,gk->gakb', blocks, jnp.eye(G, dtype=blocks.dtype)).reshape(G * a, G * b)


def _diag_blocks(mat, a, b):
    G = mat.shape[0] // a
    return jnp.einsum('gagb->gab', mat.reshape(G, a, G, b))


def _ffn_fwd(h, g, w_in, conv_w, conv_b, w_out, tag):
    hn, = _rows(_rms_fwd_fn, [h], [g], [(D_MODEL, BF16)], [], name=f"ffn{tag}_norm")
    au = _mm(hn, w_in, name=f"ffn{tag}_in")
    z = _ffn_mid_fwd(au, conv_w, conv_b, name=f"ffn{tag}_mid")
    return _mm(z, w_out, res=h, name=f"ffn{tag}_out"), (hn, au, z)


def _ffn_bwd(h, dh, saved, g, w_in, conv_w, conv_b, w_out, tag, deps=()):
    hn, au, z = saved
    dz = _mm(dh, w_out, tb=True, deps=deps, name=f"ffn{tag}_dz")
    dw_out = _mm(z, dh, ta=True, out_dtype=BF16, name=f"ffn{tag}_dwout")
    da, du, dcw, dcb = _ffn_mid_bwd(au, dz, conv_w, conv_b, name=f"ffn{tag}_dmid")
    dhn = _mm(da, w_in, tb=True, b_cols=(0, D_FF), name=f"ffn{tag}_dhn_a")
    dhn = _mm(du, w_in, tb=True, b_cols=(D_FF, D_FF), res=dhn, name=f"ffn{tag}_dhn_u")
    dw_in = jnp.concatenate([_mm(hn, da, ta=True, out_dtype=BF16, name=f"ffn{tag}_dwin_a"),
                             _mm(hn, du, ta=True, out_dtype=BF16, name=f"ffn{tag}_dwin_u")], axis=1)
    dh_in, dg = _rows(_rms_bwd_fn, [h, dhn, dh], [g], [(D_MODEL, F32)], [(1, D_MODEL)], name=f"ffn{tag}_dnorm")
    return dh_in, dict(g=dg, w_in=dw_in, conv_w=dcw, conv_b=dcb, w_out=dw_out)


def kernel(x, positions, norm_mix_g, norm_ffn_g, final_norm_g, even_w_in, hgrn_lb_logits, hgrn_norm_g, s5_a_re, s5_a_im, s5_log_dt, s5_b_re, s5_b_im, s5_c_re, s5_c_im, s5_d, s5_w_glu, s5_b_glu, even_w_out, odd_w_in, mla_q_norm_g, mla_w_uq, mla_kv_norm_g, mla_w_ukv, odd_w_out, ffn_w_in, ffn_conv_w, ffn_conv_b, ffn_w_out, loss_target, m_norm_mix_g, m_norm_ffn_g, m_final_norm_g, m_even_w_in, m_hgrn_lb_logits, m_hgrn_norm_g, m_s5_a_re, m_s5_a_im, m_s5_log_dt, m_s5_b_re, m_s5_b_im, m_s5_c_re, m_s5_c_im, m_s5_d, m_s5_w_glu, m_s5_b_glu, m_even_w_out, m_odd_w_in, m_mla_q_norm_g, m_mla_w_uq, m_mla_kv_norm_g, m_mla_w_ukv, m_odd_w_out, m_ffn_w_in, m_ffn_conv_w, m_ffn_conv_b, m_ffn_w_out, v_norm_mix_g, v_norm_ffn_g, v_final_norm_g, v_even_w_in, v_hgrn_lb_logits, v_hgrn_norm_g, v_s5_a_re, v_s5_a_im, v_s5_log_dt, v_s5_b_re, v_s5_b_im, v_s5_c_re, v_s5_c_im, v_s5_d, v_s5_w_glu, v_s5_b_glu, v_even_w_out, v_odd_w_in, v_mla_q_norm_g, v_mla_w_uq, v_mla_kv_norm_g, v_mla_w_ukv, v_odd_w_out, v_ffn_w_in, v_ffn_conv_w, v_ffn_conv_b, v_ffn_w_out):
    given = dict(locals())
    w = {n: given[n] for n in W_NAMES}
    mom = {n: given["m_" + n] for n in W_NAMES}
    var = {n: given["v_" + n] for n in W_NAMES}
    T = x.shape[1]
    h0 = x[0]
    tgt = loss_target[0]
    pos = positions.reshape(T, 1)

    gathers = []
    for s in range(len(STAGES)):
        gathers.append(_gather_start(w, s, with_small=(s == 0), after=[g[4] for g in gathers[-1:]]))
    half = MLA_ROPE // 2
    kr0 = MLA_Q_RANK + MLA_KV_RANK
    freqs = ROPE_THETA ** (-jnp.arange(0, MLA_ROPE, 2, dtype=F32) / MLA_ROPE)
    freqs_q = jnp.tile(freqs, MLA_HEADS)[None, :]
    freqs_k = jnp.concatenate([freqs, jnp.zeros((LANE - half,), F32)])[None, :]

    sp_in = (s5_a_re[0], s5_a_im[0], s5_log_dt[0][:, None], s5_b_re[0].transpose(0, 2, 1), s5_b_im[0].transpose(0, 2, 1),
             hgrn_lb_logits)
    abr, abi, bbt_re, bbt_im, lb0 = _s5_params_fwd(*sp_in)
    a_re, a_im = abr.reshape(1, S5_WIDTH), abi.reshape(1, S5_WIDTH)
    bb_re, bb_im = _block_diag(bbt_re).astype(BF16), _block_diag(bbt_im).astype(BF16)
    bb_cat = jnp.concatenate([bb_re, bb_im], axis=1)
    c_re = _block_diag(s5_c_re[0].transpose(0, 2, 1)).astype(BF16)
    c_im_neg = _block_diag(-s5_c_im[0].transpose(0, 2, 1)).astype(BF16)
    u_cols = (4 * HGRN_DIM, S5_DIM)

    hn0, = _rows(_rms_fwd_fn, [h0], [norm_mix_g[0:1]], [(D_MODEL, BF16)], [], name="mix0_norm", deps=[gathers[-1][4]])
    full, full_small = _gather_finish(gathers[0], hn0, w, 0, True)
    w_ein, w_glu, w_eout = full["even_w_in", 0], full["s5_w_glu", 0], full["even_w_out", 0]
    qg, kvg, conv_w = full_small["mla_q_norm_g"], full_small["mla_kv_norm_g"], full_small["ffn_conv_w"]
    proj = _mm(hn0, w_ein, name="even_in")
    y_a, states = _hgrn_fwd(proj, lb0, hgrn_norm_g, name="hgrn_fwd")
    bu = _mm(proj, bb_cat, a_cols=u_cols, name="s5_bu")
    s_re, s_im = _s5_scan_fwd(bu, a_re, a_im, name="s5_scan_fwd")
    ys = _mm(s_im, c_im_neg, res=_mm(s_re, c_re, name="s5_y_re"), name="s5_y_im")
    z5, = _rows(_s5_act_fn, [ys, (proj,) + u_cols], [s5_d], [(S5_DIM, F32)], [], name="s5_act")
    gl = _mm(z5, w_glu, name="s5_glu")
    mixin, = _rows(_s5_mix_fn, [y_a, z5, gl], [s5_b_glu], [(D_MODEL, BF16)], [], name="s5_mix")
    h1 = _mm(mixin, w_eout, res=h0, name="even_out")
    full.update(_gather_finish(gathers[1], h1, w, 1, False))
    w_fin, w_fout = [full["ffn_w_in", 0]], [full["ffn_w_out", 0]]
    h2, ffn0_saved = _ffn_fwd(h1, norm_ffn_g[0:1], w_fin[0], conv_w[0], ffn_conv_b[0:1], w_fout[0], 0)

    full.update(_gather_finish(gathers[2], h2, w, 2, False))
    w_oin, w_ukv, w_oout = full["odd_w_in", 0], full["mla_w_ukv", 0], full["odd_w_out", 0]
    zpad = jnp.zeros((D_MODEL, LANE - half), BF16)
    w_oin_pad = jnp.concatenate([w_oin[:, :kr0], w_oin[:, kr0:kr0 + half], zpad, w_oin[:, kr0 + half:], zpad], axis=1)
    w_uq3 = full["mla_w_uq", 0].reshape(MLA_Q_RANK, MLA_HEADS, MLA_QK)
    w_uq_perm = jnp.concatenate([w_uq3[:, :, :MLA_NOPE].reshape(MLA_Q_RANK, -1),
                                 w_uq3[:, :, MLA_NOPE:MLA_NOPE + half].reshape(MLA_Q_RANK, -1),
                                 w_uq3[:, :, MLA_NOPE + half:].reshape(MLA_Q_RANK, -1)], axis=1)
    hn1, = _rows(_rms_fwd_fn, [h2], [norm_mix_g[1:2]], [(D_MODEL, BF16)], [], name="mix1_norm")
    proj_o = _mm(hn1, w_oin_pad, name="odd_in")
    qn, kvn, ko1, ko2 = _rows(_mla_prep_fn, [proj_o, pos], [qg, kvg, freqs_k],
                              [(MLA_Q_RANK, BF16), (MLA_KV_RANK, BF16), (LANE, F32), (LANE, F32)], [], name="mla_prep")
    q_all = _mm(qn, w_uq_perm, name="mla_uq")
    kv = _mm(kvn, w_ukv, name="mla_ukv")
    nope_w = MLA_HEADS * MLA_NOPE
    rope_w = MLA_HEADS * half
    o1, o2 = _rows(_rope_q_fn, [(q_all, nope_w, rope_w), (q_all, nope_w + rope_w, rope_w), pos], [freqs_q],
                   [(rope_w, F32), (rope_w, F32)], [], name="mla_rope_q")
    q_rope = jnp.concatenate([o1.reshape(T, MLA_HEADS, half), o2.reshape(T, MLA_HEADS, half)], axis=2).transpose(1, 0, 2)
    k_rope = jnp.concatenate([ko1[:, :half], ko2[:, :half]], axis=1)
    o, lse = _attn_fwd(q_all, q_rope, kv, k_rope, name="attn_fwd")
    h3 = _mm(o, w_oout, res=h2, name="odd_out")
    full.update(_gather_finish(gathers[3], h3, w, 3, False))
    w_fin.append(full["ffn_w_in", 1])
    w_fout.append(full["ffn_w_out", 1])
    h4, ffn1_saved = _ffn_fwd(h3, norm_ffn_g[1:2], w_fin[1], conv_w[1], ffn_conv_b[1:2], w_fout[1], 1)

    dh4, d_final_g, loss_cols = _rows(_loss_fn, [h4, tgt], [final_norm_g[None, :]], [(D_MODEL, F32)],
                                      [(1, D_MODEL), (1, D_MODEL)], name="loss_head")
    loss = lax.psum(jnp.sum(loss_cols), ("x", "y", "c"))

    dh3, gf1 = _ffn_bwd(h3, dh4, ffn1_saved, norm_ffn_g[1:2], w_fin[1], conv_w[1], ffn_conv_b[1:2], w_fout[1], 1)
    scatters = {3: _scatter_start({("ffn_w_in", 1): gf1["w_in"], ("ffn_w_out", 1): gf1["w_out"]}, 3)}
    do = _mm(dh3, w_oout, tb=True, deps=[scatters[3][4]], name="odd_out_dx")
    d_w_oout = _mm(o, dh3, ta=True, out_dtype=BF16, name="odd_out_dw")
    dq_nope, dq_rope, dkv, dk_rope = _attn_bwd(q_all, q_rope, kv, k_rope, o, lse, do, name="attn_bwd")
    dq_rope_t = dq_rope.transpose(1, 0, 2)
    do1, do2 = dq_rope_t[:, :, :half].reshape(T, rope_w), dq_rope_t[:, :, half:].reshape(T, rope_w)
    lane_pad = ((0, 0), (0, LANE - half))
    dko1, dko2 = jnp.pad(dk_rope[:, :half], lane_pad), jnp.pad(dk_rope[:, half:], lane_pad)
    dq_all, = _rows(_rope_q_bwd_fn, [dq_nope, do1, do2, pos], [freqs_q], [(MLA_HEADS * MLA_QK, BF16)], [], name="mla_rope_q_bwd")
    d_w_uq_perm = _mm(qn, dq_all, ta=True, out_dtype=BF16, name="mla_uq_dw")
    dqn = _mm(dq_all, w_uq_perm, tb=True, name="mla_uq_dx")
    d_w_ukv = _mm(kvn, dkv, ta=True, out_dtype=BF16, name="mla_ukv_dw")
    dkvn = _mm(dkv, w_ukv, tb=True, name="mla_ukv_dx")
    dproj_o, d_qg, d_kvg = _rows(_mla_prep_bwd_fn, [proj_o, pos, dqn, dkvn, dko1, dko2], [qg, kvg, freqs_k],
                                 [(ODD_IN_PAD, BF16)], [(1, MLA_Q_RANK), (1, MLA_KV_RANK)], name="mla_prep_bwd")
    d_w_oin_pad = _mm(hn1, dproj_o, ta=True, out_dtype=BF16, name="odd_in_dw")
    dhn1 = _mm(dproj_o, w_oin_pad, tb=True, name="odd_in_dx")
    dh2, d_mix_g1 = _rows(_rms_bwd_fn, [h2, dhn1, dh3], [norm_mix_g[1:2]], [(D_MODEL, F32)], [(1, D_MODEL)], name="mix1_dnorm")
    d_w_oin = jnp.concatenate([d_w_oin_pad[:, :kr0 + half], d_w_oin_pad[:, kr0 + LANE:kr0 + LANE + half]], axis=1)
    d3 = d_w_uq_perm
    d_w_uq = jnp.concatenate([d3[:, :nope_w].reshape(MLA_Q_RANK, MLA_HEADS, MLA_NOPE),
                              d3[:, nope_w:nope_w + rope_w].reshape(MLA_Q_RANK, MLA_HEADS, half),
                              d3[:, nope_w + rope_w:].reshape(MLA_Q_RANK, MLA_HEADS, half)], axis=2).reshape(MLA_Q_RANK, -1)
    scatters[2] = _scatter_start({("odd_w_in", 0): d_w_oin, ("mla_w_uq", 0): d_w_uq, ("mla_w_ukv", 0): d_w_ukv,
                                  ("odd_w_out", 0): d_w_oout}, 2)

    dh1, gf0 = _ffn_bwd(h1, dh2, ffn0_saved, norm_ffn_g[0:1], w_fin[0], conv_w[0], ffn_conv_b[0:1], w_fout[0], 0,
                        deps=[scatters[2][4]])
    scatters[1] = _scatter_start({("ffn_w_in", 0): gf0["w_in"], ("ffn_w_out", 0): gf0["w_out"]}, 1)
    dmix = _mm(dh1, w_eout, tb=True, deps=[scatters[1][4]], name="even_out_dx")
    d_w_eout = _mm(mixin, dh1, ta=True, out_dtype=BF16, name="even_out_dw")
    dq, df, di, dg, d_lb0, d_hgrn_g = _hgrn_bwd(proj, lb0, hgrn_norm_g, states, dmix, name="hgrn_bwd")
    dz1, dgl, d_b_glu = _rows(_s5_gate_bwd_fn, [z5, gl, (dmix, HGRN_DIM, S5_DIM)], [s5_b_glu],
                              [(S5_DIM, F32), (S5_DIM, BF16)], [(1, S5_DIM)], name="s5_gate_bwd")
    dz2 = _mm(dgl, w_glu, tb=True, name="s5_glu_dx")
    d_w_glu = _mm(z5, dgl, ta=True, out_dtype=BF16, name="s5_glu_dw")
    dys, du1, d_s5_d = _rows(_s5_act_bwd_fn, [ys, (proj,) + u_cols, dz1, dz2], [s5_d],
                             [(S5_DIM, BF16), (S5_DIM, F32)], [(1, S5_DIM)], name="s5_act_bwd")
    ds_re = _mm(dys, c_re, tb=True, name="s5_ds_re")
    ds_im = _mm(dys, c_im_neg, tb=True, name="s5_ds_im")
    d_c_re = _mm(s_re, dys, ta=True, name="s5_dc_re")
    d_c_im_neg = _mm(s_im, dys, ta=True, name="s5_dc_im")
    lam_re, lam_im, d_ar, d_ai = _s5_scan_bwd(ds_re, ds_im, s_re, s_im, a_re, a_im, name="s5_scan_bwd")
    du2 = _mm(lam_im, bb_im, tb=True, res=_mm(lam_re, bb_re, tb=True, name="s5_du_re"), name="s5_du_im")
    d_bb_re = _mm(proj, lam_re, ta=True, a_cols=u_cols, name="s5_dbb_re")
    d_bb_im = _mm(proj, lam_im, ta=True, a_cols=u_cols, name="s5_dbb_im")
    dproj, = _rows(_dproj_fn, [dq, df, di, dg, du1, du2], [], [(EVEN_IN, BF16)], [], name="even_dproj")
    d_w_ein = _mm(hn0, dproj, ta=True, out_dtype=BF16, name="even_in_dw")
    dhn0 = _mm(dproj, w_ein, tb=True, name="even_in_dx")
    grad_x, d_mix_g0 = _rows(_rms_bwd_fn, [h0, dhn0, dh1], [norm_mix_g[0:1]], [(D_MODEL, F32)], [(1, D_MODEL)], name="mix0_dnorm")
    sp_g = _s5_params_bwd(*sp_in, d_ar.sum(0).reshape(S5_GROUPS, S5_STATE), d_ai.sum(0).reshape(S5_GROUPS, S5_STATE),
                          _diag_blocks(d_bb_re, S5_GROUP, S5_STATE), _diag_blocks(d_bb_im, S5_GROUP, S5_STATE), d_lb0)
    d_a_re, d_a_im, d_log_dt, d_bt_re, d_bt_im, d_lb_logits = sp_g

    g_small = dict(mla_q_norm_g=d_qg, mla_kv_norm_g=d_kvg, ffn_conv_w=jnp.stack([gf0["conv_w"], gf1["conv_w"]]))
    g_repl = dict(
        norm_mix_g=jnp.concatenate([d_mix_g0, d_mix_g1]), norm_ffn_g=jnp.concatenate([gf0["g"], gf1["g"]]),
        final_norm_g=d_final_g[0], hgrn_lb_logits=d_lb_logits, hgrn_norm_g=d_hgrn_g,
        s5_a_re=d_a_re[None], s5_a_im=d_a_im[None], s5_log_dt=d_log_dt[:, 0][None],
        s5_b_re=d_bt_re.transpose(0, 2, 1)[None], s5_b_im=d_bt_im.transpose(0, 2, 1)[None],
        s5_c_re=_diag_blocks(d_c_re, S5_STATE, S5_GROUP).transpose(0, 2, 1)[None],
        s5_c_im=-_diag_blocks(d_c_im_neg, S5_STATE, S5_GROUP).transpose(0, 2, 1)[None],
        s5_d=d_s5_d, s5_b_glu=d_b_glu, ffn_conv_b=jnp.concatenate([gf0["conv_b"], gf1["conv_b"]]))
    scatters[0] = _scatter_start({("even_w_in", 0): d_w_ein, ("s5_w_glu", 0): d_w_glu, ("even_w_out", 0): d_w_eout}, 0,
                                 extra=_small_grad_packs(g_small, g_repl, w))

    delta, new_m, new_v = {}, {}, {}
    per_unit = {}
    after = grad_x
    for stage in (3, 2, 1, 0):
        partial = _scatter_finish(scatters[stage], after, stage)
        for (n, l), slabs in zip(STAGES[stage], partial):
            per_unit[n, l] = _adamw(w[n][l], slabs, mom[n][l], var[n][l], name=f"adamw_{n}_{l}")
        after = per_unit[STAGES[stage][-1]][0]
    repl_sum = _sum_slabs(partial[-1], F32, name="sum_g_repl").reshape(-1)
    grads = dict(zip(REPLICATED, _unpack(repl_sum, [w[n].shape for n in REPLICATED])))
    small_sum = _sum_slabs(partial[-2], F32, name="sum_g_small").reshape(-1)
    grads.update(zip(SMALL_SHARDED_NAMES, _unpack(small_sum, [w[n].shape for n in SMALL_SHARDED_NAMES])))
    for n in BIG_NAMES:
        layers = [per_unit[n, l] for l in range(w[n].shape[0])]
        grads[n], delta[n], new_m[n], new_v[n] = (jnp.stack([lay[k] for lay in layers]) for k in range(4))
    small = [n for n in W_NAMES if n not in BIG_NAMES]
    packed = [_pack([t[n].reshape(-1) for n in small], LANE, SUBLANE) for t in (w, grads, mom, var)]
    outs = _adamw(*packed, name="adamw_small")
    small_shapes = [w[n].shape for n in small]
    for dst, o_ in zip((delta, new_m, new_v), outs):
        dst.update(zip(small, _unpack(o_.reshape(-1), small_shapes)))

    return (loss, grad_x[None], *[grads[n] for n in W_NAMES], *[delta[n] for n in W_NAMES],
            *[new_m[n] for n in W_NAMES], *[new_v[n] for n in W_NAMES])
```

```python
import functools
import math

import numpy as np
import jax
import jax.numpy as jnp
from jax import lax
from jax.experimental import pallas as pl
from jax.experimental.pallas import tpu as pltpu

F32 = jnp.float32
BF16 = jnp.bfloat16
_MXU_DTYPE = jnp.bfloat16

D_MODEL = 1024
HGRN_DIM = 512
HGRN_HEAD_DIM = 128
HGRN_HEADS = 4
HGRN_CHUNK = 64
S5_DIM = 512
S5_GROUPS = 32
S5_GROUP = 16
S5_STATE = 64
S5_WIDTH = S5_GROUPS * S5_STATE
EVEN_IN = 4 * HGRN_DIM + S5_DIM
MLA_HEADS = 8
MLA_Q_RANK = 384
MLA_KV_RANK = 256
MLA_NOPE = 128
MLA_ROPE = 64
MLA_V = 128
MLA_QK = MLA_NOPE + MLA_ROPE
ODD_IN = MLA_Q_RANK + MLA_KV_RANK + MLA_ROPE
ODD_IN_PAD = MLA_Q_RANK + MLA_KV_RANK + 2 * 128
ROPE_THETA = 10000.0
D_FF = 2816
EPS = 1e-6
ADAM_LR = 0.001
ADAM_B1 = 0.9
ADAM_B2 = 0.999
ADAM_EPS = 1e-08
ADAM_WD = 0.01
ADAM_STEP = 10

N_DEV = 8
LANE = 128
SUBLANE = 8
VMEM_LIMIT_BYTES = 56 * 1024 * 1024
MESH = pl.DeviceIdType.MESH


def _cparams(sem=None):
    return pltpu.CompilerParams(dimension_semantics=sem, vmem_limit_bytes=VMEM_LIMIT_BYTES)


def _pick(n, cands):
    for c in cands:
        if n % c == 0:
            return c
    raise ValueError(f"no tile for {n} in {cands}")


def _sigmoid(x):
    return 1.0 / (1.0 + jnp.exp(-x))


def _silu(x):
    return x * _sigmoid(x)


def _gelu(x):
    return 0.5 * x * (1.0 + jnp.tanh(math.sqrt(2.0 / math.pi) * (x + 0.044715 * (x * x * x))))


def _rms(x, g):
    return x * lax.rsqrt(jnp.mean(x * x, axis=-1, keepdims=True) + EPS) * g


def _mxu(a, b, ca, cb):
    return lax.dot_general(a.astype(_MXU_DTYPE), b.astype(_MXU_DTYPE), (((ca,), (cb,)), ((), ())),
                           preferred_element_type=F32)


@functools.partial(jax.custom_vjp, nondiff_argnums=(2, 3))
def _mxu_ad(a, b, ca, cb):
    return _mxu(a, b, ca, cb)


def _mxu_ad_fwd(a, b, ca, cb):
    return _mxu(a, b, ca, cb), (a, b)


def _mxu_ad_bwd(ca, cb, saved, g):
    a, b = saved
    fa, fb = 1 - ca, 1 - cb
    da = _mxu(g, b, 1, fb) if ca == 1 else _mxu(b, g, fb, 1)
    db = _mxu(a, g, fa, 0) if cb == 0 else _mxu(g, a, 0, fa)
    return da, db


_mxu_ad.defvjp(_mxu_ad_fwd, _mxu_ad_bwd)


def _tri(n):
    row = lax.broadcasted_iota(jnp.int32, (n, n), 0)
    col = lax.broadcasted_iota(jnp.int32, (n, n), 1)
    return col <= row


def _cumsum_rows(x, reverse=False):
    n = x.shape[0]
    rowi = lax.broadcasted_iota(jnp.int32, x.shape, 0)
    s = 1
    while s < n:
        if reverse:
            x = x + jnp.where(rowi < n - s, pltpu.roll(x, n - s, 0), 0.0)
        else:
            x = x + jnp.where(rowi >= s, pltpu.roll(x, s, 0), 0.0)
        s *= 2
    return x


@jax.custom_vjp
def _cumsum_rows_ad(x):
    return _cumsum_rows(x)


def _cumsum_rows_ad_fwd(x):
    return _cumsum_rows(x), None


def _cumsum_rows_ad_bwd(_, g):
    return (_cumsum_rows(g, reverse=True),)


_cumsum_rows_ad.defvjp(_cumsum_rows_ad_fwd, _cumsum_rows_ad_bwd)


MM_VMEM_BUDGET = 36 * 1024 * 1024
MM_MAX_TILE = 1408


def _lane_divisors(n, cap, offs=()):
    return [d for d in range(min(n, cap) // LANE * LANE, 0, -LANE) if n % d == 0 and all(o % d == 0 for o in offs)]


def _mm_tiles(M, N, K, sa, sb, so, has_res, m_offs, n_offs, k_offs):
    best = None
    for tm in _lane_divisors(M, MM_MAX_TILE, m_offs):
        for tn in _lane_divisors(N, MM_MAX_TILE, n_offs):
            for tk in _lane_divisors(K, K, k_offs):
                nk = K // tk
                vmem = 2 * (tm * tk * sa + tk * tn * sb + tm * tn * so + tm * tn * 4 * has_res) + (tm * tn * 4 if nk > 1 else 0)
                if vmem <= MM_VMEM_BUDGET:
                    key = (-nk, tm * tn, tn)
                    if best is None or key > best[0]:
                        best = (key, tm, tn, tk)
                    break
    return best[1:]


def _mm(a, b, *, ta=False, tb=False, res=None, out_dtype=F32, a_cols=None, b_cols=None, deps=(), name):
    a_minor = a.shape[1] if a_cols is None else a_cols[1]
    b_minor = b.shape[1] if b_cols is None else b_cols[1]
    K, M = (a.shape[0], a_minor) if ta else (a_minor, a.shape[0])
    N = b.shape[0] if tb else b_minor
    assert (b_minor if tb else b.shape[0]) == K, (a.shape, b.shape, ta, tb)
    a_off = 0 if a_cols is None else a_cols[0]
    b_off = 0 if b_cols is None else b_cols[0]
    has_res = res is not None
    tm, tn, tk = _mm_tiles(M, N, K, a.dtype.itemsize, b.dtype.itemsize, jnp.dtype(out_dtype).itemsize, has_res,
                           (a_off,) if ta else (), () if tb else (b_off,), ((a_off,) if not ta else ()) + ((b_off,) if tb else ()))
    nk = K // tk
    am, ak = (a_off // tm, 0) if ta else (0, a_off // tk)
    bn, bk = (0, b_off // tk) if tb else (b_off // tn, 0)
    a_spec = pl.BlockSpec((tk, tm), lambda i, j, k: (k, i + am)) if ta else pl.BlockSpec((tm, tk), lambda i, j, k: (i, k + ak))
    b_spec = pl.BlockSpec((tn, tk), lambda i, j, k: (j, k + bk)) if tb else pl.BlockSpec((tk, tn), lambda i, j, k: (k, j + bn))
    o_spec = pl.BlockSpec((tm, tn), lambda i, j, k: (i, j))
    ca, cb = (0 if ta else 1), (1 if tb else 0)

    n_fixed = 2 + has_res + len(deps)

    def body(*refs):
        a_ref, b_ref = refs[0], refs[1]
        res_ref = refs[2] if has_res else None
        o_ref = refs[n_fixed]
        part = _mxu(a_ref[...], b_ref[...], ca, cb)
        if nk == 1:
            o_ref[...] = (part + res_ref[...] if has_res else part).astype(out_dtype)
            return
        acc_ref = refs[n_fixed + 1]
        k = pl.program_id(2)

        @pl.when(k == 0)
        def _():
            acc_ref[...] = part

        @pl.when(k > 0)
        def _():
            acc_ref[...] += part

        @pl.when(k == nk - 1)
        def _():
            o_ref[...] = (acc_ref[...] + res_ref[...] if has_res else acc_ref[...]).astype(out_dtype)

    ins = [a, b] + ([res] if has_res else []) + list(deps)
    in_specs = [a_spec, b_spec] + ([o_spec] if has_res else []) + [pl.BlockSpec(memory_space=pl.ANY)] * len(deps)
    return pl.pallas_call(
        body, name=name, grid=(M // tm, N // tn, nk),
        in_specs=in_specs, out_specs=o_spec,
        out_shape=jax.ShapeDtypeStruct((M, N), out_dtype),
        scratch_shapes=[pltpu.VMEM((tm, tn), F32)] if nk > 1 else [],
        compiler_params=_cparams(("parallel", "parallel", "arbitrary")),
    )(*ins)


def _rows(fn, row_ins, const_ins, row_outs, acc_outs, *, name, tm=256, deps=()):
    norm = [(r, 0, r.shape[1]) if not isinstance(r, tuple) else r for r in row_ins]
    T = norm[0][0].shape[0]
    tm = min(tm, T)
    nr, nc, no, na = len(norm), len(const_ins), len(row_outs), len(acc_outs)
    first_out = nr + nc + len(deps)

    def body(*refs):
        i = pl.program_id(0)
        vals = [r[...] for r in refs[:nr + nc]]
        outs = fn(*vals)
        for o_ref, o in zip(refs[first_out:first_out + no], outs[:no]):
            o_ref[...] = o.astype(o_ref.dtype)
        for a_ref, o in zip(refs[first_out + no:], outs[no:]):
            @pl.when(i == 0)
            def _(a_ref=a_ref, o=o):
                a_ref[...] = o

            @pl.when(i > 0)
            def _(a_ref=a_ref, o=o):
                a_ref[...] += o

    in_specs = []
    for arr, off, w in norm:
        assert off % w == 0, (off, w)
        in_specs.append(pl.BlockSpec((tm, w), lambda i, b=off // w: (i, b)))
    for c in const_ins:
        in_specs.append(pl.BlockSpec(c.shape, lambda i: (0, 0)))
    in_specs += [pl.BlockSpec(memory_space=pl.ANY)] * len(deps)
    out_specs = [pl.BlockSpec((tm, w), lambda i: (i, 0)) for w, _ in row_outs]
    out_specs += [pl.BlockSpec(s, lambda i: (0, 0)) for s in acc_outs]
    out_shape = [jax.ShapeDtypeStruct((T, w), dt) for w, dt in row_outs]
    out_shape += [jax.ShapeDtypeStruct(s, F32) for s in acc_outs]
    return pl.pallas_call(
        body, name=name, grid=(T // tm,), in_specs=in_specs, out_specs=out_specs, out_shape=out_shape,
        compiler_params=_cparams(("arbitrary",)),
    )(*[n[0] for n in norm], *const_ins, *deps)


FFN_COL_TILE = LANE
FFN_ROW_CHUNK = 512


def _shift_down(ext, s, rows):
    return pltpu.roll(ext, s, 0)[SUBLANE:SUBLANE + rows]


def _shift_up(ext, s, rows):
    return pltpu.roll(ext, rows + SUBLANE - s, 0)[:rows]


def _ffn_chunks(T):
    r = min(FFN_ROW_CHUNK, T)
    return r, T // r


def _ext_before(ref, c, r):
    if c == 0:
        return jnp.concatenate([jnp.zeros((SUBLANE, ref.shape[1]), F32), ref[0:r, :]], axis=0)
    return ref[c * r - SUBLANE:(c + 1) * r, :]


def _ext_after(ref, c, r, nch):
    if c == nch - 1:
        return jnp.concatenate([ref[c * r:(c + 1) * r, :], jnp.zeros((SUBLANE, ref.shape[1]), F32)], axis=0)
    return ref[c * r:(c + 1) * r + SUBLANE, :]


def _ffn_mid_fwd(au, conv_w, conv_b, *, name):
    T = au.shape[0]
    tc = FFN_COL_TILE
    ncol = D_FF // tc
    r, nch = _ffn_chunks(T)

    def body(a_ref, u_ref, w_ref, b_ref, z_ref):
        w0, w1, w2, bias = w_ref[0:1, :], w_ref[1:2, :], w_ref[2:3, :], b_ref[...]
        for c in range(nch):
            ext = _ext_before(a_ref, c, r)
            pre = w0 * _shift_down(ext, 2, r) + w1 * _shift_down(ext, 1, r) + w2 * ext[SUBLANE:] + bias
            z_ref[c * r:(c + 1) * r, :] = (_silu(pre) * u_ref[c * r:(c + 1) * r, :]).astype(z_ref.dtype)

    return pl.pallas_call(
        body, name=name, grid=(ncol,),
        in_specs=[pl.BlockSpec((T, tc), lambda j: (0, j)), pl.BlockSpec((T, tc), lambda j: (0, j + ncol)),
                  pl.BlockSpec((3, tc), lambda j: (0, j)), pl.BlockSpec((1, tc), lambda j: (0, j))],
        out_specs=pl.BlockSpec((T, tc), lambda j: (0, j)),
        out_shape=jax.ShapeDtypeStruct((T, D_FF), BF16),
        compiler_params=_cparams(("parallel",)),
    )(au, au, conv_w, conv_b)


def _ffn_mid_bwd(au, dz, conv_w, conv_b, *, name):
    T = au.shape[0]
    tc = FFN_COL_TILE
    ncol = D_FF // tc
    r, nch = _ffn_chunks(T)

    def body(a_ref, u_ref, dz_ref, w_ref, b_ref, da_ref, du_ref, dw_ref, db_ref, dpre_ref):
        w0, w1, w2, bias = w_ref[0:1, :], w_ref[1:2, :], w_ref[2:3, :], b_ref[...]
        dw0 = jnp.zeros((1, tc), F32)
        dw1 = jnp.zeros((1, tc), F32)
        dw2 = jnp.zeros((1, tc), F32)
        db = jnp.zeros((1, tc), F32)
        for c in range(nch):
            rows = slice(c * r, (c + 1) * r)
            ext = _ext_before(a_ref, c, r)
            a2, a1, a0 = _shift_down(ext, 2, r), _shift_down(ext, 1, r), ext[SUBLANE:]
            pre = w0 * a2 + w1 * a1 + w2 * a0 + bias
            sg = _sigmoid(pre)
            act = pre * sg
            dzc = dz_ref[rows, :]
            du_ref[rows, :] = (dzc * act).astype(du_ref.dtype)
            dpre = dzc * u_ref[rows, :] * (sg * (1.0 + pre * (1.0 - sg)))
            dpre_ref[rows, :] = dpre
            dw0 += jnp.sum(dpre * a2, axis=0, keepdims=True)
            dw1 += jnp.sum(dpre * a1, axis=0, keepdims=True)
            dw2 += jnp.sum(dpre * a0, axis=0, keepdims=True)
            db += jnp.sum(dpre, axis=0, keepdims=True)
        for c in range(nch):
            ext = _ext_after(dpre_ref, c, r, nch)
            da = w0 * _shift_up(ext, 2, r) + w1 * _shift_up(ext, 1, r) + w2 * ext[:r]
            da_ref[c * r:(c + 1) * r, :] = da.astype(da_ref.dtype)
        dw_ref[0:1, :] = dw0
        dw_ref[1:2, :] = dw1
        dw_ref[2:3, :] = dw2
        db_ref[...] = db

    col = lambda j: (0, j)
    return pl.pallas_call(
        body, name=name, grid=(ncol,),
        in_specs=[pl.BlockSpec((T, tc), col), pl.BlockSpec((T, tc), lambda j: (0, j + ncol)), pl.BlockSpec((T, tc), col),
                  pl.BlockSpec((3, tc), col), pl.BlockSpec((1, tc), col)],
        out_specs=[pl.BlockSpec((T, tc), col), pl.BlockSpec((T, tc), col), pl.BlockSpec((3, tc), col), pl.BlockSpec((1, tc), col)],
        out_shape=[jax.ShapeDtypeStruct((T, D_FF), BF16), jax.ShapeDtypeStruct((T, D_FF), BF16),
                   jax.ShapeDtypeStruct((3, D_FF), F32), jax.ShapeDtypeStruct((1, D_FF), F32)],
        scratch_shapes=[pltpu.VMEM((T, tc), F32)],
        compiler_params=_cparams(("parallel",)),
    )(au, au, dz, conv_w, conv_b)


HGRN_BLOCK = 256


def _hgrn_chunk(dot, cumsum, q, f, i, g, lb, ng, st):
    C = q.shape[0]
    forget = lb + (1.0 - lb) * _sigmoid(f)
    k = 1.0 - forget
    b = cumsum(jnp.log(forget))
    b_last = b[C - 1:C, :]
    qd = q * jnp.exp(b)
    kd = k * jnp.exp(-b)
    att = jnp.where(_tri(C), dot(qd, kd, 1, 1), 0.0)
    o = dot(att, i, 1, 0) + dot(qd, st, 1, 1)
    st_new = st * jnp.exp(b_last) + dot(i, k * jnp.exp(b_last - b), 0, 0)
    on = o * lax.rsqrt(jnp.mean(o * o, axis=-1, keepdims=True) + EPS) * ng
    return on * _silu(g), st_new


def _hgrn_specs(T, rev):
    tb = min(HGRN_BLOCK, T)
    nb = T // tb
    blk = (lambda n: nb - 1 - n) if rev else (lambda n: n)
    hd = HGRN_HEAD_DIM
    proj_specs = [pl.BlockSpec((tb, HGRN_DIM), lambda n, k=k: (blk(n), k)) for k in range(4)]
    vec_spec = pl.BlockSpec((1, HGRN_DIM), lambda n: (0, 0))
    tok_spec = pl.BlockSpec((tb, HGRN_DIM), lambda n: (blk(n), 0))
    st_spec = pl.BlockSpec((HGRN_HEADS, tb // HGRN_CHUNK, hd, hd), lambda n: (0, blk(n), 0, 0))
    return tb, nb, proj_specs, vec_spec, tok_spec, st_spec


def _head_cols(h):
    return slice(h * HGRN_HEAD_DIM, (h + 1) * HGRN_HEAD_DIM)


def _hgrn_fwd(proj, lb, ng, *, name):
    T = proj.shape[0]
    tb, nb, proj_specs, vec_spec, tok_spec, st_spec = _hgrn_specs(T, False)
    nsub = tb // HGRN_CHUNK
    hd = HGRN_HEAD_DIM

    def body(q_ref, f_ref, i_ref, g_ref, lb_ref, ng_ref, y_ref, sts_ref, st_ref):
        @pl.when(pl.program_id(0) == 0)
        def _():
            st_ref[...] = jnp.zeros_like(st_ref)

        st = [st_ref[h] for h in range(HGRN_HEADS)]
        for s in range(nsub):
            rows = slice(s * HGRN_CHUNK, (s + 1) * HGRN_CHUNK)
            for h in range(HGRN_HEADS):
                cols = _head_cols(h)
                sts_ref[h, s] = st[h]
                y, st[h] = _hgrn_chunk(_mxu, _cumsum_rows, q_ref[rows, cols], f_ref[rows, cols], i_ref[rows, cols],
                                       g_ref[rows, cols], lb_ref[:, cols], ng_ref[:, cols], st[h])
                y_ref[rows, cols] = y
        for h in range(HGRN_HEADS):
            st_ref[h] = st[h]

    return pl.pallas_call(
        body, name=name, grid=(nb,),
        in_specs=proj_specs + [vec_spec, vec_spec], out_specs=[tok_spec, st_spec],
        out_shape=[jax.ShapeDtypeStruct((T, HGRN_DIM), F32),
                   jax.ShapeDtypeStruct((HGRN_HEADS, T // HGRN_CHUNK, hd, hd), F32)],
        scratch_shapes=[pltpu.VMEM((HGRN_HEADS, hd, hd), F32)],
        compiler_params=_cparams(("arbitrary",)),
    )(proj, proj, proj, proj, lb, ng)


def _hgrn_bwd(proj, lb, ng, states, dmix, *, name):
    T = proj.shape[0]
    tb, nb, proj_specs, vec_spec, tok_spec, st_spec = _hgrn_specs(T, True)
    nsub = tb // HGRN_CHUNK
    hd = HGRN_HEAD_DIM
    chunk = functools.partial(_hgrn_chunk, _mxu_ad, _cumsum_rows_ad)

    def body(q_ref, f_ref, i_ref, g_ref, lb_ref, ng_ref, sts_ref, dy_ref,
             dq_ref, df_ref, di_ref, dg_ref, dlb_ref, dng_ref, dst_ref):
        @pl.when(pl.program_id(0) == 0)
        def _():
            dst_ref[...] = jnp.zeros_like(dst_ref)
            dlb_ref[...] = jnp.zeros_like(dlb_ref)
            dng_ref[...] = jnp.zeros_like(dng_ref)

        dst = [dst_ref[h] for h in range(HGRN_HEADS)]
        dlb = [jnp.zeros((1, hd), F32)] * HGRN_HEADS
        dng = [jnp.zeros((1, hd), F32)] * HGRN_HEADS
        for s in reversed(range(nsub)):
            rows = slice(s * HGRN_CHUNK, (s + 1) * HGRN_CHUNK)
            for h in range(HGRN_HEADS):
                cols = _head_cols(h)
                _, vjp = jax.vjp(chunk, q_ref[rows, cols], f_ref[rows, cols], i_ref[rows, cols], g_ref[rows, cols],
                                 lb_ref[:, cols], ng_ref[:, cols], sts_ref[h, s])
                dq, df, di, dg, dlb_s, dng_s, dst[h] = vjp((dy_ref[rows, cols], dst[h]))
                dq_ref[rows, cols] = dq
                df_ref[rows, cols] = df
                di_ref[rows, cols] = di
                dg_ref[rows, cols] = dg
                dlb[h] = dlb[h] + dlb_s
                dng[h] = dng[h] + dng_s
        for h in range(HGRN_HEADS):
            dst_ref[h] = dst[h]
            dlb_ref[:, _head_cols(h)] += dlb[h]
            dng_ref[:, _head_cols(h)] += dng[h]

    tok_out = jax.ShapeDtypeStruct((T, HGRN_DIM), F32)
    vec_out = jax.ShapeDtypeStruct((1, HGRN_DIM), F32)
    return pl.pallas_call(
        body, name=name, grid=(nb,),
        in_specs=proj_specs + [vec_spec, vec_spec, st_spec, tok_spec],
        out_specs=[tok_spec] * 4 + [vec_spec, vec_spec],
        out_shape=[tok_out] * 4 + [vec_out, vec_out],
        scratch_shapes=[pltpu.VMEM((HGRN_HEADS, hd, hd), F32)],
        compiler_params=_cparams(("arbitrary",)),
    )(proj, proj, proj, proj, lb, ng, states, dmix)


S5_LANES = 512
S5_ROWS = 512


def _cmul(ar, ai, br, bi):
    return ar * br - ai * bi, ar * bi + ai * br


def _power_table(ar, ai, exps):
    a2 = _cmul(ar, ai, ar, ai)
    a4 = _cmul(*a2, *a2)
    e = exps - 1
    pr = jnp.broadcast_to(ar, exps.shape)
    pi = jnp.broadcast_to(ai, exps.shape)
    for bit, (fr, fi) in enumerate(((ar, ai), a2, a4)):
        nr, ni = _cmul(pr, pi, fr, fi)
        on = ((e >> bit) & 1) == 1
        pr, pi = jnp.where(on, nr, pr), jnp.where(on, ni, pi)
    return pr, pi, a2, a4


def _s5_scan_fwd(bu, a_re, a_im, *, name):
    T = bu.shape[0]
    w, tr = S5_LANES, min(S5_ROWS, T)
    ncol, nt = S5_WIDTH // w, T // tr

    def body(br_ref, bi_ref, ar_ref, ai_ref, sr_ref, si_ref, carry_ref):
        @pl.when(pl.program_id(1) == 0)
        def _():
            carry_ref[...] = jnp.zeros_like(carry_ref)

        ar, ai = ar_ref[...], ai_ref[...]
        rowi = lax.broadcasted_iota(jnp.int32, (SUBLANE, w), 0)
        pr, pi, a2, a4 = _power_table(ar, ai, rowi + 1)

        def tile(i, carry):
            cr, ci = carry
            rows = pl.ds(pl.multiple_of(i * SUBLANE, SUBLANE), SUBLANE)
            xr, xi = br_ref[rows, :], bi_ref[rows, :]
            for s, (fr, fi) in ((1, (ar, ai)), (2, a2), (4, a4)):
                keep = rowi >= s
                zr = jnp.where(keep, pltpu.roll(xr, s, 0), 0.0)
                zi = jnp.where(keep, pltpu.roll(xi, s, 0), 0.0)
                xr, xi = xr + fr * zr - fi * zi, xi + fr * zi + fi * zr
            xr, xi = xr + pr * cr - pi * ci, xi + pr * ci + pi * cr
            sr_ref[rows, :] = xr
            si_ref[rows, :] = xi
            return xr[SUBLANE - 1:SUBLANE, :], xi[SUBLANE - 1:SUBLANE, :]

        cr, ci = lax.fori_loop(0, tr // SUBLANE, tile, (carry_ref[0:1, :], carry_ref[1:2, :]))
        carry_ref[0:1, :] = cr
        carry_ref[1:2, :] = ci

    out = jax.ShapeDtypeStruct((T, S5_WIDTH), F32)
    return pl.pallas_call(
        body, name=name, grid=(ncol, nt),
        in_specs=[pl.BlockSpec((tr, w), lambda j, t: (t, j)), pl.BlockSpec((tr, w), lambda j, t: (t, j + ncol)),
                  pl.BlockSpec((1, w), lambda j, t: (0, j)), pl.BlockSpec((1, w), lambda j, t: (0, j))],
        out_specs=[pl.BlockSpec((tr, w), lambda j, t: (t, j))] * 2,
        out_shape=[out, out],
        scratch_shapes=[pltpu.VMEM((2, w), F32)],
        compiler_params=_cparams(("parallel", "arbitrary")),
    )(bu, bu, a_re, a_im)


def _s5_scan_bwd(g_re, g_im, s_re, s_im, a_re, a_im, *, name):
    T = g_re.shape[0]
    w, tr = S5_LANES, min(S5_ROWS, T)
    ncol, nt = S5_WIDTH // w, T // tr
    ntile = tr // SUBLANE

    def body(gr_ref, gi_ref, sr_ref, si_ref, ar_ref, ai_ref, lr_ref, li_ref, dar_ref, dai_ref, carry_ref):
        @pl.when(pl.program_id(1) == 0)
        def _():
            carry_ref[...] = jnp.zeros_like(carry_ref)
            dar_ref[...] = jnp.zeros_like(dar_ref)
            dai_ref[...] = jnp.zeros_like(dai_ref)

        ar, ai = ar_ref[...], -ai_ref[...]
        rowi = lax.broadcasted_iota(jnp.int32, (SUBLANE, w), 0)
        pr, pi, a2, a4 = _power_table(ar, ai, SUBLANE - rowi)
        last = rowi == SUBLANE - 1

        def tile(i, carry):
            cr, ci, dar, dai = carry
            rows = pl.ds(pl.multiple_of((ntile - 1 - i) * SUBLANE, SUBLANE), SUBLANE)
            xr, xi = gr_ref[rows, :], gi_ref[rows, :]
            for s, (fr, fi) in ((1, (ar, ai)), (2, a2), (4, a4)):
                keep = rowi < SUBLANE - s
                zr = jnp.where(keep, pltpu.roll(xr, SUBLANE - s, 0), 0.0)
                zi = jnp.where(keep, pltpu.roll(xi, SUBLANE - s, 0), 0.0)
                xr, xi = xr + fr * zr - fi * zi, xi + fr * zi + fi * zr
            xr, xi = xr + pr * cr - pi * ci, xi + pr * ci + pi * cr
            lr_ref[rows, :] = xr
            li_ref[rows, :] = xi
            nr = jnp.where(last, cr, pltpu.roll(xr, SUBLANE - 1, 0))
            ni = jnp.where(last, ci, pltpu.roll(xi, SUBLANE - 1, 0))
            sr, si = sr_ref[rows, :], si_ref[rows, :]
            return xr[0:1, :], xi[0:1, :], dar + nr * sr + ni * si, dai + ni * sr - nr * si

        cr, ci, dar, dai = lax.fori_loop(
            0, ntile, tile, (carry_ref[0:1, :], carry_ref[1:2, :], jnp.zeros((SUBLANE, w), F32), jnp.zeros((SUBLANE, w), F32)))
        carry_ref[0:1, :] = cr
        carry_ref[1:2, :] = ci
        dar_ref[...] += dar
        dai_ref[...] += dai

    tok = pl.BlockSpec((tr, w), lambda j, t: (nt - 1 - t, j))
    vec = pl.BlockSpec((1, w), lambda j, t: (0, j))
    acc = pl.BlockSpec((SUBLANE, w), lambda j, t: (0, j))
    out = jax.ShapeDtypeStruct((T, S5_WIDTH), F32)
    accs = jax.ShapeDtypeStruct((SUBLANE, S5_WIDTH), F32)
    return pl.pallas_call(
        body, name=name, grid=(ncol, nt),
        in_specs=[tok, tok, tok, tok, vec, vec], out_specs=[tok, tok, acc, acc],
        out_shape=[out, out, accs, accs],
        scratch_shapes=[pltpu.VMEM((2, w), F32)],
        compiler_params=_cparams(("parallel", "arbitrary")),
    )(g_re, g_im, s_re, s_im, a_re, a_im)


ATTN_BLOCK = 512
_NEG = -1e30


def _qk_cat(nope, rope):
    return jnp.concatenate([nope.astype(_MXU_DTYPE), rope.astype(_MXU_DTYPE)], axis=1)


def _attn_scores(q, k, q0, k0):
    s = _mxu(q, k, 1, 1) * (MLA_QK ** -0.5)
    qpos = q0 + lax.broadcasted_iota(jnp.int32, s.shape, 0)
    kpos = k0 + lax.broadcasted_iota(jnp.int32, s.shape, 1)
    return s, kpos <= qpos


def _attn_fwd(q_all, q_rope, kv, k_rope, *, name):
    T = q_all.shape[0]
    tq = min(ATTN_BLOCK, T)
    nq = T // tq

    def body(qn_ref, qr_ref, kn_ref, v_ref, kr_ref, o_ref, lse_ref):
        i = pl.program_id(1)
        q = _qk_cat(qn_ref[...], qr_ref[0])

        def step(j, carry):
            m, l, acc = carry
            ks = pl.ds(pl.multiple_of(j * tq, tq), tq)
            s, ok = _attn_scores(q, _qk_cat(kn_ref[ks, :], kr_ref[ks, :]), i * tq, j * tq)
            s = jnp.where(ok, s, _NEG)
            m_new = jnp.maximum(m, jnp.max(s, axis=-1, keepdims=True))
            p = jnp.exp(s - m_new)
            alpha = jnp.exp(m - m_new)
            return m_new, alpha * l + jnp.sum(p, axis=-1, keepdims=True), alpha * acc + _mxu(p, v_ref[ks, :], 1, 0)

        m, l, acc = lax.fori_loop(0, i + 1, step, (jnp.full((tq, 1), _NEG, F32), jnp.zeros((tq, 1), F32), jnp.zeros((tq, MLA_V), F32)))
        o_ref[...] = acc / l
        lse_ref[0] = m + jnp.log(l)

    return pl.pallas_call(
        body, name=name, grid=(MLA_HEADS, nq),
        in_specs=[pl.BlockSpec((tq, MLA_NOPE), lambda h, i: (i, h)), pl.BlockSpec((1, tq, MLA_ROPE), lambda h, i: (h, i, 0)),
                  pl.BlockSpec((T, MLA_NOPE), lambda h, i: (0, 2 * h)), pl.BlockSpec((T, MLA_V), lambda h, i: (0, 2 * h + 1)),
                  pl.BlockSpec((T, MLA_ROPE), lambda h, i: (0, 0))],
        out_specs=[pl.BlockSpec((tq, MLA_V), lambda h, i: (i, h)), pl.BlockSpec((1, tq, 1), lambda h, i: (h, i, 0))],
        out_shape=[jax.ShapeDtypeStruct((T, MLA_HEADS * MLA_V), F32), jax.ShapeDtypeStruct((MLA_HEADS, T, 1), F32)],
        compiler_params=_cparams(("arbitrary", "arbitrary")),
    )(q_all, q_rope, kv, kv, k_rope)


def _attn_bwd(q_all, q_rope, kv, k_rope, o, lse, do, *, name):
    T = q_all.shape[0]
    tk = min(ATTN_BLOCK, T)
    nk = T // tk
    scale = MLA_QK ** -0.5

    def body(qn_ref, qr_ref, kv_ref, kr_ref, o_ref, lse_ref, do_ref, dqn_ref, dqr_ref, dkv_ref, dkr_ref, delta_ref):
        h, j = pl.program_id(0), pl.program_id(1)

        @pl.when(j == 0)
        def _():
            dqn_ref[...] = jnp.zeros_like(dqn_ref)
            dqr_ref[...] = jnp.zeros_like(dqr_ref)
            delta_ref[...] = jnp.sum(do_ref[...] * o_ref[...], axis=-1, keepdims=True)

        @pl.when((j == 0) & (h == 0))
        def _():
            dkr_ref[...] = jnp.zeros_like(dkr_ref)

        krows = pl.ds(pl.multiple_of(j * tk, tk), tk)
        k = _qk_cat(kv_ref[:, :MLA_NOPE], kr_ref[krows, :])
        v = kv_ref[:, MLA_NOPE:].astype(_MXU_DTYPE)

        def step(i, carry):
            dk, dv = carry
            qs = pl.ds(pl.multiple_of(i * tk, tk), tk)
            q, dob = _qk_cat(qn_ref[qs, :], qr_ref[0, qs, :]), do_ref[qs, :].astype(_MXU_DTYPE)
            s, ok = _attn_scores(q, k, i * tk, j * tk)
            p = jnp.where(ok, jnp.exp(s - lse_ref[0, qs, :]), 0.0)
            ds = p * (_mxu(dob, v, 1, 1) - delta_ref[qs, :]) * scale
            dq = _mxu(ds, k, 1, 0)
            dqn_ref[qs, :] += dq[:, :MLA_NOPE]
            dqr_ref[0, qs, :] += dq[:, MLA_NOPE:]
            return dk + _mxu(ds, q, 0, 0), dv + _mxu(p, dob, 0, 0)

        dk, dv = lax.fori_loop(j, nk, step, (jnp.zeros((tk, MLA_QK), F32), jnp.zeros((tk, MLA_V), F32)))
        dkv_ref[:, :MLA_NOPE] = dk[:, :MLA_NOPE]
        dkv_ref[:, MLA_NOPE:] = dv
        dkr_ref[krows, :] += dk[:, MLA_NOPE:]

    head_cols = pl.BlockSpec((T, MLA_NOPE), lambda h, j: (0, h))
    head_rope = pl.BlockSpec((1, T, MLA_ROPE), lambda h, j: (h, 0, 0))
    kv_spec = pl.BlockSpec((tk, MLA_NOPE + MLA_V), lambda h, j: (j, h))
    kr_spec = pl.BlockSpec((T, MLA_ROPE), lambda h, j: (0, 0))
    return pl.pallas_call(
        body, name=name, grid=(MLA_HEADS, nk),
        in_specs=[head_cols, head_rope, kv_spec, kr_spec, head_cols, pl.BlockSpec((1, T, 1), lambda h, j: (h, 0, 0)), head_cols],
        out_specs=[head_cols, head_rope, kv_spec, kr_spec],
        out_shape=[jax.ShapeDtypeStruct((T, MLA_HEADS * MLA_NOPE), F32), jax.ShapeDtypeStruct((MLA_HEADS, T, MLA_ROPE), F32),
                   jax.ShapeDtypeStruct((T, MLA_HEADS * (MLA_NOPE + MLA_V)), F32), jax.ShapeDtypeStruct((T, MLA_ROPE), F32)],
        scratch_shapes=[pltpu.VMEM((T, 1), F32)],
        compiler_params=_cparams(("arbitrary", "arbitrary")),
    )(q_all, q_rope, kv, k_rope, o, lse, do)


def _s5_discretize(a_re, a_im, log_dt, bt_re, bt_im, lb_logits):
    dt = jnp.exp(log_dt)
    mag = jnp.exp(a_re * dt)
    abr, abi = mag * jnp.cos(a_im * dt), mag * jnp.sin(a_im * dt)
    den = a_re * a_re + a_im * a_im
    xr, xi = abr - 1.0, abi
    cr = ((xr * a_re + xi * a_im) / den)[:, None, :]
    ci = ((xi * a_re - xr * a_im) / den)[:, None, :]
    e = jnp.exp(lb_logits - jnp.max(lb_logits, axis=0, keepdims=True))
    lb = e[0:1, :] / jnp.sum(e, axis=0, keepdims=True)
    return abr, abi, cr * bt_re - ci * bt_im, cr * bt_im + ci * bt_re, lb


def _whole(shape):
    return pl.BlockSpec(shape, lambda: (0,) * len(shape))


def _s5_params_fwd(a_re, a_im, log_dt, bt_re, bt_im, lb_logits):
    ins = (a_re, a_im, log_dt, bt_re, bt_im, lb_logits)
    outs = [jax.ShapeDtypeStruct(s, F32) for s in (a_re.shape, a_re.shape, bt_re.shape, bt_re.shape, (1, lb_logits.shape[1]))]

    def body(*refs):
        res = _s5_discretize(*[r[...] for r in refs[:6]])
        for o_ref, o in zip(refs[6:], res):
            o_ref[...] = o

    return pl.pallas_call(body, name="s5_params_fwd", in_specs=[_whole(a.shape) for a in ins],
                          out_specs=[_whole(o.shape) for o in outs], out_shape=outs, compiler_params=_cparams())(*ins)


def _s5_params_bwd(a_re, a_im, log_dt, bt_re, bt_im, lb_logits, d_abr, d_abi, d_bbr, d_bbi, d_lb):
    ins = (a_re, a_im, log_dt, bt_re, bt_im, lb_logits, d_abr, d_abi, d_bbr, d_bbi, d_lb)
    outs = [jax.ShapeDtypeStruct(a.shape, F32) for a in ins[:6]]

    def body(*refs):
        _, vjp = jax.vjp(_s5_discretize, *[r[...] for r in refs[:6]])
        for o_ref, o in zip(refs[11:], vjp(tuple(r[...] for r in refs[6:11]))):
            o_ref[...] = o

    return pl.pallas_call(body, name="s5_params_bwd", in_specs=[_whole(a.shape) for a in ins],
                          out_specs=[_whole(o.shape) for o in outs], out_shape=outs, compiler_params=_cparams())(*ins)


def _adamw(w, g, m, v, *, name):
    R, C = w.shape
    tr = _pick(R, (256, 128, 64, 32, 16, 8)) if R % SUBLANE == 0 else R
    slabs = g.shape[0] if g.ndim == 3 else 0

    def body(w_ref, g_ref, m_ref, v_ref, *outs):
        if slabs:
            gv = g_ref[0].astype(F32)
            for s in range(1, slabs):
                gv = gv + g_ref[s].astype(F32)
            outs[0][...] = gv
            outs = outs[1:]
        else:
            gv = g_ref[...]
        d_ref, mo_ref, vo_ref = outs
        m2 = ADAM_B1 * m_ref[...] + (1.0 - ADAM_B1) * gv
        v2 = ADAM_B2 * v_ref[...] + (1.0 - ADAM_B2) * (gv * gv)
        m_hat = m2 / (1.0 - ADAM_B1 ** ADAM_STEP)
        v_hat = v2 / (1.0 - ADAM_B2 ** ADAM_STEP)
        d_ref[...] = -ADAM_LR * (m_hat / (jnp.sqrt(v_hat) + ADAM_EPS) + ADAM_WD * w_ref[...])
        mo_ref[...] = m2
        vo_ref[...] = v2

    spec = pl.BlockSpec((tr, C), lambda i: (i, 0))
    g_spec = pl.BlockSpec((slabs, tr, C), lambda i: (0, i, 0)) if slabs else spec
    out = jax.ShapeDtypeStruct((R, C), F32)
    n_out = 4 if slabs else 3
    return pl.pallas_call(body, name=name, grid=(R // tr,), in_specs=[spec, g_spec, spec, spec], out_specs=[spec] * n_out,
                          out_shape=[out] * n_out, compiler_params=_cparams(("parallel",)))(w, g, m, v)


N_CHIPS = 4
N_CORES = 2


_FLIPS = tuple((dx, dy, dc) for dx in (0, 1) for dy in (0, 1) for dc in (0, 1) if (dx, dy, dc) != (0, 0, 0))


_HBM = pl.BlockSpec(memory_space=pltpu.HBM)
_SEM = pl.BlockSpec(memory_space=pltpu.SEMAPHORE)
_SPLIT_COPY = pltpu.CompilerParams(has_side_effects=pltpu.SideEffectType.DATAFLOW_SIDE_EFFECTING)


def _exchange_copies(src_refs, land_refs, send_sems, recv_sems, scatter, arriving):
    x, y, c = lax.axis_index("x"), lax.axis_index("y"), lax.axis_index("c")
    me_chip = 2 * x + y
    copies = []
    for a, (s_ref, l_ref) in enumerate(zip(src_refs, land_refs)):
        for j, (dx, dy, dc) in enumerate(_FLIPS):
            px, py, pc = (1 - x if dx else x), (1 - y if dy else y), (1 - c if dc else c)
            k = a * len(_FLIPS) + j
            p_chip = 2 * px + py
            copies.append(pltpu.make_async_remote_copy(
                src_ref=s_ref.at[p_chip, pc] if scatter else s_ref, dst_ref=l_ref.at[p_chip, pc] if arriving else l_ref.at[me_chip, c],
                send_sem=send_sems.at[k], recv_sem=recv_sems.at[k], device_id=(px, py, pc), device_id_type=MESH))
    return copies


def _exchange_start(srcs, *, scatter, name, after=()):
    n_arr = len(srcs)
    n_sem = n_arr * len(_FLIPS)
    n_in = 2 * n_arr + len(after)
    lands = [lax.empty(s.shape if scatter else (N_CHIPS, N_CORES) + s.shape, s.dtype) for s in srcs]

    def body(*refs):
        src_refs, land_refs = refs[:n_arr], refs[n_arr:2 * n_arr]
        for cp in _exchange_copies(src_refs, land_refs, refs[n_in], refs[n_in + 1], scatter, arriving=False):
            cp.start()
        refs[-1][...] = jnp.zeros_like(refs[-1])

    thru = [pltpu.HBM(a.shape, a.dtype) for a in srcs + lands]
    outs = pl.pallas_call(
        body, name=name,
        out_shape=(pltpu.SemaphoreType.DMA((n_sem,)), pltpu.SemaphoreType.DMA((n_sem,)), *thru,
                   jax.ShapeDtypeStruct((SUBLANE, LANE), F32)),
        in_specs=[_HBM] * (2 * n_arr) + [pl.BlockSpec(memory_space=pl.ANY)] * len(after),
        out_specs=(_SEM, _SEM, *[_HBM] * (2 * n_arr), pl.BlockSpec(memory_space=pltpu.VMEM)),
        input_output_aliases={i: 2 + i for i in range(2 * n_arr)}, compiler_params=_SPLIT_COPY,
    )(*[pltpu.with_memory_space_constraint(a, pltpu.HBM) for a in srcs + lands], *after)
    return outs[0], outs[1], list(outs[2:2 + n_arr]), list(outs[2 + n_arr:2 + 2 * n_arr]), outs[-1]


def _exchange_wait(started, after, *, scatter, name):
    send_sems, recv_sems, srcs, lands, _ = started
    n_arr = len(srcs)

    def body(*refs):
        src_refs, land_refs = refs[:n_arr], refs[n_arr:2 * n_arr]
        for cp in _exchange_copies(src_refs, land_refs, refs[2 * n_arr], refs[2 * n_arr + 1], scatter, arriving=True):
            cp.wait_send()
            cp.wait_recv()

    outs = pl.pallas_call(
        body, name=name, out_shape=[pltpu.HBM(a.shape, a.dtype) for a in srcs + lands],
        in_specs=[_HBM] * (2 * n_arr) + [_SEM, _SEM, pl.BlockSpec(memory_space=pl.ANY)], out_specs=[_HBM] * (2 * n_arr),
        input_output_aliases={i: i for i in range(2 * n_arr)}, compiler_params=_SPLIT_COPY,
    )(*srcs, *lands, send_sems, recv_sems, after)
    return list(outs[:n_arr]), list(outs[n_arr:])


def _with_own(land, own):
    me_chip = 2 * lax.axis_index("x") + lax.axis_index("y")
    return lax.dynamic_update_slice(land, own[None, None], (me_chip, lax.axis_index("c")) + (0,) * own.ndim)


def _sum_slabs(buf, out_dtype, *, name):
    G, R, C = buf.shape
    tr = _pick(R, (512, 256, 128, 64, 32, 16, 8))

    def body(b_ref, o_ref):
        acc = b_ref[0].astype(F32)
        for s in range(1, G):
            acc = acc + b_ref[s].astype(F32)
        o_ref[...] = acc.astype(out_dtype)

    return pl.pallas_call(body, name=name, grid=(R // tr,), in_specs=[pl.BlockSpec((G, tr, C), lambda i: (0, i, 0))],
                          out_specs=pl.BlockSpec((tr, C), lambda i: (i, 0)), out_shape=jax.ShapeDtypeStruct((R, C), out_dtype),
                          compiler_params=_cparams(("parallel",)))(buf)


def _rms_fwd_fn(h, g):
    return (_rms(h, g),)


def _rms_bwd_fn(h, dhn, dres, g):
    _, vjp = jax.vjp(_rms, h, g)
    dh, dg = vjp(dhn)
    return dh + dres, dg


def _loss_fn(h, tgt, g):
    y, vjp = jax.vjp(_rms, h, g)
    diff = y - tgt
    dh, dg = vjp(diff * (1.0 / D_MODEL))
    return dh, dg, (0.5 / D_MODEL) * jnp.sum(diff * diff, axis=0, keepdims=True)


def _s5_act(ys, u, d):
    return _gelu(ys + d * u)


def _s5_gate(z, gl, b):
    return z * _sigmoid(gl + b)


def _s5_act_fn(ys, u, d):
    return (_s5_act(ys, u, d),)


def _s5_mix_fn(ya, z, gl, b):
    return (jnp.concatenate([ya, _s5_gate(z, gl, b)], axis=1),)


def _s5_gate_bwd_fn(z, gl, dyb, b):
    _, vjp = jax.vjp(_s5_gate, z, gl, b)
    return vjp(dyb)


def _s5_act_bwd_fn(ys, u, dz1, dz2, d):
    _, vjp = jax.vjp(_s5_act, ys, u, d)
    return vjp(dz1 + dz2)


def _dproj_fn(dq, df, di, dg, du1, du2):
    return (jnp.concatenate([dq, df, di, dg, du1 + du2], axis=1),)


def _rope_pair(r1, r2, pos, freqs):
    ang = pos.astype(F32) * freqs
    c, s = jnp.cos(ang), jnp.sin(ang)
    return r1 * c - r2 * s, r1 * s + r2 * c


_ODD_SPLITS = (0, MLA_Q_RANK, MLA_Q_RANK + MLA_KV_RANK, MLA_Q_RANK + MLA_KV_RANK + LANE, ODD_IN_PAD)


def _mla_prep(cq, ckv, k1, k2, qg, kvg, pos, freqs):
    ko1, ko2 = _rope_pair(k1, k2, pos, freqs)
    return _rms(cq, qg), _rms(ckv, kvg), ko1, ko2


def _mla_prep_fn(proj, pos, qg, kvg, freqs):
    parts = [proj[:, a:b] for a, b in zip(_ODD_SPLITS[:-1], _ODD_SPLITS[1:])]
    return _mla_prep(*parts, qg, kvg, pos, freqs)


def _mla_prep_bwd_fn(proj, pos, dqn, dkvn, dko1, dko2, qg, kvg, freqs):
    parts = [proj[:, a:b] for a, b in zip(_ODD_SPLITS[:-1], _ODD_SPLITS[1:])]
    _, vjp = jax.vjp(lambda *a: _mla_prep(*a, pos, freqs), *parts, qg, kvg)
    dcq, dckv, dk1, dk2, dqg, dkvg = vjp((dqn, dkvn, dko1, dko2))
    return jnp.concatenate([dcq, dckv, dk1, dk2], axis=1), dqg, dkvg


def _rope_q_fn(r1, r2, pos, freqs):
    return _rope_pair(r1, r2, pos, freqs)


def _rope_q_bwd_fn(dqn, do1, do2, pos, freqs):
    dr1, dr2 = _rope_pair(do1, do2, pos, -freqs)
    return (jnp.concatenate([dqn, dr1, dr2], axis=1),)


W_NAMES = ("norm_mix_g", "norm_ffn_g", "final_norm_g", "even_w_in", "hgrn_lb_logits", "hgrn_norm_g", "s5_a_re", "s5_a_im",
           "s5_log_dt", "s5_b_re", "s5_b_im", "s5_c_re", "s5_c_im", "s5_d", "s5_w_glu", "s5_b_glu", "even_w_out", "odd_w_in",
           "mla_q_norm_g", "mla_w_uq", "mla_kv_norm_g", "mla_w_ukv", "odd_w_out", "ffn_w_in", "ffn_conv_w", "ffn_conv_b",
           "ffn_w_out")
BIG_UNITS = (("even_w_in", 0, "col"), ("s5_w_glu", 0, "row"), ("even_w_out", 0, "row"), ("odd_w_in", 0, "row"),
             ("mla_w_uq", 0, "col"), ("mla_w_ukv", 0, "col"), ("odd_w_out", 0, "row"),
             ("ffn_w_in", 0, "col"), ("ffn_w_in", 1, "col"), ("ffn_w_out", 0, "row"), ("ffn_w_out", 1, "row"))
BIG_NAMES = tuple(dict.fromkeys(u[0] for u in BIG_UNITS))
SMALL_SHARDED = (("mla_q_norm_g", 1), ("mla_kv_norm_g", 1), ("ffn_conv_w", 2))
SMALL_SHARDED_NAMES = tuple(s[0] for s in SMALL_SHARDED)
REPLICATED = tuple(n for n in W_NAMES if n not in BIG_NAMES + SMALL_SHARDED_NAMES)
REPLICATED_LATE = ("norm_mix_g",)
REPLICATED_EARLY = tuple(n for n in REPLICATED if n not in REPLICATED_LATE)


def _pack(flats, cols, row_mult):
    flat = jnp.concatenate(flats, axis=-1)
    pad = (-flat.shape[-1]) % (cols * row_mult)
    flat = jnp.pad(flat, [(0, 0)] * (flat.ndim - 1) + [(0, pad)])
    return flat.reshape(flat.shape[:-1] + (-1, cols))


def _unpack(flat, shapes):
    out, off = [], 0
    for shp in shapes:
        n = int(np.prod(shp))
        out.append(flat[..., off:off + n].reshape(flat.shape[:-1] + tuple(shp)))
        off += n
    return out


UNIT_KIND = {(n, l): kind for n, l, kind in BIG_UNITS}
STAGES = ((("even_w_in", 0),),
          (("s5_w_glu", 0), ("even_w_out", 0)),
          (("ffn_w_in", 0), ("ffn_w_out", 0)),
          (("odd_w_in", 0), ("mla_w_uq", 0), ("mla_w_ukv", 0), ("odd_w_out", 0)),
          (("ffn_w_in", 1), ("ffn_w_out", 1)))


def _gather_start(w, stage, with_small, after):
    srcs = [w[n][l].astype(BF16) for n, l in STAGES[stage]]
    if with_small:
        srcs.append(_pack([w[n].reshape(-1) for n in SMALL_SHARDED_NAMES], LANE, SUBLANE))
    return _exchange_start(srcs, scatter=False, name=f"gather_start_{stage}", after=after)


def _gather_finish(started, after, w, stage, with_small):
    srcs, lands = _exchange_wait(started, after, scatter=False, name=f"gather_wait_{stage}")
    lands = [_with_own(land, src) for land, src in zip(lands, srcs)]
    big = {}
    for unit, g in zip(STAGES[stage], lands):
        r, c = g.shape[2:]
        big[unit] = g.reshape(N_DEV * r, c) if UNIT_KIND[unit] == "row" else g.transpose(2, 0, 1, 3).reshape(r, N_DEV * c)
    if not with_small:
        return big
    parts = _unpack(lands[-1].reshape(N_DEV, -1), [w[n].shape for n in SMALL_SHARDED_NAMES])
    small = {}
    for (n, ax), p in zip(SMALL_SHARDED, parts):
        shp = list(w[n].shape)
        shp[ax] *= N_DEV
        small[n] = jnp.moveaxis(p, 0, ax).reshape(shp)
    return big, small


def _scatter_start(g_big, stage, extra=()):
    srcs = []
    for unit in STAGES[stage]:
        g = g_big[unit].astype(BF16)
        if UNIT_KIND[unit] == "row":
            srcs.append(g.reshape(N_CHIPS, N_CORES, g.shape[0] // N_DEV, g.shape[1]))
        else:
            srcs.append(g.reshape(g.shape[0], N_CHIPS, N_CORES, g.shape[1] // N_DEV).transpose(1, 2, 0, 3))
    return _exchange_start(srcs + list(extra), scatter=True, name=f"scatter_start_{stage}")


def _scatter_finish(started, after, stage):
    srcs, lands = _exchange_wait(started, after, scatter=True, name=f"scatter_wait_{stage}")
    me_chip, c = 2 * lax.axis_index("x") + lax.axis_index("y"), lax.axis_index("c")
    outs = []
    for land, src in zip(lands, srcs):
        own = lax.dynamic_slice(src, (me_chip, c) + (0,) * (src.ndim - 2), (1, 1) + src.shape[2:])[0, 0]
        outs.append(_with_own(land, own).reshape((N_DEV,) + land.shape[2:]))
    return outs


def _small_sharded_pack(g_small, w):
    flats = []
    for n, ax in SMALL_SHARDED:
        shp = list(w[n].shape)
        g = g_small[n].astype(F32).reshape(shp[:ax] + [N_DEV] + shp[ax:])
        flats.append(jnp.moveaxis(g, ax, 0).reshape(N_DEV, -1))
    small = _pack(flats, LANE, SUBLANE)
    return small.reshape((N_CHIPS, N_CORES) + small.shape[1:])


def _replicated_pack(g_repl, names):
    vec = _pack([g_repl[n].reshape(-1).astype(F32) for n in names], LANE, SUBLANE)
    return jnp.broadcast_to(vec, (N_CHIPS, N_CORES) + vec.shape)


def _block_diag(blocks):
    G, a, b = blocks.shape
    return jnp.einsum('gab,gk->gakb', blocks, jnp.eye(G, dtype=blocks.dtype)).reshape(G * a, G * b)


def _diag_blocks(mat, a, b):
    G = mat.shape[0] // a
    return jnp.einsum('gagb->gab', mat.reshape(G, a, G, b))


def _ffn_fwd(h, g, w_in, conv_w, conv_b, w_out, tag):
    hn, = _rows(_rms_fwd_fn, [h], [g], [(D_MODEL, BF16)], [], name=f"ffn{tag}_norm")
    au = _mm(hn, w_in, name=f"ffn{tag}_in")
    z = _ffn_mid_fwd(au, conv_w, conv_b, name=f"ffn{tag}_mid")
    return _mm(z, w_out, res=h, name=f"ffn{tag}_out"), (hn, au, z)


def _ffn_bwd(h, dh, saved, g, w_in, conv_w, conv_b, w_out, tag, deps=()):
    hn, au, z = saved
    dz = _mm(dh, w_out, tb=True, deps=deps, name=f"ffn{tag}_dz")
    dw_out = _mm(z, dh, ta=True, out_dtype=BF16, name=f"ffn{tag}_dwout")
    da, du, dcw, dcb = _ffn_mid_bwd(au, dz, conv_w, conv_b, name=f"ffn{tag}_dmid")
    dhn = _mm(da, w_in, tb=True, b_cols=(0, D_FF), name=f"ffn{tag}_dhn_a")
    dhn = _mm(du, w_in, tb=True, b_cols=(D_FF, D_FF), res=dhn, name=f"ffn{tag}_dhn_u")
    dw_in = jnp.concatenate([_mm(hn, da, ta=True, out_dtype=BF16, name=f"ffn{tag}_dwin_a"),
                             _mm(hn, du, ta=True, out_dtype=BF16, name=f"ffn{tag}_dwin_u")], axis=1)
    dh_in, dg = _rows(_rms_bwd_fn, [h, dhn, dh], [g], [(D_MODEL, F32)], [(1, D_MODEL)], name=f"ffn{tag}_dnorm")
    return dh_in, dict(g=dg, w_in=dw_in, conv_w=dcw, conv_b=dcb, w_out=dw_out)


def kernel(x, positions, norm_mix_g, norm_ffn_g, final_norm_g, even_w_in, hgrn_lb_logits, hgrn_norm_g, s5_a_re, s5_a_im, s5_log_dt, s5_b_re, s5_b_im, s5_c_re, s5_c_im, s5_d, s5_w_glu, s5_b_glu, even_w_out, odd_w_in, mla_q_norm_g, mla_w_uq, mla_kv_norm_g, mla_w_ukv, odd_w_out, ffn_w_in, ffn_conv_w, ffn_conv_b, ffn_w_out, loss_target, m_norm_mix_g, m_norm_ffn_g, m_final_norm_g, m_even_w_in, m_hgrn_lb_logits, m_hgrn_norm_g, m_s5_a_re, m_s5_a_im, m_s5_log_dt, m_s5_b_re, m_s5_b_im, m_s5_c_re, m_s5_c_im, m_s5_d, m_s5_w_glu, m_s5_b_glu, m_even_w_out, m_odd_w_in, m_mla_q_norm_g, m_mla_w_uq, m_mla_kv_norm_g, m_mla_w_ukv, m_odd_w_out, m_ffn_w_in, m_ffn_conv_w, m_ffn_conv_b, m_ffn_w_out, v_norm_mix_g, v_norm_ffn_g, v_final_norm_g, v_even_w_in, v_hgrn_lb_logits, v_hgrn_norm_g, v_s5_a_re, v_s5_a_im, v_s5_log_dt, v_s5_b_re, v_s5_b_im, v_s5_c_re, v_s5_c_im, v_s5_d, v_s5_w_glu, v_s5_b_glu, v_even_w_out, v_odd_w_in, v_mla_q_norm_g, v_mla_w_uq, v_mla_kv_norm_g, v_mla_w_ukv, v_odd_w_out, v_ffn_w_in, v_ffn_conv_w, v_ffn_conv_b, v_ffn_w_out):
    given = dict(locals())
    w = {n: given[n] for n in W_NAMES}
    mom = {n: given["m_" + n] for n in W_NAMES}
    var = {n: given["v_" + n] for n in W_NAMES}
    T = x.shape[1]
    h0 = x[0]
    tgt = loss_target[0]
    pos = positions.reshape(T, 1)

    gathers = []
    for s in range(len(STAGES)):
        gathers.append(_gather_start(w, s, with_small=(s == 1), after=[g[4] for g in gathers[-1:]]))
    half = MLA_ROPE // 2
    kr0 = MLA_Q_RANK + MLA_KV_RANK
    freqs = ROPE_THETA ** (-jnp.arange(0, MLA_ROPE, 2, dtype=F32) / MLA_ROPE)
    freqs_q = jnp.tile(freqs, MLA_HEADS)[None, :]
    freqs_k = jnp.concatenate([freqs, jnp.zeros((LANE - half,), F32)])[None, :]

    sp_in = (s5_a_re[0], s5_a_im[0], s5_log_dt[0][:, None], s5_b_re[0].transpose(0, 2, 1), s5_b_im[0].transpose(0, 2, 1),
             hgrn_lb_logits)
    abr, abi, bbt_re, bbt_im, lb0 = _s5_params_fwd(*sp_in)
    a_re, a_im = abr.reshape(1, S5_WIDTH), abi.reshape(1, S5_WIDTH)
    bb_re, bb_im = _block_diag(bbt_re).astype(BF16), _block_diag(bbt_im).astype(BF16)
    bb_cat = jnp.concatenate([bb_re, bb_im], axis=1)
    c_re = _block_diag(s5_c_re[0].transpose(0, 2, 1)).astype(BF16)
    c_im_neg = _block_diag(-s5_c_im[0].transpose(0, 2, 1)).astype(BF16)
    u_cols = (4 * HGRN_DIM, S5_DIM)

    hn0, = _rows(_rms_fwd_fn, [h0], [norm_mix_g[0:1]], [(D_MODEL, BF16)], [], name="mix0_norm", deps=[gathers[-1][4]])
    full = _gather_finish(gathers[0], hn0, w, 0, False)
    w_ein = full["even_w_in", 0]
    proj = _mm(hn0, w_ein, name="even_in")
    y_a, states = _hgrn_fwd(proj, lb0, hgrn_norm_g, name="hgrn_fwd")
    bu = _mm(proj, bb_cat, a_cols=u_cols, name="s5_bu")
    more, full_small = _gather_finish(gathers[1], bu, w, 1, True)
    w_glu, w_eout = more["s5_w_glu", 0], more["even_w_out", 0]
    qg, kvg, conv_w = full_small["mla_q_norm_g"], full_small["mla_kv_norm_g"], full_small["ffn_conv_w"]
    s_re, s_im = _s5_scan_fwd(bu, a_re, a_im, name="s5_scan_fwd")
    ys = _mm(s_im, c_im_neg, res=_mm(s_re, c_re, name="s5_y_re"), name="s5_y_im")
    z5, = _rows(_s5_act_fn, [ys, (proj,) + u_cols], [s5_d], [(S5_DIM, F32)], [], name="s5_act")
    gl = _mm(z5, w_glu, name="s5_glu")
    mixin, = _rows(_s5_mix_fn, [y_a, z5, gl], [s5_b_glu], [(D_MODEL, BF16)], [], name="s5_mix")
    h1 = _mm(mixin, w_eout, res=h0, name="even_out")
    full.update(_gather_finish(gathers[2], h1, w, 2, False))
    w_fin, w_fout = [full["ffn_w_in", 0]], [full["ffn_w_out", 0]]
    h2, ffn0_saved = _ffn_fwd(h1, norm_ffn_g[0:1], w_fin[0], conv_w[0], ffn_conv_b[0:1], w_fout[0], 0)

    full.update(_gather_finish(gathers[3], h2, w, 3, False))
    w_oin, w_ukv, w_oout = full["odd_w_in", 0], full["mla_w_ukv", 0], full["odd_w_out", 0]
    zpad = jnp.zeros((D_MODEL, LANE - half), BF16)
    w_oin_pad = jnp.concatenate([w_oin[:, :kr0], w_oin[:, kr0:kr0 + half], zpad, w_oin[:, kr0 + half:], zpad], axis=1)
    w_uq3 = full["mla_w_uq", 0].reshape(MLA_Q_RANK, MLA_HEADS, MLA_QK)
    w_uq_perm = jnp.concatenate([w_uq3[:, :, :MLA_NOPE].reshape(MLA_Q_RANK, -1),
                                 w_uq3[:, :, MLA_NOPE:MLA_NOPE + half].reshape(MLA_Q_RANK, -1),
                                 w_uq3[:, :, MLA_NOPE + half:].reshape(MLA_Q_RANK, -1)], axis=1)
    hn1, = _rows(_rms_fwd_fn, [h2], [norm_mix_g[1:2]], [(D_MODEL, BF16)], [], name="mix1_norm")
    proj_o = _mm(hn1, w_oin_pad, name="odd_in")
    qn, kvn, ko1, ko2 = _rows(_mla_prep_fn, [proj_o, pos], [qg, kvg, freqs_k],
                              [(MLA_Q_RANK, BF16), (MLA_KV_RANK, BF16), (LANE, F32), (LANE, F32)], [], name="mla_prep")
    q_all = _mm(qn, w_uq_perm, name="mla_uq")
    kv = _mm(kvn, w_ukv, name="mla_ukv")
    nope_w = MLA_HEADS * MLA_NOPE
    rope_w = MLA_HEADS * half
    o1, o2 = _rows(_rope_q_fn, [(q_all, nope_w, rope_w), (q_all, nope_w + rope_w, rope_w), pos], [freqs_q],
                   [(rope_w, F32), (rope_w, F32)], [], name="mla_rope_q")
    q_rope = jnp.concatenate([o1.reshape(T, MLA_HEADS, half), o2.reshape(T, MLA_HEADS, half)], axis=2).transpose(1, 0, 2)
    k_rope = jnp.concatenate([ko1[:, :half], ko2[:, :half]], axis=1)
    o, lse = _attn_fwd(q_all, q_rope, kv, k_rope, name="attn_fwd")
    h3 = _mm(o, w_oout, res=h2, name="odd_out")
    full.update(_gather_finish(gathers[4], h3, w, 4, False))
    w_fin.append(full["ffn_w_in", 1])
    w_fout.append(full["ffn_w_out", 1])
    h4, ffn1_saved = _ffn_fwd(h3, norm_ffn_g[1:2], w_fin[1], conv_w[1], ffn_conv_b[1:2], w_fout[1], 1)

    dh4, d_final_g, loss_cols = _rows(_loss_fn, [h4, tgt], [final_norm_g[None, :]], [(D_MODEL, F32)],
                                      [(1, D_MODEL), (1, D_MODEL)], name="loss_head")
    loss = lax.psum(jnp.sum(loss_cols), ("x", "y", "c"))

    dh3, gf1 = _ffn_bwd(h3, dh4, ffn1_saved, norm_ffn_g[1:2], w_fin[1], conv_w[1], ffn_conv_b[1:2], w_fout[1], 1)
    scatters = {4: _scatter_start({("ffn_w_in", 1): gf1["w_in"], ("ffn_w_out", 1): gf1["w_out"]}, 4)}
    do = _mm(dh3, w_oout, tb=True, deps=[scatters[4][4]], name="odd_out_dx")
    d_w_oout = _mm(o, dh3, ta=True, out_dtype=BF16, name="odd_out_dw")
    dq_nope, dq_rope, dkv, dk_rope = _attn_bwd(q_all, q_rope, kv, k_rope, o, lse, do, name="attn_bwd")
    dq_rope_t = dq_rope.transpose(1, 0, 2)
    do1, do2 = dq_rope_t[:, :, :half].reshape(T, rope_w), dq_rope_t[:, :, half:].reshape(T, rope_w)
    lane_pad = ((0, 0), (0, LANE - half))
    dko1, dko2 = jnp.pad(dk_rope[:, :half], lane_pad), jnp.pad(dk_rope[:, half:], lane_pad)
    dq_all, = _rows(_rope_q_bwd_fn, [dq_nope, do1, do2, pos], [freqs_q], [(MLA_HEADS * MLA_QK, BF16)], [], name="mla_rope_q_bwd")
    d_w_uq_perm = _mm(qn, dq_all, ta=True, out_dtype=BF16, name="mla_uq_dw")
    dqn = _mm(dq_all, w_uq_perm, tb=True, name="mla_uq_dx")
    d_w_ukv = _mm(kvn, dkv, ta=True, out_dtype=BF16, name="mla_ukv_dw")
    dkvn = _mm(dkv, w_ukv, tb=True, name="mla_ukv_dx")
    dproj_o, d_qg, d_kvg = _rows(_mla_prep_bwd_fn, [proj_o, pos, dqn, dkvn, dko1, dko2], [qg, kvg, freqs_k],
                                 [(ODD_IN_PAD, BF16)], [(1, MLA_Q_RANK), (1, MLA_KV_RANK)], name="mla_prep_bwd")
    d_w_oin_pad = _mm(hn1, dproj_o, ta=True, out_dtype=BF16, name="odd_in_dw")
    dhn1 = _mm(dproj_o, w_oin_pad, tb=True, name="odd_in_dx")
    dh2, d_mix_g1 = _rows(_rms_bwd_fn, [h2, dhn1, dh3], [norm_mix_g[1:2]], [(D_MODEL, F32)], [(1, D_MODEL)], name="mix1_dnorm")
    d_w_oin = jnp.concatenate([d_w_oin_pad[:, :kr0 + half], d_w_oin_pad[:, kr0 + LANE:kr0 + LANE + half]], axis=1)
    d3 = d_w_uq_perm
    d_w_uq = jnp.concatenate([d3[:, :nope_w].reshape(MLA_Q_RANK, MLA_HEADS, MLA_NOPE),
                              d3[:, nope_w:nope_w + rope_w].reshape(MLA_Q_RANK, MLA_HEADS, half),
                              d3[:, nope_w + rope_w:].reshape(MLA_Q_RANK, MLA_HEADS, half)], axis=2).reshape(MLA_Q_RANK, -1)
    scatters[3] = _scatter_start({("odd_w_in", 0): d_w_oin, ("mla_w_uq", 0): d_w_uq, ("mla_w_ukv", 0): d_w_ukv,
                                  ("odd_w_out", 0): d_w_oout}, 3)

    dh1, gf0 = _ffn_bwd(h1, dh2, ffn0_saved, norm_ffn_g[0:1], w_fin[0], conv_w[0], ffn_conv_b[0:1], w_fout[0], 0,
                        deps=[scatters[3][4]])
    scatters[2] = _scatter_start({("ffn_w_in", 0): gf0["w_in"], ("ffn_w_out", 0): gf0["w_out"]}, 2)
    dmix = _mm(dh1, w_eout, tb=True, deps=[scatters[2][4]], name="even_out_dx")
    d_w_eout = _mm(mixin, dh1, ta=True, out_dtype=BF16, name="even_out_dw")
    dq, df, di, dg, d_lb0, d_hgrn_g = _hgrn_bwd(proj, lb0, hgrn_norm_g, states, dmix, name="hgrn_bwd")
    dz1, dgl, d_b_glu = _rows(_s5_gate_bwd_fn, [z5, gl, (dmix, HGRN_DIM, S5_DIM)], [s5_b_glu],
                              [(S5_DIM, F32), (S5_DIM, BF16)], [(1, S5_DIM)], name="s5_gate_bwd")
    dz2 = _mm(dgl, w_glu, tb=True, name="s5_glu_dx")
    d_w_glu = _mm(z5, dgl, ta=True, out_dtype=BF16, name="s5_glu_dw")
    dys, du1, d_s5_d = _rows(_s5_act_bwd_fn, [ys, (proj,) + u_cols, dz1, dz2], [s5_d],
                             [(S5_DIM, BF16), (S5_DIM, F32)], [(1, S5_DIM)], name="s5_act_bwd")
    ds_re = _mm(dys, c_re, tb=True, name="s5_ds_re")
    ds_im = _mm(dys, c_im_neg, tb=True, name="s5_ds_im")
    d_c_re = _mm(s_re, dys, ta=True, name="s5_dc_re")
    d_c_im_neg = _mm(s_im, dys, ta=True, name="s5_dc_im")
    lam_re, lam_im, d_ar, d_ai = _s5_scan_bwd(ds_re, ds_im, s_re, s_im, a_re, a_im, name="s5_scan_bwd")
    du2 = _mm(lam_im, bb_im, tb=True, res=_mm(lam_re, bb_re, tb=True, name="s5_du_re"), name="s5_du_im")
    d_bb_re = _mm(proj, lam_re, ta=True, a_cols=u_cols, name="s5_dbb_re")
    d_bb_im = _mm(proj, lam_im, ta=True, a_cols=u_cols, name="s5_dbb_im")
    sp_g = _s5_params_bwd(*sp_in, d_ar.sum(0).reshape(S5_GROUPS, S5_STATE), d_ai.sum(0).reshape(S5_GROUPS, S5_STATE),
                          _diag_blocks(d_bb_re, S5_GROUP, S5_STATE), _diag_blocks(d_bb_im, S5_GROUP, S5_STATE), d_lb0)
    d_a_re, d_a_im, d_log_dt, d_bt_re, d_bt_im, d_lb_logits = sp_g
    g_small = dict(mla_q_norm_g=d_qg, mla_kv_norm_g=d_kvg, ffn_conv_w=jnp.stack([gf0["conv_w"], gf1["conv_w"]]))
    g_repl = dict(
        norm_ffn_g=jnp.concatenate([gf0["g"], gf1["g"]]),
        final_norm_g=d_final_g[0], hgrn_lb_logits=d_lb_logits, hgrn_norm_g=d_hgrn_g,
        s5_a_re=d_a_re[None], s5_a_im=d_a_im[None], s5_log_dt=d_log_dt[:, 0][None],
        s5_b_re=d_bt_re.transpose(0, 2, 1)[None], s5_b_im=d_bt_im.transpose(0, 2, 1)[None],
        s5_c_re=_diag_blocks(d_c_re, S5_STATE, S5_GROUP).transpose(0, 2, 1)[None],
        s5_c_im=-_diag_blocks(d_c_im_neg, S5_STATE, S5_GROUP).transpose(0, 2, 1)[None],
        s5_d=d_s5_d, s5_b_glu=d_b_glu, ffn_conv_b=jnp.concatenate([gf0["conv_b"], gf1["conv_b"]]))
    scatters[1] = _scatter_start({("s5_w_glu", 0): d_w_glu, ("even_w_out", 0): d_w_eout}, 1,
                                 extra=[_small_sharded_pack(g_small, w), _replicated_pack(g_repl, REPLICATED_EARLY)])
    dproj, = _rows(_dproj_fn, [dq, df, di, dg, du1, du2], [], [(EVEN_IN, BF16)], [], name="even_dproj", deps=[scatters[1][4]])
    d_w_ein = _mm(hn0, dproj, ta=True, out_dtype=BF16, name="even_in_dw")
    dhn0 = _mm(dproj, w_ein, tb=True, name="even_in_dx")
    grad_x, d_mix_g0 = _rows(_rms_bwd_fn, [h0, dhn0, dh1], [norm_mix_g[0:1]], [(D_MODEL, F32)], [(1, D_MODEL)], name="mix0_dnorm")
    g_repl["norm_mix_g"] = jnp.concatenate([d_mix_g0, d_mix_g1])
    scatters[0] = _scatter_start({("even_w_in", 0): d_w_ein}, 0, extra=[_replicated_pack(g_repl, REPLICATED_LATE)])

    delta, new_m, new_v = {}, {}, {}
    per_unit, partial = {}, {}
    after = scatters[0][4]
    for stage in (4, 3, 2, 1, 0):
        partial[stage] = _scatter_finish(scatters[stage], after, stage)
        for (n, l), slabs in zip(STAGES[stage], partial[stage]):
            per_unit[n, l] = _adamw(w[n][l], slabs, mom[n][l], var[n][l], name=f"adamw_{n}_{l}")
        after = per_unit[STAGES[stage][-1]][0]
    grads = {}
    for names, slabs, tag in ((REPLICATED_EARLY, partial[1][-1], "repl"), (REPLICATED_LATE, partial[0][-1], "late"),
                              (SMALL_SHARDED_NAMES, partial[1][-2], "small")):
        total = _sum_slabs(slabs, F32, name=f"sum_g_{tag}").reshape(-1)
        grads.update(zip(names, _unpack(total, [w[n].shape for n in names])))
    for n in BIG_NAMES:
        layers = [per_unit[n, l] for l in range(w[n].shape[0])]
        grads[n], delta[n], new_m[n], new_v[n] = (jnp.stack([lay[k] for lay in layers]) for k in range(4))
    small = [n for n in W_NAMES if n not in BIG_NAMES]
    packed = [_pack([t[n].reshape(-1) for n in small], LANE, SUBLANE) for t in (w, grads, mom, var)]
    outs = _adamw(*packed, name="adamw_small")
    small_shapes = [w[n].shape for n in small]
    for dst, o_ in zip((delta, new_m, new_v), outs):
        dst.update(zip(small, _unpack(o_.reshape(-1), small_shapes)))

    return (loss, grad_x[None], *[grads[n] for n in W_NAMES], *[delta[n] for n in W_NAMES],
            *[new_m[n] for n in W_NAMES], *[new_v[n] for n in W_NAMES])
```

```python
import functools
import math

import numpy as np
import jax
import jax.numpy as jnp
from jax import lax
from jax.experimental import pallas as pl
from jax.experimental.pallas import tpu as pltpu

F32 = jnp.float32
BF16 = jnp.bfloat16
_MXU_DTYPE = jnp.bfloat16

D_MODEL = 1024
HGRN_DIM = 512
HGRN_HEAD_DIM = 128
HGRN_HEADS = 4
HGRN_CHUNK = 64
S5_DIM = 512
S5_GROUPS = 32
S5_GROUP = 16
S5_STATE = 64
S5_WIDTH = S5_GROUPS * S5_STATE
EVEN_IN = 4 * HGRN_DIM + S5_DIM
MLA_HEADS = 8
MLA_Q_RANK = 384
MLA_KV_RANK = 256
MLA_NOPE = 128
MLA_ROPE = 64
MLA_V = 128
MLA_QK = MLA_NOPE + MLA_ROPE
ODD_IN = MLA_Q_RANK + MLA_KV_RANK + MLA_ROPE
ODD_IN_PAD = MLA_Q_RANK + MLA_KV_RANK + 2 * 128
ROPE_THETA = 10000.0
D_FF = 2816
EPS = 1e-6
ADAM_LR = 0.001
ADAM_B1 = 0.9
ADAM_B2 = 0.999
ADAM_EPS = 1e-08
ADAM_WD = 0.01
ADAM_STEP = 10

N_DEV = 8
LANE = 128
SUBLANE = 8
VMEM_LIMIT_BYTES = 56 * 1024 * 1024
MESH = pl.DeviceIdType.MESH


def _cparams(sem=None):
    return pltpu.CompilerParams(dimension_semantics=sem, vmem_limit_bytes=VMEM_LIMIT_BYTES)


def _pick(n, cands):
    for c in cands:
        if n % c == 0:
            return c
    raise ValueError(f"no tile for {n} in {cands}")


def _sigmoid(x):
    return 1.0 / (1.0 + jnp.exp(-x))


def _silu(x):
    return x * _sigmoid(x)


def _gelu(x):
    return 0.5 * x * (1.0 + jnp.tanh(math.sqrt(2.0 / math.pi) * (x + 0.044715 * (x * x * x))))


def _rms(x, g):
    return x * lax.rsqrt(jnp.mean(x * x, axis=-1, keepdims=True) + EPS) * g


def _mxu(a, b, ca, cb):
    return lax.dot_general(a.astype(_MXU_DTYPE), b.astype(_MXU_DTYPE), (((ca,), (cb,)), ((), ())),
                           preferred_element_type=F32)


@functools.partial(jax.custom_vjp, nondiff_argnums=(2, 3))
def _mxu_ad(a, b, ca, cb):
    return _mxu(a, b, ca, cb)


def _mxu_ad_fwd(a, b, ca, cb):
    return _mxu(a, b, ca, cb), (a, b)


def _mxu_ad_bwd(ca, cb, saved, g):
    a, b = saved
    fa, fb = 1 - ca, 1 - cb
    da = _mxu(g, b, 1, fb) if ca == 1 else _mxu(b, g, fb, 1)
    db = _mxu(a, g, fa, 0) if cb == 0 else _mxu(g, a, 0, fa)
    return da, db


_mxu_ad.defvjp(_mxu_ad_fwd, _mxu_ad_bwd)


def _tri(n):
    row = lax.broadcasted_iota(jnp.int32, (n, n), 0)
    col = lax.broadcasted_iota(jnp.int32, (n, n), 1)
    return col <= row


def _cumsum_rows(x, reverse=False):
    n = x.shape[0]
    rowi = lax.broadcasted_iota(jnp.int32, x.shape, 0)
    s = 1
    while s < n:
        if reverse:
            x = x + jnp.where(rowi < n - s, pltpu.roll(x, n - s, 0), 0.0)
        else:
            x = x + jnp.where(rowi >= s, pltpu.roll(x, s, 0), 0.0)
        s *= 2
    return x


@jax.custom_vjp
def _cumsum_rows_ad(x):
    return _cumsum_rows(x)


def _cumsum_rows_ad_fwd(x):
    return _cumsum_rows(x), None


def _cumsum_rows_ad_bwd(_, g):
    return (_cumsum_rows(g, reverse=True),)


_cumsum_rows_ad.defvjp(_cumsum_rows_ad_fwd, _cumsum_rows_ad_bwd)


MM_VMEM_BUDGET = 36 * 1024 * 1024
MM_MAX_TILE = 1408


def _lane_divisors(n, cap, offs=()):
    return [d for d in range(min(n, cap) // LANE * LANE, 0, -LANE) if n % d == 0 and all(o % d == 0 for o in offs)]


def _mm_tiles(M, N, K, sa, sb, so, has_res, m_offs, n_offs, k_offs):
    best = None
    for tm in _lane_divisors(M, MM_MAX_TILE, m_offs):
        for tn in _lane_divisors(N, MM_MAX_TILE, n_offs):
            for tk in _lane_divisors(K, K, k_offs):
                nk = K // tk
                vmem = 2 * (tm * tk * sa + tk * tn * sb + tm * tn * so + tm * tn * 4 * has_res) + (tm * tn * 4 if nk > 1 else 0)
                if vmem <= MM_VMEM_BUDGET:
                    key = (-nk, tm * tn, tn)
                    if best is None or key > best[0]:
                        best = (key, tm, tn, tk)
                    break
    return best[1:]


def _mm(a, b, *, ta=False, tb=False, res=None, out_dtype=F32, a_cols=None, b_cols=None, deps=(), name):
    a_minor = a.shape[1] if a_cols is None else a_cols[1]
    b_minor = b.shape[1] if b_cols is None else b_cols[1]
    K, M = (a.shape[0], a_minor) if ta else (a_minor, a.shape[0])
    N = b.shape[0] if tb else b_minor
    assert (b_minor if tb else b.shape[0]) == K, (a.shape, b.shape, ta, tb)
    a_off = 0 if a_cols is None else a_cols[0]
    b_off = 0 if b_cols is None else b_cols[0]
    has_res = res is not None
    tm, tn, tk = _mm_tiles(M, N, K, a.dtype.itemsize, b.dtype.itemsize, jnp.dtype(out_dtype).itemsize, has_res,
                           (a_off,) if ta else (), () if tb else (b_off,), ((a_off,) if not ta else ()) + ((b_off,) if tb else ()))
    nk = K // tk
    am, ak = (a_off // tm, 0) if ta else (0, a_off // tk)
    bn, bk = (0, b_off // tk) if tb else (b_off // tn, 0)
    a_spec = pl.BlockSpec((tk, tm), lambda i, j, k: (k, i + am)) if ta else pl.BlockSpec((tm, tk), lambda i, j, k: (i, k + ak))
    b_spec = pl.BlockSpec((tn, tk), lambda i, j, k: (j, k + bk)) if tb else pl.BlockSpec((tk, tn), lambda i, j, k: (k, j + bn))
    o_spec = pl.BlockSpec((tm, tn), lambda i, j, k: (i, j))
    ca, cb = (0 if ta else 1), (1 if tb else 0)

    n_fixed = 2 + has_res + len(deps)

    def body(*refs):
        a_ref, b_ref = refs[0], refs[1]
        res_ref = refs[2] if has_res else None
        o_ref = refs[n_fixed]
        part = _mxu(a_ref[...], b_ref[...], ca, cb)
        if nk == 1:
            o_ref[...] = (part + res_ref[...] if has_res else part).astype(out_dtype)
            return
        acc_ref = refs[n_fixed + 1]
        k = pl.program_id(2)

        @pl.when(k == 0)
        def _():
            acc_ref[...] = part

        @pl.when(k > 0)
        def _():
            acc_ref[...] += part

        @pl.when(k == nk - 1)
        def _():
            o_ref[...] = (acc_ref[...] + res_ref[...] if has_res else acc_ref[...]).astype(out_dtype)

    ins = [a, b] + ([res] if has_res else []) + list(deps)
    in_specs = [a_spec, b_spec] + ([o_spec] if has_res else []) + [pl.BlockSpec(memory_space=pl.ANY)] * len(deps)
    return pl.pallas_call(
        body, name=name, grid=(M // tm, N // tn, nk),
        in_specs=in_specs, out_specs=o_spec,
        out_shape=jax.ShapeDtypeStruct((M, N), out_dtype),
        scratch_shapes=[pltpu.VMEM((tm, tn), F32)] if nk > 1 else [],
        compiler_params=_cparams(("parallel", "parallel", "arbitrary")),
    )(*ins)


def _rows(fn, row_ins, const_ins, row_outs, acc_outs, *, name, tm=256, deps=()):
    norm = [(r, 0, r.shape[1]) if not isinstance(r, tuple) else r for r in row_ins]
    T = norm[0][0].shape[0]
    tm = min(tm, T)
    nr, nc, no, na = len(norm), len(const_ins), len(row_outs), len(acc_outs)
    first_out = nr + nc + len(deps)

    def body(*refs):
        i = pl.program_id(0)
        vals = [r[...] for r in refs[:nr + nc]]
        outs = fn(*vals)
        for o_ref, o in zip(refs[first_out:first_out + no], outs[:no]):
            o_ref[...] = o.astype(o_ref.dtype)
        for a_ref, o in zip(refs[first_out + no:], outs[no:]):
            @pl.when(i == 0)
            def _(a_ref=a_ref, o=o):
                a_ref[...] = o

            @pl.when(i > 0)
            def _(a_ref=a_ref, o=o):
                a_ref[...] += o

    in_specs = []
    for arr, off, w in norm:
        assert off % w == 0, (off, w)
        in_specs.append(pl.BlockSpec((tm, w), lambda i, b=off // w: (i, b)))
    for c in const_ins:
        in_specs.append(pl.BlockSpec(c.shape, lambda i: (0, 0)))
    in_specs += [pl.BlockSpec(memory_space=pl.ANY)] * len(deps)
    out_specs = [pl.BlockSpec((tm, w), lambda i: (i, 0)) for w, _ in row_outs]
    out_specs += [pl.BlockSpec(s, lambda i: (0, 0)) for s in acc_outs]
    out_shape = [jax.ShapeDtypeStruct((T, w), dt) for w, dt in row_outs]
    out_shape += [jax.ShapeDtypeStruct(s, F32) for s in acc_outs]
    return pl.pallas_call(
        body, name=name, grid=(T // tm,), in_specs=in_specs, out_specs=out_specs, out_shape=out_shape,
        compiler_params=_cparams(("arbitrary",)),
    )(*[n[0] for n in norm], *const_ins, *deps)


FFN_COL_TILE = LANE
FFN_ROW_CHUNK = 512


FFN_HALO = 2 * SUBLANE


def _shift_down(ext, s, rows):
    return pltpu.roll(ext, s, 0)[FFN_HALO:FFN_HALO + rows]


def _shift_up(ext, s, rows):
    return pltpu.roll(ext, rows + FFN_HALO - s, 0)[:rows]


def _ffn_chunks(T):
    r = min(FFN_ROW_CHUNK, T)
    return r, T // r


def _ext_before(ref, c, r):
    if c == 0:
        return jnp.concatenate([jnp.zeros((FFN_HALO, ref.shape[1]), F32), ref[0:r, :].astype(F32)], axis=0)
    return ref[c * r - FFN_HALO:(c + 1) * r, :].astype(F32)


def _ext_after(ref, c, r, nch):
    if c == nch - 1:
        return jnp.concatenate([ref[c * r:(c + 1) * r, :].astype(F32), jnp.zeros((FFN_HALO, ref.shape[1]), F32)], axis=0)
    return ref[c * r:(c + 1) * r + FFN_HALO, :].astype(F32)


def _ffn_mid_fwd(au, conv_w, conv_b, *, name):
    T = au.shape[0]
    tc = FFN_COL_TILE
    ncol = D_FF // tc
    r, nch = _ffn_chunks(T)

    def body(a_ref, u_ref, w_ref, b_ref, z_ref):
        w0, w1, w2, bias = w_ref[0:1, :], w_ref[1:2, :], w_ref[2:3, :], b_ref[...]
        for c in range(nch):
            ext = _ext_before(a_ref, c, r)
            pre = w0 * _shift_down(ext, 2, r) + w1 * _shift_down(ext, 1, r) + w2 * ext[FFN_HALO:] + bias
            z_ref[c * r:(c + 1) * r, :] = (_silu(pre) * u_ref[c * r:(c + 1) * r, :].astype(F32)).astype(z_ref.dtype)

    return pl.pallas_call(
        body, name=name, grid=(ncol,),
        in_specs=[pl.BlockSpec((T, tc), lambda j: (0, j)), pl.BlockSpec((T, tc), lambda j: (0, j + ncol)),
                  pl.BlockSpec((3, tc), lambda j: (0, j)), pl.BlockSpec((1, tc), lambda j: (0, j))],
        out_specs=pl.BlockSpec((T, tc), lambda j: (0, j)),
        out_shape=jax.ShapeDtypeStruct((T, D_FF), BF16),
        compiler_params=_cparams(("parallel",)),
    )(au, au, conv_w, conv_b)


def _ffn_mid_bwd(au, dz, conv_w, conv_b, *, name):
    T = au.shape[0]
    tc = FFN_COL_TILE
    ncol = D_FF // tc
    r, nch = _ffn_chunks(T)

    def body(a_ref, u_ref, dz_ref, w_ref, b_ref, da_ref, du_ref, dw_ref, db_ref, dpre_ref):
        w0, w1, w2, bias = w_ref[0:1, :], w_ref[1:2, :], w_ref[2:3, :], b_ref[...]
        dw0 = jnp.zeros((1, tc), F32)
        dw1 = jnp.zeros((1, tc), F32)
        dw2 = jnp.zeros((1, tc), F32)
        db = jnp.zeros((1, tc), F32)
        for c in range(nch):
            rows = slice(c * r, (c + 1) * r)
            ext = _ext_before(a_ref, c, r)
            a2, a1, a0 = _shift_down(ext, 2, r), _shift_down(ext, 1, r), ext[FFN_HALO:]
            pre = w0 * a2 + w1 * a1 + w2 * a0 + bias
            sg = _sigmoid(pre)
            act = pre * sg
            dzc = dz_ref[rows, :].astype(F32)
            du_ref[rows, :] = (dzc * act).astype(du_ref.dtype)
            dpre = dzc * u_ref[rows, :].astype(F32) * (sg * (1.0 + pre * (1.0 - sg)))
            dpre_ref[rows, :] = dpre
            dw0 += jnp.sum(dpre * a2, axis=0, keepdims=True)
            dw1 += jnp.sum(dpre * a1, axis=0, keepdims=True)
            dw2 += jnp.sum(dpre * a0, axis=0, keepdims=True)
            db += jnp.sum(dpre, axis=0, keepdims=True)
        for c in range(nch):
            ext = _ext_after(dpre_ref, c, r, nch)
            da = w0 * _shift_up(ext, 2, r) + w1 * _shift_up(ext, 1, r) + w2 * ext[:r]
            da_ref[c * r:(c + 1) * r, :] = da.astype(da_ref.dtype)
        dw_ref[0:1, :] = dw0
        dw_ref[1:2, :] = dw1
        dw_ref[2:3, :] = dw2
        db_ref[...] = db

    col = lambda j: (0, j)
    return pl.pallas_call(
        body, name=name, grid=(ncol,),
        in_specs=[pl.BlockSpec((T, tc), col), pl.BlockSpec((T, tc), lambda j: (0, j + ncol)), pl.BlockSpec((T, tc), col),
                  pl.BlockSpec((3, tc), col), pl.BlockSpec((1, tc), col)],
        out_specs=[pl.BlockSpec((T, tc), col), pl.BlockSpec((T, tc), col), pl.BlockSpec((3, tc), col), pl.BlockSpec((1, tc), col)],
        out_shape=[jax.ShapeDtypeStruct((T, D_FF), BF16), jax.ShapeDtypeStruct((T, D_FF), BF16),
                   jax.ShapeDtypeStruct((3, D_FF), F32), jax.ShapeDtypeStruct((1, D_FF), F32)],
        scratch_shapes=[pltpu.VMEM((T, tc), F32)],
        compiler_params=_cparams(("parallel",)),
    )(au, au, dz, conv_w, conv_b)


HGRN_BLOCK = 256


def _hgrn_chunk(dot, cumsum, q, f, i, g, lb, ng, st):
    C = q.shape[0]
    forget = lb + (1.0 - lb) * _sigmoid(f)
    k = 1.0 - forget
    b = cumsum(jnp.log(forget))
    b_last = b[C - 1:C, :]
    qd = q * jnp.exp(b)
    kd = k * jnp.exp(-b)
    att = jnp.where(_tri(C), dot(qd, kd, 1, 1), 0.0)
    o = dot(att, i, 1, 0) + dot(qd, st, 1, 1)
    st_new = st * jnp.exp(b_last) + dot(i, k * jnp.exp(b_last - b), 0, 0)
    on = o * lax.rsqrt(jnp.mean(o * o, axis=-1, keepdims=True) + EPS) * ng
    return on * _silu(g), st_new


def _hgrn_specs(T, rev):
    tb = min(HGRN_BLOCK, T)
    nb = T // tb
    blk = (lambda n: nb - 1 - n) if rev else (lambda n: n)
    hd = HGRN_HEAD_DIM
    proj_specs = [pl.BlockSpec((tb, HGRN_DIM), lambda n, k=k: (blk(n), k)) for k in range(4)]
    vec_spec = pl.BlockSpec((1, HGRN_DIM), lambda n: (0, 0))
    tok_spec = pl.BlockSpec((tb, HGRN_DIM), lambda n: (blk(n), 0))
    st_spec = pl.BlockSpec((HGRN_HEADS, tb // HGRN_CHUNK, hd, hd), lambda n: (0, blk(n), 0, 0))
    return tb, nb, proj_specs, vec_spec, tok_spec, st_spec


def _head_cols(h):
    return slice(h * HGRN_HEAD_DIM, (h + 1) * HGRN_HEAD_DIM)


def _hgrn_fwd(proj, lb, ng, *, name):
    T = proj.shape[0]
    tb, nb, proj_specs, vec_spec, tok_spec, st_spec = _hgrn_specs(T, False)
    nsub = tb // HGRN_CHUNK
    hd = HGRN_HEAD_DIM

    def body(q_ref, f_ref, i_ref, g_ref, lb_ref, ng_ref, y_ref, sts_ref, st_ref):
        @pl.when(pl.program_id(0) == 0)
        def _():
            st_ref[...] = jnp.zeros_like(st_ref)

        st = [st_ref[h] for h in range(HGRN_HEADS)]
        for s in range(nsub):
            rows = slice(s * HGRN_CHUNK, (s + 1) * HGRN_CHUNK)
            for h in range(HGRN_HEADS):
                cols = _head_cols(h)
                sts_ref[h, s] = st[h]
                y, st[h] = _hgrn_chunk(_mxu, _cumsum_rows, q_ref[rows, cols], f_ref[rows, cols], i_ref[rows, cols],
                                       g_ref[rows, cols], lb_ref[:, cols], ng_ref[:, cols], st[h])
                y_ref[rows, cols] = y
        for h in range(HGRN_HEADS):
            st_ref[h] = st[h]

    return pl.pallas_call(
        body, name=name, grid=(nb,),
        in_specs=proj_specs + [vec_spec, vec_spec], out_specs=[tok_spec, st_spec],
        out_shape=[jax.ShapeDtypeStruct((T, HGRN_DIM), F32),
                   jax.ShapeDtypeStruct((HGRN_HEADS, T // HGRN_CHUNK, hd, hd), F32)],
        scratch_shapes=[pltpu.VMEM((HGRN_HEADS, hd, hd), F32)],
        compiler_params=_cparams(("arbitrary",)),
    )(proj, proj, proj, proj, lb, ng)


def _hgrn_bwd(proj, lb, ng, states, dmix, *, name):
    T = proj.shape[0]
    tb, nb, proj_specs, vec_spec, tok_spec, st_spec = _hgrn_specs(T, True)
    nsub = tb // HGRN_CHUNK
    hd = HGRN_HEAD_DIM
    chunk = functools.partial(_hgrn_chunk, _mxu_ad, _cumsum_rows_ad)

    def body(q_ref, f_ref, i_ref, g_ref, lb_ref, ng_ref, sts_ref, dy_ref,
             dq_ref, df_ref, di_ref, dg_ref, dlb_ref, dng_ref, dst_ref):
        @pl.when(pl.program_id(0) == 0)
        def _():
            dst_ref[...] = jnp.zeros_like(dst_ref)
            dlb_ref[...] = jnp.zeros_like(dlb_ref)
            dng_ref[...] = jnp.zeros_like(dng_ref)

        dst = [dst_ref[h] for h in range(HGRN_HEADS)]
        dlb = [jnp.zeros((1, hd), F32)] * HGRN_HEADS
        dng = [jnp.zeros((1, hd), F32)] * HGRN_HEADS
        for s in reversed(range(nsub)):
            rows = slice(s * HGRN_CHUNK, (s + 1) * HGRN_CHUNK)
            for h in range(HGRN_HEADS):
                cols = _head_cols(h)
                _, vjp = jax.vjp(chunk, q_ref[rows, cols], f_ref[rows, cols], i_ref[rows, cols], g_ref[rows, cols],
                                 lb_ref[:, cols], ng_ref[:, cols], sts_ref[h, s])
                dq, df, di, dg, dlb_s, dng_s, dst[h] = vjp((dy_ref[rows, cols], dst[h]))
                dq_ref[rows, cols] = dq
                df_ref[rows, cols] = df
                di_ref[rows, cols] = di
                dg_ref[rows, cols] = dg
                dlb[h] = dlb[h] + dlb_s
                dng[h] = dng[h] + dng_s
        for h in range(HGRN_HEADS):
            dst_ref[h] = dst[h]
            dlb_ref[:, _head_cols(h)] += dlb[h]
            dng_ref[:, _head_cols(h)] += dng[h]

    tok_out = jax.ShapeDtypeStruct((T, HGRN_DIM), F32)
    vec_out = jax.ShapeDtypeStruct((1, HGRN_DIM), F32)
    return pl.pallas_call(
        body, name=name, grid=(nb,),
        in_specs=proj_specs + [vec_spec, vec_spec, st_spec, tok_spec],
        out_specs=[tok_spec] * 4 + [vec_spec, vec_spec],
        out_shape=[tok_out] * 4 + [vec_out, vec_out],
        scratch_shapes=[pltpu.VMEM((HGRN_HEADS, hd, hd), F32)],
        compiler_params=_cparams(("arbitrary",)),
    )(proj, proj, proj, proj, lb, ng, states, dmix)


S5_LANES = 512
S5_ROWS = 512


def _cmul(ar, ai, br, bi):
    return ar * br - ai * bi, ar * bi + ai * br


def _power_table(ar, ai, exps):
    a2 = _cmul(ar, ai, ar, ai)
    a4 = _cmul(*a2, *a2)
    e = exps - 1
    pr = jnp.broadcast_to(ar, exps.shape)
    pi = jnp.broadcast_to(ai, exps.shape)
    for bit, (fr, fi) in enumerate(((ar, ai), a2, a4)):
        nr, ni = _cmul(pr, pi, fr, fi)
        on = ((e >> bit) & 1) == 1
        pr, pi = jnp.where(on, nr, pr), jnp.where(on, ni, pi)
    return pr, pi, a2, a4


def _s5_scan_fwd(bu, a_re, a_im, *, name):
    T = bu.shape[0]
    w, tr = S5_LANES, min(S5_ROWS, T)
    ncol, nt = S5_WIDTH // w, T // tr

    def body(br_ref, bi_ref, ar_ref, ai_ref, sr_ref, si_ref, carry_ref):
        @pl.when(pl.program_id(1) == 0)
        def _():
            carry_ref[...] = jnp.zeros_like(carry_ref)

        ar, ai = ar_ref[...], ai_ref[...]
        rowi = lax.broadcasted_iota(jnp.int32, (SUBLANE, w), 0)
        pr, pi, a2, a4 = _power_table(ar, ai, rowi + 1)

        def tile(i, carry):
            cr, ci = carry
            rows = pl.ds(pl.multiple_of(i * SUBLANE, SUBLANE), SUBLANE)
            xr, xi = br_ref[rows, :], bi_ref[rows, :]
            for s, (fr, fi) in ((1, (ar, ai)), (2, a2), (4, a4)):
                keep = rowi >= s
                zr = jnp.where(keep, pltpu.roll(xr, s, 0), 0.0)
                zi = jnp.where(keep, pltpu.roll(xi, s, 0), 0.0)
                xr, xi = xr + fr * zr - fi * zi, xi + fr * zi + fi * zr
            xr, xi = xr + pr * cr - pi * ci, xi + pr * ci + pi * cr
            sr_ref[rows, :] = xr
            si_ref[rows, :] = xi
            return xr[SUBLANE - 1:SUBLANE, :], xi[SUBLANE - 1:SUBLANE, :]

        cr, ci = lax.fori_loop(0, tr // SUBLANE, tile, (carry_ref[0:1, :], carry_ref[1:2, :]))
        carry_ref[0:1, :] = cr
        carry_ref[1:2, :] = ci

    out = jax.ShapeDtypeStruct((T, S5_WIDTH), F32)
    return pl.pallas_call(
        body, name=name, grid=(ncol, nt),
        in_specs=[pl.BlockSpec((tr, w), lambda j, t: (t, j)), pl.BlockSpec((tr, w), lambda j, t: (t, j + ncol)),
                  pl.BlockSpec((1, w), lambda j, t: (0, j)), pl.BlockSpec((1, w), lambda j, t: (0, j))],
        out_specs=[pl.BlockSpec((tr, w), lambda j, t: (t, j))] * 2,
        out_shape=[out, out],
        scratch_shapes=[pltpu.VMEM((2, w), F32)],
        compiler_params=_cparams(("parallel", "arbitrary")),
    )(bu, bu, a_re, a_im)


def _s5_scan_bwd(g_re, g_im, s_re, s_im, a_re, a_im, *, name):
    T = g_re.shape[0]
    w, tr = S5_LANES, min(S5_ROWS, T)
    ncol, nt = S5_WIDTH // w, T // tr
    ntile = tr // SUBLANE

    def body(gr_ref, gi_ref, sr_ref, si_ref, ar_ref, ai_ref, lr_ref, li_ref, dar_ref, dai_ref, carry_ref):
        @pl.when(pl.program_id(1) == 0)
        def _():
            carry_ref[...] = jnp.zeros_like(carry_ref)
            dar_ref[...] = jnp.zeros_like(dar_ref)
            dai_ref[...] = jnp.zeros_like(dai_ref)

        ar, ai = ar_ref[...], -ai_ref[...]
        rowi = lax.broadcasted_iota(jnp.int32, (SUBLANE, w), 0)
        pr, pi, a2, a4 = _power_table(ar, ai, SUBLANE - rowi)
        last = rowi == SUBLANE - 1

        def tile(i, carry):
            cr, ci, dar, dai = carry
            rows = pl.ds(pl.multiple_of((ntile - 1 - i) * SUBLANE, SUBLANE), SUBLANE)
            xr, xi = gr_ref[rows, :], gi_ref[rows, :]
            for s, (fr, fi) in ((1, (ar, ai)), (2, a2), (4, a4)):
                keep = rowi < SUBLANE - s
                zr = jnp.where(keep, pltpu.roll(xr, SUBLANE - s, 0), 0.0)
                zi = jnp.where(keep, pltpu.roll(xi, SUBLANE - s, 0), 0.0)
                xr, xi = xr + fr * zr - fi * zi, xi + fr * zi + fi * zr
            xr, xi = xr + pr * cr - pi * ci, xi + pr * ci + pi * cr
            lr_ref[rows, :] = xr
            li_ref[rows, :] = xi
            nr = jnp.where(last, cr, pltpu.roll(xr, SUBLANE - 1, 0))
            ni = jnp.where(last, ci, pltpu.roll(xi, SUBLANE - 1, 0))
            sr, si = sr_ref[rows, :], si_ref[rows, :]
            return xr[0:1, :], xi[0:1, :], dar + nr * sr + ni * si, dai + ni * sr - nr * si

        cr, ci, dar, dai = lax.fori_loop(
            0, ntile, tile, (carry_ref[0:1, :], carry_ref[1:2, :], jnp.zeros((SUBLANE, w), F32), jnp.zeros((SUBLANE, w), F32)))
        carry_ref[0:1, :] = cr
        carry_ref[1:2, :] = ci
        dar_ref[...] += dar
        dai_ref[...] += dai

    tok = pl.BlockSpec((tr, w), lambda j, t: (nt - 1 - t, j))
    vec = pl.BlockSpec((1, w), lambda j, t: (0, j))
    acc = pl.BlockSpec((SUBLANE, w), lambda j, t: (0, j))
    out = jax.ShapeDtypeStruct((T, S5_WIDTH), F32)
    accs = jax.ShapeDtypeStruct((SUBLANE, S5_WIDTH), F32)
    return pl.pallas_call(
        body, name=name, grid=(ncol, nt),
        in_specs=[tok, tok, tok, tok, vec, vec], out_specs=[tok, tok, acc, acc],
        out_shape=[out, out, accs, accs],
        scratch_shapes=[pltpu.VMEM((2, w), F32)],
        compiler_params=_cparams(("parallel", "arbitrary")),
    )(g_re, g_im, s_re, s_im, a_re, a_im)


ATTN_BLOCK = 512
_NEG = -1e30


def _qk_cat(nope, rope):
    return jnp.concatenate([nope.astype(_MXU_DTYPE), rope.astype(_MXU_DTYPE)], axis=1)


def _attn_scores(q, k, q0, k0):
    s = _mxu(q, k, 1, 1) * (MLA_QK ** -0.5)
    qpos = q0 + lax.broadcasted_iota(jnp.int32, s.shape, 0)
    kpos = k0 + lax.broadcasted_iota(jnp.int32, s.shape, 1)
    return s, kpos <= qpos


def _attn_fwd(q_all, q_rope, kv, k_rope, *, name):
    T = q_all.shape[0]
    tq = min(ATTN_BLOCK, T)
    nq = T // tq

    def body(qn_ref, qr_ref, kn_ref, v_ref, kr_ref, o_ref, lse_ref):
        i = pl.program_id(1)
        q = _qk_cat(qn_ref[...], qr_ref[0])

        def step(j, carry):
            m, l, acc = carry
            ks = pl.ds(pl.multiple_of(j * tq, tq), tq)
            s, ok = _attn_scores(q, _qk_cat(kn_ref[ks, :], kr_ref[ks, :]), i * tq, j * tq)
            s = jnp.where(ok, s, _NEG)
            m_new = jnp.maximum(m, jnp.max(s, axis=-1, keepdims=True))
            p = jnp.exp(s - m_new)
            alpha = jnp.exp(m - m_new)
            return m_new, alpha * l + jnp.sum(p, axis=-1, keepdims=True), alpha * acc + _mxu(p, v_ref[ks, :], 1, 0)

        m, l, acc = lax.fori_loop(0, i + 1, step, (jnp.full((tq, 1), _NEG, F32), jnp.zeros((tq, 1), F32), jnp.zeros((tq, MLA_V), F32)))
        o_ref[...] = acc / l
        lse_ref[0] = m + jnp.log(l)

    return pl.pallas_call(
        body, name=name, grid=(MLA_HEADS, nq),
        in_specs=[pl.BlockSpec((tq, MLA_NOPE), lambda h, i: (i, h)), pl.BlockSpec((1, tq, MLA_ROPE), lambda h, i: (h, i, 0)),
                  pl.BlockSpec((T, MLA_NOPE), lambda h, i: (0, 2 * h)), pl.BlockSpec((T, MLA_V), lambda h, i: (0, 2 * h + 1)),
                  pl.BlockSpec((T, MLA_ROPE), lambda h, i: (0, 0))],
        out_specs=[pl.BlockSpec((tq, MLA_V), lambda h, i: (i, h)), pl.BlockSpec((1, tq, 1), lambda h, i: (h, i, 0))],
        out_shape=[jax.ShapeDtypeStruct((T, MLA_HEADS * MLA_V), F32), jax.ShapeDtypeStruct((MLA_HEADS, T, 1), F32)],
        compiler_params=_cparams(("arbitrary", "arbitrary")),
    )(q_all, q_rope, kv, kv, k_rope)


def _attn_bwd(q_all, q_rope, kv, k_rope, o, lse, do, *, name):
    T = q_all.shape[0]
    tk = min(ATTN_BLOCK, T)
    nk = T // tk
    scale = MLA_QK ** -0.5

    def body(qn_ref, qr_ref, kv_ref, kr_ref, o_ref, lse_ref, do_ref, dqn_ref, dqr_ref, dkv_ref, dkr_ref, delta_ref):
        h, j = pl.program_id(0), pl.program_id(1)

        @pl.when(j == 0)
        def _():
            dqn_ref[...] = jnp.zeros_like(dqn_ref)
            dqr_ref[...] = jnp.zeros_like(dqr_ref)
            delta_ref[...] = jnp.sum(do_ref[...] * o_ref[...], axis=-1, keepdims=True)

        @pl.when((j == 0) & (h == 0))
        def _():
            dkr_ref[...] = jnp.zeros_like(dkr_ref)

        krows = pl.ds(pl.multiple_of(j * tk, tk), tk)
        k = _qk_cat(kv_ref[:, :MLA_NOPE], kr_ref[krows, :])
        v = kv_ref[:, MLA_NOPE:].astype(_MXU_DTYPE)

        def step(i, carry):
            dk, dv = carry
            qs = pl.ds(pl.multiple_of(i * tk, tk), tk)
            q, dob = _qk_cat(qn_ref[qs, :], qr_ref[0, qs, :]), do_ref[qs, :].astype(_MXU_DTYPE)
            s, ok = _attn_scores(q, k, i * tk, j * tk)
            p = jnp.where(ok, jnp.exp(s - lse_ref[0, qs, :]), 0.0)
            ds = p * (_mxu(dob, v, 1, 1) - delta_ref[qs, :]) * scale
            dq = _mxu(ds, k, 1, 0)
            dqn_ref[qs, :] += dq[:, :MLA_NOPE]
            dqr_ref[0, qs, :] += dq[:, MLA_NOPE:]
            return dk + _mxu(ds, q, 0, 0), dv + _mxu(p, dob, 0, 0)

        dk, dv = lax.fori_loop(j, nk, step, (jnp.zeros((tk, MLA_QK), F32), jnp.zeros((tk, MLA_V), F32)))
        dkv_ref[:, :MLA_NOPE] = dk[:, :MLA_NOPE].astype(dkv_ref.dtype)
        dkv_ref[:, MLA_NOPE:] = dv.astype(dkv_ref.dtype)
        dkr_ref[krows, :] += dk[:, MLA_NOPE:]

    head_cols = pl.BlockSpec((T, MLA_NOPE), lambda h, j: (0, h))
    head_rope = pl.BlockSpec((1, T, MLA_ROPE), lambda h, j: (h, 0, 0))
    kv_spec = pl.BlockSpec((tk, MLA_NOPE + MLA_V), lambda h, j: (j, h))
    kr_spec = pl.BlockSpec((T, MLA_ROPE), lambda h, j: (0, 0))
    return pl.pallas_call(
        body, name=name, grid=(MLA_HEADS, nk),
        in_specs=[head_cols, head_rope, kv_spec, kr_spec, head_cols, pl.BlockSpec((1, T, 1), lambda h, j: (h, 0, 0)), head_cols],
        out_specs=[head_cols, head_rope, kv_spec, kr_spec],
        out_shape=[jax.ShapeDtypeStruct((T, MLA_HEADS * MLA_NOPE), F32), jax.ShapeDtypeStruct((MLA_HEADS, T, MLA_ROPE), F32),
                   jax.ShapeDtypeStruct((T, MLA_HEADS * (MLA_NOPE + MLA_V)), BF16), jax.ShapeDtypeStruct((T, MLA_ROPE), F32)],
        scratch_shapes=[pltpu.VMEM((T, 1), F32)],
        compiler_params=_cparams(("arbitrary", "arbitrary")),
    )(q_all, q_rope, kv, k_rope, o, lse, do)


def _s5_discretize(a_re, a_im, log_dt, bt_re, bt_im, lb_logits):
    dt = jnp.exp(log_dt)
    mag = jnp.exp(a_re * dt)
    abr, abi = mag * jnp.cos(a_im * dt), mag * jnp.sin(a_im * dt)
    den = a_re * a_re + a_im * a_im
    xr, xi = abr - 1.0, abi
    cr = ((xr * a_re + xi * a_im) / den)[:, None, :]
    ci = ((xi * a_re - xr * a_im) / den)[:, None, :]
    e = jnp.exp(lb_logits - jnp.max(lb_logits, axis=0, keepdims=True))
    lb = e[0:1, :] / jnp.sum(e, axis=0, keepdims=True)
    return abr, abi, cr * bt_re - ci * bt_im, cr * bt_im + ci * bt_re, lb


def _whole(shape):
    return pl.BlockSpec(shape, lambda: (0,) * len(shape))


def _s5_params_fwd(a_re, a_im, log_dt, bt_re, bt_im, lb_logits):
    ins = (a_re, a_im, log_dt, bt_re, bt_im, lb_logits)
    outs = [jax.ShapeDtypeStruct(s, F32) for s in (a_re.shape, a_re.shape, bt_re.shape, bt_re.shape, (1, lb_logits.shape[1]))]

    def body(*refs):
        res = _s5_discretize(*[r[...] for r in refs[:6]])
        for o_ref, o in zip(refs[6:], res):
            o_ref[...] = o

    return pl.pallas_call(body, name="s5_params_fwd", in_specs=[_whole(a.shape) for a in ins],
                          out_specs=[_whole(o.shape) for o in outs], out_shape=outs, compiler_params=_cparams())(*ins)


def _s5_params_bwd(a_re, a_im, log_dt, bt_re, bt_im, lb_logits, d_abr, d_abi, d_bbr, d_bbi, d_lb):
    ins = (a_re, a_im, log_dt, bt_re, bt_im, lb_logits, d_abr, d_abi, d_bbr, d_bbi, d_lb)
    outs = [jax.ShapeDtypeStruct(a.shape, F32) for a in ins[:6]]

    def body(*refs):
        _, vjp = jax.vjp(_s5_discretize, *[r[...] for r in refs[:6]])
        for o_ref, o in zip(refs[11:], vjp(tuple(r[...] for r in refs[6:11]))):
            o_ref[...] = o

    return pl.pallas_call(body, name="s5_params_bwd", in_specs=[_whole(a.shape) for a in ins],
                          out_specs=[_whole(o.shape) for o in outs], out_shape=outs, compiler_params=_cparams())(*ins)


def _adamw(w, g, m, v, *, name):
    R, C = w.shape
    tr = _pick(R, (256, 128, 64, 32, 16, 8)) if R % SUBLANE == 0 else R
    slabs = g.shape[0] if g.ndim == 3 else 0

    def body(w_ref, g_ref, m_ref, v_ref, *outs):
        if slabs:
            gv = g_ref[0].astype(F32)
            for s in range(1, slabs):
                gv = gv + g_ref[s].astype(F32)
            outs[0][...] = gv
            outs = outs[1:]
        else:
            gv = g_ref[...]
        d_ref, mo_ref, vo_ref = outs
        m2 = ADAM_B1 * m_ref[...] + (1.0 - ADAM_B1) * gv
        v2 = ADAM_B2 * v_ref[...] + (1.0 - ADAM_B2) * (gv * gv)
        m_hat = m2 / (1.0 - ADAM_B1 ** ADAM_STEP)
        v_hat = v2 / (1.0 - ADAM_B2 ** ADAM_STEP)
        d_ref[...] = -ADAM_LR * (m_hat / (jnp.sqrt(v_hat) + ADAM_EPS) + ADAM_WD * w_ref[...])
        mo_ref[...] = m2
        vo_ref[...] = v2

    spec = pl.BlockSpec((tr, C), lambda i: (i, 0))
    g_spec = pl.BlockSpec((slabs, tr, C), lambda i: (0, i, 0)) if slabs else spec
    out = jax.ShapeDtypeStruct((R, C), F32)
    n_out = 4 if slabs else 3
    return pl.pallas_call(body, name=name, grid=(R // tr,), in_specs=[spec, g_spec, spec, spec], out_specs=[spec] * n_out,
                          out_shape=[out] * n_out, compiler_params=_cparams(("parallel",)))(w, g, m, v)


N_CHIPS = 4
N_CORES = 2


_FLIPS = tuple((dx, dy, dc) for dx in (0, 1) for dy in (0, 1) for dc in (0, 1) if (dx, dy, dc) != (0, 0, 0))


_HBM = pl.BlockSpec(memory_space=pltpu.HBM)
_SEM = pl.BlockSpec(memory_space=pltpu.SEMAPHORE)
_SPLIT_COPY = pltpu.CompilerParams(has_side_effects=pltpu.SideEffectType.DATAFLOW_SIDE_EFFECTING)


def _exchange_copies(src_refs, land_refs, send_sems, recv_sems, scatter, arriving):
    x, y, c = lax.axis_index("x"), lax.axis_index("y"), lax.axis_index("c")
    me_chip = 2 * x + y
    copies = []
    for a, (s_ref, l_ref) in enumerate(zip(src_refs, land_refs)):
        for j, (dx, dy, dc) in enumerate(_FLIPS):
            px, py, pc = (1 - x if dx else x), (1 - y if dy else y), (1 - c if dc else c)
            k = a * len(_FLIPS) + j
            p_chip = 2 * px + py
            copies.append(pltpu.make_async_remote_copy(
                src_ref=s_ref.at[p_chip, pc] if scatter else s_ref, dst_ref=l_ref.at[p_chip, pc] if arriving else l_ref.at[me_chip, c],
                send_sem=send_sems.at[k], recv_sem=recv_sems.at[k], device_id=(px, py, pc), device_id_type=MESH))
    return copies


def _exchange_start(srcs, *, scatter, name, after=()):
    n_arr = len(srcs)
    n_sem = n_arr * len(_FLIPS)
    n_in = 2 * n_arr + len(after)
    lands = [lax.empty(s.shape if scatter else (N_CHIPS, N_CORES) + s.shape, s.dtype) for s in srcs]

    def body(*refs):
        src_refs, land_refs = refs[:n_arr], refs[n_arr:2 * n_arr]
        for cp in _exchange_copies(src_refs, land_refs, refs[n_in], refs[n_in + 1], scatter, arriving=False):
            cp.start()
        refs[-1][...] = jnp.zeros_like(refs[-1])

    thru = [pltpu.HBM(a.shape, a.dtype) for a in srcs + lands]
    outs = pl.pallas_call(
        body, name=name,
        out_shape=(pltpu.SemaphoreType.DMA((n_sem,)), pltpu.SemaphoreType.DMA((n_sem,)), *thru,
                   jax.ShapeDtypeStruct((SUBLANE, LANE), F32)),
        in_specs=[_HBM] * (2 * n_arr) + [pl.BlockSpec(memory_space=pl.ANY)] * len(after),
        out_specs=(_SEM, _SEM, *[_HBM] * (2 * n_arr), pl.BlockSpec(memory_space=pltpu.VMEM)),
        input_output_aliases={i: 2 + i for i in range(2 * n_arr)}, compiler_params=_SPLIT_COPY,
    )(*[pltpu.with_memory_space_constraint(a, pltpu.HBM) for a in srcs + lands], *after)
    return outs[0], outs[1], list(outs[2:2 + n_arr]), list(outs[2 + n_arr:2 + 2 * n_arr]), outs[-1]


def _exchange_wait(started, after, *, scatter, name):
    send_sems, recv_sems, srcs, lands, _ = started
    n_arr = len(srcs)

    def body(*refs):
        src_refs, land_refs = refs[:n_arr], refs[n_arr:2 * n_arr]
        for cp in _exchange_copies(src_refs, land_refs, refs[2 * n_arr], refs[2 * n_arr + 1], scatter, arriving=True):
            cp.wait_send()
            cp.wait_recv()

    outs = pl.pallas_call(
        body, name=name, out_shape=[pltpu.HBM(a.shape, a.dtype) for a in srcs + lands],
        in_specs=[_HBM] * (2 * n_arr) + [_SEM, _SEM, pl.BlockSpec(memory_space=pl.ANY)], out_specs=[_HBM] * (2 * n_arr),
        input_output_aliases={i: i for i in range(2 * n_arr)}, compiler_params=_SPLIT_COPY,
    )(*srcs, *lands, send_sems, recv_sems, after)
    return list(outs[:n_arr]), list(outs[n_arr:])


def _with_own(land, own):
    me_chip = 2 * lax.axis_index("x") + lax.axis_index("y")
    return lax.dynamic_update_slice(land, own[None, None], (me_chip, lax.axis_index("c")) + (0,) * own.ndim)


def _sum_slabs(buf, out_dtype, *, name):
    G, R, C = buf.shape
    tr = _pick(R, (512, 256, 128, 64, 32, 16, 8))

    def body(b_ref, o_ref):
        acc = b_ref[0].astype(F32)
        for s in range(1, G):
            acc = acc + b_ref[s].astype(F32)
        o_ref[...] = acc.astype(out_dtype)

    return pl.pallas_call(body, name=name, grid=(R // tr,), in_specs=[pl.BlockSpec((G, tr, C), lambda i: (0, i, 0))],
                          out_specs=pl.BlockSpec((tr, C), lambda i: (i, 0)), out_shape=jax.ShapeDtypeStruct((R, C), out_dtype),
                          compiler_params=_cparams(("parallel",)))(buf)


def _rms_fwd_fn(h, g):
    return (_rms(h, g),)


def _rms_bwd_fn(h, dhn, dres, g):
    _, vjp = jax.vjp(_rms, h, g)
    dh, dg = vjp(dhn)
    return dh + dres, dg


def _loss_fn(h, tgt, g):
    y, vjp = jax.vjp(_rms, h, g)
    diff = y - tgt
    dh, dg = vjp(diff * (1.0 / D_MODEL))
    return dh, dg, (0.5 / D_MODEL) * jnp.sum(diff * diff, axis=0, keepdims=True)


def _s5_act(ys, u, d):
    return _gelu(ys + d * u)


def _s5_gate(z, gl, b):
    return z * _sigmoid(gl + b)


def _s5_act_fn(ys, u, d):
    return (_s5_act(ys, u, d),)


def _s5_mix_fn(ya, z, gl, b):
    return (jnp.concatenate([ya, _s5_gate(z, gl, b)], axis=1),)


def _s5_gate_bwd_fn(z, gl, dyb, b):
    _, vjp = jax.vjp(_s5_gate, z, gl, b)
    return vjp(dyb)


def _s5_act_bwd_fn(ys, u, dz1, dz2, d):
    _, vjp = jax.vjp(_s5_act, ys, u, d)
    return vjp(dz1 + dz2)


def _dproj_fn(dq, df, di, dg, du1, du2):
    return (jnp.concatenate([dq, df, di, dg, du1 + du2], axis=1),)


def _rope_pair(r1, r2, pos, freqs):
    ang = pos.astype(F32) * freqs
    c, s = jnp.cos(ang), jnp.sin(ang)
    return r1 * c - r2 * s, r1 * s + r2 * c


_ODD_SPLITS = (0, MLA_Q_RANK, MLA_Q_RANK + MLA_KV_RANK, MLA_Q_RANK + MLA_KV_RANK + LANE, ODD_IN_PAD)


def _mla_prep(cq, ckv, k1, k2, qg, kvg, pos, freqs):
    ko1, ko2 = _rope_pair(k1, k2, pos, freqs)
    return _rms(cq, qg), _rms(ckv, kvg), ko1, ko2


def _mla_prep_fn(proj, pos, qg, kvg, freqs):
    parts = [proj[:, a:b] for a, b in zip(_ODD_SPLITS[:-1], _ODD_SPLITS[1:])]
    return _mla_prep(*parts, qg, kvg, pos, freqs)


def _mla_prep_bwd_fn(proj, pos, dqn, dkvn, dko1, dko2, qg, kvg, freqs):
    parts = [proj[:, a:b] for a, b in zip(_ODD_SPLITS[:-1], _ODD_SPLITS[1:])]
    _, vjp = jax.vjp(lambda *a: _mla_prep(*a, pos, freqs), *parts, qg, kvg)
    dcq, dckv, dk1, dk2, dqg, dkvg = vjp((dqn, dkvn, dko1, dko2))
    return jnp.concatenate([dcq, dckv, dk1, dk2], axis=1), dqg, dkvg


def _rope_q_fn(r1, r2, pos, freqs):
    return _rope_pair(r1, r2, pos, freqs)


def _rope_q_bwd_fn(dqn, do1, do2, pos, freqs):
    dr1, dr2 = _rope_pair(do1, do2, pos, -freqs)
    return (jnp.concatenate([dqn, dr1, dr2], axis=1),)


W_NAMES = ("norm_mix_g", "norm_ffn_g", "final_norm_g", "even_w_in", "hgrn_lb_logits", "hgrn_norm_g", "s5_a_re", "s5_a_im",
           "s5_log_dt", "s5_b_re", "s5_b_im", "s5_c_re", "s5_c_im", "s5_d", "s5_w_glu", "s5_b_glu", "even_w_out", "odd_w_in",
           "mla_q_norm_g", "mla_w_uq", "mla_kv_norm_g", "mla_w_ukv", "odd_w_out", "ffn_w_in", "ffn_conv_w", "ffn_conv_b",
           "ffn_w_out")
BIG_UNITS = (("even_w_in", 0, "col"), ("s5_w_glu", 0, "row"), ("even_w_out", 0, "row"), ("odd_w_in", 0, "row"),
             ("mla_w_uq", 0, "col"), ("mla_w_ukv", 0, "col"), ("odd_w_out", 0, "row"),
             ("ffn_w_in", 0, "col"), ("ffn_w_in", 1, "col"), ("ffn_w_out", 0, "row"), ("ffn_w_out", 1, "row"))
BIG_NAMES = tuple(dict.fromkeys(u[0] for u in BIG_UNITS))
SMALL_SHARDED = (("mla_q_norm_g", 1), ("mla_kv_norm_g", 1), ("ffn_conv_w", 2))
SMALL_SHARDED_NAMES = tuple(s[0] for s in SMALL_SHARDED)
REPLICATED = tuple(n for n in W_NAMES if n not in BIG_NAMES + SMALL_SHARDED_NAMES)
REPLICATED_LATE = ("norm_mix_g",)
REPLICATED_EARLY = tuple(n for n in REPLICATED if n not in REPLICATED_LATE)


def _pack(flats, cols, row_mult):
    flat = jnp.concatenate(flats, axis=-1)
    pad = (-flat.shape[-1]) % (cols * row_mult)
    flat = jnp.pad(flat, [(0, 0)] * (flat.ndim - 1) + [(0, pad)])
    return flat.reshape(flat.shape[:-1] + (-1, cols))


def _unpack(flat, shapes):
    out, off = [], 0
    for shp in shapes:
        n = int(np.prod(shp))
        out.append(flat[..., off:off + n].reshape(flat.shape[:-1] + tuple(shp)))
        off += n
    return out


UNIT_KIND = {(n, l): kind for n, l, kind in BIG_UNITS}
STAGES = ((("even_w_in", 0),),
          (("s5_w_glu", 0), ("even_w_out", 0)),
          (("ffn_w_in", 0), ("ffn_w_out", 0)),
          (("odd_w_in", 0), ("mla_w_uq", 0), ("mla_w_ukv", 0), ("odd_w_out", 0)),
          (("ffn_w_in", 1), ("ffn_w_out", 1)))


def _gather_start(w, stage, with_small, after):
    srcs = [w[n][l].astype(BF16) for n, l in STAGES[stage]]
    if with_small:
        srcs.append(_pack([w[n].reshape(-1) for n in SMALL_SHARDED_NAMES], LANE, SUBLANE))
    return _exchange_start(srcs, scatter=False, name=f"gather_start_{stage}", after=after)


def _gather_finish(started, after, w, stage, with_small):
    srcs, lands = _exchange_wait(started, after, scatter=False, name=f"gather_wait_{stage}")
    lands = [_with_own(land, src) for land, src in zip(lands, srcs)]
    big = {}
    for unit, g in zip(STAGES[stage], lands):
        r, c = g.shape[2:]
        big[unit] = g.reshape(N_DEV * r, c) if UNIT_KIND[unit] == "row" else g.transpose(2, 0, 1, 3).reshape(r, N_DEV * c)
    if not with_small:
        return big
    parts = _unpack(lands[-1].reshape(N_DEV, -1), [w[n].shape for n in SMALL_SHARDED_NAMES])
    small = {}
    for (n, ax), p in zip(SMALL_SHARDED, parts):
        shp = list(w[n].shape)
        shp[ax] *= N_DEV
        small[n] = jnp.moveaxis(p, 0, ax).reshape(shp)
    return big, small


def _scatter_start(g_big, stage, extra=()):
    srcs = []
    for unit in STAGES[stage]:
        g = g_big[unit].astype(BF16)
        if UNIT_KIND[unit] == "row":
            srcs.append(g.reshape(N_CHIPS, N_CORES, g.shape[0] // N_DEV, g.shape[1]))
        else:
            srcs.append(g.reshape(g.shape[0], N_CHIPS, N_CORES, g.shape[1] // N_DEV).transpose(1, 2, 0, 3))
    return _exchange_start(srcs + list(extra), scatter=True, name=f"scatter_start_{stage}")


def _scatter_finish(started, after, stage):
    srcs, lands = _exchange_wait(started, after, scatter=True, name=f"scatter_wait_{stage}")
    me_chip, c = 2 * lax.axis_index("x") + lax.axis_index("y"), lax.axis_index("c")
    outs = []
    for land, src in zip(lands, srcs):
        own = lax.dynamic_slice(src, (me_chip, c) + (0,) * (src.ndim - 2), (1, 1) + src.shape[2:])[0, 0]
        outs.append(_with_own(land, own).reshape((N_DEV,) + land.shape[2:]))
    return outs


def _small_sharded_pack(g_small, w):
    flats = []
    for n, ax in SMALL_SHARDED:
        shp = list(w[n].shape)
        g = g_small[n].astype(F32).reshape(shp[:ax] + [N_DEV] + shp[ax:])
        flats.append(jnp.moveaxis(g, ax, 0).reshape(N_DEV, -1))
    small = _pack(flats, LANE, SUBLANE)
    return small.reshape((N_CHIPS, N_CORES) + small.shape[1:])


def _replicated_pack(g_repl, names):
    vec = _pack([g_repl[n].reshape(-1).astype(F32) for n in names], LANE, SUBLANE)
    return jnp.broadcast_to(vec, (N_CHIPS, N_CORES) + vec.shape)


def _block_diag(blocks):
    G, a, b = blocks.shape
    return jnp.einsum('gab,gk->gakb', blocks, jnp.eye(G, dtype=blocks.dtype)).reshape(G * a, G * b)


def _diag_blocks(mat, a, b):
    G = mat.shape[0] // a
    return jnp.einsum('gagb->gab', mat.reshape(G, a, G, b))


def _ffn_fwd(h, g, w_in, conv_w, conv_b, w_out, tag):
    hn, = _rows(_rms_fwd_fn, [h], [g], [(D_MODEL, BF16)], [], name=f"ffn{tag}_norm")
    au = _mm(hn, w_in, out_dtype=BF16, name=f"ffn{tag}_in")
    z = _ffn_mid_fwd(au, conv_w, conv_b, name=f"ffn{tag}_mid")
    return _mm(z, w_out, res=h, name=f"ffn{tag}_out"), (hn, au, z)


def _ffn_bwd(h, dh, saved, g, w_in, conv_w, conv_b, w_out, tag, deps=()):
    hn, au, z = saved
    dz = _mm(dh, w_out, tb=True, out_dtype=BF16, deps=deps, name=f"ffn{tag}_dz")
    dw_out = _mm(z, dh, ta=True, out_dtype=BF16, name=f"ffn{tag}_dwout")
    da, du, dcw, dcb = _ffn_mid_bwd(au, dz, conv_w, conv_b, name=f"ffn{tag}_dmid")
    dhn = _mm(da, w_in, tb=True, b_cols=(0, D_FF), name=f"ffn{tag}_dhn_a")
    dhn = _mm(du, w_in, tb=True, b_cols=(D_FF, D_FF), res=dhn, name=f"ffn{tag}_dhn_u")
    dw_in = jnp.concatenate([_mm(hn, da, ta=True, out_dtype=BF16, name=f"ffn{tag}_dwin_a"),
                             _mm(hn, du, ta=True, out_dtype=BF16, name=f"ffn{tag}_dwin_u")], axis=1)
    dh_in, dg = _rows(_rms_bwd_fn, [h, dhn, dh], [g], [(D_MODEL, F32)], [(1, D_MODEL)], name=f"ffn{tag}_dnorm")
    return dh_in, dict(g=dg, w_in=dw_in, conv_w=dcw, conv_b=dcb, w_out=dw_out)


def kernel(x, positions, norm_mix_g, norm_ffn_g, final_norm_g, even_w_in, hgrn_lb_logits, hgrn_norm_g, s5_a_re, s5_a_im, s5_log_dt, s5_b_re, s5_b_im, s5_c_re, s5_c_im, s5_d, s5_w_glu, s5_b_glu, even_w_out, odd_w_in, mla_q_norm_g, mla_w_uq, mla_kv_norm_g, mla_w_ukv, odd_w_out, ffn_w_in, ffn_conv_w, ffn_conv_b, ffn_w_out, loss_target, m_norm_mix_g, m_norm_ffn_g, m_final_norm_g, m_even_w_in, m_hgrn_lb_logits, m_hgrn_norm_g, m_s5_a_re, m_s5_a_im, m_s5_log_dt, m_s5_b_re, m_s5_b_im, m_s5_c_re, m_s5_c_im, m_s5_d, m_s5_w_glu, m_s5_b_glu, m_even_w_out, m_odd_w_in, m_mla_q_norm_g, m_mla_w_uq, m_mla_kv_norm_g, m_mla_w_ukv, m_odd_w_out, m_ffn_w_in, m_ffn_conv_w, m_ffn_conv_b, m_ffn_w_out, v_norm_mix_g, v_norm_ffn_g, v_final_norm_g, v_even_w_in, v_hgrn_lb_logits, v_hgrn_norm_g, v_s5_a_re, v_s5_a_im, v_s5_log_dt, v_s5_b_re, v_s5_b_im, v_s5_c_re, v_s5_c_im, v_s5_d, v_s5_w_glu, v_s5_b_glu, v_even_w_out, v_odd_w_in, v_mla_q_norm_g, v_mla_w_uq, v_mla_kv_norm_g, v_mla_w_ukv, v_odd_w_out, v_ffn_w_in, v_ffn_conv_w, v_ffn_conv_b, v_ffn_w_out):
    given = dict(locals())
    w = {n: given[n] for n in W_NAMES}
    mom = {n: given["m_" + n] for n in W_NAMES}
    var = {n: given["v_" + n] for n in W_NAMES}
    T = x.shape[1]
    h0 = x[0]
    tgt = loss_target[0]
    pos = positions.reshape(T, 1)

    gathers = []
    for s in range(len(STAGES)):
        gathers.append(_gather_start(w, s, with_small=(s == 1), after=[g[4] for g in gathers[-1:]]))
    half = MLA_ROPE // 2
    kr0 = MLA_Q_RANK + MLA_KV_RANK
    freqs = ROPE_THETA ** (-jnp.arange(0, MLA_ROPE, 2, dtype=F32) / MLA_ROPE)
    freqs_q = jnp.tile(freqs, MLA_HEADS)[None, :]
    freqs_k = jnp.concatenate([freqs, jnp.zeros((LANE - half,), F32)])[None, :]

    sp_in = (s5_a_re[0], s5_a_im[0], s5_log_dt[0][:, None], s5_b_re[0].transpose(0, 2, 1), s5_b_im[0].transpose(0, 2, 1),
             hgrn_lb_logits)
    abr, abi, bbt_re, bbt_im, lb0 = _s5_params_fwd(*sp_in)
    a_re, a_im = abr.reshape(1, S5_WIDTH), abi.reshape(1, S5_WIDTH)
    bb_re, bb_im = _block_diag(bbt_re).astype(BF16), _block_diag(bbt_im).astype(BF16)
    bb_cat = jnp.concatenate([bb_re, bb_im], axis=1)
    c_re = _block_diag(s5_c_re[0].transpose(0, 2, 1)).astype(BF16)
    c_im_neg = _block_diag(-s5_c_im[0].transpose(0, 2, 1)).astype(BF16)
    u_cols = (4 * HGRN_DIM, S5_DIM)

    hn0, = _rows(_rms_fwd_fn, [h0], [norm_mix_g[0:1]], [(D_MODEL, BF16)], [], name="mix0_norm", deps=[gathers[-1][4]])
    full = _gather_finish(gathers[0], hn0, w, 0, False)
    w_ein = full["even_w_in", 0]
    proj = _mm(hn0, w_ein, name="even_in")
    y_a, states = _hgrn_fwd(proj, lb0, hgrn_norm_g, name="hgrn_fwd")
    bu = _mm(proj, bb_cat, a_cols=u_cols, name="s5_bu")
    more, full_small = _gather_finish(gathers[1], bu, w, 1, True)
    w_glu, w_eout = more["s5_w_glu", 0], more["even_w_out", 0]
    qg, kvg, conv_w = full_small["mla_q_norm_g"], full_small["mla_kv_norm_g"], full_small["ffn_conv_w"]
    s_re, s_im = _s5_scan_fwd(bu, a_re, a_im, name="s5_scan_fwd")
    ys = _mm(s_im, c_im_neg, res=_mm(s_re, c_re, name="s5_y_re"), name="s5_y_im")
    z5, = _rows(_s5_act_fn, [ys, (proj,) + u_cols], [s5_d], [(S5_DIM, F32)], [], name="s5_act")
    gl = _mm(z5, w_glu, name="s5_glu")
    mixin, = _rows(_s5_mix_fn, [y_a, z5, gl], [s5_b_glu], [(D_MODEL, BF16)], [], name="s5_mix")
    h1 = _mm(mixin, w_eout, res=h0, name="even_out")
    full.update(_gather_finish(gathers[2], h1, w, 2, False))
    w_fin, w_fout = [full["ffn_w_in", 0]], [full["ffn_w_out", 0]]
    h2, ffn0_saved = _ffn_fwd(h1, norm_ffn_g[0:1], w_fin[0], conv_w[0], ffn_conv_b[0:1], w_fout[0], 0)

    full.update(_gather_finish(gathers[3], h2, w, 3, False))
    w_oin, w_ukv, w_oout = full["odd_w_in", 0], full["mla_w_ukv", 0], full["odd_w_out", 0]
    zpad = jnp.zeros((D_MODEL, LANE - half), BF16)
    w_oin_pad = jnp.concatenate([w_oin[:, :kr0], w_oin[:, kr0:kr0 + half], zpad, w_oin[:, kr0 + half:], zpad], axis=1)
    w_uq3 = full["mla_w_uq", 0].reshape(MLA_Q_RANK, MLA_HEADS, MLA_QK)
    w_uq_perm = jnp.concatenate([w_uq3[:, :, :MLA_NOPE].reshape(MLA_Q_RANK, -1),
                                 w_uq3[:, :, MLA_NOPE:MLA_NOPE + half].reshape(MLA_Q_RANK, -1),
                                 w_uq3[:, :, MLA_NOPE + half:].reshape(MLA_Q_RANK, -1)], axis=1)
    hn1, = _rows(_rms_fwd_fn, [h2], [norm_mix_g[1:2]], [(D_MODEL, BF16)], [], name="mix1_norm")
    proj_o = _mm(hn1, w_oin_pad, name="odd_in")
    qn, kvn, ko1, ko2 = _rows(_mla_prep_fn, [proj_o, pos], [qg, kvg, freqs_k],
                              [(MLA_Q_RANK, BF16), (MLA_KV_RANK, BF16), (LANE, F32), (LANE, F32)], [], name="mla_prep")
    q_all = _mm(qn, w_uq_perm, name="mla_uq")
    kv = _mm(kvn, w_ukv, out_dtype=BF16, name="mla_ukv")
    nope_w = MLA_HEADS * MLA_NOPE
    rope_w = MLA_HEADS * half
    o1, o2 = _rows(_rope_q_fn, [(q_all, nope_w, rope_w), (q_all, nope_w + rope_w, rope_w), pos], [freqs_q],
                   [(rope_w, F32), (rope_w, F32)], [], name="mla_rope_q")
    q_rope = jnp.concatenate([o1.reshape(T, MLA_HEADS, half), o2.reshape(T, MLA_HEADS, half)], axis=2).transpose(1, 0, 2)
    k_rope = jnp.concatenate([ko1[:, :half], ko2[:, :half]], axis=1)
    o, lse = _attn_fwd(q_all, q_rope, kv, k_rope, name="attn_fwd")
    h3 = _mm(o, w_oout, res=h2, name="odd_out")
    full.update(_gather_finish(gathers[4], h3, w, 4, False))
    w_fin.append(full["ffn_w_in", 1])
    w_fout.append(full["ffn_w_out", 1])
    h4, ffn1_saved = _ffn_fwd(h3, norm_ffn_g[1:2], w_fin[1], conv_w[1], ffn_conv_b[1:2], w_fout[1], 1)

    dh4, d_final_g, loss_cols = _rows(_loss_fn, [h4, tgt], [final_norm_g[None, :]], [(D_MODEL, F32)],
                                      [(1, D_MODEL), (1, D_MODEL)], name="loss_head")
    loss = lax.psum(jnp.sum(loss_cols), ("x", "y", "c"))

    dh3, gf1 = _ffn_bwd(h3, dh4, ffn1_saved, norm_ffn_g[1:2], w_fin[1], conv_w[1], ffn_conv_b[1:2], w_fout[1], 1)
    scatters = {4: _scatter_start({("ffn_w_in", 1): gf1["w_in"], ("ffn_w_out", 1): gf1["w_out"]}, 4)}
    do = _mm(dh3, w_oout, tb=True, deps=[scatters[4][4]], name="odd_out_dx")
    d_w_oout = _mm(o, dh3, ta=True, out_dtype=BF16, name="odd_out_dw")
    dq_nope, dq_rope, dkv, dk_rope = _attn_bwd(q_all, q_rope, kv, k_rope, o, lse, do, name="attn_bwd")
    dq_rope_t = dq_rope.transpose(1, 0, 2)
    do1, do2 = dq_rope_t[:, :, :half].reshape(T, rope_w), dq_rope_t[:, :, half:].reshape(T, rope_w)
    lane_pad = ((0, 0), (0, LANE - half))
    dko1, dko2 = jnp.pad(dk_rope[:, :half], lane_pad), jnp.pad(dk_rope[:, half:], lane_pad)
    dq_all, = _rows(_rope_q_bwd_fn, [dq_nope, do1, do2, pos], [freqs_q], [(MLA_HEADS * MLA_QK, BF16)], [], name="mla_rope_q_bwd")
    d_w_uq_perm = _mm(qn, dq_all, ta=True, out_dtype=BF16, name="mla_uq_dw")
    dqn = _mm(dq_all, w_uq_perm, tb=True, name="mla_uq_dx")
    d_w_ukv = _mm(kvn, dkv, ta=True, out_dtype=BF16, name="mla_ukv_dw")
    dkvn = _mm(dkv, w_ukv, tb=True, name="mla_ukv_dx")
    dproj_o, d_qg, d_kvg = _rows(_mla_prep_bwd_fn, [proj_o, pos, dqn, dkvn, dko1, dko2], [qg, kvg, freqs_k],
                                 [(ODD_IN_PAD, BF16)], [(1, MLA_Q_RANK), (1, MLA_KV_RANK)], name="mla_prep_bwd")
    d_w_oin_pad = _mm(hn1, dproj_o, ta=True, out_dtype=BF16, name="odd_in_dw")
    dhn1 = _mm(dproj_o, w_oin_pad, tb=True, name="odd_in_dx")
    dh2, d_mix_g1 = _rows(_rms_bwd_fn, [h2, dhn1, dh3], [norm_mix_g[1:2]], [(D_MODEL, F32)], [(1, D_MODEL)], name="mix1_dnorm")
    d_w_oin = jnp.concatenate([d_w_oin_pad[:, :kr0 + half], d_w_oin_pad[:, kr0 + LANE:kr0 + LANE + half]], axis=1)
    d3 = d_w_uq_perm
    d_w_uq = jnp.concatenate([d3[:, :nope_w].reshape(MLA_Q_RANK, MLA_HEADS, MLA_NOPE),
                              d3[:, nope_w:nope_w + rope_w].reshape(MLA_Q_RANK, MLA_HEADS, half),
                              d3[:, nope_w + rope_w:].reshape(MLA_Q_RANK, MLA_HEADS, half)], axis=2).reshape(MLA_Q_RANK, -1)
    scatters[3] = _scatter_start({("odd_w_in", 0): d_w_oin, ("mla_w_uq", 0): d_w_uq, ("mla_w_ukv", 0): d_w_ukv,
                                  ("odd_w_out", 0): d_w_oout}, 3)

    dh1, gf0 = _ffn_bwd(h1, dh2, ffn0_saved, norm_ffn_g[0:1], w_fin[0], conv_w[0], ffn_conv_b[0:1], w_fout[0], 0,
                        deps=[scatters[3][4]])
    scatters[2] = _scatter_start({("ffn_w_in", 0): gf0["w_in"], ("ffn_w_out", 0): gf0["w_out"]}, 2)
    dmix = _mm(dh1, w_eout, tb=True, deps=[scatters[2][4]], name="even_out_dx")
    d_w_eout = _mm(mixin, dh1, ta=True, out_dtype=BF16, name="even_out_dw")
    dq, df, di, dg, d_lb0, d_hgrn_g = _hgrn_bwd(proj, lb0, hgrn_norm_g, states, dmix, name="hgrn_bwd")
    dz1, dgl, d_b_glu = _rows(_s5_gate_bwd_fn, [z5, gl, (dmix, HGRN_DIM, S5_DIM)], [s5_b_glu],
                              [(S5_DIM, F32), (S5_DIM, BF16)], [(1, S5_DIM)], name="s5_gate_bwd")
    dz2 = _mm(dgl, w_glu, tb=True, name="s5_glu_dx")
    d_w_glu = _mm(z5, dgl, ta=True, out_dtype=BF16, name="s5_glu_dw")
    dys, du1, d_s5_d = _rows(_s5_act_bwd_fn, [ys, (proj,) + u_cols, dz1, dz2], [s5_d],
                             [(S5_DIM, BF16), (S5_DIM, F32)], [(1, S5_DIM)], name="s5_act_bwd")
    ds_re = _mm(dys, c_re, tb=True, name="s5_ds_re")
    ds_im = _mm(dys, c_im_neg, tb=True, name="s5_ds_im")
    d_c_re = _mm(s_re, dys, ta=True, name="s5_dc_re")
    d_c_im_neg = _mm(s_im, dys, ta=True, name="s5_dc_im")
    lam_re, lam_im, d_ar, d_ai = _s5_scan_bwd(ds_re, ds_im, s_re, s_im, a_re, a_im, name="s5_scan_bwd")
    du2 = _mm(lam_im, bb_im, tb=True, res=_mm(lam_re, bb_re, tb=True, name="s5_du_re"), name="s5_du_im")
    d_bb_re = _mm(proj, lam_re, ta=True, a_cols=u_cols, name="s5_dbb_re")
    d_bb_im = _mm(proj, lam_im, ta=True, a_cols=u_cols, name="s5_dbb_im")
    sp_g = _s5_params_bwd(*sp_in, d_ar.sum(0).reshape(S5_GROUPS, S5_STATE), d_ai.sum(0).reshape(S5_GROUPS, S5_STATE),
                          _diag_blocks(d_bb_re, S5_GROUP, S5_STATE), _diag_blocks(d_bb_im, S5_GROUP, S5_STATE), d_lb0)
    d_a_re, d_a_im, d_log_dt, d_bt_re, d_bt_im, d_lb_logits = sp_g
    g_small = dict(mla_q_norm_g=d_qg, mla_kv_norm_g=d_kvg, ffn_conv_w=jnp.stack([gf0["conv_w"], gf1["conv_w"]]))
    g_repl = dict(
        norm_ffn_g=jnp.concatenate([gf0["g"], gf1["g"]]),
        final_norm_g=d_final_g[0], hgrn_lb_logits=d_lb_logits, hgrn_norm_g=d_hgrn_g,
        s5_a_re=d_a_re[None], s5_a_im=d_a_im[None], s5_log_dt=d_log_dt[:, 0][None],
        s5_b_re=d_bt_re.transpose(0, 2, 1)[None], s5_b_im=d_bt_im.transpose(0, 2, 1)[None],
        s5_c_re=_diag_blocks(d_c_re, S5_STATE, S5_GROUP).transpose(0, 2, 1)[None],
        s5_c_im=-_diag_blocks(d_c_im_neg, S5_STATE, S5_GROUP).transpose(0, 2, 1)[None],
        s5_d=d_s5_d, s5_b_glu=d_b_glu, ffn_conv_b=jnp.concatenate([gf0["conv_b"], gf1["conv_b"]]))
    scatters[1] = _scatter_start({("s5_w_glu", 0): d_w_glu, ("even_w_out", 0): d_w_eout}, 1,
                                 extra=[_small_sharded_pack(g_small, w), _replicated_pack(g_repl, REPLICATED_EARLY)])
    dproj, = _rows(_dproj_fn, [dq, df, di, dg, du1, du2], [], [(EVEN_IN, BF16)], [], name="even_dproj", deps=[scatters[1][4]])
    d_w_ein = _mm(hn0, dproj, ta=True, out_dtype=BF16, name="even_in_dw")
    dhn0 = _mm(dproj, w_ein, tb=True, name="even_in_dx")
    grad_x, d_mix_g0 = _rows(_rms_bwd_fn, [h0, dhn0, dh1], [norm_mix_g[0:1]], [(D_MODEL, F32)], [(1, D_MODEL)], name="mix0_dnorm")
    g_repl["norm_mix_g"] = jnp.concatenate([d_mix_g0, d_mix_g1])
    scatters[0] = _scatter_start({("even_w_in", 0): d_w_ein}, 0, extra=[_replicated_pack(g_repl, REPLICATED_LATE)])

    delta, new_m, new_v = {}, {}, {}
    per_unit, partial = {}, {}
    after = scatters[0][4]
    for stage in (4, 3, 2, 1, 0):
        partial[stage] = _scatter_finish(scatters[stage], after, stage)
        for (n, l), slabs in zip(STAGES[stage], partial[stage]):
            per_unit[n, l] = _adamw(w[n][l], slabs, mom[n][l], var[n][l], name=f"adamw_{n}_{l}")
        after = per_unit[STAGES[stage][-1]][0]
    grads = {}
    for names, slabs, tag in ((REPLICATED_EARLY, partial[1][-1], "repl"), (REPLICATED_LATE, partial[0][-1], "late"),
                              (SMALL_SHARDED_NAMES, partial[1][-2], "small")):
        total = _sum_slabs(slabs, F32, name=f"sum_g_{tag}").reshape(-1)
        grads.update(zip(names, _unpack(total, [w[n].shape for n in names])))
    for n in BIG_NAMES:
        layers = [per_unit[n, l] for l in range(w[n].shape[0])]
        grads[n], delta[n], new_m[n], new_v[n] = (jnp.stack([lay[k] for lay in layers]) for k in range(4))
    small = [n for n in W_NAMES if n not in BIG_NAMES]
    packed = [_pack([t[n].reshape(-1) for n in small], LANE, SUBLANE) for t in (w, grads, mom, var)]
    outs = _adamw(*packed, name="adamw_small")
    small_shapes = [w[n].shape for n in small]
    for dst, o_ in zip((delta, new_m, new_v), outs):
        dst.update(zip(small, _unpack(o_.reshape(-1), small_shapes)))

    return (loss, grad_x[None], *[grads[n] for n in W_NAMES], *[delta[n] for n in W_NAMES],
            *[new_m[n] for n in W_NAMES], *[new_v[n] for n in W_NAMES])
```

```python
import functools
import math

import numpy as np
import jax
import jax.numpy as jnp
from jax import lax
from jax.experimental import pallas as pl
from jax.experimental.pallas import tpu as pltpu

F32 = jnp.float32
BF16 = jnp.bfloat16
_MXU_DTYPE = jnp.bfloat16

D_MODEL = 1024
HGRN_DIM = 512
HGRN_HEAD_DIM = 128
HGRN_HEADS = 4
HGRN_CHUNK = 64
S5_DIM = 512
S5_GROUPS = 32
S5_GROUP = 16
S5_STATE = 64
S5_WIDTH = S5_GROUPS * S5_STATE
EVEN_IN = 4 * HGRN_DIM + S5_DIM
MLA_HEADS = 8
MLA_Q_RANK = 384
MLA_KV_RANK = 256
MLA_NOPE = 128
MLA_ROPE = 64
MLA_V = 128
MLA_QK = MLA_NOPE + MLA_ROPE
ODD_IN = MLA_Q_RANK + MLA_KV_RANK + MLA_ROPE
ODD_IN_PAD = MLA_Q_RANK + MLA_KV_RANK + 2 * 128
ROPE_THETA = 10000.0
D_FF = 2816
EPS = 1e-6
ADAM_LR = 0.001
ADAM_B1 = 0.9
ADAM_B2 = 0.999
ADAM_EPS = 1e-08
ADAM_WD = 0.01
ADAM_STEP = 10

N_DEV = 8
LANE = 128
SUBLANE = 8
VMEM_LIMIT_BYTES = 56 * 1024 * 1024
MESH = pl.DeviceIdType.MESH


def _cparams(sem=None):
    return pltpu.CompilerParams(dimension_semantics=sem, vmem_limit_bytes=VMEM_LIMIT_BYTES)


def _pick(n, cands):
    for c in cands:
        if n % c == 0:
            return c
    raise ValueError(f"no tile for {n} in {cands}")


def _sigmoid(x):
    return 1.0 / (1.0 + jnp.exp(-x))


def _silu(x):
    return x * _sigmoid(x)


def _gelu(x):
    return 0.5 * x * (1.0 + jnp.tanh(math.sqrt(2.0 / math.pi) * (x + 0.044715 * (x * x * x))))


def _rms(x, g):
    return x * lax.rsqrt(jnp.mean(x * x, axis=-1, keepdims=True) + EPS) * g


def _mxu(a, b, ca, cb):
    return lax.dot_general(a.astype(_MXU_DTYPE), b.astype(_MXU_DTYPE), (((ca,), (cb,)), ((), ())),
                           preferred_element_type=F32)


@functools.partial(jax.custom_vjp, nondiff_argnums=(2, 3))
def _mxu_ad(a, b, ca, cb):
    return _mxu(a, b, ca, cb)


def _mxu_ad_fwd(a, b, ca, cb):
    return _mxu(a, b, ca, cb), (a, b)


def _mxu_ad_bwd(ca, cb, saved, g):
    a, b = saved
    fa, fb = 1 - ca, 1 - cb
    da = _mxu(g, b, 1, fb) if ca == 1 else _mxu(b, g, fb, 1)
    db = _mxu(a, g, fa, 0) if cb == 0 else _mxu(g, a, 0, fa)
    return da, db


_mxu_ad.defvjp(_mxu_ad_fwd, _mxu_ad_bwd)


def _tri(n):
    row = lax.broadcasted_iota(jnp.int32, (n, n), 0)
    col = lax.broadcasted_iota(jnp.int32, (n, n), 1)
    return col <= row


def _cumsum_rows(x, reverse=False):
    n = x.shape[0]
    rowi = lax.broadcasted_iota(jnp.int32, x.shape, 0)
    s = 1
    while s < n:
        if reverse:
            x = x + jnp.where(rowi < n - s, pltpu.roll(x, n - s, 0), 0.0)
        else:
            x = x + jnp.where(rowi >= s, pltpu.roll(x, s, 0), 0.0)
        s *= 2
    return x


@jax.custom_vjp
def _cumsum_rows_ad(x):
    return _cumsum_rows(x)


def _cumsum_rows_ad_fwd(x):
    return _cumsum_rows(x), None


def _cumsum_rows_ad_bwd(_, g):
    return (_cumsum_rows(g, reverse=True),)


_cumsum_rows_ad.defvjp(_cumsum_rows_ad_fwd, _cumsum_rows_ad_bwd)


MM_VMEM_BUDGET = 36 * 1024 * 1024
MM_MAX_TILE = 1408


def _lane_divisors(n, cap, offs=()):
    return [d for d in range(min(n, cap) // LANE * LANE, 0, -LANE) if n % d == 0 and all(o % d == 0 for o in offs)]


def _mm_tiles(M, N, K, sa, sb, so, has_res, m_offs, n_offs, k_offs):
    best = None
    for tm in _lane_divisors(M, MM_MAX_TILE, m_offs):
        for tn in _lane_divisors(N, MM_MAX_TILE, n_offs):
            for tk in _lane_divisors(K, K, k_offs):
                nk = K // tk
                vmem = 2 * (tm * tk * sa + tk * tn * sb + tm * tn * so + tm * tn * 4 * has_res) + (tm * tn * 4 if nk > 1 else 0)
                if vmem <= MM_VMEM_BUDGET:
                    key = (-nk, tm * tn, tn)
                    if best is None or key > best[0]:
                        best = (key, tm, tn, tk)
                    break
    return best[1:]


def _mm(a, b, *, ta=False, tb=False, res=None, out_dtype=F32, a_cols=None, b_cols=None, deps=(), name):
    a_minor = a.shape[1] if a_cols is None else a_cols[1]
    b_minor = b.shape[1] if b_cols is None else b_cols[1]
    K, M = (a.shape[0], a_minor) if ta else (a_minor, a.shape[0])
    N = b.shape[0] if tb else b_minor
    assert (b_minor if tb else b.shape[0]) == K, (a.shape, b.shape, ta, tb)
    a_off = 0 if a_cols is None else a_cols[0]
    b_off = 0 if b_cols is None else b_cols[0]
    has_res = res is not None
    tm, tn, tk = _mm_tiles(M, N, K, a.dtype.itemsize, b.dtype.itemsize, jnp.dtype(out_dtype).itemsize, has_res,
                           (a_off,) if ta else (), () if tb else (b_off,), ((a_off,) if not ta else ()) + ((b_off,) if tb else ()))
    nk = K // tk
    am, ak = (a_off // tm, 0) if ta else (0, a_off // tk)
    bn, bk = (0, b_off // tk) if tb else (b_off // tn, 0)
    a_spec = pl.BlockSpec((tk, tm), lambda i, j, k: (k, i + am)) if ta else pl.BlockSpec((tm, tk), lambda i, j, k: (i, k + ak))
    b_spec = pl.BlockSpec((tn, tk), lambda i, j, k: (j, k + bk)) if tb else pl.BlockSpec((tk, tn), lambda i, j, k: (k, j + bn))
    o_spec = pl.BlockSpec((tm, tn), lambda i, j, k: (i, j))
    ca, cb = (0 if ta else 1), (1 if tb else 0)

    n_fixed = 2 + has_res + len(deps)

    def body(*refs):
        a_ref, b_ref = refs[0], refs[1]
        res_ref = refs[2] if has_res else None
        o_ref = refs[n_fixed]
        part = _mxu(a_ref[...], b_ref[...], ca, cb)
        if nk == 1:
            o_ref[...] = (part + res_ref[...] if has_res else part).astype(out_dtype)
            return
        acc_ref = refs[n_fixed + 1]
        k = pl.program_id(2)

        @pl.when(k == 0)
        def _():
            acc_ref[...] = part

        @pl.when(k > 0)
        def _():
            acc_ref[...] += part

        @pl.when(k == nk - 1)
        def _():
            o_ref[...] = (acc_ref[...] + res_ref[...] if has_res else acc_ref[...]).astype(out_dtype)

    ins = [a, b] + ([res] if has_res else []) + list(deps)
    in_specs = [a_spec, b_spec] + ([o_spec] if has_res else []) + [pl.BlockSpec(memory_space=pl.ANY)] * len(deps)
    return pl.pallas_call(
        body, name=name, grid=(M // tm, N // tn, nk),
        in_specs=in_specs, out_specs=o_spec,
        out_shape=jax.ShapeDtypeStruct((M, N), out_dtype),
        scratch_shapes=[pltpu.VMEM((tm, tn), F32)] if nk > 1 else [],
        compiler_params=_cparams(("parallel", "parallel", "arbitrary")),
    )(*ins)


def _rows(fn, row_ins, const_ins, row_outs, acc_outs, *, name, tm=256, deps=()):
    norm = [(r, 0, r.shape[1]) if not isinstance(r, tuple) else r for r in row_ins]
    T = norm[0][0].shape[0]
    tm = min(tm, T)
    nr, nc, no, na = len(norm), len(const_ins), len(row_outs), len(acc_outs)
    first_out = nr + nc + len(deps)

    def body(*refs):
        i = pl.program_id(0)
        vals = [r[...] for r in refs[:nr + nc]]
        outs = fn(*vals)
        for o_ref, o in zip(refs[first_out:first_out + no], outs[:no]):
            o_ref[...] = o.astype(o_ref.dtype)
        for a_ref, o in zip(refs[first_out + no:], outs[no:]):
            @pl.when(i == 0)
            def _(a_ref=a_ref, o=o):
                a_ref[...] = o

            @pl.when(i > 0)
            def _(a_ref=a_ref, o=o):
                a_ref[...] += o

    in_specs = []
    for arr, off, w in norm:
        assert off % w == 0, (off, w)
        in_specs.append(pl.BlockSpec((tm, w), lambda i, b=off // w: (i, b)))
    for c in const_ins:
        in_specs.append(pl.BlockSpec(c.shape, lambda i: (0, 0)))
    in_specs += [pl.BlockSpec(memory_space=pl.ANY)] * len(deps)
    out_specs = [pl.BlockSpec((tm, w), lambda i: (i, 0)) for w, _ in row_outs]
    out_specs += [pl.BlockSpec(s, lambda i: (0, 0)) for s in acc_outs]
    out_shape = [jax.ShapeDtypeStruct((T, w), dt) for w, dt in row_outs]
    out_shape += [jax.ShapeDtypeStruct(s, F32) for s in acc_outs]
    return pl.pallas_call(
        body, name=name, grid=(T // tm,), in_specs=in_specs, out_specs=out_specs, out_shape=out_shape,
        compiler_params=_cparams(("arbitrary",)),
    )(*[n[0] for n in norm], *const_ins, *deps)


FFN_COL_TILE = LANE
FFN_ROW_CHUNK = 512


FFN_HALO = 2 * SUBLANE


def _shift_down(ext, s, rows):
    return pltpu.roll(ext, s, 0)[FFN_HALO:FFN_HALO + rows]


def _shift_up(ext, s, rows):
    return pltpu.roll(ext, rows + FFN_HALO - s, 0)[:rows]


def _ffn_chunks(T):
    r = min(FFN_ROW_CHUNK, T)
    return r, T // r


def _ext_before(ref, c, r):
    if c == 0:
        return jnp.concatenate([jnp.zeros((FFN_HALO, ref.shape[1]), F32), ref[0:r, :].astype(F32)], axis=0)
    return ref[c * r - FFN_HALO:(c + 1) * r, :].astype(F32)


def _ext_after(ref, c, r, nch):
    if c == nch - 1:
        return jnp.concatenate([ref[c * r:(c + 1) * r, :].astype(F32), jnp.zeros((FFN_HALO, ref.shape[1]), F32)], axis=0)
    return ref[c * r:(c + 1) * r + FFN_HALO, :].astype(F32)


def _ffn_mid_fwd(au, conv_w, conv_b, *, name):
    T = au.shape[0]
    tc = FFN_COL_TILE
    ncol = D_FF // tc
    r, nch = _ffn_chunks(T)

    def body(a_ref, u_ref, w_ref, b_ref, z_ref):
        w0, w1, w2, bias = w_ref[0:1, :], w_ref[1:2, :], w_ref[2:3, :], b_ref[...]
        for c in range(nch):
            ext = _ext_before(a_ref, c, r)
            pre = w0 * _shift_down(ext, 2, r) + w1 * _shift_down(ext, 1, r) + w2 * ext[FFN_HALO:] + bias
            z_ref[c * r:(c + 1) * r, :] = (_silu(pre) * u_ref[c * r:(c + 1) * r, :].astype(F32)).astype(z_ref.dtype)

    return pl.pallas_call(
        body, name=name, grid=(ncol,),
        in_specs=[pl.BlockSpec((T, tc), lambda j: (0, j)), pl.BlockSpec((T, tc), lambda j: (0, j + ncol)),
                  pl.BlockSpec((3, tc), lambda j: (0, j)), pl.BlockSpec((1, tc), lambda j: (0, j))],
        out_specs=pl.BlockSpec((T, tc), lambda j: (0, j)),
        out_shape=jax.ShapeDtypeStruct((T, D_FF), BF16),
        compiler_params=_cparams(("parallel",)),
    )(au, au, conv_w, conv_b)


def _ffn_mid_bwd(au, dz, conv_w, conv_b, *, name):
    T = au.shape[0]
    tc = FFN_COL_TILE
    ncol = D_FF // tc
    r, nch = _ffn_chunks(T)

    def body(a_ref, u_ref, dz_ref, w_ref, b_ref, da_ref, du_ref, dw_ref, db_ref, dpre_ref):
        w0, w1, w2, bias = w_ref[0:1, :], w_ref[1:2, :], w_ref[2:3, :], b_ref[...]
        dw0 = jnp.zeros((1, tc), F32)
        dw1 = jnp.zeros((1, tc), F32)
        dw2 = jnp.zeros((1, tc), F32)
        db = jnp.zeros((1, tc), F32)
        for c in range(nch):
            rows = slice(c * r, (c + 1) * r)
            ext = _ext_before(a_ref, c, r)
            a2, a1, a0 = _shift_down(ext, 2, r), _shift_down(ext, 1, r), ext[FFN_HALO:]
            pre = w0 * a2 + w1 * a1 + w2 * a0 + bias
            sg = _sigmoid(pre)
            act = pre * sg
            dzc = dz_ref[rows, :].astype(F32)
            du_ref[rows, :] = (dzc * act).astype(du_ref.dtype)
            dpre = dzc * u_ref[rows, :].astype(F32) * (sg * (1.0 + pre * (1.0 - sg)))
            dpre_ref[rows, :] = dpre
            dw0 += jnp.sum(dpre * a2, axis=0, keepdims=True)
            dw1 += jnp.sum(dpre * a1, axis=0, keepdims=True)
            dw2 += jnp.sum(dpre * a0, axis=0, keepdims=True)
            db += jnp.sum(dpre, axis=0, keepdims=True)
        for c in range(nch):
            ext = _ext_after(dpre_ref, c, r, nch)
            da = w0 * _shift_up(ext, 2, r) + w1 * _shift_up(ext, 1, r) + w2 * ext[:r]
            da_ref[c * r:(c + 1) * r, :] = da.astype(da_ref.dtype)
        dw_ref[0:1, :] = dw0
        dw_ref[1:2, :] = dw1
        dw_ref[2:3, :] = dw2
        db_ref[...] = db

    col = lambda j: (0, j)
    return pl.pallas_call(
        body, name=name, grid=(ncol,),
        in_specs=[pl.BlockSpec((T, tc), col), pl.BlockSpec((T, tc), lambda j: (0, j + ncol)), pl.BlockSpec((T, tc), col),
                  pl.BlockSpec((3, tc), col), pl.BlockSpec((1, tc), col)],
        out_specs=[pl.BlockSpec((T, tc), col), pl.BlockSpec((T, tc), col), pl.BlockSpec((3, tc), col), pl.BlockSpec((1, tc), col)],
        out_shape=[jax.ShapeDtypeStruct((T, D_FF), BF16), jax.ShapeDtypeStruct((T, D_FF), BF16),
                   jax.ShapeDtypeStruct((3, D_FF), F32), jax.ShapeDtypeStruct((1, D_FF), F32)],
        scratch_shapes=[pltpu.VMEM((T, tc), F32)],
        compiler_params=_cparams(("parallel",)),
    )(au, au, dz, conv_w, conv_b)


HGRN_BLOCK = 256


def _hgrn_chunk(dot, cumsum, q, f, i, g, lb, ng, st):
    C = q.shape[0]
    forget = lb + (1.0 - lb) * _sigmoid(f)
    k = 1.0 - forget
    b = cumsum(jnp.log(forget))
    b_last = b[C - 1:C, :]
    qd = q * jnp.exp(b)
    kd = k * jnp.exp(-b)
    att = jnp.where(_tri(C), dot(qd, kd, 1, 1), 0.0)
    o = dot(att, i, 1, 0) + dot(qd, st, 1, 1)
    st_new = st * jnp.exp(b_last) + dot(i, k * jnp.exp(b_last - b), 0, 0)
    on = o * lax.rsqrt(jnp.mean(o * o, axis=-1, keepdims=True) + EPS) * ng
    return on * _silu(g), st_new


def _hgrn_specs(T, rev):
    tb = min(HGRN_BLOCK, T)
    nb = T // tb
    blk = (lambda n: nb - 1 - n) if rev else (lambda n: n)
    hd = HGRN_HEAD_DIM
    proj_specs = [pl.BlockSpec((tb, HGRN_DIM), lambda n, k=k: (blk(n), k)) for k in range(4)]
    vec_spec = pl.BlockSpec((1, HGRN_DIM), lambda n: (0, 0))
    tok_spec = pl.BlockSpec((tb, HGRN_DIM), lambda n: (blk(n), 0))
    st_spec = pl.BlockSpec((HGRN_HEADS, tb // HGRN_CHUNK, hd, hd), lambda n: (0, blk(n), 0, 0))
    return tb, nb, proj_specs, vec_spec, tok_spec, st_spec


def _head_cols(h):
    return slice(h * HGRN_HEAD_DIM, (h + 1) * HGRN_HEAD_DIM)


def _hgrn_fwd(proj, lb, ng, *, name):
    T = proj.shape[0]
    tb, nb, proj_specs, vec_spec, tok_spec, st_spec = _hgrn_specs(T, False)
    nsub = tb // HGRN_CHUNK
    hd = HGRN_HEAD_DIM

    def body(q_ref, f_ref, i_ref, g_ref, lb_ref, ng_ref, y_ref, sts_ref, st_ref):
        @pl.when(pl.program_id(0) == 0)
        def _():
            st_ref[...] = jnp.zeros_like(st_ref)

        st = [st_ref[h] for h in range(HGRN_HEADS)]
        for s in range(nsub):
            rows = slice(s * HGRN_CHUNK, (s + 1) * HGRN_CHUNK)
            for h in range(HGRN_HEADS):
                cols = _head_cols(h)
                sts_ref[h, s] = st[h]
                y, st[h] = _hgrn_chunk(_mxu, _cumsum_rows, q_ref[rows, cols], f_ref[rows, cols], i_ref[rows, cols],
                                       g_ref[rows, cols], lb_ref[:, cols], ng_ref[:, cols], st[h])
                y_ref[rows, cols] = y
        for h in range(HGRN_HEADS):
            st_ref[h] = st[h]

    return pl.pallas_call(
        body, name=name, grid=(nb,),
        in_specs=proj_specs + [vec_spec, vec_spec], out_specs=[tok_spec, st_spec],
        out_shape=[jax.ShapeDtypeStruct((T, HGRN_DIM), F32),
                   jax.ShapeDtypeStruct((HGRN_HEADS, T // HGRN_CHUNK, hd, hd), F32)],
        scratch_shapes=[pltpu.VMEM((HGRN_HEADS, hd, hd), F32)],
        compiler_params=_cparams(("arbitrary",)),
    )(proj, proj, proj, proj, lb, ng)


def _hgrn_bwd(proj, lb, ng, states, dmix, *, name):
    T = proj.shape[0]
    tb, nb, proj_specs, vec_spec, tok_spec, st_spec = _hgrn_specs(T, True)
    nsub = tb // HGRN_CHUNK
    hd = HGRN_HEAD_DIM
    chunk = functools.partial(_hgrn_chunk, _mxu_ad, _cumsum_rows_ad)

    def body(q_ref, f_ref, i_ref, g_ref, lb_ref, ng_ref, sts_ref, dy_ref,
             dq_ref, df_ref, di_ref, dg_ref, dlb_ref, dng_ref, dst_ref):
        @pl.when(pl.program_id(0) == 0)
        def _():
            dst_ref[...] = jnp.zeros_like(dst_ref)
            dlb_ref[...] = jnp.zeros_like(dlb_ref)
            dng_ref[...] = jnp.zeros_like(dng_ref)

        dst = [dst_ref[h] for h in range(HGRN_HEADS)]
        dlb = [jnp.zeros((1, hd), F32)] * HGRN_HEADS
        dng = [jnp.zeros((1, hd), F32)] * HGRN_HEADS
        for s in reversed(range(nsub)):
            rows = slice(s * HGRN_CHUNK, (s + 1) * HGRN_CHUNK)
            for h in range(HGRN_HEADS):
                cols = _head_cols(h)
                _, vjp = jax.vjp(chunk, q_ref[rows, cols], f_ref[rows, cols], i_ref[rows, cols], g_ref[rows, cols],
                                 lb_ref[:, cols], ng_ref[:, cols], sts_ref[h, s])
                dq, df, di, dg, dlb_s, dng_s, dst[h] = vjp((dy_ref[rows, cols], dst[h]))
                dq_ref[rows, cols] = dq
                df_ref[rows, cols] = df
                di_ref[rows, cols] = di
                dg_ref[rows, cols] = dg
                dlb[h] = dlb[h] + dlb_s
                dng[h] = dng[h] + dng_s
        for h in range(HGRN_HEADS):
            dst_ref[h] = dst[h]
            dlb_ref[:, _head_cols(h)] += dlb[h]
            dng_ref[:, _head_cols(h)] += dng[h]

    tok_out = jax.ShapeDtypeStruct((T, HGRN_DIM), F32)
    vec_out = jax.ShapeDtypeStruct((1, HGRN_DIM), F32)
    return pl.pallas_call(
        body, name=name, grid=(nb,),
        in_specs=proj_specs + [vec_spec, vec_spec, st_spec, tok_spec],
        out_specs=[tok_spec] * 4 + [vec_spec, vec_spec],
        out_shape=[tok_out] * 4 + [vec_out, vec_out],
        scratch_shapes=[pltpu.VMEM((HGRN_HEADS, hd, hd), F32)],
        compiler_params=_cparams(("arbitrary",)),
    )(proj, proj, proj, proj, lb, ng, states, dmix)


S5_LANES = 512
S5_ROWS = 512


def _cmul(ar, ai, br, bi):
    return ar * br - ai * bi, ar * bi + ai * br


def _power_table(ar, ai, exps):
    a2 = _cmul(ar, ai, ar, ai)
    a4 = _cmul(*a2, *a2)
    e = exps - 1
    pr = jnp.broadcast_to(ar, exps.shape)
    pi = jnp.broadcast_to(ai, exps.shape)
    for bit, (fr, fi) in enumerate(((ar, ai), a2, a4)):
        nr, ni = _cmul(pr, pi, fr, fi)
        on = ((e >> bit) & 1) == 1
        pr, pi = jnp.where(on, nr, pr), jnp.where(on, ni, pi)
    return pr, pi, a2, a4


def _s5_scan_fwd(x, x_cols, b_re, b_im, a_re, a_im, *, name):
    T = x.shape[0]
    w, tr = S5_LANES, min(S5_ROWS, T)
    ncol, nt = S5_WIDTH // w, T // tr
    assert x_cols[0] % x_cols[1] == 0

    def body(x_ref, br_ref, bi_ref, ar_ref, ai_ref, sr_ref, si_ref, carry_ref):
        @pl.when(pl.program_id(1) == 0)
        def _():
            carry_ref[...] = jnp.zeros_like(carry_ref)

        u = x_ref[...].astype(_MXU_DTYPE)
        sr_ref[...] = _mxu(u, br_ref[...], 1, 0)
        si_ref[...] = _mxu(u, bi_ref[...], 1, 0)
        ar, ai = ar_ref[...], ai_ref[...]
        rowi = lax.broadcasted_iota(jnp.int32, (SUBLANE, w), 0)
        pr, pi, a2, a4 = _power_table(ar, ai, rowi + 1)

        def tile(i, carry):
            cr, ci = carry
            rows = pl.ds(pl.multiple_of(i * SUBLANE, SUBLANE), SUBLANE)
            xr, xi = sr_ref[rows, :], si_ref[rows, :]
            for s, (fr, fi) in ((1, (ar, ai)), (2, a2), (4, a4)):
                keep = rowi >= s
                zr = jnp.where(keep, pltpu.roll(xr, s, 0), 0.0)
                zi = jnp.where(keep, pltpu.roll(xi, s, 0), 0.0)
                xr, xi = xr + fr * zr - fi * zi, xi + fr * zi + fi * zr
            xr, xi = xr + pr * cr - pi * ci, xi + pr * ci + pi * cr
            sr_ref[rows, :] = xr
            si_ref[rows, :] = xi
            return xr[SUBLANE - 1:SUBLANE, :], xi[SUBLANE - 1:SUBLANE, :]

        cr, ci = lax.fori_loop(0, tr // SUBLANE, tile, (carry_ref[0:1, :], carry_ref[1:2, :]))
        carry_ref[0:1, :] = cr
        carry_ref[1:2, :] = ci

    out = jax.ShapeDtypeStruct((T, S5_WIDTH), F32)
    return pl.pallas_call(
        body, name=name, grid=(ncol, nt),
        in_specs=[pl.BlockSpec((tr, x_cols[1]), lambda j, t: (t, x_cols[0] // x_cols[1])),
                  pl.BlockSpec((S5_DIM, w), lambda j, t: (0, j)), pl.BlockSpec((S5_DIM, w), lambda j, t: (0, j)),
                  pl.BlockSpec((1, w), lambda j, t: (0, j)), pl.BlockSpec((1, w), lambda j, t: (0, j))],
        out_specs=[pl.BlockSpec((tr, w), lambda j, t: (t, j))] * 2,
        out_shape=[out, out],
        scratch_shapes=[pltpu.VMEM((2, w), F32)],
        compiler_params=_cparams(("parallel", "arbitrary")),
    )(x, b_re, b_im, a_re, a_im)


def _s5_scan_bwd(dy, c_re, c_im, s_re, s_im, a_re, a_im, *, name):
    T = dy.shape[0]
    w, tr = S5_LANES, min(S5_ROWS, T)
    ncol, nt = S5_WIDTH // w, T // tr
    ntile = tr // SUBLANE

    def body(dy_ref, cr_ref, ci_ref, sr_ref, si_ref, ar_ref, ai_ref, lr_ref, li_ref, dar_ref, dai_ref, carry_ref):
        @pl.when(pl.program_id(1) == 0)
        def _():
            carry_ref[...] = jnp.zeros_like(carry_ref)
            dar_ref[...] = jnp.zeros_like(dar_ref)
            dai_ref[...] = jnp.zeros_like(dai_ref)

        dyb = dy_ref[...].astype(_MXU_DTYPE)
        lr_ref[...] = _mxu(dyb, cr_ref[...], 1, 1)
        li_ref[...] = _mxu(dyb, ci_ref[...], 1, 1)
        ar, ai = ar_ref[...], -ai_ref[...]
        rowi = lax.broadcasted_iota(jnp.int32, (SUBLANE, w), 0)
        pr, pi, a2, a4 = _power_table(ar, ai, SUBLANE - rowi)
        last = rowi == SUBLANE - 1

        def tile(i, carry):
            cr, ci, dar, dai = carry
            rows = pl.ds(pl.multiple_of((ntile - 1 - i) * SUBLANE, SUBLANE), SUBLANE)
            xr, xi = lr_ref[rows, :], li_ref[rows, :]
            for s, (fr, fi) in ((1, (ar, ai)), (2, a2), (4, a4)):
                keep = rowi < SUBLANE - s
                zr = jnp.where(keep, pltpu.roll(xr, SUBLANE - s, 0), 0.0)
                zi = jnp.where(keep, pltpu.roll(xi, SUBLANE - s, 0), 0.0)
                xr, xi = xr + fr * zr - fi * zi, xi + fr * zi + fi * zr
            xr, xi = xr + pr * cr - pi * ci, xi + pr * ci + pi * cr
            lr_ref[rows, :] = xr
            li_ref[rows, :] = xi
            nr = jnp.where(last, cr, pltpu.roll(xr, SUBLANE - 1, 0))
            ni = jnp.where(last, ci, pltpu.roll(xi, SUBLANE - 1, 0))
            sr, si = sr_ref[rows, :], si_ref[rows, :]
            return xr[0:1, :], xi[0:1, :], dar + nr * sr + ni * si, dai + ni * sr - nr * si

        cr, ci, dar, dai = lax.fori_loop(
            0, ntile, tile, (carry_ref[0:1, :], carry_ref[1:2, :], jnp.zeros((SUBLANE, w), F32), jnp.zeros((SUBLANE, w), F32)))
        carry_ref[0:1, :] = cr
        carry_ref[1:2, :] = ci
        dar_ref[...] += dar
        dai_ref[...] += dai

    tok = pl.BlockSpec((tr, w), lambda j, t: (nt - 1 - t, j))
    vec = pl.BlockSpec((1, w), lambda j, t: (0, j))
    acc = pl.BlockSpec((SUBLANE, w), lambda j, t: (0, j))
    out = jax.ShapeDtypeStruct((T, S5_WIDTH), F32)
    accs = jax.ShapeDtypeStruct((SUBLANE, S5_WIDTH), F32)
    return pl.pallas_call(
        body, name=name, grid=(ncol, nt),
        in_specs=[pl.BlockSpec((tr, S5_DIM), lambda j, t: (nt - 1 - t, 0)),
                  pl.BlockSpec((w, S5_DIM), lambda j, t: (j, 0)), pl.BlockSpec((w, S5_DIM), lambda j, t: (j, 0)),
                  tok, tok, vec, vec],
        out_specs=[tok, tok, acc, acc],
        out_shape=[out, out, accs, accs],
        scratch_shapes=[pltpu.VMEM((2, w), F32)],
        compiler_params=_cparams(("parallel", "arbitrary")),
    )(dy, c_re, c_im, s_re, s_im, a_re, a_im)


ATTN_BLOCK = 512
_NEG = -1e30


def _qk_cat(nope, rope):
    return jnp.concatenate([nope.astype(_MXU_DTYPE), rope.astype(_MXU_DTYPE)], axis=1)


_QK_SCALE = MLA_QK ** -0.5
_LOG2E = math.log2(math.e)


def _attn_scores(q, k, diagonal):
    s = _mxu(q, k, 1, 1) * (_QK_SCALE * _LOG2E)
    if diagonal:
        s = jnp.where(_tri(s.shape[0]), s, _NEG)
    return s


def _attn_fwd(q_all, q_rope, kv, k_rope, *, name):
    T = q_all.shape[0]
    tq = min(ATTN_BLOCK, T)
    nq = T // tq

    def body(qn_ref, qr_ref, kn_ref, v_ref, kr_ref, o_ref, lse_ref):
        i = pl.program_id(1)
        q = _qk_cat(qn_ref[...], qr_ref[0])

        def step(j, carry, diagonal):
            m, l, acc = carry
            ks = pl.ds(pl.multiple_of(j * tq, tq), tq)
            s = _attn_scores(q, _qk_cat(kn_ref[ks, :], kr_ref[ks, :]), diagonal)
            m_new = jnp.maximum(m, jnp.max(s, axis=-1, keepdims=True))
            p = jnp.exp2(s - m_new)
            alpha = jnp.exp2(m - m_new)
            return m_new, alpha * l + jnp.sum(p, axis=-1, keepdims=True), alpha * acc + _mxu(p, v_ref[ks, :], 1, 0)

        init = (jnp.full((tq, 1), _NEG, F32), jnp.zeros((tq, 1), F32), jnp.zeros((tq, MLA_V), F32))
        below = lax.fori_loop(0, i, functools.partial(step, diagonal=False), init)
        m, l, acc = step(i, below, diagonal=True)
        o_ref[...] = acc / l
        lse_ref[0] = m + jnp.log2(l)

    return pl.pallas_call(
        body, name=name, grid=(MLA_HEADS, nq),
        in_specs=[pl.BlockSpec((tq, MLA_NOPE), lambda h, i: (i, h)), pl.BlockSpec((1, tq, MLA_ROPE), lambda h, i: (h, i, 0)),
                  pl.BlockSpec((T, MLA_NOPE), lambda h, i: (0, 2 * h)), pl.BlockSpec((T, MLA_V), lambda h, i: (0, 2 * h + 1)),
                  pl.BlockSpec((T, MLA_ROPE), lambda h, i: (0, 0))],
        out_specs=[pl.BlockSpec((tq, MLA_V), lambda h, i: (i, h)), pl.BlockSpec((1, tq, 1), lambda h, i: (h, i, 0))],
        out_shape=[jax.ShapeDtypeStruct((T, MLA_HEADS * MLA_V), F32), jax.ShapeDtypeStruct((MLA_HEADS, T, 1), F32)],
        compiler_params=_cparams(("arbitrary", "arbitrary")),
    )(q_all, q_rope, kv, kv, k_rope)


def _attn_bwd(q_all, q_rope, kv, k_rope, o, lse, do, *, name):
    T = q_all.shape[0]
    tk = min(ATTN_BLOCK, T)
    nk = T // tk

    def body(qn_ref, qr_ref, kv_ref, kr_ref, o_ref, lse_ref, do_ref, dqn_ref, dqr_ref, dkv_ref, dkr_ref, delta_ref):
        h, j = pl.program_id(0), pl.program_id(1)

        @pl.when(j == 0)
        def _():
            dqn_ref[...] = jnp.zeros_like(dqn_ref)
            dqr_ref[...] = jnp.zeros_like(dqr_ref)
            delta_ref[...] = jnp.sum(do_ref[...] * o_ref[...], axis=-1, keepdims=True)

        @pl.when((j == 0) & (h == 0))
        def _():
            dkr_ref[...] = jnp.zeros_like(dkr_ref)

        krows = pl.ds(pl.multiple_of(j * tk, tk), tk)
        k = _qk_cat(kv_ref[:, :MLA_NOPE], kr_ref[krows, :])
        v = kv_ref[:, MLA_NOPE:].astype(_MXU_DTYPE)

        def step(i, carry, diagonal):
            dk, dv = carry
            qs = pl.ds(pl.multiple_of(i * tk, tk), tk)
            q, dob = _qk_cat(qn_ref[qs, :], qr_ref[0, qs, :]), do_ref[qs, :].astype(_MXU_DTYPE)
            p = jnp.exp2(_attn_scores(q, k, diagonal) - lse_ref[0, qs, :])
            ds = p * (_mxu(dob, v, 1, 1) - delta_ref[qs, :]) * _QK_SCALE
            dq = _mxu(ds, k, 1, 0)
            dqn_ref[qs, :] += dq[:, :MLA_NOPE]
            dqr_ref[0, qs, :] += dq[:, MLA_NOPE:]
            return dk + _mxu(ds, q, 0, 0), dv + _mxu(p, dob, 0, 0)

        on_diagonal = step(j, (jnp.zeros((tk, MLA_QK), F32), jnp.zeros((tk, MLA_V), F32)), diagonal=True)
        dk, dv = lax.fori_loop(j + 1, nk, functools.partial(step, diagonal=False), on_diagonal)
        dkv_ref[:, :MLA_NOPE] = dk[:, :MLA_NOPE].astype(dkv_ref.dtype)
        dkv_ref[:, MLA_NOPE:] = dv.astype(dkv_ref.dtype)
        dkr_ref[krows, :] += dk[:, MLA_NOPE:]

    head_cols = pl.BlockSpec((T, MLA_NOPE), lambda h, j: (0, h))
    head_rope = pl.BlockSpec((1, T, MLA_ROPE), lambda h, j: (h, 0, 0))
    kv_spec = pl.BlockSpec((tk, MLA_NOPE + MLA_V), lambda h, j: (j, h))
    kr_spec = pl.BlockSpec((T, MLA_ROPE), lambda h, j: (0, 0))
    return pl.pallas_call(
        body, name=name, grid=(MLA_HEADS, nk),
        in_specs=[head_cols, head_rope, kv_spec, kr_spec, head_cols, pl.BlockSpec((1, T, 1), lambda h, j: (h, 0, 0)), head_cols],
        out_specs=[head_cols, head_rope, kv_spec, kr_spec],
        out_shape=[jax.ShapeDtypeStruct((T, MLA_HEADS * MLA_NOPE), F32), jax.ShapeDtypeStruct((MLA_HEADS, T, MLA_ROPE), F32),
                   jax.ShapeDtypeStruct((T, MLA_HEADS * (MLA_NOPE + MLA_V)), BF16), jax.ShapeDtypeStruct((T, MLA_ROPE), F32)],
        scratch_shapes=[pltpu.VMEM((T, 1), F32)],
        compiler_params=_cparams(("arbitrary", "arbitrary")),
    )(q_all, q_rope, kv, k_rope, o, lse, do)


def _s5_discretize(a_re, a_im, log_dt, bt_re, bt_im, lb_logits):
    dt = jnp.exp(log_dt)
    mag = jnp.exp(a_re * dt)
    abr, abi = mag * jnp.cos(a_im * dt), mag * jnp.sin(a_im * dt)
    den = a_re * a_re + a_im * a_im
    xr, xi = abr - 1.0, abi
    cr = ((xr * a_re + xi * a_im) / den)[:, None, :]
    ci = ((xi * a_re - xr * a_im) / den)[:, None, :]
    e = jnp.exp(lb_logits - jnp.max(lb_logits, axis=0, keepdims=True))
    lb = e[0:1, :] / jnp.sum(e, axis=0, keepdims=True)
    return abr, abi, cr * bt_re - ci * bt_im, cr * bt_im + ci * bt_re, lb


def _whole(shape):
    return pl.BlockSpec(shape, lambda: (0,) * len(shape))


def _s5_params_fwd(a_re, a_im, log_dt, bt_re, bt_im, lb_logits):
    ins = (a_re, a_im, log_dt, bt_re, bt_im, lb_logits)
    outs = [jax.ShapeDtypeStruct(s, F32) for s in (a_re.shape, a_re.shape, bt_re.shape, bt_re.shape, (1, lb_logits.shape[1]))]

    def body(*refs):
        res = _s5_discretize(*[r[...] for r in refs[:6]])
        for o_ref, o in zip(refs[6:], res):
            o_ref[...] = o

    return pl.pallas_call(body, name="s5_params_fwd", in_specs=[_whole(a.shape) for a in ins],
                          out_specs=[_whole(o.shape) for o in outs], out_shape=outs, compiler_params=_cparams())(*ins)


def _s5_params_bwd(a_re, a_im, log_dt, bt_re, bt_im, lb_logits, d_abr, d_abi, d_bbr, d_bbi, d_lb):
    ins = (a_re, a_im, log_dt, bt_re, bt_im, lb_logits, d_abr, d_abi, d_bbr, d_bbi, d_lb)
    outs = [jax.ShapeDtypeStruct(a.shape, F32) for a in ins[:6]]

    def body(*refs):
        _, vjp = jax.vjp(_s5_discretize, *[r[...] for r in refs[:6]])
        for o_ref, o in zip(refs[11:], vjp(tuple(r[...] for r in refs[6:11]))):
            o_ref[...] = o

    return pl.pallas_call(body, name="s5_params_bwd", in_specs=[_whole(a.shape) for a in ins],
                          out_specs=[_whole(o.shape) for o in outs], out_shape=outs, compiler_params=_cparams())(*ins)


def _adamw(w, g, m, v, *, name):
    R, C = w.shape
    tr = _pick(R, (256, 128, 64, 32, 16, 8)) if R % SUBLANE == 0 else R
    slabs = g.shape[0] if g.ndim == 3 else 0

    def body(w_ref, g_ref, m_ref, v_ref, *outs):
        if slabs:
            gv = g_ref[0].astype(F32)
            for s in range(1, slabs):
                gv = gv + g_ref[s].astype(F32)
            outs[0][...] = gv
            outs = outs[1:]
        else:
            gv = g_ref[...]
        d_ref, mo_ref, vo_ref = outs
        m2 = ADAM_B1 * m_ref[...] + (1.0 - ADAM_B1) * gv
        v2 = ADAM_B2 * v_ref[...] + (1.0 - ADAM_B2) * (gv * gv)
        m_hat = m2 / (1.0 - ADAM_B1 ** ADAM_STEP)
        v_hat = v2 / (1.0 - ADAM_B2 ** ADAM_STEP)
        d_ref[...] = -ADAM_LR * (m_hat / (jnp.sqrt(v_hat) + ADAM_EPS) + ADAM_WD * w_ref[...])
        mo_ref[...] = m2
        vo_ref[...] = v2

    spec = pl.BlockSpec((tr, C), lambda i: (i, 0))
    g_spec = pl.BlockSpec((slabs, tr, C), lambda i: (0, i, 0)) if slabs else spec
    out = jax.ShapeDtypeStruct((R, C), F32)
    n_out = 4 if slabs else 3
    return pl.pallas_call(body, name=name, grid=(R // tr,), in_specs=[spec, g_spec, spec, spec], out_specs=[spec] * n_out,
                          out_shape=[out] * n_out, compiler_params=_cparams(("parallel",)))(w, g, m, v)


N_CHIPS = 4
N_CORES = 2


_FLIPS = tuple((dx, dy, dc) for dx in (0, 1) for dy in (0, 1) for dc in (0, 1) if (dx, dy, dc) != (0, 0, 0))


_HBM = pl.BlockSpec(memory_space=pltpu.HBM)
_SEM = pl.BlockSpec(memory_space=pltpu.SEMAPHORE)
_SPLIT_COPY = pltpu.CompilerParams(has_side_effects=pltpu.SideEffectType.DATAFLOW_SIDE_EFFECTING)


def _exchange_copies(src_refs, land_refs, send_sems, recv_sems, scatter, arriving):
    x, y, c = lax.axis_index("x"), lax.axis_index("y"), lax.axis_index("c")
    me_chip = 2 * x + y
    copies = []
    for a, (s_ref, l_ref) in enumerate(zip(src_refs, land_refs)):
        for j, (dx, dy, dc) in enumerate(_FLIPS):
            px, py, pc = (1 - x if dx else x), (1 - y if dy else y), (1 - c if dc else c)
            k = a * len(_FLIPS) + j
            p_chip = 2 * px + py
            copies.append(pltpu.make_async_remote_copy(
                src_ref=s_ref.at[p_chip, pc] if scatter else s_ref, dst_ref=l_ref.at[p_chip, pc] if arriving else l_ref.at[me_chip, c],
                send_sem=send_sems.at[k], recv_sem=recv_sems.at[k], device_id=(px, py, pc), device_id_type=MESH))
    return copies


def _exchange_start(srcs, *, scatter, name, after=()):
    n_arr = len(srcs)
    n_sem = n_arr * len(_FLIPS)
    n_in = 2 * n_arr + len(after)
    lands = [lax.empty(s.shape if scatter else (N_CHIPS, N_CORES) + s.shape, s.dtype) for s in srcs]

    def body(*refs):
        src_refs, land_refs = refs[:n_arr], refs[n_arr:2 * n_arr]
        for cp in _exchange_copies(src_refs, land_refs, refs[n_in], refs[n_in + 1], scatter, arriving=False):
            cp.start()
        refs[-1][...] = jnp.zeros_like(refs[-1])

    thru = [pltpu.HBM(a.shape, a.dtype) for a in srcs + lands]
    outs = pl.pallas_call(
        body, name=name,
        out_shape=(pltpu.SemaphoreType.DMA((n_sem,)), pltpu.SemaphoreType.DMA((n_sem,)), *thru,
                   jax.ShapeDtypeStruct((SUBLANE, LANE), F32)),
        in_specs=[_HBM] * (2 * n_arr) + [pl.BlockSpec(memory_space=pl.ANY)] * len(after),
        out_specs=(_SEM, _SEM, *[_HBM] * (2 * n_arr), pl.BlockSpec(memory_space=pltpu.VMEM)),
        input_output_aliases={i: 2 + i for i in range(2 * n_arr)}, compiler_params=_SPLIT_COPY,
    )(*[pltpu.with_memory_space_constraint(a, pltpu.HBM) for a in srcs + lands], *after)
    return outs[0], outs[1], list(outs[2:2 + n_arr]), list(outs[2 + n_arr:2 + 2 * n_arr]), outs[-1]


def _exchange_wait(started, after, *, scatter, name):
    send_sems, recv_sems, srcs, lands, _ = started
    n_arr = len(srcs)

    def body(*refs):
        src_refs, land_refs = refs[:n_arr], refs[n_arr:2 * n_arr]
        for cp in _exchange_copies(src_refs, land_refs, refs[2 * n_arr], refs[2 * n_arr + 1], scatter, arriving=True):
            cp.wait_send()
            cp.wait_recv()

    outs = pl.pallas_call(
        body, name=name, out_shape=[pltpu.HBM(a.shape, a.dtype) for a in srcs + lands],
        in_specs=[_HBM] * (2 * n_arr) + [_SEM, _SEM, pl.BlockSpec(memory_space=pl.ANY)], out_specs=[_HBM] * (2 * n_arr),
        input_output_aliases={i: i for i in range(2 * n_arr)}, compiler_params=_SPLIT_COPY,
    )(*srcs, *lands, send_sems, recv_sems, after)
    return list(outs[:n_arr]), list(outs[n_arr:])


def _with_own(land, own):
    me_chip = 2 * lax.axis_index("x") + lax.axis_index("y")
    return lax.dynamic_update_slice(land, own[None, None], (me_chip, lax.axis_index("c")) + (0,) * own.ndim)


def _sum_slabs(buf, out_dtype, *, name):
    G, R, C = buf.shape
    tr = _pick(R, (512, 256, 128, 64, 32, 16, 8))

    def body(b_ref, o_ref):
        acc = b_ref[0].astype(F32)
        for s in range(1, G):
            acc = acc + b_ref[s].astype(F32)
        o_ref[...] = acc.astype(out_dtype)

    return pl.pallas_call(body, name=name, grid=(R // tr,), in_specs=[pl.BlockSpec((G, tr, C), lambda i: (0, i, 0))],
                          out_specs=pl.BlockSpec((tr, C), lambda i: (i, 0)), out_shape=jax.ShapeDtypeStruct((R, C), out_dtype),
                          compiler_params=_cparams(("parallel",)))(buf)


def _rms_fwd_fn(h, g):
    return (_rms(h, g),)


def _rms_bwd_fn(h, dhn, dres, g):
    _, vjp = jax.vjp(_rms, h, g)
    dh, dg = vjp(dhn)
    return dh + dres, dg


def _loss_fn(h, tgt, g):
    y, vjp = jax.vjp(_rms, h, g)
    diff = y - tgt
    dh, dg = vjp(diff * (1.0 / D_MODEL))
    return dh, dg, (0.5 / D_MODEL) * jnp.sum(diff * diff, axis=0, keepdims=True)


def _s5_act(ys, u, d):
    return _gelu(ys + d * u)


def _s5_gate(z, gl, b):
    return z * _sigmoid(gl + b)


def _s5_act_fn(ys, u, d):
    return (_s5_act(ys, u, d),)


def _s5_mix_fn(ya, z, gl, b):
    return (jnp.concatenate([ya, _s5_gate(z, gl, b)], axis=1),)


def _s5_gate_bwd_fn(z, gl, dyb, b):
    _, vjp = jax.vjp(_s5_gate, z, gl, b)
    return vjp(dyb)


def _s5_act_bwd_fn(ys, u, dz1, dz2, d):
    _, vjp = jax.vjp(_s5_act, ys, u, d)
    return vjp(dz1 + dz2)


def _dproj_fn(dq, df, di, dg, du1, du2):
    return (jnp.concatenate([dq, df, di, dg, du1 + du2], axis=1),)


def _rope_pair(r1, r2, pos, freqs):
    ang = pos.astype(F32) * freqs
    c, s = jnp.cos(ang), jnp.sin(ang)
    return r1 * c - r2 * s, r1 * s + r2 * c


_ODD_SPLITS = (0, MLA_Q_RANK, MLA_Q_RANK + MLA_KV_RANK, MLA_Q_RANK + MLA_KV_RANK + LANE, ODD_IN_PAD)


def _mla_prep(cq, ckv, k1, k2, qg, kvg, pos, freqs):
    ko1, ko2 = _rope_pair(k1, k2, pos, freqs)
    return _rms(cq, qg), _rms(ckv, kvg), ko1, ko2


def _mla_prep_fn(proj, pos, qg, kvg, freqs):
    parts = [proj[:, a:b] for a, b in zip(_ODD_SPLITS[:-1], _ODD_SPLITS[1:])]
    return _mla_prep(*parts, qg, kvg, pos, freqs)


def _mla_prep_bwd_fn(proj, pos, dqn, dkvn, dko1, dko2, qg, kvg, freqs):
    parts = [proj[:, a:b] for a, b in zip(_ODD_SPLITS[:-1], _ODD_SPLITS[1:])]
    _, vjp = jax.vjp(lambda *a: _mla_prep(*a, pos, freqs), *parts, qg, kvg)
    dcq, dckv, dk1, dk2, dqg, dkvg = vjp((dqn, dkvn, dko1, dko2))
    return jnp.concatenate([dcq, dckv, dk1, dk2], axis=1), dqg, dkvg


def _rope_q_fn(r1, r2, pos, freqs):
    return _rope_pair(r1, r2, pos, freqs)


def _rope_q_bwd_fn(dqn, do1, do2, pos, freqs):
    dr1, dr2 = _rope_pair(do1, do2, pos, -freqs)
    return (jnp.concatenate([dqn, dr1, dr2], axis=1),)


W_NAMES = ("norm_mix_g", "norm_ffn_g", "final_norm_g", "even_w_in", "hgrn_lb_logits", "hgrn_norm_g", "s5_a_re", "s5_a_im",
           "s5_log_dt", "s5_b_re", "s5_b_im", "s5_c_re", "s5_c_im", "s5_d", "s5_w_glu", "s5_b_glu", "even_w_out", "odd_w_in",
           "mla_q_norm_g", "mla_w_uq", "mla_kv_norm_g", "mla_w_ukv", "odd_w_out", "ffn_w_in", "ffn_conv_w", "ffn_conv_b",
           "ffn_w_out")
BIG_UNITS = (("even_w_in", 0, "col"), ("s5_w_glu", 0, "row"), ("even_w_out", 0, "row"), ("odd_w_in", 0, "row"),
             ("mla_w_uq", 0, "col"), ("mla_w_ukv", 0, "col"), ("odd_w_out", 0, "row"),
             ("ffn_w_in", 0, "col"), ("ffn_w_in", 1, "col"), ("ffn_w_out", 0, "row"), ("ffn_w_out", 1, "row"))
BIG_NAMES = tuple(dict.fromkeys(u[0] for u in BIG_UNITS))
SMALL_SHARDED = (("mla_q_norm_g", 1), ("mla_kv_norm_g", 1), ("ffn_conv_w", 2))
SMALL_SHARDED_NAMES = tuple(s[0] for s in SMALL_SHARDED)
REPLICATED = tuple(n for n in W_NAMES if n not in BIG_NAMES + SMALL_SHARDED_NAMES)
REPLICATED_LATE = ("norm_mix_g",)
REPLICATED_EARLY = tuple(n for n in REPLICATED if n not in REPLICATED_LATE)


def _pack(flats, cols, row_mult):
    flat = jnp.concatenate(flats, axis=-1)
    pad = (-flat.shape[-1]) % (cols * row_mult)
    flat = jnp.pad(flat, [(0, 0)] * (flat.ndim - 1) + [(0, pad)])
    return flat.reshape(flat.shape[:-1] + (-1, cols))


def _unpack(flat, shapes):
    out, off = [], 0
    for shp in shapes:
        n = int(np.prod(shp))
        out.append(flat[..., off:off + n].reshape(flat.shape[:-1] + tuple(shp)))
        off += n
    return out


UNIT_KIND = {(n, l): kind for n, l, kind in BIG_UNITS}
STAGES = ((("even_w_in", 0),),
          (("s5_w_glu", 0), ("even_w_out", 0)),
          (("ffn_w_in", 0), ("ffn_w_out", 0)),
          (("odd_w_in", 0), ("mla_w_uq", 0), ("mla_w_ukv", 0), ("odd_w_out", 0)),
          (("ffn_w_in", 1), ("ffn_w_out", 1)))


def _gather_start(w, stage, with_small, after):
    srcs = [w[n][l].astype(BF16) for n, l in STAGES[stage]]
    if with_small:
        srcs.append(_pack([w[n].reshape(-1) for n in SMALL_SHARDED_NAMES], LANE, SUBLANE))
    return _exchange_start(srcs, scatter=False, name=f"gather_start_{stage}", after=after)


def _gather_finish(started, after, w, stage, with_small):
    srcs, lands = _exchange_wait(started, after, scatter=False, name=f"gather_wait_{stage}")
    lands = [_with_own(land, src) for land, src in zip(lands, srcs)]
    big = {}
    for unit, g in zip(STAGES[stage], lands):
        r, c = g.shape[2:]
        big[unit] = g.reshape(N_DEV * r, c) if UNIT_KIND[unit] == "row" else g.transpose(2, 0, 1, 3).reshape(r, N_DEV * c)
    if not with_small:
        return big
    parts = _unpack(lands[-1].reshape(N_DEV, -1), [w[n].shape for n in SMALL_SHARDED_NAMES])
    small = {}
    for (n, ax), p in zip(SMALL_SHARDED, parts):
        shp = list(w[n].shape)
        shp[ax] *= N_DEV
        small[n] = jnp.moveaxis(p, 0, ax).reshape(shp)
    return big, small


def _scatter_start(g_big, stage, extra=()):
    srcs = []
    for unit in STAGES[stage]:
        g = g_big[unit].astype(BF16)
        if UNIT_KIND[unit] == "row":
            srcs.append(g.reshape(N_CHIPS, N_CORES, g.shape[0] // N_DEV, g.shape[1]))
        else:
            srcs.append(g.reshape(g.shape[0], N_CHIPS, N_CORES, g.shape[1] // N_DEV).transpose(1, 2, 0, 3))
    return _exchange_start(srcs + list(extra), scatter=True, name=f"scatter_start_{stage}")


def _scatter_finish(started, after, stage):
    srcs, lands = _exchange_wait(started, after, scatter=True, name=f"scatter_wait_{stage}")
    me_chip, c = 2 * lax.axis_index("x") + lax.axis_index("y"), lax.axis_index("c")
    outs = []
    for land, src in zip(lands, srcs):
        own = lax.dynamic_slice(src, (me_chip, c) + (0,) * (src.ndim - 2), (1, 1) + src.shape[2:])[0, 0]
        outs.append(_with_own(land, own).reshape((N_DEV,) + land.shape[2:]))
    return outs


def _small_sharded_pack(g_small, w):
    flats = []
    for n, ax in SMALL_SHARDED:
        shp = list(w[n].shape)
        g = g_small[n].astype(F32).reshape(shp[:ax] + [N_DEV] + shp[ax:])
        flats.append(jnp.moveaxis(g, ax, 0).reshape(N_DEV, -1))
    small = _pack(flats, LANE, SUBLANE)
    return small.reshape((N_CHIPS, N_CORES) + small.shape[1:])


def _replicated_pack(g_repl, names):
    vec = _pack([g_repl[n].reshape(-1).astype(F32) for n in names], LANE, SUBLANE)
    return jnp.broadcast_to(vec, (N_CHIPS, N_CORES) + vec.shape)


def _block_diag(blocks):
    G, a, b = blocks.shape
    return jnp.einsum('gab,gk->gakb', blocks, jnp.eye(G, dtype=blocks.dtype)).reshape(G * a, G * b)


def _diag_blocks(mat, a, b):
    G = mat.shape[0] // a
    return jnp.einsum('gagb->gab', mat.reshape(G, a, G, b))


def _ffn_fwd(h, g, w_in, conv_w, conv_b, w_out, tag):
    hn, = _rows(_rms_fwd_fn, [h], [g], [(D_MODEL, BF16)], [], name=f"ffn{tag}_norm")
    au = _mm(hn, w_in, out_dtype=BF16, name=f"ffn{tag}_in")
    z = _ffn_mid_fwd(au, conv_w, conv_b, name=f"ffn{tag}_mid")
    return _mm(z, w_out, res=h, name=f"ffn{tag}_out"), (hn, au, z)


def _ffn_bwd(h, dh, saved, g, w_in, conv_w, conv_b, w_out, tag, deps=()):
    hn, au, z = saved
    dz = _mm(dh, w_out, tb=True, out_dtype=BF16, deps=deps, name=f"ffn{tag}_dz")
    dw_out = _mm(z, dh, ta=True, out_dtype=BF16, name=f"ffn{tag}_dwout")
    da, du, dcw, dcb = _ffn_mid_bwd(au, dz, conv_w, conv_b, name=f"ffn{tag}_dmid")
    dhn = _mm(da, w_in, tb=True, b_cols=(0, D_FF), name=f"ffn{tag}_dhn_a")
    dhn = _mm(du, w_in, tb=True, b_cols=(D_FF, D_FF), res=dhn, name=f"ffn{tag}_dhn_u")
    dw_in = jnp.concatenate([_mm(hn, da, ta=True, out_dtype=BF16, name=f"ffn{tag}_dwin_a"),
                             _mm(hn, du, ta=True, out_dtype=BF16, name=f"ffn{tag}_dwin_u")], axis=1)
    dh_in, dg = _rows(_rms_bwd_fn, [h, dhn, dh], [g], [(D_MODEL, F32)], [(1, D_MODEL)], name=f"ffn{tag}_dnorm")
    return dh_in, dict(g=dg, w_in=dw_in, conv_w=dcw, conv_b=dcb, w_out=dw_out)


def kernel(x, positions, norm_mix_g, norm_ffn_g, final_norm_g, even_w_in, hgrn_lb_logits, hgrn_norm_g, s5_a_re, s5_a_im, s5_log_dt, s5_b_re, s5_b_im, s5_c_re, s5_c_im, s5_d, s5_w_glu, s5_b_glu, even_w_out, odd_w_in, mla_q_norm_g, mla_w_uq, mla_kv_norm_g, mla_w_ukv, odd_w_out, ffn_w_in, ffn_conv_w, ffn_conv_b, ffn_w_out, loss_target, m_norm_mix_g, m_norm_ffn_g, m_final_norm_g, m_even_w_in, m_hgrn_lb_logits, m_hgrn_norm_g, m_s5_a_re, m_s5_a_im, m_s5_log_dt, m_s5_b_re, m_s5_b_im, m_s5_c_re, m_s5_c_im, m_s5_d, m_s5_w_glu, m_s5_b_glu, m_even_w_out, m_odd_w_in, m_mla_q_norm_g, m_mla_w_uq, m_mla_kv_norm_g, m_mla_w_ukv, m_odd_w_out, m_ffn_w_in, m_ffn_conv_w, m_ffn_conv_b, m_ffn_w_out, v_norm_mix_g, v_norm_ffn_g, v_final_norm_g, v_even_w_in, v_hgrn_lb_logits, v_hgrn_norm_g, v_s5_a_re, v_s5_a_im, v_s5_log_dt, v_s5_b_re, v_s5_b_im, v_s5_c_re, v_s5_c_im, v_s5_d, v_s5_w_glu, v_s5_b_glu, v_even_w_out, v_odd_w_in, v_mla_q_norm_g, v_mla_w_uq, v_mla_kv_norm_g, v_mla_w_ukv, v_odd_w_out, v_ffn_w_in, v_ffn_conv_w, v_ffn_conv_b, v_ffn_w_out):
    given = dict(locals())
    w = {n: given[n] for n in W_NAMES}
    mom = {n: given["m_" + n] for n in W_NAMES}
    var = {n: given["v_" + n] for n in W_NAMES}
    T = x.shape[1]
    h0 = x[0]
    tgt = loss_target[0]
    pos = positions.reshape(T, 1)

    gathers = []
    for s in range(len(STAGES)):
        gathers.append(_gather_start(w, s, with_small=(s == 1), after=[g[4] for g in gathers[-1:]]))
    half = MLA_ROPE // 2
    kr0 = MLA_Q_RANK + MLA_KV_RANK
    freqs = ROPE_THETA ** (-jnp.arange(0, MLA_ROPE, 2, dtype=F32) / MLA_ROPE)
    freqs_q = jnp.tile(freqs, MLA_HEADS)[None, :]
    freqs_k = jnp.concatenate([freqs, jnp.zeros((LANE - half,), F32)])[None, :]

    sp_in = (s5_a_re[0], s5_a_im[0], s5_log_dt[0][:, None], s5_b_re[0].transpose(0, 2, 1), s5_b_im[0].transpose(0, 2, 1),
             hgrn_lb_logits)
    abr, abi, bbt_re, bbt_im, lb0 = _s5_params_fwd(*sp_in)
    a_re, a_im = abr.reshape(1, S5_WIDTH), abi.reshape(1, S5_WIDTH)
    bb_re, bb_im = _block_diag(bbt_re).astype(BF16), _block_diag(bbt_im).astype(BF16)
    c_re = _block_diag(s5_c_re[0].transpose(0, 2, 1)).astype(BF16)
    c_im_neg = _block_diag(-s5_c_im[0].transpose(0, 2, 1)).astype(BF16)
    u_cols = (4 * HGRN_DIM, S5_DIM)

    hn0, = _rows(_rms_fwd_fn, [h0], [norm_mix_g[0:1]], [(D_MODEL, BF16)], [], name="mix0_norm", deps=[gathers[-1][4]])
    full = _gather_finish(gathers[0], hn0, w, 0, False)
    w_ein = full["even_w_in", 0]
    proj = _mm(hn0, w_ein, name="even_in")
    y_a, states = _hgrn_fwd(proj, lb0, hgrn_norm_g, name="hgrn_fwd")
    s_re, s_im = _s5_scan_fwd(proj, u_cols, bb_re, bb_im, a_re, a_im, name="s5_scan_fwd")
    more, full_small = _gather_finish(gathers[1], s_re, w, 1, True)
    w_glu, w_eout = more["s5_w_glu", 0], more["even_w_out", 0]
    qg, kvg, conv_w = full_small["mla_q_norm_g"], full_small["mla_kv_norm_g"], full_small["ffn_conv_w"]
    ys = _mm(s_im, c_im_neg, res=_mm(s_re, c_re, name="s5_y_re"), name="s5_y_im")
    z5, = _rows(_s5_act_fn, [ys, (proj,) + u_cols], [s5_d], [(S5_DIM, F32)], [], name="s5_act")
    gl = _mm(z5, w_glu, name="s5_glu")
    mixin, = _rows(_s5_mix_fn, [y_a, z5, gl], [s5_b_glu], [(D_MODEL, BF16)], [], name="s5_mix")
    h1 = _mm(mixin, w_eout, res=h0, name="even_out")
    full.update(_gather_finish(gathers[2], h1, w, 2, False))
    w_fin, w_fout = [full["ffn_w_in", 0]], [full["ffn_w_out", 0]]
    h2, ffn0_saved = _ffn_fwd(h1, norm_ffn_g[0:1], w_fin[0], conv_w[0], ffn_conv_b[0:1], w_fout[0], 0)

    full.update(_gather_finish(gathers[3], h2, w, 3, False))
    w_oin, w_ukv, w_oout = full["odd_w_in", 0], full["mla_w_ukv", 0], full["odd_w_out", 0]
    zpad = jnp.zeros((D_MODEL, LANE - half), BF16)
    w_oin_pad = jnp.concatenate([w_oin[:, :kr0], w_oin[:, kr0:kr0 + half], zpad, w_oin[:, kr0 + half:], zpad], axis=1)
    w_uq3 = full["mla_w_uq", 0].reshape(MLA_Q_RANK, MLA_HEADS, MLA_QK)
    w_uq_perm = jnp.concatenate([w_uq3[:, :, :MLA_NOPE].reshape(MLA_Q_RANK, -1),
                                 w_uq3[:, :, MLA_NOPE:MLA_NOPE + half].reshape(MLA_Q_RANK, -1),
                                 w_uq3[:, :, MLA_NOPE + half:].reshape(MLA_Q_RANK, -1)], axis=1)
    hn1, = _rows(_rms_fwd_fn, [h2], [norm_mix_g[1:2]], [(D_MODEL, BF16)], [], name="mix1_norm")
    proj_o = _mm(hn1, w_oin_pad, name="odd_in")
    qn, kvn, ko1, ko2 = _rows(_mla_prep_fn, [proj_o, pos], [qg, kvg, freqs_k],
                              [(MLA_Q_RANK, BF16), (MLA_KV_RANK, BF16), (LANE, F32), (LANE, F32)], [], name="mla_prep")
    q_all = _mm(qn, w_uq_perm, name="mla_uq")
    kv = _mm(kvn, w_ukv, out_dtype=BF16, name="mla_ukv")
    nope_w = MLA_HEADS * MLA_NOPE
    rope_w = MLA_HEADS * half
    o1, o2 = _rows(_rope_q_fn, [(q_all, nope_w, rope_w), (q_all, nope_w + rope_w, rope_w), pos], [freqs_q],
                   [(rope_w, F32), (rope_w, F32)], [], name="mla_rope_q")
    q_rope = jnp.concatenate([o1.reshape(T, MLA_HEADS, half), o2.reshape(T, MLA_HEADS, half)], axis=2).transpose(1, 0, 2)
    k_rope = jnp.concatenate([ko1[:, :half], ko2[:, :half]], axis=1)
    o, lse = _attn_fwd(q_all, q_rope, kv, k_rope, name="attn_fwd")
    h3 = _mm(o, w_oout, res=h2, name="odd_out")
    full.update(_gather_finish(gathers[4], h3, w, 4, False))
    w_fin.append(full["ffn_w_in", 1])
    w_fout.append(full["ffn_w_out", 1])
    h4, ffn1_saved = _ffn_fwd(h3, norm_ffn_g[1:2], w_fin[1], conv_w[1], ffn_conv_b[1:2], w_fout[1], 1)

    dh4, d_final_g, loss_cols = _rows(_loss_fn, [h4, tgt], [final_norm_g[None, :]], [(D_MODEL, F32)],
                                      [(1, D_MODEL), (1, D_MODEL)], name="loss_head")
    loss = lax.psum(jnp.sum(loss_cols), ("x", "y", "c"))

    dh3, gf1 = _ffn_bwd(h3, dh4, ffn1_saved, norm_ffn_g[1:2], w_fin[1], conv_w[1], ffn_conv_b[1:2], w_fout[1], 1)
    scatters = {4: _scatter_start({("ffn_w_in", 1): gf1["w_in"], ("ffn_w_out", 1): gf1["w_out"]}, 4)}
    do = _mm(dh3, w_oout, tb=True, deps=[scatters[4][4]], name="odd_out_dx")
    d_w_oout = _mm(o, dh3, ta=True, out_dtype=BF16, name="odd_out_dw")
    dq_nope, dq_rope, dkv, dk_rope = _attn_bwd(q_all, q_rope, kv, k_rope, o, lse, do, name="attn_bwd")
    dq_rope_t = dq_rope.transpose(1, 0, 2)
    do1, do2 = dq_rope_t[:, :, :half].reshape(T, rope_w), dq_rope_t[:, :, half:].reshape(T, rope_w)
    lane_pad = ((0, 0), (0, LANE - half))
    dko1, dko2 = jnp.pad(dk_rope[:, :half], lane_pad), jnp.pad(dk_rope[:, half:], lane_pad)
    dq_all, = _rows(_rope_q_bwd_fn, [dq_nope, do1, do2, pos], [freqs_q], [(MLA_HEADS * MLA_QK, BF16)], [], name="mla_rope_q_bwd")
    d_w_uq_perm = _mm(qn, dq_all, ta=True, out_dtype=BF16, name="mla_uq_dw")
    dqn = _mm(dq_all, w_uq_perm, tb=True, name="mla_uq_dx")
    d_w_ukv = _mm(kvn, dkv, ta=True, out_dtype=BF16, name="mla_ukv_dw")
    dkvn = _mm(dkv, w_ukv, tb=True, name="mla_ukv_dx")
    dproj_o, d_qg, d_kvg = _rows(_mla_prep_bwd_fn, [proj_o, pos, dqn, dkvn, dko1, dko2], [qg, kvg, freqs_k],
                                 [(ODD_IN_PAD, BF16)], [(1, MLA_Q_RANK), (1, MLA_KV_RANK)], name="mla_prep_bwd")
    d_w_oin_pad = _mm(hn1, dproj_o, ta=True, out_dtype=BF16, name="odd_in_dw")
    dhn1 = _mm(dproj_o, w_oin_pad, tb=True, name="odd_in_dx")
    dh2, d_mix_g1 = _rows(_rms_bwd_fn, [h2, dhn1, dh3], [norm_mix_g[1:2]], [(D_MODEL, F32)], [(1, D_MODEL)], name="mix1_dnorm")
    d_w_oin = jnp.concatenate([d_w_oin_pad[:, :kr0 + half], d_w_oin_pad[:, kr0 + LANE:kr0 + LANE + half]], axis=1)
    d3 = d_w_uq_perm
    d_w_uq = jnp.concatenate([d3[:, :nope_w].reshape(MLA_Q_RANK, MLA_HEADS, MLA_NOPE),
                              d3[:, nope_w:nope_w + rope_w].reshape(MLA_Q_RANK, MLA_HEADS, half),
                              d3[:, nope_w + rope_w:].reshape(MLA_Q_RANK, MLA_HEADS, half)], axis=2).reshape(MLA_Q_RANK, -1)
    scatters[3] = _scatter_start({("odd_w_in", 0): d_w_oin, ("mla_w_uq", 0): d_w_uq, ("mla_w_ukv", 0): d_w_ukv,
                                  ("odd_w_out", 0): d_w_oout}, 3)

    dh1, gf0 = _ffn_bwd(h1, dh2, ffn0_saved, norm_ffn_g[0:1], w_fin[0], conv_w[0], ffn_conv_b[0:1], w_fout[0], 0,
                        deps=[scatters[3][4]])
    scatters[2] = _scatter_start({("ffn_w_in", 0): gf0["w_in"], ("ffn_w_out", 0): gf0["w_out"]}, 2)
    dmix = _mm(dh1, w_eout, tb=True, deps=[scatters[2][4]], name="even_out_dx")
    d_w_eout = _mm(mixin, dh1, ta=True, out_dtype=BF16, name="even_out_dw")
    dq, df, di, dg, d_lb0, d_hgrn_g = _hgrn_bwd(proj, lb0, hgrn_norm_g, states, dmix, name="hgrn_bwd")
    dz1, dgl, d_b_glu = _rows(_s5_gate_bwd_fn, [z5, gl, (dmix, HGRN_DIM, S5_DIM)], [s5_b_glu],
                              [(S5_DIM, F32), (S5_DIM, BF16)], [(1, S5_DIM)], name="s5_gate_bwd")
    dz2 = _mm(dgl, w_glu, tb=True, name="s5_glu_dx")
    d_w_glu = _mm(z5, dgl, ta=True, out_dtype=BF16, name="s5_glu_dw")
    dys, du1, d_s5_d = _rows(_s5_act_bwd_fn, [ys, (proj,) + u_cols, dz1, dz2], [s5_d],
                             [(S5_DIM, BF16), (S5_DIM, F32)], [(1, S5_DIM)], name="s5_act_bwd")
    d_c_re = _mm(s_re, dys, ta=True, name="s5_dc_re")
    d_c_im_neg = _mm(s_im, dys, ta=True, name="s5_dc_im")
    lam_re, lam_im, d_ar, d_ai = _s5_scan_bwd(dys, c_re, c_im_neg, s_re, s_im, a_re, a_im, name="s5_scan_bwd")
    du2 = _mm(lam_im, bb_im, tb=True, res=_mm(lam_re, bb_re, tb=True, name="s5_du_re"), name="s5_du_im")
    d_bb_re = _mm(proj, lam_re, ta=True, a_cols=u_cols, name="s5_dbb_re")
    d_bb_im = _mm(proj, lam_im, ta=True, a_cols=u_cols, name="s5_dbb_im")
    sp_g = _s5_params_bwd(*sp_in, d_ar.sum(0).reshape(S5_GROUPS, S5_STATE), d_ai.sum(0).reshape(S5_GROUPS, S5_STATE),
                          _diag_blocks(d_bb_re, S5_GROUP, S5_STATE), _diag_blocks(d_bb_im, S5_GROUP, S5_STATE), d_lb0)
    d_a_re, d_a_im, d_log_dt, d_bt_re, d_bt_im, d_lb_logits = sp_g
    g_small = dict(mla_q_norm_g=d_qg, mla_kv_norm_g=d_kvg, ffn_conv_w=jnp.stack([gf0["conv_w"], gf1["conv_w"]]))
    g_repl = dict(
        norm_ffn_g=jnp.concatenate([gf0["g"], gf1["g"]]),
        final_norm_g=d_final_g[0], hgrn_lb_logits=d_lb_logits, hgrn_norm_g=d_hgrn_g,
        s5_a_re=d_a_re[None], s5_a_im=d_a_im[None], s5_log_dt=d_log_dt[:, 0][None],
        s5_b_re=d_bt_re.transpose(0, 2, 1)[None], s5_b_im=d_bt_im.transpose(0, 2, 1)[None],
        s5_c_re=_diag_blocks(d_c_re, S5_STATE, S5_GROUP).transpose(0, 2, 1)[None],
        s5_c_im=-_diag_blocks(d_c_im_neg, S5_STATE, S5_GROUP).transpose(0, 2, 1)[None],
        s5_d=d_s5_d, s5_b_glu=d_b_glu, ffn_conv_b=jnp.concatenate([gf0["conv_b"], gf1["conv_b"]]))
    scatters[1] = _scatter_start({("s5_w_glu", 0): d_w_glu, ("even_w_out", 0): d_w_eout}, 1,
                                 extra=[_small_sharded_pack(g_small, w), _replicated_pack(g_repl, REPLICATED_EARLY)])
    dproj, = _rows(_dproj_fn, [dq, df, di, dg, du1, du2], [], [(EVEN_IN, BF16)], [], name="even_dproj", deps=[scatters[1][4]])
    d_w_ein = _mm(hn0, dproj, ta=True, out_dtype=BF16, name="even_in_dw")
    dhn0 = _mm(dproj, w_ein, tb=True, name="even_in_dx")
    grad_x, d_mix_g0 = _rows(_rms_bwd_fn, [h0, dhn0, dh1], [norm_mix_g[0:1]], [(D_MODEL, F32)], [(1, D_MODEL)], name="mix0_dnorm")
    g_repl["norm_mix_g"] = jnp.concatenate([d_mix_g0, d_mix_g1])
    scatters[0] = _scatter_start({("even_w_in", 0): d_w_ein}, 0, extra=[_replicated_pack(g_repl, REPLICATED_LATE)])

    delta, new_m, new_v = {}, {}, {}
    per_unit, partial = {}, {}
    after = scatters[0][4]
    for stage in (4, 3, 2, 1, 0):
        partial[stage] = _scatter_finish(scatters[stage], after, stage)
        for (n, l), slabs in zip(STAGES[stage], partial[stage]):
            per_unit[n, l] = _adamw(w[n][l], slabs, mom[n][l], var[n][l], name=f"adamw_{n}_{l}")
        after = per_unit[STAGES[stage][-1]][0]
    grads = {}
    for names, slabs, tag in ((REPLICATED_EARLY, partial[1][-1], "repl"), (REPLICATED_LATE, partial[0][-1], "late"),
                              (SMALL_SHARDED_NAMES, partial[1][-2], "small")):
        total = _sum_slabs(slabs, F32, name=f"sum_g_{tag}").reshape(-1)
        grads.update(zip(names, _unpack(total, [w[n].shape for n in names])))
    for n in BIG_NAMES:
        layers = [per_unit[n, l] for l in range(w[n].shape[0])]
        grads[n], delta[n], new_m[n], new_v[n] = (jnp.stack([lay[k] for lay in layers]) for k in range(4))
    small = [n for n in W_NAMES if n not in BIG_NAMES]
    packed = [_pack([t[n].reshape(-1) for n in small], LANE, SUBLANE) for t in (w, grads, mom, var)]
    outs = _adamw(*packed, name="adamw_small")
    small_shapes = [w[n].shape for n in small]
    for dst, o_ in zip((delta, new_m, new_v), outs):
        dst.update(zip(small, _unpack(o_.reshape(-1), small_shapes)))

    return (loss, grad_x[None], *[grads[n] for n in W_NAMES], *[delta[n] for n in W_NAMES],
            *[new_m[n] for n in W_NAMES], *[new_v[n] for n in W_NAMES])
```

```python
import functools
import math

import numpy as np
import jax
import jax.numpy as jnp
from jax import lax
from jax.experimental import pallas as pl
from jax.experimental.pallas import tpu as pltpu

F32 = jnp.float32
BF16 = jnp.bfloat16
_MXU_DTYPE = jnp.bfloat16

D_MODEL = 1024
HGRN_DIM = 512
HGRN_HEAD_DIM = 128
HGRN_HEADS = 4
HGRN_CHUNK = 64
S5_DIM = 512
S5_GROUPS = 32
S5_GROUP = 16
S5_STATE = 64
S5_WIDTH = S5_GROUPS * S5_STATE
EVEN_IN = 4 * HGRN_DIM + S5_DIM
MLA_HEADS = 8
MLA_Q_RANK = 384
MLA_KV_RANK = 256
MLA_NOPE = 128
MLA_ROPE = 64
MLA_V = 128
MLA_QK = MLA_NOPE + MLA_ROPE
ODD_IN = MLA_Q_RANK + MLA_KV_RANK + MLA_ROPE
ODD_IN_PAD = MLA_Q_RANK + MLA_KV_RANK + 2 * 128
ROPE_THETA = 10000.0
D_FF = 2816
EPS = 1e-6
ADAM_LR = 0.001
ADAM_B1 = 0.9
ADAM_B2 = 0.999
ADAM_EPS = 1e-08
ADAM_WD = 0.01
ADAM_STEP = 10

N_DEV = 8
LANE = 128
SUBLANE = 8
VMEM_LIMIT_BYTES = 56 * 1024 * 1024
MESH = pl.DeviceIdType.MESH


def _cparams(sem=None):
    return pltpu.CompilerParams(dimension_semantics=sem, vmem_limit_bytes=VMEM_LIMIT_BYTES)


def _pick(n, cands):
    for c in cands:
        if n % c == 0:
            return c
    raise ValueError(f"no tile for {n} in {cands}")


def _sigmoid(x):
    return 1.0 / (1.0 + jnp.exp(-x))


def _silu(x):
    return x * _sigmoid(x)


def _gelu(x):
    return 0.5 * x * (1.0 + jnp.tanh(math.sqrt(2.0 / math.pi) * (x + 0.044715 * (x * x * x))))


def _rms(x, g):
    return x * lax.rsqrt(jnp.mean(x * x, axis=-1, keepdims=True) + EPS) * g


def _mxu(a, b, ca, cb):
    return lax.dot_general(a.astype(_MXU_DTYPE), b.astype(_MXU_DTYPE), (((ca,), (cb,)), ((), ())),
                           preferred_element_type=F32)


@functools.partial(jax.custom_vjp, nondiff_argnums=(2, 3))
def _mxu_ad(a, b, ca, cb):
    return _mxu(a, b, ca, cb)


def _mxu_ad_fwd(a, b, ca, cb):
    return _mxu(a, b, ca, cb), (a, b)


def _mxu_ad_bwd(ca, cb, saved, g):
    a, b = saved
    fa, fb = 1 - ca, 1 - cb
    da = _mxu(g, b, 1, fb) if ca == 1 else _mxu(b, g, fb, 1)
    db = _mxu(a, g, fa, 0) if cb == 0 else _mxu(g, a, 0, fa)
    return da, db


_mxu_ad.defvjp(_mxu_ad_fwd, _mxu_ad_bwd)


def _tri(n):
    row = lax.broadcasted_iota(jnp.int32, (n, n), 0)
    col = lax.broadcasted_iota(jnp.int32, (n, n), 1)
    return col <= row


def _cumsum_rows(x, reverse=False):
    n = x.shape[0]
    rowi = lax.broadcasted_iota(jnp.int32, x.shape, 0)
    s = 1
    while s < n:
        if reverse:
            x = x + jnp.where(rowi < n - s, pltpu.roll(x, n - s, 0), 0.0)
        else:
            x = x + jnp.where(rowi >= s, pltpu.roll(x, s, 0), 0.0)
        s *= 2
    return x


@jax.custom_vjp
def _cumsum_rows_ad(x):
    return _cumsum_rows(x)


def _cumsum_rows_ad_fwd(x):
    return _cumsum_rows(x), None


def _cumsum_rows_ad_bwd(_, g):
    return (_cumsum_rows(g, reverse=True),)


_cumsum_rows_ad.defvjp(_cumsum_rows_ad_fwd, _cumsum_rows_ad_bwd)


MM_VMEM_BUDGET = 36 * 1024 * 1024
MM_MAX_TILE = 1408


def _lane_divisors(n, cap, offs=()):
    return [d for d in range(min(n, cap) // LANE * LANE, 0, -LANE) if n % d == 0 and all(o % d == 0 for o in offs)]


def _mm_tiles(M, N, K, sa, sb, so, has_res, m_offs, n_offs, k_offs):
    best = None
    for tm in _lane_divisors(M, MM_MAX_TILE, m_offs):
        for tn in _lane_divisors(N, MM_MAX_TILE, n_offs):
            for tk in _lane_divisors(K, K, k_offs):
                nk = K // tk
                vmem = 2 * (tm * tk * sa + tk * tn * sb + tm * tn * so + tm * tn * 4 * has_res) + (tm * tn * 4 if nk > 1 else 0)
                if vmem <= MM_VMEM_BUDGET:
                    key = (-nk, tm * tn, tn)
                    if best is None or key > best[0]:
                        best = (key, tm, tn, tk)
                    break
    return best[1:]


def _mm(a, b, *, ta=False, tb=False, res=None, out_dtype=F32, a_cols=None, b_cols=None, deps=(), name):
    a_minor = a.shape[1] if a_cols is None else a_cols[1]
    b_minor = b.shape[1] if b_cols is None else b_cols[1]
    K, M = (a.shape[0], a_minor) if ta else (a_minor, a.shape[0])
    N = b.shape[0] if tb else b_minor
    assert (b_minor if tb else b.shape[0]) == K, (a.shape, b.shape, ta, tb)
    a_off = 0 if a_cols is None else a_cols[0]
    b_off = 0 if b_cols is None else b_cols[0]
    has_res = res is not None
    tm, tn, tk = _mm_tiles(M, N, K, a.dtype.itemsize, b.dtype.itemsize, jnp.dtype(out_dtype).itemsize, has_res,
                           (a_off,) if ta else (), () if tb else (b_off,), ((a_off,) if not ta else ()) + ((b_off,) if tb else ()))
    nk = K // tk
    am, ak = (a_off // tm, 0) if ta else (0, a_off // tk)
    bn, bk = (0, b_off // tk) if tb else (b_off // tn, 0)
    a_spec = pl.BlockSpec((tk, tm), lambda i, j, k: (k, i + am)) if ta else pl.BlockSpec((tm, tk), lambda i, j, k: (i, k + ak))
    b_spec = pl.BlockSpec((tn, tk), lambda i, j, k: (j, k + bk)) if tb else pl.BlockSpec((tk, tn), lambda i, j, k: (k, j + bn))
    o_spec = pl.BlockSpec((tm, tn), lambda i, j, k: (i, j))
    ca, cb = (0 if ta else 1), (1 if tb else 0)

    n_fixed = 2 + has_res + len(deps)

    def body(*refs):
        a_ref, b_ref = refs[0], refs[1]
        res_ref = refs[2] if has_res else None
        o_ref = refs[n_fixed]
        part = _mxu(a_ref[...], b_ref[...], ca, cb)
        if nk == 1:
            o_ref[...] = (part + res_ref[...] if has_res else part).astype(out_dtype)
            return
        acc_ref = refs[n_fixed + 1]
        k = pl.program_id(2)

        @pl.when(k == 0)
        def _():
            acc_ref[...] = part

        @pl.when(k > 0)
        def _():
            acc_ref[...] += part

        @pl.when(k == nk - 1)
        def _():
            o_ref[...] = (acc_ref[...] + res_ref[...] if has_res else acc_ref[...]).astype(out_dtype)

    ins = [a, b] + ([res] if has_res else []) + list(deps)
    in_specs = [a_spec, b_spec] + ([o_spec] if has_res else []) + [pl.BlockSpec(memory_space=pl.ANY)] * len(deps)
    return pl.pallas_call(
        body, name=name, grid=(M // tm, N // tn, nk),
        in_specs=in_specs, out_specs=o_spec,
        out_shape=jax.ShapeDtypeStruct((M, N), out_dtype),
        scratch_shapes=[pltpu.VMEM((tm, tn), F32)] if nk > 1 else [],
        compiler_params=_cparams(("parallel", "parallel", "arbitrary")),
    )(*ins)


def _rows(fn, row_ins, const_ins, row_outs, acc_outs, *, name, tm=512, deps=()):
    norm = [(r, 0, r.shape[1]) if not isinstance(r, tuple) else r for r in row_ins]
    T = norm[0][0].shape[0]
    tm = min(tm, T)
    nr, nc, no, na = len(norm), len(const_ins), len(row_outs), len(acc_outs)
    first_out = nr + nc + len(deps)

    def body(*refs):
        i = pl.program_id(0)
        vals = [r[...] for r in refs[:nr + nc]]
        outs = fn(*vals)
        for o_ref, o in zip(refs[first_out:first_out + no], outs[:no]):
            o_ref[...] = o.astype(o_ref.dtype)
        for a_ref, o in zip(refs[first_out + no:], outs[no:]):
            @pl.when(i == 0)
            def _(a_ref=a_ref, o=o):
                a_ref[...] = o

            @pl.when(i > 0)
            def _(a_ref=a_ref, o=o):
                a_ref[...] += o

    in_specs = []
    for arr, off, w in norm:
        assert off % w == 0, (off, w)
        in_specs.append(pl.BlockSpec((tm, w), lambda i, b=off // w: (i, b)))
    for c in const_ins:
        in_specs.append(pl.BlockSpec(c.shape, lambda i: (0, 0)))
    in_specs += [pl.BlockSpec(memory_space=pl.ANY)] * len(deps)
    out_specs = [pl.BlockSpec((tm, w), lambda i: (i, 0)) for w, _ in row_outs]
    out_specs += [pl.BlockSpec(s, lambda i: (0, 0)) for s in acc_outs]
    out_shape = [jax.ShapeDtypeStruct((T, w), dt) for w, dt in row_outs]
    out_shape += [jax.ShapeDtypeStruct(s, F32) for s in acc_outs]
    return pl.pallas_call(
        body, name=name, grid=(T // tm,), in_specs=in_specs, out_specs=out_specs, out_shape=out_shape,
        compiler_params=_cparams(("arbitrary",)),
    )(*[n[0] for n in norm], *const_ins, *deps)


FFN_COL_TILE = LANE
FFN_ROW_CHUNK = 512


FFN_HALO = 2 * SUBLANE


def _shift_down(ext, s, rows):
    return pltpu.roll(ext, s, 0)[FFN_HALO:FFN_HALO + rows]


def _shift_up(ext, s, rows):
    return pltpu.roll(ext, rows + FFN_HALO - s, 0)[:rows]


def _ffn_chunks(T):
    r = min(FFN_ROW_CHUNK, T)
    return r, T // r


def _ext_before(ref, c, r):
    if c == 0:
        return jnp.concatenate([jnp.zeros((FFN_HALO, ref.shape[1]), F32), ref[0:r, :].astype(F32)], axis=0)
    return ref[c * r - FFN_HALO:(c + 1) * r, :].astype(F32)


def _ext_after(ref, c, r, nch):
    if c == nch - 1:
        return jnp.concatenate([ref[c * r:(c + 1) * r, :].astype(F32), jnp.zeros((FFN_HALO, ref.shape[1]), F32)], axis=0)
    return ref[c * r:(c + 1) * r + FFN_HALO, :].astype(F32)


def _ffn_mid_fwd(au, conv_w, conv_b, *, name):
    T = au.shape[0]
    tc = FFN_COL_TILE
    ncol = D_FF // tc
    r, nch = _ffn_chunks(T)

    def body(a_ref, u_ref, w_ref, b_ref, z_ref):
        w0, w1, w2, bias = w_ref[0:1, :], w_ref[1:2, :], w_ref[2:3, :], b_ref[...]
        for c in range(nch):
            ext = _ext_before(a_ref, c, r)
            pre = w0 * _shift_down(ext, 2, r) + w1 * _shift_down(ext, 1, r) + w2 * ext[FFN_HALO:] + bias
            z_ref[c * r:(c + 1) * r, :] = (_silu(pre) * u_ref[c * r:(c + 1) * r, :].astype(F32)).astype(z_ref.dtype)

    return pl.pallas_call(
        body, name=name, grid=(ncol,),
        in_specs=[pl.BlockSpec((T, tc), lambda j: (0, j)), pl.BlockSpec((T, tc), lambda j: (0, j + ncol)),
                  pl.BlockSpec((3, tc), lambda j: (0, j)), pl.BlockSpec((1, tc), lambda j: (0, j))],
        out_specs=pl.BlockSpec((T, tc), lambda j: (0, j)),
        out_shape=jax.ShapeDtypeStruct((T, D_FF), BF16),
        compiler_params=_cparams(("parallel",)),
    )(au, au, conv_w, conv_b)


def _ffn_mid_bwd(au, dz, conv_w, conv_b, *, name):
    T = au.shape[0]
    tc = FFN_COL_TILE
    ncol = D_FF // tc
    r, nch = _ffn_chunks(T)

    def body(a_ref, u_ref, dz_ref, w_ref, b_ref, da_ref, du_ref, dw_ref, db_ref, dpre_ref):
        w0, w1, w2, bias = w_ref[0:1, :], w_ref[1:2, :], w_ref[2:3, :], b_ref[...]
        dw0 = jnp.zeros((1, tc), F32)
        dw1 = jnp.zeros((1, tc), F32)
        dw2 = jnp.zeros((1, tc), F32)
        db = jnp.zeros((1, tc), F32)
        for c in range(nch):
            rows = slice(c * r, (c + 1) * r)
            ext = _ext_before(a_ref, c, r)
            a2, a1, a0 = _shift_down(ext, 2, r), _shift_down(ext, 1, r), ext[FFN_HALO:]
            pre = w0 * a2 + w1 * a1 + w2 * a0 + bias
            sg = _sigmoid(pre)
            act = pre * sg
            dzc = dz_ref[rows, :].astype(F32)
            du_ref[rows, :] = (dzc * act).astype(du_ref.dtype)
            dpre = dzc * u_ref[rows, :].astype(F32) * (sg * (1.0 + pre * (1.0 - sg)))
            dpre_ref[rows, :] = dpre
            dw0 += jnp.sum(dpre * a2, axis=0, keepdims=True)
            dw1 += jnp.sum(dpre * a1, axis=0, keepdims=True)
            dw2 += jnp.sum(dpre * a0, axis=0, keepdims=True)
            db += jnp.sum(dpre, axis=0, keepdims=True)
        for c in range(nch):
            ext = _ext_after(dpre_ref, c, r, nch)
            da = w0 * _shift_up(ext, 2, r) + w1 * _shift_up(ext, 1, r) + w2 * ext[:r]
            da_ref[c * r:(c + 1) * r, :] = da.astype(da_ref.dtype)
        dw_ref[0:1, :] = dw0
        dw_ref[1:2, :] = dw1
        dw_ref[2:3, :] = dw2
        db_ref[...] = db

    col = lambda j: (0, j)
    return pl.pallas_call(
        body, name=name, grid=(ncol,),
        in_specs=[pl.BlockSpec((T, tc), col), pl.BlockSpec((T, tc), lambda j: (0, j + ncol)), pl.BlockSpec((T, tc), col),
                  pl.BlockSpec((3, tc), col), pl.BlockSpec((1, tc), col)],
        out_specs=[pl.BlockSpec((T, tc), col), pl.BlockSpec((T, tc), col), pl.BlockSpec((3, tc), col), pl.BlockSpec((1, tc), col)],
        out_shape=[jax.ShapeDtypeStruct((T, D_FF), BF16), jax.ShapeDtypeStruct((T, D_FF), BF16),
                   jax.ShapeDtypeStruct((3, D_FF), F32), jax.ShapeDtypeStruct((1, D_FF), F32)],
        scratch_shapes=[pltpu.VMEM((T, tc), F32)],
        compiler_params=_cparams(("parallel",)),
    )(au, au, dz, conv_w, conv_b)


HGRN_BLOCK = 256


def _hgrn_chunk(dot, cumsum, q, f, i, g, lb, ng, st):
    C = q.shape[0]
    forget = lb + (1.0 - lb) * _sigmoid(f)
    k = 1.0 - forget
    b = cumsum(jnp.log(forget))
    b_last = b[C - 1:C, :]
    qd = q * jnp.exp(b)
    kd = k * jnp.exp(-b)
    att = jnp.where(_tri(C), dot(qd, kd, 1, 1), 0.0)
    o = dot(att, i, 1, 0) + dot(qd, st, 1, 1)
    st_new = st * jnp.exp(b_last) + dot(i, k * jnp.exp(b_last - b), 0, 0)
    on = o * lax.rsqrt(jnp.mean(o * o, axis=-1, keepdims=True) + EPS) * ng
    return on * _silu(g), st_new


def _hgrn_specs(T, rev):
    tb = min(HGRN_BLOCK, T)
    nb = T // tb
    blk = (lambda n: nb - 1 - n) if rev else (lambda n: n)
    hd = HGRN_HEAD_DIM
    proj_specs = [pl.BlockSpec((tb, HGRN_DIM), lambda n, k=k: (blk(n), k)) for k in range(4)]
    vec_spec = pl.BlockSpec((1, HGRN_DIM), lambda n: (0, 0))
    tok_spec = pl.BlockSpec((tb, HGRN_DIM), lambda n: (blk(n), 0))
    st_spec = pl.BlockSpec((HGRN_HEADS, tb // HGRN_CHUNK, hd, hd), lambda n: (0, blk(n), 0, 0))
    return tb, nb, proj_specs, vec_spec, tok_spec, st_spec


def _head_cols(h):
    return slice(h * HGRN_HEAD_DIM, (h + 1) * HGRN_HEAD_DIM)


def _hgrn_fwd(proj, lb, ng, *, name):
    T = proj.shape[0]
    tb, nb, proj_specs, vec_spec, tok_spec, st_spec = _hgrn_specs(T, False)
    nsub = tb // HGRN_CHUNK
    hd = HGRN_HEAD_DIM

    def body(q_ref, f_ref, i_ref, g_ref, lb_ref, ng_ref, y_ref, sts_ref, st_ref):
        @pl.when(pl.program_id(0) == 0)
        def _():
            st_ref[...] = jnp.zeros_like(st_ref)

        st = [st_ref[h] for h in range(HGRN_HEADS)]
        for s in range(nsub):
            rows = slice(s * HGRN_CHUNK, (s + 1) * HGRN_CHUNK)
            for h in range(HGRN_HEADS):
                cols = _head_cols(h)
                sts_ref[h, s] = st[h]
                y, st[h] = _hgrn_chunk(_mxu, _cumsum_rows, q_ref[rows, cols], f_ref[rows, cols], i_ref[rows, cols],
                                       g_ref[rows, cols], lb_ref[:, cols], ng_ref[:, cols], st[h])
                y_ref[rows, cols] = y
        for h in range(HGRN_HEADS):
            st_ref[h] = st[h]

    return pl.pallas_call(
        body, name=name, grid=(nb,),
        in_specs=proj_specs + [vec_spec, vec_spec], out_specs=[tok_spec, st_spec],
        out_shape=[jax.ShapeDtypeStruct((T, HGRN_DIM), F32),
                   jax.ShapeDtypeStruct((HGRN_HEADS, T // HGRN_CHUNK, hd, hd), F32)],
        scratch_shapes=[pltpu.VMEM((HGRN_HEADS, hd, hd), F32)],
        compiler_params=_cparams(("arbitrary",)),
    )(proj, proj, proj, proj, lb, ng)


def _hgrn_bwd(proj, lb, ng, states, dmix, *, name):
    T = proj.shape[0]
    tb, nb, proj_specs, vec_spec, tok_spec, st_spec = _hgrn_specs(T, True)
    nsub = tb // HGRN_CHUNK
    hd = HGRN_HEAD_DIM
    chunk = functools.partial(_hgrn_chunk, _mxu_ad, _cumsum_rows_ad)

    def body(q_ref, f_ref, i_ref, g_ref, lb_ref, ng_ref, sts_ref, dy_ref,
             dq_ref, df_ref, di_ref, dg_ref, dlb_ref, dng_ref, dst_ref):
        @pl.when(pl.program_id(0) == 0)
        def _():
            dst_ref[...] = jnp.zeros_like(dst_ref)
            dlb_ref[...] = jnp.zeros_like(dlb_ref)
            dng_ref[...] = jnp.zeros_like(dng_ref)

        dst = [dst_ref[h] for h in range(HGRN_HEADS)]
        dlb = [jnp.zeros((1, hd), F32)] * HGRN_HEADS
        dng = [jnp.zeros((1, hd), F32)] * HGRN_HEADS
        for s in reversed(range(nsub)):
            rows = slice(s * HGRN_CHUNK, (s + 1) * HGRN_CHUNK)
            for h in range(HGRN_HEADS):
                cols = _head_cols(h)
                _, vjp = jax.vjp(chunk, q_ref[rows, cols], f_ref[rows, cols], i_ref[rows, cols], g_ref[rows, cols],
                                 lb_ref[:, cols], ng_ref[:, cols], sts_ref[h, s])
                dq, df, di, dg, dlb_s, dng_s, dst[h] = vjp((dy_ref[rows, cols], dst[h]))
                dq_ref[rows, cols] = dq
                df_ref[rows, cols] = df
                di_ref[rows, cols] = di
                dg_ref[rows, cols] = dg
                dlb[h] = dlb[h] + dlb_s
                dng[h] = dng[h] + dng_s
        for h in range(HGRN_HEADS):
            dst_ref[h] = dst[h]
            dlb_ref[:, _head_cols(h)] += dlb[h]
            dng_ref[:, _head_cols(h)] += dng[h]

    tok_out = jax.ShapeDtypeStruct((T, HGRN_DIM), F32)
    vec_out = jax.ShapeDtypeStruct((1, HGRN_DIM), F32)
    return pl.pallas_call(
        body, name=name, grid=(nb,),
        in_specs=proj_specs + [vec_spec, vec_spec, st_spec, tok_spec],
        out_specs=[tok_spec] * 4 + [vec_spec, vec_spec],
        out_shape=[tok_out] * 4 + [vec_out, vec_out],
        scratch_shapes=[pltpu.VMEM((HGRN_HEADS, hd, hd), F32)],
        compiler_params=_cparams(("arbitrary",)),
    )(proj, proj, proj, proj, lb, ng, states, dmix)


S5_LANES = 512
S5_ROWS = 512


def _cmul(ar, ai, br, bi):
    return ar * br - ai * bi, ar * bi + ai * br


def _power_table(ar, ai, exps):
    a2 = _cmul(ar, ai, ar, ai)
    a4 = _cmul(*a2, *a2)
    e = exps - 1
    pr = jnp.broadcast_to(ar, exps.shape)
    pi = jnp.broadcast_to(ai, exps.shape)
    for bit, (fr, fi) in enumerate(((ar, ai), a2, a4)):
        nr, ni = _cmul(pr, pi, fr, fi)
        on = ((e >> bit) & 1) == 1
        pr, pi = jnp.where(on, nr, pr), jnp.where(on, ni, pi)
    return pr, pi, a2, a4


def _s5_scan_fwd(x, x_cols, b_re, b_im, a_re, a_im, *, name):
    T = x.shape[0]
    w, tr = S5_LANES, min(S5_ROWS, T)
    ncol, nt = S5_WIDTH // w, T // tr
    assert x_cols[0] % x_cols[1] == 0

    def body(x_ref, br_ref, bi_ref, ar_ref, ai_ref, sr_ref, si_ref, carry_ref):
        @pl.when(pl.program_id(1) == 0)
        def _():
            carry_ref[...] = jnp.zeros_like(carry_ref)

        u = x_ref[...].astype(_MXU_DTYPE)
        sr_ref[...] = _mxu(u, br_ref[...], 1, 0)
        si_ref[...] = _mxu(u, bi_ref[...], 1, 0)
        ar, ai = ar_ref[...], ai_ref[...]
        rowi = lax.broadcasted_iota(jnp.int32, (SUBLANE, w), 0)
        pr, pi, a2, a4 = _power_table(ar, ai, rowi + 1)
        steps = [(s, jnp.where(rowi >= s, fr, 0.0), jnp.where(rowi >= s, fi, 0.0)) for s, (fr, fi) in ((1, (ar, ai)), (2, a2), (4, a4))]

        def tile(i, carry):
            cr, ci = carry
            rows = pl.ds(pl.multiple_of(i * SUBLANE, SUBLANE), SUBLANE)
            xr, xi = sr_ref[rows, :], si_ref[rows, :]
            for s, fr, fi in steps:
                zr, zi = pltpu.roll(xr, s, 0), pltpu.roll(xi, s, 0)
                xr, xi = xr + fr * zr - fi * zi, xi + fr * zi + fi * zr
            xr, xi = xr + pr * cr - pi * ci, xi + pr * ci + pi * cr
            sr_ref[rows, :] = xr
            si_ref[rows, :] = xi
            return xr[SUBLANE - 1:SUBLANE, :], xi[SUBLANE - 1:SUBLANE, :]

        cr, ci = lax.fori_loop(0, tr // SUBLANE, tile, (carry_ref[0:1, :], carry_ref[1:2, :]))
        carry_ref[0:1, :] = cr
        carry_ref[1:2, :] = ci

    out = jax.ShapeDtypeStruct((T, S5_WIDTH), F32)
    return pl.pallas_call(
        body, name=name, grid=(ncol, nt),
        in_specs=[pl.BlockSpec((tr, x_cols[1]), lambda j, t: (t, x_cols[0] // x_cols[1])),
                  pl.BlockSpec((S5_DIM, w), lambda j, t: (0, j)), pl.BlockSpec((S5_DIM, w), lambda j, t: (0, j)),
                  pl.BlockSpec((1, w), lambda j, t: (0, j)), pl.BlockSpec((1, w), lambda j, t: (0, j))],
        out_specs=[pl.BlockSpec((tr, w), lambda j, t: (t, j))] * 2,
        out_shape=[out, out],
        scratch_shapes=[pltpu.VMEM((2, w), F32)],
        compiler_params=_cparams(("parallel", "arbitrary")),
    )(x, b_re, b_im, a_re, a_im)


def _s5_scan_bwd(dy, c_re, c_im, s_re, s_im, a_re, a_im, *, name):
    T = dy.shape[0]
    w, tr = S5_LANES, min(S5_ROWS, T)
    ncol, nt = S5_WIDTH // w, T // tr
    ntile = tr // SUBLANE

    def body(dy_ref, cr_ref, ci_ref, sr_ref, si_ref, ar_ref, ai_ref, lr_ref, li_ref, dar_ref, dai_ref, carry_ref):
        @pl.when(pl.program_id(1) == 0)
        def _():
            carry_ref[...] = jnp.zeros_like(carry_ref)
            dar_ref[...] = jnp.zeros_like(dar_ref)
            dai_ref[...] = jnp.zeros_like(dai_ref)

        dyb = dy_ref[...].astype(_MXU_DTYPE)
        lr_ref[...] = _mxu(dyb, cr_ref[...], 1, 1)
        li_ref[...] = _mxu(dyb, ci_ref[...], 1, 1)
        ar, ai = ar_ref[...], -ai_ref[...]
        rowi = lax.broadcasted_iota(jnp.int32, (SUBLANE, w), 0)
        pr, pi, a2, a4 = _power_table(ar, ai, SUBLANE - rowi)
        last = rowi == SUBLANE - 1
        steps = [(s, jnp.where(rowi < SUBLANE - s, fr, 0.0), jnp.where(rowi < SUBLANE - s, fi, 0.0))
                 for s, (fr, fi) in ((1, (ar, ai)), (2, a2), (4, a4))]

        def tile(i, carry):
            cr, ci, dar, dai = carry
            rows = pl.ds(pl.multiple_of((ntile - 1 - i) * SUBLANE, SUBLANE), SUBLANE)
            xr, xi = lr_ref[rows, :], li_ref[rows, :]
            for s, fr, fi in steps:
                zr, zi = pltpu.roll(xr, SUBLANE - s, 0), pltpu.roll(xi, SUBLANE - s, 0)
                xr, xi = xr + fr * zr - fi * zi, xi + fr * zi + fi * zr
            xr, xi = xr + pr * cr - pi * ci, xi + pr * ci + pi * cr
            lr_ref[rows, :] = xr
            li_ref[rows, :] = xi
            nr = jnp.where(last, cr, pltpu.roll(xr, SUBLANE - 1, 0))
            ni = jnp.where(last, ci, pltpu.roll(xi, SUBLANE - 1, 0))
            sr, si = sr_ref[rows, :], si_ref[rows, :]
            return xr[0:1, :], xi[0:1, :], dar + nr * sr + ni * si, dai + ni * sr - nr * si

        cr, ci, dar, dai = lax.fori_loop(
            0, ntile, tile, (carry_ref[0:1, :], carry_ref[1:2, :], jnp.zeros((SUBLANE, w), F32), jnp.zeros((SUBLANE, w), F32)))
        carry_ref[0:1, :] = cr
        carry_ref[1:2, :] = ci
        dar_ref[...] += dar
        dai_ref[...] += dai

    tok = pl.BlockSpec((tr, w), lambda j, t: (nt - 1 - t, j))
    vec = pl.BlockSpec((1, w), lambda j, t: (0, j))
    acc = pl.BlockSpec((SUBLANE, w), lambda j, t: (0, j))
    out = jax.ShapeDtypeStruct((T, S5_WIDTH), F32)
    accs = jax.ShapeDtypeStruct((SUBLANE, S5_WIDTH), F32)
    return pl.pallas_call(
        body, name=name, grid=(ncol, nt),
        in_specs=[pl.BlockSpec((tr, S5_DIM), lambda j, t: (nt - 1 - t, 0)),
                  pl.BlockSpec((w, S5_DIM), lambda j, t: (j, 0)), pl.BlockSpec((w, S5_DIM), lambda j, t: (j, 0)),
                  tok, tok, vec, vec],
        out_specs=[tok, tok, acc, acc],
        out_shape=[out, out, accs, accs],
        scratch_shapes=[pltpu.VMEM((2, w), F32)],
        compiler_params=_cparams(("parallel", "arbitrary")),
    )(dy, c_re, c_im, s_re, s_im, a_re, a_im)


ATTN_BLOCK = 512
_NEG = -1e30


def _qk_cat(nope, rope):
    return jnp.concatenate([nope.astype(_MXU_DTYPE), rope.astype(_MXU_DTYPE)], axis=1)


_QK_SCALE = MLA_QK ** -0.5
_LOG2E = math.log2(math.e)


def _attn_scores(q, k, diagonal):
    s = _mxu(q, k, 1, 1) * (_QK_SCALE * _LOG2E)
    if diagonal:
        s = jnp.where(_tri(s.shape[0]), s, _NEG)
    return s


def _attn_fwd(q_all, q_rope, kv, k_rope, *, name):
    T = q_all.shape[0]
    tq = min(ATTN_BLOCK, T)
    nq = T // tq

    def body(qn_ref, qr_ref, kn_ref, v_ref, kr_ref, o_ref, lse_ref):
        i = pl.program_id(1)
        q = _qk_cat(qn_ref[...], qr_ref[0])

        def step(j, carry, diagonal):
            m, l, acc = carry
            ks = pl.ds(pl.multiple_of(j * tq, tq), tq)
            s = _attn_scores(q, _qk_cat(kn_ref[ks, :], kr_ref[ks, :]), diagonal)
            m_new = jnp.maximum(m, jnp.max(s, axis=-1, keepdims=True))
            p = jnp.exp2(s - m_new)
            alpha = jnp.exp2(m - m_new)
            return m_new, alpha * l + jnp.sum(p, axis=-1, keepdims=True), alpha * acc + _mxu(p, v_ref[ks, :], 1, 0)

        init = (jnp.full((tq, 1), _NEG, F32), jnp.zeros((tq, 1), F32), jnp.zeros((tq, MLA_V), F32))
        below = lax.fori_loop(0, i, functools.partial(step, diagonal=False), init)
        m, l, acc = step(i, below, diagonal=True)
        o_ref[...] = acc / l
        lse_ref[0] = m + jnp.log2(l)

    return pl.pallas_call(
        body, name=name, grid=(MLA_HEADS, nq),
        in_specs=[pl.BlockSpec((tq, MLA_NOPE), lambda h, i: (i, h)), pl.BlockSpec((1, tq, MLA_ROPE), lambda h, i: (h, i, 0)),
                  pl.BlockSpec((T, MLA_NOPE), lambda h, i: (0, 2 * h)), pl.BlockSpec((T, MLA_V), lambda h, i: (0, 2 * h + 1)),
                  pl.BlockSpec((T, MLA_ROPE), lambda h, i: (0, 0))],
        out_specs=[pl.BlockSpec((tq, MLA_V), lambda h, i: (i, h)), pl.BlockSpec((1, tq, 1), lambda h, i: (h, i, 0))],
        out_shape=[jax.ShapeDtypeStruct((T, MLA_HEADS * MLA_V), F32), jax.ShapeDtypeStruct((MLA_HEADS, T, 1), F32)],
        compiler_params=_cparams(("arbitrary", "arbitrary")),
    )(q_all, q_rope, kv, kv, k_rope)


def _attn_bwd(q_all, q_rope, kv, k_rope, o, lse, do, *, name):
    T = q_all.shape[0]
    tk = min(ATTN_BLOCK, T)
    nk = T // tk

    def body(qn_ref, qr_ref, kv_ref, kr_ref, o_ref, lse_ref, do_ref, dqn_ref, dqr_ref, dkv_ref, dkr_ref, delta_ref):
        h, j = pl.program_id(0), pl.program_id(1)

        @pl.when(j == 0)
        def _():
            dqn_ref[...] = jnp.zeros_like(dqn_ref)
            dqr_ref[...] = jnp.zeros_like(dqr_ref)
            delta_ref[...] = jnp.sum(do_ref[...] * o_ref[...], axis=-1, keepdims=True)

        @pl.when((j == 0) & (h == 0))
        def _():
            dkr_ref[...] = jnp.zeros_like(dkr_ref)

        krows = pl.ds(pl.multiple_of(j * tk, tk), tk)
        k = _qk_cat(kv_ref[:, :MLA_NOPE], kr_ref[krows, :])
        v = kv_ref[:, MLA_NOPE:].astype(_MXU_DTYPE)

        def step(i, carry, diagonal):
            dk, dv = carry
            qs = pl.ds(pl.multiple_of(i * tk, tk), tk)
            q, dob = _qk_cat(qn_ref[qs, :], qr_ref[0, qs, :]), do_ref[qs, :].astype(_MXU_DTYPE)
            p = jnp.exp2(_attn_scores(q, k, diagonal) - lse_ref[0, qs, :])
            ds = p * (_mxu(dob, v, 1, 1) - delta_ref[qs, :]) * _QK_SCALE
            dq = _mxu(ds, k, 1, 0)
            dqn_ref[qs, :] += dq[:, :MLA_NOPE]
            dqr_ref[0, qs, :] += dq[:, MLA_NOPE:]
            return dk + _mxu(ds, q, 0, 0), dv + _mxu(p, dob, 0, 0)

        on_diagonal = step(j, (jnp.zeros((tk, MLA_QK), F32), jnp.zeros((tk, MLA_V), F32)), diagonal=True)
        dk, dv = lax.fori_loop(j + 1, nk, functools.partial(step, diagonal=False), on_diagonal)
        dkv_ref[:, :MLA_NOPE] = dk[:, :MLA_NOPE].astype(dkv_ref.dtype)
        dkv_ref[:, MLA_NOPE:] = dv.astype(dkv_ref.dtype)
        dkr_ref[krows, :] += dk[:, MLA_NOPE:]

    head_cols = pl.BlockSpec((T, MLA_NOPE), lambda h, j: (0, h))
    head_rope = pl.BlockSpec((1, T, MLA_ROPE), lambda h, j: (h, 0, 0))
    kv_spec = pl.BlockSpec((tk, MLA_NOPE + MLA_V), lambda h, j: (j, h))
    kr_spec = pl.BlockSpec((T, MLA_ROPE), lambda h, j: (0, 0))
    return pl.pallas_call(
        body, name=name, grid=(MLA_HEADS, nk),
        in_specs=[head_cols, head_rope, kv_spec, kr_spec, head_cols, pl.BlockSpec((1, T, 1), lambda h, j: (h, 0, 0)), head_cols],
        out_specs=[head_cols, head_rope, kv_spec, kr_spec],
        out_shape=[jax.ShapeDtypeStruct((T, MLA_HEADS * MLA_NOPE), F32), jax.ShapeDtypeStruct((MLA_HEADS, T, MLA_ROPE), F32),
                   jax.ShapeDtypeStruct((T, MLA_HEADS * (MLA_NOPE + MLA_V)), BF16), jax.ShapeDtypeStruct((T, MLA_ROPE), F32)],
        scratch_shapes=[pltpu.VMEM((T, 1), F32)],
        compiler_params=_cparams(("arbitrary", "arbitrary")),
    )(q_all, q_rope, kv, k_rope, o, lse, do)


def _s5_discretize(a_re, a_im, log_dt, bt_re, bt_im, lb_logits):
    dt = jnp.exp(log_dt)
    mag = jnp.exp(a_re * dt)
    abr, abi = mag * jnp.cos(a_im * dt), mag * jnp.sin(a_im * dt)
    den = a_re * a_re + a_im * a_im
    xr, xi = abr - 1.0, abi
    cr = ((xr * a_re + xi * a_im) / den)[:, None, :]
    ci = ((xi * a_re - xr * a_im) / den)[:, None, :]
    e = jnp.exp(lb_logits - jnp.max(lb_logits, axis=0, keepdims=True))
    lb = e[0:1, :] / jnp.sum(e, axis=0, keepdims=True)
    return abr, abi, cr * bt_re - ci * bt_im, cr * bt_im + ci * bt_re, lb


def _whole(shape):
    return pl.BlockSpec(shape, lambda: (0,) * len(shape))


def _s5_params_fwd(a_re, a_im, log_dt, bt_re, bt_im, lb_logits):
    ins = (a_re, a_im, log_dt, bt_re, bt_im, lb_logits)
    outs = [jax.ShapeDtypeStruct(s, F32) for s in (a_re.shape, a_re.shape, bt_re.shape, bt_re.shape, (1, lb_logits.shape[1]))]

    def body(*refs):
        res = _s5_discretize(*[r[...] for r in refs[:6]])
        for o_ref, o in zip(refs[6:], res):
            o_ref[...] = o

    return pl.pallas_call(body, name="s5_params_fwd", in_specs=[_whole(a.shape) for a in ins],
                          out_specs=[_whole(o.shape) for o in outs], out_shape=outs, compiler_params=_cparams())(*ins)


def _s5_params_bwd(a_re, a_im, log_dt, bt_re, bt_im, lb_logits, d_abr, d_abi, d_bbr, d_bbi, d_lb):
    ins = (a_re, a_im, log_dt, bt_re, bt_im, lb_logits, d_abr, d_abi, d_bbr, d_bbi, d_lb)
    outs = [jax.ShapeDtypeStruct(a.shape, F32) for a in ins[:6]]

    def body(*refs):
        _, vjp = jax.vjp(_s5_discretize, *[r[...] for r in refs[:6]])
        for o_ref, o in zip(refs[11:], vjp(tuple(r[...] for r in refs[6:11]))):
            o_ref[...] = o

    return pl.pallas_call(body, name="s5_params_bwd", in_specs=[_whole(a.shape) for a in ins],
                          out_specs=[_whole(o.shape) for o in outs], out_shape=outs, compiler_params=_cparams())(*ins)


ADAMW_WHOLE_BYTES = 1024 * 1024


def _adamw(w, g, m, v, *, name):
    R, C = w.shape
    whole = R % SUBLANE != 0 or R * C * w.dtype.itemsize <= ADAMW_WHOLE_BYTES
    tr = R if whole else _pick(R, (256, 128, 64, 32, 16, 8))
    slabs = g.shape[0] if g.ndim == 3 else 0

    def body(w_ref, g_ref, m_ref, v_ref, *outs):
        if slabs:
            gv = g_ref[0].astype(F32)
            for s in range(1, slabs):
                gv = gv + g_ref[s].astype(F32)
            outs[0][...] = gv
            outs = outs[1:]
        else:
            gv = g_ref[...]
        d_ref, mo_ref, vo_ref = outs
        m2 = ADAM_B1 * m_ref[...] + (1.0 - ADAM_B1) * gv
        v2 = ADAM_B2 * v_ref[...] + (1.0 - ADAM_B2) * (gv * gv)
        m_hat = m2 / (1.0 - ADAM_B1 ** ADAM_STEP)
        v_hat = v2 / (1.0 - ADAM_B2 ** ADAM_STEP)
        d_ref[...] = -ADAM_LR * (m_hat / (jnp.sqrt(v_hat) + ADAM_EPS) + ADAM_WD * w_ref[...])
        mo_ref[...] = m2
        vo_ref[...] = v2

    spec = pl.BlockSpec((tr, C), lambda i: (i, 0))
    g_spec = pl.BlockSpec((slabs, tr, C), lambda i: (0, i, 0)) if slabs else spec
    out = jax.ShapeDtypeStruct((R, C), F32)
    n_out = 4 if slabs else 3
    return pl.pallas_call(body, name=name, grid=(R // tr,), in_specs=[spec, g_spec, spec, spec], out_specs=[spec] * n_out,
                          out_shape=[out] * n_out, compiler_params=_cparams(("parallel",)))(w, g, m, v)


N_CHIPS = 4
N_CORES = 2


_FLIPS = tuple((dx, dy, dc) for dx in (0, 1) for dy in (0, 1) for dc in (0, 1) if (dx, dy, dc) != (0, 0, 0))


_HBM = pl.BlockSpec(memory_space=pltpu.HBM)
_SEM = pl.BlockSpec(memory_space=pltpu.SEMAPHORE)
_SPLIT_COPY = pltpu.CompilerParams(has_side_effects=pltpu.SideEffectType.DATAFLOW_SIDE_EFFECTING)


def _exchange_copies(src_refs, land_refs, send_sems, recv_sems, scatter, arriving):
    x, y, c = lax.axis_index("x"), lax.axis_index("y"), lax.axis_index("c")
    me_chip = 2 * x + y
    copies = []
    for a, (s_ref, l_ref) in enumerate(zip(src_refs, land_refs)):
        for j, (dx, dy, dc) in enumerate(_FLIPS):
            px, py, pc = (1 - x if dx else x), (1 - y if dy else y), (1 - c if dc else c)
            k = a * len(_FLIPS) + j
            p_chip = 2 * px + py
            copies.append(pltpu.make_async_remote_copy(
                src_ref=s_ref.at[p_chip, pc] if scatter else s_ref, dst_ref=l_ref.at[p_chip, pc] if arriving else l_ref.at[me_chip, c],
                send_sem=send_sems.at[k], recv_sem=recv_sems.at[k], device_id=(px, py, pc), device_id_type=MESH))
    return copies


def _exchange_start(srcs, *, scatter, name, after=()):
    n_arr = len(srcs)
    n_sem = n_arr * len(_FLIPS)
    n_in = 2 * n_arr + len(after)
    lands = [lax.empty(s.shape if scatter else (N_CHIPS, N_CORES) + s.shape, s.dtype) for s in srcs]

    def body(*refs):
        src_refs, land_refs = refs[:n_arr], refs[n_arr:2 * n_arr]
        for cp in _exchange_copies(src_refs, land_refs, refs[n_in], refs[n_in + 1], scatter, arriving=False):
            cp.start()
        refs[-1][...] = jnp.zeros_like(refs[-1])

    thru = [pltpu.HBM(a.shape, a.dtype) for a in srcs + lands]
    outs = pl.pallas_call(
        body, name=name,
        out_shape=(pltpu.SemaphoreType.DMA((n_sem,)), pltpu.SemaphoreType.DMA((n_sem,)), *thru,
                   jax.ShapeDtypeStruct((SUBLANE, LANE), F32)),
        in_specs=[_HBM] * (2 * n_arr) + [pl.BlockSpec(memory_space=pl.ANY)] * len(after),
        out_specs=(_SEM, _SEM, *[_HBM] * (2 * n_arr), pl.BlockSpec(memory_space=pltpu.VMEM)),
        input_output_aliases={i: 2 + i for i in range(2 * n_arr)}, compiler_params=_SPLIT_COPY,
    )(*[pltpu.with_memory_space_constraint(a, pltpu.HBM) for a in srcs + lands], *after)
    return outs[0], outs[1], list(outs[2:2 + n_arr]), list(outs[2 + n_arr:2 + 2 * n_arr]), outs[-1]


def _exchange_wait(started, after, *, scatter, name):
    send_sems, recv_sems, srcs, lands, _ = started
    n_arr = len(srcs)

    def body(*refs):
        src_refs, land_refs = refs[:n_arr], refs[n_arr:2 * n_arr]
        for cp in _exchange_copies(src_refs, land_refs, refs[2 * n_arr], refs[2 * n_arr + 1], scatter, arriving=True):
            cp.wait_send()
            cp.wait_recv()

    outs = pl.pallas_call(
        body, name=name, out_shape=[pltpu.HBM(a.shape, a.dtype) for a in srcs + lands],
        in_specs=[_HBM] * (2 * n_arr) + [_SEM, _SEM, pl.BlockSpec(memory_space=pl.ANY)], out_specs=[_HBM] * (2 * n_arr),
        input_output_aliases={i: i for i in range(2 * n_arr)}, compiler_params=_SPLIT_COPY,
    )(*srcs, *lands, send_sems, recv_sems, after)
    return list(outs[:n_arr]), list(outs[n_arr:])


def _with_own(land, own):
    me_chip = 2 * lax.axis_index("x") + lax.axis_index("y")
    return lax.dynamic_update_slice(land, own[None, None], (me_chip, lax.axis_index("c")) + (0,) * own.ndim)


def _rms_fwd_fn(h, g):
    return (_rms(h, g),)


def _rms_bwd_fn(h, dhn, dres, g):
    _, vjp = jax.vjp(_rms, h, g)
    dh, dg = vjp(dhn)
    return dh + dres, dg


def _loss_fn(h, tgt, g):
    y, vjp = jax.vjp(_rms, h, g)
    diff = y - tgt
    dh, dg = vjp(diff * (1.0 / D_MODEL))
    return dh, dg, (0.5 / D_MODEL) * jnp.sum(diff * diff, axis=0, keepdims=True)


def _s5_act(ys, u, d):
    return _gelu(ys + d * u)


def _s5_gate(z, gl, b):
    return z * _sigmoid(gl + b)


def _s5_act_fn(ys, u, d):
    return (_s5_act(ys, u, d),)


def _s5_mix_fn(ya, z, gl, b):
    return (jnp.concatenate([ya, _s5_gate(z, gl, b)], axis=1),)


def _s5_gate_bwd_fn(z, gl, dyb, b):
    _, vjp = jax.vjp(_s5_gate, z, gl, b)
    return vjp(dyb)


def _s5_act_bwd_fn(ys, u, dz1, dz2, d):
    _, vjp = jax.vjp(_s5_act, ys, u, d)
    return vjp(dz1 + dz2)


def _dproj_fn(dq, df, di, dg, du1, du2):
    return (jnp.concatenate([dq, df, di, dg, du1 + du2], axis=1),)


def _rope_pair(r1, r2, pos, freqs):
    ang = pos.astype(F32) * freqs
    c, s = jnp.cos(ang), jnp.sin(ang)
    return r1 * c - r2 * s, r1 * s + r2 * c


_ODD_SPLITS = (0, MLA_Q_RANK, MLA_Q_RANK + MLA_KV_RANK, MLA_Q_RANK + MLA_KV_RANK + LANE, ODD_IN_PAD)


def _mla_prep(cq, ckv, k1, k2, qg, kvg, pos, freqs):
    ko1, ko2 = _rope_pair(k1, k2, pos, freqs)
    return _rms(cq, qg), _rms(ckv, kvg), ko1, ko2


def _mla_prep_fn(proj, pos, qg, kvg, freqs):
    parts = [proj[:, a:b] for a, b in zip(_ODD_SPLITS[:-1], _ODD_SPLITS[1:])]
    return _mla_prep(*parts, qg, kvg, pos, freqs)


def _mla_prep_bwd_fn(proj, pos, dqn, dkvn, dko1, dko2, qg, kvg, freqs):
    parts = [proj[:, a:b] for a, b in zip(_ODD_SPLITS[:-1], _ODD_SPLITS[1:])]
    _, vjp = jax.vjp(lambda *a: _mla_prep(*a, pos, freqs), *parts, qg, kvg)
    dcq, dckv, dk1, dk2, dqg, dkvg = vjp((dqn, dkvn, dko1, dko2))
    return jnp.concatenate([dcq, dckv, dk1, dk2], axis=1), dqg, dkvg


def _rope_q_fn(r1, r2, pos, freqs):
    return _rope_pair(r1, r2, pos, freqs)


def _rope_q_bwd_fn(dqn, do1, do2, pos, freqs):
    dr1, dr2 = _rope_pair(do1, do2, pos, -freqs)
    return (jnp.concatenate([dqn, dr1, dr2], axis=1),)


W_NAMES = ("norm_mix_g", "norm_ffn_g", "final_norm_g", "even_w_in", "hgrn_lb_logits", "hgrn_norm_g", "s5_a_re", "s5_a_im",
           "s5_log_dt", "s5_b_re", "s5_b_im", "s5_c_re", "s5_c_im", "s5_d", "s5_w_glu", "s5_b_glu", "even_w_out", "odd_w_in",
           "mla_q_norm_g", "mla_w_uq", "mla_kv_norm_g", "mla_w_ukv", "odd_w_out", "ffn_w_in", "ffn_conv_w", "ffn_conv_b",
           "ffn_w_out")
BIG_UNITS = (("even_w_in", 0, "col"), ("s5_w_glu", 0, "row"), ("even_w_out", 0, "row"), ("odd_w_in", 0, "row"),
             ("mla_w_uq", 0, "col"), ("mla_w_ukv", 0, "col"), ("odd_w_out", 0, "row"),
             ("ffn_w_in", 0, "col"), ("ffn_w_in", 1, "col"), ("ffn_w_out", 0, "row"), ("ffn_w_out", 1, "row"))
BIG_NAMES = tuple(dict.fromkeys(u[0] for u in BIG_UNITS))
SMALL_SHARDED = (("mla_q_norm_g", 1), ("mla_kv_norm_g", 1), ("ffn_conv_w", 2))
SMALL_SHARDED_NAMES = tuple(s[0] for s in SMALL_SHARDED)
REPLICATED = tuple(n for n in W_NAMES if n not in BIG_NAMES + SMALL_SHARDED_NAMES)
REPLICATED_LATE = ("norm_mix_g",)
REPLICATED_EARLY = tuple(n for n in REPLICATED if n not in REPLICATED_LATE)


def _pack(flats, cols, row_mult):
    flat = jnp.concatenate(flats, axis=-1)
    pad = (-flat.shape[-1]) % (cols * row_mult)
    flat = jnp.pad(flat, [(0, 0)] * (flat.ndim - 1) + [(0, pad)])
    return flat.reshape(flat.shape[:-1] + (-1, cols))


def _unpack(flat, shapes):
    out, off = [], 0
    for shp in shapes:
        n = int(np.prod(shp))
        out.append(flat[..., off:off + n].reshape(flat.shape[:-1] + tuple(shp)))
        off += n
    return out


UNIT_KIND = {(n, l): kind for n, l, kind in BIG_UNITS}
STAGES = ((("even_w_in", 0),),
          (("s5_w_glu", 0), ("even_w_out", 0)),
          (("ffn_w_in", 0), ("ffn_w_out", 0)),
          (("odd_w_in", 0), ("mla_w_uq", 0), ("mla_w_ukv", 0), ("odd_w_out", 0)),
          (("ffn_w_in", 1), ("ffn_w_out", 1)))


def _gather_start(w, stage, with_small, after):
    srcs = [w[n][l].astype(BF16) for n, l in STAGES[stage]]
    if with_small:
        srcs.append(_pack([w[n].reshape(-1) for n in SMALL_SHARDED_NAMES], LANE, SUBLANE))
    return _exchange_start(srcs, scatter=False, name=f"gather_start_{stage}", after=after)


def _gather_finish(started, after, w, stage, with_small):
    srcs, lands = _exchange_wait(started, after, scatter=False, name=f"gather_wait_{stage}")
    lands = [_with_own(land, src) for land, src in zip(lands, srcs)]
    big = {}
    for unit, g in zip(STAGES[stage], lands):
        r, c = g.shape[2:]
        big[unit] = g.reshape(N_DEV * r, c) if UNIT_KIND[unit] == "row" else g.transpose(2, 0, 1, 3).reshape(r, N_DEV * c)
    if not with_small:
        return big
    parts = _unpack(lands[-1].reshape(N_DEV, -1), [w[n].shape for n in SMALL_SHARDED_NAMES])
    small = {}
    for (n, ax), p in zip(SMALL_SHARDED, parts):
        shp = list(w[n].shape)
        shp[ax] *= N_DEV
        small[n] = jnp.moveaxis(p, 0, ax).reshape(shp)
    return big, small


def _scatter_start(g_big, stage, extra=()):
    srcs = []
    for unit in STAGES[stage]:
        g = g_big[unit].astype(BF16)
        if UNIT_KIND[unit] == "row":
            srcs.append(g.reshape(N_CHIPS, N_CORES, g.shape[0] // N_DEV, g.shape[1]))
        else:
            srcs.append(g.reshape(g.shape[0], N_CHIPS, N_CORES, g.shape[1] // N_DEV).transpose(1, 2, 0, 3))
    return _exchange_start(srcs + list(extra), scatter=True, name=f"scatter_start_{stage}")


def _scatter_finish(started, after, stage):
    srcs, lands = _exchange_wait(started, after, scatter=True, name=f"scatter_wait_{stage}")
    me_chip, c = 2 * lax.axis_index("x") + lax.axis_index("y"), lax.axis_index("c")
    outs = []
    for land, src in zip(lands, srcs):
        own = lax.dynamic_slice(src, (me_chip, c) + (0,) * (src.ndim - 2), (1, 1) + src.shape[2:])[0, 0]
        outs.append(_with_own(land, own).reshape((N_DEV,) + land.shape[2:]))
    return outs


def _small_sharded_pack(g_small, w):
    flats = []
    for n, ax in SMALL_SHARDED:
        shp = list(w[n].shape)
        g = g_small[n].astype(F32).reshape(shp[:ax] + [N_DEV] + shp[ax:])
        flats.append(jnp.moveaxis(g, ax, 0).reshape(N_DEV, -1))
    small = _pack(flats, LANE, SUBLANE)
    return small.reshape((N_CHIPS, N_CORES) + small.shape[1:])


def _replicated_pack(g_repl, names):
    vec = _pack([g_repl[n].reshape(-1).astype(F32) for n in names], LANE, SUBLANE)
    return jnp.broadcast_to(vec, (N_CHIPS, N_CORES) + vec.shape)


def _block_diag(blocks):
    G, a, b = blocks.shape
    return jnp.einsum('gab,gk->gakb', blocks, jnp.eye(G, dtype=blocks.dtype)).reshape(G * a, G * b)


def _diag_blocks(mat, a, b):
    G = mat.shape[0] // a
    return jnp.einsum('gagb->gab', mat.reshape(G, a, G, b))


def _ffn_fwd(h, g, w_in, conv_w, conv_b, w_out, tag):
    hn, = _rows(_rms_fwd_fn, [h], [g], [(D_MODEL, BF16)], [], name=f"ffn{tag}_norm")
    au = _mm(hn, w_in, out_dtype=BF16, name=f"ffn{tag}_in")
    z = _ffn_mid_fwd(au, conv_w, conv_b, name=f"ffn{tag}_mid")
    return _mm(z, w_out, res=h, name=f"ffn{tag}_out"), (hn, au, z)


def _ffn_bwd(h, dh, saved, g, w_in, conv_w, conv_b, w_out, tag, deps=()):
    hn, au, z = saved
    dz = _mm(dh, w_out, tb=True, out_dtype=BF16, deps=deps, name=f"ffn{tag}_dz")
    dw_out = _mm(z, dh, ta=True, out_dtype=BF16, name=f"ffn{tag}_dwout")
    da, du, dcw, dcb = _ffn_mid_bwd(au, dz, conv_w, conv_b, name=f"ffn{tag}_dmid")
    dhn = _mm(da, w_in, tb=True, b_cols=(0, D_FF), name=f"ffn{tag}_dhn_a")
    dhn = _mm(du, w_in, tb=True, b_cols=(D_FF, D_FF), res=dhn, name=f"ffn{tag}_dhn_u")
    dw_in = jnp.concatenate([_mm(hn, da, ta=True, out_dtype=BF16, name=f"ffn{tag}_dwin_a"),
                             _mm(hn, du, ta=True, out_dtype=BF16, name=f"ffn{tag}_dwin_u")], axis=1)
    dh_in, dg = _rows(_rms_bwd_fn, [h, dhn, dh], [g], [(D_MODEL, F32)], [(1, D_MODEL)], name=f"ffn{tag}_dnorm")
    return dh_in, dict(g=dg, w_in=dw_in, conv_w=dcw, conv_b=dcb, w_out=dw_out)


def kernel(x, positions, norm_mix_g, norm_ffn_g, final_norm_g, even_w_in, hgrn_lb_logits, hgrn_norm_g, s5_a_re, s5_a_im, s5_log_dt, s5_b_re, s5_b_im, s5_c_re, s5_c_im, s5_d, s5_w_glu, s5_b_glu, even_w_out, odd_w_in, mla_q_norm_g, mla_w_uq, mla_kv_norm_g, mla_w_ukv, odd_w_out, ffn_w_in, ffn_conv_w, ffn_conv_b, ffn_w_out, loss_target, m_norm_mix_g, m_norm_ffn_g, m_final_norm_g, m_even_w_in, m_hgrn_lb_logits, m_hgrn_norm_g, m_s5_a_re, m_s5_a_im, m_s5_log_dt, m_s5_b_re, m_s5_b_im, m_s5_c_re, m_s5_c_im, m_s5_d, m_s5_w_glu, m_s5_b_glu, m_even_w_out, m_odd_w_in, m_mla_q_norm_g, m_mla_w_uq, m_mla_kv_norm_g, m_mla_w_ukv, m_odd_w_out, m_ffn_w_in, m_ffn_conv_w, m_ffn_conv_b, m_ffn_w_out, v_norm_mix_g, v_norm_ffn_g, v_final_norm_g, v_even_w_in, v_hgrn_lb_logits, v_hgrn_norm_g, v_s5_a_re, v_s5_a_im, v_s5_log_dt, v_s5_b_re, v_s5_b_im, v_s5_c_re, v_s5_c_im, v_s5_d, v_s5_w_glu, v_s5_b_glu, v_even_w_out, v_odd_w_in, v_mla_q_norm_g, v_mla_w_uq, v_mla_kv_norm_g, v_mla_w_ukv, v_odd_w_out, v_ffn_w_in, v_ffn_conv_w, v_ffn_conv_b, v_ffn_w_out):
    given = dict(locals())
    w = {n: given[n] for n in W_NAMES}
    mom = {n: given["m_" + n] for n in W_NAMES}
    var = {n: given["v_" + n] for n in W_NAMES}
    T = x.shape[1]
    h0 = x[0]
    tgt = loss_target[0]
    pos = positions.reshape(T, 1)

    gathers = []
    for s in range(len(STAGES)):
        gathers.append(_gather_start(w, s, with_small=(s == 1), after=[g[4] for g in gathers[-1:]]))
    half = MLA_ROPE // 2
    kr0 = MLA_Q_RANK + MLA_KV_RANK
    freqs = ROPE_THETA ** (-jnp.arange(0, MLA_ROPE, 2, dtype=F32) / MLA_ROPE)
    freqs_q = jnp.tile(freqs, MLA_HEADS)[None, :]
    freqs_k = jnp.concatenate([freqs, jnp.zeros((LANE - half,), F32)])[None, :]

    sp_in = (s5_a_re[0], s5_a_im[0], s5_log_dt[0][:, None], s5_b_re[0].transpose(0, 2, 1), s5_b_im[0].transpose(0, 2, 1),
             hgrn_lb_logits)
    abr, abi, bbt_re, bbt_im, lb0 = _s5_params_fwd(*sp_in)
    a_re, a_im = abr.reshape(1, S5_WIDTH), abi.reshape(1, S5_WIDTH)
    bb_re, bb_im = _block_diag(bbt_re).astype(BF16), _block_diag(bbt_im).astype(BF16)
    c_re = _block_diag(s5_c_re[0].transpose(0, 2, 1)).astype(BF16)
    c_im_neg = _block_diag(-s5_c_im[0].transpose(0, 2, 1)).astype(BF16)
    u_cols = (4 * HGRN_DIM, S5_DIM)

    hn0, = _rows(_rms_fwd_fn, [h0], [norm_mix_g[0:1]], [(D_MODEL, BF16)], [], name="mix0_norm", deps=[gathers[-1][4]])
    full = _gather_finish(gathers[0], hn0, w, 0, False)
    w_ein = full["even_w_in", 0]
    proj = _mm(hn0, w_ein, name="even_in")
    y_a, states = _hgrn_fwd(proj, lb0, hgrn_norm_g, name="hgrn_fwd")
    s_re, s_im = _s5_scan_fwd(proj, u_cols, bb_re, bb_im, a_re, a_im, name="s5_scan_fwd")
    more, full_small = _gather_finish(gathers[1], s_re, w, 1, True)
    w_glu, w_eout = more["s5_w_glu", 0], more["even_w_out", 0]
    qg, kvg, conv_w = full_small["mla_q_norm_g"], full_small["mla_kv_norm_g"], full_small["ffn_conv_w"]
    ys = _mm(s_im, c_im_neg, res=_mm(s_re, c_re, name="s5_y_re"), name="s5_y_im")
    z5, = _rows(_s5_act_fn, [ys, (proj,) + u_cols], [s5_d], [(S5_DIM, F32)], [], name="s5_act")
    gl = _mm(z5, w_glu, name="s5_glu")
    mixin, = _rows(_s5_mix_fn, [y_a, z5, gl], [s5_b_glu], [(D_MODEL, BF16)], [], name="s5_mix")
    h1 = _mm(mixin, w_eout, res=h0, name="even_out")
    full.update(_gather_finish(gathers[2], h1, w, 2, False))
    w_fin, w_fout = [full["ffn_w_in", 0]], [full["ffn_w_out", 0]]
    h2, ffn0_saved = _ffn_fwd(h1, norm_ffn_g[0:1], w_fin[0], conv_w[0], ffn_conv_b[0:1], w_fout[0], 0)

    full.update(_gather_finish(gathers[3], h2, w, 3, False))
    w_oin, w_ukv, w_oout = full["odd_w_in", 0], full["mla_w_ukv", 0], full["odd_w_out", 0]
    zpad = jnp.zeros((D_MODEL, LANE - half), BF16)
    w_oin_pad = jnp.concatenate([w_oin[:, :kr0], w_oin[:, kr0:kr0 + half], zpad, w_oin[:, kr0 + half:], zpad], axis=1)
    w_uq3 = full["mla_w_uq", 0].reshape(MLA_Q_RANK, MLA_HEADS, MLA_QK)
    w_uq_perm = jnp.concatenate([w_uq3[:, :, :MLA_NOPE].reshape(MLA_Q_RANK, -1),
                                 w_uq3[:, :, MLA_NOPE:MLA_NOPE + half].reshape(MLA_Q_RANK, -1),
                                 w_uq3[:, :, MLA_NOPE + half:].reshape(MLA_Q_RANK, -1)], axis=1)
    hn1, = _rows(_rms_fwd_fn, [h2], [norm_mix_g[1:2]], [(D_MODEL, BF16)], [], name="mix1_norm")
    proj_o = _mm(hn1, w_oin_pad, name="odd_in")
    qn, kvn, ko1, ko2 = _rows(_mla_prep_fn, [proj_o, pos], [qg, kvg, freqs_k],
                              [(MLA_Q_RANK, BF16), (MLA_KV_RANK, BF16), (LANE, F32), (LANE, F32)], [], name="mla_prep")
    q_all = _mm(qn, w_uq_perm, name="mla_uq")
    kv = _mm(kvn, w_ukv, out_dtype=BF16, name="mla_ukv")
    nope_w = MLA_HEADS * MLA_NOPE
    rope_w = MLA_HEADS * half
    o1, o2 = _rows(_rope_q_fn, [(q_all, nope_w, rope_w), (q_all, nope_w + rope_w, rope_w), pos], [freqs_q],
                   [(rope_w, F32), (rope_w, F32)], [], name="mla_rope_q")
    q_rope = jnp.concatenate([o1.reshape(T, MLA_HEADS, half), o2.reshape(T, MLA_HEADS, half)], axis=2).transpose(1, 0, 2)
    k_rope = jnp.concatenate([ko1[:, :half], ko2[:, :half]], axis=1)
    o, lse = _attn_fwd(q_all, q_rope, kv, k_rope, name="attn_fwd")
    h3 = _mm(o, w_oout, res=h2, name="odd_out")
    full.update(_gather_finish(gathers[4], h3, w, 4, False))
    w_fin.append(full["ffn_w_in", 1])
    w_fout.append(full["ffn_w_out", 1])
    h4, ffn1_saved = _ffn_fwd(h3, norm_ffn_g[1:2], w_fin[1], conv_w[1], ffn_conv_b[1:2], w_fout[1], 1)

    dh4, d_final_g, loss_cols = _rows(_loss_fn, [h4, tgt], [final_norm_g[None, :]], [(D_MODEL, F32)],
                                      [(1, D_MODEL), (1, D_MODEL)], name="loss_head")
    loss = lax.psum(jnp.sum(loss_cols), ("x", "y", "c"))

    dh3, gf1 = _ffn_bwd(h3, dh4, ffn1_saved, norm_ffn_g[1:2], w_fin[1], conv_w[1], ffn_conv_b[1:2], w_fout[1], 1)
    scatters = {4: _scatter_start({("ffn_w_in", 1): gf1["w_in"], ("ffn_w_out", 1): gf1["w_out"]}, 4)}
    do = _mm(dh3, w_oout, tb=True, deps=[scatters[4][4]], name="odd_out_dx")
    d_w_oout = _mm(o, dh3, ta=True, out_dtype=BF16, name="odd_out_dw")
    dq_nope, dq_rope, dkv, dk_rope = _attn_bwd(q_all, q_rope, kv, k_rope, o, lse, do, name="attn_bwd")
    dq_rope_t = dq_rope.transpose(1, 0, 2)
    do1, do2 = dq_rope_t[:, :, :half].reshape(T, rope_w), dq_rope_t[:, :, half:].reshape(T, rope_w)
    lane_pad = ((0, 0), (0, LANE - half))
    dko1, dko2 = jnp.pad(dk_rope[:, :half], lane_pad), jnp.pad(dk_rope[:, half:], lane_pad)
    dq_all, = _rows(_rope_q_bwd_fn, [dq_nope, do1, do2, pos], [freqs_q], [(MLA_HEADS * MLA_QK, BF16)], [], name="mla_rope_q_bwd")
    d_w_uq_perm = _mm(qn, dq_all, ta=True, out_dtype=BF16, name="mla_uq_dw")
    dqn = _mm(dq_all, w_uq_perm, tb=True, name="mla_uq_dx")
    d_w_ukv = _mm(kvn, dkv, ta=True, out_dtype=BF16, name="mla_ukv_dw")
    dkvn = _mm(dkv, w_ukv, tb=True, name="mla_ukv_dx")
    dproj_o, d_qg, d_kvg = _rows(_mla_prep_bwd_fn, [proj_o, pos, dqn, dkvn, dko1, dko2], [qg, kvg, freqs_k],
                                 [(ODD_IN_PAD, BF16)], [(1, MLA_Q_RANK), (1, MLA_KV_RANK)], name="mla_prep_bwd")
    d_w_oin_pad = _mm(hn1, dproj_o, ta=True, out_dtype=BF16, name="odd_in_dw")
    dhn1 = _mm(dproj_o, w_oin_pad, tb=True, name="odd_in_dx")
    dh2, d_mix_g1 = _rows(_rms_bwd_fn, [h2, dhn1, dh3], [norm_mix_g[1:2]], [(D_MODEL, F32)], [(1, D_MODEL)], name="mix1_dnorm")
    d_w_oin = jnp.concatenate([d_w_oin_pad[:, :kr0 + half], d_w_oin_pad[:, kr0 + LANE:kr0 + LANE + half]], axis=1)
    d3 = d_w_uq_perm
    d_w_uq = jnp.concatenate([d3[:, :nope_w].reshape(MLA_Q_RANK, MLA_HEADS, MLA_NOPE),
                              d3[:, nope_w:nope_w + rope_w].reshape(MLA_Q_RANK, MLA_HEADS, half),
                              d3[:, nope_w + rope_w:].reshape(MLA_Q_RANK, MLA_HEADS, half)], axis=2).reshape(MLA_Q_RANK, -1)
    scatters[3] = _scatter_start({("odd_w_in", 0): d_w_oin, ("mla_w_uq", 0): d_w_uq, ("mla_w_ukv", 0): d_w_ukv,
                                  ("odd_w_out", 0): d_w_oout}, 3)

    dh1, gf0 = _ffn_bwd(h1, dh2, ffn0_saved, norm_ffn_g[0:1], w_fin[0], conv_w[0], ffn_conv_b[0:1], w_fout[0], 0,
                        deps=[scatters[3][4]])
    scatters[2] = _scatter_start({("ffn_w_in", 0): gf0["w_in"], ("ffn_w_out", 0): gf0["w_out"]}, 2)
    dmix = _mm(dh1, w_eout, tb=True, deps=[scatters[2][4]], name="even_out_dx")
    d_w_eout = _mm(mixin, dh1, ta=True, out_dtype=BF16, name="even_out_dw")
    dq, df, di, dg, d_lb0, d_hgrn_g = _hgrn_bwd(proj, lb0, hgrn_norm_g, states, dmix, name="hgrn_bwd")
    dz1, dgl, d_b_glu = _rows(_s5_gate_bwd_fn, [z5, gl, (dmix, HGRN_DIM, S5_DIM)], [s5_b_glu],
                              [(S5_DIM, F32), (S5_DIM, BF16)], [(1, S5_DIM)], name="s5_gate_bwd")
    dz2 = _mm(dgl, w_glu, tb=True, name="s5_glu_dx")
    d_w_glu = _mm(z5, dgl, ta=True, out_dtype=BF16, name="s5_glu_dw")
    dys, du1, d_s5_d = _rows(_s5_act_bwd_fn, [ys, (proj,) + u_cols, dz1, dz2], [s5_d],
                             [(S5_DIM, BF16), (S5_DIM, F32)], [(1, S5_DIM)], name="s5_act_bwd")
    d_c_re = _mm(s_re, dys, ta=True, name="s5_dc_re")
    d_c_im_neg = _mm(s_im, dys, ta=True, name="s5_dc_im")
    lam_re, lam_im, d_ar, d_ai = _s5_scan_bwd(dys, c_re, c_im_neg, s_re, s_im, a_re, a_im, name="s5_scan_bwd")
    du2 = _mm(lam_im, bb_im, tb=True, res=_mm(lam_re, bb_re, tb=True, name="s5_du_re"), name="s5_du_im")
    d_bb_re = _mm(proj, lam_re, ta=True, a_cols=u_cols, name="s5_dbb_re")
    d_bb_im = _mm(proj, lam_im, ta=True, a_cols=u_cols, name="s5_dbb_im")
    sp_g = _s5_params_bwd(*sp_in, d_ar.sum(0).reshape(S5_GROUPS, S5_STATE), d_ai.sum(0).reshape(S5_GROUPS, S5_STATE),
                          _diag_blocks(d_bb_re, S5_GROUP, S5_STATE), _diag_blocks(d_bb_im, S5_GROUP, S5_STATE), d_lb0)
    d_a_re, d_a_im, d_log_dt, d_bt_re, d_bt_im, d_lb_logits = sp_g
    g_small = dict(mla_q_norm_g=d_qg, mla_kv_norm_g=d_kvg, ffn_conv_w=jnp.stack([gf0["conv_w"], gf1["conv_w"]]))
    g_repl = dict(
        norm_ffn_g=jnp.concatenate([gf0["g"], gf1["g"]]),
        final_norm_g=d_final_g[0], hgrn_lb_logits=d_lb_logits, hgrn_norm_g=d_hgrn_g,
        s5_a_re=d_a_re[None], s5_a_im=d_a_im[None], s5_log_dt=d_log_dt[:, 0][None],
        s5_b_re=d_bt_re.transpose(0, 2, 1)[None], s5_b_im=d_bt_im.transpose(0, 2, 1)[None],
        s5_c_re=_diag_blocks(d_c_re, S5_STATE, S5_GROUP).transpose(0, 2, 1)[None],
        s5_c_im=-_diag_blocks(d_c_im_neg, S5_STATE, S5_GROUP).transpose(0, 2, 1)[None],
        s5_d=d_s5_d, s5_b_glu=d_b_glu, ffn_conv_b=jnp.concatenate([gf0["conv_b"], gf1["conv_b"]]))
    scatters[1] = _scatter_start({("s5_w_glu", 0): d_w_glu, ("even_w_out", 0): d_w_eout}, 1,
                                 extra=[_small_sharded_pack(g_small, w), _replicated_pack(g_repl, REPLICATED_EARLY)])
    dproj, = _rows(_dproj_fn, [dq, df, di, dg, du1, du2], [], [(EVEN_IN, BF16)], [], name="even_dproj", deps=[scatters[1][4]])
    d_w_ein = _mm(hn0, dproj, ta=True, out_dtype=BF16, name="even_in_dw")
    dhn0 = _mm(dproj, w_ein, tb=True, name="even_in_dx")
    grad_x, d_mix_g0 = _rows(_rms_bwd_fn, [h0, dhn0, dh1], [norm_mix_g[0:1]], [(D_MODEL, F32)], [(1, D_MODEL)], name="mix0_dnorm")
    g_repl["norm_mix_g"] = jnp.concatenate([d_mix_g0, d_mix_g1])
    scatters[0] = _scatter_start({("even_w_in", 0): d_w_ein}, 0, extra=[_replicated_pack(g_repl, REPLICATED_LATE)])

    delta, new_m, new_v = {}, {}, {}
    per_unit, partial = {}, {}
    after = scatters[0][4]
    for stage in (4, 3, 2, 1, 0):
        partial[stage] = _scatter_finish(scatters[stage], after, stage)
        for (n, l), slabs in zip(STAGES[stage], partial[stage]):
            per_unit[n, l] = _adamw(w[n][l], slabs, mom[n][l], var[n][l], name=f"adamw_{n}_{l}")
        after = per_unit[STAGES[stage][-1]][0]
    grads = {}
    for n in BIG_NAMES:
        layers = [per_unit[n, l] for l in range(w[n].shape[0])]
        grads[n], delta[n], new_m[n], new_v[n] = (jnp.stack([lay[k] for lay in layers]) for k in range(4))
    for names, slabs, tag in ((REPLICATED_EARLY, partial[1][-1], "repl"), (REPLICATED_LATE, partial[0][-1], "late"),
                              (SMALL_SHARDED_NAMES, partial[1][-2], "small")):
        packs = [_pack([t[n].reshape(-1) for n in names], LANE, SUBLANE) for t in (w, mom, var)]
        outs = _adamw(packs[0], slabs, packs[1], packs[2], name=f"adamw_{tag}")
        shapes = [w[n].shape for n in names]
        for dst, o_ in zip((grads, delta, new_m, new_v), outs):
            dst.update(zip(names, _unpack(o_.reshape(-1), shapes)))

    return (loss, grad_x[None], *[grads[n] for n in W_NAMES], *[delta[n] for n in W_NAMES],
            *[new_m[n] for n in W_NAMES], *[new_v[n] for n in W_NAMES])
```

```python
import functools
import math

import numpy as np
import jax
import jax.numpy as jnp
from jax import lax
from jax.experimental import pallas as pl
from jax.experimental.pallas import tpu as pltpu

F32 = jnp.float32
BF16 = jnp.bfloat16
_MXU_DTYPE = jnp.bfloat16

D_MODEL = 1024
HGRN_DIM = 512
HGRN_HEAD_DIM = 128
HGRN_HEADS = 4
HGRN_CHUNK = 64
S5_DIM = 512
S5_GROUPS = 32
S5_GROUP = 16
S5_STATE = 64
S5_WIDTH = S5_GROUPS * S5_STATE
EVEN_IN = 4 * HGRN_DIM + S5_DIM
MLA_HEADS = 8
MLA_Q_RANK = 384
MLA_KV_RANK = 256
MLA_NOPE = 128
MLA_ROPE = 64
MLA_V = 128
MLA_QK = MLA_NOPE + MLA_ROPE
ODD_IN = MLA_Q_RANK + MLA_KV_RANK + MLA_ROPE
ODD_IN_PAD = MLA_Q_RANK + MLA_KV_RANK + 2 * 128
ROPE_THETA = 10000.0
D_FF = 2816
EPS = 1e-6
ADAM_LR = 0.001
ADAM_B1 = 0.9
ADAM_B2 = 0.999
ADAM_EPS = 1e-08
ADAM_WD = 0.01
ADAM_STEP = 10

N_DEV = 8
LANE = 128
SUBLANE = 8
VMEM_LIMIT_BYTES = 56 * 1024 * 1024
MESH = pl.DeviceIdType.MESH


def _cparams(sem=None):
    return pltpu.CompilerParams(dimension_semantics=sem, vmem_limit_bytes=VMEM_LIMIT_BYTES)


def _pick(n, cands):
    for c in cands:
        if n % c == 0:
            return c
    raise ValueError(f"no tile for {n} in {cands}")


def _sigmoid(x):
    return 0.5 * jnp.tanh(0.5 * x) + 0.5


def _silu(x):
    return x * _sigmoid(x)


def _gelu(x):
    return 0.5 * x * (1.0 + jnp.tanh(math.sqrt(2.0 / math.pi) * (x + 0.044715 * (x * x * x))))


def _rms(x, g):
    return x * lax.rsqrt(jnp.mean(x * x, axis=-1, keepdims=True) + EPS) * g


def _mxu(a, b, ca, cb):
    return lax.dot_general(a.astype(_MXU_DTYPE), b.astype(_MXU_DTYPE), (((ca,), (cb,)), ((), ())),
                           preferred_element_type=F32)


@functools.partial(jax.custom_vjp, nondiff_argnums=(2, 3))
def _mxu_ad(a, b, ca, cb):
    return _mxu(a, b, ca, cb)


def _mxu_ad_fwd(a, b, ca, cb):
    return _mxu(a, b, ca, cb), (a, b)


def _mxu_ad_bwd(ca, cb, saved, g):
    a, b = saved
    fa, fb = 1 - ca, 1 - cb
    da = _mxu(g, b, 1, fb) if ca == 1 else _mxu(b, g, fb, 1)
    db = _mxu(a, g, fa, 0) if cb == 0 else _mxu(g, a, 0, fa)
    return da, db


_mxu_ad.defvjp(_mxu_ad_fwd, _mxu_ad_bwd)


def _tri(n):
    row = lax.broadcasted_iota(jnp.int32, (n, n), 0)
    col = lax.broadcasted_iota(jnp.int32, (n, n), 1)
    return col <= row


def _cumsum_rows(x, reverse=False):
    n = x.shape[0]
    rowi = lax.broadcasted_iota(jnp.int32, x.shape, 0)
    s = 1
    while s < n:
        if reverse:
            x = x + jnp.where(rowi < n - s, pltpu.roll(x, n - s, 0), 0.0)
        else:
            x = x + jnp.where(rowi >= s, pltpu.roll(x, s, 0), 0.0)
        s *= 2
    return x


@jax.custom_vjp
def _cumsum_rows_ad(x):
    return _cumsum_rows(x)


def _cumsum_rows_ad_fwd(x):
    return _cumsum_rows(x), None


def _cumsum_rows_ad_bwd(_, g):
    return (_cumsum_rows(g, reverse=True),)


_cumsum_rows_ad.defvjp(_cumsum_rows_ad_fwd, _cumsum_rows_ad_bwd)


MM_VMEM_BUDGET = 36 * 1024 * 1024
MM_MAX_TILE = 1408


def _lane_divisors(n, cap, offs=()):
    return [d for d in range(min(n, cap) // LANE * LANE, 0, -LANE) if n % d == 0 and all(o % d == 0 for o in offs)]


def _mm_tiles(M, N, K, sa, sb, so, has_res, m_offs, n_offs, k_offs):
    best = None
    for tm in _lane_divisors(M, MM_MAX_TILE, m_offs):
        for tn in _lane_divisors(N, MM_MAX_TILE, n_offs):
            for tk in _lane_divisors(K, K, k_offs):
                nk = K // tk
                vmem = 2 * (tm * tk * sa + tk * tn * sb + tm * tn * so + tm * tn * 4 * has_res) + (tm * tn * 4 if nk > 1 else 0)
                if vmem <= MM_VMEM_BUDGET:
                    key = (-nk, tm * tn, tn)
                    if best is None or key > best[0]:
                        best = (key, tm, tn, tk)
                    break
    return best[1:]


def _mm(a, b, *, ta=False, tb=False, res=None, out_dtype=F32, a_cols=None, b_cols=None, deps=(), name):
    a_minor = a.shape[1] if a_cols is None else a_cols[1]
    b_minor = b.shape[1] if b_cols is None else b_cols[1]
    K, M = (a.shape[0], a_minor) if ta else (a_minor, a.shape[0])
    N = b.shape[0] if tb else b_minor
    assert (b_minor if tb else b.shape[0]) == K, (a.shape, b.shape, ta, tb)
    a_off = 0 if a_cols is None else a_cols[0]
    b_off = 0 if b_cols is None else b_cols[0]
    has_res = res is not None
    tm, tn, tk = _mm_tiles(M, N, K, a.dtype.itemsize, b.dtype.itemsize, jnp.dtype(out_dtype).itemsize, has_res,
                           (a_off,) if ta else (), () if tb else (b_off,), ((a_off,) if not ta else ()) + ((b_off,) if tb else ()))
    nk = K // tk
    am, ak = (a_off // tm, 0) if ta else (0, a_off // tk)
    bn, bk = (0, b_off // tk) if tb else (b_off // tn, 0)
    a_spec = pl.BlockSpec((tk, tm), lambda i, j, k: (k, i + am)) if ta else pl.BlockSpec((tm, tk), lambda i, j, k: (i, k + ak))
    b_spec = pl.BlockSpec((tn, tk), lambda i, j, k: (j, k + bk)) if tb else pl.BlockSpec((tk, tn), lambda i, j, k: (k, j + bn))
    o_spec = pl.BlockSpec((tm, tn), lambda i, j, k: (i, j))
    ca, cb = (0 if ta else 1), (1 if tb else 0)

    n_fixed = 2 + has_res + len(deps)

    def body(*refs):
        a_ref, b_ref = refs[0], refs[1]
        res_ref = refs[2] if has_res else None
        o_ref = refs[n_fixed]
        part = _mxu(a_ref[...], b_ref[...], ca, cb)
        if nk == 1:
            o_ref[...] = (part + res_ref[...] if has_res else part).astype(out_dtype)
            return
        acc_ref = refs[n_fixed + 1]
        k = pl.program_id(2)

        @pl.when(k == 0)
        def _():
            acc_ref[...] = part

        @pl.when(k > 0)
        def _():
            acc_ref[...] += part

        @pl.when(k == nk - 1)
        def _():
            o_ref[...] = (acc_ref[...] + res_ref[...] if has_res else acc_ref[...]).astype(out_dtype)

    ins = [a, b] + ([res] if has_res else []) + list(deps)
    in_specs = [a_spec, b_spec] + ([o_spec] if has_res else []) + [pl.BlockSpec(memory_space=pl.ANY)] * len(deps)
    return pl.pallas_call(
        body, name=name, grid=(M // tm, N // tn, nk),
        in_specs=in_specs, out_specs=o_spec,
        out_shape=jax.ShapeDtypeStruct((M, N), out_dtype),
        scratch_shapes=[pltpu.VMEM((tm, tn), F32)] if nk > 1 else [],
        compiler_params=_cparams(("parallel", "parallel", "arbitrary")),
    )(*ins)


def _rows(fn, row_ins, const_ins, row_outs, acc_outs, *, name, tm=512, deps=()):
    norm = [(r, 0, r.shape[1]) if not isinstance(r, tuple) else r for r in row_ins]
    T = norm[0][0].shape[0]
    tm = min(tm, T)
    nr, nc, no, na = len(norm), len(const_ins), len(row_outs), len(acc_outs)
    first_out = nr + nc + len(deps)

    def body(*refs):
        i = pl.program_id(0)
        vals = [r[...] for r in refs[:nr + nc]]
        outs = fn(*vals)
        for o_ref, o in zip(refs[first_out:first_out + no], outs[:no]):
            o_ref[...] = o.astype(o_ref.dtype)
        for a_ref, o in zip(refs[first_out + no:], outs[no:]):
            @pl.when(i == 0)
            def _(a_ref=a_ref, o=o):
                a_ref[...] = o

            @pl.when(i > 0)
            def _(a_ref=a_ref, o=o):
                a_ref[...] += o

    in_specs = []
    for arr, off, w in norm:
        assert off % w == 0, (off, w)
        in_specs.append(pl.BlockSpec((tm, w), lambda i, b=off // w: (i, b)))
    for c in const_ins:
        in_specs.append(pl.BlockSpec(c.shape, lambda i: (0, 0)))
    in_specs += [pl.BlockSpec(memory_space=pl.ANY)] * len(deps)
    out_specs = [pl.BlockSpec((tm, w), lambda i: (i, 0)) for w, _ in row_outs]
    out_specs += [pl.BlockSpec(s, lambda i: (0, 0)) for s in acc_outs]
    out_shape = [jax.ShapeDtypeStruct((T, w), dt) for w, dt in row_outs]
    out_shape += [jax.ShapeDtypeStruct(s, F32) for s in acc_outs]
    return pl.pallas_call(
        body, name=name, grid=(T // tm,), in_specs=in_specs, out_specs=out_specs, out_shape=out_shape,
        compiler_params=_cparams(("arbitrary",)),
    )(*[n[0] for n in norm], *const_ins, *deps)


FFN_COL_TILE = LANE
FFN_ROW_CHUNK = 512


FFN_HALO = 2 * SUBLANE


def _shift_down(ext, s, rows):
    return pltpu.roll(ext, s, 0)[FFN_HALO:FFN_HALO + rows]


def _shift_up(ext, s, rows):
    return pltpu.roll(ext, rows + FFN_HALO - s, 0)[:rows]


def _ffn_chunks(T):
    r = min(FFN_ROW_CHUNK, T)
    return r, T // r


def _ext_before(ref, c, r):
    if c == 0:
        return jnp.concatenate([jnp.zeros((FFN_HALO, ref.shape[1]), F32), ref[0:r, :].astype(F32)], axis=0)
    return ref[c * r - FFN_HALO:(c + 1) * r, :].astype(F32)


def _ext_after(ref, c, r, nch):
    if c == nch - 1:
        return jnp.concatenate([ref[c * r:(c + 1) * r, :].astype(F32), jnp.zeros((FFN_HALO, ref.shape[1]), F32)], axis=0)
    return ref[c * r:(c + 1) * r + FFN_HALO, :].astype(F32)


def _ffn_mid_fwd(au, conv_w, conv_b, *, name):
    T = au.shape[0]
    tc = FFN_COL_TILE
    ncol = D_FF // tc
    r, nch = _ffn_chunks(T)

    def body(a_ref, u_ref, w_ref, b_ref, z_ref):
        w0, w1, w2, bias = w_ref[0:1, :], w_ref[1:2, :], w_ref[2:3, :], b_ref[...]
        for c in range(nch):
            ext = _ext_before(a_ref, c, r)
            pre = w0 * _shift_down(ext, 2, r) + w1 * _shift_down(ext, 1, r) + w2 * ext[FFN_HALO:] + bias
            z_ref[c * r:(c + 1) * r, :] = (_silu(pre) * u_ref[c * r:(c + 1) * r, :].astype(F32)).astype(z_ref.dtype)

    return pl.pallas_call(
        body, name=name, grid=(ncol,),
        in_specs=[pl.BlockSpec((T, tc), lambda j: (0, j)), pl.BlockSpec((T, tc), lambda j: (0, j + ncol)),
                  pl.BlockSpec((3, tc), lambda j: (0, j)), pl.BlockSpec((1, tc), lambda j: (0, j))],
        out_specs=pl.BlockSpec((T, tc), lambda j: (0, j)),
        out_shape=jax.ShapeDtypeStruct((T, D_FF), BF16),
        compiler_params=_cparams(("parallel",)),
    )(au, au, conv_w, conv_b)


def _ffn_mid_bwd(au, dz, conv_w, conv_b, *, name):
    T = au.shape[0]
    tc = FFN_COL_TILE
    ncol = D_FF // tc
    r, nch = _ffn_chunks(T)

    def body(a_ref, u_ref, dz_ref, w_ref, b_ref, da_ref, du_ref, dw_ref, db_ref, dpre_ref):
        w0, w1, w2, bias = w_ref[0:1, :], w_ref[1:2, :], w_ref[2:3, :], b_ref[...]
        dw0 = jnp.zeros((1, tc), F32)
        dw1 = jnp.zeros((1, tc), F32)
        dw2 = jnp.zeros((1, tc), F32)
        db = jnp.zeros((1, tc), F32)
        for c in range(nch):
            rows = slice(c * r, (c + 1) * r)
            ext = _ext_before(a_ref, c, r)
            a2, a1, a0 = _shift_down(ext, 2, r), _shift_down(ext, 1, r), ext[FFN_HALO:]
            pre = w0 * a2 + w1 * a1 + w2 * a0 + bias
            sg = _sigmoid(pre)
            act = pre * sg
            dzc = dz_ref[rows, :].astype(F32)
            du_ref[rows, :] = (dzc * act).astype(du_ref.dtype)
            dpre = dzc * u_ref[rows, :].astype(F32) * (sg * (1.0 + pre * (1.0 - sg)))
            dpre_ref[rows, :] = dpre
            dw0 += jnp.sum(dpre * a2, axis=0, keepdims=True)
            dw1 += jnp.sum(dpre * a1, axis=0, keepdims=True)
            dw2 += jnp.sum(dpre * a0, axis=0, keepdims=True)
            db += jnp.sum(dpre, axis=0, keepdims=True)
        for c in range(nch):
            ext = _ext_after(dpre_ref, c, r, nch)
            da = w0 * _shift_up(ext, 2, r) + w1 * _shift_up(ext, 1, r) + w2 * ext[:r]
            da_ref[c * r:(c + 1) * r, :] = da.astype(da_ref.dtype)
        dw_ref[0:1, :] = dw0
        dw_ref[1:2, :] = dw1
        dw_ref[2:3, :] = dw2
        db_ref[...] = db

    col = lambda j: (0, j)
    return pl.pallas_call(
        body, name=name, grid=(ncol,),
        in_specs=[pl.BlockSpec((T, tc), col), pl.BlockSpec((T, tc), lambda j: (0, j + ncol)), pl.BlockSpec((T, tc), col),
                  pl.BlockSpec((3, tc), col), pl.BlockSpec((1, tc), col)],
        out_specs=[pl.BlockSpec((T, tc), col), pl.BlockSpec((T, tc), col), pl.BlockSpec((3, tc), col), pl.BlockSpec((1, tc), col)],
        out_shape=[jax.ShapeDtypeStruct((T, D_FF), BF16), jax.ShapeDtypeStruct((T, D_FF), BF16),
                   jax.ShapeDtypeStruct((3, D_FF), F32), jax.ShapeDtypeStruct((1, D_FF), F32)],
        scratch_shapes=[pltpu.VMEM((T, tc), F32)],
        compiler_params=_cparams(("parallel",)),
    )(au, au, dz, conv_w, conv_b)


HGRN_BLOCK = 256


def _hgrn_chunk(dot, cumsum, q, f, i, g, lb, ng, st):
    C = q.shape[0]
    forget = lb + (1.0 - lb) * _sigmoid(f)
    k = 1.0 - forget
    b = cumsum(jnp.log(forget))
    b_last = b[C - 1:C, :]
    qd = q * jnp.exp(b)
    kd = k * jnp.exp(-b)
    att = jnp.where(_tri(C), dot(qd, kd, 1, 1), 0.0)
    o = dot(att, i, 1, 0) + dot(qd, st, 1, 1)
    st_new = st * jnp.exp(b_last) + dot(i, k * jnp.exp(b_last - b), 0, 0)
    on = o * lax.rsqrt(jnp.mean(o * o, axis=-1, keepdims=True) + EPS) * ng
    return on * _silu(g), st_new


def _hgrn_specs(T, rev):
    tb = min(HGRN_BLOCK, T)
    nb = T // tb
    blk = (lambda n: nb - 1 - n) if rev else (lambda n: n)
    hd = HGRN_HEAD_DIM
    proj_specs = [pl.BlockSpec((tb, HGRN_DIM), lambda n, k=k: (blk(n), k)) for k in range(4)]
    vec_spec = pl.BlockSpec((1, HGRN_DIM), lambda n: (0, 0))
    tok_spec = pl.BlockSpec((tb, HGRN_DIM), lambda n: (blk(n), 0))
    st_spec = pl.BlockSpec((HGRN_HEADS, tb // HGRN_CHUNK, hd, hd), lambda n: (0, blk(n), 0, 0))
    return tb, nb, proj_specs, vec_spec, tok_spec, st_spec


def _head_cols(h):
    return slice(h * HGRN_HEAD_DIM, (h + 1) * HGRN_HEAD_DIM)


def _hgrn_fwd(proj, lb, ng, *, name):
    T = proj.shape[0]
    tb, nb, proj_specs, vec_spec, tok_spec, st_spec = _hgrn_specs(T, False)
    nsub = tb // HGRN_CHUNK
    hd = HGRN_HEAD_DIM

    def body(q_ref, f_ref, i_ref, g_ref, lb_ref, ng_ref, y_ref, sts_ref, st_ref):
        @pl.when(pl.program_id(0) == 0)
        def _():
            st_ref[...] = jnp.zeros_like(st_ref)

        st = [st_ref[h] for h in range(HGRN_HEADS)]
        for s in range(nsub):
            rows = slice(s * HGRN_CHUNK, (s + 1) * HGRN_CHUNK)
            for h in range(HGRN_HEADS):
                cols = _head_cols(h)
                sts_ref[h, s] = st[h]
                y, st[h] = _hgrn_chunk(_mxu, _cumsum_rows, q_ref[rows, cols], f_ref[rows, cols], i_ref[rows, cols],
                                       g_ref[rows, cols], lb_ref[:, cols], ng_ref[:, cols], st[h])
                y_ref[rows, cols] = y
        for h in range(HGRN_HEADS):
            st_ref[h] = st[h]

    return pl.pallas_call(
        body, name=name, grid=(nb,),
        in_specs=proj_specs + [vec_spec, vec_spec], out_specs=[tok_spec, st_spec],
        out_shape=[jax.ShapeDtypeStruct((T, HGRN_DIM), F32),
                   jax.ShapeDtypeStruct((HGRN_HEADS, T // HGRN_CHUNK, hd, hd), F32)],
        scratch_shapes=[pltpu.VMEM((HGRN_HEADS, hd, hd), F32)],
        compiler_params=_cparams(("arbitrary",)),
    )(proj, proj, proj, proj, lb, ng)


def _hgrn_bwd(proj, lb, ng, states, dmix, *, name):
    T = proj.shape[0]
    tb, nb, proj_specs, vec_spec, tok_spec, st_spec = _hgrn_specs(T, True)
    nsub = tb // HGRN_CHUNK
    hd = HGRN_HEAD_DIM
    chunk = functools.partial(_hgrn_chunk, _mxu_ad, _cumsum_rows_ad)

    def body(q_ref, f_ref, i_ref, g_ref, lb_ref, ng_ref, sts_ref, dy_ref,
             dq_ref, df_ref, di_ref, dg_ref, dlb_ref, dng_ref, dst_ref):
        @pl.when(pl.program_id(0) == 0)
        def _():
            dst_ref[...] = jnp.zeros_like(dst_ref)
            dlb_ref[...] = jnp.zeros_like(dlb_ref)
            dng_ref[...] = jnp.zeros_like(dng_ref)

        dst = [dst_ref[h] for h in range(HGRN_HEADS)]
        dlb = [jnp.zeros((1, hd), F32)] * HGRN_HEADS
        dng = [jnp.zeros((1, hd), F32)] * HGRN_HEADS
        for s in reversed(range(nsub)):
            rows = slice(s * HGRN_CHUNK, (s + 1) * HGRN_CHUNK)
            for h in range(HGRN_HEADS):
                cols = _head_cols(h)
                _, vjp = jax.vjp(chunk, q_ref[rows, cols], f_ref[rows, cols], i_ref[rows, cols], g_ref[rows, cols],
                                 lb_ref[:, cols], ng_ref[:, cols], sts_ref[h, s])
                dq, df, di, dg, dlb_s, dng_s, dst[h] = vjp((dy_ref[rows, cols], dst[h]))
                dq_ref[rows, cols] = dq
                df_ref[rows, cols] = df
                di_ref[rows, cols] = di
                dg_ref[rows, cols] = dg
                dlb[h] = dlb[h] + dlb_s
                dng[h] = dng[h] + dng_s
        for h in range(HGRN_HEADS):
            dst_ref[h] = dst[h]
            dlb_ref[:, _head_cols(h)] += dlb[h]
            dng_ref[:, _head_cols(h)] += dng[h]

    tok_out = jax.ShapeDtypeStruct((T, HGRN_DIM), F32)
    vec_out = jax.ShapeDtypeStruct((1, HGRN_DIM), F32)
    return pl.pallas_call(
        body, name=name, grid=(nb,),
        in_specs=proj_specs + [vec_spec, vec_spec, st_spec, tok_spec],
        out_specs=[tok_spec] * 4 + [vec_spec, vec_spec],
        out_shape=[tok_out] * 4 + [vec_out, vec_out],
        scratch_shapes=[pltpu.VMEM((HGRN_HEADS, hd, hd), F32)],
        compiler_params=_cparams(("arbitrary",)),
    )(proj, proj, proj, proj, lb, ng, states, dmix)


S5_LANES = 512
S5_ROWS = 512


def _cmul(ar, ai, br, bi):
    return ar * br - ai * bi, ar * bi + ai * br


def _power_table(ar, ai, exps):
    a2 = _cmul(ar, ai, ar, ai)
    a4 = _cmul(*a2, *a2)
    e = exps - 1
    pr = jnp.broadcast_to(ar, exps.shape)
    pi = jnp.broadcast_to(ai, exps.shape)
    for bit, (fr, fi) in enumerate(((ar, ai), a2, a4)):
        nr, ni = _cmul(pr, pi, fr, fi)
        on = ((e >> bit) & 1) == 1
        pr, pi = jnp.where(on, nr, pr), jnp.where(on, ni, pi)
    return pr, pi, a2, a4


def _s5_scan_fwd(x, x_cols, b_re, b_im, a_re, a_im, *, name):
    T = x.shape[0]
    w, tr = S5_LANES, min(S5_ROWS, T)
    ncol, nt = S5_WIDTH // w, T // tr
    assert x_cols[0] % x_cols[1] == 0

    def body(x_ref, br_ref, bi_ref, ar_ref, ai_ref, sr_ref, si_ref, carry_ref):
        @pl.when(pl.program_id(1) == 0)
        def _():
            carry_ref[...] = jnp.zeros_like(carry_ref)

        u = x_ref[...].astype(_MXU_DTYPE)
        sr_ref[...] = _mxu(u, br_ref[...], 1, 0)
        si_ref[...] = _mxu(u, bi_ref[...], 1, 0)
        ar, ai = ar_ref[...], ai_ref[...]
        rowi = lax.broadcasted_iota(jnp.int32, (SUBLANE, w), 0)
        pr, pi, a2, a4 = _power_table(ar, ai, rowi + 1)
        steps = [(s, jnp.where(rowi >= s, fr, 0.0), jnp.where(rowi >= s, fi, 0.0)) for s, (fr, fi) in ((1, (ar, ai)), (2, a2), (4, a4))]

        def tile(i, carry):
            cr, ci = carry
            rows = pl.ds(pl.multiple_of(i * SUBLANE, SUBLANE), SUBLANE)
            xr, xi = sr_ref[rows, :], si_ref[rows, :]
            for s, fr, fi in steps:
                zr, zi = pltpu.roll(xr, s, 0), pltpu.roll(xi, s, 0)
                xr, xi = xr + fr * zr - fi * zi, xi + fr * zi + fi * zr
            xr, xi = xr + pr * cr - pi * ci, xi + pr * ci + pi * cr
            sr_ref[rows, :] = xr
            si_ref[rows, :] = xi
            return xr[SUBLANE - 1:SUBLANE, :], xi[SUBLANE - 1:SUBLANE, :]

        cr, ci = lax.fori_loop(0, tr // SUBLANE, tile, (carry_ref[0:1, :], carry_ref[1:2, :]))
        carry_ref[0:1, :] = cr
        carry_ref[1:2, :] = ci

    out = jax.ShapeDtypeStruct((T, S5_WIDTH), F32)
    return pl.pallas_call(
        body, name=name, grid=(ncol, nt),
        in_specs=[pl.BlockSpec((tr, x_cols[1]), lambda j, t: (t, x_cols[0] // x_cols[1])),
                  pl.BlockSpec((S5_DIM, w), lambda j, t: (0, j)), pl.BlockSpec((S5_DIM, w), lambda j, t: (0, j)),
                  pl.BlockSpec((1, w), lambda j, t: (0, j)), pl.BlockSpec((1, w), lambda j, t: (0, j))],
        out_specs=[pl.BlockSpec((tr, w), lambda j, t: (t, j))] * 2,
        out_shape=[out, out],
        scratch_shapes=[pltpu.VMEM((2, w), F32)],
        compiler_params=_cparams(("parallel", "arbitrary")),
    )(x, b_re, b_im, a_re, a_im)


def _s5_scan_bwd(dy, c_re, c_im, s_re, s_im, a_re, a_im, *, name):
    T = dy.shape[0]
    w, tr = S5_LANES, min(S5_ROWS, T)
    ncol, nt = S5_WIDTH // w, T // tr
    ntile = tr // SUBLANE

    def body(dy_ref, cr_ref, ci_ref, sr_ref, si_ref, ar_ref, ai_ref, lr_ref, li_ref, dar_ref, dai_ref, carry_ref):
        @pl.when(pl.program_id(1) == 0)
        def _():
            carry_ref[...] = jnp.zeros_like(carry_ref)
            dar_ref[...] = jnp.zeros_like(dar_ref)
            dai_ref[...] = jnp.zeros_like(dai_ref)

        dyb = dy_ref[...].astype(_MXU_DTYPE)
        lr_ref[...] = _mxu(dyb, cr_ref[...], 1, 1)
        li_ref[...] = _mxu(dyb, ci_ref[...], 1, 1)
        ar, ai = ar_ref[...], -ai_ref[...]
        rowi = lax.broadcasted_iota(jnp.int32, (SUBLANE, w), 0)
        pr, pi, a2, a4 = _power_table(ar, ai, SUBLANE - rowi)
        last = rowi == SUBLANE - 1
        steps = [(s, jnp.where(rowi < SUBLANE - s, fr, 0.0), jnp.where(rowi < SUBLANE - s, fi, 0.0))
                 for s, (fr, fi) in ((1, (ar, ai)), (2, a2), (4, a4))]

        def tile(i, carry):
            cr, ci, dar, dai = carry
            rows = pl.ds(pl.multiple_of((ntile - 1 - i) * SUBLANE, SUBLANE), SUBLANE)
            xr, xi = lr_ref[rows, :], li_ref[rows, :]
            for s, fr, fi in steps:
                zr, zi = pltpu.roll(xr, SUBLANE - s, 0), pltpu.roll(xi, SUBLANE - s, 0)
                xr, xi = xr + fr * zr - fi * zi, xi + fr * zi + fi * zr
            xr, xi = xr + pr * cr - pi * ci, xi + pr * ci + pi * cr
            lr_ref[rows, :] = xr
            li_ref[rows, :] = xi
            nr = jnp.where(last, cr, pltpu.roll(xr, SUBLANE - 1, 0))
            ni = jnp.where(last, ci, pltpu.roll(xi, SUBLANE - 1, 0))
            sr, si = sr_ref[rows, :], si_ref[rows, :]
            return xr[0:1, :], xi[0:1, :], dar + nr * sr + ni * si, dai + ni * sr - nr * si

        cr, ci, dar, dai = lax.fori_loop(
            0, ntile, tile, (carry_ref[0:1, :], carry_ref[1:2, :], jnp.zeros((SUBLANE, w), F32), jnp.zeros((SUBLANE, w), F32)))
        carry_ref[0:1, :] = cr
        carry_ref[1:2, :] = ci
        dar_ref[...] += dar
        dai_ref[...] += dai

    tok = pl.BlockSpec((tr, w), lambda j, t: (nt - 1 - t, j))
    vec = pl.BlockSpec((1, w), lambda j, t: (0, j))
    acc = pl.BlockSpec((SUBLANE, w), lambda j, t: (0, j))
    out = jax.ShapeDtypeStruct((T, S5_WIDTH), F32)
    accs = jax.ShapeDtypeStruct((SUBLANE, S5_WIDTH), F32)
    return pl.pallas_call(
        body, name=name, grid=(ncol, nt),
        in_specs=[pl.BlockSpec((tr, S5_DIM), lambda j, t: (nt - 1 - t, 0)),
                  pl.BlockSpec((w, S5_DIM), lambda j, t: (j, 0)), pl.BlockSpec((w, S5_DIM), lambda j, t: (j, 0)),
                  tok, tok, vec, vec],
        out_specs=[tok, tok, acc, acc],
        out_shape=[out, out, accs, accs],
        scratch_shapes=[pltpu.VMEM((2, w), F32)],
        compiler_params=_cparams(("parallel", "arbitrary")),
    )(dy, c_re, c_im, s_re, s_im, a_re, a_im)


ATTN_BLOCK = 512
_NEG = -1e30


def _qk_cat(nope, rope):
    return jnp.concatenate([nope.astype(_MXU_DTYPE), rope.astype(_MXU_DTYPE)], axis=1)


_QK_SCALE = MLA_QK ** -0.5
_LOG2E = math.log2(math.e)


def _attn_scores(q, k, diagonal):
    s = _mxu(q, k, 1, 1) * (_QK_SCALE * _LOG2E)
    if diagonal:
        s = jnp.where(_tri(s.shape[0]), s, _NEG)
    return s


def _attn_fwd(q_all, q_rope, kv, k_rope, *, name):
    T = q_all.shape[0]
    tq = min(ATTN_BLOCK, T)
    nq = T // tq

    def body(qn_ref, qr_ref, kn_ref, v_ref, kr_ref, o_ref, lse_ref):
        i = pl.program_id(1)
        q = _qk_cat(qn_ref[...], qr_ref[0])

        def step(j, carry, diagonal):
            m, l, acc = carry
            ks = pl.ds(pl.multiple_of(j * tq, tq), tq)
            s = _attn_scores(q, _qk_cat(kn_ref[ks, :], kr_ref[ks, :]), diagonal)
            m_new = jnp.maximum(m, jnp.max(s, axis=-1, keepdims=True))
            p = jnp.exp2(s - m_new)
            alpha = jnp.exp2(m - m_new)
            return m_new, alpha * l + jnp.sum(p, axis=-1, keepdims=True), alpha * acc + _mxu(p, v_ref[ks, :], 1, 0)

        init = (jnp.full((tq, 1), _NEG, F32), jnp.zeros((tq, 1), F32), jnp.zeros((tq, MLA_V), F32))
        below = lax.fori_loop(0, i, functools.partial(step, diagonal=False), init)
        m, l, acc = step(i, below, diagonal=True)
        o_ref[...] = acc / l
        lse_ref[0] = m + jnp.log2(l)

    return pl.pallas_call(
        body, name=name, grid=(MLA_HEADS, nq),
        in_specs=[pl.BlockSpec((tq, MLA_NOPE), lambda h, i: (i, h)), pl.BlockSpec((1, tq, MLA_ROPE), lambda h, i: (h, i, 0)),
                  pl.BlockSpec((T, MLA_NOPE), lambda h, i: (0, 2 * h)), pl.BlockSpec((T, MLA_V), lambda h, i: (0, 2 * h + 1)),
                  pl.BlockSpec((T, MLA_ROPE), lambda h, i: (0, 0))],
        out_specs=[pl.BlockSpec((tq, MLA_V), lambda h, i: (i, h)), pl.BlockSpec((1, tq, 1), lambda h, i: (h, i, 0))],
        out_shape=[jax.ShapeDtypeStruct((T, MLA_HEADS * MLA_V), F32), jax.ShapeDtypeStruct((MLA_HEADS, T, 1), F32)],
        compiler_params=_cparams(("arbitrary", "arbitrary")),
    )(q_all, q_rope, kv, kv, k_rope)


def _attn_bwd(q_all, q_rope, kv, k_rope, o, lse, do, *, name):
    T = q_all.shape[0]
    tk = min(ATTN_BLOCK, T)
    nk = T // tk

    def body(qn_ref, qr_ref, kv_ref, kr_ref, o_ref, lse_ref, do_ref, dqn_ref, dqr_ref, dkv_ref, dkr_ref, delta_ref):
        h, j = pl.program_id(0), pl.program_id(1)

        @pl.when(j == 0)
        def _():
            dqn_ref[...] = jnp.zeros_like(dqn_ref)
            dqr_ref[...] = jnp.zeros_like(dqr_ref)
            delta_ref[...] = jnp.sum(do_ref[...] * o_ref[...], axis=-1, keepdims=True)

        @pl.when((j == 0) & (h == 0))
        def _():
            dkr_ref[...] = jnp.zeros_like(dkr_ref)

        krows = pl.ds(pl.multiple_of(j * tk, tk), tk)
        k = _qk_cat(kv_ref[:, :MLA_NOPE], kr_ref[krows, :])
        v = kv_ref[:, MLA_NOPE:].astype(_MXU_DTYPE)

        def step(i, carry, diagonal):
            dk, dv = carry
            qs = pl.ds(pl.multiple_of(i * tk, tk), tk)
            q, dob = _qk_cat(qn_ref[qs, :], qr_ref[0, qs, :]), do_ref[qs, :].astype(_MXU_DTYPE)
            p = jnp.exp2(_attn_scores(q, k, diagonal) - lse_ref[0, qs, :])
            ds = p * (_mxu(dob, v, 1, 1) - delta_ref[qs, :]) * _QK_SCALE
            dq = _mxu(ds, k, 1, 0)
            dqn_ref[qs, :] += dq[:, :MLA_NOPE]
            dqr_ref[0, qs, :] += dq[:, MLA_NOPE:]
            return dk + _mxu(ds, q, 0, 0), dv + _mxu(p, dob, 0, 0)

        on_diagonal = step(j, (jnp.zeros((tk, MLA_QK), F32), jnp.zeros((tk, MLA_V), F32)), diagonal=True)
        dk, dv = lax.fori_loop(j + 1, nk, functools.partial(step, diagonal=False), on_diagonal)
        dkv_ref[:, :MLA_NOPE] = dk[:, :MLA_NOPE].astype(dkv_ref.dtype)
        dkv_ref[:, MLA_NOPE:] = dv.astype(dkv_ref.dtype)
        dkr_ref[krows, :] += dk[:, MLA_NOPE:]

    head_cols = pl.BlockSpec((T, MLA_NOPE), lambda h, j: (0, h))
    head_rope = pl.BlockSpec((1, T, MLA_ROPE), lambda h, j: (h, 0, 0))
    kv_spec = pl.BlockSpec((tk, MLA_NOPE + MLA_V), lambda h, j: (j, h))
    kr_spec = pl.BlockSpec((T, MLA_ROPE), lambda h, j: (0, 0))
    return pl.pallas_call(
        body, name=name, grid=(MLA_HEADS, nk),
        in_specs=[head_cols, head_rope, kv_spec, kr_spec, head_cols, pl.BlockSpec((1, T, 1), lambda h, j: (h, 0, 0)), head_cols],
        out_specs=[head_cols, head_rope, kv_spec, kr_spec],
        out_shape=[jax.ShapeDtypeStruct((T, MLA_HEADS * MLA_NOPE), F32), jax.ShapeDtypeStruct((MLA_HEADS, T, MLA_ROPE), F32),
                   jax.ShapeDtypeStruct((T, MLA_HEADS * (MLA_NOPE + MLA_V)), BF16), jax.ShapeDtypeStruct((T, MLA_ROPE), F32)],
        scratch_shapes=[pltpu.VMEM((T, 1), F32)],
        compiler_params=_cparams(("arbitrary", "arbitrary")),
    )(q_all, q_rope, kv, k_rope, o, lse, do)


def _s5_discretize(a_re, a_im, log_dt, bt_re, bt_im, lb_logits):
    dt = jnp.exp(log_dt)
    mag = jnp.exp(a_re * dt)
    abr, abi = mag * jnp.cos(a_im * dt), mag * jnp.sin(a_im * dt)
    den = a_re * a_re + a_im * a_im
    xr, xi = abr - 1.0, abi
    cr = ((xr * a_re + xi * a_im) / den)[:, None, :]
    ci = ((xi * a_re - xr * a_im) / den)[:, None, :]
    e = jnp.exp(lb_logits - jnp.max(lb_logits, axis=0, keepdims=True))
    lb = e[0:1, :] / jnp.sum(e, axis=0, keepdims=True)
    return abr, abi, cr * bt_re - ci * bt_im, cr * bt_im + ci * bt_re, lb


def _whole(shape):
    return pl.BlockSpec(shape, lambda: (0,) * len(shape))


def _s5_params_fwd(a_re, a_im, log_dt, bt_re, bt_im, lb_logits):
    ins = (a_re, a_im, log_dt, bt_re, bt_im, lb_logits)
    outs = [jax.ShapeDtypeStruct(s, F32) for s in (a_re.shape, a_re.shape, bt_re.shape, bt_re.shape, (1, lb_logits.shape[1]))]

    def body(*refs):
        res = _s5_discretize(*[r[...] for r in refs[:6]])
        for o_ref, o in zip(refs[6:], res):
            o_ref[...] = o

    return pl.pallas_call(body, name="s5_params_fwd", in_specs=[_whole(a.shape) for a in ins],
                          out_specs=[_whole(o.shape) for o in outs], out_shape=outs, compiler_params=_cparams())(*ins)


def _s5_params_bwd(a_re, a_im, log_dt, bt_re, bt_im, lb_logits, d_abr, d_abi, d_bbr, d_bbi, d_lb):
    ins = (a_re, a_im, log_dt, bt_re, bt_im, lb_logits, d_abr, d_abi, d_bbr, d_bbi, d_lb)
    outs = [jax.ShapeDtypeStruct(a.shape, F32) for a in ins[:6]]

    def body(*refs):
        _, vjp = jax.vjp(_s5_discretize, *[r[...] for r in refs[:6]])
        for o_ref, o in zip(refs[11:], vjp(tuple(r[...] for r in refs[6:11]))):
            o_ref[...] = o

    return pl.pallas_call(body, name="s5_params_bwd", in_specs=[_whole(a.shape) for a in ins],
                          out_specs=[_whole(o.shape) for o in outs], out_shape=outs, compiler_params=_cparams())(*ins)


ADAMW_WHOLE_BYTES = 1024 * 1024


def _adamw(w, g, m, v, *, name):
    R, C = w.shape
    whole = R % SUBLANE != 0 or R * C * w.dtype.itemsize <= ADAMW_WHOLE_BYTES
    tr = R if whole else _pick(R, (256, 128, 64, 32, 16, 8))
    slabs = g.shape[0] if g.ndim == 3 else 0

    def body(w_ref, g_ref, m_ref, v_ref, *outs):
        if slabs:
            gv = g_ref[0].astype(F32)
            for s in range(1, slabs):
                gv = gv + g_ref[s].astype(F32)
            outs[0][...] = gv
            outs = outs[1:]
        else:
            gv = g_ref[...]
        d_ref, mo_ref, vo_ref = outs
        m2 = ADAM_B1 * m_ref[...] + (1.0 - ADAM_B1) * gv
        v2 = ADAM_B2 * v_ref[...] + (1.0 - ADAM_B2) * (gv * gv)
        m_hat = m2 / (1.0 - ADAM_B1 ** ADAM_STEP)
        v_hat = v2 / (1.0 - ADAM_B2 ** ADAM_STEP)
        d_ref[...] = -ADAM_LR * (m_hat / (jnp.sqrt(v_hat) + ADAM_EPS) + ADAM_WD * w_ref[...])
        mo_ref[...] = m2
        vo_ref[...] = v2

    spec = pl.BlockSpec((tr, C), lambda i: (i, 0))
    g_spec = pl.BlockSpec((slabs, tr, C), lambda i: (0, i, 0)) if slabs else spec
    out = jax.ShapeDtypeStruct((R, C), F32)
    n_out = 4 if slabs else 3
    return pl.pallas_call(body, name=name, grid=(R // tr,), in_specs=[spec, g_spec, spec, spec], out_specs=[spec] * n_out,
                          out_shape=[out] * n_out, compiler_params=_cparams(("parallel",)))(w, g, m, v)


N_CHIPS = 4
N_CORES = 2


_FLIPS = tuple((dx, dy, dc) for dx in (0, 1) for dy in (0, 1) for dc in (0, 1) if (dx, dy, dc) != (0, 0, 0))


_HBM = pl.BlockSpec(memory_space=pltpu.HBM)
_SEM = pl.BlockSpec(memory_space=pltpu.SEMAPHORE)
_SPLIT_COPY = pltpu.CompilerParams(has_side_effects=pltpu.SideEffectType.DATAFLOW_SIDE_EFFECTING)


def _exchange_copies(src_refs, land_refs, send_sems, recv_sems, scatter, arriving):
    x, y, c = lax.axis_index("x"), lax.axis_index("y"), lax.axis_index("c")
    me_chip = 2 * x + y
    copies = []
    for a, (s_ref, l_ref) in enumerate(zip(src_refs, land_refs)):
        for j, (dx, dy, dc) in enumerate(_FLIPS):
            px, py, pc = (1 - x if dx else x), (1 - y if dy else y), (1 - c if dc else c)
            k = a * len(_FLIPS) + j
            p_chip = 2 * px + py
            copies.append(pltpu.make_async_remote_copy(
                src_ref=s_ref.at[p_chip, pc] if scatter else s_ref, dst_ref=l_ref.at[p_chip, pc] if arriving else l_ref.at[me_chip, c],
                send_sem=send_sems.at[k], recv_sem=recv_sems.at[k], device_id=(px, py, pc), device_id_type=MESH))
    return copies


def _exchange_start(srcs, *, scatter, name, after=()):
    n_arr = len(srcs)
    n_sem = n_arr * len(_FLIPS)
    n_in = 2 * n_arr + len(after)
    lands = [lax.empty(s.shape if scatter else (N_CHIPS, N_CORES) + s.shape, s.dtype) for s in srcs]

    def body(*refs):
        src_refs, land_refs = refs[:n_arr], refs[n_arr:2 * n_arr]
        for cp in _exchange_copies(src_refs, land_refs, refs[n_in], refs[n_in + 1], scatter, arriving=False):
            cp.start()
        refs[-1][...] = jnp.zeros_like(refs[-1])

    thru = [pltpu.HBM(a.shape, a.dtype) for a in srcs + lands]
    outs = pl.pallas_call(
        body, name=name,
        out_shape=(pltpu.SemaphoreType.DMA((n_sem,)), pltpu.SemaphoreType.DMA((n_sem,)), *thru,
                   jax.ShapeDtypeStruct((SUBLANE, LANE), F32)),
        in_specs=[_HBM] * (2 * n_arr) + [pl.BlockSpec(memory_space=pl.ANY)] * len(after),
        out_specs=(_SEM, _SEM, *[_HBM] * (2 * n_arr), pl.BlockSpec(memory_space=pltpu.VMEM)),
        input_output_aliases={i: 2 + i for i in range(2 * n_arr)}, compiler_params=_SPLIT_COPY,
    )(*[pltpu.with_memory_space_constraint(a, pltpu.HBM) for a in srcs + lands], *after)
    return outs[0], outs[1], list(outs[2:2 + n_arr]), list(outs[2 + n_arr:2 + 2 * n_arr]), outs[-1]


def _exchange_wait(started, after, *, scatter, name):
    send_sems, recv_sems, srcs, lands, _ = started
    n_arr = len(srcs)

    def body(*refs):
        src_refs, land_refs = refs[:n_arr], refs[n_arr:2 * n_arr]
        for cp in _exchange_copies(src_refs, land_refs, refs[2 * n_arr], refs[2 * n_arr + 1], scatter, arriving=True):
            cp.wait_send()
            cp.wait_recv()

    outs = pl.pallas_call(
        body, name=name, out_shape=[pltpu.HBM(a.shape, a.dtype) for a in srcs + lands],
        in_specs=[_HBM] * (2 * n_arr) + [_SEM, _SEM, pl.BlockSpec(memory_space=pl.ANY)], out_specs=[_HBM] * (2 * n_arr),
        input_output_aliases={i: i for i in range(2 * n_arr)}, compiler_params=_SPLIT_COPY,
    )(*srcs, *lands, send_sems, recv_sems, after)
    return list(outs[:n_arr]), list(outs[n_arr:])


def _with_own(land, own):
    me_chip = 2 * lax.axis_index("x") + lax.axis_index("y")
    return lax.dynamic_update_slice(land, own[None, None], (me_chip, lax.axis_index("c")) + (0,) * own.ndim)


def _rms_fwd_fn(h, g):
    return (_rms(h, g),)


def _rms_bwd_fn(h, dhn, dres, g):
    _, vjp = jax.vjp(_rms, h, g)
    dh, dg = vjp(dhn)
    return dh + dres, dg


def _loss_fn(h, tgt, g):
    y, vjp = jax.vjp(_rms, h, g)
    diff = y - tgt
    dh, dg = vjp(diff * (1.0 / D_MODEL))
    return dh, dg, (0.5 / D_MODEL) * jnp.sum(diff * diff, axis=0, keepdims=True)


def _s5_act(ys, u, d):
    return _gelu(ys + d * u)


def _s5_gate(z, gl, b):
    return z * _sigmoid(gl + b)


def _s5_act_fn(ys, u, d):
    return (_s5_act(ys, u, d),)


def _s5_mix_fn(ya, z, gl, b):
    return (jnp.concatenate([ya, _s5_gate(z, gl, b)], axis=1),)


def _s5_gate_bwd_fn(z, gl, dyb, b):
    _, vjp = jax.vjp(_s5_gate, z, gl, b)
    return vjp(dyb)


def _s5_act_bwd_fn(ys, u, dz1, dz2, d):
    _, vjp = jax.vjp(_s5_act, ys, u, d)
    return vjp(dz1 + dz2)


def _dproj_fn(dq, df, di, dg, du1, du2):
    return (jnp.concatenate([dq, df, di, dg, du1 + du2], axis=1),)


def _rope_pair(r1, r2, pos, freqs):
    ang = pos.astype(F32) * freqs
    c, s = jnp.cos(ang), jnp.sin(ang)
    return r1 * c - r2 * s, r1 * s + r2 * c


_ODD_SPLITS = (0, MLA_Q_RANK, MLA_Q_RANK + MLA_KV_RANK, MLA_Q_RANK + MLA_KV_RANK + LANE, ODD_IN_PAD)


def _mla_prep(cq, ckv, k1, k2, qg, kvg, pos, freqs):
    ko1, ko2 = _rope_pair(k1, k2, pos, freqs)
    return _rms(cq, qg), _rms(ckv, kvg), ko1, ko2


def _mla_prep_fn(proj, pos, qg, kvg, freqs):
    parts = [proj[:, a:b] for a, b in zip(_ODD_SPLITS[:-1], _ODD_SPLITS[1:])]
    return _mla_prep(*parts, qg, kvg, pos, freqs)


def _mla_prep_bwd_fn(proj, pos, dqn, dkvn, dko1, dko2, qg, kvg, freqs):
    parts = [proj[:, a:b] for a, b in zip(_ODD_SPLITS[:-1], _ODD_SPLITS[1:])]
    _, vjp = jax.vjp(lambda *a: _mla_prep(*a, pos, freqs), *parts, qg, kvg)
    dcq, dckv, dk1, dk2, dqg, dkvg = vjp((dqn, dkvn, dko1, dko2))
    return jnp.concatenate([dcq, dckv, dk1, dk2], axis=1), dqg, dkvg


def _rope_q_fn(r1, r2, pos, freqs):
    return _rope_pair(r1, r2, pos, freqs)


def _rope_q_bwd_fn(dqn, do1, do2, pos, freqs):
    dr1, dr2 = _rope_pair(do1, do2, pos, -freqs)
    return (jnp.concatenate([dqn, dr1, dr2], axis=1),)


W_NAMES = ("norm_mix_g", "norm_ffn_g", "final_norm_g", "even_w_in", "hgrn_lb_logits", "hgrn_norm_g", "s5_a_re", "s5_a_im",
           "s5_log_dt", "s5_b_re", "s5_b_im", "s5_c_re", "s5_c_im", "s5_d", "s5_w_glu", "s5_b_glu", "even_w_out", "odd_w_in",
           "mla_q_norm_g", "mla_w_uq", "mla_kv_norm_g", "mla_w_ukv", "odd_w_out", "ffn_w_in", "ffn_conv_w", "ffn_conv_b",
           "ffn_w_out")
BIG_UNITS = (("even_w_in", 0, "col"), ("s5_w_glu", 0, "row"), ("even_w_out", 0, "row"), ("odd_w_in", 0, "row"),
             ("mla_w_uq", 0, "col"), ("mla_w_ukv", 0, "col"), ("odd_w_out", 0, "row"),
             ("ffn_w_in", 0, "col"), ("ffn_w_in", 1, "col"), ("ffn_w_out", 0, "row"), ("ffn_w_out", 1, "row"))
BIG_NAMES = tuple(dict.fromkeys(u[0] for u in BIG_UNITS))
SMALL_SHARDED = (("mla_q_norm_g", 1), ("mla_kv_norm_g", 1), ("ffn_conv_w", 2))
SMALL_SHARDED_NAMES = tuple(s[0] for s in SMALL_SHARDED)
REPLICATED = tuple(n for n in W_NAMES if n not in BIG_NAMES + SMALL_SHARDED_NAMES)
REPLICATED_LATE = ("norm_mix_g",)
REPLICATED_EARLY = tuple(n for n in REPLICATED if n not in REPLICATED_LATE)


def _pack(flats, cols, row_mult):
    flat = jnp.concatenate(flats, axis=-1)
    pad = (-flat.shape[-1]) % (cols * row_mult)
    flat = jnp.pad(flat, [(0, 0)] * (flat.ndim - 1) + [(0, pad)])
    return flat.reshape(flat.shape[:-1] + (-1, cols))


def _unpack(flat, shapes):
    out, off = [], 0
    for shp in shapes:
        n = int(np.prod(shp))
        out.append(flat[..., off:off + n].reshape(flat.shape[:-1] + tuple(shp)))
        off += n
    return out


UNIT_KIND = {(n, l): kind for n, l, kind in BIG_UNITS}
STAGES = ((("even_w_in", 0),),
          (("s5_w_glu", 0), ("even_w_out", 0)),
          (("ffn_w_in", 0), ("ffn_w_out", 0)),
          (("odd_w_in", 0), ("mla_w_uq", 0), ("mla_w_ukv", 0), ("odd_w_out", 0)),
          (("ffn_w_in", 1), ("ffn_w_out", 1)))


def _gather_start(w, stage, with_small, after):
    srcs = [w[n][l].astype(BF16) for n, l in STAGES[stage]]
    if with_small:
        srcs.append(_pack([w[n].reshape(-1) for n in SMALL_SHARDED_NAMES], LANE, SUBLANE))
    return _exchange_start(srcs, scatter=False, name=f"gather_start_{stage}", after=after)


def _gather_finish(started, after, w, stage, with_small):
    srcs, lands = _exchange_wait(started, after, scatter=False, name=f"gather_wait_{stage}")
    lands = [_with_own(land, src) for land, src in zip(lands, srcs)]
    big = {}
    for unit, g in zip(STAGES[stage], lands):
        r, c = g.shape[2:]
        big[unit] = g.reshape(N_DEV * r, c) if UNIT_KIND[unit] == "row" else g.transpose(2, 0, 1, 3).reshape(r, N_DEV * c)
    if not with_small:
        return big
    parts = _unpack(lands[-1].reshape(N_DEV, -1), [w[n].shape for n in SMALL_SHARDED_NAMES])
    small = {}
    for (n, ax), p in zip(SMALL_SHARDED, parts):
        shp = list(w[n].shape)
        shp[ax] *= N_DEV
        small[n] = jnp.moveaxis(p, 0, ax).reshape(shp)
    return big, small


def _scatter_start(g_big, stage, extra=()):
    srcs = []
    for unit in STAGES[stage]:
        g = g_big[unit].astype(BF16)
        if UNIT_KIND[unit] == "row":
            srcs.append(g.reshape(N_CHIPS, N_CORES, g.shape[0] // N_DEV, g.shape[1]))
        else:
            srcs.append(g.reshape(g.shape[0], N_CHIPS, N_CORES, g.shape[1] // N_DEV).transpose(1, 2, 0, 3))
    return _exchange_start(srcs + list(extra), scatter=True, name=f"scatter_start_{stage}")


def _scatter_finish(started, after, stage):
    srcs, lands = _exchange_wait(started, after, scatter=True, name=f"scatter_wait_{stage}")
    me_chip, c = 2 * lax.axis_index("x") + lax.axis_index("y"), lax.axis_index("c")
    outs = []
    for land, src in zip(lands, srcs):
        own = lax.dynamic_slice(src, (me_chip, c) + (0,) * (src.ndim - 2), (1, 1) + src.shape[2:])[0, 0]
        outs.append(_with_own(land, own).reshape((N_DEV,) + land.shape[2:]))
    return outs


def _small_sharded_pack(g_small, w):
    flats = []
    for n, ax in SMALL_SHARDED:
        shp = list(w[n].shape)
        g = g_small[n].astype(F32).reshape(shp[:ax] + [N_DEV] + shp[ax:])
        flats.append(jnp.moveaxis(g, ax, 0).reshape(N_DEV, -1))
    small = _pack(flats, LANE, SUBLANE)
    return small.reshape((N_CHIPS, N_CORES) + small.shape[1:])


def _replicated_pack(g_repl, names):
    vec = _pack([g_repl[n].reshape(-1).astype(F32) for n in names], LANE, SUBLANE)
    return jnp.broadcast_to(vec, (N_CHIPS, N_CORES) + vec.shape)


def _block_diag(blocks):
    G, a, b = blocks.shape
    return jnp.einsum('gab,gk->gakb', blocks, jnp.eye(G, dtype=blocks.dtype)).reshape(G * a, G * b)


def _diag_blocks(mat, a, b):
    G = mat.shape[0] // a
    return jnp.einsum('gagb->gab', mat.reshape(G, a, G, b))


def _ffn_fwd(h, g, w_in, conv_w, conv_b, w_out, tag):
    hn, = _rows(_rms_fwd_fn, [h], [g], [(D_MODEL, BF16)], [], name=f"ffn{tag}_norm")
    au = _mm(hn, w_in, out_dtype=BF16, name=f"ffn{tag}_in")
    z = _ffn_mid_fwd(au, conv_w, conv_b, name=f"ffn{tag}_mid")
    return _mm(z, w_out, res=h, name=f"ffn{tag}_out"), (hn, au, z)


def _ffn_bwd(h, dh, saved, g, w_in, conv_w, conv_b, w_out, tag, deps=()):
    hn, au, z = saved
    dz = _mm(dh, w_out, tb=True, out_dtype=BF16, deps=deps, name=f"ffn{tag}_dz")
    dw_out = _mm(z, dh, ta=True, out_dtype=BF16, name=f"ffn{tag}_dwout")
    da, du, dcw, dcb = _ffn_mid_bwd(au, dz, conv_w, conv_b, name=f"ffn{tag}_dmid")
    dhn = _mm(da, w_in, tb=True, b_cols=(0, D_FF), name=f"ffn{tag}_dhn_a")
    dhn = _mm(du, w_in, tb=True, b_cols=(D_FF, D_FF), res=dhn, name=f"ffn{tag}_dhn_u")
    dw_in = jnp.concatenate([_mm(hn, da, ta=True, out_dtype=BF16, name=f"ffn{tag}_dwin_a"),
                             _mm(hn, du, ta=True, out_dtype=BF16, name=f"ffn{tag}_dwin_u")], axis=1)
    dh_in, dg = _rows(_rms_bwd_fn, [h, dhn, dh], [g], [(D_MODEL, F32)], [(1, D_MODEL)], name=f"ffn{tag}_dnorm")
    return dh_in, dict(g=dg, w_in=dw_in, conv_w=dcw, conv_b=dcb, w_out=dw_out)


def kernel(x, positions, norm_mix_g, norm_ffn_g, final_norm_g, even_w_in, hgrn_lb_logits, hgrn_norm_g, s5_a_re, s5_a_im, s5_log_dt, s5_b_re, s5_b_im, s5_c_re, s5_c_im, s5_d, s5_w_glu, s5_b_glu, even_w_out, odd_w_in, mla_q_norm_g, mla_w_uq, mla_kv_norm_g, mla_w_ukv, odd_w_out, ffn_w_in, ffn_conv_w, ffn_conv_b, ffn_w_out, loss_target, m_norm_mix_g, m_norm_ffn_g, m_final_norm_g, m_even_w_in, m_hgrn_lb_logits, m_hgrn_norm_g, m_s5_a_re, m_s5_a_im, m_s5_log_dt, m_s5_b_re, m_s5_b_im, m_s5_c_re, m_s5_c_im, m_s5_d, m_s5_w_glu, m_s5_b_glu, m_even_w_out, m_odd_w_in, m_mla_q_norm_g, m_mla_w_uq, m_mla_kv_norm_g, m_mla_w_ukv, m_odd_w_out, m_ffn_w_in, m_ffn_conv_w, m_ffn_conv_b, m_ffn_w_out, v_norm_mix_g, v_norm_ffn_g, v_final_norm_g, v_even_w_in, v_hgrn_lb_logits, v_hgrn_norm_g, v_s5_a_re, v_s5_a_im, v_s5_log_dt, v_s5_b_re, v_s5_b_im, v_s5_c_re, v_s5_c_im, v_s5_d, v_s5_w_glu, v_s5_b_glu, v_even_w_out, v_odd_w_in, v_mla_q_norm_g, v_mla_w_uq, v_mla_kv_norm_g, v_mla_w_ukv, v_odd_w_out, v_ffn_w_in, v_ffn_conv_w, v_ffn_conv_b, v_ffn_w_out):
    given = dict(locals())
    w = {n: given[n] for n in W_NAMES}
    mom = {n: given["m_" + n] for n in W_NAMES}
    var = {n: given["v_" + n] for n in W_NAMES}
    T = x.shape[1]
    h0 = x[0]
    tgt = loss_target[0]
    pos = positions.reshape(T, 1)

    gathers = []
    for s in range(len(STAGES)):
        gathers.append(_gather_start(w, s, with_small=(s == 1), after=[g[4] for g in gathers[-1:]]))
    half = MLA_ROPE // 2
    kr0 = MLA_Q_RANK + MLA_KV_RANK
    freqs = ROPE_THETA ** (-jnp.arange(0, MLA_ROPE, 2, dtype=F32) / MLA_ROPE)
    freqs_q = jnp.tile(freqs, MLA_HEADS)[None, :]
    freqs_k = jnp.concatenate([freqs, jnp.zeros((LANE - half,), F32)])[None, :]

    sp_in = (s5_a_re[0], s5_a_im[0], s5_log_dt[0][:, None], s5_b_re[0].transpose(0, 2, 1), s5_b_im[0].transpose(0, 2, 1),
             hgrn_lb_logits)
    abr, abi, bbt_re, bbt_im, lb0 = _s5_params_fwd(*sp_in)
    a_re, a_im = abr.reshape(1, S5_WIDTH), abi.reshape(1, S5_WIDTH)
    bb_re, bb_im = _block_diag(bbt_re).astype(BF16), _block_diag(bbt_im).astype(BF16)
    c_re = _block_diag(s5_c_re[0].transpose(0, 2, 1)).astype(BF16)
    c_im_neg = _block_diag(-s5_c_im[0].transpose(0, 2, 1)).astype(BF16)
    u_cols = (4 * HGRN_DIM, S5_DIM)

    hn0, = _rows(_rms_fwd_fn, [h0], [norm_mix_g[0:1]], [(D_MODEL, BF16)], [], name="mix0_norm", deps=[gathers[-1][4]])
    full = _gather_finish(gathers[0], hn0, w, 0, False)
    w_ein = full["even_w_in", 0]
    proj = _mm(hn0, w_ein, name="even_in")
    y_a, states = _hgrn_fwd(proj, lb0, hgrn_norm_g, name="hgrn_fwd")
    s_re, s_im = _s5_scan_fwd(proj, u_cols, bb_re, bb_im, a_re, a_im, name="s5_scan_fwd")
    more, full_small = _gather_finish(gathers[1], s_re, w, 1, True)
    w_glu, w_eout = more["s5_w_glu", 0], more["even_w_out", 0]
    qg, kvg, conv_w = full_small["mla_q_norm_g"], full_small["mla_kv_norm_g"], full_small["ffn_conv_w"]
    ys = _mm(s_im, c_im_neg, res=_mm(s_re, c_re, name="s5_y_re"), name="s5_y_im")
    z5, = _rows(_s5_act_fn, [ys, (proj,) + u_cols], [s5_d], [(S5_DIM, F32)], [], name="s5_act")
    gl = _mm(z5, w_glu, name="s5_glu")
    mixin, = _rows(_s5_mix_fn, [y_a, z5, gl], [s5_b_glu], [(D_MODEL, BF16)], [], name="s5_mix")
    h1 = _mm(mixin, w_eout, res=h0, name="even_out")
    full.update(_gather_finish(gathers[2], h1, w, 2, False))
    w_fin, w_fout = [full["ffn_w_in", 0]], [full["ffn_w_out", 0]]
    h2, ffn0_saved = _ffn_fwd(h1, norm_ffn_g[0:1], w_fin[0], conv_w[0], ffn_conv_b[0:1], w_fout[0], 0)

    full.update(_gather_finish(gathers[3], h2, w, 3, False))
    w_oin, w_ukv, w_oout = full["odd_w_in", 0], full["mla_w_ukv", 0], full["odd_w_out", 0]
    zpad = jnp.zeros((D_MODEL, LANE - half), BF16)
    w_oin_pad = jnp.concatenate([w_oin[:, :kr0], w_oin[:, kr0:kr0 + half], zpad, w_oin[:, kr0 + half:], zpad], axis=1)
    w_uq3 = full["mla_w_uq", 0].reshape(MLA_Q_RANK, MLA_HEADS, MLA_QK)
    w_uq_perm = jnp.concatenate([w_uq3[:, :, :MLA_NOPE].reshape(MLA_Q_RANK, -1),
                                 w_uq3[:, :, MLA_NOPE:MLA_NOPE + half].reshape(MLA_Q_RANK, -1),
                                 w_uq3[:, :, MLA_NOPE + half:].reshape(MLA_Q_RANK, -1)], axis=1)
    hn1, = _rows(_rms_fwd_fn, [h2], [norm_mix_g[1:2]], [(D_MODEL, BF16)], [], name="mix1_norm")
    proj_o = _mm(hn1, w_oin_pad, name="odd_in")
    qn, kvn, ko1, ko2 = _rows(_mla_prep_fn, [proj_o, pos], [qg, kvg, freqs_k],
                              [(MLA_Q_RANK, BF16), (MLA_KV_RANK, BF16), (LANE, F32), (LANE, F32)], [], name="mla_prep")
    q_all = _mm(qn, w_uq_perm, name="mla_uq")
    kv = _mm(kvn, w_ukv, out_dtype=BF16, name="mla_ukv")
    nope_w = MLA_HEADS * MLA_NOPE
    rope_w = MLA_HEADS * half
    o1, o2 = _rows(_rope_q_fn, [(q_all, nope_w, rope_w), (q_all, nope_w + rope_w, rope_w), pos], [freqs_q],
                   [(rope_w, F32), (rope_w, F32)], [], name="mla_rope_q")
    q_rope = jnp.concatenate([o1.reshape(T, MLA_HEADS, half), o2.reshape(T, MLA_HEADS, half)], axis=2).transpose(1, 0, 2)
    k_rope = jnp.concatenate([ko1[:, :half], ko2[:, :half]], axis=1)
    o, lse = _attn_fwd(q_all, q_rope, kv, k_rope, name="attn_fwd")
    h3 = _mm(o, w_oout, res=h2, name="odd_out")
    full.update(_gather_finish(gathers[4], h3, w, 4, False))
    w_fin.append(full["ffn_w_in", 1])
    w_fout.append(full["ffn_w_out", 1])
    h4, ffn1_saved = _ffn_fwd(h3, norm_ffn_g[1:2], w_fin[1], conv_w[1], ffn_conv_b[1:2], w_fout[1], 1)

    dh4, d_final_g, loss_cols = _rows(_loss_fn, [h4, tgt], [final_norm_g[None, :]], [(D_MODEL, F32)],
                                      [(1, D_MODEL), (1, D_MODEL)], name="loss_head")
    loss = lax.psum(jnp.sum(loss_cols), ("x", "y", "c"))

    dh3, gf1 = _ffn_bwd(h3, dh4, ffn1_saved, norm_ffn_g[1:2], w_fin[1], conv_w[1], ffn_conv_b[1:2], w_fout[1], 1)
    scatters = {4: _scatter_start({("ffn_w_in", 1): gf1["w_in"], ("ffn_w_out", 1): gf1["w_out"]}, 4)}
    do = _mm(dh3, w_oout, tb=True, deps=[scatters[4][4]], name="odd_out_dx")
    d_w_oout = _mm(o, dh3, ta=True, out_dtype=BF16, name="odd_out_dw")
    dq_nope, dq_rope, dkv, dk_rope = _attn_bwd(q_all, q_rope, kv, k_rope, o, lse, do, name="attn_bwd")
    dq_rope_t = dq_rope.transpose(1, 0, 2)
    do1, do2 = dq_rope_t[:, :, :half].reshape(T, rope_w), dq_rope_t[:, :, half:].reshape(T, rope_w)
    lane_pad = ((0, 0), (0, LANE - half))
    dko1, dko2 = jnp.pad(dk_rope[:, :half], lane_pad), jnp.pad(dk_rope[:, half:], lane_pad)
    dq_all, = _rows(_rope_q_bwd_fn, [dq_nope, do1, do2, pos], [freqs_q], [(MLA_HEADS * MLA_QK, BF16)], [], name="mla_rope_q_bwd")
    d_w_uq_perm = _mm(qn, dq_all, ta=True, out_dtype=BF16, name="mla_uq_dw")
    dqn = _mm(dq_all, w_uq_perm, tb=True, name="mla_uq_dx")
    d_w_ukv = _mm(kvn, dkv, ta=True, out_dtype=BF16, name="mla_ukv_dw")
    dkvn = _mm(dkv, w_ukv, tb=True, name="mla_ukv_dx")
    dproj_o, d_qg, d_kvg = _rows(_mla_prep_bwd_fn, [proj_o, pos, dqn, dkvn, dko1, dko2], [qg, kvg, freqs_k],
                                 [(ODD_IN_PAD, BF16)], [(1, MLA_Q_RANK), (1, MLA_KV_RANK)], name="mla_prep_bwd")
    d_w_oin_pad = _mm(hn1, dproj_o, ta=True, out_dtype=BF16, name="odd_in_dw")
    dhn1 = _mm(dproj_o, w_oin_pad, tb=True, name="odd_in_dx")
    dh2, d_mix_g1 = _rows(_rms_bwd_fn, [h2, dhn1, dh3], [norm_mix_g[1:2]], [(D_MODEL, F32)], [(1, D_MODEL)], name="mix1_dnorm")
    d_w_oin = jnp.concatenate([d_w_oin_pad[:, :kr0 + half], d_w_oin_pad[:, kr0 + LANE:kr0 + LANE + half]], axis=1)
    d3 = d_w_uq_perm
    d_w_uq = jnp.concatenate([d3[:, :nope_w].reshape(MLA_Q_RANK, MLA_HEADS, MLA_NOPE),
                              d3[:, nope_w:nope_w + rope_w].reshape(MLA_Q_RANK, MLA_HEADS, half),
                              d3[:, nope_w + rope_w:].reshape(MLA_Q_RANK, MLA_HEADS, half)], axis=2).reshape(MLA_Q_RANK, -1)
    scatters[3] = _scatter_start({("odd_w_in", 0): d_w_oin, ("mla_w_uq", 0): d_w_uq, ("mla_w_ukv", 0): d_w_ukv,
                                  ("odd_w_out", 0): d_w_oout}, 3)

    dh1, gf0 = _ffn_bwd(h1, dh2, ffn0_saved, norm_ffn_g[0:1], w_fin[0], conv_w[0], ffn_conv_b[0:1], w_fout[0], 0,
                        deps=[scatters[3][4]])
    scatters[2] = _scatter_start({("ffn_w_in", 0): gf0["w_in"], ("ffn_w_out", 0): gf0["w_out"]}, 2)
    dmix = _mm(dh1, w_eout, tb=True, deps=[scatters[2][4]], name="even_out_dx")
    d_w_eout = _mm(mixin, dh1, ta=True, out_dtype=BF16, name="even_out_dw")
    dq, df, di, dg, d_lb0, d_hgrn_g = _hgrn_bwd(proj, lb0, hgrn_norm_g, states, dmix, name="hgrn_bwd")
    dz1, dgl, d_b_glu = _rows(_s5_gate_bwd_fn, [z5, gl, (dmix, HGRN_DIM, S5_DIM)], [s5_b_glu],
                              [(S5_DIM, F32), (S5_DIM, BF16)], [(1, S5_DIM)], name="s5_gate_bwd")
    dz2 = _mm(dgl, w_glu, tb=True, name="s5_glu_dx")
    d_w_glu = _mm(z5, dgl, ta=True, out_dtype=BF16, name="s5_glu_dw")
    dys, du1, d_s5_d = _rows(_s5_act_bwd_fn, [ys, (proj,) + u_cols, dz1, dz2], [s5_d],
                             [(S5_DIM, BF16), (S5_DIM, F32)], [(1, S5_DIM)], name="s5_act_bwd")
    d_c_re = _mm(s_re, dys, ta=True, name="s5_dc_re")
    d_c_im_neg = _mm(s_im, dys, ta=True, name="s5_dc_im")
    lam_re, lam_im, d_ar, d_ai = _s5_scan_bwd(dys, c_re, c_im_neg, s_re, s_im, a_re, a_im, name="s5_scan_bwd")
    du2 = _mm(lam_im, bb_im, tb=True, res=_mm(lam_re, bb_re, tb=True, name="s5_du_re"), name="s5_du_im")
    d_bb_re = _mm(proj, lam_re, ta=True, a_cols=u_cols, name="s5_dbb_re")
    d_bb_im = _mm(proj, lam_im, ta=True, a_cols=u_cols, name="s5_dbb_im")
    sp_g = _s5_params_bwd(*sp_in, d_ar.sum(0).reshape(S5_GROUPS, S5_STATE), d_ai.sum(0).reshape(S5_GROUPS, S5_STATE),
                          _diag_blocks(d_bb_re, S5_GROUP, S5_STATE), _diag_blocks(d_bb_im, S5_GROUP, S5_STATE), d_lb0)
    d_a_re, d_a_im, d_log_dt, d_bt_re, d_bt_im, d_lb_logits = sp_g
    g_small = dict(mla_q_norm_g=d_qg, mla_kv_norm_g=d_kvg, ffn_conv_w=jnp.stack([gf0["conv_w"], gf1["conv_w"]]))
    g_repl = dict(
        norm_ffn_g=jnp.concatenate([gf0["g"], gf1["g"]]),
        final_norm_g=d_final_g[0], hgrn_lb_logits=d_lb_logits, hgrn_norm_g=d_hgrn_g,
        s5_a_re=d_a_re[None], s5_a_im=d_a_im[None], s5_log_dt=d_log_dt[:, 0][None],
        s5_b_re=d_bt_re.transpose(0, 2, 1)[None], s5_b_im=d_bt_im.transpose(0, 2, 1)[None],
        s5_c_re=_diag_blocks(d_c_re, S5_STATE, S5_GROUP).transpose(0, 2, 1)[None],
        s5_c_im=-_diag_blocks(d_c_im_neg, S5_STATE, S5_GROUP).transpose(0, 2, 1)[None],
        s5_d=d_s5_d, s5_b_glu=d_b_glu, ffn_conv_b=jnp.concatenate([gf0["conv_b"], gf1["conv_b"]]))
    scatters[1] = _scatter_start({("s5_w_glu", 0): d_w_glu, ("even_w_out", 0): d_w_eout}, 1,
                                 extra=[_small_sharded_pack(g_small, w), _replicated_pack(g_repl, REPLICATED_EARLY)])
    dproj, = _rows(_dproj_fn, [dq, df, di, dg, du1, du2], [], [(EVEN_IN, BF16)], [], name="even_dproj", deps=[scatters[1][4]])
    d_w_ein = _mm(hn0, dproj, ta=True, out_dtype=BF16, name="even_in_dw")
    dhn0 = _mm(dproj, w_ein, tb=True, name="even_in_dx")
    grad_x, d_mix_g0 = _rows(_rms_bwd_fn, [h0, dhn0, dh1], [norm_mix_g[0:1]], [(D_MODEL, F32)], [(1, D_MODEL)], name="mix0_dnorm")
    g_repl["norm_mix_g"] = jnp.concatenate([d_mix_g0, d_mix_g1])
    scatters[0] = _scatter_start({("even_w_in", 0): d_w_ein}, 0, extra=[_replicated_pack(g_repl, REPLICATED_LATE)])

    delta, new_m, new_v = {}, {}, {}
    per_unit, partial = {}, {}
    after = scatters[0][4]
    for stage in (4, 3, 2, 1, 0):
        partial[stage] = _scatter_finish(scatters[stage], after, stage)
        for (n, l), slabs in zip(STAGES[stage], partial[stage]):
            per_unit[n, l] = _adamw(w[n][l], slabs, mom[n][l], var[n][l], name=f"adamw_{n}_{l}")
        after = per_unit[STAGES[stage][-1]][0]
    grads = {}
    for n in BIG_NAMES:
        layers = [per_unit[n, l] for l in range(w[n].shape[0])]
        grads[n], delta[n], new_m[n], new_v[n] = (jnp.stack([lay[k] for lay in layers]) for k in range(4))
    for names, slabs, tag in ((REPLICATED_EARLY, partial[1][-1], "repl"), (REPLICATED_LATE, partial[0][-1], "late"),
                              (SMALL_SHARDED_NAMES, partial[1][-2], "small")):
        packs = [_pack([t[n].reshape(-1) for n in names], LANE, SUBLANE) for t in (w, mom, var)]
        outs = _adamw(packs[0], slabs, packs[1], packs[2], name=f"adamw_{tag}")
        shapes = [w[n].shape for n in names]
        for dst, o_ in zip((grads, delta, new_m, new_v), outs):
            dst.update(zip(names, _unpack(o_.reshape(-1), shapes)))

    return (loss, grad_x[None], *[grads[n] for n in W_NAMES], *[delta[n] for n in W_NAMES],
            *[new_m[n] for n in W_NAMES], *[new_v[n] for n in W_NAMES])
```

```python
import functools
import math

import numpy as np
import jax
import jax.numpy as jnp
from jax import lax
from jax.experimental import pallas as pl
from jax.experimental.pallas import tpu as pltpu

F32 = jnp.float32
BF16 = jnp.bfloat16
_MXU_DTYPE = jnp.bfloat16

D_MODEL = 1024
HGRN_DIM = 512
HGRN_HEAD_DIM = 128
HGRN_HEADS = 4
HGRN_CHUNK = 64
S5_DIM = 512
S5_GROUPS = 32
S5_GROUP = 16
S5_STATE = 64
S5_WIDTH = S5_GROUPS * S5_STATE
EVEN_IN = 4 * HGRN_DIM + S5_DIM
MLA_HEADS = 8
MLA_Q_RANK = 384
MLA_KV_RANK = 256
MLA_NOPE = 128
MLA_ROPE = 64
MLA_V = 128
MLA_QK = MLA_NOPE + MLA_ROPE
ODD_IN = MLA_Q_RANK + MLA_KV_RANK + MLA_ROPE
ODD_IN_PAD = MLA_Q_RANK + MLA_KV_RANK + 2 * 128
ROPE_THETA = 10000.0
D_FF = 2816
EPS = 1e-6
ADAM_LR = 0.001
ADAM_B1 = 0.9
ADAM_B2 = 0.999
ADAM_EPS = 1e-08
ADAM_WD = 0.01
ADAM_STEP = 10

N_DEV = 8
LANE = 128
SUBLANE = 8
VMEM_LIMIT_BYTES = 56 * 1024 * 1024
MESH = pl.DeviceIdType.MESH


def _cparams(sem=None):
    return pltpu.CompilerParams(dimension_semantics=sem, vmem_limit_bytes=VMEM_LIMIT_BYTES)


def _pick(n, cands):
    for c in cands:
        if n % c == 0:
            return c
    raise ValueError(f"no tile for {n} in {cands}")


def _sigmoid(x):
    return 0.5 * jnp.tanh(0.5 * x) + 0.5


def _silu(x):
    return x * _sigmoid(x)


def _gelu(x):
    return 0.5 * x * (1.0 + jnp.tanh(math.sqrt(2.0 / math.pi) * (x + 0.044715 * (x * x * x))))


def _rms(x, g):
    return x * lax.rsqrt(jnp.mean(x * x, axis=-1, keepdims=True) + EPS) * g


def _mxu(a, b, ca, cb):
    return lax.dot_general(a.astype(_MXU_DTYPE), b.astype(_MXU_DTYPE), (((ca,), (cb,)), ((), ())),
                           preferred_element_type=F32)


@functools.partial(jax.custom_vjp, nondiff_argnums=(2, 3))
def _mxu_ad(a, b, ca, cb):
    return _mxu(a, b, ca, cb)


def _mxu_ad_fwd(a, b, ca, cb):
    return _mxu(a, b, ca, cb), (a, b)


def _mxu_ad_bwd(ca, cb, saved, g):
    a, b = saved
    fa, fb = 1 - ca, 1 - cb
    da = _mxu(g, b, 1, fb) if ca == 1 else _mxu(b, g, fb, 1)
    db = _mxu(a, g, fa, 0) if cb == 0 else _mxu(g, a, 0, fa)
    return da, db


_mxu_ad.defvjp(_mxu_ad_fwd, _mxu_ad_bwd)


def _tri(n):
    row = lax.broadcasted_iota(jnp.int32, (n, n), 0)
    col = lax.broadcasted_iota(jnp.int32, (n, n), 1)
    return col <= row


def _cumsum_rows(x, reverse=False):
    n = x.shape[0]
    rowi = lax.broadcasted_iota(jnp.int32, x.shape, 0)
    s = 1
    while s < n:
        if reverse:
            x = x + jnp.where(rowi < n - s, pltpu.roll(x, n - s, 0), 0.0)
        else:
            x = x + jnp.where(rowi >= s, pltpu.roll(x, s, 0), 0.0)
        s *= 2
    return x


@jax.custom_vjp
def _cumsum_rows_ad(x):
    return _cumsum_rows(x)


def _cumsum_rows_ad_fwd(x):
    return _cumsum_rows(x), None


def _cumsum_rows_ad_bwd(_, g):
    return (_cumsum_rows(g, reverse=True),)


_cumsum_rows_ad.defvjp(_cumsum_rows_ad_fwd, _cumsum_rows_ad_bwd)


MM_VMEM_BUDGET = 36 * 1024 * 1024
MM_MAX_TILE = 1408
MM_FLOPS = 9e14
MM_HBM_BPS = 3e12
MM_STEP_S = 0.35e-6
MXU_TILE = 256


def _lane_divisors(n, cap, offs=()):
    return [d for d in range(min(n, cap) // LANE * LANE, 0, -LANE) if n % d == 0 and all(o % d == 0 for o in offs)]


def _mm_tiles(M, N, K, sa, sb, so, has_res, m_offs, n_offs, k_offs):
    best = None
    for tm in _lane_divisors(M, MM_MAX_TILE, m_offs):
        for tn in _lane_divisors(N, MM_MAX_TILE, n_offs):
            for tk in _lane_divisors(K, K, k_offs):
                nk = K // tk
                vmem = 2 * (tm * tk * sa + tk * tn * sb + tm * tn * so + tm * tn * 4 * has_res) + (tm * tn * 4 if nk > 1 else 0)
                if vmem > MM_VMEM_BUDGET:
                    continue
                steps = (M // tm) * (N // tn) * nk
                mxu = 2 * M * N * K / MM_FLOPS * (1 + LANE / tm) * (1 if tn >= MXU_TILE else MXU_TILE / tn)
                a_reads = 1 if nk == 1 else N // tn
                hbm = (M * K * sa * a_reads + K * N * sb * (M // tm) + M * N * (so + 4 * has_res)) / MM_HBM_BPS
                edge = (tm * tk * sa + tk * tn * sb + tm * tn * so) / MM_HBM_BPS
                acc = steps * tm * tn * 8 / MM_HBM_BPS if nk > 1 else 0
                cost = max(mxu, hbm) + steps * MM_STEP_S + edge + acc
                if best is None or cost < best[0]:
                    best = (cost, tm, tn, tk)
    return best[1:]


def _mm(a, b, *, ta=False, tb=False, res=None, out_dtype=F32, a_cols=None, b_cols=None, deps=(), name):
    a_minor = a.shape[1] if a_cols is None else a_cols[1]
    b_minor = b.shape[1] if b_cols is None else b_cols[1]
    K, M = (a.shape[0], a_minor) if ta else (a_minor, a.shape[0])
    N = b.shape[0] if tb else b_minor
    assert (b_minor if tb else b.shape[0]) == K, (a.shape, b.shape, ta, tb)
    a_off = 0 if a_cols is None else a_cols[0]
    b_off = 0 if b_cols is None else b_cols[0]
    has_res = res is not None
    tm, tn, tk = _mm_tiles(M, N, K, a.dtype.itemsize, b.dtype.itemsize, jnp.dtype(out_dtype).itemsize, has_res,
                           (a_off,) if ta else (), () if tb else (b_off,), ((a_off,) if not ta else ()) + ((b_off,) if tb else ()))
    nk = K // tk
    am, ak = (a_off // tm, 0) if ta else (0, a_off // tk)
    bn, bk = (0, b_off // tk) if tb else (b_off // tn, 0)
    a_spec = pl.BlockSpec((tk, tm), lambda i, j, k: (k, i + am)) if ta else pl.BlockSpec((tm, tk), lambda i, j, k: (i, k + ak))
    b_spec = pl.BlockSpec((tn, tk), lambda i, j, k: (j, k + bk)) if tb else pl.BlockSpec((tk, tn), lambda i, j, k: (k, j + bn))
    o_spec = pl.BlockSpec((tm, tn), lambda i, j, k: (i, j))
    ca, cb = (0 if ta else 1), (1 if tb else 0)

    n_fixed = 2 + has_res + len(deps)

    def body(*refs):
        a_ref, b_ref = refs[0], refs[1]
        res_ref = refs[2] if has_res else None
        o_ref = refs[n_fixed]
        part = _mxu(a_ref[...], b_ref[...], ca, cb)
        if nk == 1:
            o_ref[...] = (part + res_ref[...] if has_res else part).astype(out_dtype)
            return
        acc_ref = refs[n_fixed + 1]
        k = pl.program_id(2)

        @pl.when(k == 0)
        def _():
            acc_ref[...] = part

        @pl.when(k > 0)
        def _():
            acc_ref[...] += part

        @pl.when(k == nk - 1)
        def _():
            o_ref[...] = (acc_ref[...] + res_ref[...] if has_res else acc_ref[...]).astype(out_dtype)

    ins = [a, b] + ([res] if has_res else []) + list(deps)
    in_specs = [a_spec, b_spec] + ([o_spec] if has_res else []) + [pl.BlockSpec(memory_space=pl.ANY)] * len(deps)
    return pl.pallas_call(
        body, name=name, grid=(M // tm, N // tn, nk),
        in_specs=in_specs, out_specs=o_spec,
        out_shape=jax.ShapeDtypeStruct((M, N), out_dtype),
        scratch_shapes=[pltpu.VMEM((tm, tn), F32)] if nk > 1 else [],
        compiler_params=_cparams(("parallel", "parallel", "arbitrary")),
    )(*ins)


def _rows(fn, row_ins, const_ins, row_outs, acc_outs, *, name, tm=512, deps=()):
    norm = [(r, 0, r.shape[1]) if not isinstance(r, tuple) else r for r in row_ins]
    T = norm[0][0].shape[0]
    tm = min(tm, T)
    nr, nc, no, na = len(norm), len(const_ins), len(row_outs), len(acc_outs)
    first_out = nr + nc + len(deps)

    def body(*refs):
        i = pl.program_id(0)
        vals = [r[...] for r in refs[:nr + nc]]
        outs = fn(*vals)
        for o_ref, o in zip(refs[first_out:first_out + no], outs[:no]):
            o_ref[...] = o.astype(o_ref.dtype)
        for a_ref, o in zip(refs[first_out + no:], outs[no:]):
            @pl.when(i == 0)
            def _(a_ref=a_ref, o=o):
                a_ref[...] = o

            @pl.when(i > 0)
            def _(a_ref=a_ref, o=o):
                a_ref[...] += o

    in_specs = []
    for arr, off, w in norm:
        assert off % w == 0, (off, w)
        in_specs.append(pl.BlockSpec((tm, w), lambda i, b=off // w: (i, b)))
    for c in const_ins:
        in_specs.append(pl.BlockSpec(c.shape, lambda i: (0, 0)))
    in_specs += [pl.BlockSpec(memory_space=pl.ANY)] * len(deps)
    out_specs = [pl.BlockSpec((tm, w), lambda i: (i, 0)) for w, _ in row_outs]
    out_specs += [pl.BlockSpec(s, lambda i: (0, 0)) for s in acc_outs]
    out_shape = [jax.ShapeDtypeStruct((T, w), dt) for w, dt in row_outs]
    out_shape += [jax.ShapeDtypeStruct(s, F32) for s in acc_outs]
    return pl.pallas_call(
        body, name=name, grid=(T // tm,), in_specs=in_specs, out_specs=out_specs, out_shape=out_shape,
        compiler_params=_cparams(("arbitrary",)),
    )(*[n[0] for n in norm], *const_ins, *deps)


FFN_COL_TILE = LANE
FFN_ROW_CHUNK = 512


FFN_HALO = 2 * SUBLANE


def _shift_down(ext, s, rows):
    return pltpu.roll(ext, s, 0)[FFN_HALO:FFN_HALO + rows]


def _shift_up(ext, s, rows):
    return pltpu.roll(ext, rows + FFN_HALO - s, 0)[:rows]


def _ffn_chunks(T):
    r = min(FFN_ROW_CHUNK, T)
    return r, T // r


def _ext_before(ref, c, r):
    if c == 0:
        return jnp.concatenate([jnp.zeros((FFN_HALO, ref.shape[1]), F32), ref[0:r, :].astype(F32)], axis=0)
    return ref[c * r - FFN_HALO:(c + 1) * r, :].astype(F32)


def _ext_after(ref, c, r, nch):
    if c == nch - 1:
        return jnp.concatenate([ref[c * r:(c + 1) * r, :].astype(F32), jnp.zeros((FFN_HALO, ref.shape[1]), F32)], axis=0)
    return ref[c * r:(c + 1) * r + FFN_HALO, :].astype(F32)


def _ffn_mid_fwd(au, conv_w, conv_b, *, name):
    T = au.shape[0]
    tc = FFN_COL_TILE
    ncol = D_FF // tc
    r, nch = _ffn_chunks(T)

    def body(a_ref, u_ref, w_ref, b_ref, z_ref):
        w0, w1, w2, bias = w_ref[0:1, :], w_ref[1:2, :], w_ref[2:3, :], b_ref[...]
        for c in range(nch):
            ext = _ext_before(a_ref, c, r)
            pre = w0 * _shift_down(ext, 2, r) + w1 * _shift_down(ext, 1, r) + w2 * ext[FFN_HALO:] + bias
            z_ref[c * r:(c + 1) * r, :] = (_silu(pre) * u_ref[c * r:(c + 1) * r, :].astype(F32)).astype(z_ref.dtype)

    return pl.pallas_call(
        body, name=name, grid=(ncol,),
        in_specs=[pl.BlockSpec((T, tc), lambda j: (0, j)), pl.BlockSpec((T, tc), lambda j: (0, j + ncol)),
                  pl.BlockSpec((3, tc), lambda j: (0, j)), pl.BlockSpec((1, tc), lambda j: (0, j))],
        out_specs=pl.BlockSpec((T, tc), lambda j: (0, j)),
        out_shape=jax.ShapeDtypeStruct((T, D_FF), BF16),
        compiler_params=_cparams(("parallel",)),
    )(au, au, conv_w, conv_b)


def _ffn_mid_bwd(au, dz, conv_w, conv_b, *, name):
    T = au.shape[0]
    tc = FFN_COL_TILE
    ncol = D_FF // tc
    r, nch = _ffn_chunks(T)

    def body(a_ref, u_ref, dz_ref, w_ref, b_ref, da_ref, du_ref, dw_ref, db_ref, dpre_ref):
        w0, w1, w2, bias = w_ref[0:1, :], w_ref[1:2, :], w_ref[2:3, :], b_ref[...]
        dw0 = jnp.zeros((1, tc), F32)
        dw1 = jnp.zeros((1, tc), F32)
        dw2 = jnp.zeros((1, tc), F32)
        db = jnp.zeros((1, tc), F32)
        for c in range(nch):
            rows = slice(c * r, (c + 1) * r)
            ext = _ext_before(a_ref, c, r)
            a2, a1, a0 = _shift_down(ext, 2, r), _shift_down(ext, 1, r), ext[FFN_HALO:]
            pre = w0 * a2 + w1 * a1 + w2 * a0 + bias
            sg = _sigmoid(pre)
            act = pre * sg
            dzc = dz_ref[rows, :].astype(F32)
            du_ref[rows, :] = (dzc * act).astype(du_ref.dtype)
            dpre = dzc * u_ref[rows, :].astype(F32) * (sg * (1.0 + pre * (1.0 - sg)))
            dpre_ref[rows, :] = dpre
            dw0 += jnp.sum(dpre * a2, axis=0, keepdims=True)
            dw1 += jnp.sum(dpre * a1, axis=0, keepdims=True)
            dw2 += jnp.sum(dpre * a0, axis=0, keepdims=True)
            db += jnp.sum(dpre, axis=0, keepdims=True)
        for c in range(nch):
            ext = _ext_after(dpre_ref, c, r, nch)
            da = w0 * _shift_up(ext, 2, r) + w1 * _shift_up(ext, 1, r) + w2 * ext[:r]
            da_ref[c * r:(c + 1) * r, :] = da.astype(da_ref.dtype)
        dw_ref[0:1, :] = dw0
        dw_ref[1:2, :] = dw1
        dw_ref[2:3, :] = dw2
        db_ref[...] = db

    col = lambda j: (0, j)
    return pl.pallas_call(
        body, name=name, grid=(ncol,),
        in_specs=[pl.BlockSpec((T, tc), col), pl.BlockSpec((T, tc), lambda j: (0, j + ncol)), pl.BlockSpec((T, tc), col),
                  pl.BlockSpec((3, tc), col), pl.BlockSpec((1, tc), col)],
        out_specs=[pl.BlockSpec((T, tc), col), pl.BlockSpec((T, tc), col), pl.BlockSpec((3, tc), col), pl.BlockSpec((1, tc), col)],
        out_shape=[jax.ShapeDtypeStruct((T, D_FF), BF16), jax.ShapeDtypeStruct((T, D_FF), BF16),
                   jax.ShapeDtypeStruct((3, D_FF), F32), jax.ShapeDtypeStruct((1, D_FF), F32)],
        scratch_shapes=[pltpu.VMEM((T, tc), F32)],
        compiler_params=_cparams(("parallel",)),
    )(au, au, dz, conv_w, conv_b)


HGRN_BLOCK = 256


def _hgrn_chunk(dot, cumsum, q, f, i, g, lb, ng, st):
    C = q.shape[0]
    forget = lb + (1.0 - lb) * _sigmoid(f)
    k = 1.0 - forget
    b = cumsum(jnp.log(forget))
    b_last = b[C - 1:C, :]
    qd = q * jnp.exp(b)
    kd = k * jnp.exp(-b)
    att = jnp.where(_tri(C), dot(qd, kd, 1, 1), 0.0)
    o = dot(att, i, 1, 0) + dot(qd, st, 1, 1)
    st_new = st * jnp.exp(b_last) + dot(i, k * jnp.exp(b_last - b), 0, 0)
    on = o * lax.rsqrt(jnp.mean(o * o, axis=-1, keepdims=True) + EPS) * ng
    return on * _silu(g), st_new


def _hgrn_specs(T, rev):
    tb = min(HGRN_BLOCK, T)
    nb = T // tb
    blk = (lambda n: nb - 1 - n) if rev else (lambda n: n)
    hd = HGRN_HEAD_DIM
    proj_specs = [pl.BlockSpec((tb, HGRN_DIM), lambda n, k=k: (blk(n), k)) for k in range(4)]
    vec_spec = pl.BlockSpec((1, HGRN_DIM), lambda n: (0, 0))
    tok_spec = pl.BlockSpec((tb, HGRN_DIM), lambda n: (blk(n), 0))
    st_spec = pl.BlockSpec((HGRN_HEADS, tb // HGRN_CHUNK, hd, hd), lambda n: (0, blk(n), 0, 0))
    return tb, nb, proj_specs, vec_spec, tok_spec, st_spec


def _head_cols(h):
    return slice(h * HGRN_HEAD_DIM, (h + 1) * HGRN_HEAD_DIM)


def _hgrn_fwd(proj, lb, ng, *, name):
    T = proj.shape[0]
    tb, nb, proj_specs, vec_spec, tok_spec, st_spec = _hgrn_specs(T, False)
    nsub = tb // HGRN_CHUNK
    hd = HGRN_HEAD_DIM

    def body(q_ref, f_ref, i_ref, g_ref, lb_ref, ng_ref, y_ref, sts_ref, st_ref):
        @pl.when(pl.program_id(0) == 0)
        def _():
            st_ref[...] = jnp.zeros_like(st_ref)

        st = [st_ref[h] for h in range(HGRN_HEADS)]
        for s in range(nsub):
            rows = slice(s * HGRN_CHUNK, (s + 1) * HGRN_CHUNK)
            for h in range(HGRN_HEADS):
                cols = _head_cols(h)
                sts_ref[h, s] = st[h]
                y, st[h] = _hgrn_chunk(_mxu, _cumsum_rows, q_ref[rows, cols], f_ref[rows, cols], i_ref[rows, cols],
                                       g_ref[rows, cols], lb_ref[:, cols], ng_ref[:, cols], st[h])
                y_ref[rows, cols] = y
        for h in range(HGRN_HEADS):
            st_ref[h] = st[h]

    return pl.pallas_call(
        body, name=name, grid=(nb,),
        in_specs=proj_specs + [vec_spec, vec_spec], out_specs=[tok_spec, st_spec],
        out_shape=[jax.ShapeDtypeStruct((T, HGRN_DIM), F32),
                   jax.ShapeDtypeStruct((HGRN_HEADS, T // HGRN_CHUNK, hd, hd), F32)],
        scratch_shapes=[pltpu.VMEM((HGRN_HEADS, hd, hd), F32)],
        compiler_params=_cparams(("arbitrary",)),
    )(proj, proj, proj, proj, lb, ng)


def _hgrn_bwd(proj, lb, ng, states, dmix, *, name):
    T = proj.shape[0]
    tb, nb, proj_specs, vec_spec, tok_spec, st_spec = _hgrn_specs(T, True)
    nsub = tb // HGRN_CHUNK
    hd = HGRN_HEAD_DIM
    chunk = functools.partial(_hgrn_chunk, _mxu_ad, _cumsum_rows_ad)

    def body(q_ref, f_ref, i_ref, g_ref, lb_ref, ng_ref, sts_ref, dy_ref,
             dq_ref, df_ref, di_ref, dg_ref, dlb_ref, dng_ref, dst_ref):
        @pl.when(pl.program_id(0) == 0)
        def _():
            dst_ref[...] = jnp.zeros_like(dst_ref)
            dlb_ref[...] = jnp.zeros_like(dlb_ref)
            dng_ref[...] = jnp.zeros_like(dng_ref)

        dst = [dst_ref[h] for h in range(HGRN_HEADS)]
        dlb = [jnp.zeros((1, hd), F32)] * HGRN_HEADS
        dng = [jnp.zeros((1, hd), F32)] * HGRN_HEADS
        for s in reversed(range(nsub)):
            rows = slice(s * HGRN_CHUNK, (s + 1) * HGRN_CHUNK)
            for h in range(HGRN_HEADS):
                cols = _head_cols(h)
                _, vjp = jax.vjp(chunk, q_ref[rows, cols], f_ref[rows, cols], i_ref[rows, cols], g_ref[rows, cols],
                                 lb_ref[:, cols], ng_ref[:, cols], sts_ref[h, s])
                dq, df, di, dg, dlb_s, dng_s, dst[h] = vjp((dy_ref[rows, cols], dst[h]))
                dq_ref[rows, cols] = dq
                df_ref[rows, cols] = df
                di_ref[rows, cols] = di
                dg_ref[rows, cols] = dg
                dlb[h] = dlb[h] + dlb_s
                dng[h] = dng[h] + dng_s
        for h in range(HGRN_HEADS):
            dst_ref[h] = dst[h]
            dlb_ref[:, _head_cols(h)] += dlb[h]
            dng_ref[:, _head_cols(h)] += dng[h]

    tok_out = jax.ShapeDtypeStruct((T, HGRN_DIM), F32)
    vec_out = jax.ShapeDtypeStruct((1, HGRN_DIM), F32)
    return pl.pallas_call(
        body, name=name, grid=(nb,),
        in_specs=proj_specs + [vec_spec, vec_spec, st_spec, tok_spec],
        out_specs=[tok_spec] * 4 + [vec_spec, vec_spec],
        out_shape=[tok_out] * 4 + [vec_out, vec_out],
        scratch_shapes=[pltpu.VMEM((HGRN_HEADS, hd, hd), F32)],
        compiler_params=_cparams(("arbitrary",)),
    )(proj, proj, proj, proj, lb, ng, states, dmix)


S5_LANES = 512
S5_ROWS = 512


def _cmul(ar, ai, br, bi):
    return ar * br - ai * bi, ar * bi + ai * br


def _power_table(ar, ai, exps):
    a2 = _cmul(ar, ai, ar, ai)
    a4 = _cmul(*a2, *a2)
    e = exps - 1
    pr = jnp.broadcast_to(ar, exps.shape)
    pi = jnp.broadcast_to(ai, exps.shape)
    for bit, (fr, fi) in enumerate(((ar, ai), a2, a4)):
        nr, ni = _cmul(pr, pi, fr, fi)
        on = ((e >> bit) & 1) == 1
        pr, pi = jnp.where(on, nr, pr), jnp.where(on, ni, pi)
    return pr, pi, a2, a4


def _s5_scan_fwd(x, x_cols, b_re, b_im, a_re, a_im, *, name):
    T = x.shape[0]
    w, tr = S5_LANES, min(S5_ROWS, T)
    ncol, nt = S5_WIDTH // w, T // tr
    assert x_cols[0] % x_cols[1] == 0

    def body(x_ref, br_ref, bi_ref, ar_ref, ai_ref, sr_ref, si_ref, carry_ref):
        @pl.when(pl.program_id(1) == 0)
        def _():
            carry_ref[...] = jnp.zeros_like(carry_ref)

        u = x_ref[...].astype(_MXU_DTYPE)
        sr_ref[...] = _mxu(u, br_ref[...], 1, 0)
        si_ref[...] = _mxu(u, bi_ref[...], 1, 0)
        ar, ai = ar_ref[...], ai_ref[...]
        rowi = lax.broadcasted_iota(jnp.int32, (SUBLANE, w), 0)
        pr, pi, a2, a4 = _power_table(ar, ai, rowi + 1)
        steps = [(s, jnp.where(rowi >= s, fr, 0.0), jnp.where(rowi >= s, fi, 0.0)) for s, (fr, fi) in ((1, (ar, ai)), (2, a2), (4, a4))]

        def tile(i, carry):
            cr, ci = carry
            rows = pl.ds(pl.multiple_of(i * SUBLANE, SUBLANE), SUBLANE)
            xr, xi = sr_ref[rows, :], si_ref[rows, :]
            for s, fr, fi in steps:
                zr, zi = pltpu.roll(xr, s, 0), pltpu.roll(xi, s, 0)
                xr, xi = xr + fr * zr - fi * zi, xi + fr * zi + fi * zr
            xr, xi = xr + pr * cr - pi * ci, xi + pr * ci + pi * cr
            sr_ref[rows, :] = xr
            si_ref[rows, :] = xi
            return xr[SUBLANE - 1:SUBLANE, :], xi[SUBLANE - 1:SUBLANE, :]

        cr, ci = lax.fori_loop(0, tr // SUBLANE, tile, (carry_ref[0:1, :], carry_ref[1:2, :]))
        carry_ref[0:1, :] = cr
        carry_ref[1:2, :] = ci

    out = jax.ShapeDtypeStruct((T, S5_WIDTH), F32)
    return pl.pallas_call(
        body, name=name, grid=(ncol, nt),
        in_specs=[pl.BlockSpec((tr, x_cols[1]), lambda j, t: (t, x_cols[0] // x_cols[1])),
                  pl.BlockSpec((S5_DIM, w), lambda j, t: (0, j)), pl.BlockSpec((S5_DIM, w), lambda j, t: (0, j)),
                  pl.BlockSpec((1, w), lambda j, t: (0, j)), pl.BlockSpec((1, w), lambda j, t: (0, j))],
        out_specs=[pl.BlockSpec((tr, w), lambda j, t: (t, j))] * 2,
        out_shape=[out, out],
        scratch_shapes=[pltpu.VMEM((2, w), F32)],
        compiler_params=_cparams(("parallel", "arbitrary")),
    )(x, b_re, b_im, a_re, a_im)


def _s5_scan_bwd(dy, c_re, c_im, s_re, s_im, a_re, a_im, *, name):
    T = dy.shape[0]
    w, tr = S5_LANES, min(S5_ROWS, T)
    ncol, nt = S5_WIDTH // w, T // tr
    ntile = tr // SUBLANE

    def body(dy_ref, cr_ref, ci_ref, sr_ref, si_ref, ar_ref, ai_ref, lr_ref, li_ref, dar_ref, dai_ref, carry_ref):
        @pl.when(pl.program_id(1) == 0)
        def _():
            carry_ref[...] = jnp.zeros_like(carry_ref)
            dar_ref[...] = jnp.zeros_like(dar_ref)
            dai_ref[...] = jnp.zeros_like(dai_ref)

        dyb = dy_ref[...].astype(_MXU_DTYPE)
        lr_ref[...] = _mxu(dyb, cr_ref[...], 1, 1)
        li_ref[...] = _mxu(dyb, ci_ref[...], 1, 1)
        ar, ai = ar_ref[...], -ai_ref[...]
        rowi = lax.broadcasted_iota(jnp.int32, (SUBLANE, w), 0)
        pr, pi, a2, a4 = _power_table(ar, ai, SUBLANE - rowi)
        last = rowi == SUBLANE - 1
        steps = [(s, jnp.where(rowi < SUBLANE - s, fr, 0.0), jnp.where(rowi < SUBLANE - s, fi, 0.0))
                 for s, (fr, fi) in ((1, (ar, ai)), (2, a2), (4, a4))]

        def tile(i, carry):
            cr, ci, dar, dai = carry
            rows = pl.ds(pl.multiple_of((ntile - 1 - i) * SUBLANE, SUBLANE), SUBLANE)
            xr, xi = lr_ref[rows, :], li_ref[rows, :]
            for s, fr, fi in steps:
                zr, zi = pltpu.roll(xr, SUBLANE - s, 0), pltpu.roll(xi, SUBLANE - s, 0)
                xr, xi = xr + fr * zr - fi * zi, xi + fr * zi + fi * zr
            xr, xi = xr + pr * cr - pi * ci, xi + pr * ci + pi * cr
            lr_ref[rows, :] = xr
            li_ref[rows, :] = xi
            nr = jnp.where(last, cr, pltpu.roll(xr, SUBLANE - 1, 0))
            ni = jnp.where(last, ci, pltpu.roll(xi, SUBLANE - 1, 0))
            sr, si = sr_ref[rows, :], si_ref[rows, :]
            return xr[0:1, :], xi[0:1, :], dar + nr * sr + ni * si, dai + ni * sr - nr * si

        cr, ci, dar, dai = lax.fori_loop(
            0, ntile, tile, (carry_ref[0:1, :], carry_ref[1:2, :], jnp.zeros((SUBLANE, w), F32), jnp.zeros((SUBLANE, w), F32)))
        carry_ref[0:1, :] = cr
        carry_ref[1:2, :] = ci
        dar_ref[...] += dar
        dai_ref[...] += dai

    tok = pl.BlockSpec((tr, w), lambda j, t: (nt - 1 - t, j))
    vec = pl.BlockSpec((1, w), lambda j, t: (0, j))
    acc = pl.BlockSpec((SUBLANE, w), lambda j, t: (0, j))
    out = jax.ShapeDtypeStruct((T, S5_WIDTH), F32)
    accs = jax.ShapeDtypeStruct((SUBLANE, S5_WIDTH), F32)
    return pl.pallas_call(
        body, name=name, grid=(ncol, nt),
        in_specs=[pl.BlockSpec((tr, S5_DIM), lambda j, t: (nt - 1 - t, 0)),
                  pl.BlockSpec((w, S5_DIM), lambda j, t: (j, 0)), pl.BlockSpec((w, S5_DIM), lambda j, t: (j, 0)),
                  tok, tok, vec, vec],
        out_specs=[tok, tok, acc, acc],
        out_shape=[out, out, accs, accs],
        scratch_shapes=[pltpu.VMEM((2, w), F32)],
        compiler_params=_cparams(("parallel", "arbitrary")),
    )(dy, c_re, c_im, s_re, s_im, a_re, a_im)


ATTN_BLOCK = 512
_NEG = -1e30


def _qk_cat(nope, rope):
    return jnp.concatenate([nope.astype(_MXU_DTYPE), rope.astype(_MXU_DTYPE)], axis=1)


_QK_SCALE = MLA_QK ** -0.5
_LOG2E = math.log2(math.e)


def _attn_scores(q, k, diagonal):
    s = _mxu(q, k, 1, 1) * (_QK_SCALE * _LOG2E)
    if diagonal:
        s = jnp.where(_tri(s.shape[0]), s, _NEG)
    return s


def _attn_fwd(q_all, q_rope, kv, k_rope, *, name):
    T = q_all.shape[0]
    tq = min(ATTN_BLOCK, T)
    nq = T // tq

    def body(qn_ref, qr_ref, kn_ref, v_ref, kr_ref, o_ref, lse_ref):
        i = pl.program_id(1)
        q = _qk_cat(qn_ref[...], qr_ref[0])

        def step(j, carry, diagonal):
            m, l, acc = carry
            ks = pl.ds(pl.multiple_of(j * tq, tq), tq)
            s = _attn_scores(q, _qk_cat(kn_ref[ks, :], kr_ref[ks, :]), diagonal)
            m_new = jnp.maximum(m, jnp.max(s, axis=-1, keepdims=True))
            p = jnp.exp2(s - m_new)
            alpha = jnp.exp2(m - m_new)
            return m_new, alpha * l + jnp.sum(p, axis=-1, keepdims=True), alpha * acc + _mxu(p, v_ref[ks, :], 1, 0)

        init = (jnp.full((tq, 1), _NEG, F32), jnp.zeros((tq, 1), F32), jnp.zeros((tq, MLA_V), F32))
        below = lax.fori_loop(0, i, functools.partial(step, diagonal=False), init)
        m, l, acc = step(i, below, diagonal=True)
        o_ref[...] = acc / l
        lse_ref[0] = m + jnp.log2(l)

    return pl.pallas_call(
        body, name=name, grid=(MLA_HEADS, nq),
        in_specs=[pl.BlockSpec((tq, MLA_NOPE), lambda h, i: (i, h)), pl.BlockSpec((1, tq, MLA_ROPE), lambda h, i: (h, i, 0)),
                  pl.BlockSpec((T, MLA_NOPE), lambda h, i: (0, 2 * h)), pl.BlockSpec((T, MLA_V), lambda h, i: (0, 2 * h + 1)),
                  pl.BlockSpec((T, MLA_ROPE), lambda h, i: (0, 0))],
        out_specs=[pl.BlockSpec((tq, MLA_V), lambda h, i: (i, h)), pl.BlockSpec((1, tq, 1), lambda h, i: (h, i, 0))],
        out_shape=[jax.ShapeDtypeStruct((T, MLA_HEADS * MLA_V), F32), jax.ShapeDtypeStruct((MLA_HEADS, T, 1), F32)],
        compiler_params=_cparams(("arbitrary", "arbitrary")),
    )(q_all, q_rope, kv, kv, k_rope)


def _attn_bwd(q_all, q_rope, kv, k_rope, o, lse, do, *, name):
    T = q_all.shape[0]
    tk = min(ATTN_BLOCK, T)
    nk = T // tk

    def body(qn_ref, qr_ref, kv_ref, kr_ref, o_ref, lse_ref, do_ref, dqn_ref, dqr_ref, dkv_ref, dkr_ref, delta_ref):
        h, j = pl.program_id(0), pl.program_id(1)

        @pl.when(j == 0)
        def _():
            dqn_ref[...] = jnp.zeros_like(dqn_ref)
            dqr_ref[...] = jnp.zeros_like(dqr_ref)
            delta_ref[...] = jnp.sum(do_ref[...] * o_ref[...], axis=-1, keepdims=True)

        @pl.when((j == 0) & (h == 0))
        def _():
            dkr_ref[...] = jnp.zeros_like(dkr_ref)

        krows = pl.ds(pl.multiple_of(j * tk, tk), tk)
        k = _qk_cat(kv_ref[:, :MLA_NOPE], kr_ref[krows, :])
        v = kv_ref[:, MLA_NOPE:].astype(_MXU_DTYPE)

        def step(i, carry, diagonal):
            dk, dv = carry
            qs = pl.ds(pl.multiple_of(i * tk, tk), tk)
            q, dob = _qk_cat(qn_ref[qs, :], qr_ref[0, qs, :]), do_ref[qs, :].astype(_MXU_DTYPE)
            p = jnp.exp2(_attn_scores(q, k, diagonal) - lse_ref[0, qs, :])
            ds = p * (_mxu(dob, v, 1, 1) - delta_ref[qs, :]) * _QK_SCALE
            dq = _mxu(ds, k, 1, 0)
            dqn_ref[qs, :] += dq[:, :MLA_NOPE]
            dqr_ref[0, qs, :] += dq[:, MLA_NOPE:]
            return dk + _mxu(ds, q, 0, 0), dv + _mxu(p, dob, 0, 0)

        on_diagonal = step(j, (jnp.zeros((tk, MLA_QK), F32), jnp.zeros((tk, MLA_V), F32)), diagonal=True)
        dk, dv = lax.fori_loop(j + 1, nk, functools.partial(step, diagonal=False), on_diagonal)
        dkv_ref[:, :MLA_NOPE] = dk[:, :MLA_NOPE].astype(dkv_ref.dtype)
        dkv_ref[:, MLA_NOPE:] = dv.astype(dkv_ref.dtype)
        dkr_ref[krows, :] += dk[:, MLA_NOPE:]

    head_cols = pl.BlockSpec((T, MLA_NOPE), lambda h, j: (0, h))
    head_rope = pl.BlockSpec((1, T, MLA_ROPE), lambda h, j: (h, 0, 0))
    kv_spec = pl.BlockSpec((tk, MLA_NOPE + MLA_V), lambda h, j: (j, h))
    kr_spec = pl.BlockSpec((T, MLA_ROPE), lambda h, j: (0, 0))
    return pl.pallas_call(
        body, name=name, grid=(MLA_HEADS, nk),
        in_specs=[head_cols, head_rope, kv_spec, kr_spec, head_cols, pl.BlockSpec((1, T, 1), lambda h, j: (h, 0, 0)), head_cols],
        out_specs=[head_cols, head_rope, kv_spec, kr_spec],
        out_shape=[jax.ShapeDtypeStruct((T, MLA_HEADS * MLA_NOPE), F32), jax.ShapeDtypeStruct((MLA_HEADS, T, MLA_ROPE), F32),
                   jax.ShapeDtypeStruct((T, MLA_HEADS * (MLA_NOPE + MLA_V)), BF16), jax.ShapeDtypeStruct((T, MLA_ROPE), F32)],
        scratch_shapes=[pltpu.VMEM((T, 1), F32)],
        compiler_params=_cparams(("arbitrary", "arbitrary")),
    )(q_all, q_rope, kv, k_rope, o, lse, do)


def _s5_discretize(a_re, a_im, log_dt, bt_re, bt_im, lb_logits):
    dt = jnp.exp(log_dt)
    mag = jnp.exp(a_re * dt)
    abr, abi = mag * jnp.cos(a_im * dt), mag * jnp.sin(a_im * dt)
    den = a_re * a_re + a_im * a_im
    xr, xi = abr - 1.0, abi
    cr = ((xr * a_re + xi * a_im) / den)[:, None, :]
    ci = ((xi * a_re - xr * a_im) / den)[:, None, :]
    e = jnp.exp(lb_logits - jnp.max(lb_logits, axis=0, keepdims=True))
    lb = e[0:1, :] / jnp.sum(e, axis=0, keepdims=True)
    return abr, abi, cr * bt_re - ci * bt_im, cr * bt_im + ci * bt_re, lb


def _whole(shape):
    return pl.BlockSpec(shape, lambda: (0,) * len(shape))


def _s5_params_fwd(a_re, a_im, log_dt, bt_re, bt_im, lb_logits):
    ins = (a_re, a_im, log_dt, bt_re, bt_im, lb_logits)
    outs = [jax.ShapeDtypeStruct(s, F32) for s in (a_re.shape, a_re.shape, bt_re.shape, bt_re.shape, (1, lb_logits.shape[1]))]

    def body(*refs):
        res = _s5_discretize(*[r[...] for r in refs[:6]])
        for o_ref, o in zip(refs[6:], res):
            o_ref[...] = o

    return pl.pallas_call(body, name="s5_params_fwd", in_specs=[_whole(a.shape) for a in ins],
                          out_specs=[_whole(o.shape) for o in outs], out_shape=outs, compiler_params=_cparams())(*ins)


def _s5_params_bwd(a_re, a_im, log_dt, bt_re, bt_im, lb_logits, d_abr, d_abi, d_bbr, d_bbi, d_lb):
    ins = (a_re, a_im, log_dt, bt_re, bt_im, lb_logits, d_abr, d_abi, d_bbr, d_bbi, d_lb)
    outs = [jax.ShapeDtypeStruct(a.shape, F32) for a in ins[:6]]

    def body(*refs):
        _, vjp = jax.vjp(_s5_discretize, *[r[...] for r in refs[:6]])
        for o_ref, o in zip(refs[11:], vjp(tuple(r[...] for r in refs[6:11]))):
            o_ref[...] = o

    return pl.pallas_call(body, name="s5_params_bwd", in_specs=[_whole(a.shape) for a in ins],
                          out_specs=[_whole(o.shape) for o in outs], out_shape=outs, compiler_params=_cparams())(*ins)


ADAMW_WHOLE_BYTES = 1024 * 1024


def _adamw(w, g, m, v, *, name):
    R, C = w.shape
    whole = R % SUBLANE != 0 or R * C * w.dtype.itemsize <= ADAMW_WHOLE_BYTES
    tr = R if whole else _pick(R, (256, 128, 64, 32, 16, 8))
    slabs = g.shape[0] if g.ndim == 3 else 0

    def body(w_ref, g_ref, m_ref, v_ref, *outs):
        if slabs:
            gv = g_ref[0].astype(F32)
            for s in range(1, slabs):
                gv = gv + g_ref[s].astype(F32)
            outs[0][...] = gv
            outs = outs[1:]
        else:
            gv = g_ref[...]
        d_ref, mo_ref, vo_ref = outs
        m2 = ADAM_B1 * m_ref[...] + (1.0 - ADAM_B1) * gv
        v2 = ADAM_B2 * v_ref[...] + (1.0 - ADAM_B2) * (gv * gv)
        m_hat = m2 / (1.0 - ADAM_B1 ** ADAM_STEP)
        v_hat = v2 / (1.0 - ADAM_B2 ** ADAM_STEP)
        d_ref[...] = -ADAM_LR * (m_hat / (jnp.sqrt(v_hat) + ADAM_EPS) + ADAM_WD * w_ref[...])
        mo_ref[...] = m2
        vo_ref[...] = v2

    spec = pl.BlockSpec((tr, C), lambda i: (i, 0))
    g_spec = pl.BlockSpec((slabs, tr, C), lambda i: (0, i, 0)) if slabs else spec
    out = jax.ShapeDtypeStruct((R, C), F32)
    n_out = 4 if slabs else 3
    return pl.pallas_call(body, name=name, grid=(R // tr,), in_specs=[spec, g_spec, spec, spec], out_specs=[spec] * n_out,
                          out_shape=[out] * n_out, compiler_params=_cparams(("parallel",)))(w, g, m, v)


N_CHIPS = 4
N_CORES = 2


_FLIPS = tuple((dx, dy, dc) for dx in (0, 1) for dy in (0, 1) for dc in (0, 1) if (dx, dy, dc) != (0, 0, 0))


_HBM = pl.BlockSpec(memory_space=pltpu.HBM)
_SEM = pl.BlockSpec(memory_space=pltpu.SEMAPHORE)
_SPLIT_COPY = pltpu.CompilerParams(has_side_effects=pltpu.SideEffectType.DATAFLOW_SIDE_EFFECTING)


def _exchange_copies(src_refs, land_refs, send_sems, recv_sems, scatter, arriving):
    x, y, c = lax.axis_index("x"), lax.axis_index("y"), lax.axis_index("c")
    me_chip = 2 * x + y
    copies = []
    for a, (s_ref, l_ref) in enumerate(zip(src_refs, land_refs)):
        for j, (dx, dy, dc) in enumerate(_FLIPS):
            px, py, pc = (1 - x if dx else x), (1 - y if dy else y), (1 - c if dc else c)
            k = a * len(_FLIPS) + j
            p_chip = 2 * px + py
            copies.append(pltpu.make_async_remote_copy(
                src_ref=s_ref.at[p_chip, pc] if scatter else s_ref, dst_ref=l_ref.at[p_chip, pc] if arriving else l_ref.at[me_chip, c],
                send_sem=send_sems.at[k], recv_sem=recv_sems.at[k], device_id=(px, py, pc), device_id_type=MESH))
    return copies


def _exchange_start(srcs, *, scatter, name, after=()):
    n_arr = len(srcs)
    n_sem = n_arr * len(_FLIPS)
    n_in = 2 * n_arr + len(after)
    lands = [lax.empty(s.shape if scatter else (N_CHIPS, N_CORES) + s.shape, s.dtype) for s in srcs]

    def body(*refs):
        src_refs, land_refs = refs[:n_arr], refs[n_arr:2 * n_arr]
        for cp in _exchange_copies(src_refs, land_refs, refs[n_in], refs[n_in + 1], scatter, arriving=False):
            cp.start()
        refs[-1][...] = jnp.zeros_like(refs[-1])

    thru = [pltpu.HBM(a.shape, a.dtype) for a in srcs + lands]
    outs = pl.pallas_call(
        body, name=name,
        out_shape=(pltpu.SemaphoreType.DMA((n_sem,)), pltpu.SemaphoreType.DMA((n_sem,)), *thru,
                   jax.ShapeDtypeStruct((SUBLANE, LANE), F32)),
        in_specs=[_HBM] * (2 * n_arr) + [pl.BlockSpec(memory_space=pl.ANY)] * len(after),
        out_specs=(_SEM, _SEM, *[_HBM] * (2 * n_arr), pl.BlockSpec(memory_space=pltpu.VMEM)),
        input_output_aliases={i: 2 + i for i in range(2 * n_arr)}, compiler_params=_SPLIT_COPY,
    )(*[pltpu.with_memory_space_constraint(a, pltpu.HBM) for a in srcs + lands], *after)
    return outs[0], outs[1], list(outs[2:2 + n_arr]), list(outs[2 + n_arr:2 + 2 * n_arr]), outs[-1]


def _exchange_wait(started, after, *, scatter, name):
    send_sems, recv_sems, srcs, lands, _ = started
    n_arr = len(srcs)

    def body(*refs):
        src_refs, land_refs = refs[:n_arr], refs[n_arr:2 * n_arr]
        for cp in _exchange_copies(src_refs, land_refs, refs[2 * n_arr], refs[2 * n_arr + 1], scatter, arriving=True):
            cp.wait_send()
            cp.wait_recv()

    outs = pl.pallas_call(
        body, name=name, out_shape=[pltpu.HBM(a.shape, a.dtype) for a in srcs + lands],
        in_specs=[_HBM] * (2 * n_arr) + [_SEM, _SEM, pl.BlockSpec(memory_space=pl.ANY)], out_specs=[_HBM] * (2 * n_arr),
        input_output_aliases={i: i for i in range(2 * n_arr)}, compiler_params=_SPLIT_COPY,
    )(*srcs, *lands, send_sems, recv_sems, after)
    return list(outs[:n_arr]), list(outs[n_arr:])


def _with_own(land, own):
    me_chip = 2 * lax.axis_index("x") + lax.axis_index("y")
    return lax.dynamic_update_slice(land, own[None, None], (me_chip, lax.axis_index("c")) + (0,) * own.ndim)


def _rms_fwd_fn(h, g):
    return (_rms(h, g),)


def _rms_bwd_fn(h, dhn, dres, g):
    _, vjp = jax.vjp(_rms, h, g)
    dh, dg = vjp(dhn)
    return dh + dres, dg


def _loss_fn(h, tgt, g):
    y, vjp = jax.vjp(_rms, h, g)
    diff = y - tgt
    dh, dg = vjp(diff * (1.0 / D_MODEL))
    return dh, dg, (0.5 / D_MODEL) * jnp.sum(diff * diff, axis=0, keepdims=True)


def _s5_act(ys, u, d):
    return _gelu(ys + d * u)


def _s5_gate(z, gl, b):
    return z * _sigmoid(gl + b)


def _s5_act_fn(ys, u, d):
    return (_s5_act(ys, u, d),)


def _s5_mix_fn(ya, z, gl, b):
    return (jnp.concatenate([ya, _s5_gate(z, gl, b)], axis=1),)


def _s5_gate_bwd_fn(z, gl, dyb, b):
    _, vjp = jax.vjp(_s5_gate, z, gl, b)
    return vjp(dyb)


def _s5_act_bwd_fn(ys, u, dz1, dz2, d):
    _, vjp = jax.vjp(_s5_act, ys, u, d)
    return vjp(dz1 + dz2)


def _dproj_fn(dq, df, di, dg, du1, du2):
    return (jnp.concatenate([dq, df, di, dg, du1 + du2], axis=1),)


def _rope_pair(r1, r2, pos, freqs):
    ang = pos.astype(F32) * freqs
    c, s = jnp.cos(ang), jnp.sin(ang)
    return r1 * c - r2 * s, r1 * s + r2 * c


_ODD_SPLITS = (0, MLA_Q_RANK, MLA_Q_RANK + MLA_KV_RANK, MLA_Q_RANK + MLA_KV_RANK + LANE, ODD_IN_PAD)


def _mla_prep(cq, ckv, k1, k2, qg, kvg, pos, freqs):
    ko1, ko2 = _rope_pair(k1, k2, pos, freqs)
    return _rms(cq, qg), _rms(ckv, kvg), ko1, ko2


def _mla_prep_fn(proj, pos, qg, kvg, freqs):
    parts = [proj[:, a:b] for a, b in zip(_ODD_SPLITS[:-1], _ODD_SPLITS[1:])]
    return _mla_prep(*parts, qg, kvg, pos, freqs)


def _mla_prep_bwd_fn(proj, pos, dqn, dkvn, dko1, dko2, qg, kvg, freqs):
    parts = [proj[:, a:b] for a, b in zip(_ODD_SPLITS[:-1], _ODD_SPLITS[1:])]
    _, vjp = jax.vjp(lambda *a: _mla_prep(*a, pos, freqs), *parts, qg, kvg)
    dcq, dckv, dk1, dk2, dqg, dkvg = vjp((dqn, dkvn, dko1, dko2))
    return jnp.concatenate([dcq, dckv, dk1, dk2], axis=1), dqg, dkvg


def _rope_q_fn(r1, r2, pos, freqs):
    return _rope_pair(r1, r2, pos, freqs)


def _rope_q_bwd_fn(dqn, do1, do2, pos, freqs):
    dr1, dr2 = _rope_pair(do1, do2, pos, -freqs)
    return (jnp.concatenate([dqn, dr1, dr2], axis=1),)


W_NAMES = ("norm_mix_g", "norm_ffn_g", "final_norm_g", "even_w_in", "hgrn_lb_logits", "hgrn_norm_g", "s5_a_re", "s5_a_im",
           "s5_log_dt", "s5_b_re", "s5_b_im", "s5_c_re", "s5_c_im", "s5_d", "s5_w_glu", "s5_b_glu", "even_w_out", "odd_w_in",
           "mla_q_norm_g", "mla_w_uq", "mla_kv_norm_g", "mla_w_ukv", "odd_w_out", "ffn_w_in", "ffn_conv_w", "ffn_conv_b",
           "ffn_w_out")
BIG_UNITS = (("even_w_in", 0, "col"), ("s5_w_glu", 0, "row"), ("even_w_out", 0, "row"), ("odd_w_in", 0, "row"),
             ("mla_w_uq", 0, "col"), ("mla_w_ukv", 0, "col"), ("odd_w_out", 0, "row"),
             ("ffn_w_in", 0, "col"), ("ffn_w_in", 1, "col"), ("ffn_w_out", 0, "row"), ("ffn_w_out", 1, "row"))
BIG_NAMES = tuple(dict.fromkeys(u[0] for u in BIG_UNITS))
SMALL_SHARDED = (("mla_q_norm_g", 1), ("mla_kv_norm_g", 1), ("ffn_conv_w", 2))
SMALL_SHARDED_NAMES = tuple(s[0] for s in SMALL_SHARDED)
REPLICATED = tuple(n for n in W_NAMES if n not in BIG_NAMES + SMALL_SHARDED_NAMES)
REPLICATED_LATE = ("norm_mix_g",)
REPLICATED_EARLY = tuple(n for n in REPLICATED if n not in REPLICATED_LATE)


def _pack(flats, cols, row_mult):
    flat = jnp.concatenate(flats, axis=-1)
    pad = (-flat.shape[-1]) % (cols * row_mult)
    flat = jnp.pad(flat, [(0, 0)] * (flat.ndim - 1) + [(0, pad)])
    return flat.reshape(flat.shape[:-1] + (-1, cols))


def _unpack(flat, shapes):
    out, off = [], 0
    for shp in shapes:
        n = int(np.prod(shp))
        out.append(flat[..., off:off + n].reshape(flat.shape[:-1] + tuple(shp)))
        off += n
    return out


UNIT_KIND = {(n, l): kind for n, l, kind in BIG_UNITS}
STAGES = ((("even_w_in", 0),),
          (("s5_w_glu", 0), ("even_w_out", 0)),
          (("ffn_w_in", 0), ("ffn_w_out", 0)),
          (("odd_w_in", 0), ("mla_w_uq", 0), ("mla_w_ukv", 0), ("odd_w_out", 0)),
          (("ffn_w_in", 1), ("ffn_w_out", 1)))


def _gather_start(w, stage, with_small, after):
    srcs = [w[n][l].astype(BF16) for n, l in STAGES[stage]]
    if with_small:
        srcs.append(_pack([w[n].reshape(-1) for n in SMALL_SHARDED_NAMES], LANE, SUBLANE))
    return _exchange_start(srcs, scatter=False, name=f"gather_start_{stage}", after=after)


def _gather_finish(started, after, w, stage, with_small):
    srcs, lands = _exchange_wait(started, after, scatter=False, name=f"gather_wait_{stage}")
    lands = [_with_own(land, src) for land, src in zip(lands, srcs)]
    big = {}
    for unit, g in zip(STAGES[stage], lands):
        r, c = g.shape[2:]
        big[unit] = g.reshape(N_DEV * r, c) if UNIT_KIND[unit] == "row" else g.transpose(2, 0, 1, 3).reshape(r, N_DEV * c)
    if not with_small:
        return big
    parts = _unpack(lands[-1].reshape(N_DEV, -1), [w[n].shape for n in SMALL_SHARDED_NAMES])
    small = {}
    for (n, ax), p in zip(SMALL_SHARDED, parts):
        shp = list(w[n].shape)
        shp[ax] *= N_DEV
        small[n] = jnp.moveaxis(p, 0, ax).reshape(shp)
    return big, small


def _scatter_start(g_big, stage, extra=()):
    srcs = []
    for unit in STAGES[stage]:
        g = g_big[unit].astype(BF16)
        if UNIT_KIND[unit] == "row":
            srcs.append(g.reshape(N_CHIPS, N_CORES, g.shape[0] // N_DEV, g.shape[1]))
        else:
            srcs.append(g.reshape(g.shape[0], N_CHIPS, N_CORES, g.shape[1] // N_DEV).transpose(1, 2, 0, 3))
    return _exchange_start(srcs + list(extra), scatter=True, name=f"scatter_start_{stage}")


def _scatter_finish(started, after, stage):
    srcs, lands = _exchange_wait(started, after, scatter=True, name=f"scatter_wait_{stage}")
    me_chip, c = 2 * lax.axis_index("x") + lax.axis_index("y"), lax.axis_index("c")
    outs = []
    for land, src in zip(lands, srcs):
        own = lax.dynamic_slice(src, (me_chip, c) + (0,) * (src.ndim - 2), (1, 1) + src.shape[2:])[0, 0]
        outs.append(_with_own(land, own).reshape((N_DEV,) + land.shape[2:]))
    return outs


def _small_sharded_pack(g_small, w):
    flats = []
    for n, ax in SMALL_SHARDED:
        shp = list(w[n].shape)
        g = g_small[n].astype(F32).reshape(shp[:ax] + [N_DEV] + shp[ax:])
        flats.append(jnp.moveaxis(g, ax, 0).reshape(N_DEV, -1))
    small = _pack(flats, LANE, SUBLANE)
    return small.reshape((N_CHIPS, N_CORES) + small.shape[1:])


def _replicated_pack(g_repl, names):
    vec = _pack([g_repl[n].reshape(-1).astype(F32) for n in names], LANE, SUBLANE)
    return jnp.broadcast_to(vec, (N_CHIPS, N_CORES) + vec.shape)


def _block_diag(blocks):
    G, a, b = blocks.shape
    return jnp.einsum('gab,gk->gakb', blocks, jnp.eye(G, dtype=blocks.dtype)).reshape(G * a, G * b)


def _diag_blocks(mat, a, b):
    G = mat.shape[0] // a
    return jnp.einsum('gagb->gab', mat.reshape(G, a, G, b))


def _ffn_fwd(h, g, w_in, conv_w, conv_b, w_out, tag):
    hn, = _rows(_rms_fwd_fn, [h], [g], [(D_MODEL, BF16)], [], name=f"ffn{tag}_norm")
    au = _mm(hn, w_in, out_dtype=BF16, name=f"ffn{tag}_in")
    z = _ffn_mid_fwd(au, conv_w, conv_b, name=f"ffn{tag}_mid")
    return _mm(z, w_out, res=h, name=f"ffn{tag}_out"), (hn, au, z)


def _ffn_bwd(h, dh, saved, g, w_in, conv_w, conv_b, w_out, tag, deps=()):
    hn, au, z = saved
    dz = _mm(dh, w_out, tb=True, out_dtype=BF16, deps=deps, name=f"ffn{tag}_dz")
    dw_out = _mm(z, dh, ta=True, out_dtype=BF16, name=f"ffn{tag}_dwout")
    da, du, dcw, dcb = _ffn_mid_bwd(au, dz, conv_w, conv_b, name=f"ffn{tag}_dmid")
    dhn = _mm(da, w_in, tb=True, b_cols=(0, D_FF), name=f"ffn{tag}_dhn_a")
    dhn = _mm(du, w_in, tb=True, b_cols=(D_FF, D_FF), res=dhn, name=f"ffn{tag}_dhn_u")
    dw_in = jnp.concatenate([_mm(hn, da, ta=True, out_dtype=BF16, name=f"ffn{tag}_dwin_a"),
                             _mm(hn, du, ta=True, out_dtype=BF16, name=f"ffn{tag}_dwin_u")], axis=1)
    dh_in, dg = _rows(_rms_bwd_fn, [h, dhn, dh], [g], [(D_MODEL, F32)], [(1, D_MODEL)], name=f"ffn{tag}_dnorm")
    return dh_in, dict(g=dg, w_in=dw_in, conv_w=dcw, conv_b=dcb, w_out=dw_out)


def kernel(x, positions, norm_mix_g, norm_ffn_g, final_norm_g, even_w_in, hgrn_lb_logits, hgrn_norm_g, s5_a_re, s5_a_im, s5_log_dt, s5_b_re, s5_b_im, s5_c_re, s5_c_im, s5_d, s5_w_glu, s5_b_glu, even_w_out, odd_w_in, mla_q_norm_g, mla_w_uq, mla_kv_norm_g, mla_w_ukv, odd_w_out, ffn_w_in, ffn_conv_w, ffn_conv_b, ffn_w_out, loss_target, m_norm_mix_g, m_norm_ffn_g, m_final_norm_g, m_even_w_in, m_hgrn_lb_logits, m_hgrn_norm_g, m_s5_a_re, m_s5_a_im, m_s5_log_dt, m_s5_b_re, m_s5_b_im, m_s5_c_re, m_s5_c_im, m_s5_d, m_s5_w_glu, m_s5_b_glu, m_even_w_out, m_odd_w_in, m_mla_q_norm_g, m_mla_w_uq, m_mla_kv_norm_g, m_mla_w_ukv, m_odd_w_out, m_ffn_w_in, m_ffn_conv_w, m_ffn_conv_b, m_ffn_w_out, v_norm_mix_g, v_norm_ffn_g, v_final_norm_g, v_even_w_in, v_hgrn_lb_logits, v_hgrn_norm_g, v_s5_a_re, v_s5_a_im, v_s5_log_dt, v_s5_b_re, v_s5_b_im, v_s5_c_re, v_s5_c_im, v_s5_d, v_s5_w_glu, v_s5_b_glu, v_even_w_out, v_odd_w_in, v_mla_q_norm_g, v_mla_w_uq, v_mla_kv_norm_g, v_mla_w_ukv, v_odd_w_out, v_ffn_w_in, v_ffn_conv_w, v_ffn_conv_b, v_ffn_w_out):
    given = dict(locals())
    w = {n: given[n] for n in W_NAMES}
    mom = {n: given["m_" + n] for n in W_NAMES}
    var = {n: given["v_" + n] for n in W_NAMES}
    T = x.shape[1]
    h0 = x[0]
    tgt = loss_target[0]
    pos = positions.reshape(T, 1)

    gathers = []
    for s in range(len(STAGES)):
        gathers.append(_gather_start(w, s, with_small=(s == 1), after=[g[4] for g in gathers[-1:]]))
    half = MLA_ROPE // 2
    kr0 = MLA_Q_RANK + MLA_KV_RANK
    freqs = ROPE_THETA ** (-jnp.arange(0, MLA_ROPE, 2, dtype=F32) / MLA_ROPE)
    freqs_q = jnp.tile(freqs, MLA_HEADS)[None, :]
    freqs_k = jnp.concatenate([freqs, jnp.zeros((LANE - half,), F32)])[None, :]

    sp_in = (s5_a_re[0], s5_a_im[0], s5_log_dt[0][:, None], s5_b_re[0].transpose(0, 2, 1), s5_b_im[0].transpose(0, 2, 1),
             hgrn_lb_logits)
    abr, abi, bbt_re, bbt_im, lb0 = _s5_params_fwd(*sp_in)
    a_re, a_im = abr.reshape(1, S5_WIDTH), abi.reshape(1, S5_WIDTH)
    bb_re, bb_im = _block_diag(bbt_re).astype(BF16), _block_diag(bbt_im).astype(BF16)
    c_re = _block_diag(s5_c_re[0].transpose(0, 2, 1)).astype(BF16)
    c_im_neg = _block_diag(-s5_c_im[0].transpose(0, 2, 1)).astype(BF16)
    u_cols = (4 * HGRN_DIM, S5_DIM)

    hn0, = _rows(_rms_fwd_fn, [h0], [norm_mix_g[0:1]], [(D_MODEL, BF16)], [], name="mix0_norm", deps=[gathers[-1][4]])
    full = _gather_finish(gathers[0], hn0, w, 0, False)
    w_ein = full["even_w_in", 0]
    proj = _mm(hn0, w_ein, name="even_in")
    y_a, states = _hgrn_fwd(proj, lb0, hgrn_norm_g, name="hgrn_fwd")
    s_re, s_im = _s5_scan_fwd(proj, u_cols, bb_re, bb_im, a_re, a_im, name="s5_scan_fwd")
    more, full_small = _gather_finish(gathers[1], s_re, w, 1, True)
    w_glu, w_eout = more["s5_w_glu", 0], more["even_w_out", 0]
    qg, kvg, conv_w = full_small["mla_q_norm_g"], full_small["mla_kv_norm_g"], full_small["ffn_conv_w"]
    ys = _mm(s_im, c_im_neg, res=_mm(s_re, c_re, name="s5_y_re"), name="s5_y_im")
    z5, = _rows(_s5_act_fn, [ys, (proj,) + u_cols], [s5_d], [(S5_DIM, F32)], [], name="s5_act")
    gl = _mm(z5, w_glu, name="s5_glu")
    mixin, = _rows(_s5_mix_fn, [y_a, z5, gl], [s5_b_glu], [(D_MODEL, BF16)], [], name="s5_mix")
    h1 = _mm(mixin, w_eout, res=h0, name="even_out")
    full.update(_gather_finish(gathers[2], h1, w, 2, False))
    w_fin, w_fout = [full["ffn_w_in", 0]], [full["ffn_w_out", 0]]
    h2, ffn0_saved = _ffn_fwd(h1, norm_ffn_g[0:1], w_fin[0], conv_w[0], ffn_conv_b[0:1], w_fout[0], 0)

    full.update(_gather_finish(gathers[3], h2, w, 3, False))
    w_oin, w_ukv, w_oout = full["odd_w_in", 0], full["mla_w_ukv", 0], full["odd_w_out", 0]
    zpad = jnp.zeros((D_MODEL, LANE - half), BF16)
    w_oin_pad = jnp.concatenate([w_oin[:, :kr0], w_oin[:, kr0:kr0 + half], zpad, w_oin[:, kr0 + half:], zpad], axis=1)
    w_uq3 = full["mla_w_uq", 0].reshape(MLA_Q_RANK, MLA_HEADS, MLA_QK)
    w_uq_perm = jnp.concatenate([w_uq3[:, :, :MLA_NOPE].reshape(MLA_Q_RANK, -1),
                                 w_uq3[:, :, MLA_NOPE:MLA_NOPE + half].reshape(MLA_Q_RANK, -1),
                                 w_uq3[:, :, MLA_NOPE + half:].reshape(MLA_Q_RANK, -1)], axis=1)
    hn1, = _rows(_rms_fwd_fn, [h2], [norm_mix_g[1:2]], [(D_MODEL, BF16)], [], name="mix1_norm")
    proj_o = _mm(hn1, w_oin_pad, name="odd_in")
    qn, kvn, ko1, ko2 = _rows(_mla_prep_fn, [proj_o, pos], [qg, kvg, freqs_k],
                              [(MLA_Q_RANK, BF16), (MLA_KV_RANK, BF16), (LANE, F32), (LANE, F32)], [], name="mla_prep")
    q_all = _mm(qn, w_uq_perm, name="mla_uq")
    kv = _mm(kvn, w_ukv, out_dtype=BF16, name="mla_ukv")
    nope_w = MLA_HEADS * MLA_NOPE
    rope_w = MLA_HEADS * half
    o1, o2 = _rows(_rope_q_fn, [(q_all, nope_w, rope_w), (q_all, nope_w + rope_w, rope_w), pos], [freqs_q],
                   [(rope_w, F32), (rope_w, F32)], [], name="mla_rope_q")
    q_rope = jnp.concatenate([o1.reshape(T, MLA_HEADS, half), o2.reshape(T, MLA_HEADS, half)], axis=2).transpose(1, 0, 2)
    k_rope = jnp.concatenate([ko1[:, :half], ko2[:, :half]], axis=1)
    o, lse = _attn_fwd(q_all, q_rope, kv, k_rope, name="attn_fwd")
    h3 = _mm(o, w_oout, res=h2, name="odd_out")
    full.update(_gather_finish(gathers[4], h3, w, 4, False))
    w_fin.append(full["ffn_w_in", 1])
    w_fout.append(full["ffn_w_out", 1])
    h4, ffn1_saved = _ffn_fwd(h3, norm_ffn_g[1:2], w_fin[1], conv_w[1], ffn_conv_b[1:2], w_fout[1], 1)

    dh4, d_final_g, loss_cols = _rows(_loss_fn, [h4, tgt], [final_norm_g[None, :]], [(D_MODEL, F32)],
                                      [(1, D_MODEL), (1, D_MODEL)], name="loss_head")
    loss = lax.psum(jnp.sum(loss_cols), ("x", "y", "c"))

    dh3, gf1 = _ffn_bwd(h3, dh4, ffn1_saved, norm_ffn_g[1:2], w_fin[1], conv_w[1], ffn_conv_b[1:2], w_fout[1], 1)
    scatters = {4: _scatter_start({("ffn_w_in", 1): gf1["w_in"], ("ffn_w_out", 1): gf1["w_out"]}, 4)}
    do = _mm(dh3, w_oout, tb=True, deps=[scatters[4][4]], name="odd_out_dx")
    d_w_oout = _mm(o, dh3, ta=True, out_dtype=BF16, name="odd_out_dw")
    dq_nope, dq_rope, dkv, dk_rope = _attn_bwd(q_all, q_rope, kv, k_rope, o, lse, do, name="attn_bwd")
    dq_rope_t = dq_rope.transpose(1, 0, 2)
    do1, do2 = dq_rope_t[:, :, :half].reshape(T, rope_w), dq_rope_t[:, :, half:].reshape(T, rope_w)
    lane_pad = ((0, 0), (0, LANE - half))
    dko1, dko2 = jnp.pad(dk_rope[:, :half], lane_pad), jnp.pad(dk_rope[:, half:], lane_pad)
    dq_all, = _rows(_rope_q_bwd_fn, [dq_nope, do1, do2, pos], [freqs_q], [(MLA_HEADS * MLA_QK, BF16)], [], name="mla_rope_q_bwd")
    d_w_uq_perm = _mm(qn, dq_all, ta=True, out_dtype=BF16, name="mla_uq_dw")
    dqn = _mm(dq_all, w_uq_perm, tb=True, name="mla_uq_dx")
    d_w_ukv = _mm(kvn, dkv, ta=True, out_dtype=BF16, name="mla_ukv_dw")
    dkvn = _mm(dkv, w_ukv, tb=True, name="mla_ukv_dx")
    dproj_o, d_qg, d_kvg = _rows(_mla_prep_bwd_fn, [proj_o, pos, dqn, dkvn, dko1, dko2], [qg, kvg, freqs_k],
                                 [(ODD_IN_PAD, BF16)], [(1, MLA_Q_RANK), (1, MLA_KV_RANK)], name="mla_prep_bwd")
    d_w_oin_pad = _mm(hn1, dproj_o, ta=True, out_dtype=BF16, name="odd_in_dw")
    dhn1 = _mm(dproj_o, w_oin_pad, tb=True, name="odd_in_dx")
    dh2, d_mix_g1 = _rows(_rms_bwd_fn, [h2, dhn1, dh3], [norm_mix_g[1:2]], [(D_MODEL, F32)], [(1, D_MODEL)], name="mix1_dnorm")
    d_w_oin = jnp.concatenate([d_w_oin_pad[:, :kr0 + half], d_w_oin_pad[:, kr0 + LANE:kr0 + LANE + half]], axis=1)
    d3 = d_w_uq_perm
    d_w_uq = jnp.concatenate([d3[:, :nope_w].reshape(MLA_Q_RANK, MLA_HEADS, MLA_NOPE),
                              d3[:, nope_w:nope_w + rope_w].reshape(MLA_Q_RANK, MLA_HEADS, half),
                              d3[:, nope_w + rope_w:].reshape(MLA_Q_RANK, MLA_HEADS, half)], axis=2).reshape(MLA_Q_RANK, -1)
    scatters[3] = _scatter_start({("odd_w_in", 0): d_w_oin, ("mla_w_uq", 0): d_w_uq, ("mla_w_ukv", 0): d_w_ukv,
                                  ("odd_w_out", 0): d_w_oout}, 3)

    dh1, gf0 = _ffn_bwd(h1, dh2, ffn0_saved, norm_ffn_g[0:1], w_fin[0], conv_w[0], ffn_conv_b[0:1], w_fout[0], 0,
                        deps=[scatters[3][4]])
    scatters[2] = _scatter_start({("ffn_w_in", 0): gf0["w_in"], ("ffn_w_out", 0): gf0["w_out"]}, 2)
    dmix = _mm(dh1, w_eout, tb=True, deps=[scatters[2][4]], name="even_out_dx")
    d_w_eout = _mm(mixin, dh1, ta=True, out_dtype=BF16, name="even_out_dw")
    dq, df, di, dg, d_lb0, d_hgrn_g = _hgrn_bwd(proj, lb0, hgrn_norm_g, states, dmix, name="hgrn_bwd")
    dz1, dgl, d_b_glu = _rows(_s5_gate_bwd_fn, [z5, gl, (dmix, HGRN_DIM, S5_DIM)], [s5_b_glu],
                              [(S5_DIM, F32), (S5_DIM, BF16)], [(1, S5_DIM)], name="s5_gate_bwd")
    dz2 = _mm(dgl, w_glu, tb=True, name="s5_glu_dx")
    d_w_glu = _mm(z5, dgl, ta=True, out_dtype=BF16, name="s5_glu_dw")
    dys, du1, d_s5_d = _rows(_s5_act_bwd_fn, [ys, (proj,) + u_cols, dz1, dz2], [s5_d],
                             [(S5_DIM, BF16), (S5_DIM, F32)], [(1, S5_DIM)], name="s5_act_bwd")
    d_c_re = _mm(s_re, dys, ta=True, name="s5_dc_re")
    d_c_im_neg = _mm(s_im, dys, ta=True, name="s5_dc_im")
    lam_re, lam_im, d_ar, d_ai = _s5_scan_bwd(dys, c_re, c_im_neg, s_re, s_im, a_re, a_im, name="s5_scan_bwd")
    du2 = _mm(lam_im, bb_im, tb=True, res=_mm(lam_re, bb_re, tb=True, name="s5_du_re"), name="s5_du_im")
    d_bb_re = _mm(proj, lam_re, ta=True, a_cols=u_cols, name="s5_dbb_re")
    d_bb_im = _mm(proj, lam_im, ta=True, a_cols=u_cols, name="s5_dbb_im")
    sp_g = _s5_params_bwd(*sp_in, d_ar.sum(0).reshape(S5_GROUPS, S5_STATE), d_ai.sum(0).reshape(S5_GROUPS, S5_STATE),
                          _diag_blocks(d_bb_re, S5_GROUP, S5_STATE), _diag_blocks(d_bb_im, S5_GROUP, S5_STATE), d_lb0)
    d_a_re, d_a_im, d_log_dt, d_bt_re, d_bt_im, d_lb_logits = sp_g
    g_small = dict(mla_q_norm_g=d_qg, mla_kv_norm_g=d_kvg, ffn_conv_w=jnp.stack([gf0["conv_w"], gf1["conv_w"]]))
    g_repl = dict(
        norm_ffn_g=jnp.concatenate([gf0["g"], gf1["g"]]),
        final_norm_g=d_final_g[0], hgrn_lb_logits=d_lb_logits, hgrn_norm_g=d_hgrn_g,
        s5_a_re=d_a_re[None], s5_a_im=d_a_im[None], s5_log_dt=d_log_dt[:, 0][None],
        s5_b_re=d_bt_re.transpose(0, 2, 1)[None], s5_b_im=d_bt_im.transpose(0, 2, 1)[None],
        s5_c_re=_diag_blocks(d_c_re, S5_STATE, S5_GROUP).transpose(0, 2, 1)[None],
        s5_c_im=-_diag_blocks(d_c_im_neg, S5_STATE, S5_GROUP).transpose(0, 2, 1)[None],
        s5_d=d_s5_d, s5_b_glu=d_b_glu, ffn_conv_b=jnp.concatenate([gf0["conv_b"], gf1["conv_b"]]))
    scatters[1] = _scatter_start({("s5_w_glu", 0): d_w_glu, ("even_w_out", 0): d_w_eout}, 1,
                                 extra=[_small_sharded_pack(g_small, w), _replicated_pack(g_repl, REPLICATED_EARLY)])
    dproj, = _rows(_dproj_fn, [dq, df, di, dg, du1, du2], [], [(EVEN_IN, BF16)], [], name="even_dproj", deps=[scatters[1][4]])
    d_w_ein = _mm(hn0, dproj, ta=True, out_dtype=BF16, name="even_in_dw")
    dhn0 = _mm(dproj, w_ein, tb=True, name="even_in_dx")
    grad_x, d_mix_g0 = _rows(_rms_bwd_fn, [h0, dhn0, dh1], [norm_mix_g[0:1]], [(D_MODEL, F32)], [(1, D_MODEL)], name="mix0_dnorm")
    g_repl["norm_mix_g"] = jnp.concatenate([d_mix_g0, d_mix_g1])
    scatters[0] = _scatter_start({("even_w_in", 0): d_w_ein}, 0, extra=[_replicated_pack(g_repl, REPLICATED_LATE)])

    delta, new_m, new_v = {}, {}, {}
    per_unit, partial = {}, {}
    after = scatters[0][4]
    for stage in (4, 3, 2, 1, 0):
        partial[stage] = _scatter_finish(scatters[stage], after, stage)
        for (n, l), slabs in zip(STAGES[stage], partial[stage]):
            per_unit[n, l] = _adamw(w[n][l], slabs, mom[n][l], var[n][l], name=f"adamw_{n}_{l}")
        after = per_unit[STAGES[stage][-1]][0]
    grads = {}
    for n in BIG_NAMES:
        layers = [per_unit[n, l] for l in range(w[n].shape[0])]
        grads[n], delta[n], new_m[n], new_v[n] = (jnp.stack([lay[k] for lay in layers]) for k in range(4))
    for names, slabs, tag in ((REPLICATED_EARLY, partial[1][-1], "repl"), (REPLICATED_LATE, partial[0][-1], "late"),
                              (SMALL_SHARDED_NAMES, partial[1][-2], "small")):
        packs = [_pack([t[n].reshape(-1) for n in names], LANE, SUBLANE) for t in (w, mom, var)]
        outs = _adamw(packs[0], slabs, packs[1], packs[2], name=f"adamw_{tag}")
        shapes = [w[n].shape for n in names]
        for dst, o_ in zip((grads, delta, new_m, new_v), outs):
            dst.update(zip(names, _unpack(o_.reshape(-1), shapes)))

    return (loss, grad_x[None], *[grads[n] for n in W_NAMES], *[delta[n] for n in W_NAMES],
            *[new_m[n] for n in W_NAMES], *[new_v[n] for n in W_NAMES])
```

```python
import functools
import math

import numpy as np
import jax
import jax.numpy as jnp
from jax import lax
from jax.experimental import pallas as pl
from jax.experimental.pallas import tpu as pltpu

F32 = jnp.float32
BF16 = jnp.bfloat16
_MXU_DTYPE = jnp.bfloat16

D_MODEL = 1024
HGRN_DIM = 512
HGRN_HEAD_DIM = 128
HGRN_HEADS = 4
HGRN_CHUNK = 64
S5_DIM = 512
S5_GROUPS = 32
S5_GROUP = 16
S5_STATE = 64
S5_WIDTH = S5_GROUPS * S5_STATE
EVEN_IN = 4 * HGRN_DIM + S5_DIM
MLA_HEADS = 8
MLA_Q_RANK = 384
MLA_KV_RANK = 256
MLA_NOPE = 128
MLA_ROPE = 64
MLA_V = 128
MLA_QK = MLA_NOPE + MLA_ROPE
ODD_IN = MLA_Q_RANK + MLA_KV_RANK + MLA_ROPE
ODD_IN_PAD = MLA_Q_RANK + MLA_KV_RANK + 2 * 128
ROPE_THETA = 10000.0
D_FF = 2816
EPS = 1e-6
ADAM_LR = 0.001
ADAM_B1 = 0.9
ADAM_B2 = 0.999
ADAM_EPS = 1e-08
ADAM_WD = 0.01
ADAM_STEP = 10

N_DEV = 8
LANE = 128
SUBLANE = 8
VMEM_LIMIT_BYTES = 56 * 1024 * 1024
MESH = pl.DeviceIdType.MESH


def _cparams(sem=None):
    return pltpu.CompilerParams(dimension_semantics=sem, vmem_limit_bytes=VMEM_LIMIT_BYTES)


def _pick(n, cands):
    for c in cands:
        if n % c == 0:
            return c
    raise ValueError(f"no tile for {n} in {cands}")


def _sigmoid(x):
    return 0.5 * jnp.tanh(0.5 * x) + 0.5


def _silu(x):
    return x * _sigmoid(x)


def _gelu(x):
    return 0.5 * x * (1.0 + jnp.tanh(math.sqrt(2.0 / math.pi) * (x + 0.044715 * (x * x * x))))


def _rms(x, g):
    return x * lax.rsqrt(jnp.mean(x * x, axis=-1, keepdims=True) + EPS) * g


def _mxu(a, b, ca, cb):
    return lax.dot_general(a.astype(_MXU_DTYPE), b.astype(_MXU_DTYPE), (((ca,), (cb,)), ((), ())),
                           preferred_element_type=F32)


@functools.partial(jax.custom_vjp, nondiff_argnums=(2, 3))
def _mxu_ad(a, b, ca, cb):
    return _mxu(a, b, ca, cb)


def _mxu_ad_fwd(a, b, ca, cb):
    return _mxu(a, b, ca, cb), (a, b)


def _mxu_ad_bwd(ca, cb, saved, g):
    a, b = saved
    fa, fb = 1 - ca, 1 - cb
    da = _mxu(g, b, 1, fb) if ca == 1 else _mxu(b, g, fb, 1)
    db = _mxu(a, g, fa, 0) if cb == 0 else _mxu(g, a, 0, fa)
    return da, db


_mxu_ad.defvjp(_mxu_ad_fwd, _mxu_ad_bwd)


def _tri(n):
    row = lax.broadcasted_iota(jnp.int32, (n, n), 0)
    col = lax.broadcasted_iota(jnp.int32, (n, n), 1)
    return col <= row


def _cumsum_rows(x, reverse=False):
    n = x.shape[0]
    rowi = lax.broadcasted_iota(jnp.int32, x.shape, 0)
    s = 1
    while s < n:
        if reverse:
            x = x + jnp.where(rowi < n - s, pltpu.roll(x, n - s, 0), 0.0)
        else:
            x = x + jnp.where(rowi >= s, pltpu.roll(x, s, 0), 0.0)
        s *= 2
    return x


@jax.custom_vjp
def _cumsum_rows_ad(x):
    return _cumsum_rows(x)


def _cumsum_rows_ad_fwd(x):
    return _cumsum_rows(x), None


def _cumsum_rows_ad_bwd(_, g):
    return (_cumsum_rows(g, reverse=True),)


_cumsum_rows_ad.defvjp(_cumsum_rows_ad_fwd, _cumsum_rows_ad_bwd)


MM_VMEM_BUDGET = 36 * 1024 * 1024
MM_MAX_TILE = 1408


def _lane_divisors(n, cap, offs=()):
    return [d for d in range(min(n, cap) // LANE * LANE, 0, -LANE) if n % d == 0 and all(o % d == 0 for o in offs)]


def _mm_tiles(M, N, K, sa, sb, so, has_res, m_offs, n_offs, k_offs):
    best = None
    for tm in _lane_divisors(M, MM_MAX_TILE, m_offs):
        for tn in _lane_divisors(N, MM_MAX_TILE, n_offs):
            for tk in _lane_divisors(K, K, k_offs):
                nk = K // tk
                vmem = 2 * (tm * tk * sa + tk * tn * sb + tm * tn * so + tm * tn * 4 * has_res) + (tm * tn * 4 if nk > 1 else 0)
                if vmem <= MM_VMEM_BUDGET:
                    key = (-nk, tm * tn, tn)
                    if best is None or key > best[0]:
                        best = (key, tm, tn, tk)
                    break
    return best[1:]


def _mm(a, b, *, ta=False, tb=False, res=None, out_dtype=F32, a_cols=None, b_cols=None, deps=(), name):
    a_minor = a.shape[1] if a_cols is None else a_cols[1]
    b_minor = b.shape[1] if b_cols is None else b_cols[1]
    K, M = (a.shape[0], a_minor) if ta else (a_minor, a.shape[0])
    N = b.shape[0] if tb else b_minor
    assert (b_minor if tb else b.shape[0]) == K, (a.shape, b.shape, ta, tb)
    a_off = 0 if a_cols is None else a_cols[0]
    b_off = 0 if b_cols is None else b_cols[0]
    has_res = res is not None
    tm, tn, tk = _mm_tiles(M, N, K, a.dtype.itemsize, b.dtype.itemsize, jnp.dtype(out_dtype).itemsize, has_res,
                           (a_off,) if ta else (), () if tb else (b_off,), ((a_off,) if not ta else ()) + ((b_off,) if tb else ()))
    nk = K // tk
    am, ak = (a_off // tm, 0) if ta else (0, a_off // tk)
    bn, bk = (0, b_off // tk) if tb else (b_off // tn, 0)
    a_spec = pl.BlockSpec((tk, tm), lambda i, j, k: (k, i + am)) if ta else pl.BlockSpec((tm, tk), lambda i, j, k: (i, k + ak))
    b_spec = pl.BlockSpec((tn, tk), lambda i, j, k: (j, k + bk)) if tb else pl.BlockSpec((tk, tn), lambda i, j, k: (k, j + bn))
    o_spec = pl.BlockSpec((tm, tn), lambda i, j, k: (i, j))
    ca, cb = (0 if ta else 1), (1 if tb else 0)

    n_fixed = 2 + has_res + len(deps)

    def body(*refs):
        a_ref, b_ref = refs[0], refs[1]
        res_ref = refs[2] if has_res else None
        o_ref = refs[n_fixed]
        part = _mxu(a_ref[...], b_ref[...], ca, cb)
        if nk == 1:
            o_ref[...] = (part + res_ref[...] if has_res else part).astype(out_dtype)
            return
        acc_ref = refs[n_fixed + 1]
        k = pl.program_id(2)

        @pl.when(k == 0)
        def _():
            acc_ref[...] = part

        @pl.when(k > 0)
        def _():
            acc_ref[...] += part

        @pl.when(k == nk - 1)
        def _():
            o_ref[...] = (acc_ref[...] + res_ref[...] if has_res else acc_ref[...]).astype(out_dtype)

    ins = [a, b] + ([res] if has_res else []) + list(deps)
    in_specs = [a_spec, b_spec] + ([o_spec] if has_res else []) + [pl.BlockSpec(memory_space=pl.ANY)] * len(deps)
    return pl.pallas_call(
        body, name=name, grid=(M // tm, N // tn, nk),
        in_specs=in_specs, out_specs=o_spec,
        out_shape=jax.ShapeDtypeStruct((M, N), out_dtype),
        scratch_shapes=[pltpu.VMEM((tm, tn), F32)] if nk > 1 else [],
        compiler_params=_cparams(("parallel", "parallel", "arbitrary")),
    )(*ins)


def _rows(fn, row_ins, const_ins, row_outs, acc_outs, *, name, tm=512, deps=()):
    norm = [(r, 0, r.shape[1]) if not isinstance(r, tuple) else r for r in row_ins]
    T = norm[0][0].shape[0]
    tm = min(tm, T)
    nr, nc, no, na = len(norm), len(const_ins), len(row_outs), len(acc_outs)
    first_out = nr + nc + len(deps)

    def body(*refs):
        i = pl.program_id(0)
        vals = [r[...] for r in refs[:nr + nc]]
        outs = fn(*vals)
        for o_ref, o in zip(refs[first_out:first_out + no], outs[:no]):
            o_ref[...] = o.astype(o_ref.dtype)
        for a_ref, o in zip(refs[first_out + no:], outs[no:]):
            @pl.when(i == 0)
            def _(a_ref=a_ref, o=o):
                a_ref[...] = o

            @pl.when(i > 0)
            def _(a_ref=a_ref, o=o):
                a_ref[...] += o

    in_specs = []
    for arr, off, w in norm:
        assert off % w == 0, (off, w)
        in_specs.append(pl.BlockSpec((tm, w), lambda i, b=off // w: (i, b)))
    for c in const_ins:
        in_specs.append(pl.BlockSpec(c.shape, lambda i: (0, 0)))
    in_specs += [pl.BlockSpec(memory_space=pl.ANY)] * len(deps)
    out_specs = [pl.BlockSpec((tm, w), lambda i: (i, 0)) for w, _ in row_outs]
    out_specs += [pl.BlockSpec(s, lambda i: (0, 0)) for s in acc_outs]
    out_shape = [jax.ShapeDtypeStruct((T, w), dt) for w, dt in row_outs]
    out_shape += [jax.ShapeDtypeStruct(s, F32) for s in acc_outs]
    return pl.pallas_call(
        body, name=name, grid=(T // tm,), in_specs=in_specs, out_specs=out_specs, out_shape=out_shape,
        compiler_params=_cparams(("arbitrary",)),
    )(*[n[0] for n in norm], *const_ins, *deps)


FFN_COL_TILE = LANE
FFN_ROW_CHUNK = 512


FFN_HALO = 2 * SUBLANE


def _shift_down(ext, s, rows):
    return pltpu.roll(ext, s, 0)[FFN_HALO:FFN_HALO + rows]


def _shift_up(ext, s, rows):
    return pltpu.roll(ext, rows + FFN_HALO - s, 0)[:rows]


def _ffn_chunks(T):
    r = min(FFN_ROW_CHUNK, T)
    return r, T // r


def _ext_before(ref, c, r):
    if c == 0:
        return jnp.concatenate([jnp.zeros((FFN_HALO, ref.shape[1]), F32), ref[0:r, :].astype(F32)], axis=0)
    return ref[c * r - FFN_HALO:(c + 1) * r, :].astype(F32)


def _ext_after(ref, c, r, nch):
    if c == nch - 1:
        return jnp.concatenate([ref[c * r:(c + 1) * r, :].astype(F32), jnp.zeros((FFN_HALO, ref.shape[1]), F32)], axis=0)
    return ref[c * r:(c + 1) * r + FFN_HALO, :].astype(F32)


def _ffn_mid_fwd(au, conv_w, conv_b, *, name):
    T = au.shape[0]
    tc = FFN_COL_TILE
    ncol = D_FF // tc
    r, nch = _ffn_chunks(T)

    def body(a_ref, u_ref, w_ref, b_ref, z_ref):
        w0, w1, w2, bias = w_ref[0:1, :], w_ref[1:2, :], w_ref[2:3, :], b_ref[...]
        for c in range(nch):
            ext = _ext_before(a_ref, c, r)
            pre = w0 * _shift_down(ext, 2, r) + w1 * _shift_down(ext, 1, r) + w2 * ext[FFN_HALO:] + bias
            z_ref[c * r:(c + 1) * r, :] = (_silu(pre) * u_ref[c * r:(c + 1) * r, :].astype(F32)).astype(z_ref.dtype)

    return pl.pallas_call(
        body, name=name, grid=(ncol,),
        in_specs=[pl.BlockSpec((T, tc), lambda j: (0, j)), pl.BlockSpec((T, tc), lambda j: (0, j + ncol)),
                  pl.BlockSpec((3, tc), lambda j: (0, j)), pl.BlockSpec((1, tc), lambda j: (0, j))],
        out_specs=pl.BlockSpec((T, tc), lambda j: (0, j)),
        out_shape=jax.ShapeDtypeStruct((T, D_FF), BF16),
        compiler_params=_cparams(("parallel",)),
    )(au, au, conv_w, conv_b)


def _ffn_mid_bwd(au, dz, conv_w, conv_b, *, name):
    T = au.shape[0]
    tc = FFN_COL_TILE
    ncol = D_FF // tc
    r, nch = _ffn_chunks(T)

    def body(a_ref, u_ref, dz_ref, w_ref, b_ref, da_ref, du_ref, dw_ref, db_ref, dpre_ref):
        w0, w1, w2, bias = w_ref[0:1, :], w_ref[1:2, :], w_ref[2:3, :], b_ref[...]
        dw0 = jnp.zeros((1, tc), F32)
        dw1 = jnp.zeros((1, tc), F32)
        dw2 = jnp.zeros((1, tc), F32)
        db = jnp.zeros((1, tc), F32)
        for c in range(nch):
            rows = slice(c * r, (c + 1) * r)
            ext = _ext_before(a_ref, c, r)
            a2, a1, a0 = _shift_down(ext, 2, r), _shift_down(ext, 1, r), ext[FFN_HALO:]
            pre = w0 * a2 + w1 * a1 + w2 * a0 + bias
            sg = _sigmoid(pre)
            act = pre * sg
            dzc = dz_ref[rows, :].astype(F32)
            du_ref[rows, :] = (dzc * act).astype(du_ref.dtype)
            dpre = dzc * u_ref[rows, :].astype(F32) * (sg * (1.0 + pre * (1.0 - sg)))
            dpre_ref[rows, :] = dpre
            dw0 += jnp.sum(dpre * a2, axis=0, keepdims=True)
            dw1 += jnp.sum(dpre * a1, axis=0, keepdims=True)
            dw2 += jnp.sum(dpre * a0, axis=0, keepdims=True)
            db += jnp.sum(dpre, axis=0, keepdims=True)
        for c in range(nch):
            ext = _ext_after(dpre_ref, c, r, nch)
            da = w0 * _shift_up(ext, 2, r) + w1 * _shift_up(ext, 1, r) + w2 * ext[:r]
            da_ref[c * r:(c + 1) * r, :] = da.astype(da_ref.dtype)
        dw_ref[0:1, :] = dw0
        dw_ref[1:2, :] = dw1
        dw_ref[2:3, :] = dw2
        db_ref[...] = db

    col = lambda j: (0, j)
    return pl.pallas_call(
        body, name=name, grid=(ncol,),
        in_specs=[pl.BlockSpec((T, tc), col), pl.BlockSpec((T, tc), lambda j: (0, j + ncol)), pl.BlockSpec((T, tc), col),
                  pl.BlockSpec((3, tc), col), pl.BlockSpec((1, tc), col)],
        out_specs=[pl.BlockSpec((T, tc), col), pl.BlockSpec((T, tc), col), pl.BlockSpec((3, tc), col), pl.BlockSpec((1, tc), col)],
        out_shape=[jax.ShapeDtypeStruct((T, D_FF), BF16), jax.ShapeDtypeStruct((T, D_FF), BF16),
                   jax.ShapeDtypeStruct((3, D_FF), F32), jax.ShapeDtypeStruct((1, D_FF), F32)],
        scratch_shapes=[pltpu.VMEM((T, tc), F32)],
        compiler_params=_cparams(("parallel",)),
    )(au, au, dz, conv_w, conv_b)


HGRN_BLOCK = 256


def _hgrn_chunk(dot, cumsum, q, f, i, g, lb, ng, st):
    C = q.shape[0]
    forget = lb + (1.0 - lb) * _sigmoid(f)
    k = 1.0 - forget
    b = cumsum(jnp.log(forget))
    b_last = b[C - 1:C, :]
    qd = q * jnp.exp(b)
    kd = k * jnp.exp(-b)
    att = jnp.where(_tri(C), dot(qd, kd, 1, 1), 0.0)
    o = dot(att, i, 1, 0) + dot(qd, st, 1, 1)
    st_new = st * jnp.exp(b_last) + dot(i, k * jnp.exp(b_last - b), 0, 0)
    on = o * lax.rsqrt(jnp.mean(o * o, axis=-1, keepdims=True) + EPS) * ng
    return on * _silu(g), st_new


def _hgrn_specs(T, rev):
    tb = min(HGRN_BLOCK, T)
    nb = T // tb
    blk = (lambda n: nb - 1 - n) if rev else (lambda n: n)
    hd = HGRN_HEAD_DIM
    proj_specs = [pl.BlockSpec((tb, HGRN_DIM), lambda n, k=k: (blk(n), k)) for k in range(4)]
    vec_spec = pl.BlockSpec((1, HGRN_DIM), lambda n: (0, 0))
    tok_spec = pl.BlockSpec((tb, HGRN_DIM), lambda n: (blk(n), 0))
    st_spec = pl.BlockSpec((HGRN_HEADS, tb // HGRN_CHUNK, hd, hd), lambda n: (0, blk(n), 0, 0))
    return tb, nb, proj_specs, vec_spec, tok_spec, st_spec


def _head_cols(h):
    return slice(h * HGRN_HEAD_DIM, (h + 1) * HGRN_HEAD_DIM)


def _hgrn_fwd(proj, lb, ng, *, name):
    T = proj.shape[0]
    tb, nb, proj_specs, vec_spec, tok_spec, st_spec = _hgrn_specs(T, False)
    nsub = tb // HGRN_CHUNK
    hd = HGRN_HEAD_DIM

    def body(q_ref, f_ref, i_ref, g_ref, lb_ref, ng_ref, y_ref, sts_ref, st_ref):
        @pl.when(pl.program_id(0) == 0)
        def _():
            st_ref[...] = jnp.zeros_like(st_ref)

        st = [st_ref[h] for h in range(HGRN_HEADS)]
        for s in range(nsub):
            rows = slice(s * HGRN_CHUNK, (s + 1) * HGRN_CHUNK)
            for h in range(HGRN_HEADS):
                cols = _head_cols(h)
                sts_ref[h, s] = st[h]
                y, st[h] = _hgrn_chunk(_mxu, _cumsum_rows, q_ref[rows, cols], f_ref[rows, cols], i_ref[rows, cols],
                                       g_ref[rows, cols], lb_ref[:, cols], ng_ref[:, cols], st[h])
                y_ref[rows, cols] = y
        for h in range(HGRN_HEADS):
            st_ref[h] = st[h]

    return pl.pallas_call(
        body, name=name, grid=(nb,),
        in_specs=proj_specs + [vec_spec, vec_spec], out_specs=[tok_spec, st_spec],
        out_shape=[jax.ShapeDtypeStruct((T, HGRN_DIM), F32),
                   jax.ShapeDtypeStruct((HGRN_HEADS, T // HGRN_CHUNK, hd, hd), F32)],
        scratch_shapes=[pltpu.VMEM((HGRN_HEADS, hd, hd), F32)],
        compiler_params=_cparams(("arbitrary",)),
    )(proj, proj, proj, proj, lb, ng)


def _hgrn_bwd(proj, lb, ng, states, dmix, *, name):
    T = proj.shape[0]
    tb, nb, proj_specs, vec_spec, tok_spec, st_spec = _hgrn_specs(T, True)
    nsub = tb // HGRN_CHUNK
    hd = HGRN_HEAD_DIM
    chunk = functools.partial(_hgrn_chunk, _mxu_ad, _cumsum_rows_ad)

    def body(q_ref, f_ref, i_ref, g_ref, lb_ref, ng_ref, sts_ref, dy_ref,
             dq_ref, df_ref, di_ref, dg_ref, dlb_ref, dng_ref, dst_ref):
        @pl.when(pl.program_id(0) == 0)
        def _():
            dst_ref[...] = jnp.zeros_like(dst_ref)
            dlb_ref[...] = jnp.zeros_like(dlb_ref)
            dng_ref[...] = jnp.zeros_like(dng_ref)

        dst = [dst_ref[h] for h in range(HGRN_HEADS)]
        dlb = [jnp.zeros((1, hd), F32)] * HGRN_HEADS
        dng = [jnp.zeros((1, hd), F32)] * HGRN_HEADS
        for s in reversed(range(nsub)):
            rows = slice(s * HGRN_CHUNK, (s + 1) * HGRN_CHUNK)
            for h in range(HGRN_HEADS):
                cols = _head_cols(h)
                _, vjp = jax.vjp(chunk, q_ref[rows, cols], f_ref[rows, cols], i_ref[rows, cols], g_ref[rows, cols],
                                 lb_ref[:, cols], ng_ref[:, cols], sts_ref[h, s])
                dq, df, di, dg, dlb_s, dng_s, dst[h] = vjp((dy_ref[rows, cols], dst[h]))
                dq_ref[rows, cols] = dq
                df_ref[rows, cols] = df
                di_ref[rows, cols] = di
                dg_ref[rows, cols] = dg
                dlb[h] = dlb[h] + dlb_s
                dng[h] = dng[h] + dng_s
        for h in range(HGRN_HEADS):
            dst_ref[h] = dst[h]
            dlb_ref[:, _head_cols(h)] += dlb[h]
            dng_ref[:, _head_cols(h)] += dng[h]

    tok_out = jax.ShapeDtypeStruct((T, HGRN_DIM), F32)
    vec_out = jax.ShapeDtypeStruct((1, HGRN_DIM), F32)
    return pl.pallas_call(
        body, name=name, grid=(nb,),
        in_specs=proj_specs + [vec_spec, vec_spec, st_spec, tok_spec],
        out_specs=[tok_spec] * 4 + [vec_spec, vec_spec],
        out_shape=[tok_out] * 4 + [vec_out, vec_out],
        scratch_shapes=[pltpu.VMEM((HGRN_HEADS, hd, hd), F32)],
        compiler_params=_cparams(("arbitrary",)),
    )(proj, proj, proj, proj, lb, ng, states, dmix)


S5_LANES = 512
S5_ROWS = 512


def _cmul(ar, ai, br, bi):
    return ar * br - ai * bi, ar * bi + ai * br


def _power_table(ar, ai, exps):
    a2 = _cmul(ar, ai, ar, ai)
    a4 = _cmul(*a2, *a2)
    e = exps - 1
    pr = jnp.broadcast_to(ar, exps.shape)
    pi = jnp.broadcast_to(ai, exps.shape)
    for bit, (fr, fi) in enumerate(((ar, ai), a2, a4)):
        nr, ni = _cmul(pr, pi, fr, fi)
        on = ((e >> bit) & 1) == 1
        pr, pi = jnp.where(on, nr, pr), jnp.where(on, ni, pi)
    return pr, pi, a2, a4


def _s5_scan_fwd(x, x_cols, b_re, b_im, a_re, a_im, *, name):
    T = x.shape[0]
    w, tr = S5_LANES, min(S5_ROWS, T)
    ncol, nt = S5_WIDTH // w, T // tr
    assert x_cols[0] % x_cols[1] == 0

    def body(x_ref, br_ref, bi_ref, ar_ref, ai_ref, sr_ref, si_ref, carry_ref):
        @pl.when(pl.program_id(1) == 0)
        def _():
            carry_ref[...] = jnp.zeros_like(carry_ref)

        u = x_ref[...].astype(_MXU_DTYPE)
        sr_ref[...] = _mxu(u, br_ref[...], 1, 0)
        si_ref[...] = _mxu(u, bi_ref[...], 1, 0)
        ar, ai = ar_ref[...], ai_ref[...]
        rowi = lax.broadcasted_iota(jnp.int32, (SUBLANE, w), 0)
        pr, pi, a2, a4 = _power_table(ar, ai, rowi + 1)
        steps = [(s, jnp.where(rowi >= s, fr, 0.0), jnp.where(rowi >= s, fi, 0.0)) for s, (fr, fi) in ((1, (ar, ai)), (2, a2), (4, a4))]

        def tile(i, carry):
            cr, ci = carry
            rows = pl.ds(pl.multiple_of(i * SUBLANE, SUBLANE), SUBLANE)
            xr, xi = sr_ref[rows, :], si_ref[rows, :]
            for s, fr, fi in steps:
                zr, zi = pltpu.roll(xr, s, 0), pltpu.roll(xi, s, 0)
                xr, xi = xr + fr * zr - fi * zi, xi + fr * zi + fi * zr
            xr, xi = xr + pr * cr - pi * ci, xi + pr * ci + pi * cr
            sr_ref[rows, :] = xr
            si_ref[rows, :] = xi
            return xr[SUBLANE - 1:SUBLANE, :], xi[SUBLANE - 1:SUBLANE, :]

        cr, ci = lax.fori_loop(0, tr // SUBLANE, tile, (carry_ref[0:1, :], carry_ref[1:2, :]))
        carry_ref[0:1, :] = cr
        carry_ref[1:2, :] = ci

    out = jax.ShapeDtypeStruct((T, S5_WIDTH), F32)
    return pl.pallas_call(
        body, name=name, grid=(ncol, nt),
        in_specs=[pl.BlockSpec((tr, x_cols[1]), lambda j, t: (t, x_cols[0] // x_cols[1])),
                  pl.BlockSpec((S5_DIM, w), lambda j, t: (0, j)), pl.BlockSpec((S5_DIM, w), lambda j, t: (0, j)),
                  pl.BlockSpec((1, w), lambda j, t: (0, j)), pl.BlockSpec((1, w), lambda j, t: (0, j))],
        out_specs=[pl.BlockSpec((tr, w), lambda j, t: (t, j))] * 2,
        out_shape=[out, out],
        scratch_shapes=[pltpu.VMEM((2, w), F32)],
        compiler_params=_cparams(("parallel", "arbitrary")),
    )(x, b_re, b_im, a_re, a_im)


def _s5_scan_bwd(dy, c_re, c_im, s_re, s_im, a_re, a_im, *, name):
    T = dy.shape[0]
    w, tr = S5_LANES, min(S5_ROWS, T)
    ncol, nt = S5_WIDTH // w, T // tr
    ntile = tr // SUBLANE

    def body(dy_ref, cr_ref, ci_ref, sr_ref, si_ref, ar_ref, ai_ref, lr_ref, li_ref, dar_ref, dai_ref, carry_ref):
        @pl.when(pl.program_id(1) == 0)
        def _():
            carry_ref[...] = jnp.zeros_like(carry_ref)
            dar_ref[...] = jnp.zeros_like(dar_ref)
            dai_ref[...] = jnp.zeros_like(dai_ref)

        dyb = dy_ref[...].astype(_MXU_DTYPE)
        lr_ref[...] = _mxu(dyb, cr_ref[...], 1, 1)
        li_ref[...] = _mxu(dyb, ci_ref[...], 1, 1)
        ar, ai = ar_ref[...], -ai_ref[...]
        rowi = lax.broadcasted_iota(jnp.int32, (SUBLANE, w), 0)
        pr, pi, a2, a4 = _power_table(ar, ai, SUBLANE - rowi)
        last = rowi == SUBLANE - 1
        steps = [(s, jnp.where(rowi < SUBLANE - s, fr, 0.0), jnp.where(rowi < SUBLANE - s, fi, 0.0))
                 for s, (fr, fi) in ((1, (ar, ai)), (2, a2), (4, a4))]

        def tile(i, carry):
            cr, ci, dar, dai = carry
            rows = pl.ds(pl.multiple_of((ntile - 1 - i) * SUBLANE, SUBLANE), SUBLANE)
            xr, xi = lr_ref[rows, :], li_ref[rows, :]
            for s, fr, fi in steps:
                zr, zi = pltpu.roll(xr, SUBLANE - s, 0), pltpu.roll(xi, SUBLANE - s, 0)
                xr, xi = xr + fr * zr - fi * zi, xi + fr * zi + fi * zr
            xr, xi = xr + pr * cr - pi * ci, xi + pr * ci + pi * cr
            lr_ref[rows, :] = xr
            li_ref[rows, :] = xi
            nr = jnp.where(last, cr, pltpu.roll(xr, SUBLANE - 1, 0))
            ni = jnp.where(last, ci, pltpu.roll(xi, SUBLANE - 1, 0))
            sr, si = sr_ref[rows, :], si_ref[rows, :]
            return xr[0:1, :], xi[0:1, :], dar + nr * sr + ni * si, dai + ni * sr - nr * si

        cr, ci, dar, dai = lax.fori_loop(
            0, ntile, tile, (carry_ref[0:1, :], carry_ref[1:2, :], jnp.zeros((SUBLANE, w), F32), jnp.zeros((SUBLANE, w), F32)))
        carry_ref[0:1, :] = cr
        carry_ref[1:2, :] = ci
        dar_ref[...] += dar
        dai_ref[...] += dai

    tok = pl.BlockSpec((tr, w), lambda j, t: (nt - 1 - t, j))
    vec = pl.BlockSpec((1, w), lambda j, t: (0, j))
    acc = pl.BlockSpec((SUBLANE, w), lambda j, t: (0, j))
    out = jax.ShapeDtypeStruct((T, S5_WIDTH), F32)
    accs = jax.ShapeDtypeStruct((SUBLANE, S5_WIDTH), F32)
    return pl.pallas_call(
        body, name=name, grid=(ncol, nt),
        in_specs=[pl.BlockSpec((tr, S5_DIM), lambda j, t: (nt - 1 - t, 0)),
                  pl.BlockSpec((w, S5_DIM), lambda j, t: (j, 0)), pl.BlockSpec((w, S5_DIM), lambda j, t: (j, 0)),
                  tok, tok, vec, vec],
        out_specs=[tok, tok, acc, acc],
        out_shape=[out, out, accs, accs],
        scratch_shapes=[pltpu.VMEM((2, w), F32)],
        compiler_params=_cparams(("parallel", "arbitrary")),
    )(dy, c_re, c_im, s_re, s_im, a_re, a_im)


ATTN_BLOCK = 512
_NEG = -1e30


def _qk_cat(nope, rope):
    return jnp.concatenate([nope.astype(_MXU_DTYPE), rope.astype(_MXU_DTYPE)], axis=1)


_QK_SCALE = MLA_QK ** -0.5
_LOG2E = math.log2(math.e)


def _attn_scores(q, k, diagonal):
    s = _mxu(q, k, 1, 1) * (_QK_SCALE * _LOG2E)
    if diagonal:
        s = jnp.where(_tri(s.shape[0]), s, _NEG)
    return s


def _attn_fwd(q_all, q_rope, kv, k_rope, *, name):
    T = q_all.shape[0]
    tq = min(ATTN_BLOCK, T)
    nq = T // tq

    def body(qn_ref, qr_ref, kn_ref, v_ref, kr_ref, o_ref, lse_ref):
        i = pl.program_id(1)
        q = _qk_cat(qn_ref[...], qr_ref[0])

        def step(j, carry, diagonal):
            m, l, acc = carry
            ks = pl.ds(pl.multiple_of(j * tq, tq), tq)
            s = _attn_scores(q, _qk_cat(kn_ref[ks, :], kr_ref[ks, :]), diagonal)
            m_new = jnp.maximum(m, jnp.max(s, axis=-1, keepdims=True))
            p = jnp.exp2(s - m_new)
            alpha = jnp.exp2(m - m_new)
            return m_new, alpha * l + jnp.sum(p, axis=-1, keepdims=True), alpha * acc + _mxu(p, v_ref[ks, :], 1, 0)

        init = (jnp.full((tq, 1), _NEG, F32), jnp.zeros((tq, 1), F32), jnp.zeros((tq, MLA_V), F32))
        below = lax.fori_loop(0, i, functools.partial(step, diagonal=False), init)
        m, l, acc = step(i, below, diagonal=True)
        o_ref[...] = acc / l
        lse_ref[0] = m + jnp.log2(l)

    return pl.pallas_call(
        body, name=name, grid=(MLA_HEADS, nq),
        in_specs=[pl.BlockSpec((tq, MLA_NOPE), lambda h, i: (i, h)), pl.BlockSpec((1, tq, MLA_ROPE), lambda h, i: (h, i, 0)),
                  pl.BlockSpec((T, MLA_NOPE), lambda h, i: (0, 2 * h)), pl.BlockSpec((T, MLA_V), lambda h, i: (0, 2 * h + 1)),
                  pl.BlockSpec((T, MLA_ROPE), lambda h, i: (0, 0))],
        out_specs=[pl.BlockSpec((tq, MLA_V), lambda h, i: (i, h)), pl.BlockSpec((1, tq, 1), lambda h, i: (h, i, 0))],
        out_shape=[jax.ShapeDtypeStruct((T, MLA_HEADS * MLA_V), F32), jax.ShapeDtypeStruct((MLA_HEADS, T, 1), F32)],
        compiler_params=_cparams(("arbitrary", "arbitrary")),
    )(q_all, q_rope, kv, kv, k_rope)


def _attn_bwd(q_all, q_rope, kv, k_rope, o, lse, do, *, name):
    T = q_all.shape[0]
    tk = min(ATTN_BLOCK, T)
    nk = T // tk

    def body(qn_ref, qr_ref, kv_ref, kr_ref, o_ref, lse_ref, do_ref, dqn_ref, dqr_ref, dkv_ref, dkr_ref, delta_ref):
        h, j = pl.program_id(0), pl.program_id(1)

        @pl.when(j == 0)
        def _():
            dqn_ref[...] = jnp.zeros_like(dqn_ref)
            dqr_ref[...] = jnp.zeros_like(dqr_ref)
            delta_ref[...] = jnp.sum(do_ref[...] * o_ref[...], axis=-1, keepdims=True)

        @pl.when((j == 0) & (h == 0))
        def _():
            dkr_ref[...] = jnp.zeros_like(dkr_ref)

        krows = pl.ds(pl.multiple_of(j * tk, tk), tk)
        k = _qk_cat(kv_ref[:, :MLA_NOPE], kr_ref[krows, :])
        v = kv_ref[:, MLA_NOPE:].astype(_MXU_DTYPE)

        def step(i, carry, diagonal):
            dk, dv = carry
            qs = pl.ds(pl.multiple_of(i * tk, tk), tk)
            q, dob = _qk_cat(qn_ref[qs, :], qr_ref[0, qs, :]), do_ref[qs, :].astype(_MXU_DTYPE)
            p = jnp.exp2(_attn_scores(q, k, diagonal) - lse_ref[0, qs, :])
            ds = p * (_mxu(dob, v, 1, 1) - delta_ref[qs, :]) * _QK_SCALE
            dq = _mxu(ds, k, 1, 0)
            dqn_ref[qs, :] += dq[:, :MLA_NOPE]
            dqr_ref[0, qs, :] += dq[:, MLA_NOPE:]
            return dk + _mxu(ds, q, 0, 0), dv + _mxu(p, dob, 0, 0)

        on_diagonal = step(j, (jnp.zeros((tk, MLA_QK), F32), jnp.zeros((tk, MLA_V), F32)), diagonal=True)
        dk, dv = lax.fori_loop(j + 1, nk, functools.partial(step, diagonal=False), on_diagonal)
        dkv_ref[:, :MLA_NOPE] = dk[:, :MLA_NOPE].astype(dkv_ref.dtype)
        dkv_ref[:, MLA_NOPE:] = dv.astype(dkv_ref.dtype)
        dkr_ref[krows, :] += dk[:, MLA_NOPE:]

    head_cols = pl.BlockSpec((T, MLA_NOPE), lambda h, j: (0, h))
    head_rope = pl.BlockSpec((1, T, MLA_ROPE), lambda h, j: (h, 0, 0))
    kv_spec = pl.BlockSpec((tk, MLA_NOPE + MLA_V), lambda h, j: (j, h))
    kr_spec = pl.BlockSpec((T, MLA_ROPE), lambda h, j: (0, 0))
    return pl.pallas_call(
        body, name=name, grid=(MLA_HEADS, nk),
        in_specs=[head_cols, head_rope, kv_spec, kr_spec, head_cols, pl.BlockSpec((1, T, 1), lambda h, j: (h, 0, 0)), head_cols],
        out_specs=[head_cols, head_rope, kv_spec, kr_spec],
        out_shape=[jax.ShapeDtypeStruct((T, MLA_HEADS * MLA_NOPE), F32), jax.ShapeDtypeStruct((MLA_HEADS, T, MLA_ROPE), F32),
                   jax.ShapeDtypeStruct((T, MLA_HEADS * (MLA_NOPE + MLA_V)), BF16), jax.ShapeDtypeStruct((T, MLA_ROPE), F32)],
        scratch_shapes=[pltpu.VMEM((T, 1), F32)],
        compiler_params=_cparams(("arbitrary", "arbitrary")),
    )(q_all, q_rope, kv, k_rope, o, lse, do)


def _s5_discretize(a_re, a_im, log_dt, bt_re, bt_im, lb_logits):
    dt = jnp.exp(log_dt)
    mag = jnp.exp(a_re * dt)
    abr, abi = mag * jnp.cos(a_im * dt), mag * jnp.sin(a_im * dt)
    den = a_re * a_re + a_im * a_im
    xr, xi = abr - 1.0, abi
    cr = ((xr * a_re + xi * a_im) / den)[:, None, :]
    ci = ((xi * a_re - xr * a_im) / den)[:, None, :]
    e = jnp.exp(lb_logits - jnp.max(lb_logits, axis=0, keepdims=True))
    lb = e[0:1, :] / jnp.sum(e, axis=0, keepdims=True)
    return abr, abi, cr * bt_re - ci * bt_im, cr * bt_im + ci * bt_re, lb


def _whole(shape):
    return pl.BlockSpec(shape, lambda: (0,) * len(shape))


def _s5_params_fwd(a_re, a_im, log_dt, bt_re, bt_im, lb_logits):
    ins = (a_re, a_im, log_dt, bt_re, bt_im, lb_logits)
    outs = [jax.ShapeDtypeStruct(s, F32) for s in (a_re.shape, a_re.shape, bt_re.shape, bt_re.shape, (1, lb_logits.shape[1]))]

    def body(*refs):
        res = _s5_discretize(*[r[...] for r in refs[:6]])
        for o_ref, o in zip(refs[6:], res):
            o_ref[...] = o

    return pl.pallas_call(body, name="s5_params_fwd", in_specs=[_whole(a.shape) for a in ins],
                          out_specs=[_whole(o.shape) for o in outs], out_shape=outs, compiler_params=_cparams())(*ins)


def _s5_params_bwd(a_re, a_im, log_dt, bt_re, bt_im, lb_logits, d_abr, d_abi, d_bbr, d_bbi, d_lb):
    ins = (a_re, a_im, log_dt, bt_re, bt_im, lb_logits, d_abr, d_abi, d_bbr, d_bbi, d_lb)
    outs = [jax.ShapeDtypeStruct(a.shape, F32) for a in ins[:6]]

    def body(*refs):
        _, vjp = jax.vjp(_s5_discretize, *[r[...] for r in refs[:6]])
        for o_ref, o in zip(refs[11:], vjp(tuple(r[...] for r in refs[6:11]))):
            o_ref[...] = o

    return pl.pallas_call(body, name="s5_params_bwd", in_specs=[_whole(a.shape) for a in ins],
                          out_specs=[_whole(o.shape) for o in outs], out_shape=outs, compiler_params=_cparams())(*ins)


ADAMW_WHOLE_BYTES = 1024 * 1024


def _adamw(w, g, m, v, *, name, layer=0, n_layers=1, into=None):
    R, C = w.shape
    whole = R % SUBLANE != 0 or R * C * w.dtype.itemsize <= ADAMW_WHOLE_BYTES
    tr = R if whole else _pick(R, (256, 128, 64, 32, 16, 8))
    slabs = g.shape[0]
    n_prev = 0 if into is None else len(into)

    def body(w_ref, g_ref, m_ref, v_ref, *rest):
        g_out, d_ref, mo_ref, vo_ref = rest[n_prev:]
        gv = g_ref[0].astype(F32)
        for s in range(1, slabs):
            gv = gv + g_ref[s].astype(F32)
        m2 = ADAM_B1 * m_ref[...] + (1.0 - ADAM_B1) * gv
        v2 = ADAM_B2 * v_ref[...] + (1.0 - ADAM_B2) * (gv * gv)
        m_hat = m2 / (1.0 - ADAM_B1 ** ADAM_STEP)
        v_hat = v2 / (1.0 - ADAM_B2 ** ADAM_STEP)
        g_out[0] = gv
        d_ref[0] = -ADAM_LR * (m_hat / (jnp.sqrt(v_hat) + ADAM_EPS) + ADAM_WD * w_ref[...])
        mo_ref[0] = m2
        vo_ref[0] = v2

    spec = pl.BlockSpec((tr, C), lambda i: (i, 0))
    out_spec = pl.BlockSpec((1, tr, C), lambda i: (layer, i, 0))
    out = jax.ShapeDtypeStruct((n_layers, R, C), F32)
    return pl.pallas_call(
        body, name=name, grid=(R // tr,),
        in_specs=[spec, pl.BlockSpec((slabs, tr, C), lambda i: (0, i, 0)), spec, spec] + [pl.BlockSpec(memory_space=pl.ANY)] * n_prev,
        out_specs=[out_spec] * 4, out_shape=[out] * 4, input_output_aliases={4 + k: k for k in range(n_prev)},
        compiler_params=_cparams(("parallel",)))(w, g, m, v, *(into or ()))


N_CHIPS = 4
N_CORES = 2


_FLIPS = tuple((dx, dy, dc) for dx in (0, 1) for dy in (0, 1) for dc in (0, 1) if (dx, dy, dc) != (0, 0, 0))


_HBM = pl.BlockSpec(memory_space=pltpu.HBM)
_SEM = pl.BlockSpec(memory_space=pltpu.SEMAPHORE)
_SPLIT_COPY = pltpu.CompilerParams(has_side_effects=pltpu.SideEffectType.DATAFLOW_SIDE_EFFECTING)


def _exchange_copies(src_refs, land_refs, send_sems, recv_sems, scatter, arriving):
    x, y, c = lax.axis_index("x"), lax.axis_index("y"), lax.axis_index("c")
    me_chip = 2 * x + y
    copies = []
    for a, (s_ref, l_ref) in enumerate(zip(src_refs, land_refs)):
        for j, (dx, dy, dc) in enumerate(_FLIPS):
            px, py, pc = (1 - x if dx else x), (1 - y if dy else y), (1 - c if dc else c)
            k = a * len(_FLIPS) + j
            p_chip = 2 * px + py
            copies.append(pltpu.make_async_remote_copy(
                src_ref=s_ref.at[p_chip, pc] if scatter else s_ref, dst_ref=l_ref.at[p_chip, pc] if arriving else l_ref.at[me_chip, c],
                send_sem=send_sems.at[k], recv_sem=recv_sems.at[k], device_id=(px, py, pc), device_id_type=MESH))
    return copies


def _exchange_start(srcs, *, scatter, name, after=()):
    n_arr = len(srcs)
    n_sem = n_arr * len(_FLIPS)
    n_in = 2 * n_arr + len(after)
    lands = [lax.empty(s.shape if scatter else (N_CHIPS, N_CORES) + s.shape, s.dtype) for s in srcs]

    def body(*refs):
        src_refs, land_refs = refs[:n_arr], refs[n_arr:2 * n_arr]
        for cp in _exchange_copies(src_refs, land_refs, refs[n_in], refs[n_in + 1], scatter, arriving=False):
            cp.start()
        refs[-1][...] = jnp.zeros_like(refs[-1])

    thru = [pltpu.HBM(a.shape, a.dtype) for a in srcs + lands]
    outs = pl.pallas_call(
        body, name=name,
        out_shape=(pltpu.SemaphoreType.DMA((n_sem,)), pltpu.SemaphoreType.DMA((n_sem,)), *thru,
                   jax.ShapeDtypeStruct((SUBLANE, LANE), F32)),
        in_specs=[_HBM] * (2 * n_arr) + [pl.BlockSpec(memory_space=pl.ANY)] * len(after),
        out_specs=(_SEM, _SEM, *[_HBM] * (2 * n_arr), pl.BlockSpec(memory_space=pltpu.VMEM)),
        input_output_aliases={i: 2 + i for i in range(2 * n_arr)}, compiler_params=_SPLIT_COPY,
    )(*[pltpu.with_memory_space_constraint(a, pltpu.HBM) for a in srcs + lands], *after)
    return outs[0], outs[1], list(outs[2:2 + n_arr]), list(outs[2 + n_arr:2 + 2 * n_arr]), outs[-1]


def _exchange_wait(started, after, *, scatter, name):
    send_sems, recv_sems, srcs, lands, _ = started
    n_arr = len(srcs)

    def body(*refs):
        src_refs, land_refs = refs[:n_arr], refs[n_arr:2 * n_arr]
        for cp in _exchange_copies(src_refs, land_refs, refs[2 * n_arr], refs[2 * n_arr + 1], scatter, arriving=True):
            cp.wait_send()
            cp.wait_recv()

    outs = pl.pallas_call(
        body, name=name, out_shape=[pltpu.HBM(a.shape, a.dtype) for a in srcs + lands],
        in_specs=[_HBM] * (2 * n_arr) + [_SEM, _SEM, pl.BlockSpec(memory_space=pl.ANY)], out_specs=[_HBM] * (2 * n_arr),
        input_output_aliases={i: i for i in range(2 * n_arr)}, compiler_params=_SPLIT_COPY,
    )(*srcs, *lands, send_sems, recv_sems, after)
    return list(outs[:n_arr]), list(outs[n_arr:])


def _with_own(land, own):
    me_chip = 2 * lax.axis_index("x") + lax.axis_index("y")
    return lax.dynamic_update_slice(land, own[None, None], (me_chip, lax.axis_index("c")) + (0,) * own.ndim)


def _rms_fwd_fn(h, g):
    return (_rms(h, g),)


def _rms_bwd_fn(h, dhn, dres, g):
    _, vjp = jax.vjp(_rms, h, g)
    dh, dg = vjp(dhn)
    return dh + dres, dg


def _loss_fn(h, tgt, g):
    y, vjp = jax.vjp(_rms, h, g)
    diff = y - tgt
    dh, dg = vjp(diff * (1.0 / D_MODEL))
    return dh, dg, (0.5 / D_MODEL) * jnp.sum(diff * diff, axis=0, keepdims=True)


def _s5_act(ys, u, d):
    return _gelu(ys + d * u)


def _s5_gate(z, gl, b):
    return z * _sigmoid(gl + b)


def _s5_act_fn(ys, u, d):
    return (_s5_act(ys, u, d),)


def _s5_mix_fn(ya, z, gl, b):
    return (jnp.concatenate([ya, _s5_gate(z, gl, b)], axis=1),)


def _s5_gate_bwd_fn(z, gl, dyb, b):
    _, vjp = jax.vjp(_s5_gate, z, gl, b)
    return vjp(dyb)


def _s5_act_bwd_fn(ys, u, dz1, dz2, d):
    _, vjp = jax.vjp(_s5_act, ys, u, d)
    return vjp(dz1 + dz2)


def _dproj_fn(dq, df, di, dg, du1, du2):
    return (jnp.concatenate([dq, df, di, dg, du1 + du2], axis=1),)


def _rope_pair(r1, r2, pos, freqs):
    ang = pos.astype(F32) * freqs
    c, s = jnp.cos(ang), jnp.sin(ang)
    return r1 * c - r2 * s, r1 * s + r2 * c


_ODD_SPLITS = (0, MLA_Q_RANK, MLA_Q_RANK + MLA_KV_RANK, MLA_Q_RANK + MLA_KV_RANK + LANE, ODD_IN_PAD)


def _mla_prep(cq, ckv, k1, k2, qg, kvg, pos, freqs):
    ko1, ko2 = _rope_pair(k1, k2, pos, freqs)
    return _rms(cq, qg), _rms(ckv, kvg), ko1, ko2


def _mla_prep_fn(proj, pos, qg, kvg, freqs):
    parts = [proj[:, a:b] for a, b in zip(_ODD_SPLITS[:-1], _ODD_SPLITS[1:])]
    return _mla_prep(*parts, qg, kvg, pos, freqs)


def _mla_prep_bwd_fn(proj, pos, dqn, dkvn, dko1, dko2, qg, kvg, freqs):
    parts = [proj[:, a:b] for a, b in zip(_ODD_SPLITS[:-1], _ODD_SPLITS[1:])]
    _, vjp = jax.vjp(lambda *a: _mla_prep(*a, pos, freqs), *parts, qg, kvg)
    dcq, dckv, dk1, dk2, dqg, dkvg = vjp((dqn, dkvn, dko1, dko2))
    return jnp.concatenate([dcq, dckv, dk1, dk2], axis=1), dqg, dkvg


ROPE_ROWS = 512


def _rope_heads(x, pos, freqs):
    half = MLA_ROPE // 2
    ang = pos.astype(F32) * freqs
    lane = lax.broadcasted_iota(jnp.int32, x.shape, 1)
    first = (lane % MLA_ROPE) < half
    other = jnp.where(first, pltpu.roll(x, x.shape[1] - half, 1), pltpu.roll(x, half, 1))
    return x * jnp.cos(ang) + other * jnp.where(first, -jnp.sin(ang), jnp.sin(ang))


def _rope_q_fwd(q_all, pos, freqs, *, name):
    T = q_all.shape[0]
    tm = min(ROPE_ROWS, T)
    nope_w, rope_w = MLA_HEADS * MLA_NOPE, MLA_HEADS * MLA_ROPE

    def body(r_ref, pos_ref, f_ref, o_ref):
        out = _rope_heads(r_ref[...], pos_ref[...], f_ref[...])
        for h in range(MLA_HEADS):
            o_ref[h] = out[:, h * MLA_ROPE:(h + 1) * MLA_ROPE]

    return pl.pallas_call(
        body, name=name, grid=(T // tm,),
        in_specs=[pl.BlockSpec((tm, rope_w), lambda i: (i, nope_w // rope_w)), pl.BlockSpec((tm, 1), lambda i: (i, 0)),
                  pl.BlockSpec((1, rope_w), lambda i: (0, 0))],
        out_specs=pl.BlockSpec((MLA_HEADS, tm, MLA_ROPE), lambda i: (0, i, 0)),
        out_shape=jax.ShapeDtypeStruct((MLA_HEADS, T, MLA_ROPE), F32),
        compiler_params=_cparams(("parallel",)),
    )(q_all, pos, freqs)


def _rope_q_bwd(dq_nope, dq_rope, pos, freqs, *, name):
    T = dq_nope.shape[0]
    tm = min(ROPE_ROWS, T)
    nope_w, rope_w = MLA_HEADS * MLA_NOPE, MLA_HEADS * MLA_ROPE

    def body(dn_ref, dr_ref, pos_ref, f_ref, o_ref):
        d_out = jnp.concatenate([dr_ref[h] for h in range(MLA_HEADS)], axis=1)
        o_ref[:, :nope_w] = dn_ref[...].astype(o_ref.dtype)
        o_ref[:, nope_w:] = _rope_heads(d_out, pos_ref[...], -f_ref[...]).astype(o_ref.dtype)

    return pl.pallas_call(
        body, name=name, grid=(T // tm,),
        in_specs=[pl.BlockSpec((tm, nope_w), lambda i: (i, 0)), pl.BlockSpec((MLA_HEADS, tm, MLA_ROPE), lambda i: (0, i, 0)),
                  pl.BlockSpec((tm, 1), lambda i: (i, 0)), pl.BlockSpec((1, rope_w), lambda i: (0, 0))],
        out_specs=pl.BlockSpec((tm, nope_w + rope_w), lambda i: (i, 0)),
        out_shape=jax.ShapeDtypeStruct((T, nope_w + rope_w), BF16),
        compiler_params=_cparams(("parallel",)),
    )(dq_nope, dq_rope, pos, freqs)


W_NAMES = ("norm_mix_g", "norm_ffn_g", "final_norm_g", "even_w_in", "hgrn_lb_logits", "hgrn_norm_g", "s5_a_re", "s5_a_im",
           "s5_log_dt", "s5_b_re", "s5_b_im", "s5_c_re", "s5_c_im", "s5_d", "s5_w_glu", "s5_b_glu", "even_w_out", "odd_w_in",
           "mla_q_norm_g", "mla_w_uq", "mla_kv_norm_g", "mla_w_ukv", "odd_w_out", "ffn_w_in", "ffn_conv_w", "ffn_conv_b",
           "ffn_w_out")
BIG_UNITS = (("even_w_in", 0, "col"), ("s5_w_glu", 0, "row"), ("even_w_out", 0, "row"), ("odd_w_in", 0, "row"),
             ("mla_w_uq", 0, "col"), ("mla_w_ukv", 0, "col"), ("odd_w_out", 0, "row"),
             ("ffn_w_in", 0, "col"), ("ffn_w_in", 1, "col"), ("ffn_w_out", 0, "row"), ("ffn_w_out", 1, "row"))
BIG_NAMES = tuple(dict.fromkeys(u[0] for u in BIG_UNITS))
SMALL_SHARDED = (("mla_q_norm_g", 1), ("mla_kv_norm_g", 1), ("ffn_conv_w", 2))
SMALL_SHARDED_NAMES = tuple(s[0] for s in SMALL_SHARDED)
REPLICATED = tuple(n for n in W_NAMES if n not in BIG_NAMES + SMALL_SHARDED_NAMES)
REPLICATED_LATE = ("norm_mix_g",)
REPLICATED_EARLY = tuple(n for n in REPLICATED if n not in REPLICATED_LATE)


def _pack(flats, cols, row_mult):
    flat = jnp.concatenate(flats, axis=-1)
    pad = (-flat.shape[-1]) % (cols * row_mult)
    flat = jnp.pad(flat, [(0, 0)] * (flat.ndim - 1) + [(0, pad)])
    return flat.reshape(flat.shape[:-1] + (-1, cols))


def _unpack(flat, shapes):
    out, off = [], 0
    for shp in shapes:
        n = int(np.prod(shp))
        out.append(flat[..., off:off + n].reshape(flat.shape[:-1] + tuple(shp)))
        off += n
    return out


UNIT_KIND = {(n, l): kind for n, l, kind in BIG_UNITS}
STAGES = ((("even_w_in", 0),),
          (("s5_w_glu", 0), ("even_w_out", 0)),
          (("ffn_w_in", 0), ("ffn_w_out", 0)),
          (("odd_w_in", 0), ("mla_w_uq", 0), ("mla_w_ukv", 0), ("odd_w_out", 0)),
          (("ffn_w_in", 1), ("ffn_w_out", 1)))


def _gather_start(w, stage, with_small, after):
    srcs = [w[n][l].astype(BF16) for n, l in STAGES[stage]]
    if with_small:
        srcs.append(_pack([w[n].reshape(-1) for n in SMALL_SHARDED_NAMES], LANE, SUBLANE))
    return _exchange_start(srcs, scatter=False, name=f"gather_start_{stage}", after=after)


def _gather_finish(started, after, w, stage, with_small):
    srcs, lands = _exchange_wait(started, after, scatter=False, name=f"gather_wait_{stage}")
    lands = [_with_own(land, src) for land, src in zip(lands, srcs)]
    big = {}
    for unit, g in zip(STAGES[stage], lands):
        r, c = g.shape[2:]
        big[unit] = g.reshape(N_DEV * r, c) if UNIT_KIND[unit] == "row" else g.transpose(2, 0, 1, 3).reshape(r, N_DEV * c)
    if not with_small:
        return big
    parts = _unpack(lands[-1].reshape(N_DEV, -1), [w[n].shape for n in SMALL_SHARDED_NAMES])
    small = {}
    for (n, ax), p in zip(SMALL_SHARDED, parts):
        shp = list(w[n].shape)
        shp[ax] *= N_DEV
        small[n] = jnp.moveaxis(p, 0, ax).reshape(shp)
    return big, small


def _scatter_start(g_big, stage, extra=()):
    srcs = []
    for unit in STAGES[stage]:
        g = g_big[unit].astype(BF16)
        if UNIT_KIND[unit] == "row":
            srcs.append(g.reshape(N_CHIPS, N_CORES, g.shape[0] // N_DEV, g.shape[1]))
        else:
            srcs.append(g.reshape(g.shape[0], N_CHIPS, N_CORES, g.shape[1] // N_DEV).transpose(1, 2, 0, 3))
    return _exchange_start(srcs + list(extra), scatter=True, name=f"scatter_start_{stage}")


def _scatter_finish(started, after, stage):
    srcs, lands = _exchange_wait(started, after, scatter=True, name=f"scatter_wait_{stage}")
    me_chip, c = 2 * lax.axis_index("x") + lax.axis_index("y"), lax.axis_index("c")
    outs = []
    for land, src in zip(lands, srcs):
        own = lax.dynamic_slice(src, (me_chip, c) + (0,) * (src.ndim - 2), (1, 1) + src.shape[2:])[0, 0]
        outs.append(_with_own(land, own).reshape((N_DEV,) + land.shape[2:]))
    return outs


def _small_sharded_pack(g_small, w):
    flats = []
    for n, ax in SMALL_SHARDED:
        shp = list(w[n].shape)
        g = g_small[n].astype(F32).reshape(shp[:ax] + [N_DEV] + shp[ax:])
        flats.append(jnp.moveaxis(g, ax, 0).reshape(N_DEV, -1))
    small = _pack(flats, LANE, SUBLANE)
    return small.reshape((N_CHIPS, N_CORES) + small.shape[1:])


def _replicated_pack(g_repl, names):
    vec = _pack([g_repl[n].reshape(-1).astype(F32) for n in names], LANE, SUBLANE)
    return jnp.broadcast_to(vec, (N_CHIPS, N_CORES) + vec.shape)


def _block_diag(blocks):
    G, a, b = blocks.shape
    return jnp.einsum('gab,gk->gakb', blocks, jnp.eye(G, dtype=blocks.dtype)).reshape(G * a, G * b)


def _diag_blocks(mat, a, b):
    G = mat.shape[0] // a
    return jnp.einsum('gagb->gab', mat.reshape(G, a, G, b))


def _ffn_fwd(h, g, w_in, conv_w, conv_b, w_out, tag):
    hn, = _rows(_rms_fwd_fn, [h], [g], [(D_MODEL, BF16)], [], name=f"ffn{tag}_norm")
    au = _mm(hn, w_in, out_dtype=BF16, name=f"ffn{tag}_in")
    z = _ffn_mid_fwd(au, conv_w, conv_b, name=f"ffn{tag}_mid")
    return _mm(z, w_out, res=h, name=f"ffn{tag}_out"), (hn, au, z)


def _ffn_bwd(h, dh, saved, g, w_in, conv_w, conv_b, w_out, tag, deps=()):
    hn, au, z = saved
    dz = _mm(dh, w_out, tb=True, out_dtype=BF16, deps=deps, name=f"ffn{tag}_dz")
    dw_out = _mm(z, dh, ta=True, out_dtype=BF16, name=f"ffn{tag}_dwout")
    da, du, dcw, dcb = _ffn_mid_bwd(au, dz, conv_w, conv_b, name=f"ffn{tag}_dmid")
    dhn = _mm(da, w_in, tb=True, b_cols=(0, D_FF), name=f"ffn{tag}_dhn_a")
    dhn = _mm(du, w_in, tb=True, b_cols=(D_FF, D_FF), res=dhn, name=f"ffn{tag}_dhn_u")
    dw_in = jnp.concatenate([_mm(hn, da, ta=True, out_dtype=BF16, name=f"ffn{tag}_dwin_a"),
                             _mm(hn, du, ta=True, out_dtype=BF16, name=f"ffn{tag}_dwin_u")], axis=1)
    dh_in, dg = _rows(_rms_bwd_fn, [h, dhn, dh], [g], [(D_MODEL, F32)], [(1, D_MODEL)], name=f"ffn{tag}_dnorm")
    return dh_in, dict(g=dg, w_in=dw_in, conv_w=dcw, conv_b=dcb, w_out=dw_out)


def kernel(x, positions, norm_mix_g, norm_ffn_g, final_norm_g, even_w_in, hgrn_lb_logits, hgrn_norm_g, s5_a_re, s5_a_im, s5_log_dt, s5_b_re, s5_b_im, s5_c_re, s5_c_im, s5_d, s5_w_glu, s5_b_glu, even_w_out, odd_w_in, mla_q_norm_g, mla_w_uq, mla_kv_norm_g, mla_w_ukv, odd_w_out, ffn_w_in, ffn_conv_w, ffn_conv_b, ffn_w_out, loss_target, m_norm_mix_g, m_norm_ffn_g, m_final_norm_g, m_even_w_in, m_hgrn_lb_logits, m_hgrn_norm_g, m_s5_a_re, m_s5_a_im, m_s5_log_dt, m_s5_b_re, m_s5_b_im, m_s5_c_re, m_s5_c_im, m_s5_d, m_s5_w_glu, m_s5_b_glu, m_even_w_out, m_odd_w_in, m_mla_q_norm_g, m_mla_w_uq, m_mla_kv_norm_g, m_mla_w_ukv, m_odd_w_out, m_ffn_w_in, m_ffn_conv_w, m_ffn_conv_b, m_ffn_w_out, v_norm_mix_g, v_norm_ffn_g, v_final_norm_g, v_even_w_in, v_hgrn_lb_logits, v_hgrn_norm_g, v_s5_a_re, v_s5_a_im, v_s5_log_dt, v_s5_b_re, v_s5_b_im, v_s5_c_re, v_s5_c_im, v_s5_d, v_s5_w_glu, v_s5_b_glu, v_even_w_out, v_odd_w_in, v_mla_q_norm_g, v_mla_w_uq, v_mla_kv_norm_g, v_mla_w_ukv, v_odd_w_out, v_ffn_w_in, v_ffn_conv_w, v_ffn_conv_b, v_ffn_w_out):
    given = dict(locals())
    w = {n: given[n] for n in W_NAMES}
    mom = {n: given["m_" + n] for n in W_NAMES}
    var = {n: given["v_" + n] for n in W_NAMES}
    T = x.shape[1]
    h0 = x[0]
    tgt = loss_target[0]
    pos = positions.reshape(T, 1)

    gathers = []
    for s in range(len(STAGES)):
        gathers.append(_gather_start(w, s, with_small=(s == 1), after=[g[4] for g in gathers[-1:]]))
    half = MLA_ROPE // 2
    kr0 = MLA_Q_RANK + MLA_KV_RANK
    freqs = ROPE_THETA ** (-jnp.arange(0, MLA_ROPE, 2, dtype=F32) / MLA_ROPE)
    freqs_q = jnp.tile(jnp.concatenate([freqs, freqs]), MLA_HEADS)[None, :]
    freqs_k = jnp.concatenate([freqs, jnp.zeros((LANE - half,), F32)])[None, :]

    sp_in = (s5_a_re[0], s5_a_im[0], s5_log_dt[0][:, None], s5_b_re[0].transpose(0, 2, 1), s5_b_im[0].transpose(0, 2, 1),
             hgrn_lb_logits)
    abr, abi, bbt_re, bbt_im, lb0 = _s5_params_fwd(*sp_in)
    a_re, a_im = abr.reshape(1, S5_WIDTH), abi.reshape(1, S5_WIDTH)
    bb_re, bb_im = _block_diag(bbt_re).astype(BF16), _block_diag(bbt_im).astype(BF16)
    c_re = _block_diag(s5_c_re[0].transpose(0, 2, 1)).astype(BF16)
    c_im_neg = _block_diag(-s5_c_im[0].transpose(0, 2, 1)).astype(BF16)
    u_cols = (4 * HGRN_DIM, S5_DIM)

    hn0, = _rows(_rms_fwd_fn, [h0], [norm_mix_g[0:1]], [(D_MODEL, BF16)], [], name="mix0_norm", deps=[gathers[-1][4]])
    full = _gather_finish(gathers[0], hn0, w, 0, False)
    w_ein = full["even_w_in", 0]
    proj = _mm(hn0, w_ein, name="even_in")
    y_a, states = _hgrn_fwd(proj, lb0, hgrn_norm_g, name="hgrn_fwd")
    s_re, s_im = _s5_scan_fwd(proj, u_cols, bb_re, bb_im, a_re, a_im, name="s5_scan_fwd")
    more, full_small = _gather_finish(gathers[1], s_re, w, 1, True)
    w_glu, w_eout = more["s5_w_glu", 0], more["even_w_out", 0]
    qg, kvg, conv_w = full_small["mla_q_norm_g"], full_small["mla_kv_norm_g"], full_small["ffn_conv_w"]
    ys = _mm(s_im, c_im_neg, res=_mm(s_re, c_re, name="s5_y_re"), name="s5_y_im")
    z5, = _rows(_s5_act_fn, [ys, (proj,) + u_cols], [s5_d], [(S5_DIM, F32)], [], name="s5_act")
    gl = _mm(z5, w_glu, name="s5_glu")
    mixin, = _rows(_s5_mix_fn, [y_a, z5, gl], [s5_b_glu], [(D_MODEL, BF16)], [], name="s5_mix")
    h1 = _mm(mixin, w_eout, res=h0, name="even_out")
    full.update(_gather_finish(gathers[2], h1, w, 2, False))
    w_fin, w_fout = [full["ffn_w_in", 0]], [full["ffn_w_out", 0]]
    h2, ffn0_saved = _ffn_fwd(h1, norm_ffn_g[0:1], w_fin[0], conv_w[0], ffn_conv_b[0:1], w_fout[0], 0)

    full.update(_gather_finish(gathers[3], h2, w, 3, False))
    w_oin, w_ukv, w_oout = full["odd_w_in", 0], full["mla_w_ukv", 0], full["odd_w_out", 0]
    zpad = jnp.zeros((D_MODEL, LANE - half), BF16)
    w_oin_pad = jnp.concatenate([w_oin[:, :kr0], w_oin[:, kr0:kr0 + half], zpad, w_oin[:, kr0 + half:], zpad], axis=1)
    w_uq3 = full["mla_w_uq", 0].reshape(MLA_Q_RANK, MLA_HEADS, MLA_QK)
    w_uq_perm = jnp.concatenate([w_uq3[:, :, :MLA_NOPE].reshape(MLA_Q_RANK, -1),
                                 w_uq3[:, :, MLA_NOPE:].reshape(MLA_Q_RANK, -1)], axis=1)
    hn1, = _rows(_rms_fwd_fn, [h2], [norm_mix_g[1:2]], [(D_MODEL, BF16)], [], name="mix1_norm")
    proj_o = _mm(hn1, w_oin_pad, name="odd_in")
    qn, kvn, ko1, ko2 = _rows(_mla_prep_fn, [proj_o, pos], [qg, kvg, freqs_k],
                              [(MLA_Q_RANK, BF16), (MLA_KV_RANK, BF16), (LANE, F32), (LANE, F32)], [], name="mla_prep")
    q_all = _mm(qn, w_uq_perm, name="mla_uq")
    kv = _mm(kvn, w_ukv, out_dtype=BF16, name="mla_ukv")
    nope_w = MLA_HEADS * MLA_NOPE
    q_rope = _rope_q_fwd(q_all, pos, freqs_q, name="mla_rope_q")
    k_rope = jnp.concatenate([ko1[:, :half], ko2[:, :half]], axis=1)
    o, lse = _attn_fwd(q_all, q_rope, kv, k_rope, name="attn_fwd")
    h3 = _mm(o, w_oout, res=h2, name="odd_out")
    full.update(_gather_finish(gathers[4], h3, w, 4, False))
    w_fin.append(full["ffn_w_in", 1])
    w_fout.append(full["ffn_w_out", 1])
    h4, ffn1_saved = _ffn_fwd(h3, norm_ffn_g[1:2], w_fin[1], conv_w[1], ffn_conv_b[1:2], w_fout[1], 1)

    dh4, d_final_g, loss_cols = _rows(_loss_fn, [h4, tgt], [final_norm_g[None, :]], [(D_MODEL, F32)],
                                      [(1, D_MODEL), (1, D_MODEL)], name="loss_head")
    loss = lax.psum(jnp.sum(loss_cols), ("x", "y", "c"))

    dh3, gf1 = _ffn_bwd(h3, dh4, ffn1_saved, norm_ffn_g[1:2], w_fin[1], conv_w[1], ffn_conv_b[1:2], w_fout[1], 1)
    scatters = {4: _scatter_start({("ffn_w_in", 1): gf1["w_in"], ("ffn_w_out", 1): gf1["w_out"]}, 4)}
    do = _mm(dh3, w_oout, tb=True, deps=[scatters[4][4]], name="odd_out_dx")
    d_w_oout = _mm(o, dh3, ta=True, out_dtype=BF16, name="odd_out_dw")
    dq_nope, dq_rope, dkv, dk_rope = _attn_bwd(q_all, q_rope, kv, k_rope, o, lse, do, name="attn_bwd")
    lane_pad = ((0, 0), (0, LANE - half))
    dko1, dko2 = jnp.pad(dk_rope[:, :half], lane_pad), jnp.pad(dk_rope[:, half:], lane_pad)
    dq_all = _rope_q_bwd(dq_nope, dq_rope, pos, freqs_q, name="mla_rope_q_bwd")
    d_w_uq_perm = _mm(qn, dq_all, ta=True, out_dtype=BF16, name="mla_uq_dw")
    dqn = _mm(dq_all, w_uq_perm, tb=True, name="mla_uq_dx")
    d_w_ukv = _mm(kvn, dkv, ta=True, out_dtype=BF16, name="mla_ukv_dw")
    dkvn = _mm(dkv, w_ukv, tb=True, name="mla_ukv_dx")
    dproj_o, d_qg, d_kvg = _rows(_mla_prep_bwd_fn, [proj_o, pos, dqn, dkvn, dko1, dko2], [qg, kvg, freqs_k],
                                 [(ODD_IN_PAD, BF16)], [(1, MLA_Q_RANK), (1, MLA_KV_RANK)], name="mla_prep_bwd")
    d_w_oin_pad = _mm(hn1, dproj_o, ta=True, out_dtype=BF16, name="odd_in_dw")
    dhn1 = _mm(dproj_o, w_oin_pad, tb=True, name="odd_in_dx")
    dh2, d_mix_g1 = _rows(_rms_bwd_fn, [h2, dhn1, dh3], [norm_mix_g[1:2]], [(D_MODEL, F32)], [(1, D_MODEL)], name="mix1_dnorm")
    d_w_oin = jnp.concatenate([d_w_oin_pad[:, :kr0 + half], d_w_oin_pad[:, kr0 + LANE:kr0 + LANE + half]], axis=1)
    d_w_uq = jnp.concatenate([d_w_uq_perm[:, :nope_w].reshape(MLA_Q_RANK, MLA_HEADS, MLA_NOPE),
                              d_w_uq_perm[:, nope_w:].reshape(MLA_Q_RANK, MLA_HEADS, MLA_ROPE)], axis=2).reshape(MLA_Q_RANK, -1)
    scatters[3] = _scatter_start({("odd_w_in", 0): d_w_oin, ("mla_w_uq", 0): d_w_uq, ("mla_w_ukv", 0): d_w_ukv,
                                  ("odd_w_out", 0): d_w_oout}, 3)

    dh1, gf0 = _ffn_bwd(h1, dh2, ffn0_saved, norm_ffn_g[0:1], w_fin[0], conv_w[0], ffn_conv_b[0:1], w_fout[0], 0,
                        deps=[scatters[3][4]])
    scatters[2] = _scatter_start({("ffn_w_in", 0): gf0["w_in"], ("ffn_w_out", 0): gf0["w_out"]}, 2)
    dmix = _mm(dh1, w_eout, tb=True, deps=[scatters[2][4]], name="even_out_dx")
    d_w_eout = _mm(mixin, dh1, ta=True, out_dtype=BF16, name="even_out_dw")
    dq, df, di, dg, d_lb0, d_hgrn_g = _hgrn_bwd(proj, lb0, hgrn_norm_g, states, dmix, name="hgrn_bwd")
    dz1, dgl, d_b_glu = _rows(_s5_gate_bwd_fn, [z5, gl, (dmix, HGRN_DIM, S5_DIM)], [s5_b_glu],
                              [(S5_DIM, F32), (S5_DIM, BF16)], [(1, S5_DIM)], name="s5_gate_bwd")
    dz2 = _mm(dgl, w_glu, tb=True, name="s5_glu_dx")
    d_w_glu = _mm(z5, dgl, ta=True, out_dtype=BF16, name="s5_glu_dw")
    dys, du1, d_s5_d = _rows(_s5_act_bwd_fn, [ys, (proj,) + u_cols, dz1, dz2], [s5_d],
                             [(S5_DIM, BF16), (S5_DIM, F32)], [(1, S5_DIM)], name="s5_act_bwd")
    d_c_re = _mm(s_re, dys, ta=True, name="s5_dc_re")
    d_c_im_neg = _mm(s_im, dys, ta=True, name="s5_dc_im")
    lam_re, lam_im, d_ar, d_ai = _s5_scan_bwd(dys, c_re, c_im_neg, s_re, s_im, a_re, a_im, name="s5_scan_bwd")
    du2 = _mm(lam_im, bb_im, tb=True, res=_mm(lam_re, bb_re, tb=True, name="s5_du_re"), name="s5_du_im")
    d_bb_re = _mm(proj, lam_re, ta=True, a_cols=u_cols, name="s5_dbb_re")
    d_bb_im = _mm(proj, lam_im, ta=True, a_cols=u_cols, name="s5_dbb_im")
    sp_g = _s5_params_bwd(*sp_in, d_ar.sum(0).reshape(S5_GROUPS, S5_STATE), d_ai.sum(0).reshape(S5_GROUPS, S5_STATE),
                          _diag_blocks(d_bb_re, S5_GROUP, S5_STATE), _diag_blocks(d_bb_im, S5_GROUP, S5_STATE), d_lb0)
    d_a_re, d_a_im, d_log_dt, d_bt_re, d_bt_im, d_lb_logits = sp_g
    g_small = dict(mla_q_norm_g=d_qg, mla_kv_norm_g=d_kvg, ffn_conv_w=jnp.stack([gf0["conv_w"], gf1["conv_w"]]))
    g_repl = dict(
        norm_ffn_g=jnp.concatenate([gf0["g"], gf1["g"]]),
        final_norm_g=d_final_g[0], hgrn_lb_logits=d_lb_logits, hgrn_norm_g=d_hgrn_g,
        s5_a_re=d_a_re[None], s5_a_im=d_a_im[None], s5_log_dt=d_log_dt[:, 0][None],
        s5_b_re=d_bt_re.transpose(0, 2, 1)[None], s5_b_im=d_bt_im.transpose(0, 2, 1)[None],
        s5_c_re=_diag_blocks(d_c_re, S5_STATE, S5_GROUP).transpose(0, 2, 1)[None],
        s5_c_im=-_diag_blocks(d_c_im_neg, S5_STATE, S5_GROUP).transpose(0, 2, 1)[None],
        s5_d=d_s5_d, s5_b_glu=d_b_glu, ffn_conv_b=jnp.concatenate([gf0["conv_b"], gf1["conv_b"]]))
    scatters[1] = _scatter_start({("s5_w_glu", 0): d_w_glu, ("even_w_out", 0): d_w_eout}, 1,
                                 extra=[_small_sharded_pack(g_small, w), _replicated_pack(g_repl, REPLICATED_EARLY)])
    dproj, = _rows(_dproj_fn, [dq, df, di, dg, du1, du2], [], [(EVEN_IN, BF16)], [], name="even_dproj", deps=[scatters[1][4]])
    d_w_ein = _mm(hn0, dproj, ta=True, out_dtype=BF16, name="even_in_dw")
    dhn0 = _mm(dproj, w_ein, tb=True, name="even_in_dx")
    grad_x, d_mix_g0 = _rows(_rms_bwd_fn, [h0, dhn0, dh1], [norm_mix_g[0:1]], [(D_MODEL, F32)], [(1, D_MODEL)], name="mix0_dnorm")
    g_repl["norm_mix_g"] = jnp.concatenate([d_mix_g0, d_mix_g1])
    scatters[0] = _scatter_start({("even_w_in", 0): d_w_ein}, 0, extra=[_replicated_pack(g_repl, REPLICATED_LATE)])

    delta, new_m, new_v = {}, {}, {}
    updated, partial = {}, {}
    after = scatters[0][4]
    for stage in (4, 3, 2, 1, 0):
        partial[stage] = _scatter_finish(scatters[stage], after, stage)
        for (n, l), slabs in zip(STAGES[stage], partial[stage]):
            updated[n] = _adamw(w[n][l], slabs, mom[n][l], var[n][l], name=f"adamw_{n}_{l}", layer=l,
                                n_layers=w[n].shape[0], into=updated.get(n))
        after = updated[STAGES[stage][-1][0]][0]
    grads = {}
    for n in BIG_NAMES:
        grads[n], delta[n], new_m[n], new_v[n] = updated[n]
    for names, slabs, tag in ((REPLICATED_EARLY, partial[1][-1], "repl"), (REPLICATED_LATE, partial[0][-1], "late"),
                              (SMALL_SHARDED_NAMES, partial[1][-2], "small")):
        packs = [_pack([t[n].reshape(-1) for n in names], LANE, SUBLANE) for t in (w, mom, var)]
        outs = _adamw(packs[0], slabs, packs[1], packs[2], name=f"adamw_{tag}")
        shapes = [w[n].shape for n in names]
        for dst, o_ in zip((grads, delta, new_m, new_v), outs):
            dst.update(zip(names, _unpack(o_.reshape(-1), shapes)))

    return (loss, grad_x[None], *[grads[n] for n in W_NAMES], *[delta[n] for n in W_NAMES],
            *[new_m[n] for n in W_NAMES], *[new_v[n] for n in W_NAMES])
```

```python
import functools
import math

import numpy as np
import jax
import jax.numpy as jnp
from jax import lax
from jax.experimental import pallas as pl
from jax.experimental.pallas import tpu as pltpu

F32 = jnp.float32
BF16 = jnp.bfloat16
_MXU_DTYPE = jnp.bfloat16

D_MODEL = 1024
HGRN_DIM = 512
HGRN_HEAD_DIM = 128
HGRN_HEADS = 4
HGRN_CHUNK = 64
S5_DIM = 512
S5_GROUPS = 32
S5_GROUP = 16
S5_STATE = 64
S5_WIDTH = S5_GROUPS * S5_STATE
EVEN_IN = 4 * HGRN_DIM + S5_DIM
MLA_HEADS = 8
MLA_Q_RANK = 384
MLA_KV_RANK = 256
MLA_NOPE = 128
MLA_ROPE = 64
MLA_V = 128
MLA_QK = MLA_NOPE + MLA_ROPE
ODD_IN = MLA_Q_RANK + MLA_KV_RANK + MLA_ROPE
ODD_IN_PAD = MLA_Q_RANK + MLA_KV_RANK + 2 * 128
ROPE_THETA = 10000.0
D_FF = 2816
EPS = 1e-6
ADAM_LR = 0.001
ADAM_B1 = 0.9
ADAM_B2 = 0.999
ADAM_EPS = 1e-08
ADAM_WD = 0.01
ADAM_STEP = 10

N_DEV = 8
LANE = 128
SUBLANE = 8
VMEM_LIMIT_BYTES = 56 * 1024 * 1024
MESH = pl.DeviceIdType.MESH


def _cparams(sem=None):
    return pltpu.CompilerParams(dimension_semantics=sem, vmem_limit_bytes=VMEM_LIMIT_BYTES)


def _pick(n, cands):
    for c in cands:
        if n % c == 0:
            return c
    raise ValueError(f"no tile for {n} in {cands}")


def _sigmoid(x):
    return 0.5 * jnp.tanh(0.5 * x) + 0.5


def _silu(x):
    return x * _sigmoid(x)


def _gelu(x):
    return 0.5 * x * (1.0 + jnp.tanh(math.sqrt(2.0 / math.pi) * (x + 0.044715 * (x * x * x))))


def _rms(x, g):
    return x * lax.rsqrt(jnp.mean(x * x, axis=-1, keepdims=True) + EPS) * g


def _mxu(a, b, ca, cb):
    return lax.dot_general(a.astype(_MXU_DTYPE), b.astype(_MXU_DTYPE), (((ca,), (cb,)), ((), ())),
                           preferred_element_type=F32)


@functools.partial(jax.custom_vjp, nondiff_argnums=(2, 3))
def _mxu_ad(a, b, ca, cb):
    return _mxu(a, b, ca, cb)


def _mxu_ad_fwd(a, b, ca, cb):
    return _mxu(a, b, ca, cb), (a, b)


def _mxu_ad_bwd(ca, cb, saved, g):
    a, b = saved
    fa, fb = 1 - ca, 1 - cb
    da = _mxu(g, b, 1, fb) if ca == 1 else _mxu(b, g, fb, 1)
    db = _mxu(a, g, fa, 0) if cb == 0 else _mxu(g, a, 0, fa)
    return da, db


_mxu_ad.defvjp(_mxu_ad_fwd, _mxu_ad_bwd)


def _tri(n):
    row = lax.broadcasted_iota(jnp.int32, (n, n), 0)
    col = lax.broadcasted_iota(jnp.int32, (n, n), 1)
    return col <= row


def _cumsum_rows(x, reverse=False):
    n = x.shape[0]
    rowi = lax.broadcasted_iota(jnp.int32, x.shape, 0)
    s = 1
    while s < n:
        if reverse:
            x = x + jnp.where(rowi < n - s, pltpu.roll(x, n - s, 0), 0.0)
        else:
            x = x + jnp.where(rowi >= s, pltpu.roll(x, s, 0), 0.0)
        s *= 2
    return x


@jax.custom_vjp
def _cumsum_rows_ad(x):
    return _cumsum_rows(x)


def _cumsum_rows_ad_fwd(x):
    return _cumsum_rows(x), None


def _cumsum_rows_ad_bwd(_, g):
    return (_cumsum_rows(g, reverse=True),)


_cumsum_rows_ad.defvjp(_cumsum_rows_ad_fwd, _cumsum_rows_ad_bwd)


MM_VMEM_BUDGET = 36 * 1024 * 1024
MM_MAX_TILE = 1408


def _lane_divisors(n, cap, offs=()):
    return [d for d in range(min(n, cap) // LANE * LANE, 0, -LANE) if n % d == 0 and all(o % d == 0 for o in offs)]


def _mm_tiles(M, N, K, sa, sb, so, has_res, m_offs, n_offs, k_offs):
    best = None
    for tm in _lane_divisors(M, MM_MAX_TILE, m_offs):
        for tn in _lane_divisors(N, MM_MAX_TILE, n_offs):
            for tk in _lane_divisors(K, K, k_offs):
                nk = K // tk
                vmem = 2 * (tm * tk * sa + tk * tn * sb + tm * tn * so + tm * tn * 4 * has_res) + (tm * tn * 4 if nk > 1 else 0)
                if vmem <= MM_VMEM_BUDGET:
                    key = (-nk, tm * tn, tn)
                    if best is None or key > best[0]:
                        best = (key, tm, tn, tk)
                    break
    return best[1:]


def _mm(a, b, *, ta=False, tb=False, res=None, out_dtype=F32, a_cols=None, b_rows=None, deps=(), name):
    a_minor = a.shape[1] if a_cols is None else a_cols[1]
    b_major = b.shape[0] if b_rows is None else b_rows[1]
    K, M = (a.shape[0], a_minor) if ta else (a_minor, a.shape[0])
    N = b_major if tb else b.shape[1]
    assert (b.shape[1] if tb else b_major) == K, (a.shape, b.shape, ta, tb)
    assert b_rows is None or not tb
    a_off = 0 if a_cols is None else a_cols[0]
    b_off = 0 if b_rows is None else b_rows[0]
    has_res = res is not None
    tm, tn, tk = _mm_tiles(M, N, K, a.dtype.itemsize, b.dtype.itemsize, jnp.dtype(out_dtype).itemsize, has_res,
                           (a_off,) if ta else (), (), ((a_off,) if not ta else ()) + (b_off,))
    nk = K // tk
    am, ak = (a_off // tm, 0) if ta else (0, a_off // tk)
    bk = b_off // tk
    a_spec = pl.BlockSpec((tk, tm), lambda i, j, k: (k, i + am)) if ta else pl.BlockSpec((tm, tk), lambda i, j, k: (i, k + ak))
    b_spec = pl.BlockSpec((tn, tk), lambda i, j, k: (j, k)) if tb else pl.BlockSpec((tk, tn), lambda i, j, k: (k + bk, j))
    o_spec = pl.BlockSpec((tm, tn), lambda i, j, k: (i, j))
    ca, cb = (0 if ta else 1), (1 if tb else 0)

    n_fixed = 2 + has_res + len(deps)

    def body(*refs):
        a_ref, b_ref = refs[0], refs[1]
        res_ref = refs[2] if has_res else None
        o_ref = refs[n_fixed]
        part = _mxu(a_ref[...], b_ref[...], ca, cb)
        if nk == 1:
            o_ref[...] = (part + res_ref[...] if has_res else part).astype(out_dtype)
            return
        acc_ref = refs[n_fixed + 1]
        k = pl.program_id(2)

        @pl.when(k == 0)
        def _():
            acc_ref[...] = part

        @pl.when(k > 0)
        def _():
            acc_ref[...] += part

        @pl.when(k == nk - 1)
        def _():
            o_ref[...] = (acc_ref[...] + res_ref[...] if has_res else acc_ref[...]).astype(out_dtype)

    ins = [a, b] + ([res] if has_res else []) + list(deps)
    in_specs = [a_spec, b_spec] + ([o_spec] if has_res else []) + [pl.BlockSpec(memory_space=pl.ANY)] * len(deps)
    return pl.pallas_call(
        body, name=name, grid=(M // tm, N // tn, nk),
        in_specs=in_specs, out_specs=o_spec,
        out_shape=jax.ShapeDtypeStruct((M, N), out_dtype),
        scratch_shapes=[pltpu.VMEM((tm, tn), F32)] if nk > 1 else [],
        compiler_params=_cparams(("parallel", "parallel", "arbitrary")),
    )(*ins)


def _rows(fn, row_ins, const_ins, row_outs, acc_outs, *, name, tm=512, deps=()):
    norm = [(r, 0, r.shape[1]) if not isinstance(r, tuple) else r for r in row_ins]
    T = norm[0][0].shape[0]
    tm = min(tm, T)
    nr, nc, no, na = len(norm), len(const_ins), len(row_outs), len(acc_outs)
    first_out = nr + nc + len(deps)

    def body(*refs):
        i = pl.program_id(0)
        vals = [r[...] for r in refs[:nr + nc]]
        outs = fn(*vals)
        for o_ref, o in zip(refs[first_out:first_out + no], outs[:no]):
            o_ref[...] = o.astype(o_ref.dtype)
        for a_ref, o in zip(refs[first_out + no:], outs[no:]):
            @pl.when(i == 0)
            def _(a_ref=a_ref, o=o):
                a_ref[...] = o

            @pl.when(i > 0)
            def _(a_ref=a_ref, o=o):
                a_ref[...] += o

    in_specs = []
    for arr, off, w in norm:
        assert off % w == 0, (off, w)
        in_specs.append(pl.BlockSpec((tm, w), lambda i, b=off // w: (i, b)))
    for c in const_ins:
        in_specs.append(pl.BlockSpec(c.shape, lambda i: (0, 0)))
    in_specs += [pl.BlockSpec(memory_space=pl.ANY)] * len(deps)
    out_specs = [pl.BlockSpec((tm, w), lambda i: (i, 0)) for w, _ in row_outs]
    out_specs += [pl.BlockSpec(s, lambda i: (0, 0)) for s in acc_outs]
    out_shape = [jax.ShapeDtypeStruct((T, w), dt) for w, dt in row_outs]
    out_shape += [jax.ShapeDtypeStruct(s, F32) for s in acc_outs]
    return pl.pallas_call(
        body, name=name, grid=(T // tm,), in_specs=in_specs, out_specs=out_specs, out_shape=out_shape,
        compiler_params=_cparams(("arbitrary",)),
    )(*[n[0] for n in norm], *const_ins, *deps)


FFN_COL_TILE = LANE
FFN_ROW_CHUNK = 512


FFN_HALO = 2 * SUBLANE


def _shift_down(ext, s, rows):
    return pltpu.roll(ext, s, 0)[FFN_HALO:FFN_HALO + rows]


def _shift_up(ext, s, rows):
    return pltpu.roll(ext, rows + FFN_HALO - s, 0)[:rows]


def _ffn_chunks(T):
    r = min(FFN_ROW_CHUNK, T)
    return r, T // r


def _ext_before(ref, c, r):
    if c == 0:
        return jnp.concatenate([jnp.zeros((FFN_HALO, ref.shape[1]), F32), ref[0:r, :].astype(F32)], axis=0)
    return ref[c * r - FFN_HALO:(c + 1) * r, :].astype(F32)


def _ext_after(ref, c, r, nch):
    if c == nch - 1:
        return jnp.concatenate([ref[c * r:(c + 1) * r, :].astype(F32), jnp.zeros((FFN_HALO, ref.shape[1]), F32)], axis=0)
    return ref[c * r:(c + 1) * r + FFN_HALO, :].astype(F32)


def _ffn_mid_fwd(au, conv_w, conv_b, *, name):
    T = au.shape[0]
    tc = FFN_COL_TILE
    ncol = D_FF // tc
    r, nch = _ffn_chunks(T)

    def body(a_ref, u_ref, w_ref, b_ref, z_ref):
        w0, w1, w2, bias = w_ref[0:1, :], w_ref[1:2, :], w_ref[2:3, :], b_ref[...]
        for c in range(nch):
            ext = _ext_before(a_ref, c, r)
            pre = w0 * _shift_down(ext, 2, r) + w1 * _shift_down(ext, 1, r) + w2 * ext[FFN_HALO:] + bias
            z_ref[c * r:(c + 1) * r, :] = (_silu(pre) * u_ref[c * r:(c + 1) * r, :].astype(F32)).astype(z_ref.dtype)

    return pl.pallas_call(
        body, name=name, grid=(ncol,),
        in_specs=[pl.BlockSpec((T, tc), lambda j: (0, j)), pl.BlockSpec((T, tc), lambda j: (0, j + ncol)),
                  pl.BlockSpec((3, tc), lambda j: (0, j)), pl.BlockSpec((1, tc), lambda j: (0, j))],
        out_specs=pl.BlockSpec((T, tc), lambda j: (0, j)),
        out_shape=jax.ShapeDtypeStruct((T, D_FF), BF16),
        compiler_params=_cparams(("parallel",)),
    )(au, au, conv_w, conv_b)


def _ffn_mid_bwd(au, dz, conv_w, conv_b, *, name):
    T = au.shape[0]
    tc = FFN_COL_TILE
    ncol = D_FF // tc
    r, nch = _ffn_chunks(T)

    def body(a_ref, u_ref, dz_ref, w_ref, b_ref, da_ref, du_ref, dw_ref, db_ref, dpre_ref):
        w0, w1, w2, bias = w_ref[0:1, :], w_ref[1:2, :], w_ref[2:3, :], b_ref[...]
        dw0 = jnp.zeros((1, tc), F32)
        dw1 = jnp.zeros((1, tc), F32)
        dw2 = jnp.zeros((1, tc), F32)
        db = jnp.zeros((1, tc), F32)
        for c in range(nch):
            rows = slice(c * r, (c + 1) * r)
            ext = _ext_before(a_ref, c, r)
            a2, a1, a0 = _shift_down(ext, 2, r), _shift_down(ext, 1, r), ext[FFN_HALO:]
            pre = w0 * a2 + w1 * a1 + w2 * a0 + bias
            sg = _sigmoid(pre)
            act = pre * sg
            dzc = dz_ref[rows, :].astype(F32)
            du_ref[rows, :] = (dzc * act).astype(du_ref.dtype)
            dpre = dzc * u_ref[rows, :].astype(F32) * (sg * (1.0 + pre * (1.0 - sg)))
            dpre_ref[rows, :] = dpre
            dw0 += jnp.sum(dpre * a2, axis=0, keepdims=True)
            dw1 += jnp.sum(dpre * a1, axis=0, keepdims=True)
            dw2 += jnp.sum(dpre * a0, axis=0, keepdims=True)
            db += jnp.sum(dpre, axis=0, keepdims=True)
        for c in range(nch):
            ext = _ext_after(dpre_ref, c, r, nch)
            da = w0 * _shift_up(ext, 2, r) + w1 * _shift_up(ext, 1, r) + w2 * ext[:r]
            da_ref[c * r:(c + 1) * r, :] = da.astype(da_ref.dtype)
        dw_ref[0:1, :] = dw0
        dw_ref[1:2, :] = dw1
        dw_ref[2:3, :] = dw2
        db_ref[...] = db

    col = lambda j: (0, j)
    return pl.pallas_call(
        body, name=name, grid=(ncol,),
        in_specs=[pl.BlockSpec((T, tc), col), pl.BlockSpec((T, tc), lambda j: (0, j + ncol)), pl.BlockSpec((T, tc), col),
                  pl.BlockSpec((3, tc), col), pl.BlockSpec((1, tc), col)],
        out_specs=[pl.BlockSpec((T, tc), col), pl.BlockSpec((T, tc), col), pl.BlockSpec((3, tc), col), pl.BlockSpec((1, tc), col)],
        out_shape=[jax.ShapeDtypeStruct((T, D_FF), BF16), jax.ShapeDtypeStruct((T, D_FF), BF16),
                   jax.ShapeDtypeStruct((3, D_FF), F32), jax.ShapeDtypeStruct((1, D_FF), F32)],
        scratch_shapes=[pltpu.VMEM((T, tc), F32)],
        compiler_params=_cparams(("parallel",)),
    )(au, au, dz, conv_w, conv_b)


HGRN_BLOCK = 256


def _hgrn_chunk(dot, cumsum, q, f, i, g, lb, ng, st):
    C = q.shape[0]
    forget = lb + (1.0 - lb) * _sigmoid(f)
    k = 1.0 - forget
    b = cumsum(jnp.log(forget))
    b_last = b[C - 1:C, :]
    qd = q * jnp.exp(b)
    kd = k * jnp.exp(-b)
    att = jnp.where(_tri(C), dot(qd, kd, 1, 1), 0.0)
    o = dot(att, i, 1, 0) + dot(qd, st, 1, 1)
    st_new = st * jnp.exp(b_last) + dot(i, k * jnp.exp(b_last - b), 0, 0)
    on = o * lax.rsqrt(jnp.mean(o * o, axis=-1, keepdims=True) + EPS) * ng
    return on * _silu(g), st_new


def _hgrn_specs(T, rev):
    tb = min(HGRN_BLOCK, T)
    nb = T // tb
    blk = (lambda n: nb - 1 - n) if rev else (lambda n: n)
    hd = HGRN_HEAD_DIM
    proj_specs = [pl.BlockSpec((tb, HGRN_DIM), lambda n, k=k: (blk(n), k)) for k in range(4)]
    vec_spec = pl.BlockSpec((1, HGRN_DIM), lambda n: (0, 0))
    tok_spec = pl.BlockSpec((tb, HGRN_DIM), lambda n: (blk(n), 0))
    st_spec = pl.BlockSpec((HGRN_HEADS, tb // HGRN_CHUNK, hd, hd), lambda n: (0, blk(n), 0, 0))
    return tb, nb, proj_specs, vec_spec, tok_spec, st_spec


def _head_cols(h):
    return slice(h * HGRN_HEAD_DIM, (h + 1) * HGRN_HEAD_DIM)


def _hgrn_fwd(proj, lb, ng, *, name):
    T = proj.shape[0]
    tb, nb, proj_specs, vec_spec, tok_spec, st_spec = _hgrn_specs(T, False)
    nsub = tb // HGRN_CHUNK
    hd = HGRN_HEAD_DIM

    def body(q_ref, f_ref, i_ref, g_ref, lb_ref, ng_ref, y_ref, sts_ref, st_ref):
        @pl.when(pl.program_id(0) == 0)
        def _():
            st_ref[...] = jnp.zeros_like(st_ref)

        st = [st_ref[h] for h in range(HGRN_HEADS)]
        for s in range(nsub):
            rows = slice(s * HGRN_CHUNK, (s + 1) * HGRN_CHUNK)
            for h in range(HGRN_HEADS):
                cols = _head_cols(h)
                sts_ref[h, s] = st[h]
                y, st[h] = _hgrn_chunk(_mxu, _cumsum_rows, q_ref[rows, cols], f_ref[rows, cols], i_ref[rows, cols],
                                       g_ref[rows, cols], lb_ref[:, cols], ng_ref[:, cols], st[h])
                y_ref[rows, cols] = y
        for h in range(HGRN_HEADS):
            st_ref[h] = st[h]

    return pl.pallas_call(
        body, name=name, grid=(nb,),
        in_specs=proj_specs + [vec_spec, vec_spec], out_specs=[tok_spec, st_spec],
        out_shape=[jax.ShapeDtypeStruct((T, HGRN_DIM), F32),
                   jax.ShapeDtypeStruct((HGRN_HEADS, T // HGRN_CHUNK, hd, hd), F32)],
        scratch_shapes=[pltpu.VMEM((HGRN_HEADS, hd, hd), F32)],
        compiler_params=_cparams(("arbitrary",)),
    )(proj, proj, proj, proj, lb, ng)


def _hgrn_bwd(proj, lb, ng, states, dmix, *, name):
    T = proj.shape[0]
    tb, nb, proj_specs, vec_spec, tok_spec, st_spec = _hgrn_specs(T, True)
    nsub = tb // HGRN_CHUNK
    hd = HGRN_HEAD_DIM
    chunk = functools.partial(_hgrn_chunk, _mxu_ad, _cumsum_rows_ad)

    def body(q_ref, f_ref, i_ref, g_ref, lb_ref, ng_ref, sts_ref, dy_ref,
             dq_ref, df_ref, di_ref, dg_ref, dlb_ref, dng_ref, dst_ref):
        @pl.when(pl.program_id(0) == 0)
        def _():
            dst_ref[...] = jnp.zeros_like(dst_ref)
            dlb_ref[...] = jnp.zeros_like(dlb_ref)
            dng_ref[...] = jnp.zeros_like(dng_ref)

        dst = [dst_ref[h] for h in range(HGRN_HEADS)]
        dlb = [jnp.zeros((1, hd), F32)] * HGRN_HEADS
        dng = [jnp.zeros((1, hd), F32)] * HGRN_HEADS
        for s in reversed(range(nsub)):
            rows = slice(s * HGRN_CHUNK, (s + 1) * HGRN_CHUNK)
            for h in range(HGRN_HEADS):
                cols = _head_cols(h)
                _, vjp = jax.vjp(chunk, q_ref[rows, cols], f_ref[rows, cols], i_ref[rows, cols], g_ref[rows, cols],
                                 lb_ref[:, cols], ng_ref[:, cols], sts_ref[h, s])
                dq, df, di, dg, dlb_s, dng_s, dst[h] = vjp((dy_ref[rows, cols], dst[h]))
                dq_ref[rows, cols] = dq
                df_ref[rows, cols] = df
                di_ref[rows, cols] = di
                dg_ref[rows, cols] = dg
                dlb[h] = dlb[h] + dlb_s
                dng[h] = dng[h] + dng_s
        for h in range(HGRN_HEADS):
            dst_ref[h] = dst[h]
            dlb_ref[:, _head_cols(h)] += dlb[h]
            dng_ref[:, _head_cols(h)] += dng[h]

    tok_out = jax.ShapeDtypeStruct((T, HGRN_DIM), F32)
    vec_out = jax.ShapeDtypeStruct((1, HGRN_DIM), F32)
    return pl.pallas_call(
        body, name=name, grid=(nb,),
        in_specs=proj_specs + [vec_spec, vec_spec, st_spec, tok_spec],
        out_specs=[tok_spec] * 4 + [vec_spec, vec_spec],
        out_shape=[tok_out] * 4 + [vec_out, vec_out],
        scratch_shapes=[pltpu.VMEM((HGRN_HEADS, hd, hd), F32)],
        compiler_params=_cparams(("arbitrary",)),
    )(proj, proj, proj, proj, lb, ng, states, dmix)


S5_LANES = 512
S5_ROWS = 512


def _cmul(ar, ai, br, bi):
    return ar * br - ai * bi, ar * bi + ai * br


def _power_table(ar, ai, exps):
    a2 = _cmul(ar, ai, ar, ai)
    a4 = _cmul(*a2, *a2)
    e = exps - 1
    pr = jnp.broadcast_to(ar, exps.shape)
    pi = jnp.broadcast_to(ai, exps.shape)
    for bit, (fr, fi) in enumerate(((ar, ai), a2, a4)):
        nr, ni = _cmul(pr, pi, fr, fi)
        on = ((e >> bit) & 1) == 1
        pr, pi = jnp.where(on, nr, pr), jnp.where(on, ni, pi)
    return pr, pi, a2, a4


def _s5_scan_fwd(x, x_cols, b_re, b_im, a_re, a_im, *, name):
    T = x.shape[0]
    w, tr = S5_LANES, min(S5_ROWS, T)
    ncol, nt = S5_WIDTH // w, T // tr
    assert x_cols[0] % x_cols[1] == 0

    def body(x_ref, br_ref, bi_ref, ar_ref, ai_ref, sr_ref, si_ref, carry_ref):
        @pl.when(pl.program_id(1) == 0)
        def _():
            carry_ref[...] = jnp.zeros_like(carry_ref)

        u = x_ref[...].astype(_MXU_DTYPE)
        sr_ref[...] = _mxu(u, br_ref[...], 1, 0)
        si_ref[...] = _mxu(u, bi_ref[...], 1, 0)
        ar, ai = ar_ref[...], ai_ref[...]
        rowi = lax.broadcasted_iota(jnp.int32, (SUBLANE, w), 0)
        pr, pi, a2, a4 = _power_table(ar, ai, rowi + 1)
        steps = [(s, jnp.where(rowi >= s, fr, 0.0), jnp.where(rowi >= s, fi, 0.0)) for s, (fr, fi) in ((1, (ar, ai)), (2, a2), (4, a4))]

        def tile(i, carry):
            cr, ci = carry
            rows = pl.ds(pl.multiple_of(i * SUBLANE, SUBLANE), SUBLANE)
            xr, xi = sr_ref[rows, :], si_ref[rows, :]
            for s, fr, fi in steps:
                zr, zi = pltpu.roll(xr, s, 0), pltpu.roll(xi, s, 0)
                xr, xi = xr + fr * zr - fi * zi, xi + fr * zi + fi * zr
            xr, xi = xr + pr * cr - pi * ci, xi + pr * ci + pi * cr
            sr_ref[rows, :] = xr
            si_ref[rows, :] = xi
            return xr[SUBLANE - 1:SUBLANE, :], xi[SUBLANE - 1:SUBLANE, :]

        cr, ci = lax.fori_loop(0, tr // SUBLANE, tile, (carry_ref[0:1, :], carry_ref[1:2, :]))
        carry_ref[0:1, :] = cr
        carry_ref[1:2, :] = ci

    out = jax.ShapeDtypeStruct((T, S5_WIDTH), F32)
    return pl.pallas_call(
        body, name=name, grid=(ncol, nt),
        in_specs=[pl.BlockSpec((tr, x_cols[1]), lambda j, t: (t, x_cols[0] // x_cols[1])),
                  pl.BlockSpec((S5_DIM, w), lambda j, t: (0, j)), pl.BlockSpec((S5_DIM, w), lambda j, t: (0, j)),
                  pl.BlockSpec((1, w), lambda j, t: (0, j)), pl.BlockSpec((1, w), lambda j, t: (0, j))],
        out_specs=[pl.BlockSpec((tr, w), lambda j, t: (t, j))] * 2,
        out_shape=[out, out],
        scratch_shapes=[pltpu.VMEM((2, w), F32)],
        compiler_params=_cparams(("parallel", "arbitrary")),
    )(x, b_re, b_im, a_re, a_im)


def _s5_scan_bwd(dy, c_re, c_im, s_re, s_im, a_re, a_im, *, name):
    T = dy.shape[0]
    w, tr = S5_LANES, min(S5_ROWS, T)
    ncol, nt = S5_WIDTH // w, T // tr
    ntile = tr // SUBLANE

    def body(dy_ref, cr_ref, ci_ref, sr_ref, si_ref, ar_ref, ai_ref, lr_ref, li_ref, dar_ref, dai_ref, carry_ref):
        @pl.when(pl.program_id(1) == 0)
        def _():
            carry_ref[...] = jnp.zeros_like(carry_ref)
            dar_ref[...] = jnp.zeros_like(dar_ref)
            dai_ref[...] = jnp.zeros_like(dai_ref)

        dyb = dy_ref[...].astype(_MXU_DTYPE)
        lr_ref[...] = _mxu(dyb, cr_ref[...], 1, 1)
        li_ref[...] = _mxu(dyb, ci_ref[...], 1, 1)
        ar, ai = ar_ref[...], -ai_ref[...]
        rowi = lax.broadcasted_iota(jnp.int32, (SUBLANE, w), 0)
        pr, pi, a2, a4 = _power_table(ar, ai, SUBLANE - rowi)
        last = rowi == SUBLANE - 1
        steps = [(s, jnp.where(rowi < SUBLANE - s, fr, 0.0), jnp.where(rowi < SUBLANE - s, fi, 0.0))
                 for s, (fr, fi) in ((1, (ar, ai)), (2, a2), (4, a4))]

        def tile(i, carry):
            cr, ci, dar, dai = carry
            rows = pl.ds(pl.multiple_of((ntile - 1 - i) * SUBLANE, SUBLANE), SUBLANE)
            xr, xi = lr_ref[rows, :], li_ref[rows, :]
            for s, fr, fi in steps:
                zr, zi = pltpu.roll(xr, SUBLANE - s, 0), pltpu.roll(xi, SUBLANE - s, 0)
                xr, xi = xr + fr * zr - fi * zi, xi + fr * zi + fi * zr
            xr, xi = xr + pr * cr - pi * ci, xi + pr * ci + pi * cr
            lr_ref[rows, :] = xr
            li_ref[rows, :] = xi
            nr = jnp.where(last, cr, pltpu.roll(xr, SUBLANE - 1, 0))
            ni = jnp.where(last, ci, pltpu.roll(xi, SUBLANE - 1, 0))
            sr, si = sr_ref[rows, :], si_ref[rows, :]
            return xr[0:1, :], xi[0:1, :], dar + nr * sr + ni * si, dai + ni * sr - nr * si

        cr, ci, dar, dai = lax.fori_loop(
            0, ntile, tile, (carry_ref[0:1, :], carry_ref[1:2, :], jnp.zeros((SUBLANE, w), F32), jnp.zeros((SUBLANE, w), F32)))
        carry_ref[0:1, :] = cr
        carry_ref[1:2, :] = ci
        dar_ref[...] += dar
        dai_ref[...] += dai

    tok = pl.BlockSpec((tr, w), lambda j, t: (nt - 1 - t, j))
    vec = pl.BlockSpec((1, w), lambda j, t: (0, j))
    acc = pl.BlockSpec((SUBLANE, w), lambda j, t: (0, j))
    out = jax.ShapeDtypeStruct((T, S5_WIDTH), F32)
    accs = jax.ShapeDtypeStruct((SUBLANE, S5_WIDTH), F32)
    return pl.pallas_call(
        body, name=name, grid=(ncol, nt),
        in_specs=[pl.BlockSpec((tr, S5_DIM), lambda j, t: (nt - 1 - t, 0)),
                  pl.BlockSpec((w, S5_DIM), lambda j, t: (j, 0)), pl.BlockSpec((w, S5_DIM), lambda j, t: (j, 0)),
                  tok, tok, vec, vec],
        out_specs=[tok, tok, acc, acc],
        out_shape=[out, out, accs, accs],
        scratch_shapes=[pltpu.VMEM((2, w), F32)],
        compiler_params=_cparams(("parallel", "arbitrary")),
    )(dy, c_re, c_im, s_re, s_im, a_re, a_im)


ATTN_BLOCK = 512
_NEG = -1e30


def _qk_cat(nope, rope):
    return jnp.concatenate([nope.astype(_MXU_DTYPE), rope.astype(_MXU_DTYPE)], axis=1)


_QK_SCALE = MLA_QK ** -0.5
_LOG2E = math.log2(math.e)


def _attn_scores(q, k, diagonal):
    s = _mxu(q, k, 1, 1) * (_QK_SCALE * _LOG2E)
    if diagonal:
        s = jnp.where(_tri(s.shape[0]), s, _NEG)
    return s


def _attn_fwd(q_all, q_rope, kv, k_rope, *, name):
    T = q_all.shape[0]
    tq = min(ATTN_BLOCK, T)
    nq = T // tq

    def body(qn_ref, qr_ref, kn_ref, v_ref, kr_ref, o_ref, lse_ref):
        i = pl.program_id(1)
        q = _qk_cat(qn_ref[...], qr_ref[0])

        def step(j, carry, diagonal):
            m, l, acc = carry
            ks = pl.ds(pl.multiple_of(j * tq, tq), tq)
            s = _attn_scores(q, _qk_cat(kn_ref[ks, :], kr_ref[ks, :]), diagonal)
            m_new = jnp.maximum(m, jnp.max(s, axis=-1, keepdims=True))
            p = jnp.exp2(s - m_new)
            alpha = jnp.exp2(m - m_new)
            return m_new, alpha * l + jnp.sum(p, axis=-1, keepdims=True), alpha * acc + _mxu(p, v_ref[ks, :], 1, 0)

        init = (jnp.full((tq, 1), _NEG, F32), jnp.zeros((tq, 1), F32), jnp.zeros((tq, MLA_V), F32))
        below = lax.fori_loop(0, i, functools.partial(step, diagonal=False), init)
        m, l, acc = step(i, below, diagonal=True)
        o_ref[...] = acc / l
        lse_ref[0] = m + jnp.log2(l)

    return pl.pallas_call(
        body, name=name, grid=(MLA_HEADS, nq),
        in_specs=[pl.BlockSpec((tq, MLA_NOPE), lambda h, i: (i, h)), pl.BlockSpec((1, tq, MLA_ROPE), lambda h, i: (h, i, 0)),
                  pl.BlockSpec((T, MLA_NOPE), lambda h, i: (0, 2 * h)), pl.BlockSpec((T, MLA_V), lambda h, i: (0, 2 * h + 1)),
                  pl.BlockSpec((T, MLA_ROPE), lambda h, i: (0, 0))],
        out_specs=[pl.BlockSpec((tq, MLA_V), lambda h, i: (i, h)), pl.BlockSpec((1, tq, 1), lambda h, i: (h, i, 0))],
        out_shape=[jax.ShapeDtypeStruct((T, MLA_HEADS * MLA_V), F32), jax.ShapeDtypeStruct((MLA_HEADS, T, 1), F32)],
        compiler_params=_cparams(("arbitrary", "arbitrary")),
    )(q_all, q_rope, kv, kv, k_rope)


def _attn_bwd(q_all, q_rope, kv, k_rope, o, lse, do, *, name):
    T = q_all.shape[0]
    tk = min(ATTN_BLOCK, T)
    nk = T // tk

    def body(qn_ref, qr_ref, kv_ref, kr_ref, o_ref, lse_ref, do_ref, dqn_ref, dqr_ref, dkv_ref, dkr_ref, delta_ref):
        h, j = pl.program_id(0), pl.program_id(1)

        @pl.when(j == 0)
        def _():
            dqn_ref[...] = jnp.zeros_like(dqn_ref)
            dqr_ref[...] = jnp.zeros_like(dqr_ref)
            delta_ref[...] = jnp.sum(do_ref[...] * o_ref[...], axis=-1, keepdims=True)

        @pl.when((j == 0) & (h == 0))
        def _():
            dkr_ref[...] = jnp.zeros_like(dkr_ref)

        krows = pl.ds(pl.multiple_of(j * tk, tk), tk)
        k = _qk_cat(kv_ref[:, :MLA_NOPE], kr_ref[krows, :])
        v = kv_ref[:, MLA_NOPE:].astype(_MXU_DTYPE)

        def step(i, carry, diagonal):
            dk, dv = carry
            qs = pl.ds(pl.multiple_of(i * tk, tk), tk)
            q, dob = _qk_cat(qn_ref[qs, :], qr_ref[0, qs, :]), do_ref[qs, :].astype(_MXU_DTYPE)
            p = jnp.exp2(_attn_scores(q, k, diagonal) - lse_ref[0, qs, :])
            ds = p * (_mxu(dob, v, 1, 1) - delta_ref[qs, :]) * _QK_SCALE
            dq = _mxu(ds, k, 1, 0)
            dqn_ref[qs, :] += dq[:, :MLA_NOPE]
            dqr_ref[0, qs, :] += dq[:, MLA_NOPE:]
            return dk + _mxu(ds, q, 0, 0), dv + _mxu(p, dob, 0, 0)

        on_diagonal = step(j, (jnp.zeros((tk, MLA_QK), F32), jnp.zeros((tk, MLA_V), F32)), diagonal=True)
        dk, dv = lax.fori_loop(j + 1, nk, functools.partial(step, diagonal=False), on_diagonal)
        dkv_ref[:, :MLA_NOPE] = dk[:, :MLA_NOPE].astype(dkv_ref.dtype)
        dkv_ref[:, MLA_NOPE:] = dv.astype(dkv_ref.dtype)
        dkr_ref[krows, :] += dk[:, MLA_NOPE:]

    head_cols = pl.BlockSpec((T, MLA_NOPE), lambda h, j: (0, h))
    head_rope = pl.BlockSpec((1, T, MLA_ROPE), lambda h, j: (h, 0, 0))
    kv_spec = pl.BlockSpec((tk, MLA_NOPE + MLA_V), lambda h, j: (j, h))
    kr_spec = pl.BlockSpec((T, MLA_ROPE), lambda h, j: (0, 0))
    return pl.pallas_call(
        body, name=name, grid=(MLA_HEADS, nk),
        in_specs=[head_cols, head_rope, kv_spec, kr_spec, head_cols, pl.BlockSpec((1, T, 1), lambda h, j: (h, 0, 0)), head_cols],
        out_specs=[head_cols, head_rope, kv_spec, kr_spec],
        out_shape=[jax.ShapeDtypeStruct((T, MLA_HEADS * MLA_NOPE), F32), jax.ShapeDtypeStruct((MLA_HEADS, T, MLA_ROPE), F32),
                   jax.ShapeDtypeStruct((T, MLA_HEADS * (MLA_NOPE + MLA_V)), BF16), jax.ShapeDtypeStruct((T, MLA_ROPE), F32)],
        scratch_shapes=[pltpu.VMEM((T, 1), F32)],
        compiler_params=_cparams(("arbitrary", "arbitrary")),
    )(q_all, q_rope, kv, k_rope, o, lse, do)


def _s5_discretize(a_re, a_im, log_dt, bt_re, bt_im, lb_logits):
    dt = jnp.exp(log_dt)
    mag = jnp.exp(a_re * dt)
    abr, abi = mag * jnp.cos(a_im * dt), mag * jnp.sin(a_im * dt)
    den = a_re * a_re + a_im * a_im
    xr, xi = abr - 1.0, abi
    cr = ((xr * a_re + xi * a_im) / den)[:, None, :]
    ci = ((xi * a_re - xr * a_im) / den)[:, None, :]
    e = jnp.exp(lb_logits - jnp.max(lb_logits, axis=0, keepdims=True))
    lb = e[0:1, :] / jnp.sum(e, axis=0, keepdims=True)
    return abr, abi, cr * bt_re - ci * bt_im, cr * bt_im + ci * bt_re, lb


def _whole(shape):
    return pl.BlockSpec(shape, lambda: (0,) * len(shape))


def _s5_params_fwd(a_re, a_im, log_dt, bt_re, bt_im, lb_logits):
    ins = (a_re, a_im, log_dt, bt_re, bt_im, lb_logits)
    outs = [jax.ShapeDtypeStruct(s, F32) for s in (a_re.shape, a_re.shape, bt_re.shape, bt_re.shape, (1, lb_logits.shape[1]))]

    def body(*refs):
        res = _s5_discretize(*[r[...] for r in refs[:6]])
        for o_ref, o in zip(refs[6:], res):
            o_ref[...] = o

    return pl.pallas_call(body, name="s5_params_fwd", in_specs=[_whole(a.shape) for a in ins],
                          out_specs=[_whole(o.shape) for o in outs], out_shape=outs, compiler_params=_cparams())(*ins)


def _s5_params_bwd(a_re, a_im, log_dt, bt_re, bt_im, lb_logits, d_abr, d_abi, d_bbr, d_bbi, d_lb):
    ins = (a_re, a_im, log_dt, bt_re, bt_im, lb_logits, d_abr, d_abi, d_bbr, d_bbi, d_lb)
    outs = [jax.ShapeDtypeStruct(a.shape, F32) for a in ins[:6]]

    def body(*refs):
        _, vjp = jax.vjp(_s5_discretize, *[r[...] for r in refs[:6]])
        for o_ref, o in zip(refs[11:], vjp(tuple(r[...] for r in refs[6:11]))):
            o_ref[...] = o

    return pl.pallas_call(body, name="s5_params_bwd", in_specs=[_whole(a.shape) for a in ins],
                          out_specs=[_whole(o.shape) for o in outs], out_shape=outs, compiler_params=_cparams())(*ins)


ADAMW_WHOLE_BYTES = 1024 * 1024


def _adamw(w, g, m, v, *, name, layer=0, n_layers=1, into=None):
    R, C = w.shape
    whole = R % SUBLANE != 0 or R * C * w.dtype.itemsize <= ADAMW_WHOLE_BYTES
    tr = R if whole else _pick(R, (256, 128, 64, 32, 16, 8))
    slabs = g.shape[0]
    n_prev = 0 if into is None else len(into)

    def body(w_ref, g_ref, m_ref, v_ref, *rest):
        g_out, d_ref, mo_ref, vo_ref = rest[n_prev:]
        gv = g_ref[0].astype(F32)
        for s in range(1, slabs):
            gv = gv + g_ref[s].astype(F32)
        m2 = ADAM_B1 * m_ref[...] + (1.0 - ADAM_B1) * gv
        v2 = ADAM_B2 * v_ref[...] + (1.0 - ADAM_B2) * (gv * gv)
        m_hat = m2 / (1.0 - ADAM_B1 ** ADAM_STEP)
        v_hat = v2 / (1.0 - ADAM_B2 ** ADAM_STEP)
        g_out[0] = gv
        d_ref[0] = -ADAM_LR * (m_hat / (jnp.sqrt(v_hat) + ADAM_EPS) + ADAM_WD * w_ref[...])
        mo_ref[0] = m2
        vo_ref[0] = v2

    spec = pl.BlockSpec((tr, C), lambda i: (i, 0))
    out_spec = pl.BlockSpec((1, tr, C), lambda i: (layer, i, 0))
    out = jax.ShapeDtypeStruct((n_layers, R, C), F32)
    return pl.pallas_call(
        body, name=name, grid=(R // tr,),
        in_specs=[spec, pl.BlockSpec((slabs, tr, C), lambda i: (0, i, 0)), spec, spec] + [pl.BlockSpec(memory_space=pl.ANY)] * n_prev,
        out_specs=[out_spec] * 4, out_shape=[out] * 4, input_output_aliases={4 + k: k for k in range(n_prev)},
        compiler_params=_cparams(("parallel",)))(w, g, m, v, *(into or ()))


N_CHIPS = 4
N_CORES = 2


_FLIPS = tuple((dx, dy, dc) for dx in (0, 1) for dy in (0, 1) for dc in (0, 1) if (dx, dy, dc) != (0, 0, 0))


_HBM = pl.BlockSpec(memory_space=pltpu.HBM)
_SEM = pl.BlockSpec(memory_space=pltpu.SEMAPHORE)
_SPLIT_COPY = pltpu.CompilerParams(has_side_effects=pltpu.SideEffectType.DATAFLOW_SIDE_EFFECTING)


def _exchange_copies(src_refs, land_refs, send_sems, recv_sems, scatter, arriving):
    x, y, c = lax.axis_index("x"), lax.axis_index("y"), lax.axis_index("c")
    me_chip = 2 * x + y
    copies = []
    for a, (s_ref, l_ref) in enumerate(zip(src_refs, land_refs)):
        for j, (dx, dy, dc) in enumerate(_FLIPS):
            px, py, pc = (1 - x if dx else x), (1 - y if dy else y), (1 - c if dc else c)
            k = a * len(_FLIPS) + j
            p_chip = 2 * px + py
            copies.append(pltpu.make_async_remote_copy(
                src_ref=s_ref.at[p_chip, pc] if scatter else s_ref, dst_ref=l_ref.at[p_chip, pc] if arriving else l_ref.at[me_chip, c],
                send_sem=send_sems.at[k], recv_sem=recv_sems.at[k], device_id=(px, py, pc), device_id_type=MESH))
    return copies


def _exchange_start(srcs, *, scatter, name, after=()):
    n_arr = len(srcs)
    n_sem = n_arr * len(_FLIPS)
    n_in = 2 * n_arr + len(after)
    lands = [lax.empty(s.shape if scatter else (N_CHIPS, N_CORES) + s.shape, s.dtype) for s in srcs]

    def body(*refs):
        src_refs, land_refs = refs[:n_arr], refs[n_arr:2 * n_arr]
        for cp in _exchange_copies(src_refs, land_refs, refs[n_in], refs[n_in + 1], scatter, arriving=False):
            cp.start()
        refs[-1][...] = jnp.zeros_like(refs[-1])

    thru = [pltpu.HBM(a.shape, a.dtype) for a in srcs + lands]
    outs = pl.pallas_call(
        body, name=name,
        out_shape=(pltpu.SemaphoreType.DMA((n_sem,)), pltpu.SemaphoreType.DMA((n_sem,)), *thru,
                   jax.ShapeDtypeStruct((SUBLANE, LANE), F32)),
        in_specs=[_HBM] * (2 * n_arr) + [pl.BlockSpec(memory_space=pl.ANY)] * len(after),
        out_specs=(_SEM, _SEM, *[_HBM] * (2 * n_arr), pl.BlockSpec(memory_space=pltpu.VMEM)),
        input_output_aliases={i: 2 + i for i in range(2 * n_arr)}, compiler_params=_SPLIT_COPY,
    )(*[pltpu.with_memory_space_constraint(a, pltpu.HBM) for a in srcs + lands], *after)
    return outs[0], outs[1], list(outs[2:2 + n_arr]), list(outs[2 + n_arr:2 + 2 * n_arr]), outs[-1]


def _exchange_wait(started, after, *, scatter, name):
    send_sems, recv_sems, srcs, lands, _ = started
    n_arr = len(srcs)

    def body(*refs):
        src_refs, land_refs = refs[:n_arr], refs[n_arr:2 * n_arr]
        for cp in _exchange_copies(src_refs, land_refs, refs[2 * n_arr], refs[2 * n_arr + 1], scatter, arriving=True):
            cp.wait_send()
            cp.wait_recv()

    outs = pl.pallas_call(
        body, name=name, out_shape=[pltpu.HBM(a.shape, a.dtype) for a in srcs + lands],
        in_specs=[_HBM] * (2 * n_arr) + [_SEM, _SEM, pl.BlockSpec(memory_space=pl.ANY)], out_specs=[_HBM] * (2 * n_arr),
        input_output_aliases={i: i for i in range(2 * n_arr)}, compiler_params=_SPLIT_COPY,
    )(*srcs, *lands, send_sems, recv_sems, after)
    return list(outs[:n_arr]), list(outs[n_arr:])


def _with_own(land, own):
    me_chip = 2 * lax.axis_index("x") + lax.axis_index("y")
    return lax.dynamic_update_slice(land, own[None, None], (me_chip, lax.axis_index("c")) + (0,) * own.ndim)


def _rms_fwd_fn(h, g):
    return (_rms(h, g),)


def _rms_bwd_fn(h, dhn, dres, g):
    _, vjp = jax.vjp(_rms, h, g)
    dh, dg = vjp(dhn)
    return dh + dres, dg


def _loss_fn(h, tgt, g):
    y, vjp = jax.vjp(_rms, h, g)
    diff = y - tgt
    dh, dg = vjp(diff * (1.0 / D_MODEL))
    return dh, dg, (0.5 / D_MODEL) * jnp.sum(diff * diff, axis=0, keepdims=True)


def _s5_act(ys, u, d):
    return _gelu(ys + d * u)


def _s5_gate(z, gl, b):
    return z * _sigmoid(gl + b)


def _s5_act_fn(ys, u, d):
    return (_s5_act(ys, u, d),)


def _s5_mix_fn(ya, z, gl, b):
    return (jnp.concatenate([ya, _s5_gate(z, gl, b)], axis=1),)


def _s5_gate_bwd_fn(z, gl, dyb, b):
    _, vjp = jax.vjp(_s5_gate, z, gl, b)
    return vjp(dyb)


def _s5_act_bwd_fn(ys, u, dz1, dz2, d):
    _, vjp = jax.vjp(_s5_act, ys, u, d)
    return vjp(dz1 + dz2)


def _dproj_fn(dq, df, di, dg, du1, du2):
    return (jnp.concatenate([dq, df, di, dg, du1 + du2], axis=1),)


def _rope_pair(r1, r2, pos, freqs):
    ang = pos.astype(F32) * freqs
    c, s = jnp.cos(ang), jnp.sin(ang)
    return r1 * c - r2 * s, r1 * s + r2 * c


_ODD_SPLITS = (0, MLA_Q_RANK, MLA_Q_RANK + MLA_KV_RANK, MLA_Q_RANK + MLA_KV_RANK + LANE, ODD_IN_PAD)


def _mla_prep(cq, ckv, k1, k2, qg, kvg, pos, freqs):
    ko1, ko2 = _rope_pair(k1, k2, pos, freqs)
    return _rms(cq, qg), _rms(ckv, kvg), ko1, ko2


def _mla_prep_fn(proj, pos, qg, kvg, freqs):
    parts = [proj[:, a:b] for a, b in zip(_ODD_SPLITS[:-1], _ODD_SPLITS[1:])]
    return _mla_prep(*parts, qg, kvg, pos, freqs)


def _mla_prep_bwd_fn(proj, pos, dqn, dkvn, dko1, dko2, qg, kvg, freqs):
    parts = [proj[:, a:b] for a, b in zip(_ODD_SPLITS[:-1], _ODD_SPLITS[1:])]
    _, vjp = jax.vjp(lambda *a: _mla_prep(*a, pos, freqs), *parts, qg, kvg)
    dcq, dckv, dk1, dk2, dqg, dkvg = vjp((dqn, dkvn, dko1, dko2))
    return jnp.concatenate([dcq, dckv, dk1, dk2], axis=1), dqg, dkvg


ROPE_ROWS = 512


def _rope_heads(x, pos, freqs):
    half = MLA_ROPE // 2
    ang = pos.astype(F32) * freqs
    lane = lax.broadcasted_iota(jnp.int32, x.shape, 1)
    first = (lane % MLA_ROPE) < half
    other = jnp.where(first, pltpu.roll(x, x.shape[1] - half, 1), pltpu.roll(x, half, 1))
    return x * jnp.cos(ang) + other * jnp.where(first, -jnp.sin(ang), jnp.sin(ang))


def _rope_q_fwd(q_all, pos, freqs, *, name):
    T = q_all.shape[0]
    tm = min(ROPE_ROWS, T)
    nope_w, rope_w = MLA_HEADS * MLA_NOPE, MLA_HEADS * MLA_ROPE

    def body(r_ref, pos_ref, f_ref, o_ref):
        out = _rope_heads(r_ref[...], pos_ref[...], f_ref[...])
        for h in range(MLA_HEADS):
            o_ref[h] = out[:, h * MLA_ROPE:(h + 1) * MLA_ROPE]

    return pl.pallas_call(
        body, name=name, grid=(T // tm,),
        in_specs=[pl.BlockSpec((tm, rope_w), lambda i: (i, nope_w // rope_w)), pl.BlockSpec((tm, 1), lambda i: (i, 0)),
                  pl.BlockSpec((1, rope_w), lambda i: (0, 0))],
        out_specs=pl.BlockSpec((MLA_HEADS, tm, MLA_ROPE), lambda i: (0, i, 0)),
        out_shape=jax.ShapeDtypeStruct((MLA_HEADS, T, MLA_ROPE), F32),
        compiler_params=_cparams(("parallel",)),
    )(q_all, pos, freqs)


def _rope_q_bwd(dq_nope, dq_rope, pos, freqs, *, name):
    T = dq_nope.shape[0]
    tm = min(ROPE_ROWS, T)
    nope_w, rope_w = MLA_HEADS * MLA_NOPE, MLA_HEADS * MLA_ROPE

    def body(dn_ref, dr_ref, pos_ref, f_ref, o_ref):
        d_out = jnp.concatenate([dr_ref[h] for h in range(MLA_HEADS)], axis=1)
        o_ref[:, :nope_w] = dn_ref[...].astype(o_ref.dtype)
        o_ref[:, nope_w:] = _rope_heads(d_out, pos_ref[...], -f_ref[...]).astype(o_ref.dtype)

    return pl.pallas_call(
        body, name=name, grid=(T // tm,),
        in_specs=[pl.BlockSpec((tm, nope_w), lambda i: (i, 0)), pl.BlockSpec((MLA_HEADS, tm, MLA_ROPE), lambda i: (0, i, 0)),
                  pl.BlockSpec((tm, 1), lambda i: (i, 0)), pl.BlockSpec((1, rope_w), lambda i: (0, 0))],
        out_specs=pl.BlockSpec((tm, nope_w + rope_w), lambda i: (i, 0)),
        out_shape=jax.ShapeDtypeStruct((T, nope_w + rope_w), BF16),
        compiler_params=_cparams(("parallel",)),
    )(dq_nope, dq_rope, pos, freqs)


W_NAMES = ("norm_mix_g", "norm_ffn_g", "final_norm_g", "even_w_in", "hgrn_lb_logits", "hgrn_norm_g", "s5_a_re", "s5_a_im",
           "s5_log_dt", "s5_b_re", "s5_b_im", "s5_c_re", "s5_c_im", "s5_d", "s5_w_glu", "s5_b_glu", "even_w_out", "odd_w_in",
           "mla_q_norm_g", "mla_w_uq", "mla_kv_norm_g", "mla_w_ukv", "odd_w_out", "ffn_w_in", "ffn_conv_w", "ffn_conv_b",
           "ffn_w_out")
BIG_UNITS = (("even_w_in", 0, True), ("s5_w_glu", 0, False), ("even_w_out", 0, False), ("odd_w_in", 0, False),
             ("mla_w_uq", 0, True), ("mla_w_ukv", 0, True), ("odd_w_out", 0, False),
             ("ffn_w_in", 0, True), ("ffn_w_in", 1, True), ("ffn_w_out", 0, False), ("ffn_w_out", 1, False))
BIG_NAMES = tuple(dict.fromkeys(u[0] for u in BIG_UNITS))
SMALL_SHARDED = (("mla_q_norm_g", 1), ("mla_kv_norm_g", 1), ("ffn_conv_w", 2))
SMALL_SHARDED_NAMES = tuple(s[0] for s in SMALL_SHARDED)
REPLICATED = tuple(n for n in W_NAMES if n not in BIG_NAMES + SMALL_SHARDED_NAMES)
REPLICATED_LATE = ("norm_mix_g",)
REPLICATED_EARLY = tuple(n for n in REPLICATED if n not in REPLICATED_LATE)


def _pack(flats, cols, row_mult):
    flat = jnp.concatenate(flats, axis=-1)
    pad = (-flat.shape[-1]) % (cols * row_mult)
    flat = jnp.pad(flat, [(0, 0)] * (flat.ndim - 1) + [(0, pad)])
    return flat.reshape(flat.shape[:-1] + (-1, cols))


def _unpack(flat, shapes):
    out, off = [], 0
    for shp in shapes:
        n = int(np.prod(shp))
        out.append(flat[..., off:off + n].reshape(flat.shape[:-1] + tuple(shp)))
        off += n
    return out


UNIT_TRANSPOSED = {(n, l): t for n, l, t in BIG_UNITS}


def _unit_block(arrs, unit):
    blk = arrs[unit[0]][unit[1]]
    return blk.T if UNIT_TRANSPOSED[unit] else blk
STAGES = ((("even_w_in", 0),),
          (("s5_w_glu", 0), ("even_w_out", 0)),
          (("ffn_w_in", 0), ("ffn_w_out", 0)),
          (("odd_w_in", 0), ("mla_w_uq", 0), ("mla_w_ukv", 0), ("odd_w_out", 0)),
          (("ffn_w_in", 1), ("ffn_w_out", 1)))


def _gather_start(w, stage, with_small, after):
    srcs = [_unit_block(w, unit).astype(BF16) for unit in STAGES[stage]]
    if with_small:
        srcs.append(_pack([w[n].reshape(-1) for n in SMALL_SHARDED_NAMES], LANE, SUBLANE))
    return _exchange_start(srcs, scatter=False, name=f"gather_start_{stage}", after=after)


def _gather_finish(started, after, w, stage, with_small):
    srcs, lands = _exchange_wait(started, after, scatter=False, name=f"gather_wait_{stage}")
    lands = [_with_own(land, src) for land, src in zip(lands, srcs)]
    big = {unit: g.reshape(N_DEV * g.shape[2], g.shape[3]) for unit, g in zip(STAGES[stage], lands)}
    if not with_small:
        return big
    parts = _unpack(lands[-1].reshape(N_DEV, -1), [w[n].shape for n in SMALL_SHARDED_NAMES])
    small = {}
    for (n, ax), p in zip(SMALL_SHARDED, parts):
        shp = list(w[n].shape)
        shp[ax] *= N_DEV
        small[n] = jnp.moveaxis(p, 0, ax).reshape(shp)
    return big, small


def _scatter_start(g_big, stage, extra=()):
    srcs = []
    for unit in STAGES[stage]:
        g = g_big[unit].astype(BF16)
        srcs.append(g.reshape(N_CHIPS, N_CORES, g.shape[0] // N_DEV, g.shape[1]))
    return _exchange_start(srcs + list(extra), scatter=True, name=f"scatter_start_{stage}")


def _scatter_finish(started, after, stage):
    srcs, lands = _exchange_wait(started, after, scatter=True, name=f"scatter_wait_{stage}")
    me_chip, c = 2 * lax.axis_index("x") + lax.axis_index("y"), lax.axis_index("c")
    outs = []
    for land, src in zip(lands, srcs):
        own = lax.dynamic_slice(src, (me_chip, c) + (0,) * (src.ndim - 2), (1, 1) + src.shape[2:])[0, 0]
        outs.append(_with_own(land, own).reshape((N_DEV,) + land.shape[2:]))
    return outs


def _small_sharded_pack(g_small, w):
    flats = []
    for n, ax in SMALL_SHARDED:
        shp = list(w[n].shape)
        g = g_small[n].astype(F32).reshape(shp[:ax] + [N_DEV] + shp[ax:])
        flats.append(jnp.moveaxis(g, ax, 0).reshape(N_DEV, -1))
    small = _pack(flats, LANE, SUBLANE)
    return small.reshape((N_CHIPS, N_CORES) + small.shape[1:])


def _replicated_pack(g_repl, names):
    vec = _pack([g_repl[n].reshape(-1).astype(F32) for n in names], LANE, SUBLANE)
    return jnp.broadcast_to(vec, (N_CHIPS, N_CORES) + vec.shape)


def _block_diag(blocks):
    G, a, b = blocks.shape
    return jnp.einsum('gab,gk->gakb', blocks, jnp.eye(G, dtype=blocks.dtype)).reshape(G * a, G * b)


def _diag_blocks(mat, a, b):
    G = mat.shape[0] // a
    return jnp.einsum('gagb->gab', mat.reshape(G, a, G, b))


def _ffn_fwd(h, g, w_in_t, conv_w, conv_b, w_out, tag):
    hn, = _rows(_rms_fwd_fn, [h], [g], [(D_MODEL, BF16)], [], name=f"ffn{tag}_norm")
    au = _mm(hn, w_in_t, tb=True, out_dtype=BF16, name=f"ffn{tag}_in")
    z = _ffn_mid_fwd(au, conv_w, conv_b, name=f"ffn{tag}_mid")
    return _mm(z, w_out, res=h, name=f"ffn{tag}_out"), (hn, au, z)


def _ffn_bwd(h, dh, saved, g, w_in_t, conv_w, conv_b, w_out, tag, deps=()):
    hn, au, z = saved
    dz = _mm(dh, w_out, tb=True, out_dtype=BF16, deps=deps, name=f"ffn{tag}_dz")
    dw_out = _mm(z, dh, ta=True, out_dtype=BF16, name=f"ffn{tag}_dwout")
    da, du, dcw, dcb = _ffn_mid_bwd(au, dz, conv_w, conv_b, name=f"ffn{tag}_dmid")
    dhn = _mm(da, w_in_t, b_rows=(0, D_FF), name=f"ffn{tag}_dhn_a")
    dhn = _mm(du, w_in_t, b_rows=(D_FF, D_FF), res=dhn, name=f"ffn{tag}_dhn_u")
    dw_in = jnp.concatenate([_mm(da, hn, ta=True, out_dtype=BF16, name=f"ffn{tag}_dwin_a"),
                             _mm(du, hn, ta=True, out_dtype=BF16, name=f"ffn{tag}_dwin_u")], axis=0)
    dh_in, dg = _rows(_rms_bwd_fn, [h, dhn, dh], [g], [(D_MODEL, F32)], [(1, D_MODEL)], name=f"ffn{tag}_dnorm")
    return dh_in, dict(g=dg, w_in=dw_in, conv_w=dcw, conv_b=dcb, w_out=dw_out)


def kernel(x, positions, norm_mix_g, norm_ffn_g, final_norm_g, even_w_in, hgrn_lb_logits, hgrn_norm_g, s5_a_re, s5_a_im, s5_log_dt, s5_b_re, s5_b_im, s5_c_re, s5_c_im, s5_d, s5_w_glu, s5_b_glu, even_w_out, odd_w_in, mla_q_norm_g, mla_w_uq, mla_kv_norm_g, mla_w_ukv, odd_w_out, ffn_w_in, ffn_conv_w, ffn_conv_b, ffn_w_out, loss_target, m_norm_mix_g, m_norm_ffn_g, m_final_norm_g, m_even_w_in, m_hgrn_lb_logits, m_hgrn_norm_g, m_s5_a_re, m_s5_a_im, m_s5_log_dt, m_s5_b_re, m_s5_b_im, m_s5_c_re, m_s5_c_im, m_s5_d, m_s5_w_glu, m_s5_b_glu, m_even_w_out, m_odd_w_in, m_mla_q_norm_g, m_mla_w_uq, m_mla_kv_norm_g, m_mla_w_ukv, m_odd_w_out, m_ffn_w_in, m_ffn_conv_w, m_ffn_conv_b, m_ffn_w_out, v_norm_mix_g, v_norm_ffn_g, v_final_norm_g, v_even_w_in, v_hgrn_lb_logits, v_hgrn_norm_g, v_s5_a_re, v_s5_a_im, v_s5_log_dt, v_s5_b_re, v_s5_b_im, v_s5_c_re, v_s5_c_im, v_s5_d, v_s5_w_glu, v_s5_b_glu, v_even_w_out, v_odd_w_in, v_mla_q_norm_g, v_mla_w_uq, v_mla_kv_norm_g, v_mla_w_ukv, v_odd_w_out, v_ffn_w_in, v_ffn_conv_w, v_ffn_conv_b, v_ffn_w_out):
    given = dict(locals())
    w = {n: given[n] for n in W_NAMES}
    mom = {n: given["m_" + n] for n in W_NAMES}
    var = {n: given["v_" + n] for n in W_NAMES}
    T = x.shape[1]
    h0 = x[0]
    tgt = loss_target[0]
    pos = positions.reshape(T, 1)

    gathers = []
    for s in range(len(STAGES)):
        gathers.append(_gather_start(w, s, with_small=(s == 1), after=[g[4] for g in gathers[-1:]]))
    half = MLA_ROPE // 2
    kr0 = MLA_Q_RANK + MLA_KV_RANK
    freqs = ROPE_THETA ** (-jnp.arange(0, MLA_ROPE, 2, dtype=F32) / MLA_ROPE)
    freqs_q = jnp.tile(jnp.concatenate([freqs, freqs]), MLA_HEADS)[None, :]
    freqs_k = jnp.concatenate([freqs, jnp.zeros((LANE - half,), F32)])[None, :]

    sp_in = (s5_a_re[0], s5_a_im[0], s5_log_dt[0][:, None], s5_b_re[0].transpose(0, 2, 1), s5_b_im[0].transpose(0, 2, 1),
             hgrn_lb_logits)
    abr, abi, bbt_re, bbt_im, lb0 = _s5_params_fwd(*sp_in)
    a_re, a_im = abr.reshape(1, S5_WIDTH), abi.reshape(1, S5_WIDTH)
    bb_re, bb_im = _block_diag(bbt_re).astype(BF16), _block_diag(bbt_im).astype(BF16)
    c_re = _block_diag(s5_c_re[0].transpose(0, 2, 1)).astype(BF16)
    c_im_neg = _block_diag(-s5_c_im[0].transpose(0, 2, 1)).astype(BF16)
    u_cols = (4 * HGRN_DIM, S5_DIM)

    hn0, = _rows(_rms_fwd_fn, [h0], [norm_mix_g[0:1]], [(D_MODEL, BF16)], [], name="mix0_norm", deps=[gathers[-1][4]])
    full = _gather_finish(gathers[0], hn0, w, 0, False)
    w_ein_t = full["even_w_in", 0]
    proj = _mm(hn0, w_ein_t, tb=True, name="even_in")
    y_a, states = _hgrn_fwd(proj, lb0, hgrn_norm_g, name="hgrn_fwd")
    s_re, s_im = _s5_scan_fwd(proj, u_cols, bb_re, bb_im, a_re, a_im, name="s5_scan_fwd")
    more, full_small = _gather_finish(gathers[1], s_re, w, 1, True)
    w_glu, w_eout = more["s5_w_glu", 0], more["even_w_out", 0]
    qg, kvg, conv_w = full_small["mla_q_norm_g"], full_small["mla_kv_norm_g"], full_small["ffn_conv_w"]
    ys = _mm(s_im, c_im_neg, res=_mm(s_re, c_re, name="s5_y_re"), name="s5_y_im")
    z5, = _rows(_s5_act_fn, [ys, (proj,) + u_cols], [s5_d], [(S5_DIM, F32)], [], name="s5_act")
    gl = _mm(z5, w_glu, name="s5_glu")
    mixin, = _rows(_s5_mix_fn, [y_a, z5, gl], [s5_b_glu], [(D_MODEL, BF16)], [], name="s5_mix")
    h1 = _mm(mixin, w_eout, res=h0, name="even_out")
    full.update(_gather_finish(gathers[2], h1, w, 2, False))
    w_fin, w_fout = [full["ffn_w_in", 0]], [full["ffn_w_out", 0]]
    h2, ffn0_saved = _ffn_fwd(h1, norm_ffn_g[0:1], w_fin[0], conv_w[0], ffn_conv_b[0:1], w_fout[0], 0)

    full.update(_gather_finish(gathers[3], h2, w, 3, False))
    w_oin, w_ukv_t, w_oout = full["odd_w_in", 0], full["mla_w_ukv", 0], full["odd_w_out", 0]
    zpad = jnp.zeros((D_MODEL, LANE - half), BF16)
    w_oin_pad = jnp.concatenate([w_oin[:, :kr0], w_oin[:, kr0:kr0 + half], zpad, w_oin[:, kr0 + half:], zpad], axis=1)
    w_uq3 = full["mla_w_uq", 0].reshape(MLA_HEADS, MLA_QK, MLA_Q_RANK)
    w_uq_perm_t = jnp.concatenate([w_uq3[:, :MLA_NOPE].reshape(-1, MLA_Q_RANK),
                                   w_uq3[:, MLA_NOPE:].reshape(-1, MLA_Q_RANK)], axis=0)
    hn1, = _rows(_rms_fwd_fn, [h2], [norm_mix_g[1:2]], [(D_MODEL, BF16)], [], name="mix1_norm")
    proj_o = _mm(hn1, w_oin_pad, name="odd_in")
    qn, kvn, ko1, ko2 = _rows(_mla_prep_fn, [proj_o, pos], [qg, kvg, freqs_k],
                              [(MLA_Q_RANK, BF16), (MLA_KV_RANK, BF16), (LANE, F32), (LANE, F32)], [], name="mla_prep")
    q_all = _mm(qn, w_uq_perm_t, tb=True, name="mla_uq")
    kv = _mm(kvn, w_ukv_t, tb=True, out_dtype=BF16, name="mla_ukv")
    nope_w = MLA_HEADS * MLA_NOPE
    q_rope = _rope_q_fwd(q_all, pos, freqs_q, name="mla_rope_q")
    k_rope = jnp.concatenate([ko1[:, :half], ko2[:, :half]], axis=1)
    o, lse = _attn_fwd(q_all, q_rope, kv, k_rope, name="attn_fwd")
    h3 = _mm(o, w_oout, res=h2, name="odd_out")
    full.update(_gather_finish(gathers[4], h3, w, 4, False))
    w_fin.append(full["ffn_w_in", 1])
    w_fout.append(full["ffn_w_out", 1])
    h4, ffn1_saved = _ffn_fwd(h3, norm_ffn_g[1:2], w_fin[1], conv_w[1], ffn_conv_b[1:2], w_fout[1], 1)

    dh4, d_final_g, loss_cols = _rows(_loss_fn, [h4, tgt], [final_norm_g[None, :]], [(D_MODEL, F32)],
                                      [(1, D_MODEL), (1, D_MODEL)], name="loss_head")
    loss = lax.psum(jnp.sum(loss_cols), ("x", "y", "c"))

    dh3, gf1 = _ffn_bwd(h3, dh4, ffn1_saved, norm_ffn_g[1:2], w_fin[1], conv_w[1], ffn_conv_b[1:2], w_fout[1], 1)
    scatters = {4: _scatter_start({("ffn_w_in", 1): gf1["w_in"], ("ffn_w_out", 1): gf1["w_out"]}, 4)}
    do = _mm(dh3, w_oout, tb=True, deps=[scatters[4][4]], name="odd_out_dx")
    d_w_oout = _mm(o, dh3, ta=True, out_dtype=BF16, name="odd_out_dw")
    dq_nope, dq_rope, dkv, dk_rope = _attn_bwd(q_all, q_rope, kv, k_rope, o, lse, do, name="attn_bwd")
    lane_pad = ((0, 0), (0, LANE - half))
    dko1, dko2 = jnp.pad(dk_rope[:, :half], lane_pad), jnp.pad(dk_rope[:, half:], lane_pad)
    dq_all = _rope_q_bwd(dq_nope, dq_rope, pos, freqs_q, name="mla_rope_q_bwd")
    d_w_uq_perm_t = _mm(dq_all, qn, ta=True, out_dtype=BF16, name="mla_uq_dw")
    dqn = _mm(dq_all, w_uq_perm_t, name="mla_uq_dx")
    d_w_ukv_t = _mm(dkv, kvn, ta=True, out_dtype=BF16, name="mla_ukv_dw")
    dkvn = _mm(dkv, w_ukv_t, name="mla_ukv_dx")
    dproj_o, d_qg, d_kvg = _rows(_mla_prep_bwd_fn, [proj_o, pos, dqn, dkvn, dko1, dko2], [qg, kvg, freqs_k],
                                 [(ODD_IN_PAD, BF16)], [(1, MLA_Q_RANK), (1, MLA_KV_RANK)], name="mla_prep_bwd")
    d_w_oin_pad = _mm(hn1, dproj_o, ta=True, out_dtype=BF16, name="odd_in_dw")
    dhn1 = _mm(dproj_o, w_oin_pad, tb=True, name="odd_in_dx")
    dh2, d_mix_g1 = _rows(_rms_bwd_fn, [h2, dhn1, dh3], [norm_mix_g[1:2]], [(D_MODEL, F32)], [(1, D_MODEL)], name="mix1_dnorm")
    d_w_oin = jnp.concatenate([d_w_oin_pad[:, :kr0 + half], d_w_oin_pad[:, kr0 + LANE:kr0 + LANE + half]], axis=1)
    d_w_uq_t = jnp.concatenate([d_w_uq_perm_t[:nope_w].reshape(MLA_HEADS, MLA_NOPE, MLA_Q_RANK),
                                d_w_uq_perm_t[nope_w:].reshape(MLA_HEADS, MLA_ROPE, MLA_Q_RANK)], axis=1).reshape(-1, MLA_Q_RANK)
    scatters[3] = _scatter_start({("odd_w_in", 0): d_w_oin, ("mla_w_uq", 0): d_w_uq_t, ("mla_w_ukv", 0): d_w_ukv_t,
                                  ("odd_w_out", 0): d_w_oout}, 3)

    dh1, gf0 = _ffn_bwd(h1, dh2, ffn0_saved, norm_ffn_g[0:1], w_fin[0], conv_w[0], ffn_conv_b[0:1], w_fout[0], 0,
                        deps=[scatters[3][4]])
    scatters[2] = _scatter_start({("ffn_w_in", 0): gf0["w_in"], ("ffn_w_out", 0): gf0["w_out"]}, 2)
    dmix = _mm(dh1, w_eout, tb=True, deps=[scatters[2][4]], name="even_out_dx")
    d_w_eout = _mm(mixin, dh1, ta=True, out_dtype=BF16, name="even_out_dw")
    dq, df, di, dg, d_lb0, d_hgrn_g = _hgrn_bwd(proj, lb0, hgrn_norm_g, states, dmix, name="hgrn_bwd")
    dz1, dgl, d_b_glu = _rows(_s5_gate_bwd_fn, [z5, gl, (dmix, HGRN_DIM, S5_DIM)], [s5_b_glu],
                              [(S5_DIM, F32), (S5_DIM, BF16)], [(1, S5_DIM)], name="s5_gate_bwd")
    dz2 = _mm(dgl, w_glu, tb=True, name="s5_glu_dx")
    d_w_glu = _mm(z5, dgl, ta=True, out_dtype=BF16, name="s5_glu_dw")
    dys, du1, d_s5_d = _rows(_s5_act_bwd_fn, [ys, (proj,) + u_cols, dz1, dz2], [s5_d],
                             [(S5_DIM, BF16), (S5_DIM, F32)], [(1, S5_DIM)], name="s5_act_bwd")
    d_c_re = _mm(s_re, dys, ta=True, name="s5_dc_re")
    d_c_im_neg = _mm(s_im, dys, ta=True, name="s5_dc_im")
    lam_re, lam_im, d_ar, d_ai = _s5_scan_bwd(dys, c_re, c_im_neg, s_re, s_im, a_re, a_im, name="s5_scan_bwd")
    du2 = _mm(lam_im, bb_im, tb=True, res=_mm(lam_re, bb_re, tb=True, name="s5_du_re"), name="s5_du_im")
    d_bb_re = _mm(proj, lam_re, ta=True, a_cols=u_cols, name="s5_dbb_re")
    d_bb_im = _mm(proj, lam_im, ta=True, a_cols=u_cols, name="s5_dbb_im")
    sp_g = _s5_params_bwd(*sp_in, d_ar.sum(0).reshape(S5_GROUPS, S5_STATE), d_ai.sum(0).reshape(S5_GROUPS, S5_STATE),
                          _diag_blocks(d_bb_re, S5_GROUP, S5_STATE), _diag_blocks(d_bb_im, S5_GROUP, S5_STATE), d_lb0)
    d_a_re, d_a_im, d_log_dt, d_bt_re, d_bt_im, d_lb_logits = sp_g
    g_small = dict(mla_q_norm_g=d_qg, mla_kv_norm_g=d_kvg, ffn_conv_w=jnp.stack([gf0["conv_w"], gf1["conv_w"]]))
    g_repl = dict(
        norm_ffn_g=jnp.concatenate([gf0["g"], gf1["g"]]),
        final_norm_g=d_final_g[0], hgrn_lb_logits=d_lb_logits, hgrn_norm_g=d_hgrn_g,
        s5_a_re=d_a_re[None], s5_a_im=d_a_im[None], s5_log_dt=d_log_dt[:, 0][None],
        s5_b_re=d_bt_re.transpose(0, 2, 1)[None], s5_b_im=d_bt_im.transpose(0, 2, 1)[None],
        s5_c_re=_diag_blocks(d_c_re, S5_STATE, S5_GROUP).transpose(0, 2, 1)[None],
        s5_c_im=-_diag_blocks(d_c_im_neg, S5_STATE, S5_GROUP).transpose(0, 2, 1)[None],
        s5_d=d_s5_d, s5_b_glu=d_b_glu, ffn_conv_b=jnp.concatenate([gf0["conv_b"], gf1["conv_b"]]))
    scatters[1] = _scatter_start({("s5_w_glu", 0): d_w_glu, ("even_w_out", 0): d_w_eout}, 1,
                                 extra=[_small_sharded_pack(g_small, w), _replicated_pack(g_repl, REPLICATED_EARLY)])
    dproj, = _rows(_dproj_fn, [dq, df, di, dg, du1, du2], [], [(EVEN_IN, BF16)], [], name="even_dproj", deps=[scatters[1][4]])
    d_w_ein_t = _mm(dproj, hn0, ta=True, out_dtype=BF16, name="even_in_dw")
    dhn0 = _mm(dproj, w_ein_t, name="even_in_dx")
    grad_x, d_mix_g0 = _rows(_rms_bwd_fn, [h0, dhn0, dh1], [norm_mix_g[0:1]], [(D_MODEL, F32)], [(1, D_MODEL)], name="mix0_dnorm")
    g_repl["norm_mix_g"] = jnp.concatenate([d_mix_g0, d_mix_g1])
    scatters[0] = _scatter_start({("even_w_in", 0): d_w_ein_t}, 0, extra=[_replicated_pack(g_repl, REPLICATED_LATE)])

    delta, new_m, new_v = {}, {}, {}
    updated, partial = {}, {}
    after = scatters[0][4]
    for stage in (4, 3, 2, 1, 0):
        partial[stage] = _scatter_finish(scatters[stage], after, stage)
        for (n, l), slabs in zip(STAGES[stage], partial[stage]):
            updated[n] = _adamw(_unit_block(w, (n, l)), slabs, _unit_block(mom, (n, l)), _unit_block(var, (n, l)),
                                name=f"adamw_{n}_{l}", layer=l, n_layers=w[n].shape[0], into=updated.get(n))
        after = updated[STAGES[stage][-1][0]][0]
    grads = {}
    for n in BIG_NAMES:
        outs = [o.transpose(0, 2, 1) for o in updated[n]] if UNIT_TRANSPOSED[n, 0] else updated[n]
        grads[n], delta[n], new_m[n], new_v[n] = outs
    for names, slabs, tag in ((REPLICATED_EARLY, partial[1][-1], "repl"), (REPLICATED_LATE, partial[0][-1], "late"),
                              (SMALL_SHARDED_NAMES, partial[1][-2], "small")):
        packs = [_pack([t[n].reshape(-1) for n in names], LANE, SUBLANE) for t in (w, mom, var)]
        outs = _adamw(packs[0], slabs, packs[1], packs[2], name=f"adamw_{tag}")
        shapes = [w[n].shape for n in names]
        for dst, o_ in zip((grads, delta, new_m, new_v), outs):
            dst.update(zip(names, _unpack(o_.reshape(-1), shapes)))

    return (loss, grad_x[None], *[grads[n] for n in W_NAMES], *[delta[n] for n in W_NAMES],
            *[new_m[n] for n in W_NAMES], *[new_v[n] for n in W_NAMES])
```

```python
import functools
import math

import numpy as np
import jax
import jax.numpy as jnp
from jax import lax
from jax.experimental import pallas as pl
from jax.experimental.pallas import tpu as pltpu

F32 = jnp.float32
BF16 = jnp.bfloat16
_MXU_DTYPE = jnp.bfloat16

D_MODEL = 1024
HGRN_DIM = 512
HGRN_HEAD_DIM = 128
HGRN_HEADS = 4
HGRN_CHUNK = 64
S5_DIM = 512
S5_GROUPS = 32
S5_GROUP = 16
S5_STATE = 64
S5_WIDTH = S5_GROUPS * S5_STATE
EVEN_IN = 4 * HGRN_DIM + S5_DIM
MLA_HEADS = 8
MLA_Q_RANK = 384
MLA_KV_RANK = 256
MLA_NOPE = 128
MLA_ROPE = 64
MLA_V = 128
MLA_QK = MLA_NOPE + MLA_ROPE
ODD_IN = MLA_Q_RANK + MLA_KV_RANK + MLA_ROPE
ODD_IN_PAD = MLA_Q_RANK + MLA_KV_RANK + 2 * 128
ROPE_THETA = 10000.0
D_FF = 2816
EPS = 1e-6
ADAM_LR = 0.001
ADAM_B1 = 0.9
ADAM_B2 = 0.999
ADAM_EPS = 1e-08
ADAM_WD = 0.01
ADAM_STEP = 10

N_DEV = 8
LANE = 128
SUBLANE = 8
VMEM_LIMIT_BYTES = 56 * 1024 * 1024
MESH = pl.DeviceIdType.MESH


def _cparams(sem=None):
    return pltpu.CompilerParams(dimension_semantics=sem, vmem_limit_bytes=VMEM_LIMIT_BYTES)


def _pick(n, cands):
    for c in cands:
        if n % c == 0:
            return c
    raise ValueError(f"no tile for {n} in {cands}")


def _sigmoid(x):
    return 0.5 * jnp.tanh(0.5 * x) + 0.5


def _silu(x):
    return x * _sigmoid(x)


def _gelu(x):
    return 0.5 * x * (1.0 + jnp.tanh(math.sqrt(2.0 / math.pi) * (x + 0.044715 * (x * x * x))))


def _rms(x, g):
    return x * lax.rsqrt(jnp.mean(x * x, axis=-1, keepdims=True) + EPS) * g


def _mxu(a, b, ca, cb):
    return lax.dot_general(a.astype(_MXU_DTYPE), b.astype(_MXU_DTYPE), (((ca,), (cb,)), ((), ())),
                           preferred_element_type=F32)


@functools.partial(jax.custom_vjp, nondiff_argnums=(2, 3))
def _mxu_ad(a, b, ca, cb):
    return _mxu(a, b, ca, cb)


def _mxu_ad_fwd(a, b, ca, cb):
    return _mxu(a, b, ca, cb), (a, b)


def _mxu_ad_bwd(ca, cb, saved, g):
    a, b = saved
    fa, fb = 1 - ca, 1 - cb
    da = _mxu(g, b, 1, fb) if ca == 1 else _mxu(b, g, fb, 1)
    db = _mxu(a, g, fa, 0) if cb == 0 else _mxu(g, a, 0, fa)
    return da, db


_mxu_ad.defvjp(_mxu_ad_fwd, _mxu_ad_bwd)


def _tri(n):
    row = lax.broadcasted_iota(jnp.int32, (n, n), 0)
    col = lax.broadcasted_iota(jnp.int32, (n, n), 1)
    return col <= row


def _cumsum_rows(x, reverse=False):
    n = x.shape[0]
    rowi = lax.broadcasted_iota(jnp.int32, x.shape, 0)
    s = 1
    while s < n:
        if reverse:
            x = x + jnp.where(rowi < n - s, pltpu.roll(x, n - s, 0), 0.0)
        else:
            x = x + jnp.where(rowi >= s, pltpu.roll(x, s, 0), 0.0)
        s *= 2
    return x


@jax.custom_vjp
def _cumsum_rows_ad(x):
    return _cumsum_rows(x)


def _cumsum_rows_ad_fwd(x):
    return _cumsum_rows(x), None


def _cumsum_rows_ad_bwd(_, g):
    return (_cumsum_rows(g, reverse=True),)


_cumsum_rows_ad.defvjp(_cumsum_rows_ad_fwd, _cumsum_rows_ad_bwd)


MM_VMEM_BUDGET = 36 * 1024 * 1024
MM_MAX_TILE = 1408


def _lane_divisors(n, cap, offs=()):
    return [d for d in range(min(n, cap) // LANE * LANE, 0, -LANE) if n % d == 0 and all(o % d == 0 for o in offs)]


def _mm_tiles(M, N, K, sa, sb, so, has_res, m_offs, n_offs, k_offs):
    best = None
    for tm in _lane_divisors(M, MM_MAX_TILE, m_offs):
        for tn in _lane_divisors(N, MM_MAX_TILE, n_offs):
            for tk in _lane_divisors(K, K, k_offs):
                nk = K // tk
                vmem = 2 * (tm * tk * sa + tk * tn * sb + tm * tn * so + tm * tn * 4 * has_res) + (tm * tn * 4 if nk > 1 else 0)
                if vmem <= MM_VMEM_BUDGET:
                    key = (-nk, tm * tn, tn)
                    if best is None or key > best[0]:
                        best = (key, tm, tn, tk)
                    break
    return best[1:]


def _mm(a, b, *, ta=False, tb=False, res=None, out_dtype=F32, a_cols=None, b_rows=None, deps=(), out_rows=None, into=None,
        name):
    a_minor = a.shape[1] if a_cols is None else a_cols[1]
    b_major = b.shape[0] if b_rows is None else b_rows[1]
    K, M = (a.shape[0], a_minor) if ta else (a_minor, a.shape[0])
    N = b_major if tb else b.shape[1]
    assert (b.shape[1] if tb else b_major) == K, (a.shape, b.shape, ta, tb)
    assert b_rows is None or not tb
    a_off = 0 if a_cols is None else a_cols[0]
    b_off = 0 if b_rows is None else b_rows[0]
    has_res = res is not None
    o_off, o_total = (0, M) if out_rows is None else out_rows
    tm, tn, tk = _mm_tiles(M, N, K, a.dtype.itemsize, b.dtype.itemsize, jnp.dtype(out_dtype).itemsize, has_res,
                           ((a_off,) if ta else ()) + (o_off,), (), ((a_off,) if not ta else ()) + (b_off,))
    nk = K // tk
    am, ak = (a_off // tm, 0) if ta else (0, a_off // tk)
    bk, om = b_off // tk, o_off // tm
    a_spec = pl.BlockSpec((tk, tm), lambda i, j, k: (k, i + am)) if ta else pl.BlockSpec((tm, tk), lambda i, j, k: (i, k + ak))
    b_spec = pl.BlockSpec((tn, tk), lambda i, j, k: (j, k)) if tb else pl.BlockSpec((tk, tn), lambda i, j, k: (k + bk, j))
    o_spec = pl.BlockSpec((tm, tn), lambda i, j, k: (i, j))
    ca, cb = (0 if ta else 1), (1 if tb else 0)
    extra = list(deps) + ([into] if into is not None else [])

    n_fixed = 2 + has_res + len(extra)

    def body(*refs):
        a_ref, b_ref = refs[0], refs[1]
        res_ref = refs[2] if has_res else None
        o_ref = refs[n_fixed]
        part = _mxu(a_ref[...], b_ref[...], ca, cb)
        if nk == 1:
            o_ref[...] = (part + res_ref[...] if has_res else part).astype(out_dtype)
            return
        acc_ref = refs[n_fixed + 1]
        k = pl.program_id(2)

        @pl.when(k == 0)
        def _():
            acc_ref[...] = part

        @pl.when(k > 0)
        def _():
            acc_ref[...] += part

        @pl.when(k == nk - 1)
        def _():
            o_ref[...] = (acc_ref[...] + res_ref[...] if has_res else acc_ref[...]).astype(out_dtype)

    ins = [a, b] + ([res] if has_res else []) + extra
    in_specs = [a_spec, b_spec] + ([o_spec] if has_res else []) + [pl.BlockSpec(memory_space=pl.ANY)] * len(extra)
    return pl.pallas_call(
        body, name=name, grid=(M // tm, N // tn, nk),
        in_specs=in_specs, out_specs=pl.BlockSpec((tm, tn), lambda i, j, k: (i + om, j)),
        out_shape=jax.ShapeDtypeStruct((o_total, N), out_dtype),
        scratch_shapes=[pltpu.VMEM((tm, tn), F32)] if nk > 1 else [],
        input_output_aliases={} if into is None else {len(ins) - 1: 0},
        compiler_params=_cparams(("parallel", "parallel", "arbitrary")),
    )(*ins)


def _rows(fn, row_ins, const_ins, row_outs, acc_outs, *, name, tm=512, deps=()):
    norm = [(r, 0, r.shape[1]) if not isinstance(r, tuple) else r for r in row_ins]
    T = norm[0][0].shape[0]
    tm = min(tm, T)
    nr, nc, no, na = len(norm), len(const_ins), len(row_outs), len(acc_outs)
    first_out = nr + nc + len(deps)

    def body(*refs):
        i = pl.program_id(0)
        vals = [r[...] for r in refs[:nr + nc]]
        outs = fn(*vals)
        for o_ref, o in zip(refs[first_out:first_out + no], outs[:no]):
            o_ref[...] = o.astype(o_ref.dtype)
        for a_ref, o in zip(refs[first_out + no:], outs[no:]):
            @pl.when(i == 0)
            def _(a_ref=a_ref, o=o):
                a_ref[...] = o

            @pl.when(i > 0)
            def _(a_ref=a_ref, o=o):
                a_ref[...] += o

    in_specs = []
    for arr, off, w in norm:
        assert off % w == 0, (off, w)
        in_specs.append(pl.BlockSpec((tm, w), lambda i, b=off // w: (i, b)))
    for c in const_ins:
        in_specs.append(pl.BlockSpec(c.shape, lambda i: (0, 0)))
    in_specs += [pl.BlockSpec(memory_space=pl.ANY)] * len(deps)
    out_specs = [pl.BlockSpec((tm, w), lambda i: (i, 0)) for w, _ in row_outs]
    out_specs += [pl.BlockSpec(s, lambda i: (0, 0)) for s in acc_outs]
    out_shape = [jax.ShapeDtypeStruct((T, w), dt) for w, dt in row_outs]
    out_shape += [jax.ShapeDtypeStruct(s, F32) for s in acc_outs]
    return pl.pallas_call(
        body, name=name, grid=(T // tm,), in_specs=in_specs, out_specs=out_specs, out_shape=out_shape,
        compiler_params=_cparams(("arbitrary",)),
    )(*[n[0] for n in norm], *const_ins, *deps)


FFN_COL_TILE = LANE
FFN_ROW_CHUNK = 512


FFN_HALO = 2 * SUBLANE


def _shift_down(ext, s, rows):
    return pltpu.roll(ext, s, 0)[FFN_HALO:FFN_HALO + rows]


def _shift_up(ext, s, rows):
    return pltpu.roll(ext, rows + FFN_HALO - s, 0)[:rows]


def _ffn_chunks(T):
    r = min(FFN_ROW_CHUNK, T)
    return r, T // r


def _ext_before(ref, c, r):
    if c == 0:
        return jnp.concatenate([jnp.zeros((FFN_HALO, ref.shape[1]), F32), ref[0:r, :].astype(F32)], axis=0)
    return ref[c * r - FFN_HALO:(c + 1) * r, :].astype(F32)


def _ext_after(ref, c, r, nch):
    if c == nch - 1:
        return jnp.concatenate([ref[c * r:(c + 1) * r, :].astype(F32), jnp.zeros((FFN_HALO, ref.shape[1]), F32)], axis=0)
    return ref[c * r:(c + 1) * r + FFN_HALO, :].astype(F32)


def _ffn_mid_fwd(au, conv_w, conv_b, *, name):
    T = au.shape[0]
    tc = FFN_COL_TILE
    ncol = D_FF // tc
    r, nch = _ffn_chunks(T)

    def body(a_ref, u_ref, w_ref, b_ref, z_ref):
        w0, w1, w2, bias = w_ref[0:1, :], w_ref[1:2, :], w_ref[2:3, :], b_ref[...]
        for c in range(nch):
            ext = _ext_before(a_ref, c, r)
            pre = w0 * _shift_down(ext, 2, r) + w1 * _shift_down(ext, 1, r) + w2 * ext[FFN_HALO:] + bias
            z_ref[c * r:(c + 1) * r, :] = (_silu(pre) * u_ref[c * r:(c + 1) * r, :].astype(F32)).astype(z_ref.dtype)

    return pl.pallas_call(
        body, name=name, grid=(ncol,),
        in_specs=[pl.BlockSpec((T, tc), lambda j: (0, j)), pl.BlockSpec((T, tc), lambda j: (0, j + ncol)),
                  pl.BlockSpec((3, tc), lambda j: (0, j)), pl.BlockSpec((1, tc), lambda j: (0, j))],
        out_specs=pl.BlockSpec((T, tc), lambda j: (0, j)),
        out_shape=jax.ShapeDtypeStruct((T, D_FF), BF16),
        compiler_params=_cparams(("parallel",)),
    )(au, au, conv_w, conv_b)


def _ffn_mid_bwd(au, dz, conv_w, conv_b, *, name):
    T = au.shape[0]
    tc = FFN_COL_TILE
    ncol = D_FF // tc
    r, nch = _ffn_chunks(T)

    def body(a_ref, u_ref, dz_ref, w_ref, b_ref, da_ref, du_ref, dw_ref, db_ref, dpre_ref):
        w0, w1, w2, bias = w_ref[0:1, :], w_ref[1:2, :], w_ref[2:3, :], b_ref[...]
        dw0 = jnp.zeros((1, tc), F32)
        dw1 = jnp.zeros((1, tc), F32)
        dw2 = jnp.zeros((1, tc), F32)
        db = jnp.zeros((1, tc), F32)
        for c in range(nch):
            rows = slice(c * r, (c + 1) * r)
            ext = _ext_before(a_ref, c, r)
            a2, a1, a0 = _shift_down(ext, 2, r), _shift_down(ext, 1, r), ext[FFN_HALO:]
            pre = w0 * a2 + w1 * a1 + w2 * a0 + bias
            sg = _sigmoid(pre)
            act = pre * sg
            dzc = dz_ref[rows, :].astype(F32)
            du_ref[rows, :] = (dzc * act).astype(du_ref.dtype)
            dpre = dzc * u_ref[rows, :].astype(F32) * (sg * (1.0 + pre * (1.0 - sg)))
            dpre_ref[rows, :] = dpre
            dw0 += jnp.sum(dpre * a2, axis=0, keepdims=True)
            dw1 += jnp.sum(dpre * a1, axis=0, keepdims=True)
            dw2 += jnp.sum(dpre * a0, axis=0, keepdims=True)
            db += jnp.sum(dpre, axis=0, keepdims=True)
        for c in range(nch):
            ext = _ext_after(dpre_ref, c, r, nch)
            da = w0 * _shift_up(ext, 2, r) + w1 * _shift_up(ext, 1, r) + w2 * ext[:r]
            da_ref[c * r:(c + 1) * r, :] = da.astype(da_ref.dtype)
        dw_ref[0:1, :] = dw0
        dw_ref[1:2, :] = dw1
        dw_ref[2:3, :] = dw2
        db_ref[...] = db

    col = lambda j: (0, j)
    return pl.pallas_call(
        body, name=name, grid=(ncol,),
        in_specs=[pl.BlockSpec((T, tc), col), pl.BlockSpec((T, tc), lambda j: (0, j + ncol)), pl.BlockSpec((T, tc), col),
                  pl.BlockSpec((3, tc), col), pl.BlockSpec((1, tc), col)],
        out_specs=[pl.BlockSpec((T, tc), col), pl.BlockSpec((T, tc), col), pl.BlockSpec((3, tc), col), pl.BlockSpec((1, tc), col)],
        out_shape=[jax.ShapeDtypeStruct((T, D_FF), BF16), jax.ShapeDtypeStruct((T, D_FF), BF16),
                   jax.ShapeDtypeStruct((3, D_FF), F32), jax.ShapeDtypeStruct((1, D_FF), F32)],
        scratch_shapes=[pltpu.VMEM((T, tc), F32)],
        compiler_params=_cparams(("parallel",)),
    )(au, au, dz, conv_w, conv_b)


HGRN_BLOCK = 256


def _hgrn_chunk(dot, cumsum, q, f, i, g, lb, ng, st):
    C = q.shape[0]
    forget = lb + (1.0 - lb) * _sigmoid(f)
    k = 1.0 - forget
    b = cumsum(jnp.log(forget))
    b_last = b[C - 1:C, :]
    qd = q * jnp.exp(b)
    kd = k * jnp.exp(-b)
    att = jnp.where(_tri(C), dot(qd, kd, 1, 1), 0.0)
    o = dot(att, i, 1, 0) + dot(qd, st, 1, 1)
    st_new = st * jnp.exp(b_last) + dot(i, k * jnp.exp(b_last - b), 0, 0)
    on = o * lax.rsqrt(jnp.mean(o * o, axis=-1, keepdims=True) + EPS) * ng
    return on * _silu(g), st_new


def _hgrn_specs(T, rev):
    tb = min(HGRN_BLOCK, T)
    nb = T // tb
    blk = (lambda n: nb - 1 - n) if rev else (lambda n: n)
    hd = HGRN_HEAD_DIM
    proj_specs = [pl.BlockSpec((tb, HGRN_DIM), lambda n, k=k: (blk(n), k)) for k in range(4)]
    vec_spec = pl.BlockSpec((1, HGRN_DIM), lambda n: (0, 0))
    tok_spec = pl.BlockSpec((tb, HGRN_DIM), lambda n: (blk(n), 0))
    st_spec = pl.BlockSpec((HGRN_HEADS, tb // HGRN_CHUNK, hd, hd), lambda n: (0, blk(n), 0, 0))
    return tb, nb, proj_specs, vec_spec, tok_spec, st_spec


def _head_cols(h):
    return slice(h * HGRN_HEAD_DIM, (h + 1) * HGRN_HEAD_DIM)


def _hgrn_fwd(proj, lb, ng, *, name):
    T = proj.shape[0]
    tb, nb, proj_specs, vec_spec, tok_spec, st_spec = _hgrn_specs(T, False)
    nsub = tb // HGRN_CHUNK
    hd = HGRN_HEAD_DIM

    def body(q_ref, f_ref, i_ref, g_ref, lb_ref, ng_ref, y_ref, sts_ref, st_ref):
        @pl.when(pl.program_id(0) == 0)
        def _():
            st_ref[...] = jnp.zeros_like(st_ref)

        st = [st_ref[h] for h in range(HGRN_HEADS)]
        for s in range(nsub):
            rows = slice(s * HGRN_CHUNK, (s + 1) * HGRN_CHUNK)
            for h in range(HGRN_HEADS):
                cols = _head_cols(h)
                sts_ref[h, s] = st[h]
                y, st[h] = _hgrn_chunk(_mxu, _cumsum_rows, q_ref[rows, cols], f_ref[rows, cols], i_ref[rows, cols],
                                       g_ref[rows, cols], lb_ref[:, cols], ng_ref[:, cols], st[h])
                y_ref[rows, cols] = y
        for h in range(HGRN_HEADS):
            st_ref[h] = st[h]

    return pl.pallas_call(
        body, name=name, grid=(nb,),
        in_specs=proj_specs + [vec_spec, vec_spec], out_specs=[tok_spec, st_spec],
        out_shape=[jax.ShapeDtypeStruct((T, HGRN_DIM), F32),
                   jax.ShapeDtypeStruct((HGRN_HEADS, T // HGRN_CHUNK, hd, hd), F32)],
        scratch_shapes=[pltpu.VMEM((HGRN_HEADS, hd, hd), F32)],
        compiler_params=_cparams(("arbitrary",)),
    )(proj, proj, proj, proj, lb, ng)


def _hgrn_bwd(proj, lb, ng, states, dmix, *, name):
    T = proj.shape[0]
    tb, nb, proj_specs, vec_spec, tok_spec, st_spec = _hgrn_specs(T, True)
    nsub = tb // HGRN_CHUNK
    hd = HGRN_HEAD_DIM
    chunk = functools.partial(_hgrn_chunk, _mxu_ad, _cumsum_rows_ad)

    def body(q_ref, f_ref, i_ref, g_ref, lb_ref, ng_ref, sts_ref, dy_ref,
             dq_ref, df_ref, di_ref, dg_ref, dlb_ref, dng_ref, dst_ref):
        @pl.when(pl.program_id(0) == 0)
        def _():
            dst_ref[...] = jnp.zeros_like(dst_ref)
            dlb_ref[...] = jnp.zeros_like(dlb_ref)
            dng_ref[...] = jnp.zeros_like(dng_ref)

        dst = [dst_ref[h] for h in range(HGRN_HEADS)]
        dlb = [jnp.zeros((1, hd), F32)] * HGRN_HEADS
        dng = [jnp.zeros((1, hd), F32)] * HGRN_HEADS
        for s in reversed(range(nsub)):
            rows = slice(s * HGRN_CHUNK, (s + 1) * HGRN_CHUNK)
            for h in range(HGRN_HEADS):
                cols = _head_cols(h)
                _, vjp = jax.vjp(chunk, q_ref[rows, cols], f_ref[rows, cols], i_ref[rows, cols], g_ref[rows, cols],
                                 lb_ref[:, cols], ng_ref[:, cols], sts_ref[h, s])
                dq, df, di, dg, dlb_s, dng_s, dst[h] = vjp((dy_ref[rows, cols], dst[h]))
                dq_ref[rows, cols] = dq
                df_ref[rows, cols] = df
                di_ref[rows, cols] = di
                dg_ref[rows, cols] = dg
                dlb[h] = dlb[h] + dlb_s
                dng[h] = dng[h] + dng_s
        for h in range(HGRN_HEADS):
            dst_ref[h] = dst[h]
            dlb_ref[:, _head_cols(h)] += dlb[h]
            dng_ref[:, _head_cols(h)] += dng[h]

    tok_out = jax.ShapeDtypeStruct((T, HGRN_DIM), F32)
    vec_out = jax.ShapeDtypeStruct((1, HGRN_DIM), F32)
    return pl.pallas_call(
        body, name=name, grid=(nb,),
        in_specs=proj_specs + [vec_spec, vec_spec, st_spec, tok_spec],
        out_specs=[tok_spec] * 4 + [vec_spec, vec_spec],
        out_shape=[tok_out] * 4 + [vec_out, vec_out],
        scratch_shapes=[pltpu.VMEM((HGRN_HEADS, hd, hd), F32)],
        compiler_params=_cparams(("arbitrary",)),
    )(proj, proj, proj, proj, lb, ng, states, dmix)


S5_LANES = 512
S5_ROWS = 512


def _cmul(ar, ai, br, bi):
    return ar * br - ai * bi, ar * bi + ai * br


def _power_table(ar, ai, exps):
    a2 = _cmul(ar, ai, ar, ai)
    a4 = _cmul(*a2, *a2)
    e = exps - 1
    pr = jnp.broadcast_to(ar, exps.shape)
    pi = jnp.broadcast_to(ai, exps.shape)
    for bit, (fr, fi) in enumerate(((ar, ai), a2, a4)):
        nr, ni = _cmul(pr, pi, fr, fi)
        on = ((e >> bit) & 1) == 1
        pr, pi = jnp.where(on, nr, pr), jnp.where(on, ni, pi)
    return pr, pi, a2, a4


def _s5_scan_fwd(x, x_cols, b_re, b_im, a_re, a_im, *, name):
    T = x.shape[0]
    w, tr = S5_LANES, min(S5_ROWS, T)
    ncol, nt = S5_WIDTH // w, T // tr
    assert x_cols[0] % x_cols[1] == 0

    def body(x_ref, br_ref, bi_ref, ar_ref, ai_ref, sr_ref, si_ref, carry_ref):
        @pl.when(pl.program_id(1) == 0)
        def _():
            carry_ref[...] = jnp.zeros_like(carry_ref)

        u = x_ref[...].astype(_MXU_DTYPE)
        sr_ref[...] = _mxu(u, br_ref[...], 1, 0)
        si_ref[...] = _mxu(u, bi_ref[...], 1, 0)
        ar, ai = ar_ref[...], ai_ref[...]
        rowi = lax.broadcasted_iota(jnp.int32, (SUBLANE, w), 0)
        pr, pi, a2, a4 = _power_table(ar, ai, rowi + 1)
        steps = [(s, jnp.where(rowi >= s, fr, 0.0), jnp.where(rowi >= s, fi, 0.0)) for s, (fr, fi) in ((1, (ar, ai)), (2, a2), (4, a4))]

        def tile(i, carry):
            cr, ci = carry
            rows = pl.ds(pl.multiple_of(i * SUBLANE, SUBLANE), SUBLANE)
            xr, xi = sr_ref[rows, :], si_ref[rows, :]
            for s, fr, fi in steps:
                zr, zi = pltpu.roll(xr, s, 0), pltpu.roll(xi, s, 0)
                xr, xi = xr + fr * zr - fi * zi, xi + fr * zi + fi * zr
            xr, xi = xr + pr * cr - pi * ci, xi + pr * ci + pi * cr
            sr_ref[rows, :] = xr
            si_ref[rows, :] = xi
            return xr[SUBLANE - 1:SUBLANE, :], xi[SUBLANE - 1:SUBLANE, :]

        cr, ci = lax.fori_loop(0, tr // SUBLANE, tile, (carry_ref[0:1, :], carry_ref[1:2, :]))
        carry_ref[0:1, :] = cr
        carry_ref[1:2, :] = ci

    out = jax.ShapeDtypeStruct((T, S5_WIDTH), F32)
    return pl.pallas_call(
        body, name=name, grid=(ncol, nt),
        in_specs=[pl.BlockSpec((tr, x_cols[1]), lambda j, t: (t, x_cols[0] // x_cols[1])),
                  pl.BlockSpec((S5_DIM, w), lambda j, t: (0, j)), pl.BlockSpec((S5_DIM, w), lambda j, t: (0, j)),
                  pl.BlockSpec((1, w), lambda j, t: (0, j)), pl.BlockSpec((1, w), lambda j, t: (0, j))],
        out_specs=[pl.BlockSpec((tr, w), lambda j, t: (t, j))] * 2,
        out_shape=[out, out],
        scratch_shapes=[pltpu.VMEM((2, w), F32)],
        compiler_params=_cparams(("parallel", "arbitrary")),
    )(x, b_re, b_im, a_re, a_im)


def _s5_scan_bwd(dy, c_re, c_im, s_re, s_im, a_re, a_im, *, name):
    T = dy.shape[0]
    w, tr = S5_LANES, min(S5_ROWS, T)
    ncol, nt = S5_WIDTH // w, T // tr
    ntile = tr // SUBLANE

    def body(dy_ref, cr_ref, ci_ref, sr_ref, si_ref, ar_ref, ai_ref, lr_ref, li_ref, dar_ref, dai_ref, carry_ref):
        @pl.when(pl.program_id(1) == 0)
        def _():
            carry_ref[...] = jnp.zeros_like(carry_ref)
            dar_ref[...] = jnp.zeros_like(dar_ref)
            dai_ref[...] = jnp.zeros_like(dai_ref)

        dyb = dy_ref[...].astype(_MXU_DTYPE)
        lr_ref[...] = _mxu(dyb, cr_ref[...], 1, 1)
        li_ref[...] = _mxu(dyb, ci_ref[...], 1, 1)
        ar, ai = ar_ref[...], -ai_ref[...]
        rowi = lax.broadcasted_iota(jnp.int32, (SUBLANE, w), 0)
        pr, pi, a2, a4 = _power_table(ar, ai, SUBLANE - rowi)
        last = rowi == SUBLANE - 1
        steps = [(s, jnp.where(rowi < SUBLANE - s, fr, 0.0), jnp.where(rowi < SUBLANE - s, fi, 0.0))
                 for s, (fr, fi) in ((1, (ar, ai)), (2, a2), (4, a4))]

        def tile(i, carry):
            cr, ci, dar, dai = carry
            rows = pl.ds(pl.multiple_of((ntile - 1 - i) * SUBLANE, SUBLANE), SUBLANE)
            xr, xi = lr_ref[rows, :], li_ref[rows, :]
            for s, fr, fi in steps:
                zr, zi = pltpu.roll(xr, SUBLANE - s, 0), pltpu.roll(xi, SUBLANE - s, 0)
                xr, xi = xr + fr * zr - fi * zi, xi + fr * zi + fi * zr
            xr, xi = xr + pr * cr - pi * ci, xi + pr * ci + pi * cr
            lr_ref[rows, :] = xr
            li_ref[rows, :] = xi
            nr = jnp.where(last, cr, pltpu.roll(xr, SUBLANE - 1, 0))
            ni = jnp.where(last, ci, pltpu.roll(xi, SUBLANE - 1, 0))
            sr, si = sr_ref[rows, :], si_ref[rows, :]
            return xr[0:1, :], xi[0:1, :], dar + nr * sr + ni * si, dai + ni * sr - nr * si

        cr, ci, dar, dai = lax.fori_loop(
            0, ntile, tile, (carry_ref[0:1, :], carry_ref[1:2, :], jnp.zeros((SUBLANE, w), F32), jnp.zeros((SUBLANE, w), F32)))
        carry_ref[0:1, :] = cr
        carry_ref[1:2, :] = ci
        dar_ref[...] += dar
        dai_ref[...] += dai

    tok = pl.BlockSpec((tr, w), lambda j, t: (nt - 1 - t, j))
    vec = pl.BlockSpec((1, w), lambda j, t: (0, j))
    acc = pl.BlockSpec((SUBLANE, w), lambda j, t: (0, j))
    out = jax.ShapeDtypeStruct((T, S5_WIDTH), F32)
    accs = jax.ShapeDtypeStruct((SUBLANE, S5_WIDTH), F32)
    return pl.pallas_call(
        body, name=name, grid=(ncol, nt),
        in_specs=[pl.BlockSpec((tr, S5_DIM), lambda j, t: (nt - 1 - t, 0)),
                  pl.BlockSpec((w, S5_DIM), lambda j, t: (j, 0)), pl.BlockSpec((w, S5_DIM), lambda j, t: (j, 0)),
                  tok, tok, vec, vec],
        out_specs=[tok, tok, acc, acc],
        out_shape=[out, out, accs, accs],
        scratch_shapes=[pltpu.VMEM((2, w), F32)],
        compiler_params=_cparams(("parallel", "arbitrary")),
    )(dy, c_re, c_im, s_re, s_im, a_re, a_im)


ATTN_BLOCK = 512
_NEG = -1e30


def _qk_cat(nope, rope):
    return jnp.concatenate([nope.astype(_MXU_DTYPE), rope.astype(_MXU_DTYPE)], axis=1)


_QK_SCALE = MLA_QK ** -0.5
_LOG2E = math.log2(math.e)


def _attn_scores(q, k, diagonal):
    s = _mxu(q, k, 1, 1) * (_QK_SCALE * _LOG2E)
    if diagonal:
        s = jnp.where(_tri(s.shape[0]), s, _NEG)
    return s


def _attn_fwd(q_all, q_rope, kv, k_rope, *, name):
    T = q_all.shape[0]
    tq = min(ATTN_BLOCK, T)
    nq = T // tq

    def body(qn_ref, qr_ref, kn_ref, v_ref, kr_ref, o_ref, lse_ref):
        i = pl.program_id(1)
        q = _qk_cat(qn_ref[...], qr_ref[0])

        def step(j, carry, diagonal):
            m, l, acc = carry
            ks = pl.ds(pl.multiple_of(j * tq, tq), tq)
            s = _attn_scores(q, _qk_cat(kn_ref[ks, :], kr_ref[ks, :]), diagonal)
            m_new = jnp.maximum(m, jnp.max(s, axis=-1, keepdims=True))
            p = jnp.exp2(s - m_new)
            alpha = jnp.exp2(m - m_new)
            return m_new, alpha * l + jnp.sum(p, axis=-1, keepdims=True), alpha * acc + _mxu(p, v_ref[ks, :], 1, 0)

        init = (jnp.full((tq, 1), _NEG, F32), jnp.zeros((tq, 1), F32), jnp.zeros((tq, MLA_V), F32))
        below = lax.fori_loop(0, i, functools.partial(step, diagonal=False), init)
        m, l, acc = step(i, below, diagonal=True)
        o_ref[...] = acc / l
        lse_ref[0] = m + jnp.log2(l)

    return pl.pallas_call(
        body, name=name, grid=(MLA_HEADS, nq),
        in_specs=[pl.BlockSpec((tq, MLA_NOPE), lambda h, i: (i, h)), pl.BlockSpec((1, tq, MLA_ROPE), lambda h, i: (h, i, 0)),
                  pl.BlockSpec((T, MLA_NOPE), lambda h, i: (0, 2 * h)), pl.BlockSpec((T, MLA_V), lambda h, i: (0, 2 * h + 1)),
                  pl.BlockSpec((T, MLA_ROPE), lambda h, i: (0, 0))],
        out_specs=[pl.BlockSpec((tq, MLA_V), lambda h, i: (i, h)), pl.BlockSpec((1, tq, 1), lambda h, i: (h, i, 0))],
        out_shape=[jax.ShapeDtypeStruct((T, MLA_HEADS * MLA_V), F32), jax.ShapeDtypeStruct((MLA_HEADS, T, 1), F32)],
        compiler_params=_cparams(("arbitrary", "arbitrary")),
    )(q_all, q_rope, kv, kv, k_rope)


def _attn_bwd(q_all, q_rope, kv, k_rope, o, lse, do, *, name):
    T = q_all.shape[0]
    tk = min(ATTN_BLOCK, T)
    nk = T // tk

    def body(qn_ref, qr_ref, kv_ref, kr_ref, o_ref, lse_ref, do_ref, dqn_ref, dqr_ref, dkv_ref, dkr_ref, delta_ref):
        h, j = pl.program_id(0), pl.program_id(1)

        @pl.when(j == 0)
        def _():
            dqn_ref[...] = jnp.zeros_like(dqn_ref)
            dqr_ref[...] = jnp.zeros_like(dqr_ref)
            delta_ref[...] = jnp.sum(do_ref[...] * o_ref[...], axis=-1, keepdims=True)

        @pl.when((j == 0) & (h == 0))
        def _():
            dkr_ref[...] = jnp.zeros_like(dkr_ref)

        krows = pl.ds(pl.multiple_of(j * tk, tk), tk)
        k = _qk_cat(kv_ref[:, :MLA_NOPE], kr_ref[krows, :])
        v = kv_ref[:, MLA_NOPE:].astype(_MXU_DTYPE)

        def step(i, carry, diagonal):
            dk, dv = carry
            qs = pl.ds(pl.multiple_of(i * tk, tk), tk)
            q, dob = _qk_cat(qn_ref[qs, :], qr_ref[0, qs, :]), do_ref[qs, :].astype(_MXU_DTYPE)
            p = jnp.exp2(_attn_scores(q, k, diagonal) - lse_ref[0, qs, :])
            ds = p * (_mxu(dob, v, 1, 1) - delta_ref[qs, :]) * _QK_SCALE
            dq = _mxu(ds, k, 1, 0)
            dqn_ref[qs, :] += dq[:, :MLA_NOPE]
            dqr_ref[0, qs, :] += dq[:, MLA_NOPE:]
            return dk + _mxu(ds, q, 0, 0), dv + _mxu(p, dob, 0, 0)

        on_diagonal = step(j, (jnp.zeros((tk, MLA_QK), F32), jnp.zeros((tk, MLA_V), F32)), diagonal=True)
        dk, dv = lax.fori_loop(j + 1, nk, functools.partial(step, diagonal=False), on_diagonal)
        dkv_ref[:, :MLA_NOPE] = dk[:, :MLA_NOPE].astype(dkv_ref.dtype)
        dkv_ref[:, MLA_NOPE:] = dv.astype(dkv_ref.dtype)
        dkr_ref[krows, :] += dk[:, MLA_NOPE:]

    head_cols = pl.BlockSpec((T, MLA_NOPE), lambda h, j: (0, h))
    head_rope = pl.BlockSpec((1, T, MLA_ROPE), lambda h, j: (h, 0, 0))
    kv_spec = pl.BlockSpec((tk, MLA_NOPE + MLA_V), lambda h, j: (j, h))
    kr_spec = pl.BlockSpec((T, MLA_ROPE), lambda h, j: (0, 0))
    return pl.pallas_call(
        body, name=name, grid=(MLA_HEADS, nk),
        in_specs=[head_cols, head_rope, kv_spec, kr_spec, head_cols, pl.BlockSpec((1, T, 1), lambda h, j: (h, 0, 0)), head_cols],
        out_specs=[head_cols, head_rope, kv_spec, kr_spec],
        out_shape=[jax.ShapeDtypeStruct((T, MLA_HEADS * MLA_NOPE), F32), jax.ShapeDtypeStruct((MLA_HEADS, T, MLA_ROPE), F32),
                   jax.ShapeDtypeStruct((T, MLA_HEADS * (MLA_NOPE + MLA_V)), BF16), jax.ShapeDtypeStruct((T, MLA_ROPE), F32)],
        scratch_shapes=[pltpu.VMEM((T, 1), F32)],
        compiler_params=_cparams(("arbitrary", "arbitrary")),
    )(q_all, q_rope, kv, k_rope, o, lse, do)


def _s5_discretize(a_re, a_im, log_dt, bt_re, bt_im, lb_logits):
    dt = jnp.exp(log_dt)
    mag = jnp.exp(a_re * dt)
    abr, abi = mag * jnp.cos(a_im * dt), mag * jnp.sin(a_im * dt)
    den = a_re * a_re + a_im * a_im
    xr, xi = abr - 1.0, abi
    cr = ((xr * a_re + xi * a_im) / den)[:, None, :]
    ci = ((xi * a_re - xr * a_im) / den)[:, None, :]
    e = jnp.exp(lb_logits - jnp.max(lb_logits, axis=0, keepdims=True))
    lb = e[0:1, :] / jnp.sum(e, axis=0, keepdims=True)
    return abr, abi, cr * bt_re - ci * bt_im, cr * bt_im + ci * bt_re, lb


def _whole(shape):
    return pl.BlockSpec(shape, lambda: (0,) * len(shape))


def _s5_params_fwd(a_re, a_im, log_dt, bt_re, bt_im, lb_logits):
    ins = (a_re, a_im, log_dt, bt_re, bt_im, lb_logits)
    outs = [jax.ShapeDtypeStruct(s, F32) for s in (a_re.shape, a_re.shape, bt_re.shape, bt_re.shape, (1, lb_logits.shape[1]))]

    def body(*refs):
        res = _s5_discretize(*[r[...] for r in refs[:6]])
        for o_ref, o in zip(refs[6:], res):
            o_ref[...] = o

    return pl.pallas_call(body, name="s5_params_fwd", in_specs=[_whole(a.shape) for a in ins],
                          out_specs=[_whole(o.shape) for o in outs], out_shape=outs, compiler_params=_cparams())(*ins)


def _s5_params_bwd(a_re, a_im, log_dt, bt_re, bt_im, lb_logits, d_abr, d_abi, d_bbr, d_bbi, d_lb):
    ins = (a_re, a_im, log_dt, bt_re, bt_im, lb_logits, d_abr, d_abi, d_bbr, d_bbi, d_lb)
    outs = [jax.ShapeDtypeStruct(a.shape, F32) for a in ins[:6]]

    def body(*refs):
        _, vjp = jax.vjp(_s5_discretize, *[r[...] for r in refs[:6]])
        for o_ref, o in zip(refs[11:], vjp(tuple(r[...] for r in refs[6:11]))):
            o_ref[...] = o

    return pl.pallas_call(body, name="s5_params_bwd", in_specs=[_whole(a.shape) for a in ins],
                          out_specs=[_whole(o.shape) for o in outs], out_shape=outs, compiler_params=_cparams())(*ins)


ADAMW_WHOLE_BYTES = 1024 * 1024


def _adamw(w, g, m, v, *, name, layer=0, n_layers=1, into=None):
    R, C = w.shape
    whole = R % SUBLANE != 0 or R * C * w.dtype.itemsize <= ADAMW_WHOLE_BYTES
    tr = R if whole else _pick(R, (256, 128, 64, 32, 16, 8))
    slabs = g.shape[0]
    n_prev = 0 if into is None else len(into)

    def body(w_ref, g_ref, m_ref, v_ref, *rest):
        g_out, d_ref, mo_ref, vo_ref = rest[n_prev:]
        gv = g_ref[0].astype(F32)
        for s in range(1, slabs):
            gv = gv + g_ref[s].astype(F32)
        m2 = ADAM_B1 * m_ref[...] + (1.0 - ADAM_B1) * gv
        v2 = ADAM_B2 * v_ref[...] + (1.0 - ADAM_B2) * (gv * gv)
        m_hat = m2 / (1.0 - ADAM_B1 ** ADAM_STEP)
        v_hat = v2 / (1.0 - ADAM_B2 ** ADAM_STEP)
        g_out[0] = gv
        d_ref[0] = -ADAM_LR * (m_hat / (jnp.sqrt(v_hat) + ADAM_EPS) + ADAM_WD * w_ref[...])
        mo_ref[0] = m2
        vo_ref[0] = v2

    spec = pl.BlockSpec((tr, C), lambda i: (i, 0))
    out_spec = pl.BlockSpec((1, tr, C), lambda i: (layer, i, 0))
    out = jax.ShapeDtypeStruct((n_layers, R, C), F32)
    return pl.pallas_call(
        body, name=name, grid=(R // tr,),
        in_specs=[spec, pl.BlockSpec((slabs, tr, C), lambda i: (0, i, 0)), spec, spec] + [pl.BlockSpec(memory_space=pl.ANY)] * n_prev,
        out_specs=[out_spec] * 4, out_shape=[out] * 4, input_output_aliases={4 + k: k for k in range(n_prev)},
        compiler_params=_cparams(("parallel",)))(w, g, m, v, *(into or ()))


N_CHIPS = 4
N_CORES = 2


_FLIPS = tuple((dx, dy, dc) for dx in (0, 1) for dy in (0, 1) for dc in (0, 1) if (dx, dy, dc) != (0, 0, 0))


_HBM = pl.BlockSpec(memory_space=pltpu.HBM)
_SEM = pl.BlockSpec(memory_space=pltpu.SEMAPHORE)
_SPLIT_COPY = pltpu.CompilerParams(has_side_effects=pltpu.SideEffectType.DATAFLOW_SIDE_EFFECTING)


def _exchange_copies(src_refs, land_refs, send_sems, recv_sems, scatter, arriving):
    x, y, c = lax.axis_index("x"), lax.axis_index("y"), lax.axis_index("c")
    me_chip = 2 * x + y
    copies = []
    for a, (s_ref, l_ref) in enumerate(zip(src_refs, land_refs)):
        for j, (dx, dy, dc) in enumerate(_FLIPS):
            px, py, pc = (1 - x if dx else x), (1 - y if dy else y), (1 - c if dc else c)
            k = a * len(_FLIPS) + j
            p_chip = 2 * px + py
            copies.append(pltpu.make_async_remote_copy(
                src_ref=s_ref.at[p_chip, pc] if scatter else s_ref, dst_ref=l_ref.at[p_chip, pc] if arriving else l_ref.at[me_chip, c],
                send_sem=send_sems.at[k], recv_sem=recv_sems.at[k], device_id=(px, py, pc), device_id_type=MESH))
    return copies


def _exchange_start(srcs, *, scatter, name, after=()):
    n_arr = len(srcs)
    n_sem = n_arr * len(_FLIPS)
    n_in = 2 * n_arr + len(after)
    lands = [lax.empty(s.shape if scatter else (N_CHIPS, N_CORES) + s.shape, s.dtype) for s in srcs]

    def body(*refs):
        src_refs, land_refs = refs[:n_arr], refs[n_arr:2 * n_arr]
        for cp in _exchange_copies(src_refs, land_refs, refs[n_in], refs[n_in + 1], scatter, arriving=False):
            cp.start()
        refs[-1][...] = jnp.zeros_like(refs[-1])

    thru = [pltpu.HBM(a.shape, a.dtype) for a in srcs + lands]
    outs = pl.pallas_call(
        body, name=name,
        out_shape=(pltpu.SemaphoreType.DMA((n_sem,)), pltpu.SemaphoreType.DMA((n_sem,)), *thru,
                   jax.ShapeDtypeStruct((SUBLANE, LANE), F32)),
        in_specs=[_HBM] * (2 * n_arr) + [pl.BlockSpec(memory_space=pl.ANY)] * len(after),
        out_specs=(_SEM, _SEM, *[_HBM] * (2 * n_arr), pl.BlockSpec(memory_space=pltpu.VMEM)),
        input_output_aliases={i: 2 + i for i in range(2 * n_arr)}, compiler_params=_SPLIT_COPY,
    )(*[pltpu.with_memory_space_constraint(a, pltpu.HBM) for a in srcs + lands], *after)
    return outs[0], outs[1], list(outs[2:2 + n_arr]), list(outs[2 + n_arr:2 + 2 * n_arr]), outs[-1]


def _exchange_wait(started, after, *, scatter, name):
    send_sems, recv_sems, srcs, lands, _ = started
    n_arr = len(srcs)

    def body(*refs):
        src_refs, land_refs = refs[:n_arr], refs[n_arr:2 * n_arr]
        for cp in _exchange_copies(src_refs, land_refs, refs[2 * n_arr], refs[2 * n_arr + 1], scatter, arriving=True):
            cp.wait_send()
            cp.wait_recv()

    outs = pl.pallas_call(
        body, name=name, out_shape=[pltpu.HBM(a.shape, a.dtype) for a in srcs + lands],
        in_specs=[_HBM] * (2 * n_arr) + [_SEM, _SEM, pl.BlockSpec(memory_space=pl.ANY)], out_specs=[_HBM] * (2 * n_arr),
        input_output_aliases={i: i for i in range(2 * n_arr)}, compiler_params=_SPLIT_COPY,
    )(*srcs, *lands, send_sems, recv_sems, after)
    return list(outs[:n_arr]), list(outs[n_arr:])


def _with_own(land, own):
    me_chip = 2 * lax.axis_index("x") + lax.axis_index("y")
    return lax.dynamic_update_slice(land, own[None, None], (me_chip, lax.axis_index("c")) + (0,) * own.ndim)


def _rms_fwd_fn(h, g):
    return (_rms(h, g),)


def _rms_bwd_fn(h, dhn, dres, g):
    _, vjp = jax.vjp(_rms, h, g)
    dh, dg = vjp(dhn)
    return dh + dres, dg


def _loss_fn(h, tgt, g):
    y, vjp = jax.vjp(_rms, h, g)
    diff = y - tgt
    dh, dg = vjp(diff * (1.0 / D_MODEL))
    return dh, dg, (0.5 / D_MODEL) * jnp.sum(diff * diff, axis=0, keepdims=True)


def _s5_act(ys, u, d):
    return _gelu(ys + d * u)


def _s5_gate(z, gl, b):
    return z * _sigmoid(gl + b)


def _s5_act_fn(ys, u, d):
    return (_s5_act(ys, u, d),)


def _s5_mix_fn(ya, z, gl, b):
    return (jnp.concatenate([ya, _s5_gate(z, gl, b)], axis=1),)


def _s5_gate_bwd_fn(z, gl, dyb, b):
    _, vjp = jax.vjp(_s5_gate, z, gl, b)
    return vjp(dyb)


def _s5_act_bwd_fn(ys, u, dz1, dz2, d):
    _, vjp = jax.vjp(_s5_act, ys, u, d)
    return vjp(dz1 + dz2)


def _dproj_fn(dq, df, di, dg, du1, du2):
    return (jnp.concatenate([dq, df, di, dg, du1 + du2], axis=1),)


def _rope_pair(r1, r2, pos, freqs):
    ang = pos.astype(F32) * freqs
    c, s = jnp.cos(ang), jnp.sin(ang)
    return r1 * c - r2 * s, r1 * s + r2 * c


_ODD_SPLITS = (0, MLA_Q_RANK, MLA_Q_RANK + MLA_KV_RANK, MLA_Q_RANK + MLA_KV_RANK + LANE, ODD_IN_PAD)


def _mla_prep(cq, ckv, k1, k2, qg, kvg, pos, freqs):
    ko1, ko2 = _rope_pair(k1, k2, pos, freqs)
    return _rms(cq, qg), _rms(ckv, kvg), ko1, ko2


def _mla_prep_fn(proj, pos, qg, kvg, freqs):
    parts = [proj[:, a:b] for a, b in zip(_ODD_SPLITS[:-1], _ODD_SPLITS[1:])]
    return _mla_prep(*parts, qg, kvg, pos, freqs)


def _mla_prep_bwd_fn(proj, pos, dqn, dkvn, dko1, dko2, qg, kvg, freqs):
    parts = [proj[:, a:b] for a, b in zip(_ODD_SPLITS[:-1], _ODD_SPLITS[1:])]
    _, vjp = jax.vjp(lambda *a: _mla_prep(*a, pos, freqs), *parts, qg, kvg)
    dcq, dckv, dk1, dk2, dqg, dkvg = vjp((dqn, dkvn, dko1, dko2))
    return jnp.concatenate([dcq, dckv, dk1, dk2], axis=1), dqg, dkvg


ROPE_ROWS = 512


def _rope_heads(x, pos, freqs):
    half = MLA_ROPE // 2
    reps = x.shape[1] // LANE
    ang = pos.astype(F32) * freqs[:, :LANE]
    cos = jnp.concatenate([jnp.cos(ang)] * reps, axis=1)
    sin = jnp.concatenate([jnp.sin(ang)] * reps, axis=1)
    first = (lax.broadcasted_iota(jnp.int32, x.shape, 1) % MLA_ROPE) < half
    other = jnp.where(first, pltpu.roll(x, x.shape[1] - half, 1), pltpu.roll(x, half, 1))
    return x * cos + other * jnp.where(first, -sin, sin)


def _rope_q_fwd(q_all, pos, freqs, *, name):
    T = q_all.shape[0]
    tm = min(ROPE_ROWS, T)
    nope_w, rope_w = MLA_HEADS * MLA_NOPE, MLA_HEADS * MLA_ROPE

    def body(r_ref, pos_ref, f_ref, o_ref):
        out = _rope_heads(r_ref[...], pos_ref[...], f_ref[...])
        for h in range(MLA_HEADS):
            o_ref[h] = out[:, h * MLA_ROPE:(h + 1) * MLA_ROPE]

    return pl.pallas_call(
        body, name=name, grid=(T // tm,),
        in_specs=[pl.BlockSpec((tm, rope_w), lambda i: (i, nope_w // rope_w)), pl.BlockSpec((tm, 1), lambda i: (i, 0)),
                  pl.BlockSpec((1, rope_w), lambda i: (0, 0))],
        out_specs=pl.BlockSpec((MLA_HEADS, tm, MLA_ROPE), lambda i: (0, i, 0)),
        out_shape=jax.ShapeDtypeStruct((MLA_HEADS, T, MLA_ROPE), F32),
        compiler_params=_cparams(("parallel",)),
    )(q_all, pos, freqs)


def _rope_q_bwd(dq_nope, dq_rope, pos, freqs, *, name):
    T = dq_nope.shape[0]
    tm = min(ROPE_ROWS, T)
    nope_w, rope_w = MLA_HEADS * MLA_NOPE, MLA_HEADS * MLA_ROPE

    def body(dn_ref, dr_ref, pos_ref, f_ref, o_ref):
        d_out = jnp.concatenate([dr_ref[h] for h in range(MLA_HEADS)], axis=1)
        o_ref[:, :nope_w] = dn_ref[...].astype(o_ref.dtype)
        o_ref[:, nope_w:] = _rope_heads(d_out, pos_ref[...], -f_ref[...]).astype(o_ref.dtype)

    return pl.pallas_call(
        body, name=name, grid=(T // tm,),
        in_specs=[pl.BlockSpec((tm, nope_w), lambda i: (i, 0)), pl.BlockSpec((MLA_HEADS, tm, MLA_ROPE), lambda i: (0, i, 0)),
                  pl.BlockSpec((tm, 1), lambda i: (i, 0)), pl.BlockSpec((1, rope_w), lambda i: (0, 0))],
        out_specs=pl.BlockSpec((tm, nope_w + rope_w), lambda i: (i, 0)),
        out_shape=jax.ShapeDtypeStruct((T, nope_w + rope_w), BF16),
        compiler_params=_cparams(("parallel",)),
    )(dq_nope, dq_rope, pos, freqs)


W_NAMES = ("norm_mix_g", "norm_ffn_g", "final_norm_g", "even_w_in", "hgrn_lb_logits", "hgrn_norm_g", "s5_a_re", "s5_a_im",
           "s5_log_dt", "s5_b_re", "s5_b_im", "s5_c_re", "s5_c_im", "s5_d", "s5_w_glu", "s5_b_glu", "even_w_out", "odd_w_in",
           "mla_q_norm_g", "mla_w_uq", "mla_kv_norm_g", "mla_w_ukv", "odd_w_out", "ffn_w_in", "ffn_conv_w", "ffn_conv_b",
           "ffn_w_out")
BIG_UNITS = (("even_w_in", 0, True), ("s5_w_glu", 0, False), ("even_w_out", 0, False), ("odd_w_in", 0, False),
             ("mla_w_uq", 0, True), ("mla_w_ukv", 0, True), ("odd_w_out", 0, False),
             ("ffn_w_in", 0, True), ("ffn_w_in", 1, True), ("ffn_w_out", 0, False), ("ffn_w_out", 1, False))
BIG_NAMES = tuple(dict.fromkeys(u[0] for u in BIG_UNITS))
SMALL_SHARDED = (("mla_q_norm_g", 1), ("mla_kv_norm_g", 1), ("ffn_conv_w", 2))
SMALL_SHARDED_NAMES = tuple(s[0] for s in SMALL_SHARDED)
REPLICATED = tuple(n for n in W_NAMES if n not in BIG_NAMES + SMALL_SHARDED_NAMES)
REPLICATED_LATE = ("norm_mix_g",)
REPLICATED_EARLY = tuple(n for n in REPLICATED if n not in REPLICATED_LATE)


def _pack(flats, cols, row_mult):
    flat = jnp.concatenate(flats, axis=-1)
    pad = (-flat.shape[-1]) % (cols * row_mult)
    flat = jnp.pad(flat, [(0, 0)] * (flat.ndim - 1) + [(0, pad)])
    return flat.reshape(flat.shape[:-1] + (-1, cols))


def _unpack(flat, shapes):
    out, off = [], 0
    for shp in shapes:
        n = int(np.prod(shp))
        out.append(flat[..., off:off + n].reshape(flat.shape[:-1] + tuple(shp)))
        off += n
    return out


UNIT_TRANSPOSED = {(n, l): t for n, l, t in BIG_UNITS}


def _unit_block(arrs, unit):
    blk = arrs[unit[0]][unit[1]]
    return blk.T if UNIT_TRANSPOSED[unit] else blk
STAGES = ((("even_w_in", 0),),
          (("s5_w_glu", 0), ("even_w_out", 0)),
          (("ffn_w_in", 0), ("ffn_w_out", 0)),
          (("odd_w_in", 0), ("mla_w_uq", 0), ("mla_w_ukv", 0), ("odd_w_out", 0)),
          (("ffn_w_in", 1), ("ffn_w_out", 1)))


def _gather_start(w, stage, with_small, after):
    srcs = [_unit_block(w, unit).astype(BF16) for unit in STAGES[stage]]
    if with_small:
        srcs.append(_pack([w[n].reshape(-1) for n in SMALL_SHARDED_NAMES], LANE, SUBLANE))
    return _exchange_start(srcs, scatter=False, name=f"gather_start_{stage}", after=after)


def _gather_finish(started, after, w, stage, with_small):
    srcs, lands = _exchange_wait(started, after, scatter=False, name=f"gather_wait_{stage}")
    lands = [_with_own(land, src) for land, src in zip(lands, srcs)]
    big = {unit: g.reshape(N_DEV * g.shape[2], g.shape[3]) for unit, g in zip(STAGES[stage], lands)}
    if not with_small:
        return big
    parts = _unpack(lands[-1].reshape(N_DEV, -1), [w[n].shape for n in SMALL_SHARDED_NAMES])
    small = {}
    for (n, ax), p in zip(SMALL_SHARDED, parts):
        shp = list(w[n].shape)
        shp[ax] *= N_DEV
        small[n] = jnp.moveaxis(p, 0, ax).reshape(shp)
    return big, small


def _scatter_start(g_big, stage, extra=()):
    srcs = []
    for unit in STAGES[stage]:
        g = g_big[unit].astype(BF16)
        srcs.append(g.reshape(N_CHIPS, N_CORES, g.shape[0] // N_DEV, g.shape[1]))
    return _exchange_start(srcs + list(extra), scatter=True, name=f"scatter_start_{stage}")


def _scatter_finish(started, after, stage):
    srcs, lands = _exchange_wait(started, after, scatter=True, name=f"scatter_wait_{stage}")
    me_chip, c = 2 * lax.axis_index("x") + lax.axis_index("y"), lax.axis_index("c")
    outs = []
    for land, src in zip(lands, srcs):
        own = lax.dynamic_slice(src, (me_chip, c) + (0,) * (src.ndim - 2), (1, 1) + src.shape[2:])[0, 0]
        outs.append(_with_own(land, own).reshape((N_DEV,) + land.shape[2:]))
    return outs


def _small_sharded_pack(g_small, w):
    flats = []
    for n, ax in SMALL_SHARDED:
        shp = list(w[n].shape)
        g = g_small[n].astype(F32).reshape(shp[:ax] + [N_DEV] + shp[ax:])
        flats.append(jnp.moveaxis(g, ax, 0).reshape(N_DEV, -1))
    small = _pack(flats, LANE, SUBLANE)
    return small.reshape((N_CHIPS, N_CORES) + small.shape[1:])


def _replicated_pack(g_repl, names):
    vec = _pack([g_repl[n].reshape(-1).astype(F32) for n in names], LANE, SUBLANE)
    return jnp.broadcast_to(vec, (N_CHIPS, N_CORES) + vec.shape)


def _block_diag(blocks):
    G, a, b = blocks.shape
    return jnp.einsum('gab,gk->gakb', blocks, jnp.eye(G, dtype=blocks.dtype)).reshape(G * a, G * b)


def _diag_blocks(mat, a, b):
    G = mat.shape[0] // a
    return jnp.einsum('gagb->gab', mat.reshape(G, a, G, b))


def _ffn_fwd(h, g, w_in_t, conv_w, conv_b, w_out, tag):
    hn, = _rows(_rms_fwd_fn, [h], [g], [(D_MODEL, BF16)], [], name=f"ffn{tag}_norm")
    au = _mm(hn, w_in_t, tb=True, out_dtype=BF16, name=f"ffn{tag}_in")
    z = _ffn_mid_fwd(au, conv_w, conv_b, name=f"ffn{tag}_mid")
    return _mm(z, w_out, res=h, name=f"ffn{tag}_out"), (hn, au, z)


def _ffn_bwd(h, dh, saved, g, w_in_t, conv_w, conv_b, w_out, tag, deps=()):
    hn, au, z = saved
    dz = _mm(dh, w_out, tb=True, out_dtype=BF16, deps=deps, name=f"ffn{tag}_dz")
    dw_out = _mm(z, dh, ta=True, out_dtype=BF16, name=f"ffn{tag}_dwout")
    da, du, dcw, dcb = _ffn_mid_bwd(au, dz, conv_w, conv_b, name=f"ffn{tag}_dmid")
    dhn = _mm(da, w_in_t, b_rows=(0, D_FF), name=f"ffn{tag}_dhn_a")
    dhn = _mm(du, w_in_t, b_rows=(D_FF, D_FF), res=dhn, name=f"ffn{tag}_dhn_u")
    dw_in = _mm(da, hn, ta=True, out_dtype=BF16, out_rows=(0, 2 * D_FF), name=f"ffn{tag}_dwin_a")
    dw_in = _mm(du, hn, ta=True, out_dtype=BF16, out_rows=(D_FF, 2 * D_FF), into=dw_in, name=f"ffn{tag}_dwin_u")
    dh_in, dg = _rows(_rms_bwd_fn, [h, dhn, dh], [g], [(D_MODEL, F32)], [(1, D_MODEL)], name=f"ffn{tag}_dnorm")
    return dh_in, dict(g=dg, w_in=dw_in, conv_w=dcw, conv_b=dcb, w_out=dw_out)


def kernel(x, positions, norm_mix_g, norm_ffn_g, final_norm_g, even_w_in, hgrn_lb_logits, hgrn_norm_g, s5_a_re, s5_a_im, s5_log_dt, s5_b_re, s5_b_im, s5_c_re, s5_c_im, s5_d, s5_w_glu, s5_b_glu, even_w_out, odd_w_in, mla_q_norm_g, mla_w_uq, mla_kv_norm_g, mla_w_ukv, odd_w_out, ffn_w_in, ffn_conv_w, ffn_conv_b, ffn_w_out, loss_target, m_norm_mix_g, m_norm_ffn_g, m_final_norm_g, m_even_w_in, m_hgrn_lb_logits, m_hgrn_norm_g, m_s5_a_re, m_s5_a_im, m_s5_log_dt, m_s5_b_re, m_s5_b_im, m_s5_c_re, m_s5_c_im, m_s5_d, m_s5_w_glu, m_s5_b_glu, m_even_w_out, m_odd_w_in, m_mla_q_norm_g, m_mla_w_uq, m_mla_kv_norm_g, m_mla_w_ukv, m_odd_w_out, m_ffn_w_in, m_ffn_conv_w, m_ffn_conv_b, m_ffn_w_out, v_norm_mix_g, v_norm_ffn_g, v_final_norm_g, v_even_w_in, v_hgrn_lb_logits, v_hgrn_norm_g, v_s5_a_re, v_s5_a_im, v_s5_log_dt, v_s5_b_re, v_s5_b_im, v_s5_c_re, v_s5_c_im, v_s5_d, v_s5_w_glu, v_s5_b_glu, v_even_w_out, v_odd_w_in, v_mla_q_norm_g, v_mla_w_uq, v_mla_kv_norm_g, v_mla_w_ukv, v_odd_w_out, v_ffn_w_in, v_ffn_conv_w, v_ffn_conv_b, v_ffn_w_out):
    given = dict(locals())
    w = {n: given[n] for n in W_NAMES}
    mom = {n: given["m_" + n] for n in W_NAMES}
    var = {n: given["v_" + n] for n in W_NAMES}
    T = x.shape[1]
    h0 = x[0]
    tgt = loss_target[0]
    pos = positions.reshape(T, 1)

    gathers = []
    for s in range(len(STAGES)):
        gathers.append(_gather_start(w, s, with_small=(s == 1), after=[g[4] for g in gathers[-1:]]))
    half = MLA_ROPE // 2
    kr0 = MLA_Q_RANK + MLA_KV_RANK
    freqs = ROPE_THETA ** (-jnp.arange(0, MLA_ROPE, 2, dtype=F32) / MLA_ROPE)
    freqs_q = jnp.tile(jnp.concatenate([freqs, freqs]), MLA_HEADS)[None, :]
    freqs_k = jnp.concatenate([freqs, jnp.zeros((LANE - half,), F32)])[None, :]

    sp_in = (s5_a_re[0], s5_a_im[0], s5_log_dt[0][:, None], s5_b_re[0].transpose(0, 2, 1), s5_b_im[0].transpose(0, 2, 1),
             hgrn_lb_logits)
    abr, abi, bbt_re, bbt_im, lb0 = _s5_params_fwd(*sp_in)
    a_re, a_im = abr.reshape(1, S5_WIDTH), abi.reshape(1, S5_WIDTH)
    bb_re, bb_im = _block_diag(bbt_re).astype(BF16), _block_diag(bbt_im).astype(BF16)
    c_re = _block_diag(s5_c_re[0].transpose(0, 2, 1)).astype(BF16)
    c_im_neg = _block_diag(-s5_c_im[0].transpose(0, 2, 1)).astype(BF16)
    u_cols = (4 * HGRN_DIM, S5_DIM)

    hn0, = _rows(_rms_fwd_fn, [h0], [norm_mix_g[0:1]], [(D_MODEL, BF16)], [], name="mix0_norm", deps=[gathers[-1][4]])
    full = _gather_finish(gathers[0], hn0, w, 0, False)
    w_ein_t = full["even_w_in", 0]
    proj = _mm(hn0, w_ein_t, tb=True, name="even_in")
    y_a, states = _hgrn_fwd(proj, lb0, hgrn_norm_g, name="hgrn_fwd")
    s_re, s_im = _s5_scan_fwd(proj, u_cols, bb_re, bb_im, a_re, a_im, name="s5_scan_fwd")
    more, full_small = _gather_finish(gathers[1], s_re, w, 1, True)
    w_glu, w_eout = more["s5_w_glu", 0], more["even_w_out", 0]
    qg, kvg, conv_w = full_small["mla_q_norm_g"], full_small["mla_kv_norm_g"], full_small["ffn_conv_w"]
    ys = _mm(s_im, c_im_neg, res=_mm(s_re, c_re, name="s5_y_re"), name="s5_y_im")
    z5, = _rows(_s5_act_fn, [ys, (proj,) + u_cols], [s5_d], [(S5_DIM, F32)], [], name="s5_act")
    gl = _mm(z5, w_glu, name="s5_glu")
    mixin, = _rows(_s5_mix_fn, [y_a, z5, gl], [s5_b_glu], [(D_MODEL, BF16)], [], name="s5_mix")
    h1 = _mm(mixin, w_eout, res=h0, name="even_out")
    full.update(_gather_finish(gathers[2], h1, w, 2, False))
    w_fin, w_fout = [full["ffn_w_in", 0]], [full["ffn_w_out", 0]]
    h2, ffn0_saved = _ffn_fwd(h1, norm_ffn_g[0:1], w_fin[0], conv_w[0], ffn_conv_b[0:1], w_fout[0], 0)

    full.update(_gather_finish(gathers[3], h2, w, 3, False))
    w_oin, w_ukv_t, w_oout = full["odd_w_in", 0], full["mla_w_ukv", 0], full["odd_w_out", 0]
    zpad = jnp.zeros((D_MODEL, LANE - half), BF16)
    w_oin_pad = jnp.concatenate([w_oin[:, :kr0], w_oin[:, kr0:kr0 + half], zpad, w_oin[:, kr0 + half:], zpad], axis=1)
    w_uq3 = full["mla_w_uq", 0].reshape(MLA_HEADS, MLA_QK, MLA_Q_RANK)
    w_uq_perm_t = jnp.concatenate([w_uq3[:, :MLA_NOPE].reshape(-1, MLA_Q_RANK),
                                   w_uq3[:, MLA_NOPE:].reshape(-1, MLA_Q_RANK)], axis=0)
    hn1, = _rows(_rms_fwd_fn, [h2], [norm_mix_g[1:2]], [(D_MODEL, BF16)], [], name="mix1_norm")
    proj_o = _mm(hn1, w_oin_pad, name="odd_in")
    qn, kvn, ko1, ko2 = _rows(_mla_prep_fn, [proj_o, pos], [qg, kvg, freqs_k],
                              [(MLA_Q_RANK, BF16), (MLA_KV_RANK, BF16), (LANE, F32), (LANE, F32)], [], name="mla_prep")
    q_all = _mm(qn, w_uq_perm_t, tb=True, name="mla_uq")
    kv = _mm(kvn, w_ukv_t, tb=True, out_dtype=BF16, name="mla_ukv")
    nope_w = MLA_HEADS * MLA_NOPE
    q_rope = _rope_q_fwd(q_all, pos, freqs_q, name="mla_rope_q")
    k_rope = jnp.concatenate([ko1[:, :half], ko2[:, :half]], axis=1)
    o, lse = _attn_fwd(q_all, q_rope, kv, k_rope, name="attn_fwd")
    h3 = _mm(o, w_oout, res=h2, name="odd_out")
    full.update(_gather_finish(gathers[4], h3, w, 4, False))
    w_fin.append(full["ffn_w_in", 1])
    w_fout.append(full["ffn_w_out", 1])
    h4, ffn1_saved = _ffn_fwd(h3, norm_ffn_g[1:2], w_fin[1], conv_w[1], ffn_conv_b[1:2], w_fout[1], 1)

    dh4, d_final_g, loss_cols = _rows(_loss_fn, [h4, tgt], [final_norm_g[None, :]], [(D_MODEL, F32)],
                                      [(1, D_MODEL), (1, D_MODEL)], name="loss_head")
    loss = lax.psum(jnp.sum(loss_cols), ("x", "y", "c"))

    dh3, gf1 = _ffn_bwd(h3, dh4, ffn1_saved, norm_ffn_g[1:2], w_fin[1], conv_w[1], ffn_conv_b[1:2], w_fout[1], 1)
    scatters = {4: _scatter_start({("ffn_w_in", 1): gf1["w_in"], ("ffn_w_out", 1): gf1["w_out"]}, 4)}
    do = _mm(dh3, w_oout, tb=True, deps=[scatters[4][4]], name="odd_out_dx")
    d_w_oout = _mm(o, dh3, ta=True, out_dtype=BF16, name="odd_out_dw")
    dq_nope, dq_rope, dkv, dk_rope = _attn_bwd(q_all, q_rope, kv, k_rope, o, lse, do, name="attn_bwd")
    lane_pad = ((0, 0), (0, LANE - half))
    dko1, dko2 = jnp.pad(dk_rope[:, :half], lane_pad), jnp.pad(dk_rope[:, half:], lane_pad)
    dq_all = _rope_q_bwd(dq_nope, dq_rope, pos, freqs_q, name="mla_rope_q_bwd")
    d_w_uq_perm_t = _mm(dq_all, qn, ta=True, out_dtype=BF16, name="mla_uq_dw")
    dqn = _mm(dq_all, w_uq_perm_t, name="mla_uq_dx")
    d_w_ukv_t = _mm(dkv, kvn, ta=True, out_dtype=BF16, name="mla_ukv_dw")
    dkvn = _mm(dkv, w_ukv_t, name="mla_ukv_dx")
    dproj_o, d_qg, d_kvg = _rows(_mla_prep_bwd_fn, [proj_o, pos, dqn, dkvn, dko1, dko2], [qg, kvg, freqs_k],
                                 [(ODD_IN_PAD, BF16)], [(1, MLA_Q_RANK), (1, MLA_KV_RANK)], name="mla_prep_bwd")
    d_w_oin_pad = _mm(hn1, dproj_o, ta=True, out_dtype=BF16, name="odd_in_dw")
    dhn1 = _mm(dproj_o, w_oin_pad, tb=True, name="odd_in_dx")
    dh2, d_mix_g1 = _rows(_rms_bwd_fn, [h2, dhn1, dh3], [norm_mix_g[1:2]], [(D_MODEL, F32)], [(1, D_MODEL)], name="mix1_dnorm")
    d_w_oin = jnp.concatenate([d_w_oin_pad[:, :kr0 + half], d_w_oin_pad[:, kr0 + LANE:kr0 + LANE + half]], axis=1)
    d_w_uq_t = jnp.concatenate([d_w_uq_perm_t[:nope_w].reshape(MLA_HEADS, MLA_NOPE, MLA_Q_RANK),
                                d_w_uq_perm_t[nope_w:].reshape(MLA_HEADS, MLA_ROPE, MLA_Q_RANK)], axis=1).reshape(-1, MLA_Q_RANK)
    scatters[3] = _scatter_start({("odd_w_in", 0): d_w_oin, ("mla_w_uq", 0): d_w_uq_t, ("mla_w_ukv", 0): d_w_ukv_t,
                                  ("odd_w_out", 0): d_w_oout}, 3)

    dh1, gf0 = _ffn_bwd(h1, dh2, ffn0_saved, norm_ffn_g[0:1], w_fin[0], conv_w[0], ffn_conv_b[0:1], w_fout[0], 0,
                        deps=[scatters[3][4]])
    scatters[2] = _scatter_start({("ffn_w_in", 0): gf0["w_in"], ("ffn_w_out", 0): gf0["w_out"]}, 2)
    dmix = _mm(dh1, w_eout, tb=True, deps=[scatters[2][4]], name="even_out_dx")
    d_w_eout = _mm(mixin, dh1, ta=True, out_dtype=BF16, name="even_out_dw")
    dq, df, di, dg, d_lb0, d_hgrn_g = _hgrn_bwd(proj, lb0, hgrn_norm_g, states, dmix, name="hgrn_bwd")
    dz1, dgl, d_b_glu = _rows(_s5_gate_bwd_fn, [z5, gl, (dmix, HGRN_DIM, S5_DIM)], [s5_b_glu],
                              [(S5_DIM, F32), (S5_DIM, BF16)], [(1, S5_DIM)], name="s5_gate_bwd")
    dz2 = _mm(dgl, w_glu, tb=True, name="s5_glu_dx")
    d_w_glu = _mm(z5, dgl, ta=True, out_dtype=BF16, name="s5_glu_dw")
    dys, du1, d_s5_d = _rows(_s5_act_bwd_fn, [ys, (proj,) + u_cols, dz1, dz2], [s5_d],
                             [(S5_DIM, BF16), (S5_DIM, F32)], [(1, S5_DIM)], name="s5_act_bwd")
    d_c_re = _mm(s_re, dys, ta=True, name="s5_dc_re")
    d_c_im_neg = _mm(s_im, dys, ta=True, name="s5_dc_im")
    lam_re, lam_im, d_ar, d_ai = _s5_scan_bwd(dys, c_re, c_im_neg, s_re, s_im, a_re, a_im, name="s5_scan_bwd")
    du2 = _mm(lam_im, bb_im, tb=True, res=_mm(lam_re, bb_re, tb=True, name="s5_du_re"), name="s5_du_im")
    d_bb_re = _mm(proj, lam_re, ta=True, a_cols=u_cols, name="s5_dbb_re")
    d_bb_im = _mm(proj, lam_im, ta=True, a_cols=u_cols, name="s5_dbb_im")
    sp_g = _s5_params_bwd(*sp_in, d_ar.sum(0).reshape(S5_GROUPS, S5_STATE), d_ai.sum(0).reshape(S5_GROUPS, S5_STATE),
                          _diag_blocks(d_bb_re, S5_GROUP, S5_STATE), _diag_blocks(d_bb_im, S5_GROUP, S5_STATE), d_lb0)
    d_a_re, d_a_im, d_log_dt, d_bt_re, d_bt_im, d_lb_logits = sp_g
    g_small = dict(mla_q_norm_g=d_qg, mla_kv_norm_g=d_kvg, ffn_conv_w=jnp.stack([gf0["conv_w"], gf1["conv_w"]]))
    g_repl = dict(
        norm_ffn_g=jnp.concatenate([gf0["g"], gf1["g"]]),
        final_norm_g=d_final_g[0], hgrn_lb_logits=d_lb_logits, hgrn_norm_g=d_hgrn_g,
        s5_a_re=d_a_re[None], s5_a_im=d_a_im[None], s5_log_dt=d_log_dt[:, 0][None],
        s5_b_re=d_bt_re.transpose(0, 2, 1)[None], s5_b_im=d_bt_im.transpose(0, 2, 1)[None],
        s5_c_re=_diag_blocks(d_c_re, S5_STATE, S5_GROUP).transpose(0, 2, 1)[None],
        s5_c_im=-_diag_blocks(d_c_im_neg, S5_STATE, S5_GROUP).transpose(0, 2, 1)[None],
        s5_d=d_s5_d, s5_b_glu=d_b_glu, ffn_conv_b=jnp.concatenate([gf0["conv_b"], gf1["conv_b"]]))
    scatters[1] = _scatter_start({("s5_w_glu", 0): d_w_glu, ("even_w_out", 0): d_w_eout}, 1,
                                 extra=[_small_sharded_pack(g_small, w), _replicated_pack(g_repl, REPLICATED_EARLY)])
    dproj, = _rows(_dproj_fn, [dq, df, di, dg, du1, du2], [], [(EVEN_IN, BF16)], [], name="even_dproj", deps=[scatters[1][4]])
    d_w_ein_t = _mm(dproj, hn0, ta=True, out_dtype=BF16, name="even_in_dw")
    dhn0 = _mm(dproj, w_ein_t, name="even_in_dx")
    grad_x, d_mix_g0 = _rows(_rms_bwd_fn, [h0, dhn0, dh1], [norm_mix_g[0:1]], [(D_MODEL, F32)], [(1, D_MODEL)], name="mix0_dnorm")
    g_repl["norm_mix_g"] = jnp.concatenate([d_mix_g0, d_mix_g1])
    scatters[0] = _scatter_start({("even_w_in", 0): d_w_ein_t}, 0, extra=[_replicated_pack(g_repl, REPLICATED_LATE)])

    delta, new_m, new_v = {}, {}, {}
    updated, partial = {}, {}
    after = scatters[0][4]
    for stage in (4, 3, 2, 1, 0):
        partial[stage] = _scatter_finish(scatters[stage], after, stage)
        for (n, l), slabs in zip(STAGES[stage], partial[stage]):
            updated[n] = _adamw(_unit_block(w, (n, l)), slabs, _unit_block(mom, (n, l)), _unit_block(var, (n, l)),
                                name=f"adamw_{n}_{l}", layer=l, n_layers=w[n].shape[0], into=updated.get(n))
        after = updated[STAGES[stage][-1][0]][0]
    grads = {}
    for n in BIG_NAMES:
        outs = [o.transpose(0, 2, 1) for o in updated[n]] if UNIT_TRANSPOSED[n, 0] else updated[n]
        grads[n], delta[n], new_m[n], new_v[n] = outs
    for names, slabs, tag in ((REPLICATED_EARLY, partial[1][-1], "repl"), (REPLICATED_LATE, partial[0][-1], "late"),
                              (SMALL_SHARDED_NAMES, partial[1][-2], "small")):
        packs = [_pack([t[n].reshape(-1) for n in names], LANE, SUBLANE) for t in (w, mom, var)]
        outs = _adamw(packs[0], slabs, packs[1], packs[2], name=f"adamw_{tag}")
        shapes = [w[n].shape for n in names]
        for dst, o_ in zip((grads, delta, new_m, new_v), outs):
            dst.update(zip(names, _unpack(o_.reshape(-1), shapes)))

    return (loss, grad_x[None], *[grads[n] for n in W_NAMES], *[delta[n] for n in W_NAMES],
            *[new_m[n] for n in W_NAMES], *[new_v[n] for n in W_NAMES])
```

```python
import functools
import math

import numpy as np
import jax
import jax.numpy as jnp
from jax import lax
from jax.experimental import pallas as pl
from jax.experimental.pallas import tpu as pltpu

F32 = jnp.float32
BF16 = jnp.bfloat16
_MXU_DTYPE = jnp.bfloat16

D_MODEL = 1024
HGRN_DIM = 512
HGRN_HEAD_DIM = 128
HGRN_HEADS = 4
HGRN_CHUNK = 64
S5_DIM = 512
S5_GROUPS = 32
S5_GROUP = 16
S5_STATE = 64
S5_WIDTH = S5_GROUPS * S5_STATE
EVEN_IN = 4 * HGRN_DIM + S5_DIM
MLA_HEADS = 8
MLA_Q_RANK = 384
MLA_KV_RANK = 256
MLA_NOPE = 128
MLA_ROPE = 64
MLA_V = 128
MLA_QK = MLA_NOPE + MLA_ROPE
ODD_IN = MLA_Q_RANK + MLA_KV_RANK + MLA_ROPE
ODD_IN_PAD = MLA_Q_RANK + MLA_KV_RANK + 2 * 128
ROPE_THETA = 10000.0
D_FF = 2816
EPS = 1e-6
ADAM_LR = 0.001
ADAM_B1 = 0.9
ADAM_B2 = 0.999
ADAM_EPS = 1e-08
ADAM_WD = 0.01
ADAM_STEP = 10

N_DEV = 8
LANE = 128
SUBLANE = 8
VMEM_LIMIT_BYTES = 56 * 1024 * 1024
MESH = pl.DeviceIdType.MESH


def _cparams(sem=None):
    return pltpu.CompilerParams(dimension_semantics=sem, vmem_limit_bytes=VMEM_LIMIT_BYTES)


def _pick(n, cands):
    for c in cands:
        if n % c == 0:
            return c
    raise ValueError(f"no tile for {n} in {cands}")


def _sigmoid(x):
    return 0.5 * jnp.tanh(0.5 * x) + 0.5


def _silu(x):
    return x * _sigmoid(x)


def _gelu(x):
    return 0.5 * x * (1.0 + jnp.tanh(math.sqrt(2.0 / math.pi) * (x + 0.044715 * (x * x * x))))


def _rms(x, g):
    return x * lax.rsqrt(jnp.mean(x * x, axis=-1, keepdims=True) + EPS) * g


def _mxu(a, b, ca, cb):
    return lax.dot_general(a.astype(_MXU_DTYPE), b.astype(_MXU_DTYPE), (((ca,), (cb,)), ((), ())),
                           preferred_element_type=F32)


@functools.partial(jax.custom_vjp, nondiff_argnums=(2, 3))
def _mxu_ad(a, b, ca, cb):
    return _mxu(a, b, ca, cb)


def _mxu_ad_fwd(a, b, ca, cb):
    return _mxu(a, b, ca, cb), (a, b)


def _mxu_ad_bwd(ca, cb, saved, g):
    a, b = saved
    fa, fb = 1 - ca, 1 - cb
    da = _mxu(g, b, 1, fb) if ca == 1 else _mxu(b, g, fb, 1)
    db = _mxu(a, g, fa, 0) if cb == 0 else _mxu(g, a, 0, fa)
    return da, db


_mxu_ad.defvjp(_mxu_ad_fwd, _mxu_ad_bwd)


def _tri(n):
    row = lax.broadcasted_iota(jnp.int32, (n, n), 0)
    col = lax.broadcasted_iota(jnp.int32, (n, n), 1)
    return col <= row


def _cumsum_rows(x, reverse=False):
    n = x.shape[0]
    rowi = lax.broadcasted_iota(jnp.int32, x.shape, 0)
    s = 1
    while s < n:
        if reverse:
            x = x + jnp.where(rowi < n - s, pltpu.roll(x, n - s, 0), 0.0)
        else:
            x = x + jnp.where(rowi >= s, pltpu.roll(x, s, 0), 0.0)
        s *= 2
    return x


@jax.custom_vjp
def _cumsum_rows_ad(x):
    return _cumsum_rows(x)


def _cumsum_rows_ad_fwd(x):
    return _cumsum_rows(x), None


def _cumsum_rows_ad_bwd(_, g):
    return (_cumsum_rows(g, reverse=True),)


_cumsum_rows_ad.defvjp(_cumsum_rows_ad_fwd, _cumsum_rows_ad_bwd)


MM_VMEM_BUDGET = 36 * 1024 * 1024
MM_MAX_TILE = 1408


def _lane_divisors(n, cap, offs=()):
    return [d for d in range(min(n, cap) // LANE * LANE, 0, -LANE) if n % d == 0 and all(o % d == 0 for o in offs)]


def _mm_tiles(M, N, K, sa, sb, so, has_res, m_offs, n_offs, k_offs):
    best = None
    for tm in _lane_divisors(M, MM_MAX_TILE, m_offs):
        for tn in _lane_divisors(N, MM_MAX_TILE, n_offs):
            for tk in _lane_divisors(K, K, k_offs):
                nk = K // tk
                vmem = 2 * (tm * tk * sa + tk * tn * sb + tm * tn * so + tm * tn * 4 * has_res) + (tm * tn * 4 if nk > 1 else 0)
                if vmem <= MM_VMEM_BUDGET:
                    key = (-nk, tm * tn, tn)
                    if best is None or key > best[0]:
                        best = (key, tm, tn, tk)
                    break
    return best[1:]


def _mm(a, b, *, ta=False, tb=False, res=None, out_dtype=F32, a_cols=None, b_rows=None, deps=(), out_rows=None, into=None,
        name):
    a_minor = a.shape[1] if a_cols is None else a_cols[1]
    b_major = b.shape[0] if b_rows is None else b_rows[1]
    K, M = (a.shape[0], a_minor) if ta else (a_minor, a.shape[0])
    N = b_major if tb else b.shape[1]
    assert (b.shape[1] if tb else b_major) == K, (a.shape, b.shape, ta, tb)
    assert b_rows is None or not tb
    a_off = 0 if a_cols is None else a_cols[0]
    b_off = 0 if b_rows is None else b_rows[0]
    has_res = res is not None
    o_off, o_total = (0, M) if out_rows is None else out_rows
    tm, tn, tk = _mm_tiles(M, N, K, a.dtype.itemsize, b.dtype.itemsize, jnp.dtype(out_dtype).itemsize, has_res,
                           ((a_off,) if ta else ()) + (o_off,), (), ((a_off,) if not ta else ()) + (b_off,))
    nk = K // tk
    am, ak = (a_off // tm, 0) if ta else (0, a_off // tk)
    bk, om = b_off // tk, o_off // tm
    a_spec = pl.BlockSpec((tk, tm), lambda i, j, k: (k, i + am)) if ta else pl.BlockSpec((tm, tk), lambda i, j, k: (i, k + ak))
    b_spec = pl.BlockSpec((tn, tk), lambda i, j, k: (j, k)) if tb else pl.BlockSpec((tk, tn), lambda i, j, k: (k + bk, j))
    o_spec = pl.BlockSpec((tm, tn), lambda i, j, k: (i, j))
    ca, cb = (0 if ta else 1), (1 if tb else 0)
    extra = list(deps) + ([into] if into is not None else [])

    n_fixed = 2 + has_res + len(extra)

    def body(*refs):
        a_ref, b_ref = refs[0], refs[1]
        res_ref = refs[2] if has_res else None
        o_ref = refs[n_fixed]
        part = _mxu(a_ref[...], b_ref[...], ca, cb)
        if nk == 1:
            o_ref[...] = (part + res_ref[...] if has_res else part).astype(out_dtype)
            return
        acc_ref = refs[n_fixed + 1]
        k = pl.program_id(2)

        @pl.when(k == 0)
        def _():
            acc_ref[...] = part

        @pl.when(k > 0)
        def _():
            acc_ref[...] += part

        @pl.when(k == nk - 1)
        def _():
            o_ref[...] = (acc_ref[...] + res_ref[...] if has_res else acc_ref[...]).astype(out_dtype)

    ins = [a, b] + ([res] if has_res else []) + extra
    in_specs = [a_spec, b_spec] + ([o_spec] if has_res else []) + [pl.BlockSpec(memory_space=pl.ANY)] * len(extra)
    return pl.pallas_call(
        body, name=name, grid=(M // tm, N // tn, nk),
        in_specs=in_specs, out_specs=pl.BlockSpec((tm, tn), lambda i, j, k: (i + om, j)),
        out_shape=jax.ShapeDtypeStruct((o_total, N), out_dtype),
        scratch_shapes=[pltpu.VMEM((tm, tn), F32)] if nk > 1 else [],
        input_output_aliases={} if into is None else {len(ins) - 1: 0},
        compiler_params=_cparams(("parallel", "parallel", "arbitrary")),
    )(*ins)


def _rows(fn, row_ins, const_ins, row_outs, acc_outs, *, name, tm=512, deps=()):
    norm = [(r, 0, r.shape[1]) if not isinstance(r, tuple) else r for r in row_ins]
    T = norm[0][0].shape[0]
    tm = min(tm, T)
    nr, nc, no, na = len(norm), len(const_ins), len(row_outs), len(acc_outs)
    first_out = nr + nc + len(deps)

    def body(*refs):
        i = pl.program_id(0)
        vals = [r[...] for r in refs[:nr + nc]]
        outs = fn(*vals)
        for o_ref, o in zip(refs[first_out:first_out + no], outs[:no]):
            o_ref[...] = o.astype(o_ref.dtype)
        for a_ref, o in zip(refs[first_out + no:], outs[no:]):
            @pl.when(i == 0)
            def _(a_ref=a_ref, o=o):
                a_ref[...] = o

            @pl.when(i > 0)
            def _(a_ref=a_ref, o=o):
                a_ref[...] += o

    in_specs = []
    for arr, off, w in norm:
        assert off % w == 0, (off, w)
        in_specs.append(pl.BlockSpec((tm, w), lambda i, b=off // w: (i, b)))
    for c in const_ins:
        in_specs.append(pl.BlockSpec(c.shape, lambda i: (0, 0)))
    in_specs += [pl.BlockSpec(memory_space=pl.ANY)] * len(deps)
    out_specs = [pl.BlockSpec((tm, w), lambda i: (i, 0)) for w, _ in row_outs]
    out_specs += [pl.BlockSpec(s, lambda i: (0, 0)) for s in acc_outs]
    out_shape = [jax.ShapeDtypeStruct((T, w), dt) for w, dt in row_outs]
    out_shape += [jax.ShapeDtypeStruct(s, F32) for s in acc_outs]
    return pl.pallas_call(
        body, name=name, grid=(T // tm,), in_specs=in_specs, out_specs=out_specs, out_shape=out_shape,
        compiler_params=_cparams(("arbitrary",)),
    )(*[n[0] for n in norm], *const_ins, *deps)


FFN_COL_TILE = LANE
FFN_ROW_CHUNK = 512


FFN_HALO = 2 * SUBLANE


def _shift_down(ext, s, rows):
    return pltpu.roll(ext, s, 0)[FFN_HALO:FFN_HALO + rows]


def _shift_up(ext, s, rows):
    return pltpu.roll(ext, rows + FFN_HALO - s, 0)[:rows]


def _ffn_chunks(T):
    r = min(FFN_ROW_CHUNK, T)
    return r, T // r


def _ext_before(ref, c, r):
    if c == 0:
        return jnp.concatenate([jnp.zeros((FFN_HALO, ref.shape[1]), F32), ref[0:r, :].astype(F32)], axis=0)
    return ref[c * r - FFN_HALO:(c + 1) * r, :].astype(F32)


def _ext_after(ref, c, r, nch):
    if c == nch - 1:
        return jnp.concatenate([ref[c * r:(c + 1) * r, :].astype(F32), jnp.zeros((FFN_HALO, ref.shape[1]), F32)], axis=0)
    return ref[c * r:(c + 1) * r + FFN_HALO, :].astype(F32)


def _ffn_mid_fwd(au, conv_w, conv_b, *, name):
    T = au.shape[0]
    tc = FFN_COL_TILE
    ncol = D_FF // tc
    r, nch = _ffn_chunks(T)

    def body(a_ref, u_ref, w_ref, b_ref, z_ref):
        w0, w1, w2, bias = w_ref[0:1, :], w_ref[1:2, :], w_ref[2:3, :], b_ref[...]
        for c in range(nch):
            ext = _ext_before(a_ref, c, r)
            pre = w0 * _shift_down(ext, 2, r) + w1 * _shift_down(ext, 1, r) + w2 * ext[FFN_HALO:] + bias
            z_ref[c * r:(c + 1) * r, :] = (_silu(pre) * u_ref[c * r:(c + 1) * r, :].astype(F32)).astype(z_ref.dtype)

    return pl.pallas_call(
        body, name=name, grid=(ncol,),
        in_specs=[pl.BlockSpec((T, tc), lambda j: (0, j)), pl.BlockSpec((T, tc), lambda j: (0, j + ncol)),
                  pl.BlockSpec((3, tc), lambda j: (0, j)), pl.BlockSpec((1, tc), lambda j: (0, j))],
        out_specs=pl.BlockSpec((T, tc), lambda j: (0, j)),
        out_shape=jax.ShapeDtypeStruct((T, D_FF), BF16),
        compiler_params=_cparams(("parallel",)),
    )(au, au, conv_w, conv_b)


def _ffn_mid_bwd(au, dz, conv_w, conv_b, *, name):
    T = au.shape[0]
    tc = FFN_COL_TILE
    ncol = D_FF // tc
    r, nch = _ffn_chunks(T)

    def body(a_ref, u_ref, dz_ref, w_ref, b_ref, da_ref, du_ref, dw_ref, db_ref, dpre_ref):
        w0, w1, w2, bias = w_ref[0:1, :], w_ref[1:2, :], w_ref[2:3, :], b_ref[...]
        dw0 = jnp.zeros((1, tc), F32)
        dw1 = jnp.zeros((1, tc), F32)
        dw2 = jnp.zeros((1, tc), F32)
        db = jnp.zeros((1, tc), F32)
        for c in range(nch):
            rows = slice(c * r, (c + 1) * r)
            ext = _ext_before(a_ref, c, r)
            a2, a1, a0 = _shift_down(ext, 2, r), _shift_down(ext, 1, r), ext[FFN_HALO:]
            pre = w0 * a2 + w1 * a1 + w2 * a0 + bias
            sg = _sigmoid(pre)
            act = pre * sg
            dzc = dz_ref[rows, :].astype(F32)
            du_ref[rows, :] = (dzc * act).astype(du_ref.dtype)
            dpre = dzc * u_ref[rows, :].astype(F32) * (sg * (1.0 + pre * (1.0 - sg)))
            dpre_ref[rows, :] = dpre
            dw0 += jnp.sum(dpre * a2, axis=0, keepdims=True)
            dw1 += jnp.sum(dpre * a1, axis=0, keepdims=True)
            dw2 += jnp.sum(dpre * a0, axis=0, keepdims=True)
            db += jnp.sum(dpre, axis=0, keepdims=True)
        for c in range(nch):
            ext = _ext_after(dpre_ref, c, r, nch)
            da = w0 * _shift_up(ext, 2, r) + w1 * _shift_up(ext, 1, r) + w2 * ext[:r]
            da_ref[c * r:(c + 1) * r, :] = da.astype(da_ref.dtype)
        dw_ref[0:1, :] = dw0
        dw_ref[1:2, :] = dw1
        dw_ref[2:3, :] = dw2
        db_ref[...] = db

    col = lambda j: (0, j)
    return pl.pallas_call(
        body, name=name, grid=(ncol,),
        in_specs=[pl.BlockSpec((T, tc), col), pl.BlockSpec((T, tc), lambda j: (0, j + ncol)), pl.BlockSpec((T, tc), col),
                  pl.BlockSpec((3, tc), col), pl.BlockSpec((1, tc), col)],
        out_specs=[pl.BlockSpec((T, tc), col), pl.BlockSpec((T, tc), col), pl.BlockSpec((3, tc), col), pl.BlockSpec((1, tc), col)],
        out_shape=[jax.ShapeDtypeStruct((T, D_FF), BF16), jax.ShapeDtypeStruct((T, D_FF), BF16),
                   jax.ShapeDtypeStruct((3, D_FF), F32), jax.ShapeDtypeStruct((1, D_FF), F32)],
        scratch_shapes=[pltpu.VMEM((T, tc), F32)],
        compiler_params=_cparams(("parallel",)),
    )(au, au, dz, conv_w, conv_b)


HGRN_BLOCK = 256


def _hgrn_chunk(dot, cumsum, q, f, i, g, lb, ng, st):
    C = q.shape[0]
    forget = lb + (1.0 - lb) * _sigmoid(f)
    k = 1.0 - forget
    b = cumsum(jnp.log(forget))
    b_last = b[C - 1:C, :]
    qd = q * jnp.exp(b)
    kd = k * jnp.exp(-b)
    att = jnp.where(_tri(C), dot(qd, kd, 1, 1), 0.0)
    o = dot(att, i, 1, 0) + dot(qd, st, 1, 1)
    st_new = st * jnp.exp(b_last) + dot(i, k * jnp.exp(b_last - b), 0, 0)
    on = o * lax.rsqrt(jnp.mean(o * o, axis=-1, keepdims=True) + EPS) * ng
    return on * _silu(g), st_new


def _hgrn_specs(T, rev):
    tb = min(HGRN_BLOCK, T)
    nb = T // tb
    blk = (lambda n: nb - 1 - n) if rev else (lambda n: n)
    hd = HGRN_HEAD_DIM
    proj_specs = [pl.BlockSpec((tb, HGRN_DIM), lambda n, k=k: (blk(n), k)) for k in range(4)]
    vec_spec = pl.BlockSpec((1, HGRN_DIM), lambda n: (0, 0))
    tok_spec = pl.BlockSpec((tb, HGRN_DIM), lambda n: (blk(n), 0))
    st_spec = pl.BlockSpec((HGRN_HEADS, tb // HGRN_CHUNK, hd, hd), lambda n: (0, blk(n), 0, 0))
    return tb, nb, proj_specs, vec_spec, tok_spec, st_spec


def _head_cols(h):
    return slice(h * HGRN_HEAD_DIM, (h + 1) * HGRN_HEAD_DIM)


def _hgrn_fwd(proj, lb, ng, *, name):
    T = proj.shape[0]
    tb, nb, proj_specs, vec_spec, tok_spec, st_spec = _hgrn_specs(T, False)
    nsub = tb // HGRN_CHUNK
    hd = HGRN_HEAD_DIM

    def body(q_ref, f_ref, i_ref, g_ref, lb_ref, ng_ref, y_ref, sts_ref, st_ref):
        @pl.when(pl.program_id(0) == 0)
        def _():
            st_ref[...] = jnp.zeros_like(st_ref)

        st = [st_ref[h] for h in range(HGRN_HEADS)]
        for s in range(nsub):
            rows = slice(s * HGRN_CHUNK, (s + 1) * HGRN_CHUNK)
            for h in range(HGRN_HEADS):
                cols = _head_cols(h)
                sts_ref[h, s] = st[h]
                y, st[h] = _hgrn_chunk(_mxu, _cumsum_rows, q_ref[rows, cols], f_ref[rows, cols], i_ref[rows, cols],
                                       g_ref[rows, cols], lb_ref[:, cols], ng_ref[:, cols], st[h])
                y_ref[rows, cols] = y
        for h in range(HGRN_HEADS):
            st_ref[h] = st[h]

    return pl.pallas_call(
        body, name=name, grid=(nb,),
        in_specs=proj_specs + [vec_spec, vec_spec], out_specs=[tok_spec, st_spec],
        out_shape=[jax.ShapeDtypeStruct((T, HGRN_DIM), F32),
                   jax.ShapeDtypeStruct((HGRN_HEADS, T // HGRN_CHUNK, hd, hd), F32)],
        scratch_shapes=[pltpu.VMEM((HGRN_HEADS, hd, hd), F32)],
        compiler_params=_cparams(("arbitrary",)),
    )(proj, proj, proj, proj, lb, ng)


def _hgrn_bwd(proj, lb, ng, states, dmix, *, name):
    T = proj.shape[0]
    tb, nb, proj_specs, vec_spec, tok_spec, st_spec = _hgrn_specs(T, True)
    nsub = tb // HGRN_CHUNK
    hd = HGRN_HEAD_DIM
    chunk = functools.partial(_hgrn_chunk, _mxu_ad, _cumsum_rows_ad)

    def body(q_ref, f_ref, i_ref, g_ref, lb_ref, ng_ref, sts_ref, dy_ref,
             dq_ref, df_ref, di_ref, dg_ref, dlb_ref, dng_ref, dst_ref):
        @pl.when(pl.program_id(0) == 0)
        def _():
            dst_ref[...] = jnp.zeros_like(dst_ref)
            dlb_ref[...] = jnp.zeros_like(dlb_ref)
            dng_ref[...] = jnp.zeros_like(dng_ref)

        dst = [dst_ref[h] for h in range(HGRN_HEADS)]
        dlb = [jnp.zeros((1, hd), F32)] * HGRN_HEADS
        dng = [jnp.zeros((1, hd), F32)] * HGRN_HEADS
        for s in reversed(range(nsub)):
            rows = slice(s * HGRN_CHUNK, (s + 1) * HGRN_CHUNK)
            for h in range(HGRN_HEADS):
                cols = _head_cols(h)
                _, vjp = jax.vjp(chunk, q_ref[rows, cols], f_ref[rows, cols], i_ref[rows, cols], g_ref[rows, cols],
                                 lb_ref[:, cols], ng_ref[:, cols], sts_ref[h, s])
                dq, df, di, dg, dlb_s, dng_s, dst[h] = vjp((dy_ref[rows, cols], dst[h]))
                dq_ref[rows, cols] = dq
                df_ref[rows, cols] = df
                di_ref[rows, cols] = di
                dg_ref[rows, cols] = dg
                dlb[h] = dlb[h] + dlb_s
                dng[h] = dng[h] + dng_s
        for h in range(HGRN_HEADS):
            dst_ref[h] = dst[h]
            dlb_ref[:, _head_cols(h)] += dlb[h]
            dng_ref[:, _head_cols(h)] += dng[h]

    tok_out = jax.ShapeDtypeStruct((T, HGRN_DIM), F32)
    vec_out = jax.ShapeDtypeStruct((1, HGRN_DIM), F32)
    return pl.pallas_call(
        body, name=name, grid=(nb,),
        in_specs=proj_specs + [vec_spec, vec_spec, st_spec, tok_spec],
        out_specs=[tok_spec] * 4 + [vec_spec, vec_spec],
        out_shape=[tok_out] * 4 + [vec_out, vec_out],
        scratch_shapes=[pltpu.VMEM((HGRN_HEADS, hd, hd), F32)],
        compiler_params=_cparams(("arbitrary",)),
    )(proj, proj, proj, proj, lb, ng, states, dmix)


S5_LANES = 512
S5_ROWS = 512


def _cmul(ar, ai, br, bi):
    return ar * br - ai * bi, ar * bi + ai * br


def _power_table(ar, ai, exps):
    a2 = _cmul(ar, ai, ar, ai)
    a4 = _cmul(*a2, *a2)
    e = exps - 1
    pr = jnp.broadcast_to(ar, exps.shape)
    pi = jnp.broadcast_to(ai, exps.shape)
    for bit, (fr, fi) in enumerate(((ar, ai), a2, a4)):
        nr, ni = _cmul(pr, pi, fr, fi)
        on = ((e >> bit) & 1) == 1
        pr, pi = jnp.where(on, nr, pr), jnp.where(on, ni, pi)
    return pr, pi, a2, a4


def _s5_scan_fwd(x, x_cols, b_re, b_im, a_re, a_im, *, name):
    T = x.shape[0]
    w, tr = S5_LANES, min(S5_ROWS, T)
    ncol, nt = S5_WIDTH // w, T // tr
    assert x_cols[0] % x_cols[1] == 0

    def body(x_ref, br_ref, bi_ref, ar_ref, ai_ref, sr_ref, si_ref, carry_ref):
        @pl.when(pl.program_id(1) == 0)
        def _():
            carry_ref[...] = jnp.zeros_like(carry_ref)

        u = x_ref[...].astype(_MXU_DTYPE)
        sr_ref[...] = _mxu(u, br_ref[...], 1, 0)
        si_ref[...] = _mxu(u, bi_ref[...], 1, 0)
        ar, ai = ar_ref[...], ai_ref[...]
        rowi = lax.broadcasted_iota(jnp.int32, (SUBLANE, w), 0)
        pr, pi, a2, a4 = _power_table(ar, ai, rowi + 1)
        steps = [(s, jnp.where(rowi >= s, fr, 0.0), jnp.where(rowi >= s, fi, 0.0)) for s, (fr, fi) in ((1, (ar, ai)), (2, a2), (4, a4))]

        def tile(i, carry):
            cr, ci = carry
            rows = pl.ds(pl.multiple_of(i * SUBLANE, SUBLANE), SUBLANE)
            xr, xi = sr_ref[rows, :], si_ref[rows, :]
            for s, fr, fi in steps:
                zr, zi = pltpu.roll(xr, s, 0), pltpu.roll(xi, s, 0)
                xr, xi = xr + fr * zr - fi * zi, xi + fr * zi + fi * zr
            xr, xi = xr + pr * cr - pi * ci, xi + pr * ci + pi * cr
            sr_ref[rows, :] = xr
            si_ref[rows, :] = xi
            return xr[SUBLANE - 1:SUBLANE, :], xi[SUBLANE - 1:SUBLANE, :]

        cr, ci = lax.fori_loop(0, tr // SUBLANE, tile, (carry_ref[0:1, :], carry_ref[1:2, :]))
        carry_ref[0:1, :] = cr
        carry_ref[1:2, :] = ci

    out = jax.ShapeDtypeStruct((T, S5_WIDTH), F32)
    return pl.pallas_call(
        body, name=name, grid=(ncol, nt),
        in_specs=[pl.BlockSpec((tr, x_cols[1]), lambda j, t: (t, x_cols[0] // x_cols[1])),
                  pl.BlockSpec((S5_DIM, w), lambda j, t: (0, j)), pl.BlockSpec((S5_DIM, w), lambda j, t: (0, j)),
                  pl.BlockSpec((1, w), lambda j, t: (0, j)), pl.BlockSpec((1, w), lambda j, t: (0, j))],
        out_specs=[pl.BlockSpec((tr, w), lambda j, t: (t, j))] * 2,
        out_shape=[out, out],
        scratch_shapes=[pltpu.VMEM((2, w), F32)],
        compiler_params=_cparams(("parallel", "arbitrary")),
    )(x, b_re, b_im, a_re, a_im)


def _s5_scan_bwd(dy, c_re, c_im, s_re, s_im, a_re, a_im, *, name):
    T = dy.shape[0]
    w, tr = S5_LANES, min(S5_ROWS, T)
    ncol, nt = S5_WIDTH // w, T // tr
    ntile = tr // SUBLANE

    def body(dy_ref, cr_ref, ci_ref, sr_ref, si_ref, ar_ref, ai_ref, lr_ref, li_ref, dar_ref, dai_ref, carry_ref):
        @pl.when(pl.program_id(1) == 0)
        def _():
            carry_ref[...] = jnp.zeros_like(carry_ref)
            dar_ref[...] = jnp.zeros_like(dar_ref)
            dai_ref[...] = jnp.zeros_like(dai_ref)

        dyb = dy_ref[...].astype(_MXU_DTYPE)
        lr_ref[...] = _mxu(dyb, cr_ref[...], 1, 1)
        li_ref[...] = _mxu(dyb, ci_ref[...], 1, 1)
        ar, ai = ar_ref[...], -ai_ref[...]
        rowi = lax.broadcasted_iota(jnp.int32, (SUBLANE, w), 0)
        pr, pi, a2, a4 = _power_table(ar, ai, SUBLANE - rowi)
        last = rowi == SUBLANE - 1
        steps = [(s, jnp.where(rowi < SUBLANE - s, fr, 0.0), jnp.where(rowi < SUBLANE - s, fi, 0.0))
                 for s, (fr, fi) in ((1, (ar, ai)), (2, a2), (4, a4))]

        def tile(i, carry):
            cr, ci, dar, dai = carry
            rows = pl.ds(pl.multiple_of((ntile - 1 - i) * SUBLANE, SUBLANE), SUBLANE)
            xr, xi = lr_ref[rows, :], li_ref[rows, :]
            for s, fr, fi in steps:
                zr, zi = pltpu.roll(xr, SUBLANE - s, 0), pltpu.roll(xi, SUBLANE - s, 0)
                xr, xi = xr + fr * zr - fi * zi, xi + fr * zi + fi * zr
            xr, xi = xr + pr * cr - pi * ci, xi + pr * ci + pi * cr
            lr_ref[rows, :] = xr
            li_ref[rows, :] = xi
            nr = jnp.where(last, cr, pltpu.roll(xr, SUBLANE - 1, 0))
            ni = jnp.where(last, ci, pltpu.roll(xi, SUBLANE - 1, 0))
            sr, si = sr_ref[rows, :], si_ref[rows, :]
            return xr[0:1, :], xi[0:1, :], dar + nr * sr + ni * si, dai + ni * sr - nr * si

        cr, ci, dar, dai = lax.fori_loop(
            0, ntile, tile, (carry_ref[0:1, :], carry_ref[1:2, :], jnp.zeros((SUBLANE, w), F32), jnp.zeros((SUBLANE, w), F32)))
        carry_ref[0:1, :] = cr
        carry_ref[1:2, :] = ci
        dar_ref[...] += dar
        dai_ref[...] += dai

    tok = pl.BlockSpec((tr, w), lambda j, t: (nt - 1 - t, j))
    vec = pl.BlockSpec((1, w), lambda j, t: (0, j))
    acc = pl.BlockSpec((SUBLANE, w), lambda j, t: (0, j))
    out = jax.ShapeDtypeStruct((T, S5_WIDTH), F32)
    accs = jax.ShapeDtypeStruct((SUBLANE, S5_WIDTH), F32)
    return pl.pallas_call(
        body, name=name, grid=(ncol, nt),
        in_specs=[pl.BlockSpec((tr, S5_DIM), lambda j, t: (nt - 1 - t, 0)),
                  pl.BlockSpec((w, S5_DIM), lambda j, t: (j, 0)), pl.BlockSpec((w, S5_DIM), lambda j, t: (j, 0)),
                  tok, tok, vec, vec],
        out_specs=[tok, tok, acc, acc],
        out_shape=[out, out, accs, accs],
        scratch_shapes=[pltpu.VMEM((2, w), F32)],
        compiler_params=_cparams(("parallel", "arbitrary")),
    )(dy, c_re, c_im, s_re, s_im, a_re, a_im)


ATTN_BLOCK = 1024
_NEG = -1e30


def _qk_cat(nope, rope):
    return jnp.concatenate([nope.astype(_MXU_DTYPE), rope.astype(_MXU_DTYPE)], axis=1)


_QK_SCALE = MLA_QK ** -0.5
_LOG2E = math.log2(math.e)


def _attn_scores(q, k, diagonal):
    s = _mxu(q, k, 1, 1) * (_QK_SCALE * _LOG2E)
    if diagonal:
        s = jnp.where(_tri(s.shape[0]), s, _NEG)
    return s


def _attn_fwd(q_all, q_rope, kv, k_rope, *, name):
    T = q_all.shape[0]
    tq = min(ATTN_BLOCK, T)
    nq = T // tq

    def body(qn_ref, qr_ref, kn_ref, v_ref, kr_ref, o_ref, lse_ref):
        i = pl.program_id(1)
        q = _qk_cat(qn_ref[...], qr_ref[0])

        def step(j, carry, diagonal):
            m, l, acc = carry
            ks = pl.ds(pl.multiple_of(j * tq, tq), tq)
            s = _attn_scores(q, _qk_cat(kn_ref[ks, :], kr_ref[ks, :]), diagonal)
            m_new = jnp.maximum(m, jnp.max(s, axis=-1, keepdims=True))
            p = jnp.exp2(s - m_new)
            alpha = jnp.exp2(m - m_new)
            return m_new, alpha * l + jnp.sum(p, axis=-1, keepdims=True), alpha * acc + _mxu(p, v_ref[ks, :], 1, 0)

        init = (jnp.full((tq, 1), _NEG, F32), jnp.zeros((tq, 1), F32), jnp.zeros((tq, MLA_V), F32))
        below = lax.fori_loop(0, i, functools.partial(step, diagonal=False), init)
        m, l, acc = step(i, below, diagonal=True)
        o_ref[...] = acc / l
        lse_ref[0] = m + jnp.log2(l)

    return pl.pallas_call(
        body, name=name, grid=(MLA_HEADS, nq),
        in_specs=[pl.BlockSpec((tq, MLA_NOPE), lambda h, i: (i, h)), pl.BlockSpec((1, tq, MLA_ROPE), lambda h, i: (h, i, 0)),
                  pl.BlockSpec((T, MLA_NOPE), lambda h, i: (0, 2 * h)), pl.BlockSpec((T, MLA_V), lambda h, i: (0, 2 * h + 1)),
                  pl.BlockSpec((T, MLA_ROPE), lambda h, i: (0, 0))],
        out_specs=[pl.BlockSpec((tq, MLA_V), lambda h, i: (i, h)), pl.BlockSpec((1, tq, 1), lambda h, i: (h, i, 0))],
        out_shape=[jax.ShapeDtypeStruct((T, MLA_HEADS * MLA_V), F32), jax.ShapeDtypeStruct((MLA_HEADS, T, 1), F32)],
        compiler_params=_cparams(("arbitrary", "arbitrary")),
    )(q_all, q_rope, kv, kv, k_rope)


def _attn_bwd(q_all, q_rope, kv, k_rope, o, lse, do, *, name):
    T = q_all.shape[0]
    tk = min(ATTN_BLOCK, T)
    nk = T // tk

    def body(qn_ref, qr_ref, kv_ref, kr_ref, o_ref, lse_ref, do_ref, dqn_ref, dqr_ref, dkv_ref, dkr_ref, delta_ref):
        h, j = pl.program_id(0), pl.program_id(1)

        @pl.when(j == 0)
        def _():
            dqn_ref[...] = jnp.zeros_like(dqn_ref)
            dqr_ref[...] = jnp.zeros_like(dqr_ref)
            delta_ref[...] = jnp.sum(do_ref[...] * o_ref[...], axis=-1, keepdims=True)

        @pl.when((j == 0) & (h == 0))
        def _():
            dkr_ref[...] = jnp.zeros_like(dkr_ref)

        krows = pl.ds(pl.multiple_of(j * tk, tk), tk)
        k = _qk_cat(kv_ref[:, :MLA_NOPE], kr_ref[krows, :])
        v = kv_ref[:, MLA_NOPE:].astype(_MXU_DTYPE)

        def step(i, carry, diagonal):
            dk, dv = carry
            qs = pl.ds(pl.multiple_of(i * tk, tk), tk)
            q, dob = _qk_cat(qn_ref[qs, :], qr_ref[0, qs, :]), do_ref[qs, :].astype(_MXU_DTYPE)
            p = jnp.exp2(_attn_scores(q, k, diagonal) - lse_ref[0, qs, :])
            ds = p * (_mxu(dob, v, 1, 1) - delta_ref[qs, :]) * _QK_SCALE
            dq = _mxu(ds, k, 1, 0)
            dqn_ref[qs, :] += dq[:, :MLA_NOPE]
            dqr_ref[0, qs, :] += dq[:, MLA_NOPE:]
            return dk + _mxu(ds, q, 0, 0), dv + _mxu(p, dob, 0, 0)

        on_diagonal = step(j, (jnp.zeros((tk, MLA_QK), F32), jnp.zeros((tk, MLA_V), F32)), diagonal=True)
        dk, dv = lax.fori_loop(j + 1, nk, functools.partial(step, diagonal=False), on_diagonal)
        dkv_ref[:, :MLA_NOPE] = dk[:, :MLA_NOPE].astype(dkv_ref.dtype)
        dkv_ref[:, MLA_NOPE:] = dv.astype(dkv_ref.dtype)
        dkr_ref[krows, :] += dk[:, MLA_NOPE:]

    head_cols = pl.BlockSpec((T, MLA_NOPE), lambda h, j: (0, h))
    head_rope = pl.BlockSpec((1, T, MLA_ROPE), lambda h, j: (h, 0, 0))
    kv_spec = pl.BlockSpec((tk, MLA_NOPE + MLA_V), lambda h, j: (j, h))
    kr_spec = pl.BlockSpec((T, MLA_ROPE), lambda h, j: (0, 0))
    return pl.pallas_call(
        body, name=name, grid=(MLA_HEADS, nk),
        in_specs=[head_cols, head_rope, kv_spec, kr_spec, head_cols, pl.BlockSpec((1, T, 1), lambda h, j: (h, 0, 0)), head_cols],
        out_specs=[head_cols, head_rope, kv_spec, kr_spec],
        out_shape=[jax.ShapeDtypeStruct((T, MLA_HEADS * MLA_NOPE), F32), jax.ShapeDtypeStruct((MLA_HEADS, T, MLA_ROPE), F32),
                   jax.ShapeDtypeStruct((T, MLA_HEADS * (MLA_NOPE + MLA_V)), BF16), jax.ShapeDtypeStruct((T, MLA_ROPE), F32)],
        scratch_shapes=[pltpu.VMEM((T, 1), F32)],
        compiler_params=_cparams(("arbitrary", "arbitrary")),
    )(q_all, q_rope, kv, k_rope, o, lse, do)


def _s5_discretize(a_re, a_im, log_dt, bt_re, bt_im, lb_logits):
    dt = jnp.exp(log_dt)
    mag = jnp.exp(a_re * dt)
    abr, abi = mag * jnp.cos(a_im * dt), mag * jnp.sin(a_im * dt)
    den = a_re * a_re + a_im * a_im
    xr, xi = abr - 1.0, abi
    cr = ((xr * a_re + xi * a_im) / den)[:, None, :]
    ci = ((xi * a_re - xr * a_im) / den)[:, None, :]
    e = jnp.exp(lb_logits - jnp.max(lb_logits, axis=0, keepdims=True))
    lb = e[0:1, :] / jnp.sum(e, axis=0, keepdims=True)
    return abr, abi, cr * bt_re - ci * bt_im, cr * bt_im + ci * bt_re, lb


def _whole(shape):
    return pl.BlockSpec(shape, lambda: (0,) * len(shape))


def _s5_params_fwd(a_re, a_im, log_dt, bt_re, bt_im, lb_logits):
    ins = (a_re, a_im, log_dt, bt_re, bt_im, lb_logits)
    outs = [jax.ShapeDtypeStruct(s, F32) for s in (a_re.shape, a_re.shape, bt_re.shape, bt_re.shape, (1, lb_logits.shape[1]))]

    def body(*refs):
        res = _s5_discretize(*[r[...] for r in refs[:6]])
        for o_ref, o in zip(refs[6:], res):
            o_ref[...] = o

    return pl.pallas_call(body, name="s5_params_fwd", in_specs=[_whole(a.shape) for a in ins],
                          out_specs=[_whole(o.shape) for o in outs], out_shape=outs, compiler_params=_cparams())(*ins)


def _s5_params_bwd(a_re, a_im, log_dt, bt_re, bt_im, lb_logits, d_abr, d_abi, d_bbr, d_bbi, d_lb):
    ins = (a_re, a_im, log_dt, bt_re, bt_im, lb_logits, d_abr, d_abi, d_bbr, d_bbi, d_lb)
    outs = [jax.ShapeDtypeStruct(a.shape, F32) for a in ins[:6]]

    def body(*refs):
        _, vjp = jax.vjp(_s5_discretize, *[r[...] for r in refs[:6]])
        for o_ref, o in zip(refs[11:], vjp(tuple(r[...] for r in refs[6:11]))):
            o_ref[...] = o

    return pl.pallas_call(body, name="s5_params_bwd", in_specs=[_whole(a.shape) for a in ins],
                          out_specs=[_whole(o.shape) for o in outs], out_shape=outs, compiler_params=_cparams())(*ins)


ADAMW_WHOLE_BYTES = 1024 * 1024


def _adamw(w, g, m, v, *, name, layer=0, n_layers=1, into=None):
    R, C = w.shape
    whole = R % SUBLANE != 0 or R * C * w.dtype.itemsize <= ADAMW_WHOLE_BYTES
    tr = R if whole else _pick(R, (256, 128, 64, 32, 16, 8))
    slabs = g.shape[0]
    n_prev = 0 if into is None else len(into)

    def body(w_ref, g_ref, m_ref, v_ref, *rest):
        g_out, d_ref, mo_ref, vo_ref = rest[n_prev:]
        gv = g_ref[0].astype(F32)
        for s in range(1, slabs):
            gv = gv + g_ref[s].astype(F32)
        m2 = ADAM_B1 * m_ref[...] + (1.0 - ADAM_B1) * gv
        v2 = ADAM_B2 * v_ref[...] + (1.0 - ADAM_B2) * (gv * gv)
        m_hat = m2 / (1.0 - ADAM_B1 ** ADAM_STEP)
        v_hat = v2 / (1.0 - ADAM_B2 ** ADAM_STEP)
        g_out[0] = gv
        d_ref[0] = -ADAM_LR * (m_hat / (jnp.sqrt(v_hat) + ADAM_EPS) + ADAM_WD * w_ref[...])
        mo_ref[0] = m2
        vo_ref[0] = v2

    spec = pl.BlockSpec((tr, C), lambda i: (i, 0))
    out_spec = pl.BlockSpec((1, tr, C), lambda i: (layer, i, 0))
    out = jax.ShapeDtypeStruct((n_layers, R, C), F32)
    return pl.pallas_call(
        body, name=name, grid=(R // tr,),
        in_specs=[spec, pl.BlockSpec((slabs, tr, C), lambda i: (0, i, 0)), spec, spec] + [pl.BlockSpec(memory_space=pl.ANY)] * n_prev,
        out_specs=[out_spec] * 4, out_shape=[out] * 4, input_output_aliases={4 + k: k for k in range(n_prev)},
        compiler_params=_cparams(("parallel",)))(w, g, m, v, *(into or ()))


N_CHIPS = 4
N_CORES = 2


_FLIPS = tuple((dx, dy, dc) for dx in (0, 1) for dy in (0, 1) for dc in (0, 1) if (dx, dy, dc) != (0, 0, 0))


_HBM = pl.BlockSpec(memory_space=pltpu.HBM)
_SEM = pl.BlockSpec(memory_space=pltpu.SEMAPHORE)
_SPLIT_COPY = pltpu.CompilerParams(has_side_effects=pltpu.SideEffectType.DATAFLOW_SIDE_EFFECTING)


def _exchange_copies(src_refs, land_refs, send_sems, recv_sems, scatter, arriving):
    x, y, c = lax.axis_index("x"), lax.axis_index("y"), lax.axis_index("c")
    me_chip = 2 * x + y
    copies = []
    for a, (s_ref, l_ref) in enumerate(zip(src_refs, land_refs)):
        for j, (dx, dy, dc) in enumerate(_FLIPS):
            px, py, pc = (1 - x if dx else x), (1 - y if dy else y), (1 - c if dc else c)
            k = a * len(_FLIPS) + j
            p_chip = 2 * px + py
            copies.append(pltpu.make_async_remote_copy(
                src_ref=s_ref.at[p_chip, pc] if scatter else s_ref, dst_ref=l_ref.at[p_chip, pc] if arriving else l_ref.at[me_chip, c],
                send_sem=send_sems.at[k], recv_sem=recv_sems.at[k], device_id=(px, py, pc), device_id_type=MESH))
    return copies


def _exchange_start(srcs, *, scatter, name, after=()):
    n_arr = len(srcs)
    n_sem = n_arr * len(_FLIPS)
    n_in = 2 * n_arr + len(after)
    lands = [lax.empty(s.shape if scatter else (N_CHIPS, N_CORES) + s.shape, s.dtype) for s in srcs]

    def body(*refs):
        src_refs, land_refs = refs[:n_arr], refs[n_arr:2 * n_arr]
        for cp in _exchange_copies(src_refs, land_refs, refs[n_in], refs[n_in + 1], scatter, arriving=False):
            cp.start()
        refs[-1][...] = jnp.zeros_like(refs[-1])

    thru = [pltpu.HBM(a.shape, a.dtype) for a in srcs + lands]
    outs = pl.pallas_call(
        body, name=name,
        out_shape=(pltpu.SemaphoreType.DMA((n_sem,)), pltpu.SemaphoreType.DMA((n_sem,)), *thru,
                   jax.ShapeDtypeStruct((SUBLANE, LANE), F32)),
        in_specs=[_HBM] * (2 * n_arr) + [pl.BlockSpec(memory_space=pl.ANY)] * len(after),
        out_specs=(_SEM, _SEM, *[_HBM] * (2 * n_arr), pl.BlockSpec(memory_space=pltpu.VMEM)),
        input_output_aliases={i: 2 + i for i in range(2 * n_arr)}, compiler_params=_SPLIT_COPY,
    )(*[pltpu.with_memory_space_constraint(a, pltpu.HBM) for a in srcs + lands], *after)
    return outs[0], outs[1], list(outs[2:2 + n_arr]), list(outs[2 + n_arr:2 + 2 * n_arr]), outs[-1]


def _exchange_wait(started, after, *, scatter, name):
    send_sems, recv_sems, srcs, lands, _ = started
    n_arr = len(srcs)

    def body(*refs):
        src_refs, land_refs = refs[:n_arr], refs[n_arr:2 * n_arr]
        for cp in _exchange_copies(src_refs, land_refs, refs[2 * n_arr], refs[2 * n_arr + 1], scatter, arriving=True):
            cp.wait_send()
            cp.wait_recv()

    outs = pl.pallas_call(
        body, name=name, out_shape=[pltpu.HBM(a.shape, a.dtype) for a in srcs + lands],
        in_specs=[_HBM] * (2 * n_arr) + [_SEM, _SEM, pl.BlockSpec(memory_space=pl.ANY)], out_specs=[_HBM] * (2 * n_arr),
        input_output_aliases={i: i for i in range(2 * n_arr)}, compiler_params=_SPLIT_COPY,
    )(*srcs, *lands, send_sems, recv_sems, after)
    return list(outs[:n_arr]), list(outs[n_arr:])


def _with_own(land, own):
    me_chip = 2 * lax.axis_index("x") + lax.axis_index("y")
    return lax.dynamic_update_slice(land, own[None, None], (me_chip, lax.axis_index("c")) + (0,) * own.ndim)


def _rms_fwd_fn(h, g):
    return (_rms(h, g),)


def _rms_bwd_fn(h, dhn, dres, g):
    _, vjp = jax.vjp(_rms, h, g)
    dh, dg = vjp(dhn)
    return dh + dres, dg


def _loss_fn(h, tgt, g):
    y, vjp = jax.vjp(_rms, h, g)
    diff = y - tgt
    dh, dg = vjp(diff * (1.0 / D_MODEL))
    return dh, dg, (0.5 / D_MODEL) * jnp.sum(diff * diff, axis=0, keepdims=True)


def _s5_act(ys, u, d):
    return _gelu(ys + d * u)


def _s5_gate(z, gl, b):
    return z * _sigmoid(gl + b)


def _s5_act_fn(ys, u, d):
    return (_s5_act(ys, u, d),)


def _s5_mix_fn(ya, z, gl, b):
    return (jnp.concatenate([ya, _s5_gate(z, gl, b)], axis=1),)


def _s5_gate_bwd_fn(z, gl, dyb, b):
    _, vjp = jax.vjp(_s5_gate, z, gl, b)
    return vjp(dyb)


def _s5_act_bwd_fn(ys, u, dz1, dz2, d):
    _, vjp = jax.vjp(_s5_act, ys, u, d)
    return vjp(dz1 + dz2)


def _dproj_fn(dq, df, di, dg, du1, du2):
    return (jnp.concatenate([dq, df, di, dg, du1 + du2], axis=1),)


def _rope_pair(r1, r2, pos, freqs):
    ang = pos.astype(F32) * freqs
    c, s = jnp.cos(ang), jnp.sin(ang)
    return r1 * c - r2 * s, r1 * s + r2 * c


_ODD_SPLITS = (0, MLA_Q_RANK, MLA_Q_RANK + MLA_KV_RANK, MLA_Q_RANK + MLA_KV_RANK + LANE, ODD_IN_PAD)


def _mla_prep(cq, ckv, k1, k2, qg, kvg, pos, freqs):
    ko1, ko2 = _rope_pair(k1, k2, pos, freqs)
    return _rms(cq, qg), _rms(ckv, kvg), ko1, ko2


def _mla_prep_fn(proj, pos, qg, kvg, freqs):
    parts = [proj[:, a:b] for a, b in zip(_ODD_SPLITS[:-1], _ODD_SPLITS[1:])]
    return _mla_prep(*parts, qg, kvg, pos, freqs)


def _mla_prep_bwd_fn(proj, pos, dqn, dkvn, dko1, dko2, qg, kvg, freqs):
    parts = [proj[:, a:b] for a, b in zip(_ODD_SPLITS[:-1], _ODD_SPLITS[1:])]
    _, vjp = jax.vjp(lambda *a: _mla_prep(*a, pos, freqs), *parts, qg, kvg)
    dcq, dckv, dk1, dk2, dqg, dkvg = vjp((dqn, dkvn, dko1, dko2))
    return jnp.concatenate([dcq, dckv, dk1, dk2], axis=1), dqg, dkvg


ROPE_ROWS = 512


def _rope_heads(x, pos, freqs):
    half = MLA_ROPE // 2
    reps = x.shape[1] // LANE
    ang = pos.astype(F32) * freqs[:, :LANE]
    cos = jnp.concatenate([jnp.cos(ang)] * reps, axis=1)
    sin = jnp.concatenate([jnp.sin(ang)] * reps, axis=1)
    first = (lax.broadcasted_iota(jnp.int32, x.shape, 1) % MLA_ROPE) < half
    other = jnp.where(first, pltpu.roll(x, x.shape[1] - half, 1), pltpu.roll(x, half, 1))
    return x * cos + other * jnp.where(first, -sin, sin)


def _rope_q_fwd(q_all, pos, freqs, *, name):
    T = q_all.shape[0]
    tm = min(ROPE_ROWS, T)
    nope_w, rope_w = MLA_HEADS * MLA_NOPE, MLA_HEADS * MLA_ROPE

    def body(r_ref, pos_ref, f_ref, o_ref):
        out = _rope_heads(r_ref[...], pos_ref[...], f_ref[...])
        for h in range(MLA_HEADS):
            o_ref[h] = out[:, h * MLA_ROPE:(h + 1) * MLA_ROPE]

    return pl.pallas_call(
        body, name=name, grid=(T // tm,),
        in_specs=[pl.BlockSpec((tm, rope_w), lambda i: (i, nope_w // rope_w)), pl.BlockSpec((tm, 1), lambda i: (i, 0)),
                  pl.BlockSpec((1, rope_w), lambda i: (0, 0))],
        out_specs=pl.BlockSpec((MLA_HEADS, tm, MLA_ROPE), lambda i: (0, i, 0)),
        out_shape=jax.ShapeDtypeStruct((MLA_HEADS, T, MLA_ROPE), F32),
        compiler_params=_cparams(("parallel",)),
    )(q_all, pos, freqs)


def _rope_q_bwd(dq_nope, dq_rope, pos, freqs, *, name):
    T = dq_nope.shape[0]
    tm = min(ROPE_ROWS, T)
    nope_w, rope_w = MLA_HEADS * MLA_NOPE, MLA_HEADS * MLA_ROPE

    def body(dn_ref, dr_ref, pos_ref, f_ref, o_ref):
        d_out = jnp.concatenate([dr_ref[h] for h in range(MLA_HEADS)], axis=1)
        o_ref[:, :nope_w] = dn_ref[...].astype(o_ref.dtype)
        o_ref[:, nope_w:] = _rope_heads(d_out, pos_ref[...], -f_ref[...]).astype(o_ref.dtype)

    return pl.pallas_call(
        body, name=name, grid=(T // tm,),
        in_specs=[pl.BlockSpec((tm, nope_w), lambda i: (i, 0)), pl.BlockSpec((MLA_HEADS, tm, MLA_ROPE), lambda i: (0, i, 0)),
                  pl.BlockSpec((tm, 1), lambda i: (i, 0)), pl.BlockSpec((1, rope_w), lambda i: (0, 0))],
        out_specs=pl.BlockSpec((tm, nope_w + rope_w), lambda i: (i, 0)),
        out_shape=jax.ShapeDtypeStruct((T, nope_w + rope_w), BF16),
        compiler_params=_cparams(("parallel",)),
    )(dq_nope, dq_rope, pos, freqs)


W_NAMES = ("norm_mix_g", "norm_ffn_g", "final_norm_g", "even_w_in", "hgrn_lb_logits", "hgrn_norm_g", "s5_a_re", "s5_a_im",
           "s5_log_dt", "s5_b_re", "s5_b_im", "s5_c_re", "s5_c_im", "s5_d", "s5_w_glu", "s5_b_glu", "even_w_out", "odd_w_in",
           "mla_q_norm_g", "mla_w_uq", "mla_kv_norm_g", "mla_w_ukv", "odd_w_out", "ffn_w_in", "ffn_conv_w", "ffn_conv_b",
           "ffn_w_out")
BIG_UNITS = (("even_w_in", 0, True), ("s5_w_glu", 0, False), ("even_w_out", 0, False), ("odd_w_in", 0, False),
             ("mla_w_uq", 0, True), ("mla_w_ukv", 0, True), ("odd_w_out", 0, False),
             ("ffn_w_in", 0, True), ("ffn_w_in", 1, True), ("ffn_w_out", 0, False), ("ffn_w_out", 1, False))
BIG_NAMES = tuple(dict.fromkeys(u[0] for u in BIG_UNITS))
SMALL_SHARDED = (("mla_q_norm_g", 1), ("mla_kv_norm_g", 1), ("ffn_conv_w", 2))
SMALL_SHARDED_NAMES = tuple(s[0] for s in SMALL_SHARDED)
REPLICATED = tuple(n for n in W_NAMES if n not in BIG_NAMES + SMALL_SHARDED_NAMES)
REPLICATED_LATE = ("norm_mix_g",)
REPLICATED_EARLY = tuple(n for n in REPLICATED if n not in REPLICATED_LATE)


def _pack(flats, cols, row_mult):
    flat = jnp.concatenate(flats, axis=-1)
    pad = (-flat.shape[-1]) % (cols * row_mult)
    flat = jnp.pad(flat, [(0, 0)] * (flat.ndim - 1) + [(0, pad)])
    return flat.reshape(flat.shape[:-1] + (-1, cols))


def _unpack(flat, shapes):
    out, off = [], 0
    for shp in shapes:
        n = int(np.prod(shp))
        out.append(flat[..., off:off + n].reshape(flat.shape[:-1] + tuple(shp)))
        off += n
    return out


UNIT_TRANSPOSED = {(n, l): t for n, l, t in BIG_UNITS}


def _unit_block(arrs, unit):
    blk = arrs[unit[0]][unit[1]]
    return blk.T if UNIT_TRANSPOSED[unit] else blk
STAGES = ((("even_w_in", 0),),
          (("s5_w_glu", 0), ("even_w_out", 0)),
          (("ffn_w_in", 0), ("ffn_w_out", 0)),
          (("odd_w_in", 0), ("mla_w_uq", 0), ("mla_w_ukv", 0), ("odd_w_out", 0)),
          (("ffn_w_in", 1), ("ffn_w_out", 1)))


def _gather_start(w, stage, with_small, after):
    srcs = [_unit_block(w, unit).astype(BF16) for unit in STAGES[stage]]
    if with_small:
        srcs.append(_pack([w[n].reshape(-1) for n in SMALL_SHARDED_NAMES], LANE, SUBLANE))
    return _exchange_start(srcs, scatter=False, name=f"gather_start_{stage}", after=after)


def _gather_finish(started, after, w, stage, with_small):
    srcs, lands = _exchange_wait(started, after, scatter=False, name=f"gather_wait_{stage}")
    lands = [_with_own(land, src) for land, src in zip(lands, srcs)]
    big = {unit: g.reshape(N_DEV * g.shape[2], g.shape[3]) for unit, g in zip(STAGES[stage], lands)}
    if not with_small:
        return big
    parts = _unpack(lands[-1].reshape(N_DEV, -1), [w[n].shape for n in SMALL_SHARDED_NAMES])
    small = {}
    for (n, ax), p in zip(SMALL_SHARDED, parts):
        shp = list(w[n].shape)
        shp[ax] *= N_DEV
        small[n] = jnp.moveaxis(p, 0, ax).reshape(shp)
    return big, small


def _scatter_start(g_big, stage, extra=()):
    srcs = []
    for unit in STAGES[stage]:
        g = g_big[unit].astype(BF16)
        srcs.append(g.reshape(N_CHIPS, N_CORES, g.shape[0] // N_DEV, g.shape[1]))
    return _exchange_start(srcs + list(extra), scatter=True, name=f"scatter_start_{stage}")


def _scatter_finish(started, after, stage):
    srcs, lands = _exchange_wait(started, after, scatter=True, name=f"scatter_wait_{stage}")
    me_chip, c = 2 * lax.axis_index("x") + lax.axis_index("y"), lax.axis_index("c")
    outs = []
    for land, src in zip(lands, srcs):
        own = lax.dynamic_slice(src, (me_chip, c) + (0,) * (src.ndim - 2), (1, 1) + src.shape[2:])[0, 0]
        outs.append(_with_own(land, own).reshape((N_DEV,) + land.shape[2:]))
    return outs


def _small_sharded_pack(g_small, w):
    flats = []
    for n, ax in SMALL_SHARDED:
        shp = list(w[n].shape)
        g = g_small[n].astype(F32).reshape(shp[:ax] + [N_DEV] + shp[ax:])
        flats.append(jnp.moveaxis(g, ax, 0).reshape(N_DEV, -1))
    small = _pack(flats, LANE, SUBLANE)
    return small.reshape((N_CHIPS, N_CORES) + small.shape[1:])


def _replicated_pack(g_repl, names):
    vec = _pack([g_repl[n].reshape(-1).astype(F32) for n in names], LANE, SUBLANE)
    return jnp.broadcast_to(vec, (N_CHIPS, N_CORES) + vec.shape)


def _block_diag(blocks):
    G, a, b = blocks.shape
    return jnp.einsum('gab,gk->gakb', blocks, jnp.eye(G, dtype=blocks.dtype)).reshape(G * a, G * b)


def _diag_blocks(mat, a, b):
    G = mat.shape[0] // a
    return jnp.einsum('gagb->gab', mat.reshape(G, a, G, b))


def _ffn_fwd(h, g, w_in_t, conv_w, conv_b, w_out, tag):
    hn, = _rows(_rms_fwd_fn, [h], [g], [(D_MODEL, BF16)], [], name=f"ffn{tag}_norm")
    au = _mm(hn, w_in_t, tb=True, out_dtype=BF16, name=f"ffn{tag}_in")
    z = _ffn_mid_fwd(au, conv_w, conv_b, name=f"ffn{tag}_mid")
    return _mm(z, w_out, res=h, name=f"ffn{tag}_out"), (hn, au, z)


def _ffn_bwd(h, dh, saved, g, w_in_t, conv_w, conv_b, w_out, tag, deps=()):
    hn, au, z = saved
    dz = _mm(dh, w_out, tb=True, out_dtype=BF16, deps=deps, name=f"ffn{tag}_dz")
    dw_out = _mm(z, dh, ta=True, out_dtype=BF16, name=f"ffn{tag}_dwout")
    da, du, dcw, dcb = _ffn_mid_bwd(au, dz, conv_w, conv_b, name=f"ffn{tag}_dmid")
    dhn = _mm(da, w_in_t, b_rows=(0, D_FF), name=f"ffn{tag}_dhn_a")
    dhn = _mm(du, w_in_t, b_rows=(D_FF, D_FF), res=dhn, name=f"ffn{tag}_dhn_u")
    dw_in = _mm(da, hn, ta=True, out_dtype=BF16, out_rows=(0, 2 * D_FF), name=f"ffn{tag}_dwin_a")
    dw_in = _mm(du, hn, ta=True, out_dtype=BF16, out_rows=(D_FF, 2 * D_FF), into=dw_in, name=f"ffn{tag}_dwin_u")
    dh_in, dg = _rows(_rms_bwd_fn, [h, dhn, dh], [g], [(D_MODEL, F32)], [(1, D_MODEL)], name=f"ffn{tag}_dnorm")
    return dh_in, dict(g=dg, w_in=dw_in, conv_w=dcw, conv_b=dcb, w_out=dw_out)


def kernel(x, positions, norm_mix_g, norm_ffn_g, final_norm_g, even_w_in, hgrn_lb_logits, hgrn_norm_g, s5_a_re, s5_a_im, s5_log_dt, s5_b_re, s5_b_im, s5_c_re, s5_c_im, s5_d, s5_w_glu, s5_b_glu, even_w_out, odd_w_in, mla_q_norm_g, mla_w_uq, mla_kv_norm_g, mla_w_ukv, odd_w_out, ffn_w_in, ffn_conv_w, ffn_conv_b, ffn_w_out, loss_target, m_norm_mix_g, m_norm_ffn_g, m_final_norm_g, m_even_w_in, m_hgrn_lb_logits, m_hgrn_norm_g, m_s5_a_re, m_s5_a_im, m_s5_log_dt, m_s5_b_re, m_s5_b_im, m_s5_c_re, m_s5_c_im, m_s5_d, m_s5_w_glu, m_s5_b_glu, m_even_w_out, m_odd_w_in, m_mla_q_norm_g, m_mla_w_uq, m_mla_kv_norm_g, m_mla_w_ukv, m_odd_w_out, m_ffn_w_in, m_ffn_conv_w, m_ffn_conv_b, m_ffn_w_out, v_norm_mix_g, v_norm_ffn_g, v_final_norm_g, v_even_w_in, v_hgrn_lb_logits, v_hgrn_norm_g, v_s5_a_re, v_s5_a_im, v_s5_log_dt, v_s5_b_re, v_s5_b_im, v_s5_c_re, v_s5_c_im, v_s5_d, v_s5_w_glu, v_s5_b_glu, v_even_w_out, v_odd_w_in, v_mla_q_norm_g, v_mla_w_uq, v_mla_kv_norm_g, v_mla_w_ukv, v_odd_w_out, v_ffn_w_in, v_ffn_conv_w, v_ffn_conv_b, v_ffn_w_out):
    given = dict(locals())
    w = {n: given[n] for n in W_NAMES}
    mom = {n: given["m_" + n] for n in W_NAMES}
    var = {n: given["v_" + n] for n in W_NAMES}
    T = x.shape[1]
    h0 = x[0]
    tgt = loss_target[0]
    pos = positions.reshape(T, 1)

    gathers = []
    for s in range(len(STAGES)):
        gathers.append(_gather_start(w, s, with_small=(s == 1), after=[g[4] for g in gathers[-1:]]))
    half = MLA_ROPE // 2
    kr0 = MLA_Q_RANK + MLA_KV_RANK
    freqs = ROPE_THETA ** (-jnp.arange(0, MLA_ROPE, 2, dtype=F32) / MLA_ROPE)
    freqs_q = jnp.tile(jnp.concatenate([freqs, freqs]), MLA_HEADS)[None, :]
    freqs_k = jnp.concatenate([freqs, jnp.zeros((LANE - half,), F32)])[None, :]

    sp_in = (s5_a_re[0], s5_a_im[0], s5_log_dt[0][:, None], s5_b_re[0].transpose(0, 2, 1), s5_b_im[0].transpose(0, 2, 1),
             hgrn_lb_logits)
    abr, abi, bbt_re, bbt_im, lb0 = _s5_params_fwd(*sp_in)
    a_re, a_im = abr.reshape(1, S5_WIDTH), abi.reshape(1, S5_WIDTH)
    bb_re, bb_im = _block_diag(bbt_re).astype(BF16), _block_diag(bbt_im).astype(BF16)
    c_re = _block_diag(s5_c_re[0].transpose(0, 2, 1)).astype(BF16)
    c_im_neg = _block_diag(-s5_c_im[0].transpose(0, 2, 1)).astype(BF16)
    u_cols = (4 * HGRN_DIM, S5_DIM)

    hn0, = _rows(_rms_fwd_fn, [h0], [norm_mix_g[0:1]], [(D_MODEL, BF16)], [], name="mix0_norm", deps=[gathers[-1][4]])
    full = _gather_finish(gathers[0], hn0, w, 0, False)
    w_ein_t = full["even_w_in", 0]
    proj = _mm(hn0, w_ein_t, tb=True, name="even_in")
    y_a, states = _hgrn_fwd(proj, lb0, hgrn_norm_g, name="hgrn_fwd")
    s_re, s_im = _s5_scan_fwd(proj, u_cols, bb_re, bb_im, a_re, a_im, name="s5_scan_fwd")
    more, full_small = _gather_finish(gathers[1], s_re, w, 1, True)
    w_glu, w_eout = more["s5_w_glu", 0], more["even_w_out", 0]
    qg, kvg, conv_w = full_small["mla_q_norm_g"], full_small["mla_kv_norm_g"], full_small["ffn_conv_w"]
    ys = _mm(s_im, c_im_neg, res=_mm(s_re, c_re, name="s5_y_re"), name="s5_y_im")
    z5, = _rows(_s5_act_fn, [ys, (proj,) + u_cols], [s5_d], [(S5_DIM, F32)], [], name="s5_act")
    gl = _mm(z5, w_glu, name="s5_glu")
    mixin, = _rows(_s5_mix_fn, [y_a, z5, gl], [s5_b_glu], [(D_MODEL, BF16)], [], name="s5_mix")
    h1 = _mm(mixin, w_eout, res=h0, name="even_out")
    full.update(_gather_finish(gathers[2], h1, w, 2, False))
    w_fin, w_fout = [full["ffn_w_in", 0]], [full["ffn_w_out", 0]]
    h2, ffn0_saved = _ffn_fwd(h1, norm_ffn_g[0:1], w_fin[0], conv_w[0], ffn_conv_b[0:1], w_fout[0], 0)

    full.update(_gather_finish(gathers[3], h2, w, 3, False))
    w_oin, w_ukv_t, w_oout = full["odd_w_in", 0], full["mla_w_ukv", 0], full["odd_w_out", 0]
    zpad = jnp.zeros((D_MODEL, LANE - half), BF16)
    w_oin_pad = jnp.concatenate([w_oin[:, :kr0], w_oin[:, kr0:kr0 + half], zpad, w_oin[:, kr0 + half:], zpad], axis=1)
    w_uq3 = full["mla_w_uq", 0].reshape(MLA_HEADS, MLA_QK, MLA_Q_RANK)
    w_uq_perm_t = jnp.concatenate([w_uq3[:, :MLA_NOPE].reshape(-1, MLA_Q_RANK),
                                   w_uq3[:, MLA_NOPE:].reshape(-1, MLA_Q_RANK)], axis=0)
    hn1, = _rows(_rms_fwd_fn, [h2], [norm_mix_g[1:2]], [(D_MODEL, BF16)], [], name="mix1_norm")
    proj_o = _mm(hn1, w_oin_pad, name="odd_in")
    qn, kvn, ko1, ko2 = _rows(_mla_prep_fn, [proj_o, pos], [qg, kvg, freqs_k],
                              [(MLA_Q_RANK, BF16), (MLA_KV_RANK, BF16), (LANE, F32), (LANE, F32)], [], name="mla_prep")
    q_all = _mm(qn, w_uq_perm_t, tb=True, name="mla_uq")
    kv = _mm(kvn, w_ukv_t, tb=True, out_dtype=BF16, name="mla_ukv")
    nope_w = MLA_HEADS * MLA_NOPE
    q_rope = _rope_q_fwd(q_all, pos, freqs_q, name="mla_rope_q")
    k_rope = jnp.concatenate([ko1[:, :half], ko2[:, :half]], axis=1)
    o, lse = _attn_fwd(q_all, q_rope, kv, k_rope, name="attn_fwd")
    h3 = _mm(o, w_oout, res=h2, name="odd_out")
    full.update(_gather_finish(gathers[4], h3, w, 4, False))
    w_fin.append(full["ffn_w_in", 1])
    w_fout.append(full["ffn_w_out", 1])
    h4, ffn1_saved = _ffn_fwd(h3, norm_ffn_g[1:2], w_fin[1], conv_w[1], ffn_conv_b[1:2], w_fout[1], 1)

    dh4, d_final_g, loss_cols = _rows(_loss_fn, [h4, tgt], [final_norm_g[None, :]], [(D_MODEL, F32)],
                                      [(1, D_MODEL), (1, D_MODEL)], name="loss_head")
    loss = lax.psum(jnp.sum(loss_cols), ("x", "y", "c"))

    dh3, gf1 = _ffn_bwd(h3, dh4, ffn1_saved, norm_ffn_g[1:2], w_fin[1], conv_w[1], ffn_conv_b[1:2], w_fout[1], 1)
    scatters = {4: _scatter_start({("ffn_w_in", 1): gf1["w_in"], ("ffn_w_out", 1): gf1["w_out"]}, 4)}
    do = _mm(dh3, w_oout, tb=True, deps=[scatters[4][4]], name="odd_out_dx")
    d_w_oout = _mm(o, dh3, ta=True, out_dtype=BF16, name="odd_out_dw")
    dq_nope, dq_rope, dkv, dk_rope = _attn_bwd(q_all, q_rope, kv, k_rope, o, lse, do, name="attn_bwd")
    lane_pad = ((0, 0), (0, LANE - half))
    dko1, dko2 = jnp.pad(dk_rope[:, :half], lane_pad), jnp.pad(dk_rope[:, half:], lane_pad)
    dq_all = _rope_q_bwd(dq_nope, dq_rope, pos, freqs_q, name="mla_rope_q_bwd")
    d_w_uq_perm_t = _mm(dq_all, qn, ta=True, out_dtype=BF16, name="mla_uq_dw")
    dqn = _mm(dq_all, w_uq_perm_t, name="mla_uq_dx")
    d_w_ukv_t = _mm(dkv, kvn, ta=True, out_dtype=BF16, name="mla_ukv_dw")
    dkvn = _mm(dkv, w_ukv_t, name="mla_ukv_dx")
    dproj_o, d_qg, d_kvg = _rows(_mla_prep_bwd_fn, [proj_o, pos, dqn, dkvn, dko1, dko2], [qg, kvg, freqs_k],
                                 [(ODD_IN_PAD, BF16)], [(1, MLA_Q_RANK), (1, MLA_KV_RANK)], name="mla_prep_bwd")
    d_w_oin_pad = _mm(hn1, dproj_o, ta=True, out_dtype=BF16, name="odd_in_dw")
    dhn1 = _mm(dproj_o, w_oin_pad, tb=True, name="odd_in_dx")
    dh2, d_mix_g1 = _rows(_rms_bwd_fn, [h2, dhn1, dh3], [norm_mix_g[1:2]], [(D_MODEL, F32)], [(1, D_MODEL)], name="mix1_dnorm")
    d_w_oin = jnp.concatenate([d_w_oin_pad[:, :kr0 + half], d_w_oin_pad[:, kr0 + LANE:kr0 + LANE + half]], axis=1)
    d_w_uq_t = jnp.concatenate([d_w_uq_perm_t[:nope_w].reshape(MLA_HEADS, MLA_NOPE, MLA_Q_RANK),
                                d_w_uq_perm_t[nope_w:].reshape(MLA_HEADS, MLA_ROPE, MLA_Q_RANK)], axis=1).reshape(-1, MLA_Q_RANK)
    scatters[3] = _scatter_start({("odd_w_in", 0): d_w_oin, ("mla_w_uq", 0): d_w_uq_t, ("mla_w_ukv", 0): d_w_ukv_t,
                                  ("odd_w_out", 0): d_w_oout}, 3)

    dh1, gf0 = _ffn_bwd(h1, dh2, ffn0_saved, norm_ffn_g[0:1], w_fin[0], conv_w[0], ffn_conv_b[0:1], w_fout[0], 0,
                        deps=[scatters[3][4]])
    scatters[2] = _scatter_start({("ffn_w_in", 0): gf0["w_in"], ("ffn_w_out", 0): gf0["w_out"]}, 2)
    dmix = _mm(dh1, w_eout, tb=True, deps=[scatters[2][4]], name="even_out_dx")
    d_w_eout = _mm(mixin, dh1, ta=True, out_dtype=BF16, name="even_out_dw")
    dq, df, di, dg, d_lb0, d_hgrn_g = _hgrn_bwd(proj, lb0, hgrn_norm_g, states, dmix, name="hgrn_bwd")
    dz1, dgl, d_b_glu = _rows(_s5_gate_bwd_fn, [z5, gl, (dmix, HGRN_DIM, S5_DIM)], [s5_b_glu],
                              [(S5_DIM, F32), (S5_DIM, BF16)], [(1, S5_DIM)], name="s5_gate_bwd")
    dz2 = _mm(dgl, w_glu, tb=True, name="s5_glu_dx")
    d_w_glu = _mm(z5, dgl, ta=True, out_dtype=BF16, name="s5_glu_dw")
    dys, du1, d_s5_d = _rows(_s5_act_bwd_fn, [ys, (proj,) + u_cols, dz1, dz2], [s5_d],
                             [(S5_DIM, BF16), (S5_DIM, F32)], [(1, S5_DIM)], name="s5_act_bwd")
    d_c_re = _mm(s_re, dys, ta=True, name="s5_dc_re")
    d_c_im_neg = _mm(s_im, dys, ta=True, name="s5_dc_im")
    lam_re, lam_im, d_ar, d_ai = _s5_scan_bwd(dys, c_re, c_im_neg, s_re, s_im, a_re, a_im, name="s5_scan_bwd")
    du2 = _mm(lam_im, bb_im, tb=True, res=_mm(lam_re, bb_re, tb=True, name="s5_du_re"), name="s5_du_im")
    d_bb_re = _mm(proj, lam_re, ta=True, a_cols=u_cols, name="s5_dbb_re")
    d_bb_im = _mm(proj, lam_im, ta=True, a_cols=u_cols, name="s5_dbb_im")
    sp_g = _s5_params_bwd(*sp_in, d_ar.sum(0).reshape(S5_GROUPS, S5_STATE), d_ai.sum(0).reshape(S5_GROUPS, S5_STATE),
                          _diag_blocks(d_bb_re, S5_GROUP, S5_STATE), _diag_blocks(d_bb_im, S5_GROUP, S5_STATE), d_lb0)
    d_a_re, d_a_im, d_log_dt, d_bt_re, d_bt_im, d_lb_logits = sp_g
    g_small = dict(mla_q_norm_g=d_qg, mla_kv_norm_g=d_kvg, ffn_conv_w=jnp.stack([gf0["conv_w"], gf1["conv_w"]]))
    g_repl = dict(
        norm_ffn_g=jnp.concatenate([gf0["g"], gf1["g"]]),
        final_norm_g=d_final_g[0], hgrn_lb_logits=d_lb_logits, hgrn_norm_g=d_hgrn_g,
        s5_a_re=d_a_re[None], s5_a_im=d_a_im[None], s5_log_dt=d_log_dt[:, 0][None],
        s5_b_re=d_bt_re.transpose(0, 2, 1)[None], s5_b_im=d_bt_im.transpose(0, 2, 1)[None],
        s5_c_re=_diag_blocks(d_c_re, S5_STATE, S5_GROUP).transpose(0, 2, 1)[None],
        s5_c_im=-_diag_blocks(d_c_im_neg, S5_STATE, S5_GROUP).transpose(0, 2, 1)[None],
        s5_d=d_s5_d, s5_b_glu=d_b_glu, ffn_conv_b=jnp.concatenate([gf0["conv_b"], gf1["conv_b"]]))
    scatters[1] = _scatter_start({("s5_w_glu", 0): d_w_glu, ("even_w_out", 0): d_w_eout}, 1,
                                 extra=[_small_sharded_pack(g_small, w), _replicated_pack(g_repl, REPLICATED_EARLY)])
    dproj, = _rows(_dproj_fn, [dq, df, di, dg, du1, du2], [], [(EVEN_IN, BF16)], [], name="even_dproj", deps=[scatters[1][4]])
    d_w_ein_t = _mm(dproj, hn0, ta=True, out_dtype=BF16, name="even_in_dw")
    dhn0 = _mm(dproj, w_ein_t, name="even_in_dx")
    grad_x, d_mix_g0 = _rows(_rms_bwd_fn, [h0, dhn0, dh1], [norm_mix_g[0:1]], [(D_MODEL, F32)], [(1, D_MODEL)], name="mix0_dnorm")
    g_repl["norm_mix_g"] = jnp.concatenate([d_mix_g0, d_mix_g1])
    scatters[0] = _scatter_start({("even_w_in", 0): d_w_ein_t}, 0, extra=[_replicated_pack(g_repl, REPLICATED_LATE)])

    delta, new_m, new_v = {}, {}, {}
    updated, partial = {}, {}
    after = scatters[0][4]
    for stage in (4, 3, 2, 1, 0):
        partial[stage] = _scatter_finish(scatters[stage], after, stage)
        for (n, l), slabs in zip(STAGES[stage], partial[stage]):
            updated[n] = _adamw(_unit_block(w, (n, l)), slabs, _unit_block(mom, (n, l)), _unit_block(var, (n, l)),
                                name=f"adamw_{n}_{l}", layer=l, n_layers=w[n].shape[0], into=updated.get(n))
        after = updated[STAGES[stage][-1][0]][0]
    grads = {}
    for n in BIG_NAMES:
        outs = [o.transpose(0, 2, 1) for o in updated[n]] if UNIT_TRANSPOSED[n, 0] else updated[n]
        grads[n], delta[n], new_m[n], new_v[n] = outs
    for names, slabs, tag in ((REPLICATED_EARLY, partial[1][-1], "repl"), (REPLICATED_LATE, partial[0][-1], "late"),
                              (SMALL_SHARDED_NAMES, partial[1][-2], "small")):
        packs = [_pack([t[n].reshape(-1) for n in names], LANE, SUBLANE) for t in (w, mom, var)]
        outs = _adamw(packs[0], slabs, packs[1], packs[2], name=f"adamw_{tag}")
        shapes = [w[n].shape for n in names]
        for dst, o_ in zip((grads, delta, new_m, new_v), outs):
            dst.update(zip(names, _unpack(o_.reshape(-1), shapes)))

    return (loss, grad_x[None], *[grads[n] for n in W_NAMES], *[delta[n] for n in W_NAMES],
            *[new_m[n] for n in W_NAMES], *[new_v[n] for n in W_NAMES])
```

```python
import functools
import math

import numpy as np
import jax
import jax.numpy as jnp
from jax import lax
from jax.experimental import pallas as pl
from jax.experimental.pallas import tpu as pltpu

F32 = jnp.float32
BF16 = jnp.bfloat16
_MXU_DTYPE = jnp.bfloat16

D_MODEL = 1024
HGRN_DIM = 512
HGRN_HEAD_DIM = 128
HGRN_HEADS = 4
HGRN_CHUNK = 64
S5_DIM = 512
S5_GROUPS = 32
S5_GROUP = 16
S5_STATE = 64
S5_WIDTH = S5_GROUPS * S5_STATE
EVEN_IN = 4 * HGRN_DIM + S5_DIM
MLA_HEADS = 8
MLA_Q_RANK = 384
MLA_KV_RANK = 256
MLA_NOPE = 128
MLA_ROPE = 64
MLA_V = 128
MLA_QK = MLA_NOPE + MLA_ROPE
ODD_IN = MLA_Q_RANK + MLA_KV_RANK + MLA_ROPE
ODD_IN_PAD = MLA_Q_RANK + MLA_KV_RANK + 2 * 128
ROPE_THETA = 10000.0
D_FF = 2816
EPS = 1e-6
ADAM_LR = 0.001
ADAM_B1 = 0.9
ADAM_B2 = 0.999
ADAM_EPS = 1e-08
ADAM_WD = 0.01
ADAM_STEP = 10

N_DEV = 8
LANE = 128
SUBLANE = 8
VMEM_LIMIT_BYTES = 56 * 1024 * 1024
MESH = pl.DeviceIdType.MESH


def _cparams(sem=None):
    return pltpu.CompilerParams(dimension_semantics=sem, vmem_limit_bytes=VMEM_LIMIT_BYTES)


def _pick(n, cands):
    for c in cands:
        if n % c == 0:
            return c
    raise ValueError(f"no tile for {n} in {cands}")


def _sigmoid(x):
    return 0.5 * jnp.tanh(0.5 * x) + 0.5


def _silu(x):
    return x * _sigmoid(x)


def _gelu(x):
    return 0.5 * x * (1.0 + jnp.tanh(math.sqrt(2.0 / math.pi) * (x + 0.044715 * (x * x * x))))


def _rms(x, g):
    return x * lax.rsqrt(jnp.mean(x * x, axis=-1, keepdims=True) + EPS) * g


def _mxu(a, b, ca, cb):
    return lax.dot_general(a.astype(_MXU_DTYPE), b.astype(_MXU_DTYPE), (((ca,), (cb,)), ((), ())),
                           preferred_element_type=F32)


@functools.partial(jax.custom_vjp, nondiff_argnums=(2, 3))
def _mxu_ad(a, b, ca, cb):
    return _mxu(a, b, ca, cb)


def _mxu_ad_fwd(a, b, ca, cb):
    return _mxu(a, b, ca, cb), (a, b)


def _mxu_ad_bwd(ca, cb, saved, g):
    a, b = saved
    fa, fb = 1 - ca, 1 - cb
    da = _mxu(g, b, 1, fb) if ca == 1 else _mxu(b, g, fb, 1)
    db = _mxu(a, g, fa, 0) if cb == 0 else _mxu(g, a, 0, fa)
    return da, db


_mxu_ad.defvjp(_mxu_ad_fwd, _mxu_ad_bwd)


def _tri(n):
    row = lax.broadcasted_iota(jnp.int32, (n, n), 0)
    col = lax.broadcasted_iota(jnp.int32, (n, n), 1)
    return col <= row


def _cumsum_rows(x, reverse=False):
    n = x.shape[0]
    rowi = lax.broadcasted_iota(jnp.int32, x.shape, 0)
    s = 1
    while s < n:
        if reverse:
            x = x + jnp.where(rowi < n - s, pltpu.roll(x, n - s, 0), 0.0)
        else:
            x = x + jnp.where(rowi >= s, pltpu.roll(x, s, 0), 0.0)
        s *= 2
    return x


@jax.custom_vjp
def _cumsum_rows_ad(x):
    return _cumsum_rows(x)


def _cumsum_rows_ad_fwd(x):
    return _cumsum_rows(x), None


def _cumsum_rows_ad_bwd(_, g):
    return (_cumsum_rows(g, reverse=True),)


_cumsum_rows_ad.defvjp(_cumsum_rows_ad_fwd, _cumsum_rows_ad_bwd)


MM_VMEM_BUDGET = 36 * 1024 * 1024
MM_MAX_TILE = 1408


def _lane_divisors(n, cap, offs=()):
    return [d for d in range(min(n, cap) // LANE * LANE, 0, -LANE) if n % d == 0 and all(o % d == 0 for o in offs)]


def _mm_tiles(M, N, K, sa, sb, so, has_res, m_offs, n_offs, k_offs):
    best = None
    for tm in _lane_divisors(M, MM_MAX_TILE, m_offs):
        for tn in _lane_divisors(N, MM_MAX_TILE, n_offs):
            for tk in _lane_divisors(K, K, k_offs):
                nk = K // tk
                vmem = 2 * (tm * tk * sa + tk * tn * sb + tm * tn * so + tm * tn * 4 * has_res) + (tm * tn * 4 if nk > 1 else 0)
                if vmem <= MM_VMEM_BUDGET:
                    key = (-nk, tm * tn, tn)
                    if best is None or key > best[0]:
                        best = (key, tm, tn, tk)
                    break
    return best[1:]


def _mm(a, b, *, ta=False, tb=False, res=None, out_dtype=F32, a_cols=None, b_rows=None, deps=(), out_rows=None, into=None,
        name):
    a_minor = a.shape[1] if a_cols is None else a_cols[1]
    b_major = b.shape[0] if b_rows is None else b_rows[1]
    K, M = (a.shape[0], a_minor) if ta else (a_minor, a.shape[0])
    N = b_major if tb else b.shape[1]
    assert (b.shape[1] if tb else b_major) == K, (a.shape, b.shape, ta, tb)
    assert b_rows is None or not tb
    a_off = 0 if a_cols is None else a_cols[0]
    b_off = 0 if b_rows is None else b_rows[0]
    has_res = res is not None
    o_off, o_total = (0, M) if out_rows is None else out_rows
    tm, tn, tk = _mm_tiles(M, N, K, a.dtype.itemsize, b.dtype.itemsize, jnp.dtype(out_dtype).itemsize, has_res,
                           ((a_off,) if ta else ()) + (o_off,), (), ((a_off,) if not ta else ()) + (b_off,))
    nk = K // tk
    am, ak = (a_off // tm, 0) if ta else (0, a_off // tk)
    bk, om = b_off // tk, o_off // tm
    a_spec = pl.BlockSpec((tk, tm), lambda i, j, k: (k, i + am)) if ta else pl.BlockSpec((tm, tk), lambda i, j, k: (i, k + ak))
    b_spec = pl.BlockSpec((tn, tk), lambda i, j, k: (j, k)) if tb else pl.BlockSpec((tk, tn), lambda i, j, k: (k + bk, j))
    o_spec = pl.BlockSpec((tm, tn), lambda i, j, k: (i, j))
    ca, cb = (0 if ta else 1), (1 if tb else 0)
    extra = list(deps) + ([into] if into is not None else [])

    n_fixed = 2 + has_res + len(extra)

    def body(*refs):
        a_ref, b_ref = refs[0], refs[1]
        res_ref = refs[2] if has_res else None
        o_ref = refs[n_fixed]
        part = _mxu(a_ref[...], b_ref[...], ca, cb)
        if nk == 1:
            o_ref[...] = (part + res_ref[...] if has_res else part).astype(out_dtype)
            return
        acc_ref = refs[n_fixed + 1]
        k = pl.program_id(2)

        @pl.when(k == 0)
        def _():
            acc_ref[...] = part

        @pl.when(k > 0)
        def _():
            acc_ref[...] += part

        @pl.when(k == nk - 1)
        def _():
            o_ref[...] = (acc_ref[...] + res_ref[...] if has_res else acc_ref[...]).astype(out_dtype)

    ins = [a, b] + ([res] if has_res else []) + extra
    in_specs = [a_spec, b_spec] + ([o_spec] if has_res else []) + [pl.BlockSpec(memory_space=pl.ANY)] * len(extra)
    return pl.pallas_call(
        body, name=name, grid=(M // tm, N // tn, nk),
        in_specs=in_specs, out_specs=pl.BlockSpec((tm, tn), lambda i, j, k: (i + om, j)),
        out_shape=jax.ShapeDtypeStruct((o_total, N), out_dtype),
        scratch_shapes=[pltpu.VMEM((tm, tn), F32)] if nk > 1 else [],
        input_output_aliases={} if into is None else {len(ins) - 1: 0},
        compiler_params=_cparams(("parallel", "parallel", "arbitrary")),
    )(*ins)


def _rows(fn, row_ins, const_ins, row_outs, acc_outs, *, name, tm=512, deps=()):
    norm = [(r, 0, r.shape[1]) if not isinstance(r, tuple) else r for r in row_ins]
    T = norm[0][0].shape[0]
    tm = min(tm, T)
    nr, nc, no, na = len(norm), len(const_ins), len(row_outs), len(acc_outs)
    first_out = nr + nc + len(deps)

    def body(*refs):
        i = pl.program_id(0)
        vals = [r[...] for r in refs[:nr + nc]]
        outs = fn(*vals)
        for o_ref, o in zip(refs[first_out:first_out + no], outs[:no]):
            o_ref[...] = o.astype(o_ref.dtype)
        for a_ref, o in zip(refs[first_out + no:], outs[no:]):
            @pl.when(i == 0)
            def _(a_ref=a_ref, o=o):
                a_ref[...] = o

            @pl.when(i > 0)
            def _(a_ref=a_ref, o=o):
                a_ref[...] += o

    in_specs = []
    for arr, off, w in norm:
        assert off % w == 0, (off, w)
        in_specs.append(pl.BlockSpec((tm, w), lambda i, b=off // w: (i, b)))
    for c in const_ins:
        in_specs.append(pl.BlockSpec(c.shape, lambda i: (0, 0)))
    in_specs += [pl.BlockSpec(memory_space=pl.ANY)] * len(deps)
    out_specs = [pl.BlockSpec((tm, w), lambda i: (i, 0)) for w, _ in row_outs]
    out_specs += [pl.BlockSpec(s, lambda i: (0, 0)) for s in acc_outs]
    out_shape = [jax.ShapeDtypeStruct((T, w), dt) for w, dt in row_outs]
    out_shape += [jax.ShapeDtypeStruct(s, F32) for s in acc_outs]
    return pl.pallas_call(
        body, name=name, grid=(T // tm,), in_specs=in_specs, out_specs=out_specs, out_shape=out_shape,
        compiler_params=_cparams(("arbitrary",)),
    )(*[n[0] for n in norm], *const_ins, *deps)


FFN_COL_TILE = 2 * LANE
FFN_ROW_CHUNK = 512


FFN_HALO = 2 * SUBLANE


def _shift_down(ext, s, rows):
    return pltpu.roll(ext, s, 0)[FFN_HALO:FFN_HALO + rows]


def _shift_up(ext, s, rows):
    return pltpu.roll(ext, rows + FFN_HALO - s, 0)[:rows]


def _ffn_chunks(T):
    r = min(FFN_ROW_CHUNK, T)
    return r, T // r


def _ext_before(ref, c, r):
    if c == 0:
        return jnp.concatenate([jnp.zeros((FFN_HALO, ref.shape[1]), F32), ref[0:r, :].astype(F32)], axis=0)
    return ref[c * r - FFN_HALO:(c + 1) * r, :].astype(F32)


def _ext_after(ref, c, r, nch):
    if c == nch - 1:
        return jnp.concatenate([ref[c * r:(c + 1) * r, :].astype(F32), jnp.zeros((FFN_HALO, ref.shape[1]), F32)], axis=0)
    return ref[c * r:(c + 1) * r + FFN_HALO, :].astype(F32)


def _ffn_mid_fwd(au, conv_w, conv_b, *, name):
    T = au.shape[0]
    tc = FFN_COL_TILE
    ncol = D_FF // tc
    r, nch = _ffn_chunks(T)

    def body(a_ref, u_ref, w_ref, b_ref, z_ref):
        w0, w1, w2, bias = w_ref[0:1, :], w_ref[1:2, :], w_ref[2:3, :], b_ref[...]
        for c in range(nch):
            ext = _ext_before(a_ref, c, r)
            pre = w0 * _shift_down(ext, 2, r) + w1 * _shift_down(ext, 1, r) + w2 * ext[FFN_HALO:] + bias
            z_ref[c * r:(c + 1) * r, :] = (_silu(pre) * u_ref[c * r:(c + 1) * r, :].astype(F32)).astype(z_ref.dtype)

    return pl.pallas_call(
        body, name=name, grid=(ncol,),
        in_specs=[pl.BlockSpec((T, tc), lambda j: (0, j)), pl.BlockSpec((T, tc), lambda j: (0, j + ncol)),
                  pl.BlockSpec((3, tc), lambda j: (0, j)), pl.BlockSpec((1, tc), lambda j: (0, j))],
        out_specs=pl.BlockSpec((T, tc), lambda j: (0, j)),
        out_shape=jax.ShapeDtypeStruct((T, D_FF), BF16),
        compiler_params=_cparams(("parallel",)),
    )(au, au, conv_w, conv_b)


def _ffn_mid_bwd(au, dz, conv_w, conv_b, *, name):
    T = au.shape[0]
    tc = FFN_COL_TILE
    ncol = D_FF // tc
    r, nch = _ffn_chunks(T)

    def body(a_ref, u_ref, dz_ref, w_ref, b_ref, da_ref, du_ref, dw_ref, db_ref, dpre_ref):
        w0, w1, w2, bias = w_ref[0:1, :], w_ref[1:2, :], w_ref[2:3, :], b_ref[...]
        dw0 = jnp.zeros((1, tc), F32)
        dw1 = jnp.zeros((1, tc), F32)
        dw2 = jnp.zeros((1, tc), F32)
        db = jnp.zeros((1, tc), F32)
        for c in range(nch):
            rows = slice(c * r, (c + 1) * r)
            ext = _ext_before(a_ref, c, r)
            a2, a1, a0 = _shift_down(ext, 2, r), _shift_down(ext, 1, r), ext[FFN_HALO:]
            pre = w0 * a2 + w1 * a1 + w2 * a0 + bias
            sg = _sigmoid(pre)
            act = pre * sg
            dzc = dz_ref[rows, :].astype(F32)
            du_ref[rows, :] = (dzc * act).astype(du_ref.dtype)
            dpre = dzc * u_ref[rows, :].astype(F32) * (sg * (1.0 + pre * (1.0 - sg)))
            dpre_ref[rows, :] = dpre
            dw0 += jnp.sum(dpre * a2, axis=0, keepdims=True)
            dw1 += jnp.sum(dpre * a1, axis=0, keepdims=True)
            dw2 += jnp.sum(dpre * a0, axis=0, keepdims=True)
            db += jnp.sum(dpre, axis=0, keepdims=True)
        for c in range(nch):
            ext = _ext_after(dpre_ref, c, r, nch)
            da = w0 * _shift_up(ext, 2, r) + w1 * _shift_up(ext, 1, r) + w2 * ext[:r]
            da_ref[c * r:(c + 1) * r, :] = da.astype(da_ref.dtype)
        dw_ref[0:1, :] = dw0
        dw_ref[1:2, :] = dw1
        dw_ref[2:3, :] = dw2
        db_ref[...] = db

    col = lambda j: (0, j)
    return pl.pallas_call(
        body, name=name, grid=(ncol,),
        in_specs=[pl.BlockSpec((T, tc), col), pl.BlockSpec((T, tc), lambda j: (0, j + ncol)), pl.BlockSpec((T, tc), col),
                  pl.BlockSpec((3, tc), col), pl.BlockSpec((1, tc), col)],
        out_specs=[pl.BlockSpec((T, tc), col), pl.BlockSpec((T, tc), col), pl.BlockSpec((3, tc), col), pl.BlockSpec((1, tc), col)],
        out_shape=[jax.ShapeDtypeStruct((T, D_FF), BF16), jax.ShapeDtypeStruct((T, D_FF), BF16),
                   jax.ShapeDtypeStruct((3, D_FF), F32), jax.ShapeDtypeStruct((1, D_FF), F32)],
        scratch_shapes=[pltpu.VMEM((T, tc), F32)],
        compiler_params=_cparams(("parallel",)),
    )(au, au, dz, conv_w, conv_b)


HGRN_BLOCK = 256


def _hgrn_chunk(dot, cumsum, q, f, i, g, lb, ng, st):
    C = q.shape[0]
    forget = lb + (1.0 - lb) * _sigmoid(f)
    k = 1.0 - forget
    b = cumsum(jnp.log(forget))
    b_last = b[C - 1:C, :]
    qd = q * jnp.exp(b)
    kd = k * jnp.exp(-b)
    att = jnp.where(_tri(C), dot(qd, kd, 1, 1), 0.0)
    o = dot(att, i, 1, 0) + dot(qd, st, 1, 1)
    st_new = st * jnp.exp(b_last) + dot(i, k * jnp.exp(b_last - b), 0, 0)
    on = o * lax.rsqrt(jnp.mean(o * o, axis=-1, keepdims=True) + EPS) * ng
    return on * _silu(g), st_new


def _hgrn_specs(T, rev):
    tb = min(HGRN_BLOCK, T)
    nb = T // tb
    blk = (lambda n: nb - 1 - n) if rev else (lambda n: n)
    hd = HGRN_HEAD_DIM
    proj_specs = [pl.BlockSpec((tb, HGRN_DIM), lambda n, k=k: (blk(n), k)) for k in range(4)]
    vec_spec = pl.BlockSpec((1, HGRN_DIM), lambda n: (0, 0))
    tok_spec = pl.BlockSpec((tb, HGRN_DIM), lambda n: (blk(n), 0))
    st_spec = pl.BlockSpec((HGRN_HEADS, tb // HGRN_CHUNK, hd, hd), lambda n: (0, blk(n), 0, 0))
    return tb, nb, proj_specs, vec_spec, tok_spec, st_spec


def _head_cols(h):
    return slice(h * HGRN_HEAD_DIM, (h + 1) * HGRN_HEAD_DIM)


def _hgrn_fwd(proj, lb, ng, *, name):
    T = proj.shape[0]
    tb, nb, proj_specs, vec_spec, tok_spec, st_spec = _hgrn_specs(T, False)
    nsub = tb // HGRN_CHUNK
    hd = HGRN_HEAD_DIM

    def body(q_ref, f_ref, i_ref, g_ref, lb_ref, ng_ref, y_ref, sts_ref, st_ref):
        @pl.when(pl.program_id(0) == 0)
        def _():
            st_ref[...] = jnp.zeros_like(st_ref)

        st = [st_ref[h] for h in range(HGRN_HEADS)]
        for s in range(nsub):
            rows = slice(s * HGRN_CHUNK, (s + 1) * HGRN_CHUNK)
            for h in range(HGRN_HEADS):
                cols = _head_cols(h)
                sts_ref[h, s] = st[h]
                y, st[h] = _hgrn_chunk(_mxu, _cumsum_rows, q_ref[rows, cols], f_ref[rows, cols], i_ref[rows, cols],
                                       g_ref[rows, cols], lb_ref[:, cols], ng_ref[:, cols], st[h])
                y_ref[rows, cols] = y
        for h in range(HGRN_HEADS):
            st_ref[h] = st[h]

    return pl.pallas_call(
        body, name=name, grid=(nb,),
        in_specs=proj_specs + [vec_spec, vec_spec], out_specs=[tok_spec, st_spec],
        out_shape=[jax.ShapeDtypeStruct((T, HGRN_DIM), F32),
                   jax.ShapeDtypeStruct((HGRN_HEADS, T // HGRN_CHUNK, hd, hd), F32)],
        scratch_shapes=[pltpu.VMEM((HGRN_HEADS, hd, hd), F32)],
        compiler_params=_cparams(("arbitrary",)),
    )(proj, proj, proj, proj, lb, ng)


def _hgrn_bwd(proj, lb, ng, states, dmix, *, name):
    T = proj.shape[0]
    tb, nb, proj_specs, vec_spec, tok_spec, st_spec = _hgrn_specs(T, True)
    nsub = tb // HGRN_CHUNK
    hd = HGRN_HEAD_DIM
    chunk = functools.partial(_hgrn_chunk, _mxu_ad, _cumsum_rows_ad)

    def body(q_ref, f_ref, i_ref, g_ref, lb_ref, ng_ref, sts_ref, dy_ref,
             dq_ref, df_ref, di_ref, dg_ref, dlb_ref, dng_ref, dst_ref):
        @pl.when(pl.program_id(0) == 0)
        def _():
            dst_ref[...] = jnp.zeros_like(dst_ref)
            dlb_ref[...] = jnp.zeros_like(dlb_ref)
            dng_ref[...] = jnp.zeros_like(dng_ref)

        dst = [dst_ref[h] for h in range(HGRN_HEADS)]
        dlb = [jnp.zeros((1, hd), F32)] * HGRN_HEADS
        dng = [jnp.zeros((1, hd), F32)] * HGRN_HEADS
        for s in reversed(range(nsub)):
            rows = slice(s * HGRN_CHUNK, (s + 1) * HGRN_CHUNK)
            for h in range(HGRN_HEADS):
                cols = _head_cols(h)
                _, vjp = jax.vjp(chunk, q_ref[rows, cols], f_ref[rows, cols], i_ref[rows, cols], g_ref[rows, cols],
                                 lb_ref[:, cols], ng_ref[:, cols], sts_ref[h, s])
                dq, df, di, dg, dlb_s, dng_s, dst[h] = vjp((dy_ref[rows, cols], dst[h]))
                dq_ref[rows, cols] = dq
                df_ref[rows, cols] = df
                di_ref[rows, cols] = di
                dg_ref[rows, cols] = dg
                dlb[h] = dlb[h] + dlb_s
                dng[h] = dng[h] + dng_s
        for h in range(HGRN_HEADS):
            dst_ref[h] = dst[h]
            dlb_ref[:, _head_cols(h)] += dlb[h]
            dng_ref[:, _head_cols(h)] += dng[h]

    tok_out = jax.ShapeDtypeStruct((T, HGRN_DIM), F32)
    vec_out = jax.ShapeDtypeStruct((1, HGRN_DIM), F32)
    return pl.pallas_call(
        body, name=name, grid=(nb,),
        in_specs=proj_specs + [vec_spec, vec_spec, st_spec, tok_spec],
        out_specs=[tok_spec] * 4 + [vec_spec, vec_spec],
        out_shape=[tok_out] * 4 + [vec_out, vec_out],
        scratch_shapes=[pltpu.VMEM((HGRN_HEADS, hd, hd), F32)],
        compiler_params=_cparams(("arbitrary",)),
    )(proj, proj, proj, proj, lb, ng, states, dmix)


S5_LANES = 512
S5_ROWS = 1024


def _cmul(ar, ai, br, bi):
    return ar * br - ai * bi, ar * bi + ai * br


def _power_table(ar, ai, exps):
    a2 = _cmul(ar, ai, ar, ai)
    a4 = _cmul(*a2, *a2)
    e = exps - 1
    pr = jnp.broadcast_to(ar, exps.shape)
    pi = jnp.broadcast_to(ai, exps.shape)
    for bit, (fr, fi) in enumerate(((ar, ai), a2, a4)):
        nr, ni = _cmul(pr, pi, fr, fi)
        on = ((e >> bit) & 1) == 1
        pr, pi = jnp.where(on, nr, pr), jnp.where(on, ni, pi)
    return pr, pi, a2, a4


def _s5_scan_fwd(x, x_cols, b_re, b_im, a_re, a_im, *, name):
    T = x.shape[0]
    w, tr = S5_LANES, min(S5_ROWS, T)
    ncol, nt = S5_WIDTH // w, T // tr
    assert x_cols[0] % x_cols[1] == 0

    def body(x_ref, br_ref, bi_ref, ar_ref, ai_ref, sr_ref, si_ref, carry_ref):
        @pl.when(pl.program_id(1) == 0)
        def _():
            carry_ref[...] = jnp.zeros_like(carry_ref)

        u = x_ref[...].astype(_MXU_DTYPE)
        sr_ref[...] = _mxu(u, br_ref[...], 1, 0)
        si_ref[...] = _mxu(u, bi_ref[...], 1, 0)
        ar, ai = ar_ref[...], ai_ref[...]
        rowi = lax.broadcasted_iota(jnp.int32, (SUBLANE, w), 0)
        pr, pi, a2, a4 = _power_table(ar, ai, rowi + 1)
        steps = [(s, jnp.where(rowi >= s, fr, 0.0), jnp.where(rowi >= s, fi, 0.0)) for s, (fr, fi) in ((1, (ar, ai)), (2, a2), (4, a4))]

        def tile(i, carry):
            cr, ci = carry
            rows = pl.ds(pl.multiple_of(i * SUBLANE, SUBLANE), SUBLANE)
            xr, xi = sr_ref[rows, :], si_ref[rows, :]
            for s, fr, fi in steps:
                zr, zi = pltpu.roll(xr, s, 0), pltpu.roll(xi, s, 0)
                xr, xi = xr + fr * zr - fi * zi, xi + fr * zi + fi * zr
            xr, xi = xr + pr * cr - pi * ci, xi + pr * ci + pi * cr
            sr_ref[rows, :] = xr
            si_ref[rows, :] = xi
            return xr[SUBLANE - 1:SUBLANE, :], xi[SUBLANE - 1:SUBLANE, :]

        cr, ci = lax.fori_loop(0, tr // SUBLANE, tile, (carry_ref[0:1, :], carry_ref[1:2, :]))
        carry_ref[0:1, :] = cr
        carry_ref[1:2, :] = ci

    out = jax.ShapeDtypeStruct((T, S5_WIDTH), F32)
    return pl.pallas_call(
        body, name=name, grid=(ncol, nt),
        in_specs=[pl.BlockSpec((tr, x_cols[1]), lambda j, t: (t, x_cols[0] // x_cols[1])),
                  pl.BlockSpec((S5_DIM, w), lambda j, t: (0, j)), pl.BlockSpec((S5_DIM, w), lambda j, t: (0, j)),
                  pl.BlockSpec((1, w), lambda j, t: (0, j)), pl.BlockSpec((1, w), lambda j, t: (0, j))],
        out_specs=[pl.BlockSpec((tr, w), lambda j, t: (t, j))] * 2,
        out_shape=[out, out],
        scratch_shapes=[pltpu.VMEM((2, w), F32)],
        compiler_params=_cparams(("parallel", "arbitrary")),
    )(x, b_re, b_im, a_re, a_im)


def _s5_scan_bwd(dy, c_re, c_im, s_re, s_im, a_re, a_im, *, name):
    T = dy.shape[0]
    w, tr = S5_LANES, min(S5_ROWS, T)
    ncol, nt = S5_WIDTH // w, T // tr
    ntile = tr // SUBLANE

    def body(dy_ref, cr_ref, ci_ref, sr_ref, si_ref, ar_ref, ai_ref, lr_ref, li_ref, dar_ref, dai_ref, carry_ref):
        @pl.when(pl.program_id(1) == 0)
        def _():
            carry_ref[...] = jnp.zeros_like(carry_ref)
            dar_ref[...] = jnp.zeros_like(dar_ref)
            dai_ref[...] = jnp.zeros_like(dai_ref)

        dyb = dy_ref[...].astype(_MXU_DTYPE)
        lr_ref[...] = _mxu(dyb, cr_ref[...], 1, 1)
        li_ref[...] = _mxu(dyb, ci_ref[...], 1, 1)
        ar, ai = ar_ref[...], -ai_ref[...]
        rowi = lax.broadcasted_iota(jnp.int32, (SUBLANE, w), 0)
        pr, pi, a2, a4 = _power_table(ar, ai, SUBLANE - rowi)
        last = rowi == SUBLANE - 1
        steps = [(s, jnp.where(rowi < SUBLANE - s, fr, 0.0), jnp.where(rowi < SUBLANE - s, fi, 0.0))
                 for s, (fr, fi) in ((1, (ar, ai)), (2, a2), (4, a4))]

        def tile(i, carry):
            cr, ci, dar, dai = carry
            rows = pl.ds(pl.multiple_of((ntile - 1 - i) * SUBLANE, SUBLANE), SUBLANE)
            xr, xi = lr_ref[rows, :], li_ref[rows, :]
            for s, fr, fi in steps:
                zr, zi = pltpu.roll(xr, SUBLANE - s, 0), pltpu.roll(xi, SUBLANE - s, 0)
                xr, xi = xr + fr * zr - fi * zi, xi + fr * zi + fi * zr
            xr, xi = xr + pr * cr - pi * ci, xi + pr * ci + pi * cr
            lr_ref[rows, :] = xr
            li_ref[rows, :] = xi
            nr = jnp.where(last, cr, pltpu.roll(xr, SUBLANE - 1, 0))
            ni = jnp.where(last, ci, pltpu.roll(xi, SUBLANE - 1, 0))
            sr, si = sr_ref[rows, :], si_ref[rows, :]
            return xr[0:1, :], xi[0:1, :], dar + nr * sr + ni * si, dai + ni * sr - nr * si

        cr, ci, dar, dai = lax.fori_loop(
            0, ntile, tile, (carry_ref[0:1, :], carry_ref[1:2, :], jnp.zeros((SUBLANE, w), F32), jnp.zeros((SUBLANE, w), F32)))
        carry_ref[0:1, :] = cr
        carry_ref[1:2, :] = ci
        dar_ref[...] += dar
        dai_ref[...] += dai

    tok = pl.BlockSpec((tr, w), lambda j, t: (nt - 1 - t, j))
    vec = pl.BlockSpec((1, w), lambda j, t: (0, j))
    acc = pl.BlockSpec((SUBLANE, w), lambda j, t: (0, j))
    out = jax.ShapeDtypeStruct((T, S5_WIDTH), F32)
    accs = jax.ShapeDtypeStruct((SUBLANE, S5_WIDTH), F32)
    return pl.pallas_call(
        body, name=name, grid=(ncol, nt),
        in_specs=[pl.BlockSpec((tr, S5_DIM), lambda j, t: (nt - 1 - t, 0)),
                  pl.BlockSpec((w, S5_DIM), lambda j, t: (j, 0)), pl.BlockSpec((w, S5_DIM), lambda j, t: (j, 0)),
                  tok, tok, vec, vec],
        out_specs=[tok, tok, acc, acc],
        out_shape=[out, out, accs, accs],
        scratch_shapes=[pltpu.VMEM((2, w), F32)],
        compiler_params=_cparams(("parallel", "arbitrary")),
    )(dy, c_re, c_im, s_re, s_im, a_re, a_im)


ATTN_BLOCK = 1024
_NEG = -1e30


def _qk_cat(nope, rope):
    return jnp.concatenate([nope.astype(_MXU_DTYPE), rope.astype(_MXU_DTYPE)], axis=1)


_QK_SCALE = MLA_QK ** -0.5
_LOG2E = math.log2(math.e)


def _attn_scores(q, k, diagonal):
    s = _mxu(q, k, 1, 1) * (_QK_SCALE * _LOG2E)
    if diagonal:
        s = jnp.where(_tri(s.shape[0]), s, _NEG)
    return s


def _attn_fwd(q_all, q_rope, kv, k_rope, *, name):
    T = q_all.shape[0]
    tq = min(ATTN_BLOCK, T)
    nq = T // tq

    def body(qn_ref, qr_ref, kn_ref, v_ref, kr_ref, o_ref, lse_ref):
        i = pl.program_id(1)
        q = _qk_cat(qn_ref[...], qr_ref[0])

        def step(j, carry, diagonal):
            m, l, acc = carry
            ks = pl.ds(pl.multiple_of(j * tq, tq), tq)
            s = _attn_scores(q, _qk_cat(kn_ref[ks, :], kr_ref[ks, :]), diagonal)
            m_new = jnp.maximum(m, jnp.max(s, axis=-1, keepdims=True))
            p = jnp.exp2(s - m_new)
            alpha = jnp.exp2(m - m_new)
            return m_new, alpha * l + jnp.sum(p, axis=-1, keepdims=True), alpha * acc + _mxu(p, v_ref[ks, :], 1, 0)

        init = (jnp.full((tq, 1), _NEG, F32), jnp.zeros((tq, 1), F32), jnp.zeros((tq, MLA_V), F32))
        below = lax.fori_loop(0, i, functools.partial(step, diagonal=False), init)
        m, l, acc = step(i, below, diagonal=True)
        o_ref[...] = acc / l
        lse_ref[0] = m + jnp.log2(l)

    return pl.pallas_call(
        body, name=name, grid=(MLA_HEADS, nq),
        in_specs=[pl.BlockSpec((tq, MLA_NOPE), lambda h, i: (i, h)), pl.BlockSpec((1, tq, MLA_ROPE), lambda h, i: (h, i, 0)),
                  pl.BlockSpec((T, MLA_NOPE), lambda h, i: (0, 2 * h)), pl.BlockSpec((T, MLA_V), lambda h, i: (0, 2 * h + 1)),
                  pl.BlockSpec((T, MLA_ROPE), lambda h, i: (0, 0))],
        out_specs=[pl.BlockSpec((tq, MLA_V), lambda h, i: (i, h)), pl.BlockSpec((1, tq, 1), lambda h, i: (h, i, 0))],
        out_shape=[jax.ShapeDtypeStruct((T, MLA_HEADS * MLA_V), F32), jax.ShapeDtypeStruct((MLA_HEADS, T, 1), F32)],
        compiler_params=_cparams(("arbitrary", "arbitrary")),
    )(q_all, q_rope, kv, kv, k_rope)


def _attn_bwd(q_all, q_rope, kv, k_rope, o, lse, do, *, name):
    T = q_all.shape[0]
    tk = min(ATTN_BLOCK, T)
    nk = T // tk

    def body(qn_ref, qr_ref, kv_ref, kr_ref, o_ref, lse_ref, do_ref, dqn_ref, dqr_ref, dkv_ref, dkr_ref, delta_ref):
        h, j = pl.program_id(0), pl.program_id(1)

        @pl.when(j == 0)
        def _():
            dqn_ref[...] = jnp.zeros_like(dqn_ref)
            dqr_ref[...] = jnp.zeros_like(dqr_ref)
            delta_ref[...] = jnp.sum(do_ref[...] * o_ref[...], axis=-1, keepdims=True)

        @pl.when((j == 0) & (h == 0))
        def _():
            dkr_ref[...] = jnp.zeros_like(dkr_ref)

        krows = pl.ds(pl.multiple_of(j * tk, tk), tk)
        k = _qk_cat(kv_ref[:, :MLA_NOPE], kr_ref[krows, :])
        v = kv_ref[:, MLA_NOPE:].astype(_MXU_DTYPE)

        def step(i, carry, diagonal):
            dk, dv = carry
            qs = pl.ds(pl.multiple_of(i * tk, tk), tk)
            q, dob = _qk_cat(qn_ref[qs, :], qr_ref[0, qs, :]), do_ref[qs, :].astype(_MXU_DTYPE)
            p = jnp.exp2(_attn_scores(q, k, diagonal) - lse_ref[0, qs, :])
            ds = p * (_mxu(dob, v, 1, 1) - delta_ref[qs, :]) * _QK_SCALE
            dq = _mxu(ds, k, 1, 0)
            dqn_ref[qs, :] += dq[:, :MLA_NOPE]
            dqr_ref[0, qs, :] += dq[:, MLA_NOPE:]
            return dk + _mxu(ds, q, 0, 0), dv + _mxu(p, dob, 0, 0)

        on_diagonal = step(j, (jnp.zeros((tk, MLA_QK), F32), jnp.zeros((tk, MLA_V), F32)), diagonal=True)
        dk, dv = lax.fori_loop(j + 1, nk, functools.partial(step, diagonal=False), on_diagonal)
        dkv_ref[:, :MLA_NOPE] = dk[:, :MLA_NOPE].astype(dkv_ref.dtype)
        dkv_ref[:, MLA_NOPE:] = dv.astype(dkv_ref.dtype)
        dkr_ref[krows, :] += dk[:, MLA_NOPE:]

    head_cols = pl.BlockSpec((T, MLA_NOPE), lambda h, j: (0, h))
    head_rope = pl.BlockSpec((1, T, MLA_ROPE), lambda h, j: (h, 0, 0))
    kv_spec = pl.BlockSpec((tk, MLA_NOPE + MLA_V), lambda h, j: (j, h))
    kr_spec = pl.BlockSpec((T, MLA_ROPE), lambda h, j: (0, 0))
    return pl.pallas_call(
        body, name=name, grid=(MLA_HEADS, nk),
        in_specs=[head_cols, head_rope, kv_spec, kr_spec, head_cols, pl.BlockSpec((1, T, 1), lambda h, j: (h, 0, 0)), head_cols],
        out_specs=[head_cols, head_rope, kv_spec, kr_spec],
        out_shape=[jax.ShapeDtypeStruct((T, MLA_HEADS * MLA_NOPE), F32), jax.ShapeDtypeStruct((MLA_HEADS, T, MLA_ROPE), F32),
                   jax.ShapeDtypeStruct((T, MLA_HEADS * (MLA_NOPE + MLA_V)), BF16), jax.ShapeDtypeStruct((T, MLA_ROPE), F32)],
        scratch_shapes=[pltpu.VMEM((T, 1), F32)],
        compiler_params=_cparams(("arbitrary", "arbitrary")),
    )(q_all, q_rope, kv, k_rope, o, lse, do)


def _s5_discretize(a_re, a_im, log_dt, bt_re, bt_im, lb_logits):
    dt = jnp.exp(log_dt)
    mag = jnp.exp(a_re * dt)
    abr, abi = mag * jnp.cos(a_im * dt), mag * jnp.sin(a_im * dt)
    den = a_re * a_re + a_im * a_im
    xr, xi = abr - 1.0, abi
    cr = ((xr * a_re + xi * a_im) / den)[:, None, :]
    ci = ((xi * a_re - xr * a_im) / den)[:, None, :]
    e = jnp.exp(lb_logits - jnp.max(lb_logits, axis=0, keepdims=True))
    lb = e[0:1, :] / jnp.sum(e, axis=0, keepdims=True)
    return abr, abi, cr * bt_re - ci * bt_im, cr * bt_im + ci * bt_re, lb


def _whole(shape):
    return pl.BlockSpec(shape, lambda: (0,) * len(shape))


def _s5_params_fwd(a_re, a_im, log_dt, bt_re, bt_im, lb_logits):
    ins = (a_re, a_im, log_dt, bt_re, bt_im, lb_logits)
    outs = [jax.ShapeDtypeStruct(s, F32) for s in (a_re.shape, a_re.shape, bt_re.shape, bt_re.shape, (1, lb_logits.shape[1]))]

    def body(*refs):
        res = _s5_discretize(*[r[...] for r in refs[:6]])
        for o_ref, o in zip(refs[6:], res):
            o_ref[...] = o

    return pl.pallas_call(body, name="s5_params_fwd", in_specs=[_whole(a.shape) for a in ins],
                          out_specs=[_whole(o.shape) for o in outs], out_shape=outs, compiler_params=_cparams())(*ins)


def _s5_params_bwd(a_re, a_im, log_dt, bt_re, bt_im, lb_logits, d_abr, d_abi, d_bbr, d_bbi, d_lb):
    ins = (a_re, a_im, log_dt, bt_re, bt_im, lb_logits, d_abr, d_abi, d_bbr, d_bbi, d_lb)
    outs = [jax.ShapeDtypeStruct(a.shape, F32) for a in ins[:6]]

    def body(*refs):
        _, vjp = jax.vjp(_s5_discretize, *[r[...] for r in refs[:6]])
        for o_ref, o in zip(refs[11:], vjp(tuple(r[...] for r in refs[6:11]))):
            o_ref[...] = o

    return pl.pallas_call(body, name="s5_params_bwd", in_specs=[_whole(a.shape) for a in ins],
                          out_specs=[_whole(o.shape) for o in outs], out_shape=outs, compiler_params=_cparams())(*ins)


ADAMW_WHOLE_BYTES = 1024 * 1024


def _adamw(w, g, m, v, *, name, layer=0, n_layers=1, into=None):
    R, C = w.shape
    whole = R % SUBLANE != 0 or R * C * w.dtype.itemsize <= ADAMW_WHOLE_BYTES
    tr = R if whole else _pick(R, (256, 128, 64, 32, 16, 8))
    slabs = g.shape[0]
    n_prev = 0 if into is None else len(into)

    def body(w_ref, g_ref, m_ref, v_ref, *rest):
        g_out, d_ref, mo_ref, vo_ref = rest[n_prev:]
        gv = g_ref[0].astype(F32)
        for s in range(1, slabs):
            gv = gv + g_ref[s].astype(F32)
        m2 = ADAM_B1 * m_ref[...] + (1.0 - ADAM_B1) * gv
        v2 = ADAM_B2 * v_ref[...] + (1.0 - ADAM_B2) * (gv * gv)
        m_hat = m2 / (1.0 - ADAM_B1 ** ADAM_STEP)
        v_hat = v2 / (1.0 - ADAM_B2 ** ADAM_STEP)
        g_out[0] = gv
        d_ref[0] = -ADAM_LR * (m_hat / (jnp.sqrt(v_hat) + ADAM_EPS) + ADAM_WD * w_ref[...])
        mo_ref[0] = m2
        vo_ref[0] = v2

    spec = pl.BlockSpec((tr, C), lambda i: (i, 0))
    out_spec = pl.BlockSpec((1, tr, C), lambda i: (layer, i, 0))
    out = jax.ShapeDtypeStruct((n_layers, R, C), F32)
    return pl.pallas_call(
        body, name=name, grid=(R // tr,),
        in_specs=[spec, pl.BlockSpec((slabs, tr, C), lambda i: (0, i, 0)), spec, spec] + [pl.BlockSpec(memory_space=pl.ANY)] * n_prev,
        out_specs=[out_spec] * 4, out_shape=[out] * 4, input_output_aliases={4 + k: k for k in range(n_prev)},
        compiler_params=_cparams(("parallel",)))(w, g, m, v, *(into or ()))


N_CHIPS = 4
N_CORES = 2


_FLIPS = tuple((dx, dy, dc) for dx in (0, 1) for dy in (0, 1) for dc in (0, 1) if (dx, dy, dc) != (0, 0, 0))


_HBM = pl.BlockSpec(memory_space=pltpu.HBM)
_SEM = pl.BlockSpec(memory_space=pltpu.SEMAPHORE)
_SPLIT_COPY = pltpu.CompilerParams(has_side_effects=pltpu.SideEffectType.DATAFLOW_SIDE_EFFECTING)


def _exchange_copies(src_refs, land_refs, send_sems, recv_sems, scatter, arriving):
    x, y, c = lax.axis_index("x"), lax.axis_index("y"), lax.axis_index("c")
    me_chip = 2 * x + y
    copies = []
    for a, (s_ref, l_ref) in enumerate(zip(src_refs, land_refs)):
        for j, (dx, dy, dc) in enumerate(_FLIPS):
            px, py, pc = (1 - x if dx else x), (1 - y if dy else y), (1 - c if dc else c)
            k = a * len(_FLIPS) + j
            p_chip = 2 * px + py
            copies.append(pltpu.make_async_remote_copy(
                src_ref=s_ref.at[p_chip, pc] if scatter else s_ref, dst_ref=l_ref.at[p_chip, pc] if arriving else l_ref.at[me_chip, c],
                send_sem=send_sems.at[k], recv_sem=recv_sems.at[k], device_id=(px, py, pc), device_id_type=MESH))
    return copies


def _exchange_start(srcs, *, scatter, name, after=()):
    n_arr = len(srcs)
    n_sem = n_arr * len(_FLIPS)
    n_in = 2 * n_arr + len(after)
    lands = [lax.empty(s.shape if scatter else (N_CHIPS, N_CORES) + s.shape, s.dtype) for s in srcs]

    def body(*refs):
        src_refs, land_refs = refs[:n_arr], refs[n_arr:2 * n_arr]
        for cp in _exchange_copies(src_refs, land_refs, refs[n_in], refs[n_in + 1], scatter, arriving=False):
            cp.start()
        refs[-1][...] = jnp.zeros_like(refs[-1])

    thru = [pltpu.HBM(a.shape, a.dtype) for a in srcs + lands]
    outs = pl.pallas_call(
        body, name=name,
        out_shape=(pltpu.SemaphoreType.DMA((n_sem,)), pltpu.SemaphoreType.DMA((n_sem,)), *thru,
                   jax.ShapeDtypeStruct((SUBLANE, LANE), F32)),
        in_specs=[_HBM] * (2 * n_arr) + [pl.BlockSpec(memory_space=pl.ANY)] * len(after),
        out_specs=(_SEM, _SEM, *[_HBM] * (2 * n_arr), pl.BlockSpec(memory_space=pltpu.VMEM)),
        input_output_aliases={i: 2 + i for i in range(2 * n_arr)}, compiler_params=_SPLIT_COPY,
    )(*[pltpu.with_memory_space_constraint(a, pltpu.HBM) for a in srcs + lands], *after)
    return outs[0], outs[1], list(outs[2:2 + n_arr]), list(outs[2 + n_arr:2 + 2 * n_arr]), outs[-1]


def _exchange_wait(started, after, *, scatter, name):
    send_sems, recv_sems, srcs, lands, _ = started
    n_arr = len(srcs)

    def body(*refs):
        src_refs, land_refs = refs[:n_arr], refs[n_arr:2 * n_arr]
        for cp in _exchange_copies(src_refs, land_refs, refs[2 * n_arr], refs[2 * n_arr + 1], scatter, arriving=True):
            cp.wait_send()
            cp.wait_recv()

    outs = pl.pallas_call(
        body, name=name, out_shape=[pltpu.HBM(a.shape, a.dtype) for a in srcs + lands],
        in_specs=[_HBM] * (2 * n_arr) + [_SEM, _SEM, pl.BlockSpec(memory_space=pl.ANY)], out_specs=[_HBM] * (2 * n_arr),
        input_output_aliases={i: i for i in range(2 * n_arr)}, compiler_params=_SPLIT_COPY,
    )(*srcs, *lands, send_sems, recv_sems, after)
    return list(outs[:n_arr]), list(outs[n_arr:])


def _with_own(land, own):
    me_chip = 2 * lax.axis_index("x") + lax.axis_index("y")
    return lax.dynamic_update_slice(land, own[None, None], (me_chip, lax.axis_index("c")) + (0,) * own.ndim)


def _rms_fwd_fn(h, g):
    return (_rms(h, g),)


def _rms_bwd_fn(h, dhn, dres, g):
    _, vjp = jax.vjp(_rms, h, g)
    dh, dg = vjp(dhn)
    return dh + dres, dg


def _loss_fn(h, tgt, g):
    y, vjp = jax.vjp(_rms, h, g)
    diff = y - tgt
    dh, dg = vjp(diff * (1.0 / D_MODEL))
    return dh, dg, (0.5 / D_MODEL) * jnp.sum(diff * diff, axis=0, keepdims=True)


def _s5_act(ys, u, d):
    return _gelu(ys + d * u)


def _s5_gate(z, gl, b):
    return z * _sigmoid(gl + b)


def _s5_act_fn(ys, u, d):
    return (_s5_act(ys, u, d),)


def _s5_mix_fn(ya, z, gl, b):
    return (jnp.concatenate([ya, _s5_gate(z, gl, b)], axis=1),)


def _s5_gate_bwd_fn(z, gl, dyb, b):
    _, vjp = jax.vjp(_s5_gate, z, gl, b)
    return vjp(dyb)


def _s5_act_bwd_fn(ys, u, dz1, dz2, d):
    _, vjp = jax.vjp(_s5_act, ys, u, d)
    return vjp(dz1 + dz2)


def _dproj_fn(dq, df, di, dg, du1, du2):
    return (jnp.concatenate([dq, df, di, dg, du1 + du2], axis=1),)


def _rope_pair(r1, r2, pos, freqs):
    ang = pos.astype(F32) * freqs
    c, s = jnp.cos(ang), jnp.sin(ang)
    return r1 * c - r2 * s, r1 * s + r2 * c


_ODD_SPLITS = (0, MLA_Q_RANK, MLA_Q_RANK + MLA_KV_RANK, MLA_Q_RANK + MLA_KV_RANK + LANE, ODD_IN_PAD)


def _mla_prep(cq, ckv, k1, k2, qg, kvg, pos, freqs):
    ko1, ko2 = _rope_pair(k1, k2, pos, freqs)
    return _rms(cq, qg), _rms(ckv, kvg), ko1, ko2


def _mla_prep_fn(proj, pos, qg, kvg, freqs):
    parts = [proj[:, a:b] for a, b in zip(_ODD_SPLITS[:-1], _ODD_SPLITS[1:])]
    return _mla_prep(*parts, qg, kvg, pos, freqs)


def _mla_prep_bwd_fn(proj, pos, dqn, dkvn, dko1, dko2, qg, kvg, freqs):
    parts = [proj[:, a:b] for a, b in zip(_ODD_SPLITS[:-1], _ODD_SPLITS[1:])]
    _, vjp = jax.vjp(lambda *a: _mla_prep(*a, pos, freqs), *parts, qg, kvg)
    dcq, dckv, dk1, dk2, dqg, dkvg = vjp((dqn, dkvn, dko1, dko2))
    return jnp.concatenate([dcq, dckv, dk1, dk2], axis=1), dqg, dkvg


ROPE_ROWS = 512


def _rope_heads(x, pos, freqs):
    half = MLA_ROPE // 2
    reps = x.shape[1] // LANE
    ang = pos.astype(F32) * freqs[:, :LANE]
    cos = jnp.concatenate([jnp.cos(ang)] * reps, axis=1)
    sin = jnp.concatenate([jnp.sin(ang)] * reps, axis=1)
    first = (lax.broadcasted_iota(jnp.int32, x.shape, 1) % MLA_ROPE) < half
    other = jnp.where(first, pltpu.roll(x, x.shape[1] - half, 1), pltpu.roll(x, half, 1))
    return x * cos + other * jnp.where(first, -sin, sin)


def _rope_q_fwd(q_all, pos, freqs, *, name):
    T = q_all.shape[0]
    tm = min(ROPE_ROWS, T)
    nope_w, rope_w = MLA_HEADS * MLA_NOPE, MLA_HEADS * MLA_ROPE

    def body(r_ref, pos_ref, f_ref, o_ref):
        out = _rope_heads(r_ref[...], pos_ref[...], f_ref[...])
        for h in range(MLA_HEADS):
            o_ref[h] = out[:, h * MLA_ROPE:(h + 1) * MLA_ROPE]

    return pl.pallas_call(
        body, name=name, grid=(T // tm,),
        in_specs=[pl.BlockSpec((tm, rope_w), lambda i: (i, nope_w // rope_w)), pl.BlockSpec((tm, 1), lambda i: (i, 0)),
                  pl.BlockSpec((1, rope_w), lambda i: (0, 0))],
        out_specs=pl.BlockSpec((MLA_HEADS, tm, MLA_ROPE), lambda i: (0, i, 0)),
        out_shape=jax.ShapeDtypeStruct((MLA_HEADS, T, MLA_ROPE), F32),
        compiler_params=_cparams(("parallel",)),
    )(q_all, pos, freqs)


def _rope_q_bwd(dq_nope, dq_rope, pos, freqs, *, name):
    T = dq_nope.shape[0]
    tm = min(ROPE_ROWS, T)
    nope_w, rope_w = MLA_HEADS * MLA_NOPE, MLA_HEADS * MLA_ROPE

    def body(dn_ref, dr_ref, pos_ref, f_ref, o_ref):
        d_out = jnp.concatenate([dr_ref[h] for h in range(MLA_HEADS)], axis=1)
        o_ref[:, :nope_w] = dn_ref[...].astype(o_ref.dtype)
        o_ref[:, nope_w:] = _rope_heads(d_out, pos_ref[...], -f_ref[...]).astype(o_ref.dtype)

    return pl.pallas_call(
        body, name=name, grid=(T // tm,),
        in_specs=[pl.BlockSpec((tm, nope_w), lambda i: (i, 0)), pl.BlockSpec((MLA_HEADS, tm, MLA_ROPE), lambda i: (0, i, 0)),
                  pl.BlockSpec((tm, 1), lambda i: (i, 0)), pl.BlockSpec((1, rope_w), lambda i: (0, 0))],
        out_specs=pl.BlockSpec((tm, nope_w + rope_w), lambda i: (i, 0)),
        out_shape=jax.ShapeDtypeStruct((T, nope_w + rope_w), BF16),
        compiler_params=_cparams(("parallel",)),
    )(dq_nope, dq_rope, pos, freqs)


W_NAMES = ("norm_mix_g", "norm_ffn_g", "final_norm_g", "even_w_in", "hgrn_lb_logits", "hgrn_norm_g", "s5_a_re", "s5_a_im",
           "s5_log_dt", "s5_b_re", "s5_b_im", "s5_c_re", "s5_c_im", "s5_d", "s5_w_glu", "s5_b_glu", "even_w_out", "odd_w_in",
           "mla_q_norm_g", "mla_w_uq", "mla_kv_norm_g", "mla_w_ukv", "odd_w_out", "ffn_w_in", "ffn_conv_w", "ffn_conv_b",
           "ffn_w_out")
BIG_UNITS = (("even_w_in", 0, True), ("s5_w_glu", 0, False), ("even_w_out", 0, False), ("odd_w_in", 0, False),
             ("mla_w_uq", 0, True), ("mla_w_ukv", 0, True), ("odd_w_out", 0, False),
             ("ffn_w_in", 0, True), ("ffn_w_in", 1, True), ("ffn_w_out", 0, False), ("ffn_w_out", 1, False))
BIG_NAMES = tuple(dict.fromkeys(u[0] for u in BIG_UNITS))
SMALL_SHARDED = (("mla_q_norm_g", 1), ("mla_kv_norm_g", 1), ("ffn_conv_w", 2))
SMALL_SHARDED_NAMES = tuple(s[0] for s in SMALL_SHARDED)
REPLICATED = tuple(n for n in W_NAMES if n not in BIG_NAMES + SMALL_SHARDED_NAMES)
REPLICATED_LATE = ("norm_mix_g",)
REPLICATED_EARLY = tuple(n for n in REPLICATED if n not in REPLICATED_LATE)


def _pack(flats, cols, row_mult):
    flat = jnp.concatenate(flats, axis=-1)
    pad = (-flat.shape[-1]) % (cols * row_mult)
    flat = jnp.pad(flat, [(0, 0)] * (flat.ndim - 1) + [(0, pad)])
    return flat.reshape(flat.shape[:-1] + (-1, cols))


def _unpack(flat, shapes):
    out, off = [], 0
    for shp in shapes:
        n = int(np.prod(shp))
        out.append(flat[..., off:off + n].reshape(flat.shape[:-1] + tuple(shp)))
        off += n
    return out


UNIT_TRANSPOSED = {(n, l): t for n, l, t in BIG_UNITS}


def _unit_block(arrs, unit):
    blk = arrs[unit[0]][unit[1]]
    return blk.T if UNIT_TRANSPOSED[unit] else blk
STAGES = ((("even_w_in", 0),),
          (("s5_w_glu", 0), ("even_w_out", 0)),
          (("ffn_w_in", 0), ("ffn_w_out", 0)),
          (("odd_w_in", 0), ("mla_w_uq", 0), ("mla_w_ukv", 0), ("odd_w_out", 0)),
          (("ffn_w_in", 1), ("ffn_w_out", 1)))


def _gather_start(w, stage, with_small, after):
    srcs = [_unit_block(w, unit).astype(BF16) for unit in STAGES[stage]]
    if with_small:
        srcs.append(_pack([w[n].reshape(-1) for n in SMALL_SHARDED_NAMES], LANE, SUBLANE))
    return _exchange_start(srcs, scatter=False, name=f"gather_start_{stage}", after=after)


def _gather_finish(started, after, w, stage, with_small):
    srcs, lands = _exchange_wait(started, after, scatter=False, name=f"gather_wait_{stage}")
    lands = [_with_own(land, src) for land, src in zip(lands, srcs)]
    big = {unit: g.reshape(N_DEV * g.shape[2], g.shape[3]) for unit, g in zip(STAGES[stage], lands)}
    if not with_small:
        return big
    parts = _unpack(lands[-1].reshape(N_DEV, -1), [w[n].shape for n in SMALL_SHARDED_NAMES])
    small = {}
    for (n, ax), p in zip(SMALL_SHARDED, parts):
        shp = list(w[n].shape)
        shp[ax] *= N_DEV
        small[n] = jnp.moveaxis(p, 0, ax).reshape(shp)
    return big, small


def _scatter_start(g_big, stage, extra=()):
    srcs = []
    for unit in STAGES[stage]:
        g = g_big[unit].astype(BF16)
        srcs.append(g.reshape(N_CHIPS, N_CORES, g.shape[0] // N_DEV, g.shape[1]))
    return _exchange_start(srcs + list(extra), scatter=True, name=f"scatter_start_{stage}")


def _scatter_finish(started, after, stage):
    srcs, lands = _exchange_wait(started, after, scatter=True, name=f"scatter_wait_{stage}")
    me_chip, c = 2 * lax.axis_index("x") + lax.axis_index("y"), lax.axis_index("c")
    outs = []
    for land, src in zip(lands, srcs):
        own = lax.dynamic_slice(src, (me_chip, c) + (0,) * (src.ndim - 2), (1, 1) + src.shape[2:])[0, 0]
        outs.append(_with_own(land, own).reshape((N_DEV,) + land.shape[2:]))
    return outs


def _small_sharded_pack(g_small, w):
    flats = []
    for n, ax in SMALL_SHARDED:
        shp = list(w[n].shape)
        g = g_small[n].astype(F32).reshape(shp[:ax] + [N_DEV] + shp[ax:])
        flats.append(jnp.moveaxis(g, ax, 0).reshape(N_DEV, -1))
    small = _pack(flats, LANE, SUBLANE)
    return small.reshape((N_CHIPS, N_CORES) + small.shape[1:])


def _replicated_pack(g_repl, names):
    vec = _pack([g_repl[n].reshape(-1).astype(F32) for n in names], LANE, SUBLANE)
    return jnp.broadcast_to(vec, (N_CHIPS, N_CORES) + vec.shape)


def _block_diag(blocks):
    G, a, b = blocks.shape
    return jnp.einsum('gab,gk->gakb', blocks, jnp.eye(G, dtype=blocks.dtype)).reshape(G * a, G * b)


def _diag_blocks(mat, a, b):
    G = mat.shape[0] // a
    return jnp.einsum('gagb->gab', mat.reshape(G, a, G, b))


def _ffn_fwd(h, g, w_in_t, conv_w, conv_b, w_out, tag):
    hn, = _rows(_rms_fwd_fn, [h], [g], [(D_MODEL, BF16)], [], name=f"ffn{tag}_norm")
    au = _mm(hn, w_in_t, tb=True, out_dtype=BF16, name=f"ffn{tag}_in")
    z = _ffn_mid_fwd(au, conv_w, conv_b, name=f"ffn{tag}_mid")
    return _mm(z, w_out, res=h, name=f"ffn{tag}_out"), (hn, au, z)


def _ffn_bwd(h, dh, saved, g, w_in_t, conv_w, conv_b, w_out, tag, deps=()):
    hn, au, z = saved
    dz = _mm(dh, w_out, tb=True, out_dtype=BF16, deps=deps, name=f"ffn{tag}_dz")
    dw_out = _mm(z, dh, ta=True, out_dtype=BF16, name=f"ffn{tag}_dwout")
    da, du, dcw, dcb = _ffn_mid_bwd(au, dz, conv_w, conv_b, name=f"ffn{tag}_dmid")
    dhn = _mm(da, w_in_t, b_rows=(0, D_FF), name=f"ffn{tag}_dhn_a")
    dhn = _mm(du, w_in_t, b_rows=(D_FF, D_FF), res=dhn, name=f"ffn{tag}_dhn_u")
    dw_in = _mm(da, hn, ta=True, out_dtype=BF16, out_rows=(0, 2 * D_FF), name=f"ffn{tag}_dwin_a")
    dw_in = _mm(du, hn, ta=True, out_dtype=BF16, out_rows=(D_FF, 2 * D_FF), into=dw_in, name=f"ffn{tag}_dwin_u")
    dh_in, dg = _rows(_rms_bwd_fn, [h, dhn, dh], [g], [(D_MODEL, F32)], [(1, D_MODEL)], name=f"ffn{tag}_dnorm")
    return dh_in, dict(g=dg, w_in=dw_in, conv_w=dcw, conv_b=dcb, w_out=dw_out)


def kernel(x, positions, norm_mix_g, norm_ffn_g, final_norm_g, even_w_in, hgrn_lb_logits, hgrn_norm_g, s5_a_re, s5_a_im, s5_log_dt, s5_b_re, s5_b_im, s5_c_re, s5_c_im, s5_d, s5_w_glu, s5_b_glu, even_w_out, odd_w_in, mla_q_norm_g, mla_w_uq, mla_kv_norm_g, mla_w_ukv, odd_w_out, ffn_w_in, ffn_conv_w, ffn_conv_b, ffn_w_out, loss_target, m_norm_mix_g, m_norm_ffn_g, m_final_norm_g, m_even_w_in, m_hgrn_lb_logits, m_hgrn_norm_g, m_s5_a_re, m_s5_a_im, m_s5_log_dt, m_s5_b_re, m_s5_b_im, m_s5_c_re, m_s5_c_im, m_s5_d, m_s5_w_glu, m_s5_b_glu, m_even_w_out, m_odd_w_in, m_mla_q_norm_g, m_mla_w_uq, m_mla_kv_norm_g, m_mla_w_ukv, m_odd_w_out, m_ffn_w_in, m_ffn_conv_w, m_ffn_conv_b, m_ffn_w_out, v_norm_mix_g, v_norm_ffn_g, v_final_norm_g, v_even_w_in, v_hgrn_lb_logits, v_hgrn_norm_g, v_s5_a_re, v_s5_a_im, v_s5_log_dt, v_s5_b_re, v_s5_b_im, v_s5_c_re, v_s5_c_im, v_s5_d, v_s5_w_glu, v_s5_b_glu, v_even_w_out, v_odd_w_in, v_mla_q_norm_g, v_mla_w_uq, v_mla_kv_norm_g, v_mla_w_ukv, v_odd_w_out, v_ffn_w_in, v_ffn_conv_w, v_ffn_conv_b, v_ffn_w_out):
    given = dict(locals())
    w = {n: given[n] for n in W_NAMES}
    mom = {n: given["m_" + n] for n in W_NAMES}
    var = {n: given["v_" + n] for n in W_NAMES}
    T = x.shape[1]
    h0 = x[0]
    tgt = loss_target[0]
    pos = positions.reshape(T, 1)

    gathers = []
    for s in range(len(STAGES)):
        gathers.append(_gather_start(w, s, with_small=(s == 1), after=[g[4] for g in gathers[-1:]]))
    half = MLA_ROPE // 2
    kr0 = MLA_Q_RANK + MLA_KV_RANK
    freqs = ROPE_THETA ** (-jnp.arange(0, MLA_ROPE, 2, dtype=F32) / MLA_ROPE)
    freqs_q = jnp.tile(jnp.concatenate([freqs, freqs]), MLA_HEADS)[None, :]
    freqs_k = jnp.concatenate([freqs, jnp.zeros((LANE - half,), F32)])[None, :]

    sp_in = (s5_a_re[0], s5_a_im[0], s5_log_dt[0][:, None], s5_b_re[0].transpose(0, 2, 1), s5_b_im[0].transpose(0, 2, 1),
             hgrn_lb_logits)
    abr, abi, bbt_re, bbt_im, lb0 = _s5_params_fwd(*sp_in)
    a_re, a_im = abr.reshape(1, S5_WIDTH), abi.reshape(1, S5_WIDTH)
    bb_re, bb_im = _block_diag(bbt_re).astype(BF16), _block_diag(bbt_im).astype(BF16)
    c_re = _block_diag(s5_c_re[0].transpose(0, 2, 1)).astype(BF16)
    c_im_neg = _block_diag(-s5_c_im[0].transpose(0, 2, 1)).astype(BF16)
    u_cols = (4 * HGRN_DIM, S5_DIM)

    hn0, = _rows(_rms_fwd_fn, [h0], [norm_mix_g[0:1]], [(D_MODEL, BF16)], [], name="mix0_norm", deps=[gathers[-1][4]])
    full = _gather_finish(gathers[0], hn0, w, 0, False)
    w_ein_t = full["even_w_in", 0]
    proj = _mm(hn0, w_ein_t, tb=True, name="even_in")
    y_a, states = _hgrn_fwd(proj, lb0, hgrn_norm_g, name="hgrn_fwd")
    s_re, s_im = _s5_scan_fwd(proj, u_cols, bb_re, bb_im, a_re, a_im, name="s5_scan_fwd")
    more, full_small = _gather_finish(gathers[1], s_re, w, 1, True)
    w_glu, w_eout = more["s5_w_glu", 0], more["even_w_out", 0]
    qg, kvg, conv_w = full_small["mla_q_norm_g"], full_small["mla_kv_norm_g"], full_small["ffn_conv_w"]
    ys = _mm(s_im, c_im_neg, res=_mm(s_re, c_re, name="s5_y_re"), name="s5_y_im")
    z5, = _rows(_s5_act_fn, [ys, (proj,) + u_cols], [s5_d], [(S5_DIM, F32)], [], name="s5_act")
    gl = _mm(z5, w_glu, name="s5_glu")
    mixin, = _rows(_s5_mix_fn, [y_a, z5, gl], [s5_b_glu], [(D_MODEL, BF16)], [], name="s5_mix")
    h1 = _mm(mixin, w_eout, res=h0, name="even_out")
    full.update(_gather_finish(gathers[2], h1, w, 2, False))
    w_fin, w_fout = [full["ffn_w_in", 0]], [full["ffn_w_out", 0]]
    h2, ffn0_saved = _ffn_fwd(h1, norm_ffn_g[0:1], w_fin[0], conv_w[0], ffn_conv_b[0:1], w_fout[0], 0)

    full.update(_gather_finish(gathers[3], h2, w, 3, False))
    w_oin, w_ukv_t, w_oout = full["odd_w_in", 0], full["mla_w_ukv", 0], full["odd_w_out", 0]
    zpad = jnp.zeros((D_MODEL, LANE - half), BF16)
    w_oin_pad = jnp.concatenate([w_oin[:, :kr0], w_oin[:, kr0:kr0 + half], zpad, w_oin[:, kr0 + half:], zpad], axis=1)
    w_uq3 = full["mla_w_uq", 0].reshape(MLA_HEADS, MLA_QK, MLA_Q_RANK)
    w_uq_perm_t = jnp.concatenate([w_uq3[:, :MLA_NOPE].reshape(-1, MLA_Q_RANK),
                                   w_uq3[:, MLA_NOPE:].reshape(-1, MLA_Q_RANK)], axis=0)
    hn1, = _rows(_rms_fwd_fn, [h2], [norm_mix_g[1:2]], [(D_MODEL, BF16)], [], name="mix1_norm")
    proj_o = _mm(hn1, w_oin_pad, name="odd_in")
    qn, kvn, ko1, ko2 = _rows(_mla_prep_fn, [proj_o, pos], [qg, kvg, freqs_k],
                              [(MLA_Q_RANK, BF16), (MLA_KV_RANK, BF16), (LANE, F32), (LANE, F32)], [], name="mla_prep")
    q_all = _mm(qn, w_uq_perm_t, tb=True, name="mla_uq")
    kv = _mm(kvn, w_ukv_t, tb=True, out_dtype=BF16, name="mla_ukv")
    nope_w = MLA_HEADS * MLA_NOPE
    q_rope = _rope_q_fwd(q_all, pos, freqs_q, name="mla_rope_q")
    k_rope = jnp.concatenate([ko1[:, :half], ko2[:, :half]], axis=1)
    o, lse = _attn_fwd(q_all, q_rope, kv, k_rope, name="attn_fwd")
    h3 = _mm(o, w_oout, res=h2, name="odd_out")
    full.update(_gather_finish(gathers[4], h3, w, 4, False))
    w_fin.append(full["ffn_w_in", 1])
    w_fout.append(full["ffn_w_out", 1])
    h4, ffn1_saved = _ffn_fwd(h3, norm_ffn_g[1:2], w_fin[1], conv_w[1], ffn_conv_b[1:2], w_fout[1], 1)

    dh4, d_final_g, loss_cols = _rows(_loss_fn, [h4, tgt], [final_norm_g[None, :]], [(D_MODEL, F32)],
                                      [(1, D_MODEL), (1, D_MODEL)], name="loss_head")
    loss = lax.psum(jnp.sum(loss_cols), ("x", "y", "c"))

    dh3, gf1 = _ffn_bwd(h3, dh4, ffn1_saved, norm_ffn_g[1:2], w_fin[1], conv_w[1], ffn_conv_b[1:2], w_fout[1], 1)
    scatters = {4: _scatter_start({("ffn_w_in", 1): gf1["w_in"], ("ffn_w_out", 1): gf1["w_out"]}, 4)}
    do = _mm(dh3, w_oout, tb=True, deps=[scatters[4][4]], name="odd_out_dx")
    d_w_oout = _mm(o, dh3, ta=True, out_dtype=BF16, name="odd_out_dw")
    dq_nope, dq_rope, dkv, dk_rope = _attn_bwd(q_all, q_rope, kv, k_rope, o, lse, do, name="attn_bwd")
    lane_pad = ((0, 0), (0, LANE - half))
    dko1, dko2 = jnp.pad(dk_rope[:, :half], lane_pad), jnp.pad(dk_rope[:, half:], lane_pad)
    dq_all = _rope_q_bwd(dq_nope, dq_rope, pos, freqs_q, name="mla_rope_q_bwd")
    d_w_uq_perm_t = _mm(dq_all, qn, ta=True, out_dtype=BF16, name="mla_uq_dw")
    dqn = _mm(dq_all, w_uq_perm_t, name="mla_uq_dx")
    d_w_ukv_t = _mm(dkv, kvn, ta=True, out_dtype=BF16, name="mla_ukv_dw")
    dkvn = _mm(dkv, w_ukv_t, name="mla_ukv_dx")
    dproj_o, d_qg, d_kvg = _rows(_mla_prep_bwd_fn, [proj_o, pos, dqn, dkvn, dko1, dko2], [qg, kvg, freqs_k],
                                 [(ODD_IN_PAD, BF16)], [(1, MLA_Q_RANK), (1, MLA_KV_RANK)], name="mla_prep_bwd")
    d_w_oin_pad = _mm(hn1, dproj_o, ta=True, out_dtype=BF16, name="odd_in_dw")
    dhn1 = _mm(dproj_o, w_oin_pad, tb=True, name="odd_in_dx")
    dh2, d_mix_g1 = _rows(_rms_bwd_fn, [h2, dhn1, dh3], [norm_mix_g[1:2]], [(D_MODEL, F32)], [(1, D_MODEL)], name="mix1_dnorm")
    d_w_oin = jnp.concatenate([d_w_oin_pad[:, :kr0 + half], d_w_oin_pad[:, kr0 + LANE:kr0 + LANE + half]], axis=1)
    d_w_uq_t = jnp.concatenate([d_w_uq_perm_t[:nope_w].reshape(MLA_HEADS, MLA_NOPE, MLA_Q_RANK),
                                d_w_uq_perm_t[nope_w:].reshape(MLA_HEADS, MLA_ROPE, MLA_Q_RANK)], axis=1).reshape(-1, MLA_Q_RANK)
    scatters[3] = _scatter_start({("odd_w_in", 0): d_w_oin, ("mla_w_uq", 0): d_w_uq_t, ("mla_w_ukv", 0): d_w_ukv_t,
                                  ("odd_w_out", 0): d_w_oout}, 3)

    dh1, gf0 = _ffn_bwd(h1, dh2, ffn0_saved, norm_ffn_g[0:1], w_fin[0], conv_w[0], ffn_conv_b[0:1], w_fout[0], 0,
                        deps=[scatters[3][4]])
    scatters[2] = _scatter_start({("ffn_w_in", 0): gf0["w_in"], ("ffn_w_out", 0): gf0["w_out"]}, 2)
    dmix = _mm(dh1, w_eout, tb=True, deps=[scatters[2][4]], name="even_out_dx")
    d_w_eout = _mm(mixin, dh1, ta=True, out_dtype=BF16, name="even_out_dw")
    dq, df, di, dg, d_lb0, d_hgrn_g = _hgrn_bwd(proj, lb0, hgrn_norm_g, states, dmix, name="hgrn_bwd")
    dz1, dgl, d_b_glu = _rows(_s5_gate_bwd_fn, [z5, gl, (dmix, HGRN_DIM, S5_DIM)], [s5_b_glu],
                              [(S5_DIM, F32), (S5_DIM, BF16)], [(1, S5_DIM)], name="s5_gate_bwd")
    dz2 = _mm(dgl, w_glu, tb=True, name="s5_glu_dx")
    d_w_glu = _mm(z5, dgl, ta=True, out_dtype=BF16, name="s5_glu_dw")
    dys, du1, d_s5_d = _rows(_s5_act_bwd_fn, [ys, (proj,) + u_cols, dz1, dz2], [s5_d],
                             [(S5_DIM, BF16), (S5_DIM, F32)], [(1, S5_DIM)], name="s5_act_bwd")
    d_c_re = _mm(s_re, dys, ta=True, name="s5_dc_re")
    d_c_im_neg = _mm(s_im, dys, ta=True, name="s5_dc_im")
    lam_re, lam_im, d_ar, d_ai = _s5_scan_bwd(dys, c_re, c_im_neg, s_re, s_im, a_re, a_im, name="s5_scan_bwd")
    du2 = _mm(lam_im, bb_im, tb=True, res=_mm(lam_re, bb_re, tb=True, name="s5_du_re"), name="s5_du_im")
    d_bb_re = _mm(proj, lam_re, ta=True, a_cols=u_cols, name="s5_dbb_re")
    d_bb_im = _mm(proj, lam_im, ta=True, a_cols=u_cols, name="s5_dbb_im")
    sp_g = _s5_params_bwd(*sp_in, d_ar.sum(0).reshape(S5_GROUPS, S5_STATE), d_ai.sum(0).reshape(S5_GROUPS, S5_STATE),
                          _diag_blocks(d_bb_re, S5_GROUP, S5_STATE), _diag_blocks(d_bb_im, S5_GROUP, S5_STATE), d_lb0)
    d_a_re, d_a_im, d_log_dt, d_bt_re, d_bt_im, d_lb_logits = sp_g
    g_small = dict(mla_q_norm_g=d_qg, mla_kv_norm_g=d_kvg, ffn_conv_w=jnp.stack([gf0["conv_w"], gf1["conv_w"]]))
    g_repl = dict(
        norm_ffn_g=jnp.concatenate([gf0["g"], gf1["g"]]),
        final_norm_g=d_final_g[0], hgrn_lb_logits=d_lb_logits, hgrn_norm_g=d_hgrn_g,
        s5_a_re=d_a_re[None], s5_a_im=d_a_im[None], s5_log_dt=d_log_dt[:, 0][None],
        s5_b_re=d_bt_re.transpose(0, 2, 1)[None], s5_b_im=d_bt_im.transpose(0, 2, 1)[None],
        s5_c_re=_diag_blocks(d_c_re, S5_STATE, S5_GROUP).transpose(0, 2, 1)[None],
        s5_c_im=-_diag_blocks(d_c_im_neg, S5_STATE, S5_GROUP).transpose(0, 2, 1)[None],
        s5_d=d_s5_d, s5_b_glu=d_b_glu, ffn_conv_b=jnp.concatenate([gf0["conv_b"], gf1["conv_b"]]))
    scatters[1] = _scatter_start({("s5_w_glu", 0): d_w_glu, ("even_w_out", 0): d_w_eout}, 1,
                                 extra=[_small_sharded_pack(g_small, w), _replicated_pack(g_repl, REPLICATED_EARLY)])
    dproj, = _rows(_dproj_fn, [dq, df, di, dg, du1, du2], [], [(EVEN_IN, BF16)], [], name="even_dproj", deps=[scatters[1][4]])
    d_w_ein_t = _mm(dproj, hn0, ta=True, out_dtype=BF16, name="even_in_dw")
    dhn0 = _mm(dproj, w_ein_t, name="even_in_dx")
    grad_x, d_mix_g0 = _rows(_rms_bwd_fn, [h0, dhn0, dh1], [norm_mix_g[0:1]], [(D_MODEL, F32)], [(1, D_MODEL)], name="mix0_dnorm")
    g_repl["norm_mix_g"] = jnp.concatenate([d_mix_g0, d_mix_g1])
    scatters[0] = _scatter_start({("even_w_in", 0): d_w_ein_t}, 0, extra=[_replicated_pack(g_repl, REPLICATED_LATE)])

    delta, new_m, new_v = {}, {}, {}
    updated, partial = {}, {}
    after = scatters[0][4]
    for stage in (4, 3, 2, 1, 0):
        partial[stage] = _scatter_finish(scatters[stage], after, stage)
        for (n, l), slabs in zip(STAGES[stage], partial[stage]):
            updated[n] = _adamw(_unit_block(w, (n, l)), slabs, _unit_block(mom, (n, l)), _unit_block(var, (n, l)),
                                name=f"adamw_{n}_{l}", layer=l, n_layers=w[n].shape[0], into=updated.get(n))
        after = updated[STAGES[stage][-1][0]][0]
    grads = {}
    for n in BIG_NAMES:
        outs = [o.transpose(0, 2, 1) for o in updated[n]] if UNIT_TRANSPOSED[n, 0] else updated[n]
        grads[n], delta[n], new_m[n], new_v[n] = outs
    for names, slabs, tag in ((REPLICATED_EARLY, partial[1][-1], "repl"), (REPLICATED_LATE, partial[0][-1], "late"),
                              (SMALL_SHARDED_NAMES, partial[1][-2], "small")):
        packs = [_pack([t[n].reshape(-1) for n in names], LANE, SUBLANE) for t in (w, mom, var)]
        outs = _adamw(packs[0], slabs, packs[1], packs[2], name=f"adamw_{tag}")
        shapes = [w[n].shape for n in names]
        for dst, o_ in zip((grads, delta, new_m, new_v), outs):
            dst.update(zip(names, _unpack(o_.reshape(-1), shapes)))

    return (loss, grad_x[None], *[grads[n] for n in W_NAMES], *[delta[n] for n in W_NAMES],
            *[new_m[n] for n in W_NAMES], *[new_v[n] for n in W_NAMES])
```

```python
import functools
import math

import numpy as np
import jax
import jax.numpy as jnp
from jax import lax
from jax.experimental import pallas as pl
from jax.experimental.pallas import tpu as pltpu

F32 = jnp.float32
BF16 = jnp.bfloat16
_MXU_DTYPE = jnp.bfloat16

D_MODEL = 1024
HGRN_DIM = 512
HGRN_HEAD_DIM = 128
HGRN_HEADS = 4
HGRN_CHUNK = 64
S5_DIM = 512
S5_GROUPS = 32
S5_GROUP = 16
S5_STATE = 64
S5_WIDTH = S5_GROUPS * S5_STATE
EVEN_IN = 4 * HGRN_DIM + S5_DIM
MLA_HEADS = 8
MLA_Q_RANK = 384
MLA_KV_RANK = 256
MLA_NOPE = 128
MLA_ROPE = 64
MLA_V = 128
MLA_QK = MLA_NOPE + MLA_ROPE
LANE = 128
SUBLANE = 8
ODD_IN_PAD = MLA_Q_RANK + MLA_KV_RANK + 2 * LANE
ROPE_THETA = 10000.0
D_FF = 2816
EPS = 1e-6
ADAM_LR = 0.001
ADAM_B1 = 0.9
ADAM_B2 = 0.999
ADAM_EPS = 1e-08
ADAM_WD = 0.01
ADAM_STEP = 10

N_DEV = 8
VMEM_LIMIT_BYTES = 56 * 1024 * 1024
MESH = pl.DeviceIdType.MESH


def _cparams(sem=None):
    return pltpu.CompilerParams(dimension_semantics=sem, vmem_limit_bytes=VMEM_LIMIT_BYTES)


def _pick(n, cands):
    for c in cands:
        if n % c == 0:
            return c
    raise ValueError(f"no tile for {n} in {cands}")


def _sigmoid(x):
    return 0.5 * jnp.tanh(0.5 * x) + 0.5


def _silu(x):
    return x * _sigmoid(x)


def _gelu(x):
    return 0.5 * x * (1.0 + jnp.tanh(math.sqrt(2.0 / math.pi) * (x + 0.044715 * (x * x * x))))


def _rms(x, g):
    return x * lax.rsqrt(jnp.mean(x * x, axis=-1, keepdims=True) + EPS) * g


def _mxu(a, b, ca, cb):
    return lax.dot_general(a.astype(_MXU_DTYPE), b.astype(_MXU_DTYPE), (((ca,), (cb,)), ((), ())),
                           preferred_element_type=F32)


@functools.partial(jax.custom_vjp, nondiff_argnums=(2, 3))
def _mxu_ad(a, b, ca, cb):
    return _mxu(a, b, ca, cb)


def _mxu_ad_fwd(a, b, ca, cb):
    return _mxu(a, b, ca, cb), (a, b)


def _mxu_ad_bwd(ca, cb, saved, g):
    a, b = saved
    fa, fb = 1 - ca, 1 - cb
    da = _mxu(g, b, 1, fb) if ca == 1 else _mxu(b, g, fb, 1)
    db = _mxu(a, g, fa, 0) if cb == 0 else _mxu(g, a, 0, fa)
    return da, db


_mxu_ad.defvjp(_mxu_ad_fwd, _mxu_ad_bwd)


def _tri(n):
    row = lax.broadcasted_iota(jnp.int32, (n, n), 0)
    col = lax.broadcasted_iota(jnp.int32, (n, n), 1)
    return col <= row


def _cumsum_rows(x, reverse=False):
    n = x.shape[0]
    rowi = lax.broadcasted_iota(jnp.int32, x.shape, 0)
    s = 1
    while s < n:
        if reverse:
            x = x + jnp.where(rowi < n - s, pltpu.roll(x, n - s, 0), 0.0)
        else:
            x = x + jnp.where(rowi >= s, pltpu.roll(x, s, 0), 0.0)
        s *= 2
    return x


@jax.custom_vjp
def _cumsum_rows_ad(x):
    return _cumsum_rows(x)


def _cumsum_rows_ad_fwd(x):
    return _cumsum_rows(x), None


def _cumsum_rows_ad_bwd(_, g):
    return (_cumsum_rows(g, reverse=True),)


_cumsum_rows_ad.defvjp(_cumsum_rows_ad_fwd, _cumsum_rows_ad_bwd)


MM_VMEM_BUDGET = 36 * 1024 * 1024
MM_MAX_TILE = 1408


def _lane_divisors(n, cap, offs=()):
    return [d for d in range(min(n, cap) // LANE * LANE, 0, -LANE) if n % d == 0 and all(o % d == 0 for o in offs)]


def _mm_tiles(M, N, K, sa, sb, so, has_res, m_offs, n_offs, k_offs):
    best = None
    for tm in _lane_divisors(M, MM_MAX_TILE, m_offs):
        for tn in _lane_divisors(N, MM_MAX_TILE, n_offs):
            for tk in _lane_divisors(K, K, k_offs):
                nk = K // tk
                vmem = 2 * (tm * tk * sa + tk * tn * sb + tm * tn * so + tm * tn * 4 * has_res) + (tm * tn * 4 if nk > 1 else 0)
                if vmem <= MM_VMEM_BUDGET:
                    key = (-nk, tm * tn, tn)
                    if best is None or key > best[0]:
                        best = (key, tm, tn, tk)
                    break
    return best[1:]


def _mm(a, b, *, ta=False, tb=False, res=None, out_dtype=F32, a_cols=None, b_rows=None, deps=(), out_rows=None, into=None,
        name):
    a_minor = a.shape[1] if a_cols is None else a_cols[1]
    b_major = b.shape[0] if b_rows is None else b_rows[1]
    K, M = (a.shape[0], a_minor) if ta else (a_minor, a.shape[0])
    N = b_major if tb else b.shape[1]
    assert (b.shape[1] if tb else b_major) == K, (a.shape, b.shape, ta, tb)
    assert b_rows is None or not tb
    a_off = 0 if a_cols is None else a_cols[0]
    b_off = 0 if b_rows is None else b_rows[0]
    has_res = res is not None
    o_off, o_total = (0, M) if out_rows is None else out_rows
    tm, tn, tk = _mm_tiles(M, N, K, a.dtype.itemsize, b.dtype.itemsize, jnp.dtype(out_dtype).itemsize, has_res,
                           ((a_off,) if ta else ()) + (o_off,), (), ((a_off,) if not ta else ()) + (b_off,))
    nk = K // tk
    am, ak = (a_off // tm, 0) if ta else (0, a_off // tk)
    bk, om = b_off // tk, o_off // tm
    a_spec = pl.BlockSpec((tk, tm), lambda i, j, k: (k, i + am)) if ta else pl.BlockSpec((tm, tk), lambda i, j, k: (i, k + ak))
    b_spec = pl.BlockSpec((tn, tk), lambda i, j, k: (j, k)) if tb else pl.BlockSpec((tk, tn), lambda i, j, k: (k + bk, j))
    o_spec = pl.BlockSpec((tm, tn), lambda i, j, k: (i, j))
    ca, cb = (0 if ta else 1), (1 if tb else 0)
    extra = list(deps) + ([into] if into is not None else [])

    n_fixed = 2 + has_res + len(extra)

    def body(*refs):
        a_ref, b_ref = refs[0], refs[1]
        res_ref = refs[2] if has_res else None
        o_ref = refs[n_fixed]
        part = _mxu(a_ref[...], b_ref[...], ca, cb)
        if nk == 1:
            o_ref[...] = (part + res_ref[...] if has_res else part).astype(out_dtype)
            return
        acc_ref = refs[n_fixed + 1]
        k = pl.program_id(2)

        @pl.when(k == 0)
        def _():
            acc_ref[...] = part

        @pl.when(k > 0)
        def _():
            acc_ref[...] += part

        @pl.when(k == nk - 1)
        def _():
            o_ref[...] = (acc_ref[...] + res_ref[...] if has_res else acc_ref[...]).astype(out_dtype)

    ins = [a, b] + ([res] if has_res else []) + extra
    in_specs = [a_spec, b_spec] + ([o_spec] if has_res else []) + [pl.BlockSpec(memory_space=pl.ANY)] * len(extra)
    return pl.pallas_call(
        body, name=name, grid=(M // tm, N // tn, nk),
        in_specs=in_specs, out_specs=pl.BlockSpec((tm, tn), lambda i, j, k: (i + om, j)),
        out_shape=jax.ShapeDtypeStruct((o_total, N), out_dtype),
        scratch_shapes=[pltpu.VMEM((tm, tn), F32)] if nk > 1 else [],
        input_output_aliases={} if into is None else {len(ins) - 1: 0},
        compiler_params=_cparams(("parallel", "parallel", "arbitrary")),
    )(*ins)


def _rows(fn, row_ins, const_ins, row_outs, acc_outs, *, name, tm=512, deps=()):
    norm = [(r, 0, r.shape[1]) if not isinstance(r, tuple) else r for r in row_ins]
    T = norm[0][0].shape[0]
    tm = min(tm, T)
    nr, nc, no, na = len(norm), len(const_ins), len(row_outs), len(acc_outs)
    first_out = nr + nc + len(deps)

    def body(*refs):
        i = pl.program_id(0)
        vals = [r[...] for r in refs[:nr + nc]]
        outs = fn(*vals)
        for o_ref, o in zip(refs[first_out:first_out + no], outs[:no]):
            o_ref[...] = o.astype(o_ref.dtype)
        for a_ref, o in zip(refs[first_out + no:], outs[no:]):
            @pl.when(i == 0)
            def _(a_ref=a_ref, o=o):
                a_ref[...] = o

            @pl.when(i > 0)
            def _(a_ref=a_ref, o=o):
                a_ref[...] += o

    in_specs = []
    for arr, off, w in norm:
        assert off % w == 0, (off, w)
        in_specs.append(pl.BlockSpec((tm, w), lambda i, b=off // w: (i, b)))
    for c in const_ins:
        in_specs.append(pl.BlockSpec(c.shape, lambda i: (0, 0)))
    in_specs += [pl.BlockSpec(memory_space=pl.ANY)] * len(deps)
    out_specs = [pl.BlockSpec((tm, w), lambda i: (i, 0)) for w, _ in row_outs]
    out_specs += [pl.BlockSpec(s, lambda i: (0, 0)) for s in acc_outs]
    out_shape = [jax.ShapeDtypeStruct((T, w), dt) for w, dt in row_outs]
    out_shape += [jax.ShapeDtypeStruct(s, F32) for s in acc_outs]
    return pl.pallas_call(
        body, name=name, grid=(T // tm,), in_specs=in_specs, out_specs=out_specs, out_shape=out_shape,
        compiler_params=_cparams(("arbitrary",)),
    )(*[n[0] for n in norm], *const_ins, *deps)


FFN_COL_TILE = 2 * LANE
FFN_ROW_CHUNK = 512


FFN_HALO = 2 * SUBLANE


def _shift_down(ext, s, rows):
    return pltpu.roll(ext, s, 0)[FFN_HALO:FFN_HALO + rows]


def _shift_up(ext, s, rows):
    return pltpu.roll(ext, rows + FFN_HALO - s, 0)[:rows]


def _ffn_chunks(T):
    r = min(FFN_ROW_CHUNK, T)
    return r, T // r


def _ext_before(ref, c, r):
    if c == 0:
        return jnp.concatenate([jnp.zeros((FFN_HALO, ref.shape[1]), F32), ref[0:r, :].astype(F32)], axis=0)
    return ref[c * r - FFN_HALO:(c + 1) * r, :].astype(F32)


def _ext_after(ref, c, r, nch):
    if c == nch - 1:
        return jnp.concatenate([ref[c * r:(c + 1) * r, :].astype(F32), jnp.zeros((FFN_HALO, ref.shape[1]), F32)], axis=0)
    return ref[c * r:(c + 1) * r + FFN_HALO, :].astype(F32)


def _ffn_mid_fwd(au, conv_w, conv_b, *, name):
    T = au.shape[0]
    tc = FFN_COL_TILE
    ncol = D_FF // tc
    r, nch = _ffn_chunks(T)

    def body(a_ref, u_ref, w_ref, b_ref, z_ref):
        w0, w1, w2, bias = w_ref[0:1, :], w_ref[1:2, :], w_ref[2:3, :], b_ref[...]
        for c in range(nch):
            ext = _ext_before(a_ref, c, r)
            pre = w0 * _shift_down(ext, 2, r) + w1 * _shift_down(ext, 1, r) + w2 * ext[FFN_HALO:] + bias
            z_ref[c * r:(c + 1) * r, :] = (_silu(pre) * u_ref[c * r:(c + 1) * r, :].astype(F32)).astype(z_ref.dtype)

    return pl.pallas_call(
        body, name=name, grid=(ncol,),
        in_specs=[pl.BlockSpec((T, tc), lambda j: (0, j)), pl.BlockSpec((T, tc), lambda j: (0, j + ncol)),
                  pl.BlockSpec((3, tc), lambda j: (0, j)), pl.BlockSpec((1, tc), lambda j: (0, j))],
        out_specs=pl.BlockSpec((T, tc), lambda j: (0, j)),
        out_shape=jax.ShapeDtypeStruct((T, D_FF), BF16),
        compiler_params=_cparams(("parallel",)),
    )(au, au, conv_w, conv_b)


def _ffn_mid_bwd(au, dz, conv_w, conv_b, *, name):
    T = au.shape[0]
    tc = FFN_COL_TILE
    ncol = D_FF // tc
    r, nch = _ffn_chunks(T)

    def body(a_ref, u_ref, dz_ref, w_ref, b_ref, da_ref, du_ref, dw_ref, db_ref, dpre_ref):
        w0, w1, w2, bias = w_ref[0:1, :], w_ref[1:2, :], w_ref[2:3, :], b_ref[...]
        dw0 = jnp.zeros((1, tc), F32)
        dw1 = jnp.zeros((1, tc), F32)
        dw2 = jnp.zeros((1, tc), F32)
        db = jnp.zeros((1, tc), F32)
        for c in range(nch):
            rows = slice(c * r, (c + 1) * r)
            ext = _ext_before(a_ref, c, r)
            a2, a1, a0 = _shift_down(ext, 2, r), _shift_down(ext, 1, r), ext[FFN_HALO:]
            pre = w0 * a2 + w1 * a1 + w2 * a0 + bias
            sg = _sigmoid(pre)
            act = pre * sg
            dzc = dz_ref[rows, :].astype(F32)
            du_ref[rows, :] = (dzc * act).astype(du_ref.dtype)
            dpre = dzc * u_ref[rows, :].astype(F32) * (sg * (1.0 + pre * (1.0 - sg)))
            dpre_ref[rows, :] = dpre
            dw0 += jnp.sum(dpre * a2, axis=0, keepdims=True)
            dw1 += jnp.sum(dpre * a1, axis=0, keepdims=True)
            dw2 += jnp.sum(dpre * a0, axis=0, keepdims=True)
            db += jnp.sum(dpre, axis=0, keepdims=True)
        for c in range(nch):
            ext = _ext_after(dpre_ref, c, r, nch)
            da = w0 * _shift_up(ext, 2, r) + w1 * _shift_up(ext, 1, r) + w2 * ext[:r]
            da_ref[c * r:(c + 1) * r, :] = da.astype(da_ref.dtype)
        dw_ref[0:1, :] = dw0
        dw_ref[1:2, :] = dw1
        dw_ref[2:3, :] = dw2
        db_ref[...] = db

    col = lambda j: (0, j)
    return pl.pallas_call(
        body, name=name, grid=(ncol,),
        in_specs=[pl.BlockSpec((T, tc), col), pl.BlockSpec((T, tc), lambda j: (0, j + ncol)), pl.BlockSpec((T, tc), col),
                  pl.BlockSpec((3, tc), col), pl.BlockSpec((1, tc), col)],
        out_specs=[pl.BlockSpec((T, tc), col), pl.BlockSpec((T, tc), col), pl.BlockSpec((3, tc), col), pl.BlockSpec((1, tc), col)],
        out_shape=[jax.ShapeDtypeStruct((T, D_FF), BF16), jax.ShapeDtypeStruct((T, D_FF), BF16),
                   jax.ShapeDtypeStruct((3, D_FF), F32), jax.ShapeDtypeStruct((1, D_FF), F32)],
        scratch_shapes=[pltpu.VMEM((T, tc), F32)],
        compiler_params=_cparams(("parallel",)),
    )(au, au, dz, conv_w, conv_b)


HGRN_BLOCK = 256


def _hgrn_chunk(dot, cumsum, q, f, i, g, lb, ng, st):
    C = q.shape[0]
    forget = lb + (1.0 - lb) * _sigmoid(f)
    k = 1.0 - forget
    b = cumsum(jnp.log(forget))
    b_last = b[C - 1:C, :]
    qd = q * jnp.exp(b)
    kd = k * jnp.exp(-b)
    att = jnp.where(_tri(C), dot(qd, kd, 1, 1), 0.0)
    o = dot(att, i, 1, 0) + dot(qd, st, 1, 1)
    st_new = st * jnp.exp(b_last) + dot(i, k * jnp.exp(b_last - b), 0, 0)
    on = o * lax.rsqrt(jnp.mean(o * o, axis=-1, keepdims=True) + EPS) * ng
    return on * _silu(g), st_new


def _hgrn_specs(T, rev):
    tb = min(HGRN_BLOCK, T)
    nb = T // tb
    blk = (lambda n: nb - 1 - n) if rev else (lambda n: n)
    hd = HGRN_HEAD_DIM
    proj_specs = [pl.BlockSpec((tb, HGRN_DIM), lambda n, k=k: (blk(n), k)) for k in range(4)]
    vec_spec = pl.BlockSpec((1, HGRN_DIM), lambda n: (0, 0))
    tok_spec = pl.BlockSpec((tb, HGRN_DIM), lambda n: (blk(n), 0))
    st_spec = pl.BlockSpec((HGRN_HEADS, tb // HGRN_CHUNK, hd, hd), lambda n: (0, blk(n), 0, 0))
    return tb, nb, proj_specs, vec_spec, tok_spec, st_spec


def _head_cols(h):
    return slice(h * HGRN_HEAD_DIM, (h + 1) * HGRN_HEAD_DIM)


def _hgrn_fwd(proj, lb, ng, *, name):
    T = proj.shape[0]
    tb, nb, proj_specs, vec_spec, tok_spec, st_spec = _hgrn_specs(T, False)
    nsub = tb // HGRN_CHUNK
    hd = HGRN_HEAD_DIM

    def body(q_ref, f_ref, i_ref, g_ref, lb_ref, ng_ref, y_ref, sts_ref, st_ref):
        @pl.when(pl.program_id(0) == 0)
        def _():
            st_ref[...] = jnp.zeros_like(st_ref)

        st = [st_ref[h] for h in range(HGRN_HEADS)]
        for s in range(nsub):
            rows = slice(s * HGRN_CHUNK, (s + 1) * HGRN_CHUNK)
            for h in range(HGRN_HEADS):
                cols = _head_cols(h)
                sts_ref[h, s] = st[h]
                y, st[h] = _hgrn_chunk(_mxu, _cumsum_rows, q_ref[rows, cols], f_ref[rows, cols], i_ref[rows, cols],
                                       g_ref[rows, cols], lb_ref[:, cols], ng_ref[:, cols], st[h])
                y_ref[rows, cols] = y
        for h in range(HGRN_HEADS):
            st_ref[h] = st[h]

    return pl.pallas_call(
        body, name=name, grid=(nb,),
        in_specs=proj_specs + [vec_spec, vec_spec], out_specs=[tok_spec, st_spec],
        out_shape=[jax.ShapeDtypeStruct((T, HGRN_DIM), F32),
                   jax.ShapeDtypeStruct((HGRN_HEADS, T // HGRN_CHUNK, hd, hd), F32)],
        scratch_shapes=[pltpu.VMEM((HGRN_HEADS, hd, hd), F32)],
        compiler_params=_cparams(("arbitrary",)),
    )(proj, proj, proj, proj, lb, ng)


def _hgrn_bwd(proj, lb, ng, states, dmix, *, name):
    T = proj.shape[0]
    tb, nb, proj_specs, vec_spec, tok_spec, st_spec = _hgrn_specs(T, True)
    nsub = tb // HGRN_CHUNK
    hd = HGRN_HEAD_DIM
    chunk = functools.partial(_hgrn_chunk, _mxu_ad, _cumsum_rows_ad)

    def body(q_ref, f_ref, i_ref, g_ref, lb_ref, ng_ref, sts_ref, dy_ref,
             dq_ref, df_ref, di_ref, dg_ref, dlb_ref, dng_ref, dst_ref):
        @pl.when(pl.program_id(0) == 0)
        def _():
            dst_ref[...] = jnp.zeros_like(dst_ref)
            dlb_ref[...] = jnp.zeros_like(dlb_ref)
            dng_ref[...] = jnp.zeros_like(dng_ref)

        dst = [dst_ref[h] for h in range(HGRN_HEADS)]
        dlb = [jnp.zeros((1, hd), F32)] * HGRN_HEADS
        dng = [jnp.zeros((1, hd), F32)] * HGRN_HEADS
        for s in reversed(range(nsub)):
            rows = slice(s * HGRN_CHUNK, (s + 1) * HGRN_CHUNK)
            for h in range(HGRN_HEADS):
                cols = _head_cols(h)
                _, vjp = jax.vjp(chunk, q_ref[rows, cols], f_ref[rows, cols], i_ref[rows, cols], g_ref[rows, cols],
                                 lb_ref[:, cols], ng_ref[:, cols], sts_ref[h, s])
                dq, df, di, dg, dlb_s, dng_s, dst[h] = vjp((dy_ref[rows, cols], dst[h]))
                dq_ref[rows, cols] = dq
                df_ref[rows, cols] = df
                di_ref[rows, cols] = di
                dg_ref[rows, cols] = dg
                dlb[h] = dlb[h] + dlb_s
                dng[h] = dng[h] + dng_s
        for h in range(HGRN_HEADS):
            dst_ref[h] = dst[h]
            dlb_ref[:, _head_cols(h)] += dlb[h]
            dng_ref[:, _head_cols(h)] += dng[h]

    tok_out = jax.ShapeDtypeStruct((T, HGRN_DIM), F32)
    vec_out = jax.ShapeDtypeStruct((1, HGRN_DIM), F32)
    return pl.pallas_call(
        body, name=name, grid=(nb,),
        in_specs=proj_specs + [vec_spec, vec_spec, st_spec, tok_spec],
        out_specs=[tok_spec] * 4 + [vec_spec, vec_spec],
        out_shape=[tok_out] * 4 + [vec_out, vec_out],
        scratch_shapes=[pltpu.VMEM((HGRN_HEADS, hd, hd), F32)],
        compiler_params=_cparams(("arbitrary",)),
    )(proj, proj, proj, proj, lb, ng, states, dmix)


S5_LANES = 512
S5_ROWS = 1024


def _cmul(ar, ai, br, bi):
    return ar * br - ai * bi, ar * bi + ai * br


def _power_table(ar, ai, exps):
    a2 = _cmul(ar, ai, ar, ai)
    a4 = _cmul(*a2, *a2)
    e = exps - 1
    pr = jnp.broadcast_to(ar, exps.shape)
    pi = jnp.broadcast_to(ai, exps.shape)
    for bit, (fr, fi) in enumerate(((ar, ai), a2, a4)):
        nr, ni = _cmul(pr, pi, fr, fi)
        on = ((e >> bit) & 1) == 1
        pr, pi = jnp.where(on, nr, pr), jnp.where(on, ni, pi)
    return pr, pi, a2, a4


def _s5_scan_fwd(x, x_cols, b_re, b_im, a_re, a_im, *, name):
    T = x.shape[0]
    w, tr = S5_LANES, min(S5_ROWS, T)
    ncol, nt = S5_WIDTH // w, T // tr
    assert x_cols[0] % x_cols[1] == 0

    def body(x_ref, br_ref, bi_ref, ar_ref, ai_ref, sr_ref, si_ref, carry_ref):
        @pl.when(pl.program_id(1) == 0)
        def _():
            carry_ref[...] = jnp.zeros_like(carry_ref)

        u = x_ref[...].astype(_MXU_DTYPE)
        sr_ref[...] = _mxu(u, br_ref[...], 1, 0)
        si_ref[...] = _mxu(u, bi_ref[...], 1, 0)
        ar, ai = ar_ref[...], ai_ref[...]
        rowi = lax.broadcasted_iota(jnp.int32, (SUBLANE, w), 0)
        pr, pi, a2, a4 = _power_table(ar, ai, rowi + 1)
        steps = [(s, jnp.where(rowi >= s, fr, 0.0), jnp.where(rowi >= s, fi, 0.0)) for s, (fr, fi) in ((1, (ar, ai)), (2, a2), (4, a4))]

        def tile(i, carry):
            cr, ci = carry
            rows = pl.ds(pl.multiple_of(i * SUBLANE, SUBLANE), SUBLANE)
            xr, xi = sr_ref[rows, :], si_ref[rows, :]
            for s, fr, fi in steps:
                zr, zi = pltpu.roll(xr, s, 0), pltpu.roll(xi, s, 0)
                xr, xi = xr + fr * zr - fi * zi, xi + fr * zi + fi * zr
            xr, xi = xr + pr * cr - pi * ci, xi + pr * ci + pi * cr
            sr_ref[rows, :] = xr
            si_ref[rows, :] = xi
            return xr[SUBLANE - 1:SUBLANE, :], xi[SUBLANE - 1:SUBLANE, :]

        cr, ci = lax.fori_loop(0, tr // SUBLANE, tile, (carry_ref[0:1, :], carry_ref[1:2, :]))
        carry_ref[0:1, :] = cr
        carry_ref[1:2, :] = ci

    out = jax.ShapeDtypeStruct((T, S5_WIDTH), F32)
    return pl.pallas_call(
        body, name=name, grid=(ncol, nt),
        in_specs=[pl.BlockSpec((tr, x_cols[1]), lambda j, t: (t, x_cols[0] // x_cols[1])),
                  pl.BlockSpec((S5_DIM, w), lambda j, t: (0, j)), pl.BlockSpec((S5_DIM, w), lambda j, t: (0, j)),
                  pl.BlockSpec((1, w), lambda j, t: (0, j)), pl.BlockSpec((1, w), lambda j, t: (0, j))],
        out_specs=[pl.BlockSpec((tr, w), lambda j, t: (t, j))] * 2,
        out_shape=[out, out],
        scratch_shapes=[pltpu.VMEM((2, w), F32)],
        compiler_params=_cparams(("parallel", "arbitrary")),
    )(x, b_re, b_im, a_re, a_im)


def _s5_scan_bwd(dy, c_re, c_im, s_re, s_im, a_re, a_im, *, name):
    T = dy.shape[0]
    w, tr = S5_LANES, min(S5_ROWS, T)
    ncol, nt = S5_WIDTH // w, T // tr
    ntile = tr // SUBLANE

    def body(dy_ref, cr_ref, ci_ref, sr_ref, si_ref, ar_ref, ai_ref, lr_ref, li_ref, dar_ref, dai_ref, carry_ref):
        @pl.when(pl.program_id(1) == 0)
        def _():
            carry_ref[...] = jnp.zeros_like(carry_ref)
            dar_ref[...] = jnp.zeros_like(dar_ref)
            dai_ref[...] = jnp.zeros_like(dai_ref)

        dyb = dy_ref[...].astype(_MXU_DTYPE)
        lr_ref[...] = _mxu(dyb, cr_ref[...], 1, 1)
        li_ref[...] = _mxu(dyb, ci_ref[...], 1, 1)
        ar, ai = ar_ref[...], -ai_ref[...]
        rowi = lax.broadcasted_iota(jnp.int32, (SUBLANE, w), 0)
        pr, pi, a2, a4 = _power_table(ar, ai, SUBLANE - rowi)
        last = rowi == SUBLANE - 1
        steps = [(s, jnp.where(rowi < SUBLANE - s, fr, 0.0), jnp.where(rowi < SUBLANE - s, fi, 0.0))
                 for s, (fr, fi) in ((1, (ar, ai)), (2, a2), (4, a4))]

        def tile(i, carry):
            cr, ci, dar, dai = carry
            rows = pl.ds(pl.multiple_of((ntile - 1 - i) * SUBLANE, SUBLANE), SUBLANE)
            xr, xi = lr_ref[rows, :], li_ref[rows, :]
            for s, fr, fi in steps:
                zr, zi = pltpu.roll(xr, SUBLANE - s, 0), pltpu.roll(xi, SUBLANE - s, 0)
                xr, xi = xr + fr * zr - fi * zi, xi + fr * zi + fi * zr
            xr, xi = xr + pr * cr - pi * ci, xi + pr * ci + pi * cr
            lr_ref[rows, :] = xr
            li_ref[rows, :] = xi
            nr = jnp.where(last, cr, pltpu.roll(xr, SUBLANE - 1, 0))
            ni = jnp.where(last, ci, pltpu.roll(xi, SUBLANE - 1, 0))
            sr, si = sr_ref[rows, :], si_ref[rows, :]
            return xr[0:1, :], xi[0:1, :], dar + nr * sr + ni * si, dai + ni * sr - nr * si

        cr, ci, dar, dai = lax.fori_loop(
            0, ntile, tile, (carry_ref[0:1, :], carry_ref[1:2, :], jnp.zeros((SUBLANE, w), F32), jnp.zeros((SUBLANE, w), F32)))
        carry_ref[0:1, :] = cr
        carry_ref[1:2, :] = ci
        dar_ref[...] += dar
        dai_ref[...] += dai

    tok = pl.BlockSpec((tr, w), lambda j, t: (nt - 1 - t, j))
    vec = pl.BlockSpec((1, w), lambda j, t: (0, j))
    acc = pl.BlockSpec((SUBLANE, w), lambda j, t: (0, j))
    out = jax.ShapeDtypeStruct((T, S5_WIDTH), F32)
    accs = jax.ShapeDtypeStruct((SUBLANE, S5_WIDTH), F32)
    return pl.pallas_call(
        body, name=name, grid=(ncol, nt),
        in_specs=[pl.BlockSpec((tr, S5_DIM), lambda j, t: (nt - 1 - t, 0)),
                  pl.BlockSpec((w, S5_DIM), lambda j, t: (j, 0)), pl.BlockSpec((w, S5_DIM), lambda j, t: (j, 0)),
                  tok, tok, vec, vec],
        out_specs=[tok, tok, acc, acc],
        out_shape=[out, out, accs, accs],
        scratch_shapes=[pltpu.VMEM((2, w), F32)],
        compiler_params=_cparams(("parallel", "arbitrary")),
    )(dy, c_re, c_im, s_re, s_im, a_re, a_im)


ATTN_BLOCK = 1024
_NEG = -1e30


def _qk_cat(nope, rope):
    return jnp.concatenate([nope.astype(_MXU_DTYPE), rope.astype(_MXU_DTYPE)], axis=1)


_QK_SCALE = MLA_QK ** -0.5
_LOG2E = math.log2(math.e)


def _attn_scores(q, k, diagonal):
    s = _mxu(q, k, 1, 1) * (_QK_SCALE * _LOG2E)
    if diagonal:
        s = jnp.where(_tri(s.shape[0]), s, _NEG)
    return s


def _attn_fwd(q_all, q_rope, kv, k_rope, *, name):
    T = q_all.shape[0]
    tq = min(ATTN_BLOCK, T)
    nq = T // tq

    def body(qn_ref, qr_ref, kn_ref, v_ref, kr_ref, o_ref, lse_ref):
        i = pl.program_id(1)
        q = _qk_cat(qn_ref[...], qr_ref[0])

        def step(j, carry, diagonal):
            m, l, acc = carry
            ks = pl.ds(pl.multiple_of(j * tq, tq), tq)
            s = _attn_scores(q, _qk_cat(kn_ref[ks, :], kr_ref[ks, :]), diagonal)
            m_new = jnp.maximum(m, jnp.max(s, axis=-1, keepdims=True))
            p = jnp.exp2(s - m_new)
            alpha = jnp.exp2(m - m_new)
            return m_new, alpha * l + jnp.sum(p, axis=-1, keepdims=True), alpha * acc + _mxu(p, v_ref[ks, :], 1, 0)

        init = (jnp.full((tq, 1), _NEG, F32), jnp.zeros((tq, 1), F32), jnp.zeros((tq, MLA_V), F32))
        below = lax.fori_loop(0, i, functools.partial(step, diagonal=False), init)
        m, l, acc = step(i, below, diagonal=True)
        o_ref[...] = acc / l
        lse_ref[0] = m + jnp.log2(l)

    return pl.pallas_call(
        body, name=name, grid=(MLA_HEADS, nq),
        in_specs=[pl.BlockSpec((tq, MLA_NOPE), lambda h, i: (i, h)), pl.BlockSpec((1, tq, MLA_ROPE), lambda h, i: (h, i, 0)),
                  pl.BlockSpec((T, MLA_NOPE), lambda h, i: (0, 2 * h)), pl.BlockSpec((T, MLA_V), lambda h, i: (0, 2 * h + 1)),
                  pl.BlockSpec((T, MLA_ROPE), lambda h, i: (0, 0))],
        out_specs=[pl.BlockSpec((tq, MLA_V), lambda h, i: (i, h)), pl.BlockSpec((1, tq, 1), lambda h, i: (h, i, 0))],
        out_shape=[jax.ShapeDtypeStruct((T, MLA_HEADS * MLA_V), F32), jax.ShapeDtypeStruct((MLA_HEADS, T, 1), F32)],
        compiler_params=_cparams(("arbitrary", "arbitrary")),
    )(q_all, q_rope, kv, kv, k_rope)


def _attn_bwd(q_all, q_rope, kv, k_rope, o, lse, do, *, name):
    T = q_all.shape[0]
    tk = min(ATTN_BLOCK, T)
    nk = T // tk

    def body(qn_ref, qr_ref, kv_ref, kr_ref, o_ref, lse_ref, do_ref, dqn_ref, dqr_ref, dkv_ref, dkr_ref, delta_ref):
        h, j = pl.program_id(0), pl.program_id(1)

        @pl.when(j == 0)
        def _():
            dqn_ref[...] = jnp.zeros_like(dqn_ref)
            dqr_ref[...] = jnp.zeros_like(dqr_ref)
            delta_ref[...] = jnp.sum(do_ref[...] * o_ref[...], axis=-1, keepdims=True)

        @pl.when((j == 0) & (h == 0))
        def _():
            dkr_ref[...] = jnp.zeros_like(dkr_ref)

        krows = pl.ds(pl.multiple_of(j * tk, tk), tk)
        k = _qk_cat(kv_ref[:, :MLA_NOPE], kr_ref[krows, :])
        v = kv_ref[:, MLA_NOPE:].astype(_MXU_DTYPE)

        def step(i, carry, diagonal):
            dk, dv = carry
            qs = pl.ds(pl.multiple_of(i * tk, tk), tk)
            q, dob = _qk_cat(qn_ref[qs, :], qr_ref[0, qs, :]), do_ref[qs, :].astype(_MXU_DTYPE)
            p = jnp.exp2(_attn_scores(q, k, diagonal) - lse_ref[0, qs, :])
            ds = p * (_mxu(dob, v, 1, 1) - delta_ref[qs, :]) * _QK_SCALE
            dq = _mxu(ds, k, 1, 0)
            dqn_ref[qs, :] += dq[:, :MLA_NOPE]
            dqr_ref[0, qs, :] += dq[:, MLA_NOPE:]
            return dk + _mxu(ds, q, 0, 0), dv + _mxu(p, dob, 0, 0)

        on_diagonal = step(j, (jnp.zeros((tk, MLA_QK), F32), jnp.zeros((tk, MLA_V), F32)), diagonal=True)
        dk, dv = lax.fori_loop(j + 1, nk, functools.partial(step, diagonal=False), on_diagonal)
        dkv_ref[:, :MLA_NOPE] = dk[:, :MLA_NOPE].astype(dkv_ref.dtype)
        dkv_ref[:, MLA_NOPE:] = dv.astype(dkv_ref.dtype)
        dkr_ref[krows, :] += dk[:, MLA_NOPE:]

    head_cols = pl.BlockSpec((T, MLA_NOPE), lambda h, j: (0, h))
    head_rope = pl.BlockSpec((1, T, MLA_ROPE), lambda h, j: (h, 0, 0))
    kv_spec = pl.BlockSpec((tk, MLA_NOPE + MLA_V), lambda h, j: (j, h))
    kr_spec = pl.BlockSpec((T, MLA_ROPE), lambda h, j: (0, 0))
    return pl.pallas_call(
        body, name=name, grid=(MLA_HEADS, nk),
        in_specs=[head_cols, head_rope, kv_spec, kr_spec, head_cols, pl.BlockSpec((1, T, 1), lambda h, j: (h, 0, 0)), head_cols],
        out_specs=[head_cols, head_rope, kv_spec, kr_spec],
        out_shape=[jax.ShapeDtypeStruct((T, MLA_HEADS * MLA_NOPE), F32), jax.ShapeDtypeStruct((MLA_HEADS, T, MLA_ROPE), F32),
                   jax.ShapeDtypeStruct((T, MLA_HEADS * (MLA_NOPE + MLA_V)), BF16), jax.ShapeDtypeStruct((T, MLA_ROPE), F32)],
        scratch_shapes=[pltpu.VMEM((T, 1), F32)],
        compiler_params=_cparams(("arbitrary", "arbitrary")),
    )(q_all, q_rope, kv, k_rope, o, lse, do)


def _s5_discretize(a_re, a_im, log_dt, bt_re, bt_im, lb_logits):
    dt = jnp.exp(log_dt)
    mag = jnp.exp(a_re * dt)
    abr, abi = mag * jnp.cos(a_im * dt), mag * jnp.sin(a_im * dt)
    den = a_re * a_re + a_im * a_im
    xr, xi = abr - 1.0, abi
    cr = ((xr * a_re + xi * a_im) / den)[:, None, :]
    ci = ((xi * a_re - xr * a_im) / den)[:, None, :]
    e = jnp.exp(lb_logits - jnp.max(lb_logits, axis=0, keepdims=True))
    lb = e[0:1, :] / jnp.sum(e, axis=0, keepdims=True)
    return abr, abi, cr * bt_re - ci * bt_im, cr * bt_im + ci * bt_re, lb


def _whole(shape):
    return pl.BlockSpec(shape, lambda: (0,) * len(shape))


def _s5_params_fwd(a_re, a_im, log_dt, bt_re, bt_im, lb_logits):
    ins = (a_re, a_im, log_dt, bt_re, bt_im, lb_logits)
    outs = [jax.ShapeDtypeStruct(s, F32) for s in (a_re.shape, a_re.shape, bt_re.shape, bt_re.shape, (1, lb_logits.shape[1]))]

    def body(*refs):
        res = _s5_discretize(*[r[...] for r in refs[:6]])
        for o_ref, o in zip(refs[6:], res):
            o_ref[...] = o

    return pl.pallas_call(body, name="s5_params_fwd", in_specs=[_whole(a.shape) for a in ins],
                          out_specs=[_whole(o.shape) for o in outs], out_shape=outs, compiler_params=_cparams())(*ins)


def _s5_params_bwd(a_re, a_im, log_dt, bt_re, bt_im, lb_logits, d_abr, d_abi, d_bbr, d_bbi, d_lb):
    ins = (a_re, a_im, log_dt, bt_re, bt_im, lb_logits, d_abr, d_abi, d_bbr, d_bbi, d_lb)
    outs = [jax.ShapeDtypeStruct(a.shape, F32) for a in ins[:6]]

    def body(*refs):
        _, vjp = jax.vjp(_s5_discretize, *[r[...] for r in refs[:6]])
        for o_ref, o in zip(refs[11:], vjp(tuple(r[...] for r in refs[6:11]))):
            o_ref[...] = o

    return pl.pallas_call(body, name="s5_params_bwd", in_specs=[_whole(a.shape) for a in ins],
                          out_specs=[_whole(o.shape) for o in outs], out_shape=outs, compiler_params=_cparams())(*ins)


ADAMW_WHOLE_BYTES = 1024 * 1024


def _adamw(w, g, m, v, *, name, layer=0, n_layers=1, into=None):
    R, C = w.shape
    whole = R % SUBLANE != 0 or R * C * w.dtype.itemsize <= ADAMW_WHOLE_BYTES
    tr = R if whole else _pick(R, (256, 128, 64, 32, 16, 8))
    slabs = g.shape[0]
    n_prev = 0 if into is None else len(into)

    def body(w_ref, g_ref, m_ref, v_ref, *rest):
        g_out, d_ref, mo_ref, vo_ref = rest[n_prev:]
        gv = g_ref[0].astype(F32)
        for s in range(1, slabs):
            gv = gv + g_ref[s].astype(F32)
        m2 = ADAM_B1 * m_ref[...] + (1.0 - ADAM_B1) * gv
        v2 = ADAM_B2 * v_ref[...] + (1.0 - ADAM_B2) * (gv * gv)
        m_hat = m2 / (1.0 - ADAM_B1 ** ADAM_STEP)
        v_hat = v2 / (1.0 - ADAM_B2 ** ADAM_STEP)
        g_out[0] = gv
        d_ref[0] = -ADAM_LR * (m_hat / (jnp.sqrt(v_hat) + ADAM_EPS) + ADAM_WD * w_ref[...])
        mo_ref[0] = m2
        vo_ref[0] = v2

    spec = pl.BlockSpec((tr, C), lambda i: (i, 0))
    out_spec = pl.BlockSpec((1, tr, C), lambda i: (layer, i, 0))
    out = jax.ShapeDtypeStruct((n_layers, R, C), F32)
    return pl.pallas_call(
        body, name=name, grid=(R // tr,),
        in_specs=[spec, pl.BlockSpec((slabs, tr, C), lambda i: (0, i, 0)), spec, spec] + [pl.BlockSpec(memory_space=pl.ANY)] * n_prev,
        out_specs=[out_spec] * 4, out_shape=[out] * 4, input_output_aliases={4 + k: k for k in range(n_prev)},
        compiler_params=_cparams(("parallel",)))(w, g, m, v, *(into or ()))


N_CHIPS = 4
N_CORES = 2


_FLIPS = tuple((dx, dy, dc) for dx in (0, 1) for dy in (0, 1) for dc in (0, 1) if (dx, dy, dc) != (0, 0, 0))


_HBM = pl.BlockSpec(memory_space=pltpu.HBM)
_SEM = pl.BlockSpec(memory_space=pltpu.SEMAPHORE)
_SPLIT_COPY = pltpu.CompilerParams(has_side_effects=pltpu.SideEffectType.DATAFLOW_SIDE_EFFECTING)


def _exchange_copies(src_refs, land_refs, send_sems, recv_sems, scatter, arriving):
    x, y, c = lax.axis_index("x"), lax.axis_index("y"), lax.axis_index("c")
    me_chip = 2 * x + y
    copies = []
    for a, (s_ref, l_ref) in enumerate(zip(src_refs, land_refs)):
        for j, (dx, dy, dc) in enumerate(_FLIPS):
            px, py, pc = (1 - x if dx else x), (1 - y if dy else y), (1 - c if dc else c)
            k = a * len(_FLIPS) + j
            p_chip = 2 * px + py
            copies.append(pltpu.make_async_remote_copy(
                src_ref=s_ref.at[p_chip, pc] if scatter else s_ref, dst_ref=l_ref.at[p_chip, pc] if arriving else l_ref.at[me_chip, c],
                send_sem=send_sems.at[k], recv_sem=recv_sems.at[k], device_id=(px, py, pc), device_id_type=MESH))
    return copies


def _exchange_start(srcs, *, scatter, name, after=()):
    n_arr = len(srcs)
    n_sem = n_arr * len(_FLIPS)
    n_in = 2 * n_arr + len(after)
    lands = [lax.empty(s.shape if scatter else (N_CHIPS, N_CORES) + s.shape, s.dtype) for s in srcs]

    def body(*refs):
        src_refs, land_refs = refs[:n_arr], refs[n_arr:2 * n_arr]
        for cp in _exchange_copies(src_refs, land_refs, refs[n_in], refs[n_in + 1], scatter, arriving=False):
            cp.start()
        refs[-1][...] = jnp.zeros_like(refs[-1])

    thru = [pltpu.HBM(a.shape, a.dtype) for a in srcs + lands]
    outs = pl.pallas_call(
        body, name=name,
        out_shape=(pltpu.SemaphoreType.DMA((n_sem,)), pltpu.SemaphoreType.DMA((n_sem,)), *thru,
                   jax.ShapeDtypeStruct((SUBLANE, LANE), F32)),
        in_specs=[_HBM] * (2 * n_arr) + [pl.BlockSpec(memory_space=pl.ANY)] * len(after),
        out_specs=(_SEM, _SEM, *[_HBM] * (2 * n_arr), pl.BlockSpec(memory_space=pltpu.VMEM)),
        input_output_aliases={i: 2 + i for i in range(2 * n_arr)}, compiler_params=_SPLIT_COPY,
    )(*[pltpu.with_memory_space_constraint(a, pltpu.HBM) for a in srcs + lands], *after)
    return outs[0], outs[1], list(outs[2:2 + n_arr]), list(outs[2 + n_arr:2 + 2 * n_arr]), outs[-1]


def _exchange_wait(started, after, *, scatter, name):
    send_sems, recv_sems, srcs, lands, _ = started
    n_arr = len(srcs)

    def body(*refs):
        src_refs, land_refs = refs[:n_arr], refs[n_arr:2 * n_arr]
        for cp in _exchange_copies(src_refs, land_refs, refs[2 * n_arr], refs[2 * n_arr + 1], scatter, arriving=True):
            cp.wait_send()
            cp.wait_recv()

    outs = pl.pallas_call(
        body, name=name, out_shape=[pltpu.HBM(a.shape, a.dtype) for a in srcs + lands],
        in_specs=[_HBM] * (2 * n_arr) + [_SEM, _SEM, pl.BlockSpec(memory_space=pl.ANY)], out_specs=[_HBM] * (2 * n_arr),
        input_output_aliases={i: i for i in range(2 * n_arr)}, compiler_params=_SPLIT_COPY,
    )(*srcs, *lands, send_sems, recv_sems, after)
    return list(outs[:n_arr]), list(outs[n_arr:])


def _with_own(land, own):
    me_chip = 2 * lax.axis_index("x") + lax.axis_index("y")
    return lax.dynamic_update_slice(land, own[None, None], (me_chip, lax.axis_index("c")) + (0,) * own.ndim)


def _rms_fwd_fn(h, g):
    return (_rms(h, g),)


def _rms_bwd_fn(h, dhn, dres, g):
    _, vjp = jax.vjp(_rms, h, g)
    dh, dg = vjp(dhn)
    return dh + dres, dh + dres, dg


def _loss_fn(h, tgt, g):
    y, vjp = jax.vjp(_rms, h, g)
    diff = y - tgt
    dh, dg = vjp(diff * (1.0 / D_MODEL))
    return dh, dh, dg, (0.5 / D_MODEL) * jnp.sum(diff * diff, axis=0, keepdims=True)


def _s5_act(ys, u, d):
    return _gelu(ys + d * u)


def _s5_gate(z, gl, b):
    return z * _sigmoid(gl + b)


def _s5_act_fn(ys, u, d):
    return (_s5_act(ys, u, d),)


def _s5_mix_fn(ya, z, gl, b):
    return (jnp.concatenate([ya, _s5_gate(z, gl, b)], axis=1),)


def _s5_gate_bwd_fn(z, gl, dyb, b):
    _, vjp = jax.vjp(_s5_gate, z, gl, b)
    return vjp(dyb)


def _s5_act_bwd_fn(ys, u, dz1, dz2, d):
    _, vjp = jax.vjp(_s5_act, ys, u, d)
    return vjp(dz1 + dz2)


def _dproj_fn(dq, df, di, dg, du1, du2):
    return (jnp.concatenate([dq, df, di, dg, du1 + du2], axis=1),)


def _rope_pair(r1, r2, pos, freqs):
    ang = pos.astype(F32) * freqs
    c, s = jnp.cos(ang), jnp.sin(ang)
    return r1 * c - r2 * s, r1 * s + r2 * c


_ODD_SPLITS = (0, MLA_Q_RANK, MLA_Q_RANK + MLA_KV_RANK, MLA_Q_RANK + MLA_KV_RANK + LANE, ODD_IN_PAD)


def _mla_prep(cq, ckv, k1, k2, qg, kvg, pos, freqs):
    ko1, ko2 = _rope_pair(k1, k2, pos, freqs)
    return _rms(cq, qg), _rms(ckv, kvg), ko1, ko2


def _mla_prep_fn(proj, pos, qg, kvg, freqs):
    parts = [proj[:, a:b] for a, b in zip(_ODD_SPLITS[:-1], _ODD_SPLITS[1:])]
    return _mla_prep(*parts, qg, kvg, pos, freqs)


def _mla_prep_bwd_fn(proj, pos, dqn, dkvn, dko1, dko2, qg, kvg, freqs):
    parts = [proj[:, a:b] for a, b in zip(_ODD_SPLITS[:-1], _ODD_SPLITS[1:])]
    _, vjp = jax.vjp(lambda *a: _mla_prep(*a, pos, freqs), *parts, qg, kvg)
    dcq, dckv, dk1, dk2, dqg, dkvg = vjp((dqn, dkvn, dko1, dko2))
    return jnp.concatenate([dcq, dckv, dk1, dk2], axis=1), dqg, dkvg


ROPE_ROWS = 512


def _rope_heads(x, pos, freqs):
    half = MLA_ROPE // 2
    reps = x.shape[1] // LANE
    ang = pos.astype(F32) * freqs[:, :LANE]
    cos = jnp.concatenate([jnp.cos(ang)] * reps, axis=1)
    sin = jnp.concatenate([jnp.sin(ang)] * reps, axis=1)
    first = (lax.broadcasted_iota(jnp.int32, x.shape, 1) % MLA_ROPE) < half
    other = jnp.where(first, pltpu.roll(x, x.shape[1] - half, 1), pltpu.roll(x, half, 1))
    return x * cos + other * jnp.where(first, -sin, sin)


def _rope_q_fwd(q_all, pos, freqs, *, name):
    T = q_all.shape[0]
    tm = min(ROPE_ROWS, T)
    nope_w, rope_w = MLA_HEADS * MLA_NOPE, MLA_HEADS * MLA_ROPE

    def body(r_ref, pos_ref, f_ref, o_ref):
        out = _rope_heads(r_ref[...], pos_ref[...], f_ref[...])
        for h in range(MLA_HEADS):
            o_ref[h] = out[:, h * MLA_ROPE:(h + 1) * MLA_ROPE]

    return pl.pallas_call(
        body, name=name, grid=(T // tm,),
        in_specs=[pl.BlockSpec((tm, rope_w), lambda i: (i, nope_w // rope_w)), pl.BlockSpec((tm, 1), lambda i: (i, 0)),
                  pl.BlockSpec((1, rope_w), lambda i: (0, 0))],
        out_specs=pl.BlockSpec((MLA_HEADS, tm, MLA_ROPE), lambda i: (0, i, 0)),
        out_shape=jax.ShapeDtypeStruct((MLA_HEADS, T, MLA_ROPE), F32),
        compiler_params=_cparams(("parallel",)),
    )(q_all, pos, freqs)


def _rope_q_bwd(dq_nope, dq_rope, pos, freqs, *, name):
    T = dq_nope.shape[0]
    tm = min(ROPE_ROWS, T)
    nope_w, rope_w = MLA_HEADS * MLA_NOPE, MLA_HEADS * MLA_ROPE

    def body(dn_ref, dr_ref, pos_ref, f_ref, o_ref):
        d_out = jnp.concatenate([dr_ref[h] for h in range(MLA_HEADS)], axis=1)
        o_ref[:, :nope_w] = dn_ref[...].astype(o_ref.dtype)
        o_ref[:, nope_w:] = _rope_heads(d_out, pos_ref[...], -f_ref[...]).astype(o_ref.dtype)

    return pl.pallas_call(
        body, name=name, grid=(T // tm,),
        in_specs=[pl.BlockSpec((tm, nope_w), lambda i: (i, 0)), pl.BlockSpec((MLA_HEADS, tm, MLA_ROPE), lambda i: (0, i, 0)),
                  pl.BlockSpec((tm, 1), lambda i: (i, 0)), pl.BlockSpec((1, rope_w), lambda i: (0, 0))],
        out_specs=pl.BlockSpec((tm, nope_w + rope_w), lambda i: (i, 0)),
        out_shape=jax.ShapeDtypeStruct((T, nope_w + rope_w), BF16),
        compiler_params=_cparams(("parallel",)),
    )(dq_nope, dq_rope, pos, freqs)


W_NAMES = ("norm_mix_g", "norm_ffn_g", "final_norm_g", "even_w_in", "hgrn_lb_logits", "hgrn_norm_g", "s5_a_re", "s5_a_im",
           "s5_log_dt", "s5_b_re", "s5_b_im", "s5_c_re", "s5_c_im", "s5_d", "s5_w_glu", "s5_b_glu", "even_w_out", "odd_w_in",
           "mla_q_norm_g", "mla_w_uq", "mla_kv_norm_g", "mla_w_ukv", "odd_w_out", "ffn_w_in", "ffn_conv_w", "ffn_conv_b",
           "ffn_w_out")
BIG_UNITS = (("even_w_in", 0, True), ("s5_w_glu", 0, False), ("even_w_out", 0, False), ("odd_w_in", 0, False),
             ("mla_w_uq", 0, True), ("mla_w_ukv", 0, True), ("odd_w_out", 0, False),
             ("ffn_w_in", 0, True), ("ffn_w_in", 1, True), ("ffn_w_out", 0, False), ("ffn_w_out", 1, False))
BIG_NAMES = tuple(dict.fromkeys(u[0] for u in BIG_UNITS))
SMALL_SHARDED = (("mla_q_norm_g", 1), ("mla_kv_norm_g", 1), ("ffn_conv_w", 2))
SMALL_SHARDED_NAMES = tuple(s[0] for s in SMALL_SHARDED)
REPLICATED = tuple(n for n in W_NAMES if n not in BIG_NAMES + SMALL_SHARDED_NAMES)
REPLICATED_LATE = ("norm_mix_g",)
REPLICATED_EARLY = tuple(n for n in REPLICATED if n not in REPLICATED_LATE)


def _pack(flats, cols, row_mult):
    flat = jnp.concatenate(flats, axis=-1)
    pad = (-flat.shape[-1]) % (cols * row_mult)
    flat = jnp.pad(flat, [(0, 0)] * (flat.ndim - 1) + [(0, pad)])
    return flat.reshape(flat.shape[:-1] + (-1, cols))


def _unpack(flat, shapes):
    out, off = [], 0
    for shp in shapes:
        n = int(np.prod(shp))
        out.append(flat[..., off:off + n].reshape(flat.shape[:-1] + tuple(shp)))
        off += n
    return out


UNIT_TRANSPOSED = {(n, l): t for n, l, t in BIG_UNITS}


def _unit_block(arrs, unit):
    blk = arrs[unit[0]][unit[1]]
    return blk.T if UNIT_TRANSPOSED[unit] else blk
STAGES = ((("even_w_in", 0),),
          (("s5_w_glu", 0), ("even_w_out", 0)),
          (("ffn_w_in", 0), ("ffn_w_out", 0)),
          (("odd_w_in", 0), ("mla_w_uq", 0), ("mla_w_ukv", 0), ("odd_w_out", 0)),
          (("ffn_w_in", 1), ("ffn_w_out", 1)))


def _gather_start(w, stage, with_small, after):
    srcs = [_unit_block(w, unit).astype(BF16) for unit in STAGES[stage]]
    if with_small:
        srcs.append(_pack([w[n].reshape(-1) for n in SMALL_SHARDED_NAMES], LANE, SUBLANE))
    return _exchange_start(srcs, scatter=False, name=f"gather_start_{stage}", after=after)


def _gather_finish(started, after, w, stage, with_small):
    srcs, lands = _exchange_wait(started, after, scatter=False, name=f"gather_wait_{stage}")
    lands = [_with_own(land, src) for land, src in zip(lands, srcs)]
    big = {unit: g.reshape(N_DEV * g.shape[2], g.shape[3]) for unit, g in zip(STAGES[stage], lands)}
    if not with_small:
        return big
    parts = _unpack(lands[-1].reshape(N_DEV, -1), [w[n].shape for n in SMALL_SHARDED_NAMES])
    small = {}
    for (n, ax), p in zip(SMALL_SHARDED, parts):
        shp = list(w[n].shape)
        shp[ax] *= N_DEV
        small[n] = jnp.moveaxis(p, 0, ax).reshape(shp)
    return big, small


def _scatter_start(g_big, stage, extra=()):
    srcs = []
    for unit in STAGES[stage]:
        g = g_big[unit].astype(BF16)
        srcs.append(g.reshape(N_CHIPS, N_CORES, g.shape[0] // N_DEV, g.shape[1]))
    return _exchange_start(srcs + list(extra), scatter=True, name=f"scatter_start_{stage}")


def _scatter_finish(started, after, stage):
    srcs, lands = _exchange_wait(started, after, scatter=True, name=f"scatter_wait_{stage}")
    me_chip, c = 2 * lax.axis_index("x") + lax.axis_index("y"), lax.axis_index("c")
    outs = []
    for land, src in zip(lands, srcs):
        own = lax.dynamic_slice(src, (me_chip, c) + (0,) * (src.ndim - 2), (1, 1) + src.shape[2:])[0, 0]
        outs.append(_with_own(land, own).reshape((N_DEV,) + land.shape[2:]))
    return outs


def _small_sharded_pack(g_small, w):
    flats = []
    for n, ax in SMALL_SHARDED:
        shp = list(w[n].shape)
        g = g_small[n].astype(F32).reshape(shp[:ax] + [N_DEV] + shp[ax:])
        flats.append(jnp.moveaxis(g, ax, 0).reshape(N_DEV, -1))
    small = _pack(flats, LANE, SUBLANE)
    return small.reshape((N_CHIPS, N_CORES) + small.shape[1:])


def _replicated_pack(g_repl, names):
    vec = _pack([g_repl[n].reshape(-1).astype(F32) for n in names], LANE, SUBLANE)
    return jnp.broadcast_to(vec, (N_CHIPS, N_CORES) + vec.shape)


def _block_diag(blocks):
    G, a, b = blocks.shape
    return jnp.einsum('gab,gk->gakb', blocks, jnp.eye(G, dtype=blocks.dtype)).reshape(G * a, G * b)


def _diag_blocks(mat, a, b):
    G = mat.shape[0] // a
    return jnp.einsum('gagb->gab', mat.reshape(G, a, G, b))


def _ffn_fwd(h, g, w_in_t, conv_w, conv_b, w_out, tag):
    hn, = _rows(_rms_fwd_fn, [h], [g], [(D_MODEL, BF16)], [], name=f"ffn{tag}_norm")
    au = _mm(hn, w_in_t, tb=True, out_dtype=BF16, name=f"ffn{tag}_in")
    z = _ffn_mid_fwd(au, conv_w, conv_b, name=f"ffn{tag}_mid")
    return _mm(z, w_out, res=h, name=f"ffn{tag}_out"), (hn, au, z)


def _ffn_bwd(h, dh, dh_mxu, saved, g, w_in_t, conv_w, conv_b, w_out, tag, deps=()):
    hn, au, z = saved
    dz = _mm(dh_mxu, w_out, tb=True, out_dtype=BF16, deps=deps, name=f"ffn{tag}_dz")
    dw_out = _mm(z, dh_mxu, ta=True, out_dtype=BF16, name=f"ffn{tag}_dwout")
    da, du, dcw, dcb = _ffn_mid_bwd(au, dz, conv_w, conv_b, name=f"ffn{tag}_dmid")
    dhn = _mm(da, w_in_t, b_rows=(0, D_FF), name=f"ffn{tag}_dhn_a")
    dhn = _mm(du, w_in_t, b_rows=(D_FF, D_FF), res=dhn, name=f"ffn{tag}_dhn_u")
    dw_in = _mm(da, hn, ta=True, out_dtype=BF16, out_rows=(0, 2 * D_FF), name=f"ffn{tag}_dwin_a")
    dw_in = _mm(du, hn, ta=True, out_dtype=BF16, out_rows=(D_FF, 2 * D_FF), into=dw_in, name=f"ffn{tag}_dwin_u")
    dh_in, dh_in_mxu, dg = _rows(_rms_bwd_fn, [h, dhn, dh], [g], [(D_MODEL, F32), (D_MODEL, BF16)], [(1, D_MODEL)],
                                 name=f"ffn{tag}_dnorm")
    return dh_in, dh_in_mxu, dict(g=dg, w_in=dw_in, conv_w=dcw, conv_b=dcb, w_out=dw_out)


def kernel(x, positions, norm_mix_g, norm_ffn_g, final_norm_g, even_w_in, hgrn_lb_logits, hgrn_norm_g, s5_a_re, s5_a_im, s5_log_dt, s5_b_re, s5_b_im, s5_c_re, s5_c_im, s5_d, s5_w_glu, s5_b_glu, even_w_out, odd_w_in, mla_q_norm_g, mla_w_uq, mla_kv_norm_g, mla_w_ukv, odd_w_out, ffn_w_in, ffn_conv_w, ffn_conv_b, ffn_w_out, loss_target, m_norm_mix_g, m_norm_ffn_g, m_final_norm_g, m_even_w_in, m_hgrn_lb_logits, m_hgrn_norm_g, m_s5_a_re, m_s5_a_im, m_s5_log_dt, m_s5_b_re, m_s5_b_im, m_s5_c_re, m_s5_c_im, m_s5_d, m_s5_w_glu, m_s5_b_glu, m_even_w_out, m_odd_w_in, m_mla_q_norm_g, m_mla_w_uq, m_mla_kv_norm_g, m_mla_w_ukv, m_odd_w_out, m_ffn_w_in, m_ffn_conv_w, m_ffn_conv_b, m_ffn_w_out, v_norm_mix_g, v_norm_ffn_g, v_final_norm_g, v_even_w_in, v_hgrn_lb_logits, v_hgrn_norm_g, v_s5_a_re, v_s5_a_im, v_s5_log_dt, v_s5_b_re, v_s5_b_im, v_s5_c_re, v_s5_c_im, v_s5_d, v_s5_w_glu, v_s5_b_glu, v_even_w_out, v_odd_w_in, v_mla_q_norm_g, v_mla_w_uq, v_mla_kv_norm_g, v_mla_w_ukv, v_odd_w_out, v_ffn_w_in, v_ffn_conv_w, v_ffn_conv_b, v_ffn_w_out):
    given = dict(locals())
    w = {n: given[n] for n in W_NAMES}
    mom = {n: given["m_" + n] for n in W_NAMES}
    var = {n: given["v_" + n] for n in W_NAMES}
    T = x.shape[1]
    h0 = x[0]
    tgt = loss_target[0]
    pos = positions.reshape(T, 1)

    gathers = []
    for s in range(len(STAGES)):
        gathers.append(_gather_start(w, s, with_small=(s == 1), after=[g[4] for g in gathers[-1:]]))
    half = MLA_ROPE // 2
    kr0 = MLA_Q_RANK + MLA_KV_RANK
    freqs = ROPE_THETA ** (-jnp.arange(0, MLA_ROPE, 2, dtype=F32) / MLA_ROPE)
    freqs_q = jnp.tile(jnp.concatenate([freqs, freqs]), MLA_HEADS)[None, :]
    freqs_k = jnp.concatenate([freqs, jnp.zeros((LANE - half,), F32)])[None, :]

    sp_in = (s5_a_re[0], s5_a_im[0], s5_log_dt[0][:, None], s5_b_re[0].transpose(0, 2, 1), s5_b_im[0].transpose(0, 2, 1),
             hgrn_lb_logits)
    abr, abi, bbt_re, bbt_im, lb0 = _s5_params_fwd(*sp_in)
    a_re, a_im = abr.reshape(1, S5_WIDTH), abi.reshape(1, S5_WIDTH)
    bb_re, bb_im = _block_diag(bbt_re).astype(BF16), _block_diag(bbt_im).astype(BF16)
    c_re = _block_diag(s5_c_re[0].transpose(0, 2, 1)).astype(BF16)
    c_im_neg = _block_diag(-s5_c_im[0].transpose(0, 2, 1)).astype(BF16)
    u_cols = (4 * HGRN_DIM, S5_DIM)

    hn0, = _rows(_rms_fwd_fn, [h0], [norm_mix_g[0:1]], [(D_MODEL, BF16)], [], name="mix0_norm", deps=[gathers[-1][4]])
    full = _gather_finish(gathers[0], hn0, w, 0, False)
    w_ein_t = full["even_w_in", 0]
    proj = _mm(hn0, w_ein_t, tb=True, name="even_in")
    y_a, states = _hgrn_fwd(proj, lb0, hgrn_norm_g, name="hgrn_fwd")
    s_re, s_im = _s5_scan_fwd(proj, u_cols, bb_re, bb_im, a_re, a_im, name="s5_scan_fwd")
    more, full_small = _gather_finish(gathers[1], s_re, w, 1, True)
    w_glu, w_eout = more["s5_w_glu", 0], more["even_w_out", 0]
    qg, kvg, conv_w = full_small["mla_q_norm_g"], full_small["mla_kv_norm_g"], full_small["ffn_conv_w"]
    ys = _mm(s_im, c_im_neg, res=_mm(s_re, c_re, name="s5_y_re"), name="s5_y_im")
    z5, = _rows(_s5_act_fn, [ys, (proj,) + u_cols], [s5_d], [(S5_DIM, F32)], [], name="s5_act")
    gl = _mm(z5, w_glu, name="s5_glu")
    mixin, = _rows(_s5_mix_fn, [y_a, z5, gl], [s5_b_glu], [(D_MODEL, BF16)], [], name="s5_mix")
    h1 = _mm(mixin, w_eout, res=h0, name="even_out")
    full.update(_gather_finish(gathers[2], h1, w, 2, False))
    w_fin, w_fout = [full["ffn_w_in", 0]], [full["ffn_w_out", 0]]
    h2, ffn0_saved = _ffn_fwd(h1, norm_ffn_g[0:1], w_fin[0], conv_w[0], ffn_conv_b[0:1], w_fout[0], 0)

    full.update(_gather_finish(gathers[3], h2, w, 3, False))
    w_oin, w_ukv_t, w_oout = full["odd_w_in", 0], full["mla_w_ukv", 0], full["odd_w_out", 0]
    zpad = jnp.zeros((D_MODEL, LANE - half), BF16)
    w_oin_pad = jnp.concatenate([w_oin[:, :kr0], w_oin[:, kr0:kr0 + half], zpad, w_oin[:, kr0 + half:], zpad], axis=1)
    w_uq3 = full["mla_w_uq", 0].reshape(MLA_HEADS, MLA_QK, MLA_Q_RANK)
    w_uq_perm_t = jnp.concatenate([w_uq3[:, :MLA_NOPE].reshape(-1, MLA_Q_RANK),
                                   w_uq3[:, MLA_NOPE:].reshape(-1, MLA_Q_RANK)], axis=0)
    hn1, = _rows(_rms_fwd_fn, [h2], [norm_mix_g[1:2]], [(D_MODEL, BF16)], [], name="mix1_norm")
    proj_o = _mm(hn1, w_oin_pad, name="odd_in")
    qn, kvn, ko1, ko2 = _rows(_mla_prep_fn, [proj_o, pos], [qg, kvg, freqs_k],
                              [(MLA_Q_RANK, BF16), (MLA_KV_RANK, BF16), (LANE, F32), (LANE, F32)], [], name="mla_prep")
    q_all = _mm(qn, w_uq_perm_t, tb=True, name="mla_uq")
    kv = _mm(kvn, w_ukv_t, tb=True, out_dtype=BF16, name="mla_ukv")
    nope_w = MLA_HEADS * MLA_NOPE
    q_rope = _rope_q_fwd(q_all, pos, freqs_q, name="mla_rope_q")
    k_rope = jnp.concatenate([ko1[:, :half], ko2[:, :half]], axis=1)
    o, lse = _attn_fwd(q_all, q_rope, kv, k_rope, name="attn_fwd")
    h3 = _mm(o, w_oout, res=h2, name="odd_out")
    full.update(_gather_finish(gathers[4], h3, w, 4, False))
    w_fin.append(full["ffn_w_in", 1])
    w_fout.append(full["ffn_w_out", 1])
    h4, ffn1_saved = _ffn_fwd(h3, norm_ffn_g[1:2], w_fin[1], conv_w[1], ffn_conv_b[1:2], w_fout[1], 1)

    dh4, dh4_mxu, d_final_g, loss_cols = _rows(_loss_fn, [h4, tgt], [final_norm_g[None, :]], [(D_MODEL, F32), (D_MODEL, BF16)],
                                               [(1, D_MODEL), (1, D_MODEL)], name="loss_head")
    loss = lax.psum(jnp.sum(loss_cols), ("x", "y", "c"))

    dh3, dh3_mxu, gf1 = _ffn_bwd(h3, dh4, dh4_mxu, ffn1_saved, norm_ffn_g[1:2], w_fin[1], conv_w[1], ffn_conv_b[1:2], w_fout[1], 1)
    scatters = {4: _scatter_start({("ffn_w_in", 1): gf1["w_in"], ("ffn_w_out", 1): gf1["w_out"]}, 4)}
    do = _mm(dh3_mxu, w_oout, tb=True, deps=[scatters[4][4]], name="odd_out_dx")
    d_w_oout = _mm(o, dh3_mxu, ta=True, out_dtype=BF16, name="odd_out_dw")
    dq_nope, dq_rope, dkv, dk_rope = _attn_bwd(q_all, q_rope, kv, k_rope, o, lse, do, name="attn_bwd")
    lane_pad = ((0, 0), (0, LANE - half))
    dko1, dko2 = jnp.pad(dk_rope[:, :half], lane_pad), jnp.pad(dk_rope[:, half:], lane_pad)
    dq_all = _rope_q_bwd(dq_nope, dq_rope, pos, freqs_q, name="mla_rope_q_bwd")
    d_w_uq_perm_t = _mm(dq_all, qn, ta=True, out_dtype=BF16, name="mla_uq_dw")
    dqn = _mm(dq_all, w_uq_perm_t, name="mla_uq_dx")
    d_w_ukv_t = _mm(dkv, kvn, ta=True, out_dtype=BF16, name="mla_ukv_dw")
    dkvn = _mm(dkv, w_ukv_t, name="mla_ukv_dx")
    dproj_o, d_qg, d_kvg = _rows(_mla_prep_bwd_fn, [proj_o, pos, dqn, dkvn, dko1, dko2], [qg, kvg, freqs_k],
                                 [(ODD_IN_PAD, BF16)], [(1, MLA_Q_RANK), (1, MLA_KV_RANK)], name="mla_prep_bwd")
    d_w_oin_pad = _mm(hn1, dproj_o, ta=True, out_dtype=BF16, name="odd_in_dw")
    dhn1 = _mm(dproj_o, w_oin_pad, tb=True, name="odd_in_dx")
    dh2, dh2_mxu, d_mix_g1 = _rows(_rms_bwd_fn, [h2, dhn1, dh3], [norm_mix_g[1:2]], [(D_MODEL, F32), (D_MODEL, BF16)],
                                   [(1, D_MODEL)], name="mix1_dnorm")
    d_w_oin = jnp.concatenate([d_w_oin_pad[:, :kr0 + half], d_w_oin_pad[:, kr0 + LANE:kr0 + LANE + half]], axis=1)
    d_w_uq_t = jnp.concatenate([d_w_uq_perm_t[:nope_w].reshape(MLA_HEADS, MLA_NOPE, MLA_Q_RANK),
                                d_w_uq_perm_t[nope_w:].reshape(MLA_HEADS, MLA_ROPE, MLA_Q_RANK)], axis=1).reshape(-1, MLA_Q_RANK)
    scatters[3] = _scatter_start({("odd_w_in", 0): d_w_oin, ("mla_w_uq", 0): d_w_uq_t, ("mla_w_ukv", 0): d_w_ukv_t,
                                  ("odd_w_out", 0): d_w_oout}, 3)

    dh1, dh1_mxu, gf0 = _ffn_bwd(h1, dh2, dh2_mxu, ffn0_saved, norm_ffn_g[0:1], w_fin[0], conv_w[0], ffn_conv_b[0:1], w_fout[0], 0,
                                 deps=[scatters[3][4]])
    scatters[2] = _scatter_start({("ffn_w_in", 0): gf0["w_in"], ("ffn_w_out", 0): gf0["w_out"]}, 2)
    dmix = _mm(dh1_mxu, w_eout, tb=True, deps=[scatters[2][4]], name="even_out_dx")
    d_w_eout = _mm(mixin, dh1_mxu, ta=True, out_dtype=BF16, name="even_out_dw")
    dq, df, di, dg, d_lb0, d_hgrn_g = _hgrn_bwd(proj, lb0, hgrn_norm_g, states, dmix, name="hgrn_bwd")
    dz1, dgl, d_b_glu = _rows(_s5_gate_bwd_fn, [z5, gl, (dmix, HGRN_DIM, S5_DIM)], [s5_b_glu],
                              [(S5_DIM, F32), (S5_DIM, BF16)], [(1, S5_DIM)], name="s5_gate_bwd")
    dz2 = _mm(dgl, w_glu, tb=True, name="s5_glu_dx")
    d_w_glu = _mm(z5, dgl, ta=True, out_dtype=BF16, name="s5_glu_dw")
    dys, du1, d_s5_d = _rows(_s5_act_bwd_fn, [ys, (proj,) + u_cols, dz1, dz2], [s5_d],
                             [(S5_DIM, BF16), (S5_DIM, F32)], [(1, S5_DIM)], name="s5_act_bwd")
    d_c_re = _mm(s_re, dys, ta=True, name="s5_dc_re")
    d_c_im_neg = _mm(s_im, dys, ta=True, name="s5_dc_im")
    lam_re, lam_im, d_ar, d_ai = _s5_scan_bwd(dys, c_re, c_im_neg, s_re, s_im, a_re, a_im, name="s5_scan_bwd")
    du2 = _mm(lam_im, bb_im, tb=True, res=_mm(lam_re, bb_re, tb=True, name="s5_du_re"), name="s5_du_im")
    d_bb_re = _mm(proj, lam_re, ta=True, a_cols=u_cols, name="s5_dbb_re")
    d_bb_im = _mm(proj, lam_im, ta=True, a_cols=u_cols, name="s5_dbb_im")
    sp_g = _s5_params_bwd(*sp_in, d_ar.sum(0).reshape(S5_GROUPS, S5_STATE), d_ai.sum(0).reshape(S5_GROUPS, S5_STATE),
                          _diag_blocks(d_bb_re, S5_GROUP, S5_STATE), _diag_blocks(d_bb_im, S5_GROUP, S5_STATE), d_lb0)
    d_a_re, d_a_im, d_log_dt, d_bt_re, d_bt_im, d_lb_logits = sp_g
    g_small = dict(mla_q_norm_g=d_qg, mla_kv_norm_g=d_kvg, ffn_conv_w=jnp.stack([gf0["conv_w"], gf1["conv_w"]]))
    g_repl = dict(
        norm_ffn_g=jnp.concatenate([gf0["g"], gf1["g"]]),
        final_norm_g=d_final_g[0], hgrn_lb_logits=d_lb_logits, hgrn_norm_g=d_hgrn_g,
        s5_a_re=d_a_re[None], s5_a_im=d_a_im[None], s5_log_dt=d_log_dt[:, 0][None],
        s5_b_re=d_bt_re.transpose(0, 2, 1)[None], s5_b_im=d_bt_im.transpose(0, 2, 1)[None],
        s5_c_re=_diag_blocks(d_c_re, S5_STATE, S5_GROUP).transpose(0, 2, 1)[None],
        s5_c_im=-_diag_blocks(d_c_im_neg, S5_STATE, S5_GROUP).transpose(0, 2, 1)[None],
        s5_d=d_s5_d, s5_b_glu=d_b_glu, ffn_conv_b=jnp.concatenate([gf0["conv_b"], gf1["conv_b"]]))
    scatters[1] = _scatter_start({("s5_w_glu", 0): d_w_glu, ("even_w_out", 0): d_w_eout}, 1,
                                 extra=[_small_sharded_pack(g_small, w), _replicated_pack(g_repl, REPLICATED_EARLY)])
    dproj, = _rows(_dproj_fn, [dq, df, di, dg, du1, du2], [], [(EVEN_IN, BF16)], [], name="even_dproj", deps=[scatters[1][4]])
    d_w_ein_t = _mm(dproj, hn0, ta=True, out_dtype=BF16, name="even_in_dw")
    dhn0 = _mm(dproj, w_ein_t, name="even_in_dx")
    grad_x, _, d_mix_g0 = _rows(_rms_bwd_fn, [h0, dhn0, dh1], [norm_mix_g[0:1]], [(D_MODEL, F32), (D_MODEL, BF16)],
                                [(1, D_MODEL)], name="mix0_dnorm")
    g_repl["norm_mix_g"] = jnp.concatenate([d_mix_g0, d_mix_g1])
    scatters[0] = _scatter_start({("even_w_in", 0): d_w_ein_t}, 0, extra=[_replicated_pack(g_repl, REPLICATED_LATE)])

    delta, new_m, new_v = {}, {}, {}
    updated, partial = {}, {}
    after = scatters[0][4]
    for stage in (4, 3, 2, 1, 0):
        partial[stage] = _scatter_finish(scatters[stage], after, stage)
        for (n, l), slabs in zip(STAGES[stage], partial[stage]):
            updated[n] = _adamw(_unit_block(w, (n, l)), slabs, _unit_block(mom, (n, l)), _unit_block(var, (n, l)),
                                name=f"adamw_{n}_{l}", layer=l, n_layers=w[n].shape[0], into=updated.get(n))
        after = updated[STAGES[stage][-1][0]][0]
    grads = {}
    for n in BIG_NAMES:
        outs = [o.transpose(0, 2, 1) for o in updated[n]] if UNIT_TRANSPOSED[n, 0] else updated[n]
        grads[n], delta[n], new_m[n], new_v[n] = outs
    for names, slabs, tag in ((REPLICATED_EARLY, partial[1][-1], "repl"), (REPLICATED_LATE, partial[0][-1], "late"),
                              (SMALL_SHARDED_NAMES, partial[1][-2], "small")):
        packs = [_pack([t[n].reshape(-1) for n in names], LANE, SUBLANE) for t in (w, mom, var)]
        outs = _adamw(packs[0], slabs, packs[1], packs[2], name=f"adamw_{tag}")
        shapes = [w[n].shape for n in names]
        for dst, o_ in zip((grads, delta, new_m, new_v), outs):
            dst.update(zip(names, _unpack(o_.reshape(-1), shapes)))

    return (loss, grad_x[None], *[grads[n] for n in W_NAMES], *[delta[n] for n in W_NAMES],
            *[new_m[n] for n in W_NAMES], *[new_v[n] for n in W_NAMES])
```
